```python
import math
import jax, jax.numpy as jnp
from jax import lax
import numpy as np

D_MODEL = 1024
BATCH = 8
SEQ = 16384
DEPTH = 1

SSM_EXPAND = 2
SSM_D_INNER = SSM_EXPAND * D_MODEL
SSM_HEADDIM = 64
SSM_HEADS = SSM_D_INNER // SSM_HEADDIM
SSM_GROUPS = 4
SSM_HEADS_PER_GROUP = SSM_HEADS // SSM_GROUPS
SSM_STATE = 128
SSM_CONV = 4
SSM_CHUNK = 128
SSM_CONV_DIM = SSM_D_INNER + 2 * SSM_GROUPS * SSM_STATE
SSM_DT_MIN = 0.001
SSM_DT_MAX = 0.1

ATTN_HEADS = 16
ATTN_KV_HEADS = 2
ATTN_HEADDIM = 64
ATTN_GROUP = ATTN_HEADS // ATTN_KV_HEADS
WINDOW = 128
REL_BUCKETS = 32
REL_MAX_DIST = 128

D_FF = 2816
FFN_CONV = 3

DEEPNORM_ALPHA = (2.0 * DEPTH) ** 0.25
DEEPNORM_BETA = (8.0 * DEPTH) ** -0.25
LN_EPS = 1e-5
RMS_EPS = 1e-5

Z_COLS = SSM_D_INNER
XBC_COLS = SSM_CONV_DIM
DT_COLS = SSM_HEADS
Q_COLS = ATTN_HEADS * ATTN_HEADDIM
KV_COLS = ATTN_KV_HEADS * ATTN_HEADDIM
GATE_COLS = 2 * D_MODEL
IN_COLS = Z_COLS + XBC_COLS + DT_COLS + Q_COLS + 2 * KV_COLS + GATE_COLS
SPLIT_POINTS = [Z_COLS,
                Z_COLS + XBC_COLS,
                Z_COLS + XBC_COLS + DT_COLS,
                Z_COLS + XBC_COLS + DT_COLS + Q_COLS,
                Z_COLS + XBC_COLS + DT_COLS + Q_COLS + KV_COLS,
                Z_COLS + XBC_COLS + DT_COLS + Q_COLS + 2 * KV_COLS]

kernel_name = 'hybrid_ssd_swa_sink_convffn_deepnorm'


def layer_norm(x, g, b):
    xf = x.astype(jnp.float32)
    mu = jnp.mean(xf, axis=-1, keepdims=True)
    xc = xf - mu
    var = jnp.mean(xc * xc, axis=-1, keepdims=True)
    y = xc * lax.rsqrt(var + LN_EPS) * g.astype(jnp.float32) + b.astype(jnp.float32)
    return y.astype(x.dtype)


def causal_depthwise_conv(u, w, b):
    k, c = w.shape
    out = lax.conv_general_dilated(u, w[:, None, :].astype(u.dtype), window_strides=(1,),
                                   padding=[(k - 1, 0)],
                                   dimension_numbers=('NWC', 'WIO', 'NWC'),
                                   feature_group_count=c)
    return out + b.astype(u.dtype)


def ssd_chunked(xs, dt, a, bm, cm):
    b, s = xs.shape[:2]
    nc, lc = s // SSM_CHUNK, SSM_CHUNK
    G, E, P, N = SSM_GROUPS, SSM_HEADS_PER_GROUP, SSM_HEADDIM, SSM_STATE
    x = xs.reshape(b, nc, lc, G, E, P)
    dt_c = dt.reshape(b, nc, lc, G, E)
    bm = bm.reshape(b, nc, lc, G, N)
    cm = cm.reshape(b, nc, lc, G, N)
    a_dt = jnp.moveaxis(dt_c * a.reshape(G, E), 2, -1)
    a_cs = jnp.cumsum(a_dt, axis=-1)
    xdt = x * dt_c[..., None]
    seg = a_cs[..., :, None] - a_cs[..., None, :]
    causal = jnp.tril(jnp.ones((lc, lc), dtype=bool))
    decay = jnp.exp(jnp.where(causal, seg, -jnp.inf))
    cb = jnp.einsum('bclgn,bcsgn->bcgls', cm, bm)
    y_diag = jnp.einsum('bcgels,bcsgep->bclgep', cb[:, :, :, None] * decay, xdt)
    decay_states = jnp.moveaxis(jnp.exp(a_cs[..., -1:] - a_cs), -1, 2)
    states = jnp.einsum('bclgn,bclgep->bcgepn', bm, xdt * decay_states[..., None])
    chunk_decay = jnp.exp(a_cs[..., -1])

    def step(h, inp):
        st, dec = inp
        return h * dec[..., None, None] + st, h

    h0 = jnp.zeros((b, G, E, P, N), jnp.float32)
    _, prev = lax.scan(step, h0, (jnp.moveaxis(states, 1, 0), jnp.moveaxis(chunk_decay, 1, 0)))
    prev = jnp.moveaxis(prev, 0, 1)
    state_decay_out = jnp.moveaxis(jnp.exp(a_cs), -1, 2)
    y_off = jnp.einsum('bclgn,bcgepn->bclgep', cm, prev) * state_decay_out[..., None]
    return (y_diag + y_off).reshape(b, s, SSM_HEADS, P)


def mamba2_branch(z, xbc, dt_raw, conv_w, conv_b, dt_bias, a_log, d_skip, norm_w):
    b, s, _ = z.shape
    xbc = jax.nn.silu(causal_depthwise_conv(xbc, conv_w, conv_b))
    xs, bm, cm = jnp.split(xbc, [SSM_D_INNER, SSM_D_INNER + SSM_GROUPS * SSM_STATE], axis=-1)
    xs = xs.reshape(b, s, SSM_HEADS, SSM_HEADDIM).astype(jnp.float32)
    bm = bm.reshape(b, s, SSM_GROUPS, SSM_STATE).astype(jnp.float32)
    cm = cm.reshape(b, s, SSM_GROUPS, SSM_STATE).astype(jnp.float32)
    dt = jax.nn.softplus(dt_raw.astype(jnp.float32) + dt_bias.astype(jnp.float32))
    a = -jnp.exp(a_log.astype(jnp.float32))
    y = ssd_chunked(xs, dt, a, bm, cm) + xs * d_skip.astype(jnp.float32)[:, None]
    y = y.reshape(b, s, SSM_D_INNER) * jax.nn.silu(z.astype(jnp.float32))
    yg = y.reshape(b, s, SSM_GROUPS, SSM_D_INNER // SSM_GROUPS)
    yg = yg * lax.rsqrt(jnp.mean(yg * yg, axis=-1, keepdims=True) + RMS_EPS)
    return (yg.reshape(b, s, SSM_D_INNER) * norm_w.astype(jnp.float32)).astype(z.dtype)


def rel_bucket(rel):
    n = jnp.maximum(rel, 0)
    max_exact = REL_BUCKETS // 2
    nf = jnp.maximum(n, 1).astype(jnp.float32)
    large = max_exact + (jnp.log(nf / max_exact) / math.log(REL_MAX_DIST / max_exact)
                         * (REL_BUCKETS - max_exact)).astype(jnp.int32)
    large = jnp.minimum(large, REL_BUCKETS - 1)
    return jnp.where(n < max_exact, n, large)


def swa_sink_attention(q, k, v, sinks, rel_bias):
    b, s, _ = q.shape
    W = WINDOW
    nb = s // W
    KV, G, Dh = ATTN_KV_HEADS, ATTN_GROUP, ATTN_HEADDIM
    qb = q.reshape(b, nb, W, KV, G, Dh).astype(jnp.float32)
    kb = k.reshape(b, nb, W, KV, Dh).astype(jnp.float32)
    vb = v.reshape(b, nb, W, KV, Dh).astype(jnp.float32)
    pad = jnp.zeros_like(kb[:, :1])
    k_band = jnp.concatenate([jnp.concatenate([pad, kb[:, :-1]], axis=1), kb], axis=2)
    v_band = jnp.concatenate([jnp.concatenate([pad, vb[:, :-1]], axis=1), vb], axis=2)
    logits = jnp.einsum('bnqkgd,bnskd->bnkgqs', qb, k_band) * (Dh ** -0.5)
    qi = jnp.arange(W)[:, None] + W
    kj = jnp.arange(2 * W)[None, :]
    rel = qi - kj
    bias = rel_bias.astype(jnp.float32)[rel_bucket(rel)]
    bias = jnp.transpose(bias, (2, 0, 1)).reshape(KV, G, W, 2 * W)
    in_window = (rel >= 0) & (rel < W)
    block_idx = jnp.arange(nb)[:, None, None]
    valid = in_window[None] & ((block_idx > 0) | (kj >= W)[None])
    logits = jnp.where(valid[None, :, None, None], logits + bias, -jnp.inf)
    sink = sinks.astype(jnp.float32).reshape(1, 1, KV, G, 1, 1)
    m = jnp.maximum(jnp.max(logits, axis=-1, keepdims=True), sink)
    p = jnp.exp(logits - m)
    probs = p / (jnp.sum(p, axis=-1, keepdims=True) + jnp.exp(sink - m))
    out = jnp.einsum('bnkgqs,bnskd->bnqkgd', probs, v_band)
    return out.reshape(b, s, ATTN_HEADS * Dh).astype(q.dtype)


def conv_ffn(h, w_up, conv_w, conv_b, w_down):
    u = causal_depthwise_conv(jnp.einsum('bsd,df->bsf', h, w_up), conv_w, conv_b)
    gate, val = jnp.split(u, 2, axis=-1)
    return jnp.einsum('bsf,fd->bsd', jax.nn.silu(gate) * val, w_down)


def _fwd_setup_inputs(seed: int = 0) -> dict:
    key = jax.random.key(seed)
    ks = jax.random.split(key, 24)
    L = DEPTH
    f32 = jnp.float32
    nrm = lambda k, shape: jax.random.normal(k, shape, f32)
    x = nrm(ks[0], (BATCH, SEQ, D_MODEL))
    rel_bias = 0.1 * nrm(ks[1], (REL_BUCKETS, ATTN_HEADS))
    w_in = nrm(ks[2], (L, D_MODEL, IN_COLS)) * D_MODEL ** -0.5
    b_gate = 0.01 * nrm(ks[3], (L, GATE_COLS))
    ssm_conv_w = 0.5 * nrm(ks[4], (L, SSM_CONV, SSM_CONV_DIM))
    ssm_conv_b = 0.01 * nrm(ks[5], (L, SSM_CONV_DIM))
    u = jax.random.uniform(ks[6], (L, SSM_HEADS), f32)
    dt0 = jnp.exp(u * (math.log(SSM_DT_MAX) - math.log(SSM_DT_MIN)) + math.log(SSM_DT_MIN))
    ssm_dt_bias = dt0 + jnp.log(-jnp.expm1(-dt0))
    ssm_a_log = jnp.log(jax.random.uniform(ks[7], (L, SSM_HEADS), f32, 1.0, 16.0))
    ssm_d = 1.0 + 0.01 * nrm(ks[8], (L, SSM_HEADS))
    ssm_norm_w = 1.0 + 0.01 * nrm(ks[9], (L, SSM_D_INNER))
    attn_sinks = 0.1 * nrm(ks[10], (L, ATTN_HEADS))
    w_branch_ssm = nrm(ks[11], (L, SSM_D_INNER, D_MODEL)) * SSM_D_INNER ** -0.5 * DEEPNORM_BETA
    w_branch_attn = nrm(ks[12], (L, Q_COLS, D_MODEL)) * Q_COLS ** -0.5 * DEEPNORM_BETA
    w_mix_out = nrm(ks[13], (L, D_MODEL, D_MODEL)) * D_MODEL ** -0.5 * DEEPNORM_BETA
    ln1_g = 1.0 + 0.01 * nrm(ks[14], (L, D_MODEL))
    ln1_b = 0.01 * nrm(ks[15], (L, D_MODEL))
    w_up = nrm(ks[16], (L, D_MODEL, 2 * D_FF)) * D_MODEL ** -0.5 * DEEPNORM_BETA
    ffn_conv_w = nrm(ks[17], (L, FFN_CONV, 2 * D_FF)) * FFN_CONV ** -0.5
    ffn_conv_b = 0.01 * nrm(ks[18], (L, 2 * D_FF))
    w_down = nrm(ks[19], (L, D_FF, D_MODEL)) * D_FF ** -0.5 * DEEPNORM_BETA
    ln2_g = 1.0 + 0.01 * nrm(ks[20], (L, D_MODEL))
    ln2_b = 0.01 * nrm(ks[21], (L, D_MODEL))
    return {'x': x, 'rel_bias': rel_bias, 'w_in': w_in, 'b_gate': b_gate,
            'ssm_conv_w': ssm_conv_w, 'ssm_conv_b': ssm_conv_b, 'ssm_dt_bias': ssm_dt_bias,
            'ssm_a_log': ssm_a_log, 'ssm_d': ssm_d, 'ssm_norm_w': ssm_norm_w,
            'attn_sinks': attn_sinks, 'w_branch_ssm': w_branch_ssm, 'w_branch_attn': w_branch_attn,
            'w_mix_out': w_mix_out, 'ln1_g': ln1_g, 'ln1_b': ln1_b, 'w_up': w_up,
            'ffn_conv_w': ffn_conv_w, 'ffn_conv_b': ffn_conv_b, 'w_down': w_down,
            'ln2_g': ln2_g, 'ln2_b': ln2_b}


def _fwd_reference(x, rel_bias, w_in, b_gate, ssm_conv_w, ssm_conv_b, ssm_dt_bias, ssm_a_log, ssm_d,
              ssm_norm_w, attn_sinks, w_branch_ssm, w_branch_attn, w_mix_out, ln1_g, ln1_b,
              w_up, ffn_conv_w, ffn_conv_b, w_down, ln2_g, ln2_b):
    h = x
    for l in range(DEPTH):
        proj = jnp.einsum('bsd,dc->bsc', h, w_in[l])
        z, xbc, dt_raw, q, k, v, gates = jnp.split(proj, SPLIT_POINTS, axis=-1)
        y_ssm = mamba2_branch(z, xbc, dt_raw, ssm_conv_w[l], ssm_conv_b[l], ssm_dt_bias[l],
                              ssm_a_log[l], ssm_d[l], ssm_norm_w[l])
        y_attn = swa_sink_attention(q, k, v, attn_sinks[l], rel_bias)
        g_ssm, g_attn = jnp.split(jax.nn.sigmoid(gates + b_gate[l]), 2, axis=-1)
        merged = (g_ssm * jnp.einsum('bsi,id->bsd', y_ssm, w_branch_ssm[l])
                  + g_attn * jnp.einsum('bsi,id->bsd', y_attn, w_branch_attn[l]))
        mix_out = jnp.einsum('bsd,de->bse', merged, w_mix_out[l])
        h = layer_norm(DEEPNORM_ALPHA * h + mix_out, ln1_g[l], ln1_b[l])
        ffn_out = conv_ffn(h, w_up[l], ffn_conv_w[l], ffn_conv_b[l], w_down[l])
        h = layer_norm(DEEPNORM_ALPHA * h + ffn_out, ln2_g[l], ln2_b[l])
    return h


import jax as _jax
import jax.numpy as _jnp

TWIN_FORMAT = 'train_step'
FWD_PARAMS = ['x', 'rel_bias', 'w_in', 'b_gate', 'ssm_conv_w', 'ssm_conv_b', 'ssm_dt_bias', 'ssm_a_log', 'ssm_d', 'ssm_norm_w', 'attn_sinks', 'w_branch_ssm', 'w_branch_attn', 'w_mix_out', 'ln1_g', 'ln1_b', 'w_up', 'ffn_conv_w', 'ffn_conv_b', 'w_down', 'ln2_g', 'ln2_b']
TWIN_WEIGHTS = ['rel_bias', 'w_in', 'b_gate', 'ssm_conv_w', 'ssm_conv_b', 'ssm_dt_bias', 'ssm_a_log', 'ssm_d', 'ssm_norm_w', 'attn_sinks', 'w_branch_ssm', 'w_branch_attn', 'w_mix_out', 'ln1_g', 'ln1_b', 'w_up', 'ffn_conv_w', 'ffn_conv_b', 'w_down', 'ln2_g', 'ln2_b']
TWIN_DIFF_INPUT = 'x'
TWIN_INPUTS = ['x', 'rel_bias', 'w_in', 'b_gate', 'ssm_conv_w', 'ssm_conv_b', 'ssm_dt_bias', 'ssm_a_log', 'ssm_d', 'ssm_norm_w', 'attn_sinks', 'w_branch_ssm', 'w_branch_attn', 'w_mix_out', 'ln1_g', 'ln1_b', 'w_up', 'ffn_conv_w', 'ffn_conv_b', 'w_down', 'ln2_g', 'ln2_b', 'loss_target', 'm_rel_bias', 'm_w_in', 'm_b_gate', 'm_ssm_conv_w', 'm_ssm_conv_b', 'm_ssm_dt_bias', 'm_ssm_a_log', 'm_ssm_d', 'm_ssm_norm_w', 'm_attn_sinks', 'm_w_branch_ssm', 'm_w_branch_attn', 'm_w_mix_out', 'm_ln1_g', 'm_ln1_b', 'm_w_up', 'm_ffn_conv_w', 'm_ffn_conv_b', 'm_w_down', 'm_ln2_g', 'm_ln2_b', 'v_rel_bias', 'v_w_in', 'v_b_gate', 'v_ssm_conv_w', 'v_ssm_conv_b', 'v_ssm_dt_bias', 'v_ssm_a_log', 'v_ssm_d', 'v_ssm_norm_w', 'v_attn_sinks', 'v_w_branch_ssm', 'v_w_branch_attn', 'v_w_mix_out', 'v_ln1_g', 'v_ln1_b', 'v_w_up', 'v_ffn_conv_w', 'v_ffn_conv_b', 'v_w_down', 'v_ln2_g', 'v_ln2_b']
TWIN_OUTPUTS = ['loss', 'grad_x', 'grad_rel_bias', 'grad_w_in', 'grad_b_gate', 'grad_ssm_conv_w', 'grad_ssm_conv_b', 'grad_ssm_dt_bias', 'grad_ssm_a_log', 'grad_ssm_d', 'grad_ssm_norm_w', 'grad_attn_sinks', 'grad_w_branch_ssm', 'grad_w_branch_attn', 'grad_w_mix_out', 'grad_ln1_g', 'grad_ln1_b', 'grad_w_up', 'grad_ffn_conv_w', 'grad_ffn_conv_b', 'grad_w_down', 'grad_ln2_g', 'grad_ln2_b', 'delta_rel_bias', 'delta_w_in', 'delta_b_gate', 'delta_ssm_conv_w', 'delta_ssm_conv_b', 'delta_ssm_dt_bias', 'delta_ssm_a_log', 'delta_ssm_d', 'delta_ssm_norm_w', 'delta_attn_sinks', 'delta_w_branch_ssm', 'delta_w_branch_attn', 'delta_w_mix_out', 'delta_ln1_g', 'delta_ln1_b', 'delta_w_up', 'delta_ffn_conv_w', 'delta_ffn_conv_b', 'delta_w_down', 'delta_ln2_g', 'delta_ln2_b', 'new_m_rel_bias', 'new_m_w_in', 'new_m_b_gate', 'new_m_ssm_conv_w', 'new_m_ssm_conv_b', 'new_m_ssm_dt_bias', 'new_m_ssm_a_log', 'new_m_ssm_d', 'new_m_ssm_norm_w', 'new_m_attn_sinks', 'new_m_w_branch_ssm', 'new_m_w_branch_attn', 'new_m_w_mix_out', 'new_m_ln1_g', 'new_m_ln1_b', 'new_m_w_up', 'new_m_ffn_conv_w', 'new_m_ffn_conv_b', 'new_m_w_down', 'new_m_ln2_g', 'new_m_ln2_b', 'new_v_rel_bias', 'new_v_w_in', 'new_v_b_gate', 'new_v_ssm_conv_w', 'new_v_ssm_conv_b', 'new_v_ssm_dt_bias', 'new_v_ssm_a_log', 'new_v_ssm_d', 'new_v_ssm_norm_w', 'new_v_attn_sinks', 'new_v_w_branch_ssm', 'new_v_w_branch_attn', 'new_v_w_mix_out', 'new_v_ln1_g', 'new_v_ln1_b', 'new_v_w_up', 'new_v_ffn_conv_w', 'new_v_ffn_conv_b', 'new_v_w_down', 'new_v_ln2_g', 'new_v_ln2_b']
TWIN_LEAF_KINDS = {'loss': 'loss', 'grad_x': 'grad_x', 'grad_rel_bias': 'grad_w', 'grad_w_in': 'grad_w', 'grad_b_gate': 'grad_w', 'grad_ssm_conv_w': 'grad_w', 'grad_ssm_conv_b': 'grad_w', 'grad_ssm_dt_bias': 'grad_w', 'grad_ssm_a_log': 'grad_w', 'grad_ssm_d': 'grad_w', 'grad_ssm_norm_w': 'grad_w', 'grad_attn_sinks': 'grad_w', 'grad_w_branch_ssm': 'grad_w', 'grad_w_branch_attn': 'grad_w', 'grad_w_mix_out': 'grad_w', 'grad_ln1_g': 'grad_w', 'grad_ln1_b': 'grad_w', 'grad_w_up': 'grad_w', 'grad_ffn_conv_w': 'grad_w', 'grad_ffn_conv_b': 'grad_w', 'grad_w_down': 'grad_w', 'grad_ln2_g': 'grad_w', 'grad_ln2_b': 'grad_w', 'delta_rel_bias': 'delta_w', 'delta_w_in': 'delta_w', 'delta_b_gate': 'delta_w', 'delta_ssm_conv_w': 'delta_w', 'delta_ssm_conv_b': 'delta_w', 'delta_ssm_dt_bias': 'delta_w', 'delta_ssm_a_log': 'delta_w', 'delta_ssm_d': 'delta_w', 'delta_ssm_norm_w': 'delta_w', 'delta_attn_sinks': 'delta_w', 'delta_w_branch_ssm': 'delta_w', 'delta_w_branch_attn': 'delta_w', 'delta_w_mix_out': 'delta_w', 'delta_ln1_g': 'delta_w', 'delta_ln1_b': 'delta_w', 'delta_w_up': 'delta_w', 'delta_ffn_conv_w': 'delta_w', 'delta_ffn_conv_b': 'delta_w', 'delta_w_down': 'delta_w', 'delta_ln2_g': 'delta_w', 'delta_ln2_b': 'delta_w', 'new_m_rel_bias': 'new_m', 'new_m_w_in': 'new_m', 'new_m_b_gate': 'new_m', 'new_m_ssm_conv_w': 'new_m', 'new_m_ssm_conv_b': 'new_m', 'new_m_ssm_dt_bias': 'new_m', 'new_m_ssm_a_log': 'new_m', 'new_m_ssm_d': 'new_m', 'new_m_ssm_norm_w': 'new_m', 'new_m_attn_sinks': 'new_m', 'new_m_w_branch_ssm': 'new_m', 'new_m_w_branch_attn': 'new_m', 'new_m_w_mix_out': 'new_m', 'new_m_ln1_g': 'new_m', 'new_m_ln1_b': 'new_m', 'new_m_w_up': 'new_m', 'new_m_ffn_conv_w': 'new_m', 'new_m_ffn_conv_b': 'new_m', 'new_m_w_down': 'new_m', 'new_m_ln2_g': 'new_m', 'new_m_ln2_b': 'new_m', 'new_v_rel_bias': 'new_v', 'new_v_w_in': 'new_v', 'new_v_b_gate': 'new_v', 'new_v_ssm_conv_w': 'new_v', 'new_v_ssm_conv_b': 'new_v', 'new_v_ssm_dt_bias': 'new_v', 'new_v_ssm_a_log': 'new_v', 'new_v_ssm_d': 'new_v', 'new_v_ssm_norm_w': 'new_v', 'new_v_attn_sinks': 'new_v', 'new_v_w_branch_ssm': 'new_v', 'new_v_w_branch_attn': 'new_v', 'new_v_w_mix_out': 'new_v', 'new_v_ln1_g': 'new_v', 'new_v_ln1_b': 'new_v', 'new_v_w_up': 'new_v', 'new_v_ffn_conv_w': 'new_v', 'new_v_ffn_conv_b': 'new_v', 'new_v_w_down': 'new_v', 'new_v_ln2_g': 'new_v', 'new_v_ln2_b': 'new_v'}


def _forward(args):
    return _fwd_reference(*[args[k] for k in FWD_PARAMS])


def _output_shape():
    def fwd():
        inp = _fwd_setup_inputs(0)
        return _fwd_reference(*[inp[k] for k in FWD_PARAMS])
    out = _jax.eval_shape(fwd)
    return out.shape, out.dtype

N_MICROBATCH = 1
ADAM_LR = 0.001
ADAM_B1 = 0.9
ADAM_B2 = 0.999
ADAM_EPS = 1e-08
ADAM_WD = 0.01
ADAM_STEP = 10
PER_EXAMPLE_BATCH_AXIS = {'x': 0, 'loss_target': 0}
SHARED_INPUTS = []
_WEIGHT_DTYPES = {'rel_bias': _jnp.float32, 'w_in': _jnp.float32, 'b_gate': _jnp.float32, 'ssm_conv_w': _jnp.float32, 'ssm_conv_b': _jnp.float32, 'ssm_dt_bias': _jnp.float32, 'ssm_a_log': _jnp.float32, 'ssm_d': _jnp.float32, 'ssm_norm_w': _jnp.float32, 'attn_sinks': _jnp.float32, 'w_branch_ssm': _jnp.float32, 'w_branch_attn': _jnp.float32, 'w_mix_out': _jnp.float32, 'ln1_g': _jnp.float32, 'ln1_b': _jnp.float32, 'w_up': _jnp.float32, 'ffn_conv_w': _jnp.float32, 'ffn_conv_b': _jnp.float32, 'w_down': _jnp.float32, 'ln2_g': _jnp.float32, 'ln2_b': _jnp.float32}
MOMENT_SCALE = {'rel_bias': 1.436124e-02, 'w_in': 3.125279e-02, 'b_gate': 1.644670e-02, 'ssm_conv_w': 3.608753e-02, 'ssm_conv_b': 5.909501e-02, 'ssm_dt_bias': 2.049686e-01, 'ssm_a_log': 3.740345e-01, 'ssm_d': 2.184232e-01, 'ssm_norm_w': 4.178984e-02, 'attn_sinks': 9.459496e-03, 'w_branch_ssm': 1.003017e-01, 'w_branch_attn': 1.975975e-02, 'w_mix_out': 9.959986e-02, 'ln1_g': 1.704238e+00, 'ln1_b': 8.612248e-01, 'w_up': 3.602124e-02, 'ffn_conv_w': 2.140276e-02, 'ffn_conv_b': 3.815690e-02, 'w_down': 5.861744e-02, 'ln2_g': 1.280185e+02, 'ln2_b': 3.225096e+00}


def _to_microbatches(a, axis):
    t = _jnp.moveaxis(a, axis, 0)
    t = t.reshape((N_MICROBATCH, t.shape[0] // N_MICROBATCH) + t.shape[1:])
    return _jnp.moveaxis(t, 1, axis + 1)


def setup_inputs(seed: int = 0) -> dict:
    inp = _fwd_setup_inputs(seed)
    key = _jax.random.fold_in(_jax.random.key(seed), 7919)
    shape, _ = _output_shape()
    out = dict(inp)
    out["loss_target"] = _jax.random.normal(_jax.random.fold_in(key, 0), shape, _jnp.float32)
    for i, name in enumerate(TWIN_WEIGHTS):
        w = inp[name].astype(_jnp.float32)
        if MOMENT_SCALE is None:
            s = _jnp.sqrt(_jnp.mean(_jnp.square(w)) + 1e-30)
        else:
            s = MOMENT_SCALE[name]
        km, kv = _jax.random.split(_jax.random.fold_in(key, i + 1))
        out[name] = w
        out["m_" + name] = s * _jax.random.normal(km, w.shape, _jnp.float32)
        out["v_" + name] = (s * s) * _jax.random.uniform(kv, w.shape, _jnp.float32, 0.5, 1.5)
    if N_MICROBATCH > 1:
        for name, axis in PER_EXAMPLE_BATCH_AXIS.items():
            out[name] = _to_microbatches(out[name], axis)
    return {'x': out['x'], 'rel_bias': out['rel_bias'], 'w_in': out['w_in'], 'b_gate': out['b_gate'], 'ssm_conv_w': out['ssm_conv_w'], 'ssm_conv_b': out['ssm_conv_b'], 'ssm_dt_bias': out['ssm_dt_bias'], 'ssm_a_log': out['ssm_a_log'], 'ssm_d': out['ssm_d'], 'ssm_norm_w': out['ssm_norm_w'], 'attn_sinks': out['attn_sinks'], 'w_branch_ssm': out['w_branch_ssm'], 'w_branch_attn': out['w_branch_attn'], 'w_mix_out': out['w_mix_out'], 'ln1_g': out['ln1_g'], 'ln1_b': out['ln1_b'], 'w_up': out['w_up'], 'ffn_conv_w': out['ffn_conv_w'], 'ffn_conv_b': out['ffn_conv_b'], 'w_down': out['w_down'], 'ln2_g': out['ln2_g'], 'ln2_b': out['ln2_b'], 'loss_target': out['loss_target'], 'm_rel_bias': out['m_rel_bias'], 'm_w_in': out['m_w_in'], 'm_b_gate': out['m_b_gate'], 'm_ssm_conv_w': out['m_ssm_conv_w'], 'm_ssm_conv_b': out['m_ssm_conv_b'], 'm_ssm_dt_bias': out['m_ssm_dt_bias'], 'm_ssm_a_log': out['m_ssm_a_log'], 'm_ssm_d': out['m_ssm_d'], 'm_ssm_norm_w': out['m_ssm_norm_w'], 'm_attn_sinks': out['m_attn_sinks'], 'm_w_branch_ssm': out['m_w_branch_ssm'], 'm_w_branch_attn': out['m_w_branch_attn'], 'm_w_mix_out': out['m_w_mix_out'], 'm_ln1_g': out['m_ln1_g'], 'm_ln1_b': out['m_ln1_b'], 'm_w_up': out['m_w_up'], 'm_ffn_conv_w': out['m_ffn_conv_w'], 'm_ffn_conv_b': out['m_ffn_conv_b'], 'm_w_down': out['m_w_down'], 'm_ln2_g': out['m_ln2_g'], 'm_ln2_b': out['m_ln2_b'], 'v_rel_bias': out['v_rel_bias'], 'v_w_in': out['v_w_in'], 'v_b_gate': out['v_b_gate'], 'v_ssm_conv_w': out['v_ssm_conv_w'], 'v_ssm_conv_b': out['v_ssm_conv_b'], 'v_ssm_dt_bias': out['v_ssm_dt_bias'], 'v_ssm_a_log': out['v_ssm_a_log'], 'v_ssm_d': out['v_ssm_d'], 'v_ssm_norm_w': out['v_ssm_norm_w'], 'v_attn_sinks': out['v_attn_sinks'], 'v_w_branch_ssm': out['v_w_branch_ssm'], 'v_w_branch_attn': out['v_w_branch_attn'], 'v_w_mix_out': out['v_w_mix_out'], 'v_ln1_g': out['v_ln1_g'], 'v_ln1_b': out['v_ln1_b'], 'v_w_up': out['v_w_up'], 'v_ffn_conv_w': out['v_ffn_conv_w'], 'v_ffn_conv_b': out['v_ffn_conv_b'], 'v_w_down': out['v_w_down'], 'v_ln2_g': out['v_ln2_g'], 'v_ln2_b': out['v_ln2_b']}


def _loss(weights, diff, rest, loss_target):
    with _jax.named_scope("forward"):
        args = {**rest, TWIN_DIFF_INPUT: diff, **{k: w.astype(_WEIGHT_DTYPES[k]) for k, w in weights.items()}}
        y = _forward(args)
    with _jax.named_scope("loss_head"):
        err = _jnp.square(y.astype(_jnp.float32) - loss_target)
        return 0.5 * _jnp.sum(_jnp.mean(err, axis=-1)) if err.ndim else 0.5 * err


def _adamw(w, g, m, v):
    m = ADAM_B1 * m + (1.0 - ADAM_B1) * g
    v = ADAM_B2 * v + (1.0 - ADAM_B2) * _jnp.square(g)
    m_hat = m / (1.0 - ADAM_B1 ** ADAM_STEP)
    v_hat = v / (1.0 - ADAM_B2 ** ADAM_STEP)
    delta = -ADAM_LR * (m_hat / (_jnp.sqrt(v_hat) + ADAM_EPS) + ADAM_WD * w)
    return delta, m, v


def reference(x, rel_bias, w_in, b_gate, ssm_conv_w, ssm_conv_b, ssm_dt_bias, ssm_a_log, ssm_d, ssm_norm_w, attn_sinks, w_branch_ssm, w_branch_attn, w_mix_out, ln1_g, ln1_b, w_up, ffn_conv_w, ffn_conv_b, w_down, ln2_g, ln2_b, loss_target, m_rel_bias, m_w_in, m_b_gate, m_ssm_conv_w, m_ssm_conv_b, m_ssm_dt_bias, m_ssm_a_log, m_ssm_d, m_ssm_norm_w, m_attn_sinks, m_w_branch_ssm, m_w_branch_attn, m_w_mix_out, m_ln1_g, m_ln1_b, m_w_up, m_ffn_conv_w, m_ffn_conv_b, m_w_down, m_ln2_g, m_ln2_b, v_rel_bias, v_w_in, v_b_gate, v_ssm_conv_w, v_ssm_conv_b, v_ssm_dt_bias, v_ssm_a_log, v_ssm_d, v_ssm_norm_w, v_attn_sinks, v_w_branch_ssm, v_w_branch_attn, v_w_mix_out, v_ln1_g, v_ln1_b, v_w_up, v_ffn_conv_w, v_ffn_conv_b, v_w_down, v_ln2_g, v_ln2_b):
    given = dict(x=x, rel_bias=rel_bias, w_in=w_in, b_gate=b_gate, ssm_conv_w=ssm_conv_w, ssm_conv_b=ssm_conv_b, ssm_dt_bias=ssm_dt_bias, ssm_a_log=ssm_a_log, ssm_d=ssm_d, ssm_norm_w=ssm_norm_w, attn_sinks=attn_sinks, w_branch_ssm=w_branch_ssm, w_branch_attn=w_branch_attn, w_mix_out=w_mix_out, ln1_g=ln1_g, ln1_b=ln1_b, w_up=w_up, ffn_conv_w=ffn_conv_w, ffn_conv_b=ffn_conv_b, w_down=w_down, ln2_g=ln2_g, ln2_b=ln2_b, loss_target=loss_target, m_rel_bias=m_rel_bias, m_w_in=m_w_in, m_b_gate=m_b_gate, m_ssm_conv_w=m_ssm_conv_w, m_ssm_conv_b=m_ssm_conv_b, m_ssm_dt_bias=m_ssm_dt_bias, m_ssm_a_log=m_ssm_a_log, m_ssm_d=m_ssm_d, m_ssm_norm_w=m_ssm_norm_w, m_attn_sinks=m_attn_sinks, m_w_branch_ssm=m_w_branch_ssm, m_w_branch_attn=m_w_branch_attn, m_w_mix_out=m_w_mix_out, m_ln1_g=m_ln1_g, m_ln1_b=m_ln1_b, m_w_up=m_w_up, m_ffn_conv_w=m_ffn_conv_w, m_ffn_conv_b=m_ffn_conv_b, m_w_down=m_w_down, m_ln2_g=m_ln2_g, m_ln2_b=m_ln2_b, v_rel_bias=v_rel_bias, v_w_in=v_w_in, v_b_gate=v_b_gate, v_ssm_conv_w=v_ssm_conv_w, v_ssm_conv_b=v_ssm_conv_b, v_ssm_dt_bias=v_ssm_dt_bias, v_ssm_a_log=v_ssm_a_log, v_ssm_d=v_ssm_d, v_ssm_norm_w=v_ssm_norm_w, v_attn_sinks=v_attn_sinks, v_w_branch_ssm=v_w_branch_ssm, v_w_branch_attn=v_w_branch_attn, v_w_mix_out=v_w_mix_out, v_ln1_g=v_ln1_g, v_ln1_b=v_ln1_b, v_w_up=v_w_up, v_ffn_conv_w=v_ffn_conv_w, v_ffn_conv_b=v_ffn_conv_b, v_w_down=v_w_down, v_ln2_g=v_ln2_g, v_ln2_b=v_ln2_b)
    weights = {n: given[n] for n in TWIN_WEIGHTS}
    shared = {n: given[n] for n in SHARED_INPUTS}
    per_example = {n: given[n] for n in ['x']}
    grad_fn = _jax.value_and_grad(_loss, argnums=(0, 1))

    def one_microbatch(ex, loss_target):
        ex = dict(ex)
        diff = ex.pop(TWIN_DIFF_INPUT)
        return grad_fn(weights, diff, {**shared, **ex}, loss_target)

    if N_MICROBATCH == 1:
        loss, (grad_w, grad_x) = one_microbatch(per_example, given["loss_target"])
    else:
        def body(carry, xs):
            loss_sum, grad_sum = carry
            l_k, (gw_k, gx_k) = one_microbatch(xs[0], xs[1])
            with _jax.named_scope("update"):
                return (loss_sum + l_k, _jax.tree.map(_jnp.add, grad_sum, gw_k)), gx_k

        init = (_jnp.zeros((), _jnp.float32), _jax.tree.map(_jnp.zeros_like, weights))
        (loss, grad_w), grad_x = _jax.lax.scan(body, init, (per_example, given["loss_target"]))
    with _jax.named_scope("update"):
        delta_w, new_m, new_v = {}, {}, {}
        for n in TWIN_WEIGHTS:
            delta_w[n], new_m[n], new_v[n] = _adamw(weights[n], grad_w[n], given["m_" + n], given["v_" + n])
    return (loss, grad_x, *[grad_w[n] for n in TWIN_WEIGHTS], *[delta_w[n] for n in TWIN_WEIGHTS],
            *[new_m[n] for n in TWIN_WEIGHTS], *[new_v[n] for n in TWIN_WEIGHTS])
```

```python
import functools
import math

import jax
import jax.numpy as jnp
from jax import lax
from jax.experimental import pallas as pl
from jax.experimental.pallas import tpu as pltpu

F32, BF16 = jnp.float32, jnp.bfloat16
HIGHEST = lax.Precision.HIGHEST
MESH_ID = pl.DeviceIdType.MESH

N_DEV = 8
D_MODEL = 1024
SSM_INNER = 2048
SSM_HEADS = 32
SSM_HEADDIM = 64
SSMD = SSM_HEADDIM
SSM_GROUPS = 4
SSM_GROUP_COLS = SSM_INNER // SSM_GROUPS
SSM_STATE = 128
SSM_CONV = 4
CHUNK = 128
XBC_COLS = SSM_INNER + 2 * SSM_GROUPS * SSM_STATE
B_OFF = SSM_INNER
C_OFF = SSM_INNER + SSM_GROUPS * SSM_STATE
ATTN_HEADS = 16
ATTN_KV = 2
ATTN_GROUP = 8
HEADDIM = 64
WINDOW = 128
REL_BUCKETS = 32
REL_MAX_DIST = 128
D_FF = 2816
FFN_CONV = 3
ALPHA = 2.0 ** 0.25
LN_EPS = 1e-5
RMS_EPS = 1e-5
IN_COLS = 8480
Z_OFF, XBC_OFF, Q_OFF, GATE_OFF, MAIN_COLS = 0, 2048, 5120, 6144, 8192
K_OFF, V_OFF, DT_OFF, TAIL_COLS = 0, 128, 256, 384
O_Z, O_XBC, O_DT, O_Q, O_K, O_V, O_GATE = 0, 2048, 5120, 5152, 6176, 6304, 6432

ADAM_LR, ADAM_B1, ADAM_B2, ADAM_EPS, ADAM_WD, ADAM_STEP = 0.001, 0.9, 0.999, 1e-08, 0.01, 10
NEG = -1e30
HALO = 8
VMEM_LIMIT = 56 * 1024 * 1024


def _cp(sem):
    return pltpu.CompilerParams(dimension_semantics=sem, vmem_limit_bytes=VMEM_LIMIT)


def _const_spec(shape):
    nd = len(shape)
    return pl.BlockSpec(shape, lambda *_: (0,) * nd)


def _sigmoid(x):
    return 1.0 / (1.0 + jnp.exp(-x))


def _softplus(x):
    return jnp.maximum(x, 0.0) + jnp.log1p(jnp.exp(-jnp.abs(x)))


def _dot(a, b, dims=(((1,), (0,)), ((), ())), precision=None):
    return lax.dot_general(a, b, dims, preferred_element_type=F32, precision=precision)


NN = (((1,), (0,)), ((), ()))
NT = (((1,), (1,)), ((), ()))
TN = (((0,), (0,)), ((), ()))


def _matmul(a, b, mode, out_dtype, name, tm=512, tn=1024, tk=1024, addend=None, addend_scale=1.0):
    if mode == "nn":
        (M, K), (K2, N) = a.shape, b.shape
    elif mode == "nt":
        (M, K), (N, K2) = a.shape, b.shape
    else:
        (K, M), (K2, N) = a.shape, b.shape
    assert K == K2, (a.shape, b.shape, mode)
    tm, tn, tk = min(tm, M), min(tn, N), min(tk, K)
    assert M % tm == 0 and N % tn == 0 and K % tk == 0, (M, N, K, tm, tn, tk)
    nk = K // tk
    dims = {"nn": NN, "nt": NT, "tn": TN}[mode]
    a_spec = pl.BlockSpec((tk, tm), lambda i, j, k: (k, i)) if mode == "tn" else pl.BlockSpec((tm, tk), lambda i, j, k: (i, k))
    b_spec = pl.BlockSpec((tn, tk), lambda i, j, k: (j, k)) if mode == "nt" else pl.BlockSpec((tk, tn), lambda i, j, k: (k, j))
    o_spec = pl.BlockSpec((tm, tn), lambda i, j, k: (i, j))

    def body(*refs):
        if addend is None:
            a_ref, b_ref, o_ref, acc = refs
        else:
            a_ref, b_ref, c_ref, o_ref, acc = refs
        k = pl.program_id(2)

        @pl.when(k == 0)
        def _():
            acc[...] = jnp.zeros_like(acc)

        acc[...] += _dot(a_ref[...].astype(BF16), b_ref[...].astype(BF16), dims)

        @pl.when(k == nk - 1)
        def _():
            r = acc[...]
            if addend is not None:
                r = r + addend_scale * c_ref[...].astype(F32)
            o_ref[...] = r.astype(out_dtype)

    in_specs = [a_spec, b_spec] + ([o_spec] if addend is not None else [])
    args = (a, b) + ((addend,) if addend is not None else ())
    return pl.pallas_call(
        body, name=name, grid=(M // tm, N // tn, nk), in_specs=in_specs, out_specs=o_spec,
        out_shape=jax.ShapeDtypeStruct((M, N), out_dtype), scratch_shapes=[pltpu.VMEM((tm, tn), F32)],
        compiler_params=_cp(("parallel", "parallel", "arbitrary")))(*args)


def _conv_fwd(pre, pre_col_off, C, w, b, K, act, name, tr=512, tc=1024):
    T = pre.shape[0]
    tc = min(tc, C)
    assert T % tr == 0 and C % tc == 0 and pre_col_off % tc == 0
    joff = pre_col_off // tc
    hb = tr // HALO

    def body(x_ref, xp_ref, w_ref, b_ref, o_ref, ext):
        i = pl.program_id(1)
        ext[0:HALO, :] = jnp.where(i > 0, xp_ref[...], 0.0)
        ext[HALO:HALO + tr, :] = x_ref[...]
        acc = b_ref[...] + w_ref[K - 1:K, :] * x_ref[...]
        for k in range(K - 1):
            s = K - 1 - k
            acc = acc + w_ref[k:k + 1, :] * ext[HALO - s:HALO - s + tr, :]
        if act == "silu":
            acc = acc * _sigmoid(acc)
        o_ref[...] = acc

    return pl.pallas_call(
        body, name=name, grid=(C // tc, T // tr),
        in_specs=[pl.BlockSpec((tr, tc), lambda j, i: (i, joff + j)),
                  pl.BlockSpec((HALO, tc), lambda j, i: (jnp.maximum(i * hb - 1, 0), joff + j)),
                  pl.BlockSpec((K, tc), lambda j, i: (0, j)),
                  pl.BlockSpec((1, tc), lambda j, i: (0, j))],
        out_specs=pl.BlockSpec((tr, tc), lambda j, i: (i, j)),
        out_shape=jax.ShapeDtypeStruct((T, C), F32),
        scratch_shapes=[pltpu.VMEM((tr + HALO, tc), F32)],
        compiler_params=_cp(("parallel", "arbitrary")))(pre, pre, w, b)


def _conv_bwd(dout, pre, pre_col_off, C, w, b, K, act, dst, dst_col_off, name, tr=512, tc=1024):
    T = pre.shape[0]
    tc = min(tc, C)
    assert T % tr == 0 and C % tc == 0 and pre_col_off % tc == 0 and dst_col_off % tc == 0
    joff, doff = pre_col_off // tc, dst_col_off // tc
    hb = tr // HALO
    nt = T // tr
    last_hblock = T // HALO - 1

    def body(g_ref, gn_ref, x_ref, xp_ref, xn_ref, w_ref, b_ref, *rest):
        o_ref, dw_ref, db_ref, ext, gext = rest[-5:]
        i = pl.program_id(1)
        not_last = i < nt - 1
        ext[0:HALO, :] = jnp.where(i > 0, xp_ref[...], 0.0)
        ext[HALO:HALO + tr, :] = x_ref[...]
        ext[HALO + tr:, :] = jnp.where(not_last, xn_ref[...], 0.0)
        g = g_ref[...]
        gn = jnp.where(not_last, gn_ref[...], 0.0)
        if act == "silu":
            co = b_ref[...] + w_ref[K - 1:K, :] * ext[HALO:, :]
            for k in range(K - 1):
                s = K - 1 - k
                co = co + w_ref[k:k + 1, :] * ext[HALO - s:2 * HALO - s + tr, :]
            sg = _sigmoid(co)
            dact = sg * (1.0 + co * (1.0 - sg))
            g = g * dact[0:tr, :]
            gn = gn * dact[tr:, :]
        gext[0:tr, :] = g
        gext[tr:, :] = gn
        dpre = w_ref[K - 1:K, :] * g
        for k in range(K - 1):
            s = K - 1 - k
            dpre = dpre + w_ref[k:k + 1, :] * gext[s:s + tr, :]
        o_ref[...] = dpre.astype(o_ref.dtype)

        @pl.when(i == 0)
        def _():
            dw_ref[...] = jnp.zeros_like(dw_ref)
            db_ref[...] = jnp.zeros_like(db_ref)

        db_ref[...] += jnp.sum(g, axis=0, keepdims=True)
        for k in range(K):
            s = K - 1 - k
            dw_ref[k:k + 1, :] += jnp.sum(g * ext[HALO - s:HALO - s + tr, :], axis=0, keepdims=True)

    tile = lambda off: pl.BlockSpec((tr, tc), lambda j, i: (i, off + j))
    nxt = lambda off: pl.BlockSpec((HALO, tc), lambda j, i: (jnp.minimum((i + 1) * hb, last_hblock), off + j))
    in_specs = [tile(0), nxt(0), tile(joff),
                pl.BlockSpec((HALO, tc), lambda j, i: (jnp.maximum(i * hb - 1, 0), joff + j)), nxt(joff),
                pl.BlockSpec((K, tc), lambda j, i: (0, j)), pl.BlockSpec((1, tc), lambda j, i: (0, j))]
    args = (dout, dout, pre, pre, pre, w, b)
    if isinstance(dst, jax.ShapeDtypeStruct):
        aliases = {}
    else:
        in_specs.append(pl.BlockSpec(memory_space=pl.ANY))
        args += (dst,)
        aliases = {7: 0}
    return pl.pallas_call(
        body, name=name, grid=(C // tc, nt), in_specs=in_specs,
        out_specs=[tile(doff), pl.BlockSpec((K, tc), lambda j, i: (0, j)), pl.BlockSpec((1, tc), lambda j, i: (0, j))],
        out_shape=[jax.ShapeDtypeStruct(dst.shape, dst.dtype), jax.ShapeDtypeStruct((K, C), F32),
                   jax.ShapeDtypeStruct((1, C), F32)],
        scratch_shapes=[pltpu.VMEM((tr + 2 * HALO, tc), F32), pltpu.VMEM((tr + HALO, tc), F32)],
        input_output_aliases=aliases,
        compiler_params=_cp(("parallel", "arbitrary")))(*args)


def _chunk_scalars(dt_raw, dtb, alog):
    row = lax.broadcasted_iota(jnp.int32, (CHUNK, CHUNK), 0)
    col = lax.broadcasted_iota(jnp.int32, (CHUNK, CHUNK), 1)
    dt = _softplus(dt_raw + dtb)
    a = -jnp.exp(alog)
    acs = _dot((row >= col).astype(F32), dt * a, NN, HIGHEST)
    return dt, a, acs, acs.T, row, col


def _group_rms(yg):
    out = []
    for g in range(SSM_GROUPS):
        v = yg[:, g * SSM_GROUP_COLS:(g + 1) * SSM_GROUP_COLS]
        out.append(lax.rsqrt(jnp.mean(v * v, axis=1, keepdims=True) + RMS_EPS))
    return out


def _ssd_fwd(xbc, proj_main, proj_tail, dtb, alog, d_exp, norm_w):
    T = xbc.shape[0]
    nc = T // CHUNK

    def body(xbc_ref, dt_ref, z_ref, dtb_ref, alog_ref, d_ref, nw_ref, y_ref, ypre_ref, hs_ref, H):
        c = pl.program_id(0)

        @pl.when(c == 0)
        def _():
            H[...] = jnp.zeros_like(H)

        hs_ref[0] = H[...]
        dt, a, acs, acsT, row, col = _chunk_scalars(dt_ref[:, 0:SSM_HEADS], dtb_ref[...], alog_ref[...])
        tril = row >= col
        eo = jnp.exp(acs)
        dst = jnp.exp(acs[CHUNK - 1:CHUNK, :] - acs)
        for g in range(SSM_GROUPS):
            gs = slice(g * SSM_GROUP_COLS, (g + 1) * SSM_GROUP_COLS)
            Bg = xbc_ref[:, B_OFF + g * SSM_STATE:B_OFF + (g + 1) * SSM_STATE].astype(BF16)
            Cg = xbc_ref[:, C_OFF + g * SSM_STATE:C_OFF + (g + 1) * SSM_STATE].astype(BF16)
            Hg = H[:, gs]
            CH = _dot(Cg, Hg.astype(BF16))
            CB = _dot(Cg, Bg, NT)
            xw = []
            for e in range(8):
                h = g * 8 + e
                hs = slice(h * SSMD, (h + 1) * SSMD)
                xdt = xbc_ref[:, hs] * dt[:, h:h + 1]
                L = jnp.exp(jnp.where(tril, acs[:, h:h + 1] - acsT[h:h + 1, :], -jnp.inf))
                yd = _dot((CB * L).astype(BF16), xdt.astype(BF16))
                ypre_ref[:, hs] = yd + CH[:, e * SSMD:(e + 1) * SSMD] * eo[:, h:h + 1]
                xw.append((xdt * dst[:, h:h + 1]).astype(BF16))
            S = _dot(Bg, jnp.concatenate(xw, axis=1), TN)
            cd = jnp.concatenate([jnp.broadcast_to(eo[CHUNK - 1:CHUNK, g * 8 + e:g * 8 + e + 1], (1, SSMD))
                                  for e in range(8)], axis=1)
            H[:, gs] = Hg * cd + S
        ypre = ypre_ref[...] + xbc_ref[:, 0:SSM_INNER] * d_ref[...]
        ypre_ref[...] = ypre
        z = z_ref[...]
        yg = ypre * (z * _sigmoid(z))
        r = _group_rms(yg)
        for g in range(SSM_GROUPS):
            gs = slice(g * SSM_GROUP_COLS, (g + 1) * SSM_GROUP_COLS)
            y_ref[:, gs] = (yg[:, gs] * r[g] * nw_ref[:, gs]).astype(BF16)

    vec = lambda n: _const_spec((1, n))
    return pl.pallas_call(
        body, name="ssd_fwd", grid=(nc,),
        in_specs=[pl.BlockSpec((CHUNK, XBC_COLS), lambda c: (c, 0)),
                  pl.BlockSpec((CHUNK, 128), lambda c: (c, DT_OFF // 128)),
                  pl.BlockSpec((CHUNK, SSM_INNER), lambda c: (c, Z_OFF // SSM_INNER)),
                  vec(SSM_HEADS), vec(SSM_HEADS), vec(SSM_INNER), vec(SSM_INNER)],
        out_specs=[pl.BlockSpec((CHUNK, SSM_INNER), lambda c: (c, 0)),
                   pl.BlockSpec((CHUNK, SSM_INNER), lambda c: (c, 0)),
                   pl.BlockSpec((1, SSM_STATE, SSM_INNER), lambda c: (c, 0, 0))],
        out_shape=[jax.ShapeDtypeStruct((T, SSM_INNER), BF16), jax.ShapeDtypeStruct((T, SSM_INNER), F32),
                   jax.ShapeDtypeStruct((nc, SSM_STATE, SSM_INNER), F32)],
        scratch_shapes=[pltpu.VMEM((SSM_STATE, SSM_INNER), F32)],
        compiler_params=_cp(("arbitrary",)))(xbc, proj_tail, proj_main, dtb, alog, d_exp, norm_w)


def _head_sum(x):
    return jnp.concatenate([jnp.sum(x[:, e * SSMD:(e + 1) * SSMD], axis=1, keepdims=True) for e in range(8)], axis=1)


def _head_expand(v, rows):
    return jnp.concatenate([jnp.broadcast_to(v[:, e:e + 1], (rows, SSMD)) for e in range(8)], axis=1)


def _ssd_bwd(dyo, ypre, xbc, hs, proj_main, proj_tail, dtb, alog, d_exp, norm_w, dmain, dtail):
    T = xbc.shape[0]
    nc = T // CHUNK

    def body(dyo_ref, ypre_ref, xbc_ref, hs_ref, dt_ref, z_ref, dtb_ref, alog_ref, d_ref, nw_ref, dmain_in, dtail_in,
             dz_ref, ddt_ref, dxbc_ref, dnw_ref, dd_ref, dalog_ref, ddtb_ref, G, dacs_s):
        del dmain_in, dtail_in
        c = pl.program_id(0)

        @pl.when(c == 0)
        def _():
            G[...] = jnp.zeros_like(G)
            dnw_ref[...] = jnp.zeros_like(dnw_ref)
            dd_ref[...] = jnp.zeros_like(dd_ref)
            dalog_ref[...] = jnp.zeros_like(dalog_ref)
            ddtb_ref[...] = jnp.zeros_like(ddtb_ref)

        dt_raw = dt_ref[:, 0:SSM_HEADS]
        dt, a, acs, acsT, row, col = _chunk_scalars(dt_raw, dtb_ref[...], alog_ref[...])
        tril, triu = row >= col, col >= row
        eo = jnp.exp(acs)
        dst = jnp.exp(acs[CHUNK - 1:CHUNK, :] - acs)
        z = z_ref[...]
        sz = _sigmoid(z)
        silu_z = z * sz
        ypre = ypre_ref[...]
        yg = ypre * silu_z
        r = _group_rms(yg)
        dyn = dyo_ref[...] * nw_ref[...]
        dyg = []
        dnw = []
        for g in range(SSM_GROUPS):
            gs = slice(g * SSM_GROUP_COLS, (g + 1) * SSM_GROUP_COLS)
            ygn = yg[:, gs] * r[g]
            dnw.append(jnp.sum(dyo_ref[:, gs] * ygn, axis=0, keepdims=True))
            cm = jnp.mean(dyn[:, gs] * ygn, axis=1, keepdims=True)
            dyg.append(r[g] * (dyn[:, gs] - ygn * cm))
        dnw_ref[...] += jnp.concatenate(dnw, axis=1)
        dyg = jnp.concatenate(dyg, axis=1)
        dz_ref[...] = (dyg * ypre * (sz * (1.0 + z * (1.0 - sz)))).astype(dz_ref.dtype)
        dY = dyg * silu_z
        xs_all = xbc_ref[:, 0:SSM_INNER]
        dd_cols = jnp.sum(dY * xs_all, axis=0, keepdims=True)
        ddt_cols = []
        dacs_cols = []
        dcd_cols = []
        for g in range(SSM_GROUPS):
            gs = slice(g * SSM_GROUP_COLS, (g + 1) * SSM_GROUP_COLS)
            hg = slice(g * 8, (g + 1) * 8)
            Bg = xbc_ref[:, B_OFF + g * SSM_STATE:B_OFF + (g + 1) * SSM_STATE].astype(BF16)
            Cg = xbc_ref[:, C_OFF + g * SSM_STATE:C_OFF + (g + 1) * SSM_STATE].astype(BF16)
            Hg = hs_ref[0, :, gs]
            Hgb = Hg.astype(BF16)
            Gg = G[:, gs]
            Ggb = Gg.astype(BF16)
            dYg = dY[:, gs]
            xs = xs_all[:, gs]
            eo_e = _head_expand(eo[:, hg], CHUNK)
            dst_e = _head_expand(dst[:, hg], CHUNK)
            dt_e = _head_expand(dt[:, hg], CHUNK)
            cd_e = _head_expand(eo[CHUNK - 1:CHUNK, hg], 1)
            xdt = xs * dt_e
            CH = _dot(Cg, Hgb)
            dYe = dYg * eo_e
            dYeb = dYe.astype(BF16)
            dC = _dot(dYeb, Hgb, NT)
            BG = _dot(Bg, Ggb)
            xw = xdt * dst_e
            dB = _dot(xw.astype(BF16), Ggb, NT)
            q = _head_sum(xdt * BG * dst_e)
            dacs = _head_sum(dYe * CH) - q
            dcd = _head_sum(jnp.sum(Gg * Hg, axis=0, keepdims=True)) * eo[CHUNK - 1:CHUNK, hg]
            dcd_cols.append(dcd + jnp.sum(q, axis=0, keepdims=True))
            dacs_cols.append(dacs)
            G[:, gs] = Gg * cd_e + _dot(Cg, dYeb, TN)
            CB = _dot(Cg, Bg, NT)
            CBT = _dot(Bg, Cg, NT)
            dCB = jnp.zeros((CHUNK, CHUNK), F32)
            dxdt_d = []
            for e in range(8):
                h = g * 8 + e
                es = slice(e * SSMD, (e + 1) * SSMD)
                seg = acs[:, h:h + 1] - acsT[h:h + 1, :]
                L = jnp.exp(jnp.where(tril, seg, -jnp.inf))
                LT = jnp.exp(jnp.where(triu, -seg, -jnp.inf))
                dYh = dYg[:, es].astype(BF16)
                xh = xdt[:, es].astype(BF16)
                dM = _dot(dYh, xh, NT)
                dMT = _dot(xh, dYh, NT)
                M = CB * L
                MT = CBT * LT
                dxdt_d.append(_dot(MT.astype(BF16), dYh))
                dCB = dCB + dM * L
                dacs_s[h:h + 1, :] = (jnp.sum(dMT * MT, axis=0, keepdims=True)
                                      - jnp.sum(dM * M, axis=0, keepdims=True))
            dCBb = dCB.astype(BF16)
            dC = dC + _dot(dCBb, Bg)
            dB = dB + _dot(dCBb, Cg, TN)
            dxdt = jnp.concatenate(dxdt_d, axis=1) + dst_e * BG
            ddt_cols.append(_head_sum(dxdt * xs))
            dxbc_ref[:, gs] = dxdt * dt_e + dYg * d_ref[:, gs]
            dxbc_ref[:, B_OFF + g * SSM_STATE:B_OFF + (g + 1) * SSM_STATE] = dB
            dxbc_ref[:, C_OFF + g * SSM_STATE:C_OFF + (g + 1) * SSM_STATE] = dC
        dacs = jnp.concatenate(dacs_cols, axis=1) + dacs_s[...].T
        last = jnp.concatenate(dcd_cols, axis=1)
        dacs = dacs + jnp.where(lax.broadcasted_iota(jnp.int32, (CHUNK, SSM_HEADS), 0) == CHUNK - 1, last, 0.0)
        dadt = _dot(triu.astype(F32), dacs, NN, HIGHEST)
        ddt = dadt * a + jnp.concatenate(ddt_cols, axis=1)
        dalog_ref[...] += jnp.sum(dadt * dt, axis=0, keepdims=True) * a
        ddt_raw = ddt * _sigmoid(dt_raw + dtb_ref[...])
        ddtb_ref[...] += jnp.sum(ddt_raw, axis=0, keepdims=True)
        ddt_ref[...] = jnp.concatenate([ddt_raw, jnp.zeros((CHUNK, 128 - SSM_HEADS), F32)], axis=1).astype(ddt_ref.dtype)
        ehead = (lax.broadcasted_iota(jnp.int32, (SSM_INNER, SSM_HEADS), 0) // SSMD
                 == lax.broadcasted_iota(jnp.int32, (SSM_INNER, SSM_HEADS), 1)).astype(F32)
        dd_ref[...] += _dot(jnp.broadcast_to(dd_cols, (8, SSM_INNER)), ehead, NN, HIGHEST)[0:1, :]

    rev = lambda c: nc - 1 - c
    vec = lambda n: _const_spec((1, n))
    any_spec = pl.BlockSpec(memory_space=pl.ANY)
    outs = pl.pallas_call(
        body, name="ssd_bwd", grid=(nc,),
        in_specs=[pl.BlockSpec((CHUNK, SSM_INNER), lambda c: (rev(c), 0)),
                  pl.BlockSpec((CHUNK, SSM_INNER), lambda c: (rev(c), 0)),
                  pl.BlockSpec((CHUNK, XBC_COLS), lambda c: (rev(c), 0)),
                  pl.BlockSpec((1, SSM_STATE, SSM_INNER), lambda c: (rev(c), 0, 0)),
                  pl.BlockSpec((CHUNK, 128), lambda c: (rev(c), DT_OFF // 128)),
                  pl.BlockSpec((CHUNK, SSM_INNER), lambda c: (rev(c), Z_OFF // SSM_INNER)),
                  vec(SSM_HEADS), vec(SSM_HEADS), vec(SSM_INNER), vec(SSM_INNER), any_spec, any_spec],
        out_specs=[pl.BlockSpec((CHUNK, SSM_INNER), lambda c: (rev(c), Z_OFF // SSM_INNER)),
                   pl.BlockSpec((CHUNK, 128), lambda c: (rev(c), DT_OFF // 128)),
                   pl.BlockSpec((CHUNK, XBC_COLS), lambda c: (rev(c), 0)),
                   vec(SSM_INNER), vec(SSM_HEADS), vec(SSM_HEADS), vec(SSM_HEADS)],
        out_shape=[jax.ShapeDtypeStruct(dmain.shape, dmain.dtype), jax.ShapeDtypeStruct(dtail.shape, dtail.dtype),
                   jax.ShapeDtypeStruct((T, XBC_COLS), F32), jax.ShapeDtypeStruct((1, SSM_INNER), F32),
                   jax.ShapeDtypeStruct((1, SSM_HEADS), F32), jax.ShapeDtypeStruct((1, SSM_HEADS), F32),
                   jax.ShapeDtypeStruct((1, SSM_HEADS), F32)],
        scratch_shapes=[pltpu.VMEM((SSM_STATE, SSM_INNER), F32), pltpu.VMEM((SSM_HEADS, CHUNK), F32)],
        input_output_aliases={10: 0, 11: 1},
        compiler_params=_cp(("arbitrary",)))(dyo, ypre, xbc, hs, proj_tail, proj_main, dtb, alog, d_exp, norm_w,
                                             dmain, dtail)
    return outs


def _rel_bucket(rel):
    n = jnp.maximum(rel, 0)
    max_exact = REL_BUCKETS // 2
    nf = jnp.maximum(n, 1).astype(F32)
    large = max_exact + (jnp.log(nf / max_exact) / math.log(REL_MAX_DIST / max_exact)
                         * (REL_BUCKETS - max_exact)).astype(jnp.int32)
    large = jnp.minimum(large, REL_BUCKETS - 1)
    return jnp.where(n < max_exact, n, large)


def _band_geometry():
    qi = jnp.arange(WINDOW)[:, None] + WINDOW
    kj = jnp.arange(2 * WINDOW)[None, :]
    rel = qi - kj
    return _rel_bucket(rel), (rel >= 0) & (rel < WINDOW)


def _attn_logits(qh, kband, bias_h, first):
    s = _dot(qh, kband, NT) * (HEADDIM ** -0.5) + bias_h
    colk = lax.broadcasted_iota(jnp.int32, (WINDOW, 2 * WINDOW), 1)
    return jnp.where(jnp.logical_and(first, colk < WINDOW), NEG, s)


def _attn_fwd(proj_main, proj_tail, bias_tbl, sinks):
    T = proj_main.shape[0]
    nb = T // WINDOW

    def body(q_ref, kv_ref, kvp_ref, bias_ref, sink_ref, o_ref, lse_ref):
        i = pl.program_id(0)
        first = i == 0
        for kvh in range(ATTN_KV):
            ks = slice(K_OFF + kvh * HEADDIM, K_OFF + (kvh + 1) * HEADDIM)
            vs = slice(V_OFF + kvh * HEADDIM, V_OFF + (kvh + 1) * HEADDIM)
            kband = jnp.concatenate([kvp_ref[:, ks], kv_ref[:, ks]], axis=0).astype(BF16)
            vband = jnp.concatenate([kvp_ref[:, vs], kv_ref[:, vs]], axis=0).astype(BF16)
            for g in range(ATTN_GROUP):
                h = kvh * ATTN_GROUP + g
                hs = slice(h * HEADDIM, (h + 1) * HEADDIM)
                s = _attn_logits(q_ref[:, hs].astype(BF16), kband, bias_ref[h], first)
                sink = sink_ref[:, h:h + 1]
                m = jnp.maximum(jnp.max(s, axis=1, keepdims=True), sink)
                p = jnp.exp(s - m)
                den = jnp.sum(p, axis=1, keepdims=True) + jnp.exp(sink - m)
                o_ref[:, hs] = _dot((p / den).astype(BF16), vband).astype(BF16)
                lse_ref[:, h:h + 1] = m + jnp.log(den)

    return pl.pallas_call(
        body, name="attn_fwd", grid=(nb,),
        in_specs=[pl.BlockSpec((WINDOW, D_MODEL), lambda i: (i, Q_OFF // D_MODEL)),
                  pl.BlockSpec((WINDOW, 256), lambda i: (i, 0)),
                  pl.BlockSpec((WINDOW, 256), lambda i: (jnp.maximum(i - 1, 0), 0)),
                  _const_spec((ATTN_HEADS, WINDOW, 2 * WINDOW)), _const_spec((1, ATTN_HEADS))],
        out_specs=[pl.BlockSpec((WINDOW, D_MODEL), lambda i: (i, 0)),
                   pl.BlockSpec((WINDOW, ATTN_HEADS), lambda i: (i, 0))],
        out_shape=[jax.ShapeDtypeStruct((T, D_MODEL), BF16), jax.ShapeDtypeStruct((T, ATTN_HEADS), F32)],
        compiler_params=_cp(("arbitrary",)))(proj_main, proj_tail, proj_tail, bias_tbl, sinks)


def _attn_bwd(dy, lse, proj_main, proj_tail, bias_tbl, sinks, dmain):
    T = proj_main.shape[0]
    nb = T // WINDOW

    def body(dy_ref, lse_ref, q_ref, kv_ref, kvp_ref, bias_ref, sink_ref, dmain_in,
             dq_ref, dkv_ref, dbias_ref, dsink_ref, carry):
        del dmain_in
        i = pl.program_id(0)
        first = i == 0

        @pl.when(first)
        def _():
            carry[...] = jnp.zeros_like(carry)
            dbias_ref[...] = jnp.zeros_like(dbias_ref)
            dsink_ref[...] = jnp.zeros_like(dsink_ref)

        @pl.when(i < nb)
        def _():
            scale = HEADDIM ** -0.5
            for kvh in range(ATTN_KV):
                ks = slice(K_OFF + kvh * HEADDIM, K_OFF + (kvh + 1) * HEADDIM)
                vs = slice(V_OFF + kvh * HEADDIM, V_OFF + (kvh + 1) * HEADDIM)
                kband = jnp.concatenate([kvp_ref[:, ks], kv_ref[:, ks]], axis=0).astype(BF16)
                vband = jnp.concatenate([kvp_ref[:, vs], kv_ref[:, vs]], axis=0).astype(BF16)
                dk = jnp.zeros((2 * WINDOW, HEADDIM), F32)
                dv = jnp.zeros((2 * WINDOW, HEADDIM), F32)
                for g in range(ATTN_GROUP):
                    h = kvh * ATTN_GROUP + g
                    hs = slice(h * HEADDIM, (h + 1) * HEADDIM)
                    qh = q_ref[:, hs].astype(BF16)
                    s = _attn_logits(qh, kband, bias_ref[h], first)
                    lse_h = lse_ref[:, h:h + 1]
                    p = jnp.exp(s - lse_h)
                    pb = p.astype(BF16)
                    do = dy_ref[:, hs]
                    dv = dv + _dot(pb, do, TN)
                    dp = _dot(do, vband, NT)
                    delta = jnp.sum(p * dp, axis=1, keepdims=True)
                    ds = p * (dp - delta)
                    psink = jnp.exp(sink_ref[:, h:h + 1] - lse_h)
                    dsink_ref[:, h:h + 1] += -jnp.sum(psink * delta, axis=0, keepdims=True)
                    dbias_ref[h] += ds
                    dsb = (ds * scale).astype(BF16)
                    dq_ref[:, hs] = _dot(dsb, kband).astype(dq_ref.dtype)
                    dk = dk + _dot(dsb, qh, TN)
                dkv_ref[:, ks] = (carry[:, ks] + dk[0:WINDOW, :]).astype(dkv_ref.dtype)
                dkv_ref[:, vs] = (carry[:, vs] + dv[0:WINDOW, :]).astype(dkv_ref.dtype)
                carry[:, ks] = dk[WINDOW:, :]
                carry[:, vs] = dv[WINDOW:, :]

        @pl.when(i == nb)
        def _():
            dkv_ref[...] = carry[...].astype(dkv_ref.dtype)

    cur = lambda i: jnp.minimum(i, nb - 1)
    return pl.pallas_call(
        body, name="attn_bwd", grid=(nb + 1,),
        in_specs=[pl.BlockSpec((WINDOW, D_MODEL), lambda i: (cur(i), 0)),
                  pl.BlockSpec((WINDOW, ATTN_HEADS), lambda i: (cur(i), 0)),
                  pl.BlockSpec((WINDOW, D_MODEL), lambda i: (cur(i), Q_OFF // D_MODEL)),
                  pl.BlockSpec((WINDOW, 256), lambda i: (cur(i), 0)),
                  pl.BlockSpec((WINDOW, 256), lambda i: (jnp.maximum(cur(i) - 1, 0), 0)),
                  _const_spec((ATTN_HEADS, WINDOW, 2 * WINDOW)), _const_spec((1, ATTN_HEADS)),
                  pl.BlockSpec(memory_space=pl.ANY)],
        out_specs=[pl.BlockSpec((WINDOW, D_MODEL), lambda i: (cur(i), Q_OFF // D_MODEL)),
                   pl.BlockSpec((WINDOW, 256), lambda i: (jnp.maximum(i - 1, 0), 0)),
                   _const_spec((ATTN_HEADS, WINDOW, 2 * WINDOW)), _const_spec((1, ATTN_HEADS))],
        out_shape=[jax.ShapeDtypeStruct(dmain.shape, dmain.dtype), jax.ShapeDtypeStruct((T, TAIL_COLS), BF16),
                   jax.ShapeDtypeStruct((ATTN_HEADS, WINDOW, 2 * WINDOW), F32),
                   jax.ShapeDtypeStruct((1, ATTN_HEADS), F32)],
        scratch_shapes=[pltpu.VMEM((WINDOW, 256), F32)],
        input_output_aliases={7: 0},
        compiler_params=_cp(("arbitrary",)))(dy, lse, proj_main, proj_tail, proj_tail, bias_tbl, sinks, dmain)


def _rel_bias_grad(dbias, onehot):
    def body(d_ref, oh_ref, o_ref):
        o_ref[...] = _dot(d_ref[...], oh_ref[...], NN, HIGHEST)

    return pl.pallas_call(body, name="rel_bias_grad",
                          out_shape=jax.ShapeDtypeStruct((ATTN_HEADS, REL_BUCKETS), F32))(dbias, onehot)


def _ln_fwd(r, g, b):
    mu = jnp.mean(r, axis=1, keepdims=True)
    xc = r - mu
    rstd = lax.rsqrt(jnp.mean(xc * xc, axis=1, keepdims=True) + LN_EPS)
    xhat = xc * rstd
    return xhat * g + b, xhat, rstd


def _ln_bwd(dy, xhat, rstd, g):
    dxh = dy * g
    return rstd * (dxh - jnp.mean(dxh, axis=1, keepdims=True) - xhat * jnp.mean(dxh * xhat, axis=1, keepdims=True))


def _merge_fwd(y_ssm, y_attn, proj_main, b_gate, w_bs, w_ba, tm=512):
    T = y_ssm.shape[0]

    def body(ys_ref, ya_ref, gs_ref, ga_ref, bg_ref, wbs_ref, wba_ref, m_ref, bs_ref, ba_ref):
        bs = _dot(ys_ref[...], wbs_ref[...])
        ba = _dot(ya_ref[...], wba_ref[...])
        g_s = _sigmoid(gs_ref[...] + bg_ref[:, 0:D_MODEL])
        g_a = _sigmoid(ga_ref[...] + bg_ref[:, D_MODEL:])
        m_ref[...] = (g_s * bs + g_a * ba).astype(BF16)
        bs_ref[...] = bs
        ba_ref[...] = ba

    row = lambda w, off=0: pl.BlockSpec((tm, w), lambda i: (i, off))
    return pl.pallas_call(
        body, name="merge_fwd", grid=(T // tm,),
        in_specs=[row(SSM_INNER), row(D_MODEL), row(D_MODEL, GATE_OFF // D_MODEL), row(D_MODEL, GATE_OFF // D_MODEL + 1),
                  _const_spec((1, 2 * D_MODEL)), _const_spec((SSM_INNER, D_MODEL)), _const_spec((D_MODEL, D_MODEL))],
        out_specs=[row(D_MODEL), row(D_MODEL), row(D_MODEL)],
        out_shape=[jax.ShapeDtypeStruct((T, D_MODEL), BF16), jax.ShapeDtypeStruct((T, D_MODEL), F32),
                   jax.ShapeDtypeStruct((T, D_MODEL), F32)],
        compiler_params=_cp(("parallel",)))(y_ssm, y_attn, proj_main, proj_main, b_gate, w_bs, w_ba)


def _mix_ln1(merged, w_mo, x, g1, b1, tm=512):
    T = x.shape[0]

    def body(m_ref, w_ref, x_ref, g_ref, b_ref, r_ref, h_ref):
        r = ALPHA * x_ref[...] + _dot(m_ref[...], w_ref[...])
        r_ref[...] = r
        h_ref[...] = _ln_fwd(r, g_ref[...], b_ref[...])[0]

    row = pl.BlockSpec((tm, D_MODEL), lambda i: (i, 0))
    return pl.pallas_call(
        body, name="mix_ln1", grid=(T // tm,),
        in_specs=[row, _const_spec((D_MODEL, D_MODEL)), row, _const_spec((1, D_MODEL)), _const_spec((1, D_MODEL))],
        out_specs=[row, row],
        out_shape=[jax.ShapeDtypeStruct((T, D_MODEL), F32), jax.ShapeDtypeStruct((T, D_MODEL), F32)],
        compiler_params=_cp(("parallel",)))(merged, w_mo, x, g1, b1)


def _glu(u, tr=512, tc=1408):
    T = u.shape[0]
    nj = D_FF // tc

    def body(g_ref, v_ref, o_ref):
        g = g_ref[...]
        o_ref[...] = (g * _sigmoid(g) * v_ref[...]).astype(BF16)

    return pl.pallas_call(
        body, name="glu", grid=(T // tr, nj),
        in_specs=[pl.BlockSpec((tr, tc), lambda i, j: (i, j)), pl.BlockSpec((tr, tc), lambda i, j: (i, nj + j))],
        out_specs=pl.BlockSpec((tr, tc), lambda i, j: (i, j)),
        out_shape=jax.ShapeDtypeStruct((T, D_FF), BF16),
        compiler_params=_cp(("parallel", "parallel")))(u, u)


def _down_ln2_loss(act, w_down, h1, target, g2, b2, tm=512):
    T = h1.shape[0]

    def body(a_ref, w_ref, h_ref, t_ref, g_ref, b_ref, dr_ref, dg_ref, db_ref, l_ref):
        @pl.when(pl.program_id(0) == 0)
        def _():
            dg_ref[...] = jnp.zeros_like(dg_ref)
            db_ref[...] = jnp.zeros_like(db_ref)
            l_ref[...] = jnp.zeros_like(l_ref)

        r = ALPHA * h_ref[...] + _dot(a_ref[...], w_ref[...])
        y, xhat, rstd = _ln_fwd(r, g_ref[...], b_ref[...])
        err = y - t_ref[...]
        l_ref[...] += jnp.sum(err * err, keepdims=True)
        dy = err * (1.0 / D_MODEL)
        dg_ref[...] += jnp.sum(dy * xhat, axis=0, keepdims=True)
        db_ref[...] += jnp.sum(dy, axis=0, keepdims=True)
        dr_ref[...] = _ln_bwd(dy, xhat, rstd, g_ref[...])

    row = pl.BlockSpec((tm, D_MODEL), lambda i: (i, 0))
    vec = _const_spec((1, D_MODEL))
    return pl.pallas_call(
        body, name="down_ln2_loss", grid=(T // tm,),
        in_specs=[pl.BlockSpec((tm, D_FF), lambda i: (i, 0)), _const_spec((D_FF, D_MODEL)), row, row, vec, vec],
        out_specs=[row, vec, vec, _const_spec((1, 1))],
        out_shape=[jax.ShapeDtypeStruct((T, D_MODEL), F32), jax.ShapeDtypeStruct((1, D_MODEL), F32),
                   jax.ShapeDtypeStruct((1, D_MODEL), F32), jax.ShapeDtypeStruct((1, 1), F32)],
        compiler_params=_cp(("arbitrary",)))(act, w_down, h1, target, g2, b2)


def _ffn_bwd_act(dr2, w_down, u, tm=512, tn=1408):
    T = dr2.shape[0]
    nj = D_FF // tn

    def body(d_ref, w_ref, g_ref, v_ref, o_ref, dact):
        half = pl.program_id(2)

        @pl.when(half == 0)
        def _():
            dact[...] = _dot(d_ref[...].astype(BF16), w_ref[...], NT)
            g = g_ref[...]
            sg = _sigmoid(g)
            o_ref[...] = dact[...] * v_ref[...] * (sg * (1.0 + g * (1.0 - sg)))

        @pl.when(half == 1)
        def _():
            g = g_ref[...]
            o_ref[...] = dact[...] * (g * _sigmoid(g))

    return pl.pallas_call(
        body, name="ffn_bwd_act", grid=(T // tm, nj, 2),
        in_specs=[pl.BlockSpec((tm, D_MODEL), lambda i, j, h: (i, 0)),
                  pl.BlockSpec((tn, D_MODEL), lambda i, j, h: (j, 0)),
                  pl.BlockSpec((tm, tn), lambda i, j, h: (i, j)),
                  pl.BlockSpec((tm, tn), lambda i, j, h: (i, nj + j))],
        out_specs=pl.BlockSpec((tm, tn), lambda i, j, h: (i, h * nj + j)),
        out_shape=jax.ShapeDtypeStruct((T, 2 * D_FF), F32),
        scratch_shapes=[pltpu.VMEM((tm, tn), F32)],
        compiler_params=_cp(("parallel", "arbitrary", "arbitrary")))(dr2, w_down, u, u)


def _ffn_bwd_in(du_pre, w_up, dr2, r1, g1, b1, tm=512, tk=1408):
    T = dr2.shape[0]
    nk = 2 * D_FF // tk

    def body(d_ref, w_ref, dr2_ref, r_ref, g_ref, b_ref, dr1_ref, dg_ref, db_ref, acc):
        i, k = pl.program_id(0), pl.program_id(1)

        @pl.when(jnp.logical_and(i == 0, k == 0))
        def _():
            dg_ref[...] = jnp.zeros_like(dg_ref)
            db_ref[...] = jnp.zeros_like(db_ref)

        @pl.when(k == 0)
        def _():
            acc[...] = ALPHA * dr2_ref[...]

        acc[...] += _dot(d_ref[...], w_ref[...], NT)

        @pl.when(k == nk - 1)
        def _():
            _, xhat, rstd = _ln_fwd(r_ref[...], g_ref[...], b_ref[...])
            dy = acc[...]
            dg_ref[...] += jnp.sum(dy * xhat, axis=0, keepdims=True)
            db_ref[...] += jnp.sum(dy, axis=0, keepdims=True)
            dr1_ref[...] = _ln_bwd(dy, xhat, rstd, g_ref[...])

    row = pl.BlockSpec((tm, D_MODEL), lambda i, k: (i, 0))
    vec = _const_spec((1, D_MODEL))
    return pl.pallas_call(
        body, name="ffn_bwd_in", grid=(T // tm, nk),
        in_specs=[pl.BlockSpec((tm, tk), lambda i, k: (i, k)), pl.BlockSpec((D_MODEL, tk), lambda i, k: (0, k)),
                  row, row, vec, vec],
        out_specs=[row, vec, vec],
        out_shape=[jax.ShapeDtypeStruct((T, D_MODEL), F32), jax.ShapeDtypeStruct((1, D_MODEL), F32),
                   jax.ShapeDtypeStruct((1, D_MODEL), F32)],
        scratch_shapes=[pltpu.VMEM((tm, D_MODEL), F32)],
        compiler_params=_cp(("arbitrary", "arbitrary")))(du_pre, w_up, dr2, r1, g1, b1)


def _mix_bwd(dr1, w_mo, w_bs, w_ba, bs, ba, proj_main, b_gate, tm=512):
    T = dr1.shape[0]

    def body(d_ref, wmo_ref, wbs_ref, wba_ref, bs_ref, ba_ref, gs_ref, ga_ref, bg_ref,
             dg_ref, dbs_ref, dba_ref, dys_ref, dya_ref, dbg_ref):
        @pl.when(pl.program_id(0) == 0)
        def _():
            dbg_ref[...] = jnp.zeros_like(dbg_ref)

        dm = _dot(d_ref[...].astype(BF16), wmo_ref[...], NT)
        g_s = _sigmoid(gs_ref[...] + bg_ref[:, 0:D_MODEL])
        g_a = _sigmoid(ga_ref[...] + bg_ref[:, D_MODEL:])
        dgs = dm * bs_ref[...] * g_s * (1.0 - g_s)
        dga = dm * ba_ref[...] * g_a * (1.0 - g_a)
        dg_ref[:, 0:D_MODEL] = dgs.astype(BF16)
        dg_ref[:, D_MODEL:] = dga.astype(BF16)
        dbg_ref[:, 0:D_MODEL] += jnp.sum(dgs, axis=0, keepdims=True)
        dbg_ref[:, D_MODEL:] += jnp.sum(dga, axis=0, keepdims=True)
        dbs = (dm * g_s).astype(BF16)
        dba = (dm * g_a).astype(BF16)
        dbs_ref[...] = dbs
        dba_ref[...] = dba
        dys_ref[...] = _dot(dbs, wbs_ref[...], NT)
        dya_ref[...] = _dot(dba, wba_ref[...], NT).astype(BF16)

    row = lambda w, off=0: pl.BlockSpec((tm, w), lambda i: (i, off))
    return pl.pallas_call(
        body, name="mix_bwd", grid=(T // tm,),
        in_specs=[row(D_MODEL), _const_spec((D_MODEL, D_MODEL)), _const_spec((SSM_INNER, D_MODEL)),
                  _const_spec((D_MODEL, D_MODEL)), row(D_MODEL), row(D_MODEL),
                  row(D_MODEL, GATE_OFF // D_MODEL), row(D_MODEL, GATE_OFF // D_MODEL + 1), _const_spec((1, 2 * D_MODEL))],
        out_specs=[row(2 * D_MODEL, GATE_OFF // (2 * D_MODEL)), row(D_MODEL), row(D_MODEL), row(SSM_INNER), row(D_MODEL),
                   _const_spec((1, 2 * D_MODEL))],
        out_shape=[jax.ShapeDtypeStruct((T, MAIN_COLS), BF16), jax.ShapeDtypeStruct((T, D_MODEL), BF16),
                   jax.ShapeDtypeStruct((T, D_MODEL), BF16), jax.ShapeDtypeStruct((T, SSM_INNER), F32),
                   jax.ShapeDtypeStruct((T, D_MODEL), BF16), jax.ShapeDtypeStruct((1, 2 * D_MODEL), F32)],
        compiler_params=_cp(("arbitrary",)))(dr1, w_mo, w_bs, w_ba, bs, ba, proj_main, proj_main, b_gate)


def _local_step(x, target, w, p):
    proj_main = _matmul(x, w["in_main"], "nn", F32, "in_proj_main")
    proj_tail = _matmul(x, w["in_tail"], "nn", F32, "in_proj_tail", tn=TAIL_COLS)
    xbc = _conv_fwd(proj_main, XBC_OFF, XBC_COLS, p["ssm_conv_w"], p["ssm_conv_b"], SSM_CONV, "silu", "ssm_conv_fwd")
    y_ssm, ypre, hs = _ssd_fwd(xbc, proj_main, proj_tail, p["ssm_dt_bias"], p["ssm_a_log"], p["d_exp"], p["ssm_norm_w"])
    y_attn, lse = _attn_fwd(proj_main, proj_tail, p["bias_tbl"], p["attn_sinks"])
    merged, bs, ba = _merge_fwd(y_ssm, y_attn, proj_main, p["b_gate"], w["bs"], w["ba"])
    r1, h1 = _mix_ln1(merged, w["mo"], x, p["ln1_g"], p["ln1_b"])
    u_pre = _matmul(h1, w["up"], "nn", F32, "ffn_up", tn=1408)
    u = _conv_fwd(u_pre, 0, 2 * D_FF, p["ffn_conv_w"], p["ffn_conv_b"], FFN_CONV, None, "ffn_conv_fwd", tc=1408)
    act = _glu(u)
    dr2, dg2, db2, sq = _down_ln2_loss(act, w["down"], h1, target, p["ln2_g"], p["ln2_b"])
    g = {"ln2_g": dg2, "ln2_b": db2}
    g["w_down"] = _matmul(act, dr2, "tn", F32, "dw_down", tm=1408, tn=1024, tk=512)
    du = _ffn_bwd_act(dr2, w["down"], u)
    du_pre, g["ffn_conv_w"], g["ffn_conv_b"] = _conv_bwd(
        du, u_pre, 0, 2 * D_FF, p["ffn_conv_w"], p["ffn_conv_b"], FFN_CONV, None,
        jax.ShapeDtypeStruct((x.shape[0], 2 * D_FF), BF16), 0, "ffn_conv_bwd", tc=1408)
    g["w_up"] = _matmul(h1, du_pre, "tn", F32, "dw_up", tm=1024, tn=1408, tk=512)
    dr1, g["ln1_g"], g["ln1_b"] = _ffn_bwd_in(du_pre, w["up"], dr2, r1, p["ln1_g"], p["ln1_b"])
    g["w_mix_out"] = _matmul(merged, dr1, "tn", F32, "dw_mix_out", tm=1024, tn=1024, tk=512)
    dmain, dbs, dba, dy_ssm, dy_attn, g["b_gate"] = _mix_bwd(dr1, w["mo"], w["bs"], w["ba"], bs, ba, proj_main, p["b_gate"])
    g["w_branch_ssm"] = _matmul(y_ssm, dbs, "tn", F32, "dw_branch_ssm", tm=1024, tn=1024, tk=512)
    g["w_branch_attn"] = _matmul(y_attn, dba, "tn", F32, "dw_branch_attn", tm=1024, tn=1024, tk=512)
    dmain, dtail, dbias, g["attn_sinks"] = _attn_bwd(dy_attn, lse, proj_main, proj_tail, p["bias_tbl"], p["attn_sinks"], dmain)
    g["rel_bias"] = _rel_bias_grad(dbias.reshape(ATTN_HEADS, WINDOW * 2 * WINDOW), p["bucket_onehot"]).T
    dmain, dtail, dxbc, g["ssm_norm_w"], g["ssm_d"], g["ssm_a_log"], g["ssm_dt_bias"] = _ssd_bwd(
        dy_ssm, ypre, xbc, hs, proj_main, proj_tail, p["ssm_dt_bias"], p["ssm_a_log"], p["d_exp"], p["ssm_norm_w"],
        dmain, dtail)
    dmain, g["ssm_conv_w"], g["ssm_conv_b"] = _conv_bwd(
        dxbc, proj_main, XBC_OFF, XBC_COLS, p["ssm_conv_w"], p["ssm_conv_b"], SSM_CONV, "silu", dmain, XBC_OFF,
        "ssm_conv_bwd")
    g["in_main"] = _matmul(x, dmain, "tn", F32, "dw_in_main", tm=1024, tn=1024, tk=512)
    g["in_tail"] = _matmul(x, dtail, "tn", F32, "dw_in_tail", tm=1024, tn=TAIL_COLS, tk=512)
    dx = _matmul(dmain, w["in_main"], "nt", F32, "dx_main", addend=dr1, addend_scale=ALPHA)
    dx = _matmul(dtail, w["in_tail"], "nt", F32, "dx_tail", tk=TAIL_COLS, addend=dx)
    return sq, dx, g


def _split_w_in(w):
    seg = lambda off, n: w[:, off:off + n]
    main = jnp.concatenate([seg(O_Z, 2048), seg(O_XBC, XBC_COLS), seg(O_Q, D_MODEL), seg(O_GATE, 2 * D_MODEL)], axis=1)
    tail = jnp.concatenate([seg(O_K, 128), seg(O_V, 128), seg(O_DT, SSM_HEADS),
                            jnp.zeros((w.shape[0], 128 - SSM_HEADS), w.dtype)], axis=1)
    return main, tail


def _join_w_in(main, tail):
    return jnp.concatenate([main[:, Z_OFF:Z_OFF + 2048], main[:, XBC_OFF:XBC_OFF + XBC_COLS],
                            tail[:, DT_OFF:DT_OFF + SSM_HEADS], main[:, Q_OFF:Q_OFF + D_MODEL],
                            tail[:, K_OFF:K_OFF + 128], tail[:, V_OFF:V_OFF + 128],
                            main[:, GATE_OFF:GATE_OFF + 2 * D_MODEL]], axis=1)


def _prep_weights(w_in, w_bs, w_ba, w_mo, w_up, w_down):
    main, tail = _split_w_in(w_in.astype(BF16))
    return {"in_main": main, "in_tail": tail, "bs": w_bs.astype(BF16), "ba": w_ba.astype(BF16),
            "mo": w_mo.astype(BF16), "up": w_up.astype(BF16), "down": w_down.astype(BF16)}


def _prep_params(rel_bias, b_gate, ssm_conv_w, ssm_conv_b, ssm_dt_bias, ssm_a_log, ssm_d, ssm_norm_w, attn_sinks,
                 ln1_g, ln1_b, ffn_conv_w, ffn_conv_b, ln2_g, ln2_b):
    bucket, in_window = _band_geometry()
    bias_tbl = jnp.where(in_window[None], jnp.transpose(rel_bias[bucket], (2, 0, 1)), NEG)
    onehot = jnp.logical_and(bucket.reshape(-1, 1) == jnp.arange(REL_BUCKETS)[None, :],
                             in_window.reshape(-1, 1)).astype(F32)
    return {"bias_tbl": bias_tbl, "bucket_onehot": onehot, "b_gate": b_gate, "ssm_conv_w": ssm_conv_w,
            "ssm_conv_b": ssm_conv_b, "ssm_dt_bias": ssm_dt_bias, "ssm_a_log": ssm_a_log,
            "d_exp": jnp.repeat(ssm_d, SSMD, axis=1), "ssm_norm_w": ssm_norm_w, "attn_sinks": attn_sinks,
            "ln1_g": ln1_g, "ln1_b": ln1_b, "ffn_conv_w": ffn_conv_w, "ffn_conv_b": ffn_conv_b,
            "ln2_g": ln2_g, "ln2_b": ln2_b}


def _mesh_pos():
    return lax.axis_index("x"), lax.axis_index("y"), lax.axis_index("c")


def _all_gather(shard, name):
    R, C = shard.shape

    def body(x_ref, out_ref, send_sems, recv_sems, local_sem):
        x, y, c = _mesh_pos()
        me, sibling = (x, y, c), (x, y, 1 - c)
        chips = [(1 - x, y), (x, 1 - y), (1 - x, 1 - y)]

        def slot(px, py, pc):
            return out_ref.at[4 * px + 2 * py + pc]

        def copy(k, block, to, src=None):
            return pltpu.make_async_remote_copy(
                src_ref=slot(*block) if src is None else src, dst_ref=slot(*block),
                send_sem=send_sems.at[k], recv_sem=recv_sems.at[k], device_id=to, device_id_type=MESH_ID)

        mine = pltpu.make_async_copy(x_ref, slot(*me), local_sem)
        mine.start()
        first = [copy(0, me, sibling, src=x_ref)]
        first += [copy(1 + j, me, (*chip, c), src=x_ref) for j, chip in enumerate(chips)]
        for cp in first:
            cp.start()
        passed = [copy(4 + j, (*chip, c), sibling) for j, chip in enumerate(chips)]
        for j, chip in enumerate(chips):
            copy(1 + j, (*chip, c), me).wait_recv()
            passed[j].start()
        copy(0, sibling, me).wait_recv()
        for j, chip in enumerate(chips):
            copy(4 + j, (*chip, 1 - c), me).wait_recv()
        for cp in first + passed:
            cp.wait_send()
        mine.wait()

    return pl.pallas_call(
        body, name=name, out_shape=jax.ShapeDtypeStruct((N_DEV, R, C), shard.dtype),
        in_specs=[pl.BlockSpec(memory_space=pl.ANY)], out_specs=pl.BlockSpec(memory_space=pl.ANY),
        scratch_shapes=[pltpu.SemaphoreType.DMA((7,)), pltpu.SemaphoreType.DMA((7,)), pltpu.SemaphoreType.DMA],
    )(shard)


def _exchange(parts, name):
    _, R, C = parts.shape

    def body(in_ref, out_ref, send_sems, recv_sems, local_sem):
        x, y, c = _mesh_pos()
        me = 4 * x + 2 * y + c
        mine = pltpu.make_async_copy(in_ref.at[me], out_ref.at[me], local_sem)
        mine.start()
        copies = []
        for r in range(1, N_DEV):
            px = 1 - x if r & 4 else x
            py = 1 - y if r & 2 else y
            pc = 1 - c if r & 1 else c
            cp = pltpu.make_async_remote_copy(
                src_ref=in_ref.at[4 * px + 2 * py + pc], dst_ref=out_ref.at[me],
                send_sem=send_sems.at[r - 1], recv_sem=recv_sems.at[r - 1],
                device_id=(px, py, pc), device_id_type=MESH_ID)
            cp.start()
            copies.append(cp)
        for cp in copies:
            cp.wait()
        mine.wait()

    return pl.pallas_call(
        body, name=name, out_shape=jax.ShapeDtypeStruct(parts.shape, parts.dtype),
        in_specs=[pl.BlockSpec(memory_space=pl.ANY)], out_specs=pl.BlockSpec(memory_space=pl.ANY),
        scratch_shapes=[pltpu.SemaphoreType.DMA((7,)), pltpu.SemaphoreType.DMA((7,)), pltpu.SemaphoreType.DMA],
    )(parts)


def _slot_sum(slots, tr=512):
    _, R, C = slots.shape
    assert R % tr == 0

    def body(s_ref, o_ref):
        acc = s_ref[0]
        for i in range(1, N_DEV):
            acc = acc + s_ref[i]
        o_ref[...] = acc

    return pl.pallas_call(
        body, name="slot_sum", grid=(R // tr,),
        in_specs=[pl.BlockSpec((N_DEV, tr, C), lambda i: (0, i, 0))], out_specs=pl.BlockSpec((tr, C), lambda i: (i, 0)),
        out_shape=jax.ShapeDtypeStruct((R, C), slots.dtype), compiler_params=_cp(("parallel",)))(slots)


def _adamw_math(w, g, m, v):
    m = ADAM_B1 * m + (1.0 - ADAM_B1) * g
    v = ADAM_B2 * v + (1.0 - ADAM_B2) * (g * g)
    m_hat = m / (1.0 - ADAM_B1 ** ADAM_STEP)
    v_hat = v / (1.0 - ADAM_B2 ** ADAM_STEP)
    return -ADAM_LR * (m_hat / (jnp.sqrt(v_hat) + ADAM_EPS) + ADAM_WD * w), m, v


def _adamw(w, g, m, v, name):
    R, C = w.shape
    tr = 256 if R % 256 == 0 and R > 256 else R

    def body(w_ref, g_ref, m_ref, v_ref, d_ref, nm_ref, nv_ref):
        d_ref[...], nm_ref[...], nv_ref[...] = _adamw_math(w_ref[...], g_ref[...], m_ref[...], v_ref[...])

    spec = pl.BlockSpec((tr, C), lambda i: (i, 0))
    return pl.pallas_call(
        body, name=name, grid=(R // tr,), in_specs=[spec] * 4, out_specs=[spec] * 3,
        out_shape=[jax.ShapeDtypeStruct((R, C), F32)] * 3, compiler_params=_cp(("parallel",)))(w, g, m, v)


def _small_update(gathered, w, m, v):
    def body(s_ref, w_ref, m_ref, v_ref, g_ref, d_ref, nm_ref, nv_ref):
        g = s_ref[0]
        for i in range(1, N_DEV):
            g = g + s_ref[i]
        g_ref[...] = g
        d_ref[...], nm_ref[...], nv_ref[...] = _adamw_math(w_ref[...], g, m_ref[...], v_ref[...])

    return pl.pallas_call(body, name="small_update", out_shape=[jax.ShapeDtypeStruct(w.shape, F32)] * 4)(gathered, w, m, v)


LANES = 128
BIG = (("w_in", 8480, "cols"), ("w_branch_ssm", 2048, "rows"), ("w_branch_attn", 1024, "rows"),
       ("w_mix_out", 1024, "rows"), ("w_up", 5632, "cols"), ("w_down", 2816, "rows"))
CONVW = (("ssm_conv_w", 16), ("ffn_conv_w", 24))
SMALL = ("rel_bias", "b_gate", "ssm_conv_b", "ssm_dt_bias", "ssm_a_log", "ssm_d", "ssm_norm_w", "attn_sinks",
         "ln1_g", "ln1_b", "ffn_conv_b", "ln2_g", "ln2_b")
WEIGHTS = ("rel_bias", "w_in", "b_gate", "ssm_conv_w", "ssm_conv_b", "ssm_dt_bias", "ssm_a_log", "ssm_d", "ssm_norm_w",
           "attn_sinks", "w_branch_ssm", "w_branch_attn", "w_mix_out", "ln1_g", "ln1_b", "w_up", "ffn_conv_w",
           "ffn_conv_b", "w_down", "ln2_g", "ln2_b")
EXCHANGE_ROWS = 21504
SMALL_ROWS = 144


def _rows(a, rows):
    flat = a.reshape(-1)
    return jnp.pad(flat, (0, rows * LANES - flat.shape[0])).reshape(rows, LANES)


def _rows8(a, rows):
    return jnp.pad(a, ((0, 0), (0, rows * LANES - a.shape[1]))).reshape(N_DEV, rows, LANES)


def _by_device(full, how):
    r, c = full.shape
    if how == "rows":
        return full.reshape(N_DEV, -1)
    return full.reshape(r, N_DEV, c // N_DEV).transpose(1, 0, 2).reshape(N_DEV, -1)


def _from_devices(slots, r, c, how):
    if how == "rows":
        return slots.reshape(r, c)
    return slots.reshape(N_DEV, r, c // N_DEV).transpose(1, 0, 2).reshape(r, c)


def kernel(x, rel_bias, w_in, b_gate, ssm_conv_w, ssm_conv_b, ssm_dt_bias, ssm_a_log, ssm_d, ssm_norm_w, attn_sinks, w_branch_ssm, w_branch_attn, w_mix_out, ln1_g, ln1_b, w_up, ffn_conv_w, ffn_conv_b, w_down, ln2_g, ln2_b, loss_target, m_rel_bias, m_w_in, m_b_gate, m_ssm_conv_w, m_ssm_conv_b, m_ssm_dt_bias, m_ssm_a_log, m_ssm_d, m_ssm_norm_w, m_attn_sinks, m_w_branch_ssm, m_w_branch_attn, m_w_mix_out, m_ln1_g, m_ln1_b, m_w_up, m_ffn_conv_w, m_ffn_conv_b, m_w_down, m_ln2_g, m_ln2_b, v_rel_bias, v_w_in, v_b_gate, v_ssm_conv_w, v_ssm_conv_b, v_ssm_dt_bias, v_ssm_a_log, v_ssm_d, v_ssm_norm_w, v_attn_sinks, v_w_branch_ssm, v_w_branch_attn, v_w_mix_out, v_ln1_g, v_ln1_b, v_w_up, v_ffn_conv_w, v_ffn_conv_b, v_w_down, v_ln2_g, v_ln2_b):
    W = dict(zip(WEIGHTS, (rel_bias, w_in, b_gate, ssm_conv_w, ssm_conv_b, ssm_dt_bias, ssm_a_log, ssm_d, ssm_norm_w,
                           attn_sinks, w_branch_ssm, w_branch_attn, w_mix_out, ln1_g, ln1_b, w_up, ffn_conv_w,
                           ffn_conv_b, w_down, ln2_g, ln2_b)))
    M = dict(zip(WEIGHTS, (m_rel_bias, m_w_in, m_b_gate, m_ssm_conv_w, m_ssm_conv_b, m_ssm_dt_bias, m_ssm_a_log, m_ssm_d,
                           m_ssm_norm_w, m_attn_sinks, m_w_branch_ssm, m_w_branch_attn, m_w_mix_out, m_ln1_g, m_ln1_b,
                           m_w_up, m_ffn_conv_w, m_ffn_conv_b, m_w_down, m_ln2_g, m_ln2_b)))
    V = dict(zip(WEIGHTS, (v_rel_bias, v_w_in, v_b_gate, v_ssm_conv_w, v_ssm_conv_b, v_ssm_dt_bias, v_ssm_a_log, v_ssm_d,
                           v_ssm_norm_w, v_attn_sinks, v_w_branch_ssm, v_w_branch_attn, v_w_mix_out, v_ln1_g, v_ln1_b,
                           v_w_up, v_ffn_conv_w, v_ffn_conv_b, v_w_down, v_ln2_g, v_ln2_b)))
    shard2d = lambda a: a.reshape(a.shape[-2], a.shape[-1])

    big_shard = jnp.concatenate([_rows(shard2d(W[n]).astype(BF16), rows) for n, rows, _ in BIG], axis=0)
    conv_shard = jnp.concatenate([_rows(shard2d(W[n]), rows) for n, rows in CONVW], axis=0)
    big_all = _all_gather(big_shard, "gather_matmul_weights")
    conv_all = _all_gather(conv_shard, "gather_conv_weights")
    full = {}
    off = 0
    for n, rows, how in BIG:
        r, c = shard2d(W[n]).shape
        fr, fc = (r * N_DEV, c) if how == "rows" else (r, c * N_DEV)
        full[n] = _from_devices(big_all[:, off:off + rows].reshape(N_DEV, -1), fr, fc, how)
        off += rows
    off = 0
    for n, rows in CONVW:
        r, c = shard2d(W[n]).shape
        full[n] = _from_devices(conv_all[:, off:off + rows].reshape(N_DEV, -1)[:, :r * c], r, c * N_DEV, "cols")
        off += rows

    w = _prep_weights(full["w_in"], full["w_branch_ssm"], full["w_branch_attn"], full["w_mix_out"], full["w_up"],
                      full["w_down"])
    p = _prep_params(rel_bias, b_gate, full["ssm_conv_w"], ssm_conv_b, ssm_dt_bias, ssm_a_log, ssm_d, ssm_norm_w,
                     attn_sinks, ln1_g, ln1_b, full["ffn_conv_w"], ffn_conv_b, ln2_g, ln2_b)
    sq, dx, g = _local_step(x[0], loss_target[0], w, p)
    g["w_in"] = _join_w_in(g.pop("in_main"), g.pop("in_tail"))
    loss = (0.5 / D_MODEL) * lax.psum(sq[0, 0], ("x", "y", "c"))

    parts = [_rows8(_by_device(g[n], how), rows) for n, rows, how in BIG]
    parts += [_rows8(_by_device(g[n], "cols"), rows) for n, rows in CONVW]
    used = sum(rows for _, rows, _ in BIG) + sum(rows for _, rows in CONVW)
    parts.append(jnp.zeros((N_DEV, EXCHANGE_ROWS - used, LANES), F32))
    g_rows = _slot_sum(_exchange(jnp.concatenate(parts, axis=1), "exchange_weight_grads"))
    grads, deltas, new_m, new_v = {}, {}, {}, {}
    off = 0
    for n, rows in [(n, rows) for n, rows, _ in BIG] + list(CONVW):
        r, c = shard2d(W[n]).shape
        gn = g_rows[off:off + rows].reshape(-1)[:r * c].reshape(r, c)
        off += rows
        d, nm, nv = _adamw(shard2d(W[n]), gn, shard2d(M[n]), shard2d(V[n]), "adamw_" + n)
        grads[n], deltas[n], new_m[n], new_v[n] = (a.reshape(W[n].shape) for a in (gn, d, nm, nv))

    pack = lambda src: _rows(jnp.concatenate([src[n].reshape(-1) for n in SMALL]), SMALL_ROWS)
    small_all = _all_gather(pack(g), "gather_small_grads")
    outs = _small_update(small_all, pack(W), pack(M), pack(V))
    off = 0
    for n in SMALL:
        size = W[n].size
        grads[n], deltas[n], new_m[n], new_v[n] = (a.reshape(-1)[off:off + size].reshape(W[n].shape) for a in outs)
        off += size

    return (loss, dx[None], *[grads[n] for n in WEIGHTS], *[deltas[n] for n in WEIGHTS],
            *[new_m[n] for n in WEIGHTS], *[new_v[n] for n in WEIGHTS])
```

```python
import functools
import math

import jax
import jax.numpy as jnp
from jax import lax
from jax.experimental import pallas as pl
from jax.experimental.pallas import tpu as pltpu

F32, BF16 = jnp.float32, jnp.bfloat16
HIGHEST = lax.Precision.HIGHEST
MESH_ID = pl.DeviceIdType.MESH

N_DEV = 8
D_MODEL = 1024
SSM_INNER = 2048
SSM_HEADS = 32
SSM_HEADDIM = 64
SSMD = SSM_HEADDIM
SSM_GROUPS = 4
SSM_GROUP_COLS = SSM_INNER // SSM_GROUPS
SSM_STATE = 128
SSM_CONV = 4
CHUNK = 128
XBC_COLS = SSM_INNER + 2 * SSM_GROUPS * SSM_STATE
B_OFF = SSM_INNER
C_OFF = SSM_INNER + SSM_GROUPS * SSM_STATE
ATTN_HEADS = 16
ATTN_KV = 2
ATTN_GROUP = 8
HEADDIM = 64
WINDOW = 128
REL_BUCKETS = 32
REL_MAX_DIST = 128
D_FF = 2816
FFN_CONV = 3
ALPHA = 2.0 ** 0.25
LN_EPS = 1e-5
RMS_EPS = 1e-5
IN_COLS = 8480
Z_OFF, XBC_OFF, Q_OFF, GATE_OFF, MAIN_COLS = 0, 2048, 5120, 6144, 8192
K_OFF, V_OFF, DT_OFF, TAIL_COLS = 0, 128, 256, 384
O_Z, O_XBC, O_DT, O_Q, O_K, O_V, O_GATE = 0, 2048, 5120, 5152, 6176, 6304, 6432

ADAM_LR, ADAM_B1, ADAM_B2, ADAM_EPS, ADAM_WD, ADAM_STEP = 0.001, 0.9, 0.999, 1e-08, 0.01, 10
NEG = -1e30
HALO = 8
VMEM_LIMIT = 56 * 1024 * 1024


def _cp(sem):
    return pltpu.CompilerParams(dimension_semantics=sem, vmem_limit_bytes=VMEM_LIMIT)


def _const_spec(shape):
    nd = len(shape)
    return pl.BlockSpec(shape, lambda *_: (0,) * nd)


def _sigmoid(x):
    return 1.0 / (1.0 + jnp.exp(-x))


def _softplus(x):
    return jnp.maximum(x, 0.0) + jnp.log1p(jnp.exp(-jnp.abs(x)))


def _dot(a, b, dims=(((1,), (0,)), ((), ())), precision=None):
    return lax.dot_general(a, b, dims, preferred_element_type=F32, precision=precision)


NN = (((1,), (0,)), ((), ()))
NT = (((1,), (1,)), ((), ()))
TN = (((0,), (0,)), ((), ()))


def _mesh_pos():
    return lax.axis_index("x"), lax.axis_index("y"), lax.axis_index("c")


def _exchange_copies(in_ref, out_ref, send_sems, recv_sems, local_sem):
    x, y, c = _mesh_pos()
    me = 4 * x + 2 * y + c
    copies = [pltpu.make_async_copy(in_ref.at[me], out_ref.at[me], local_sem)]
    for r in range(1, N_DEV):
        px = 1 - x if r & 4 else x
        py = 1 - y if r & 2 else y
        pc = 1 - c if r & 1 else c
        copies.append(pltpu.make_async_remote_copy(
            src_ref=in_ref.at[4 * px + 2 * py + pc], dst_ref=out_ref.at[me],
            send_sem=send_sems.at[r - 1], recv_sem=recv_sems.at[r - 1],
            device_id=(px, py, pc), device_id_type=MESH_ID))
    return copies


def _matmul(a, b, mode, out_dtype, name, tm=512, tn=1024, tk=1024, addend=None, addend_scale=1.0, exchange=None):
    if mode == "nn":
        (M, K), (K2, N) = a.shape, b.shape
    elif mode == "nt":
        (M, K), (N, K2) = a.shape, b.shape
    else:
        (K, M), (K2, N) = a.shape, b.shape
    assert K == K2, (a.shape, b.shape, mode)
    tm, tn, tk = min(tm, M), min(tn, N), min(tk, K)
    assert M % tm == 0 and N % tn == 0 and K % tk == 0, (M, N, K, tm, tn, tk)
    nk = K // tk
    dims = {"nn": NN, "nt": NT, "tn": TN}[mode]
    a_spec = pl.BlockSpec((tk, tm), lambda i, j, k: (k, i)) if mode == "tn" else pl.BlockSpec((tm, tk), lambda i, j, k: (i, k))
    b_spec = pl.BlockSpec((tn, tk), lambda i, j, k: (j, k)) if mode == "nt" else pl.BlockSpec((tk, tn), lambda i, j, k: (k, j))
    o_spec = pl.BlockSpec((tm, tn), lambda i, j, k: (i, j))

    ni, nj = M // tm, N // tn

    def body(*refs):
        refs = list(refs)
        a_ref, b_ref = refs[:2]
        c_ref = refs[2] if addend is not None else None
        n_in = 2 + (addend is not None) + (exchange is not None)
        o_ref, acc = refs[n_in], refs[n_in + 1 + (exchange is not None)]
        i, j, k = pl.program_id(0), pl.program_id(1), pl.program_id(2)
        if exchange is not None:
            copies = lambda: _exchange_copies(refs[n_in - 1], refs[n_in + 1], *refs[n_in + 3:])

            @pl.when(jnp.logical_and(i == 0, jnp.logical_and(j == 0, k == 0)))
            def _():
                for cp in copies():
                    cp.start()

        @pl.when(k == 0)
        def _():
            acc[...] = jnp.zeros_like(acc)

        acc[...] += _dot(a_ref[...].astype(BF16), b_ref[...].astype(BF16), dims)

        @pl.when(k == nk - 1)
        def _():
            r = acc[...]
            if addend is not None:
                r = r + addend_scale * c_ref[...].astype(F32)
            o_ref[...] = r.astype(out_dtype)

        if exchange is not None:
            @pl.when(jnp.logical_and(i == ni - 1, jnp.logical_and(j == nj - 1, k == nk - 1)))
            def _():
                for cp in copies():
                    cp.wait()

    in_specs = [a_spec, b_spec] + ([o_spec] if addend is not None else [])
    args = (a, b) + ((addend,) if addend is not None else ())
    out_specs, out_shape = o_spec, jax.ShapeDtypeStruct((M, N), out_dtype)
    scratch = [pltpu.VMEM((tm, tn), F32)]
    sem = ("parallel", "parallel", "arbitrary")
    if exchange is not None:
        any_spec = pl.BlockSpec(memory_space=pl.ANY)
        in_specs, args = in_specs + [any_spec], args + (exchange,)
        out_specs, out_shape = [o_spec, any_spec], [out_shape, jax.ShapeDtypeStruct(exchange.shape, exchange.dtype)]
        scratch += [pltpu.SemaphoreType.DMA((N_DEV - 1,)), pltpu.SemaphoreType.DMA((N_DEV - 1,)), pltpu.SemaphoreType.DMA]
        sem = ("arbitrary", "arbitrary", "arbitrary")
    return pl.pallas_call(
        body, name=name, grid=(ni, nj, nk), in_specs=in_specs, out_specs=out_specs, out_shape=out_shape,
        scratch_shapes=scratch, compiler_params=_cp(sem))(*args)


def _conv_fwd(pre, pre_col_off, C, w, b, K, act, name, tr=512, tc=1024):
    T = pre.shape[0]
    tc = min(tc, C)
    assert T % tr == 0 and C % tc == 0 and pre_col_off % tc == 0
    joff = pre_col_off // tc
    hb = tr // HALO

    def body(x_ref, xp_ref, w_ref, b_ref, o_ref, ext):
        i = pl.program_id(1)
        ext[0:HALO, :] = jnp.where(i > 0, xp_ref[...], 0.0)
        ext[HALO:HALO + tr, :] = x_ref[...]
        acc = b_ref[...] + w_ref[K - 1:K, :] * x_ref[...]
        for k in range(K - 1):
            s = K - 1 - k
            acc = acc + w_ref[k:k + 1, :] * ext[HALO - s:HALO - s + tr, :]
        if act == "silu":
            acc = acc * _sigmoid(acc)
        o_ref[...] = acc

    return pl.pallas_call(
        body, name=name, grid=(C // tc, T // tr),
        in_specs=[pl.BlockSpec((tr, tc), lambda j, i: (i, joff + j)),
                  pl.BlockSpec((HALO, tc), lambda j, i: (jnp.maximum(i * hb - 1, 0), joff + j)),
                  pl.BlockSpec((K, tc), lambda j, i: (0, j)),
                  pl.BlockSpec((1, tc), lambda j, i: (0, j))],
        out_specs=pl.BlockSpec((tr, tc), lambda j, i: (i, j)),
        out_shape=jax.ShapeDtypeStruct((T, C), F32),
        scratch_shapes=[pltpu.VMEM((tr + HALO, tc), F32)],
        compiler_params=_cp(("parallel", "arbitrary")))(pre, pre, w, b)


def _conv_bwd(dout, pre, pre_col_off, C, w, b, K, act, dst, dst_col_off, name, tr=512, tc=1024):
    T = pre.shape[0]
    tc = min(tc, C)
    assert T % tr == 0 and C % tc == 0 and pre_col_off % tc == 0 and dst_col_off % tc == 0
    joff, doff = pre_col_off // tc, dst_col_off // tc
    hb = tr // HALO
    nt = T // tr
    last_hblock = T // HALO - 1

    def body(g_ref, gn_ref, x_ref, xp_ref, xn_ref, w_ref, b_ref, *rest):
        o_ref, dw_ref, db_ref, ext, gext = rest[-5:]
        i = pl.program_id(1)
        not_last = i < nt - 1
        ext[0:HALO, :] = jnp.where(i > 0, xp_ref[...], 0.0)
        ext[HALO:HALO + tr, :] = x_ref[...]
        ext[HALO + tr:, :] = jnp.where(not_last, xn_ref[...], 0.0)
        g = g_ref[...]
        gn = jnp.where(not_last, gn_ref[...], 0.0)
        if act == "silu":
            co = b_ref[...] + w_ref[K - 1:K, :] * ext[HALO:, :]
            for k in range(K - 1):
                s = K - 1 - k
                co = co + w_ref[k:k + 1, :] * ext[HALO - s:2 * HALO - s + tr, :]
            sg = _sigmoid(co)
            dact = sg * (1.0 + co * (1.0 - sg))
            g = g * dact[0:tr, :]
            gn = gn * dact[tr:, :]
        gext[0:tr, :] = g
        gext[tr:, :] = gn
        dpre = w_ref[K - 1:K, :] * g
        for k in range(K - 1):
            s = K - 1 - k
            dpre = dpre + w_ref[k:k + 1, :] * gext[s:s + tr, :]
        o_ref[...] = dpre.astype(o_ref.dtype)

        @pl.when(i == 0)
        def _():
            dw_ref[...] = jnp.zeros_like(dw_ref)
            db_ref[...] = jnp.zeros_like(db_ref)

        db_ref[...] += jnp.sum(g, axis=0, keepdims=True)
        for k in range(K):
            s = K - 1 - k
            dw_ref[k:k + 1, :] += jnp.sum(g * ext[HALO - s:HALO - s + tr, :], axis=0, keepdims=True)

    tile = lambda off: pl.BlockSpec((tr, tc), lambda j, i: (i, off + j))
    nxt = lambda off: pl.BlockSpec((HALO, tc), lambda j, i: (jnp.minimum((i + 1) * hb, last_hblock), off + j))
    in_specs = [tile(0), nxt(0), tile(joff),
                pl.BlockSpec((HALO, tc), lambda j, i: (jnp.maximum(i * hb - 1, 0), joff + j)), nxt(joff),
                pl.BlockSpec((K, tc), lambda j, i: (0, j)), pl.BlockSpec((1, tc), lambda j, i: (0, j))]
    args = (dout, dout, pre, pre, pre, w, b)
    if isinstance(dst, jax.ShapeDtypeStruct):
        aliases = {}
    else:
        in_specs.append(pl.BlockSpec(memory_space=pl.ANY))
        args += (dst,)
        aliases = {7: 0}
    return pl.pallas_call(
        body, name=name, grid=(C // tc, nt), in_specs=in_specs,
        out_specs=[tile(doff), pl.BlockSpec((K, tc), lambda j, i: (0, j)), pl.BlockSpec((1, tc), lambda j, i: (0, j))],
        out_shape=[jax.ShapeDtypeStruct(dst.shape, dst.dtype), jax.ShapeDtypeStruct((K, C), F32),
                   jax.ShapeDtypeStruct((1, C), F32)],
        scratch_shapes=[pltpu.VMEM((tr + 2 * HALO, tc), F32), pltpu.VMEM((tr + HALO, tc), F32)],
        input_output_aliases=aliases,
        compiler_params=_cp(("parallel", "arbitrary")))(*args)


def _ssd_bwd_v1(dyo, ypre, xbc, hs, proj_main, proj_tail, dtb, alog, d_exp, norm_w, dmain, dtail):
    T = xbc.shape[0]
    nc = T // CHUNK

    def body(dyo_ref, ypre_ref, xbc_ref, hs_ref, dt_ref, z_ref, dtb_ref, alog_ref, d_ref, nw_ref, dmain_in, dtail_in,
             dz_ref, ddt_ref, dxbc_ref, dnw_ref, dd_ref, dalog_ref, ddtb_ref, G, dacs_s):
        del dmain_in, dtail_in
        c = pl.program_id(0)

        @pl.when(c == 0)
        def _():
            G[...] = jnp.zeros_like(G)
            dnw_ref[...] = jnp.zeros_like(dnw_ref)
            dd_ref[...] = jnp.zeros_like(dd_ref)
            dalog_ref[...] = jnp.zeros_like(dalog_ref)
            ddtb_ref[...] = jnp.zeros_like(ddtb_ref)

        dt_raw = dt_ref[:, 0:SSM_HEADS]
        dt, a, acs, acsT, row, col = _chunk_scalars(dt_raw, dtb_ref[...], alog_ref[...])
        tril, triu = row >= col, col >= row
        eo = jnp.exp(acs)
        dst = jnp.exp(acs[CHUNK - 1:CHUNK, :] - acs)
        z = z_ref[...]
        sz = _sigmoid(z)
        silu_z = z * sz
        ypre = ypre_ref[...]
        yg = ypre * silu_z
        r = _group_rms(yg)
        dyn = dyo_ref[...] * nw_ref[...]
        dyg = []
        dnw = []
        for g in range(SSM_GROUPS):
            gs = slice(g * SSM_GROUP_COLS, (g + 1) * SSM_GROUP_COLS)
            ygn = yg[:, gs] * r[g]
            dnw.append(jnp.sum(dyo_ref[:, gs] * ygn, axis=0, keepdims=True))
            cm = jnp.mean(dyn[:, gs] * ygn, axis=1, keepdims=True)
            dyg.append(r[g] * (dyn[:, gs] - ygn * cm))
        dnw_ref[...] += jnp.concatenate(dnw, axis=1)
        dyg = jnp.concatenate(dyg, axis=1)
        dz_ref[...] = (dyg * ypre * (sz * (1.0 + z * (1.0 - sz)))).astype(dz_ref.dtype)
        dY = dyg * silu_z
        xs_all = xbc_ref[:, 0:SSM_INNER]
        dd_cols = jnp.sum(dY * xs_all, axis=0, keepdims=True)
        ddt_cols = []
        dacs_cols = []
        dcd_cols = []
        for g in range(SSM_GROUPS):
            gs = slice(g * SSM_GROUP_COLS, (g + 1) * SSM_GROUP_COLS)
            hg = slice(g * 8, (g + 1) * 8)
            Bg = xbc_ref[:, B_OFF + g * SSM_STATE:B_OFF + (g + 1) * SSM_STATE].astype(BF16)
            Cg = xbc_ref[:, C_OFF + g * SSM_STATE:C_OFF + (g + 1) * SSM_STATE].astype(BF16)
            Hg = hs_ref[0, :, gs]
            Hgb = Hg.astype(BF16)
            Gg = G[:, gs]
            Ggb = Gg.astype(BF16)
            dYg = dY[:, gs]
            xs = xs_all[:, gs]
            eo_e = _head_expand(eo[:, hg], CHUNK)
            dst_e = _head_expand(dst[:, hg], CHUNK)
            dt_e = _head_expand(dt[:, hg], CHUNK)
            cd_e = _head_expand(eo[CHUNK - 1:CHUNK, hg], 1)
            xdt = xs * dt_e
            CH = _dot(Cg, Hgb)
            dYe = dYg * eo_e
            dYeb = dYe.astype(BF16)
            dC = _dot(dYeb, Hgb, NT)
            BG = _dot(Bg, Ggb)
            xw = xdt * dst_e
            dB = _dot(xw.astype(BF16), Ggb, NT)
            q = _head_sum(xdt * BG * dst_e)
            dacs = _head_sum(dYe * CH) - q
            dcd = _head_sum(jnp.sum(Gg * Hg, axis=0, keepdims=True)) * eo[CHUNK - 1:CHUNK, hg]
            dcd_cols.append(dcd + jnp.sum(q, axis=0, keepdims=True))
            dacs_cols.append(dacs)
            G[:, gs] = Gg * cd_e + _dot(Cg, dYeb, TN)
            CB = _dot(Cg, Bg, NT)
            CBT = _dot(Bg, Cg, NT)
            dCB = jnp.zeros((CHUNK, CHUNK), F32)
            dxdt_d = []
            for e in range(8):
                h = g * 8 + e
                es = slice(e * SSMD, (e + 1) * SSMD)
                seg = acs[:, h:h + 1] - acsT[h:h + 1, :]
                L = jnp.exp(jnp.where(tril, seg, -jnp.inf))
                LT = jnp.exp(jnp.where(triu, -seg, -jnp.inf))
                dYh = dYg[:, es].astype(BF16)
                xh = xdt[:, es].astype(BF16)
                dM = _dot(dYh, xh, NT)
                dMT = _dot(xh, dYh, NT)
                M = CB * L
                MT = CBT * LT
                dxdt_d.append(_dot(MT.astype(BF16), dYh))
                dCB = dCB + dM * L
                dacs_s[h:h + 1, :] = (jnp.sum(dMT * MT, axis=0, keepdims=True)
                                      - jnp.sum(dM * M, axis=0, keepdims=True))
            dCBb = dCB.astype(BF16)
            dC = dC + _dot(dCBb, Bg)
            dB = dB + _dot(dCBb, Cg, TN)
            dxdt = jnp.concatenate(dxdt_d, axis=1) + dst_e * BG
            ddt_cols.append(_head_sum(dxdt * xs))
            dxbc_ref[:, gs] = dxdt * dt_e + dYg * d_ref[:, gs]
            dxbc_ref[:, B_OFF + g * SSM_STATE:B_OFF + (g + 1) * SSM_STATE] = dB
            dxbc_ref[:, C_OFF + g * SSM_STATE:C_OFF + (g + 1) * SSM_STATE] = dC
        dacs = jnp.concatenate(dacs_cols, axis=1) + dacs_s[...].T
        last = jnp.concatenate(dcd_cols, axis=1)
        dacs = dacs + jnp.where(lax.broadcasted_iota(jnp.int32, (CHUNK, SSM_HEADS), 0) == CHUNK - 1, last, 0.0)
        dadt = _dot(triu.astype(F32), dacs, NN, HIGHEST)
        ddt = dadt * a + jnp.concatenate(ddt_cols, axis=1)
        dalog_ref[...] += jnp.sum(dadt * dt, axis=0, keepdims=True) * a
        ddt_raw = ddt * _sigmoid(dt_raw + dtb_ref[...])
        ddtb_ref[...] += jnp.sum(ddt_raw, axis=0, keepdims=True)
        ddt_ref[...] = jnp.concatenate([ddt_raw, jnp.zeros((CHUNK, 128 - SSM_HEADS), F32)], axis=1).astype(ddt_ref.dtype)
        ehead = (lax.broadcasted_iota(jnp.int32, (SSM_INNER, SSM_HEADS), 0) // SSMD
                 == lax.broadcasted_iota(jnp.int32, (SSM_INNER, SSM_HEADS), 1)).astype(F32)
        dd_ref[...] += _dot(jnp.broadcast_to(dd_cols, (8, SSM_INNER)), ehead, NN, HIGHEST)[0:1, :]

    rev = lambda c: nc - 1 - c
    vec = lambda n: _const_spec((1, n))
    any_spec = pl.BlockSpec(memory_space=pl.ANY)
    outs = pl.pallas_call(
        body, name="ssd_bwd", grid=(nc,),
        in_specs=[pl.BlockSpec((CHUNK, SSM_INNER), lambda c: (rev(c), 0)),
                  pl.BlockSpec((CHUNK, SSM_INNER), lambda c: (rev(c), 0)),
                  pl.BlockSpec((CHUNK, XBC_COLS), lambda c: (rev(c), 0)),
                  pl.BlockSpec((1, SSM_STATE, SSM_INNER), lambda c: (rev(c), 0, 0)),
                  pl.BlockSpec((CHUNK, 128), lambda c: (rev(c), DT_OFF // 128)),
                  pl.BlockSpec((CHUNK, SSM_INNER), lambda c: (rev(c), Z_OFF // SSM_INNER)),
                  vec(SSM_HEADS), vec(SSM_HEADS), vec(SSM_INNER), vec(SSM_INNER), any_spec, any_spec],
        out_specs=[pl.BlockSpec((CHUNK, SSM_INNER), lambda c: (rev(c), Z_OFF // SSM_INNER)),
                   pl.BlockSpec((CHUNK, 128), lambda c: (rev(c), DT_OFF // 128)),
                   pl.BlockSpec((CHUNK, XBC_COLS), lambda c: (rev(c), 0)),
                   vec(SSM_INNER), vec(SSM_HEADS), vec(SSM_HEADS), vec(SSM_HEADS)],
        out_shape=[jax.ShapeDtypeStruct(dmain.shape, dmain.dtype), jax.ShapeDtypeStruct(dtail.shape, dtail.dtype),
                   jax.ShapeDtypeStruct((T, XBC_COLS), F32), jax.ShapeDtypeStruct((1, SSM_INNER), F32),
                   jax.ShapeDtypeStruct((1, SSM_HEADS), F32), jax.ShapeDtypeStruct((1, SSM_HEADS), F32),
                   jax.ShapeDtypeStruct((1, SSM_HEADS), F32)],
        scratch_shapes=[pltpu.VMEM((SSM_STATE, SSM_INNER), F32), pltpu.VMEM((SSM_HEADS, CHUNK), F32)],
        input_output_aliases={10: 0, 11: 1},
        compiler_params=_cp(("arbitrary",)))(dyo, ypre, xbc, hs, proj_tail, proj_main, dtb, alog, d_exp, norm_w,
                                             dmain, dtail)
    return outs


PAIR = 2 * SSMD
PAIRS_PER_GROUP = SSM_GROUP_COLS // PAIR


def _dot3(x, onehot):
    h1 = x.astype(BF16)
    r = x - h1.astype(F32)
    h2 = r.astype(BF16)
    h3 = (r - h2.astype(F32)).astype(BF16)
    return _dot(h1, onehot) + _dot(h2, onehot) + _dot(h3, onehot)


def _chunk_rows(dt_raw, dtb_col, alog_col):
    row = lax.broadcasted_iota(jnp.int32, (CHUNK, CHUNK), 0)
    col = lax.broadcasted_iota(jnp.int32, (CHUNK, CHUNK), 1)
    dt_rawT = dt_raw.T
    dtT = _softplus(dt_rawT + dtb_col)
    a_col = -jnp.exp(alog_col)
    acsT = _dot3(dtT * a_col, (row <= col).astype(BF16))
    return dt_rawT, dtT, a_col, acsT, row, col


def _block_diag(x, left):
    return jnp.concatenate([jnp.where(left, x, 0.0), jnp.where(left, 0.0, x)], axis=0).astype(BF16)


def _lane_bcast(v, h):
    return jnp.broadcast_to(v[:, h:h + 1], (CHUNK, CHUNK))


def _ssd_fwd(xbc, proj_main, proj_tail, dtb_col, alog_col, d_exp, norm_w):
    T = xbc.shape[0]
    nc = T // CHUNK

    def body(xbc_ref, dt_ref, z_ref, dtb_ref, alog_ref, d_ref, nw_ref, y_ref, ypre_ref, hs_ref, H):
        c = pl.program_id(0)

        @pl.when(c == 0)
        def _():
            H[...] = jnp.zeros_like(H)

        hs_ref[0] = H[...]
        _, dtT, _, acsT, row, col = _chunk_rows(dt_ref[:, 0:SSM_HEADS], dtb_ref[...], alog_ref[...])
        tril, left = row >= col, col < SSMD
        acs = acsT.T
        w = (dtT * jnp.exp(acsT[:, CHUNK - 1:CHUNK] - acsT)).T
        cd = jnp.exp(acs[CHUNK - 1:CHUNK, :])
        for g in range(SSM_GROUPS):
            gs = slice(g * SSM_GROUP_COLS, (g + 1) * SSM_GROUP_COLS)
            Bb = xbc_ref[:, B_OFF + g * SSM_STATE:B_OFF + (g + 1) * SSM_STATE].astype(BF16)
            Cb = xbc_ref[:, C_OFF + g * SSM_STATE:C_OFF + (g + 1) * SSM_STATE].astype(BF16)
            Hg = H[:, gs]
            CH = _dot(Cb, Hg.astype(BF16))
            CB = _dot(Cb, Bb, NT)
            ys, xws = [], []
            for kk in range(PAIRS_PER_GROUP):
                k = g * PAIRS_PER_GROUP + kk
                xs_p = xbc_ref[:, k * PAIR:(k + 1) * PAIR]
                mps, ecols, wcols = [], [], []
                for j in range(2):
                    h = 2 * k + j
                    colb = _lane_bcast(acs, h)
                    L = jnp.exp(jnp.where(tril, colb - acsT[h:h + 1, :], -jnp.inf))
                    mps.append((CB * L * dtT[h:h + 1, :]).astype(BF16))
                    ecols.append(jnp.exp(colb))
                    wcols.append(_lane_bcast(w, h))
                yd = _dot(jnp.concatenate(mps, axis=1), _block_diag(xs_p, left))
                ys.append(yd + CH[:, kk * PAIR:(kk + 1) * PAIR] * jnp.where(left, ecols[0], ecols[1]))
                xws.append((xs_p * jnp.where(left, wcols[0], wcols[1])).astype(BF16))
            cd_e = jnp.concatenate([jnp.broadcast_to(cd[:, g * 8 + e:g * 8 + e + 1], (1, SSMD)) for e in range(8)], axis=1)
            H[:, gs] = Hg * cd_e + _dot(Bb, jnp.concatenate(xws, axis=1), TN)
            ypre = jnp.concatenate(ys, axis=1) + xbc_ref[:, gs] * d_ref[:, gs]
            ypre_ref[:, gs] = ypre
            z = z_ref[:, gs]
            yg = ypre * (z * _sigmoid(z))
            r = lax.rsqrt(jnp.mean(yg * yg, axis=1, keepdims=True) + RMS_EPS)
            y_ref[:, gs] = (yg * r * nw_ref[:, gs]).astype(BF16)

    vec = lambda n: _const_spec((1, n))
    colv = _const_spec((SSM_HEADS, 1))
    return pl.pallas_call(
        body, name="ssd_fwd", grid=(nc,),
        in_specs=[pl.BlockSpec((CHUNK, XBC_COLS), lambda c: (c, 0)),
                  pl.BlockSpec((CHUNK, 128), lambda c: (c, DT_OFF // 128)),
                  pl.BlockSpec((CHUNK, SSM_INNER), lambda c: (c, Z_OFF // SSM_INNER)),
                  colv, colv, vec(SSM_INNER), vec(SSM_INNER)],
        out_specs=[pl.BlockSpec((CHUNK, SSM_INNER), lambda c: (c, 0)),
                   pl.BlockSpec((CHUNK, SSM_INNER), lambda c: (c, 0)),
                   pl.BlockSpec((1, SSM_STATE, SSM_INNER), lambda c: (c, 0, 0))],
        out_shape=[jax.ShapeDtypeStruct((T, SSM_INNER), BF16), jax.ShapeDtypeStruct((T, SSM_INNER), F32),
                   jax.ShapeDtypeStruct((nc, SSM_STATE, SSM_INNER), F32)],
        scratch_shapes=[pltpu.VMEM((SSM_STATE, SSM_INNER), F32)],
        compiler_params=_cp(("arbitrary",)))(xbc, proj_tail, proj_main, dtb_col, alog_col, d_exp, norm_w)


def _ssd_bwd(dyo, ypre, xbc, hs, proj_main, proj_tail, dtb_col, alog_col, d_exp, norm_w, ehead_t, dmain, dtail):
    T = xbc.shape[0]
    nc = T // CHUNK

    def body(dyo_ref, ypre_ref, xbc_ref, hs_ref, dt_ref, z_ref, dtb_ref, alog_ref, d_ref, nw_ref, eh_ref,
             dmain_in, dtail_in, dz_ref, ddt_ref, dxbc_ref, dnw_ref, dd_ref, dalog_ref, ddtb_ref, G):
        del dmain_in, dtail_in
        c = pl.program_id(0)

        @pl.when(c == 0)
        def _():
            G[...] = jnp.zeros_like(G)
            dnw_ref[...] = jnp.zeros_like(dnw_ref)
            dd_ref[...] = jnp.zeros_like(dd_ref)
            dalog_ref[...] = jnp.zeros_like(dalog_ref)
            ddtb_ref[...] = jnp.zeros_like(ddtb_ref)

        dt_rawT, dtT, a_col, acsT, row, col = _chunk_rows(dt_ref[:, 0:SSM_HEADS], dtb_ref[...], alog_ref[...])
        tril, triu, left = row >= col, col >= row, col < SSMD
        acs = acsT.T
        dt = dtT.T
        lastT = acsT[:, CHUNK - 1:CHUNK]
        dstT = jnp.exp(lastT - acsT)
        wT = dtT * dstT
        cd = jnp.exp(acs[CHUNK - 1:CHUNK, :])
        ddt_rows, rs_rows, deo_rows, dw_rows = [], [], [], []
        dd_cols, gh_cols, dnw_cols = [], [], []
        for g in range(SSM_GROUPS):
            gs = slice(g * SSM_GROUP_COLS, (g + 1) * SSM_GROUP_COLS)
            z = z_ref[:, gs]
            sz = _sigmoid(z)
            silu_z = z * sz
            ypre = ypre_ref[:, gs]
            yg = ypre * silu_z
            r = lax.rsqrt(jnp.mean(yg * yg, axis=1, keepdims=True) + RMS_EPS)
            ygn = yg * r
            dyo = dyo_ref[:, gs]
            dyn = dyo * nw_ref[:, gs]
            dnw_cols.append(jnp.sum(dyo * ygn, axis=0, keepdims=True))
            dyg = r * (dyn - ygn * jnp.mean(dyn * ygn, axis=1, keepdims=True))
            dz_ref[:, gs] = (dyg * ypre * (sz * (1.0 + z * (1.0 - sz)))).astype(dz_ref.dtype)
            dY = dyg * silu_z
            xs = xbc_ref[:, gs]
            dd_cols.append(jnp.sum(dY * xs, axis=0, keepdims=True))
            Bf = xbc_ref[:, B_OFF + g * SSM_STATE:B_OFF + (g + 1) * SSM_STATE]
            Cf = xbc_ref[:, C_OFF + g * SSM_STATE:C_OFF + (g + 1) * SSM_STATE]
            Bb, Cb = Bf.astype(BF16), Cf.astype(BF16)
            BT, CT = Bf.T, Cf.T
            CB = _dot(Cb, Bb, NT)
            CBT = _dot(Bb, Cb, NT)
            Hg = hs_ref[0, :, gs]
            Gg = G[:, gs]
            gh_cols.append(jnp.sum(Gg * Hg, axis=0, keepdims=True))
            dCB = jnp.zeros((CHUNK, CHUNK), F32)
            dxs_d, dyes, xws, wsels = [], [], [], []
            for kk in range(PAIRS_PER_GROUP):
                k = g * PAIRS_PER_GROUP + kk
                ps = slice(kk * PAIR, (kk + 1) * PAIR)
                xs_p, dY_p = xs[:, ps], dY[:, ps]
                Ls, LTs, dtcols, ecols, wcols = [], [], [], [], []
                for j in range(2):
                    h = 2 * k + j
                    colb = _lane_bcast(acs, h)
                    seg = colb - acsT[h:h + 1, :]
                    Ls.append(jnp.exp(jnp.where(tril, seg, -jnp.inf)))
                    LTs.append(jnp.exp(jnp.where(triu, -seg, -jnp.inf)))
                    dtcol = _lane_bcast(dt, h)
                    dtcols.append(dtcol)
                    ecols.append(jnp.exp(colb))
                    wcols.append(dtcol * jnp.exp(acs[CHUNK - 1:CHUNK, h:h + 1] - colb))
                wsel = jnp.where(left, wcols[0], wcols[1])
                dYe_p = dY_p * jnp.where(left, ecols[0], ecols[1])
                bdx = _block_diag(xs_p, left)
                bddy = _block_diag(dY_p, left)
                dMx2 = _dot(dY_p.astype(BF16), bdx, NT)
                dMxT2 = _dot(xs_p.astype(BF16), bddy, NT)
                Q1 = _dot(Hg[:, ps].astype(BF16), _block_diag(dYe_p, left), NT)
                Q2 = _dot(Gg[:, ps].astype(BF16), bdx, NT)
                mts = []
                for j in range(2):
                    h = 2 * k + j
                    js = slice(j * CHUNK, (j + 1) * CHUNK)
                    dMx = dMx2[:, js]
                    A = CB * Ls[j]
                    AT = CBT * LTs[j]
                    ddt_rows.append(jnp.sum(A * dMx, axis=0, keepdims=True))
                    ATd = AT * dtcols[j]
                    rs_rows.append(jnp.sum(ATd * dMxT2[:, js], axis=0, keepdims=True))
                    dCB = dCB + dMx * Ls[j] * dtT[h:h + 1, :]
                    mts.append(ATd.astype(BF16))
                    deo_rows.append(jnp.sum(CT * Q1[:, js], axis=0, keepdims=True))
                    dw_rows.append(jnp.sum(BT * Q2[:, js], axis=0, keepdims=True))
                dxs_d.append(_dot(jnp.concatenate(mts, axis=1), bddy))
                dyes.append(dYe_p.astype(BF16))
                xws.append((xs_p * wsel).astype(BF16))
                wsels.append(wsel)
            dYe_g = jnp.concatenate(dyes, axis=1)
            xw_g = jnp.concatenate(xws, axis=1)
            Hgb, Ggb, dCBb = Hg.astype(BF16), Gg.astype(BF16), dCB.astype(BF16)
            dxbc_ref[:, C_OFF + g * SSM_STATE:C_OFF + (g + 1) * SSM_STATE] = _dot(dYe_g, Hgb, NT) + _dot(dCBb, Bb)
            dxbc_ref[:, B_OFF + g * SSM_STATE:B_OFF + (g + 1) * SSM_STATE] = _dot(xw_g, Ggb, NT) + _dot(dCBb, Cb, TN)
            BG = _dot(Bb, Ggb)
            dxbc_ref[:, gs] = (jnp.concatenate(dxs_d, axis=1) + BG * jnp.concatenate(wsels, axis=1)
                               + dY * d_ref[:, gs])
            cd_e = jnp.concatenate([jnp.broadcast_to(cd[:, g * 8 + e:g * 8 + e + 1], (1, SSMD)) for e in range(8)], axis=1)
            G[:, gs] = Gg * cd_e + _dot(Cb, dYe_g, TN)
        dnw_ref[...] += jnp.concatenate(dnw_cols, axis=1)
        eh = eh_ref[...]
        dd_ref[...] += jnp.sum(eh * jnp.concatenate(dd_cols, axis=1), axis=1, keepdims=True)
        dcd = jnp.sum(eh * jnp.concatenate(gh_cols, axis=1), axis=1, keepdims=True)
        DDT = jnp.concatenate(ddt_rows, axis=0)
        DW = jnp.concatenate(dw_rows, axis=0)
        DWw = DW * wT
        dacsT = jnp.concatenate(rs_rows, axis=0) - DDT * dtT + jnp.concatenate(deo_rows, axis=0) - DWw
        end = jnp.sum(DWw, axis=1, keepdims=True) + dcd * jnp.exp(lastT)
        lane = lax.broadcasted_iota(jnp.int32, (SSM_HEADS, CHUNK), 1)
        dacsT = dacsT + jnp.where(lane == CHUNK - 1, end, 0.0)
        dadtT = _dot3(dacsT, tril.astype(BF16))
        ddtT = dadtT * a_col + DDT + DW * dstT
        dalog_ref[...] += jnp.sum(dadtT * dtT, axis=1, keepdims=True) * a_col
        ddt_rawT = ddtT * _sigmoid(dt_rawT + dtb_ref[...])
        ddtb_ref[...] += jnp.sum(ddt_rawT, axis=1, keepdims=True)
        ddt_ref[...] = jnp.concatenate([ddt_rawT.T, jnp.zeros((CHUNK, 128 - SSM_HEADS), F32)], axis=1).astype(ddt_ref.dtype)

    rev = lambda c: nc - 1 - c
    vec = lambda n: _const_spec((1, n))
    colv = _const_spec((SSM_HEADS, 1))
    any_spec = pl.BlockSpec(memory_space=pl.ANY)
    return pl.pallas_call(
        body, name="ssd_bwd", grid=(nc,),
        in_specs=[pl.BlockSpec((CHUNK, SSM_INNER), lambda c: (rev(c), 0)),
                  pl.BlockSpec((CHUNK, SSM_INNER), lambda c: (rev(c), 0)),
                  pl.BlockSpec((CHUNK, XBC_COLS), lambda c: (rev(c), 0)),
                  pl.BlockSpec((1, SSM_STATE, SSM_INNER), lambda c: (rev(c), 0, 0)),
                  pl.BlockSpec((CHUNK, 128), lambda c: (rev(c), DT_OFF // 128)),
                  pl.BlockSpec((CHUNK, SSM_INNER), lambda c: (rev(c), Z_OFF // SSM_INNER)),
                  colv, colv, vec(SSM_INNER), vec(SSM_INNER), _const_spec((SSM_HEADS, SSM_INNER)), any_spec, any_spec],
        out_specs=[pl.BlockSpec((CHUNK, SSM_INNER), lambda c: (rev(c), Z_OFF // SSM_INNER)),
                   pl.BlockSpec((CHUNK, 128), lambda c: (rev(c), DT_OFF // 128)),
                   pl.BlockSpec((CHUNK, XBC_COLS), lambda c: (rev(c), 0)),
                   vec(SSM_INNER), colv, colv, colv],
        out_shape=[jax.ShapeDtypeStruct(dmain.shape, dmain.dtype), jax.ShapeDtypeStruct(dtail.shape, dtail.dtype),
                   jax.ShapeDtypeStruct((T, XBC_COLS), F32), jax.ShapeDtypeStruct((1, SSM_INNER), F32),
                   jax.ShapeDtypeStruct((SSM_HEADS, 1), F32), jax.ShapeDtypeStruct((SSM_HEADS, 1), F32),
                   jax.ShapeDtypeStruct((SSM_HEADS, 1), F32)],
        scratch_shapes=[pltpu.VMEM((SSM_STATE, SSM_INNER), F32)],
        input_output_aliases={11: 0, 12: 1},
        compiler_params=_cp(("arbitrary",)))(dyo, ypre, xbc, hs, proj_tail, proj_main, dtb_col, alog_col, d_exp, norm_w,
                                             ehead_t, dmain, dtail)


def _rel_bucket(rel):
    n = jnp.maximum(rel, 0)
    max_exact = REL_BUCKETS // 2
    nf = jnp.maximum(n, 1).astype(F32)
    large = max_exact + (jnp.log(nf / max_exact) / math.log(REL_MAX_DIST / max_exact)
                         * (REL_BUCKETS - max_exact)).astype(jnp.int32)
    large = jnp.minimum(large, REL_BUCKETS - 1)
    return jnp.where(n < max_exact, n, large)


def _band_geometry():
    qi = jnp.arange(WINDOW)[:, None] + WINDOW
    kj = jnp.arange(2 * WINDOW)[None, :]
    rel = qi - kj
    return _rel_bucket(rel), (rel >= 0) & (rel < WINDOW)


def _attn_logits(kband, qh, bias_h, first):
    s = _dot(kband, qh, NT) * (HEADDIM ** -0.5) + bias_h
    rowk = lax.broadcasted_iota(jnp.int32, (2 * WINDOW, WINDOW), 0)
    return jnp.where(jnp.logical_and(first, rowk < WINDOW), NEG, s)


def _attn_fwd(proj_main, proj_tail, bias_tbl, sinks):
    T = proj_main.shape[0]
    nb = T // WINDOW

    def body(q_ref, kv_ref, kvp_ref, bias_ref, sink_ref, o_ref, lse_ref):
        i = pl.program_id(0)
        first = i == 0
        outs, lses = [], []
        for kvh in range(ATTN_KV):
            ks = slice(K_OFF + kvh * HEADDIM, K_OFF + (kvh + 1) * HEADDIM)
            vs = slice(V_OFF + kvh * HEADDIM, V_OFF + (kvh + 1) * HEADDIM)
            kband = jnp.concatenate([kvp_ref[:, ks], kv_ref[:, ks]], axis=0).astype(BF16)
            vband = jnp.concatenate([kvp_ref[:, vs], kv_ref[:, vs]], axis=0).astype(BF16)
            heads = range(kvh * ATTN_GROUP, (kvh + 1) * ATTN_GROUP)
            logits = [_attn_logits(kband, q_ref[:, h * HEADDIM:(h + 1) * HEADDIM].astype(BF16), bias_ref[h], first)
                      for h in heads]
            probs = []
            for h, s in zip(heads, logits):
                sink = sink_ref[:, h:h + 1]
                m = jnp.maximum(jnp.max(s, axis=0, keepdims=True), sink)
                p = jnp.exp(s - m)
                den = jnp.sum(p, axis=0, keepdims=True) + jnp.exp(sink - m)
                probs.append((p * (1.0 / den)).astype(BF16))
                lses.append(m + jnp.log(den))
            outs += [_dot(pt, vband, TN) for pt in probs]
        o_ref[...] = jnp.concatenate(outs, axis=1).astype(BF16)
        lse_ref[...] = jnp.concatenate(lses, axis=0)

    return pl.pallas_call(
        body, name="attn_fwd", grid=(nb,),
        in_specs=[pl.BlockSpec((WINDOW, D_MODEL), lambda i: (i, Q_OFF // D_MODEL)),
                  pl.BlockSpec((WINDOW, 256), lambda i: (i, 0)),
                  pl.BlockSpec((WINDOW, 256), lambda i: (jnp.maximum(i - 1, 0), 0)),
                  _const_spec((ATTN_HEADS, 2 * WINDOW, WINDOW)), _const_spec((1, ATTN_HEADS))],
        out_specs=[pl.BlockSpec((WINDOW, D_MODEL), lambda i: (i, 0)),
                   pl.BlockSpec((ATTN_HEADS, WINDOW), lambda i: (0, i))],
        out_shape=[jax.ShapeDtypeStruct((T, D_MODEL), BF16), jax.ShapeDtypeStruct((ATTN_HEADS, T), F32)],
        compiler_params=_cp(("arbitrary",)))(proj_main, proj_tail, proj_tail, bias_tbl, sinks)


def _attn_bwd(dy, lse, proj_main, proj_tail, bias_tbl, sinks, dmain):
    T = proj_main.shape[0]
    nb = T // WINDOW

    def body(dy_ref, lse_ref, q_ref, kv_ref, kvp_ref, bias_ref, sink_ref, dmain_in,
             dq_ref, dkv_ref, dbias_ref, dsink_ref, carry):
        del dmain_in
        i = pl.program_id(0)
        first = i == 0

        @pl.when(first)
        def _():
            carry[...] = jnp.zeros_like(carry)
            dbias_ref[...] = jnp.zeros_like(dbias_ref)
            dsink_ref[...] = jnp.zeros_like(dsink_ref)

        @pl.when(i < nb)
        def _():
            scale = HEADDIM ** -0.5
            dqs, dsinks, dks, dvs = [], [], [], []
            for kvh in range(ATTN_KV):
                ks = slice(K_OFF + kvh * HEADDIM, K_OFF + (kvh + 1) * HEADDIM)
                vs = slice(V_OFF + kvh * HEADDIM, V_OFF + (kvh + 1) * HEADDIM)
                kband = jnp.concatenate([kvp_ref[:, ks], kv_ref[:, ks]], axis=0).astype(BF16)
                vband = jnp.concatenate([kvp_ref[:, vs], kv_ref[:, vs]], axis=0).astype(BF16)
                heads = range(kvh * ATTN_GROUP, (kvh + 1) * ATTN_GROUP)
                qs = [q_ref[:, h * HEADDIM:(h + 1) * HEADDIM].astype(BF16) for h in heads]
                dos = [dy_ref[:, h * HEADDIM:(h + 1) * HEADDIM] for h in heads]
                logits = [_attn_logits(kband, qh, bias_ref[h], first) for h, qh in zip(heads, qs)]
                dps = [_dot(vband, do, NT) for do in dos]
                pbs, dsbs = [], []
                for h, s, dp in zip(heads, logits, dps):
                    lse_h = lse_ref[h:h + 1, :]
                    p = jnp.exp(s - lse_h)
                    delta = jnp.sum(p * dp, axis=0, keepdims=True)
                    ds = p * (dp - delta)
                    psink = jnp.exp(sink_ref[:, h:h + 1] - lse_h)
                    dsinks.append(-jnp.sum(psink * delta, axis=1, keepdims=True))
                    dbias_ref[h] += ds
                    pbs.append(p.astype(BF16))
                    dsbs.append((ds * scale).astype(BF16))
                dqs += [_dot(dsb, kband, TN) for dsb in dsbs]
                dks.append(_dot(jnp.concatenate(dsbs, axis=1), jnp.concatenate(qs, axis=0)))
                dvs.append(_dot(jnp.concatenate(pbs, axis=1), jnp.concatenate(dos, axis=0)))
            dq_ref[...] = jnp.concatenate(dqs, axis=1).astype(dq_ref.dtype)
            dsink_ref[...] += jnp.concatenate(dsinks, axis=1)
            dkv = jnp.concatenate(dks + dvs, axis=1)
            dkv_ref[...] = (carry[...] + dkv[0:WINDOW, :]).astype(dkv_ref.dtype)
            carry[...] = dkv[WINDOW:, :]

        @pl.when(i == nb)
        def _():
            dkv_ref[...] = carry[...].astype(dkv_ref.dtype)

    cur = lambda i: jnp.minimum(i, nb - 1)
    return pl.pallas_call(
        body, name="attn_bwd", grid=(nb + 1,),
        in_specs=[pl.BlockSpec((WINDOW, D_MODEL), lambda i: (cur(i), 0)),
                  pl.BlockSpec((ATTN_HEADS, WINDOW), lambda i: (0, cur(i))),
                  pl.BlockSpec((WINDOW, D_MODEL), lambda i: (cur(i), Q_OFF // D_MODEL)),
                  pl.BlockSpec((WINDOW, 256), lambda i: (cur(i), 0)),
                  pl.BlockSpec((WINDOW, 256), lambda i: (jnp.maximum(cur(i) - 1, 0), 0)),
                  _const_spec((ATTN_HEADS, 2 * WINDOW, WINDOW)), _const_spec((1, ATTN_HEADS)),
                  pl.BlockSpec(memory_space=pl.ANY)],
        out_specs=[pl.BlockSpec((WINDOW, D_MODEL), lambda i: (cur(i), Q_OFF // D_MODEL)),
                   pl.BlockSpec((WINDOW, 256), lambda i: (jnp.maximum(i - 1, 0), 0)),
                   _const_spec((ATTN_HEADS, 2 * WINDOW, WINDOW)), _const_spec((1, ATTN_HEADS))],
        out_shape=[jax.ShapeDtypeStruct(dmain.shape, dmain.dtype), jax.ShapeDtypeStruct((T, TAIL_COLS), BF16),
                   jax.ShapeDtypeStruct((ATTN_HEADS, 2 * WINDOW, WINDOW), F32),
                   jax.ShapeDtypeStruct((1, ATTN_HEADS), F32)],
        scratch_shapes=[pltpu.VMEM((WINDOW, 256), F32)],
        input_output_aliases={7: 0},
        compiler_params=_cp(("arbitrary",)))(dy, lse, proj_main, proj_tail, proj_tail, bias_tbl, sinks, dmain)


def _rel_bias_grad(dbias, onehot):
    def body(d_ref, oh_ref, o_ref):
        o_ref[...] = _dot(d_ref[...], oh_ref[...], NN, HIGHEST)

    return pl.pallas_call(body, name="rel_bias_grad",
                          out_shape=jax.ShapeDtypeStruct((ATTN_HEADS, REL_BUCKETS), F32))(dbias, onehot)


def _ln_fwd(r, g, b):
    mu = jnp.mean(r, axis=1, keepdims=True)
    xc = r - mu
    rstd = lax.rsqrt(jnp.mean(xc * xc, axis=1, keepdims=True) + LN_EPS)
    xhat = xc * rstd
    return xhat * g + b, xhat, rstd


def _ln_bwd(dy, xhat, rstd, g):
    dxh = dy * g
    return rstd * (dxh - jnp.mean(dxh, axis=1, keepdims=True) - xhat * jnp.mean(dxh * xhat, axis=1, keepdims=True))


def _merge_fwd(y_ssm, y_attn, proj_main, b_gate, w_bs, w_ba, tm=512):
    T = y_ssm.shape[0]

    def body(ys_ref, ya_ref, gs_ref, ga_ref, bg_ref, wbs_ref, wba_ref, m_ref, bs_ref, ba_ref):
        bs = _dot(ys_ref[...], wbs_ref[...])
        ba = _dot(ya_ref[...], wba_ref[...])
        g_s = _sigmoid(gs_ref[...] + bg_ref[:, 0:D_MODEL])
        g_a = _sigmoid(ga_ref[...] + bg_ref[:, D_MODEL:])
        m_ref[...] = (g_s * bs + g_a * ba).astype(BF16)
        bs_ref[...] = bs
        ba_ref[...] = ba

    row = lambda w, off=0: pl.BlockSpec((tm, w), lambda i: (i, off))
    return pl.pallas_call(
        body, name="merge_fwd", grid=(T // tm,),
        in_specs=[row(SSM_INNER), row(D_MODEL), row(D_MODEL, GATE_OFF // D_MODEL), row(D_MODEL, GATE_OFF // D_MODEL + 1),
                  _const_spec((1, 2 * D_MODEL)), _const_spec((SSM_INNER, D_MODEL)), _const_spec((D_MODEL, D_MODEL))],
        out_specs=[row(D_MODEL), row(D_MODEL), row(D_MODEL)],
        out_shape=[jax.ShapeDtypeStruct((T, D_MODEL), BF16), jax.ShapeDtypeStruct((T, D_MODEL), F32),
                   jax.ShapeDtypeStruct((T, D_MODEL), F32)],
        compiler_params=_cp(("parallel",)))(y_ssm, y_attn, proj_main, proj_main, b_gate, w_bs, w_ba)


def _mix_ln1(merged, w_mo, x, g1, b1, tm=512):
    T = x.shape[0]

    def body(m_ref, w_ref, x_ref, g_ref, b_ref, r_ref, h_ref):
        r = ALPHA * x_ref[...] + _dot(m_ref[...], w_ref[...])
        r_ref[...] = r
        h_ref[...] = _ln_fwd(r, g_ref[...], b_ref[...])[0]

    row = pl.BlockSpec((tm, D_MODEL), lambda i: (i, 0))
    return pl.pallas_call(
        body, name="mix_ln1", grid=(T // tm,),
        in_specs=[row, _const_spec((D_MODEL, D_MODEL)), row, _const_spec((1, D_MODEL)), _const_spec((1, D_MODEL))],
        out_specs=[row, row],
        out_shape=[jax.ShapeDtypeStruct((T, D_MODEL), F32), jax.ShapeDtypeStruct((T, D_MODEL), F32)],
        compiler_params=_cp(("parallel",)))(merged, w_mo, x, g1, b1)


def _glu(u, tr=512, tc=1408):
    T = u.shape[0]
    nj = D_FF // tc

    def body(g_ref, v_ref, o_ref):
        g = g_ref[...]
        o_ref[...] = (g * _sigmoid(g) * v_ref[...]).astype(BF16)

    return pl.pallas_call(
        body, name="glu", grid=(T // tr, nj),
        in_specs=[pl.BlockSpec((tr, tc), lambda i, j: (i, j)), pl.BlockSpec((tr, tc), lambda i, j: (i, nj + j))],
        out_specs=pl.BlockSpec((tr, tc), lambda i, j: (i, j)),
        out_shape=jax.ShapeDtypeStruct((T, D_FF), BF16),
        compiler_params=_cp(("parallel", "parallel")))(u, u)


def _down_ln2_loss(act, w_down, h1, target, g2, b2, tm=512):
    T = h1.shape[0]

    def body(a_ref, w_ref, h_ref, t_ref, g_ref, b_ref, dr_ref, dg_ref, db_ref, l_ref):
        @pl.when(pl.program_id(0) == 0)
        def _():
            dg_ref[...] = jnp.zeros_like(dg_ref)
            db_ref[...] = jnp.zeros_like(db_ref)
            l_ref[...] = jnp.zeros_like(l_ref)

        r = ALPHA * h_ref[...] + _dot(a_ref[...], w_ref[...])
        y, xhat, rstd = _ln_fwd(r, g_ref[...], b_ref[...])
        err = y - t_ref[...]
        l_ref[...] += jnp.sum(err * err, keepdims=True)
        dy = err * (1.0 / D_MODEL)
        dg_ref[...] += jnp.sum(dy * xhat, axis=0, keepdims=True)
        db_ref[...] += jnp.sum(dy, axis=0, keepdims=True)
        dr_ref[...] = _ln_bwd(dy, xhat, rstd, g_ref[...])

    row = pl.BlockSpec((tm, D_MODEL), lambda i: (i, 0))
    vec = _const_spec((1, D_MODEL))
    return pl.pallas_call(
        body, name="down_ln2_loss", grid=(T // tm,),
        in_specs=[pl.BlockSpec((tm, D_FF), lambda i: (i, 0)), _const_spec((D_FF, D_MODEL)), row, row, vec, vec],
        out_specs=[row, vec, vec, _const_spec((1, 1))],
        out_shape=[jax.ShapeDtypeStruct((T, D_MODEL), F32), jax.ShapeDtypeStruct((1, D_MODEL), F32),
                   jax.ShapeDtypeStruct((1, D_MODEL), F32), jax.ShapeDtypeStruct((1, 1), F32)],
        compiler_params=_cp(("arbitrary",)))(act, w_down, h1, target, g2, b2)


def _ffn_bwd_act(dr2, w_down, u, tm=512, tn=1408):
    T = dr2.shape[0]
    nj = D_FF // tn

    def body(d_ref, w_ref, g_ref, v_ref, o_ref, dact):
        half = pl.program_id(2)

        @pl.when(half == 0)
        def _():
            dact[...] = _dot(d_ref[...].astype(BF16), w_ref[...], NT)
            g = g_ref[...]
            sg = _sigmoid(g)
            o_ref[...] = dact[...] * v_ref[...] * (sg * (1.0 + g * (1.0 - sg)))

        @pl.when(half == 1)
        def _():
            g = g_ref[...]
            o_ref[...] = dact[...] * (g * _sigmoid(g))

    return pl.pallas_call(
        body, name="ffn_bwd_act", grid=(T // tm, nj, 2),
        in_specs=[pl.BlockSpec((tm, D_MODEL), lambda i, j, h: (i, 0)),
                  pl.BlockSpec((tn, D_MODEL), lambda i, j, h: (j, 0)),
                  pl.BlockSpec((tm, tn), lambda i, j, h: (i, j)),
                  pl.BlockSpec((tm, tn), lambda i, j, h: (i, nj + j))],
        out_specs=pl.BlockSpec((tm, tn), lambda i, j, h: (i, h * nj + j)),
        out_shape=jax.ShapeDtypeStruct((T, 2 * D_FF), F32),
        scratch_shapes=[pltpu.VMEM((tm, tn), F32)],
        compiler_params=_cp(("parallel", "arbitrary", "arbitrary")))(dr2, w_down, u, u)


def _ffn_bwd_in(du_pre, w_up, dr2, r1, g1, b1, tm=512, tk=1408):
    T = dr2.shape[0]
    nk = 2 * D_FF // tk

    def body(d_ref, w_ref, dr2_ref, r_ref, g_ref, b_ref, dr1_ref, dg_ref, db_ref, acc):
        i, k = pl.program_id(0), pl.program_id(1)

        @pl.when(jnp.logical_and(i == 0, k == 0))
        def _():
            dg_ref[...] = jnp.zeros_like(dg_ref)
            db_ref[...] = jnp.zeros_like(db_ref)

        @pl.when(k == 0)
        def _():
            acc[...] = ALPHA * dr2_ref[...]

        acc[...] += _dot(d_ref[...], w_ref[...], NT)

        @pl.when(k == nk - 1)
        def _():
            _, xhat, rstd = _ln_fwd(r_ref[...], g_ref[...], b_ref[...])
            dy = acc[...]
            dg_ref[...] += jnp.sum(dy * xhat, axis=0, keepdims=True)
            db_ref[...] += jnp.sum(dy, axis=0, keepdims=True)
            dr1_ref[...] = _ln_bwd(dy, xhat, rstd, g_ref[...])

    row = pl.BlockSpec((tm, D_MODEL), lambda i, k: (i, 0))
    vec = _const_spec((1, D_MODEL))
    return pl.pallas_call(
        body, name="ffn_bwd_in", grid=(T // tm, nk),
        in_specs=[pl.BlockSpec((tm, tk), lambda i, k: (i, k)), pl.BlockSpec((D_MODEL, tk), lambda i, k: (0, k)),
                  row, row, vec, vec],
        out_specs=[row, vec, vec],
        out_shape=[jax.ShapeDtypeStruct((T, D_MODEL), F32), jax.ShapeDtypeStruct((1, D_MODEL), F32),
                   jax.ShapeDtypeStruct((1, D_MODEL), F32)],
        scratch_shapes=[pltpu.VMEM((tm, D_MODEL), F32)],
        compiler_params=_cp(("arbitrary", "arbitrary")))(du_pre, w_up, dr2, r1, g1, b1)


def _mix_bwd(dr1, w_mo, w_bs, w_ba, bs, ba, proj_main, b_gate, tm=512):
    T = dr1.shape[0]

    def body(d_ref, wmo_ref, wbs_ref, wba_ref, bs_ref, ba_ref, gs_ref, ga_ref, bg_ref,
             dg_ref, dbs_ref, dba_ref, dys_ref, dya_ref, dbg_ref):
        @pl.when(pl.program_id(0) == 0)
        def _():
            dbg_ref[...] = jnp.zeros_like(dbg_ref)

        dm = _dot(d_ref[...].astype(BF16), wmo_ref[...], NT)
        g_s = _sigmoid(gs_ref[...] + bg_ref[:, 0:D_MODEL])
        g_a = _sigmoid(ga_ref[...] + bg_ref[:, D_MODEL:])
        dgs = dm * bs_ref[...] * g_s * (1.0 - g_s)
        dga = dm * ba_ref[...] * g_a * (1.0 - g_a)
        dg_ref[:, 0:D_MODEL] = dgs.astype(BF16)
        dg_ref[:, D_MODEL:] = dga.astype(BF16)
        dbg_ref[:, 0:D_MODEL] += jnp.sum(dgs, axis=0, keepdims=True)
        dbg_ref[:, D_MODEL:] += jnp.sum(dga, axis=0, keepdims=True)
        dbs = (dm * g_s).astype(BF16)
        dba = (dm * g_a).astype(BF16)
        dbs_ref[...] = dbs
        dba_ref[...] = dba
        dys_ref[...] = _dot(dbs, wbs_ref[...], NT)
        dya_ref[...] = _dot(dba, wba_ref[...], NT).astype(BF16)

    row = lambda w, off=0: pl.BlockSpec((tm, w), lambda i: (i, off))
    return pl.pallas_call(
        body, name="mix_bwd", grid=(T // tm,),
        in_specs=[row(D_MODEL), _const_spec((D_MODEL, D_MODEL)), _const_spec((SSM_INNER, D_MODEL)),
                  _const_spec((D_MODEL, D_MODEL)), row(D_MODEL), row(D_MODEL),
                  row(D_MODEL, GATE_OFF // D_MODEL), row(D_MODEL, GATE_OFF // D_MODEL + 1), _const_spec((1, 2 * D_MODEL))],
        out_specs=[row(2 * D_MODEL, GATE_OFF // (2 * D_MODEL)), row(D_MODEL), row(D_MODEL), row(SSM_INNER), row(D_MODEL),
                   _const_spec((1, 2 * D_MODEL))],
        out_shape=[jax.ShapeDtypeStruct((T, MAIN_COLS), BF16), jax.ShapeDtypeStruct((T, D_MODEL), BF16),
                   jax.ShapeDtypeStruct((T, D_MODEL), BF16), jax.ShapeDtypeStruct((T, SSM_INNER), F32),
                   jax.ShapeDtypeStruct((T, D_MODEL), BF16), jax.ShapeDtypeStruct((1, 2 * D_MODEL), F32)],
        compiler_params=_cp(("arbitrary",)))(dr1, w_mo, w_bs, w_ba, bs, ba, proj_main, proj_main, b_gate)


def _local_step(x, target, w, p):
    proj_main = _matmul(x, w["in_main"], "nn", F32, "in_proj_main")
    proj_tail = _matmul(x, w["in_tail"], "nn", F32, "in_proj_tail", tn=TAIL_COLS)
    xbc = _conv_fwd(proj_main, XBC_OFF, XBC_COLS, p["ssm_conv_w"], p["ssm_conv_b"], SSM_CONV, "silu", "ssm_conv_fwd")
    y_ssm, ypre, hs = _ssd_fwd(xbc, proj_main, proj_tail, p["dtb_col"], p["alog_col"], p["d_exp"], p["ssm_norm_w"])
    y_attn, lse = _attn_fwd(proj_main, proj_tail, p["bias_tbl"], p["attn_sinks"])
    merged, bs, ba = _merge_fwd(y_ssm, y_attn, proj_main, p["b_gate"], w["bs"], w["ba"])
    r1, h1 = _mix_ln1(merged, w["mo"], x, p["ln1_g"], p["ln1_b"])
    u_pre = _matmul(h1, w["up"], "nn", F32, "ffn_up", tn=1408)
    u = _conv_fwd(u_pre, 0, 2 * D_FF, p["ffn_conv_w"], p["ffn_conv_b"], FFN_CONV, None, "ffn_conv_fwd", tc=1408)
    act = _glu(u)
    dr2, dg2, db2, sq = _down_ln2_loss(act, w["down"], h1, target, p["ln2_g"], p["ln2_b"])
    g = {"ln2_g": dg2, "ln2_b": db2}
    g["w_down"] = _matmul(act, dr2, "tn", F32, "dw_down", tm=1408, tn=1024, tk=512)
    du = _ffn_bwd_act(dr2, w["down"], u)
    du_pre, g["ffn_conv_w"], g["ffn_conv_b"] = _conv_bwd(
        du, u_pre, 0, 2 * D_FF, p["ffn_conv_w"], p["ffn_conv_b"], FFN_CONV, None,
        jax.ShapeDtypeStruct((x.shape[0], 2 * D_FF), BF16), 0, "ffn_conv_bwd", tc=1408)
    g["w_up"] = _matmul(h1, du_pre, "tn", F32, "dw_up", tm=1024, tn=1408, tk=512)
    dr1, g["ln1_g"], g["ln1_b"] = _ffn_bwd_in(du_pre, w["up"], dr2, r1, p["ln1_g"], p["ln1_b"])
    g["w_mix_out"] = _matmul(merged, dr1, "tn", F32, "dw_mix_out", tm=1024, tn=1024, tk=512)
    dmain, dbs, dba, dy_ssm, dy_attn, g["b_gate"] = _mix_bwd(dr1, w["mo"], w["bs"], w["ba"], bs, ba, proj_main, p["b_gate"])
    g["w_branch_ssm"] = _matmul(y_ssm, dbs, "tn", F32, "dw_branch_ssm", tm=1024, tn=1024, tk=512)
    g["w_branch_attn"] = _matmul(y_attn, dba, "tn", F32, "dw_branch_attn", tm=1024, tn=1024, tk=512)
    dmain, dtail, dbias, g["attn_sinks"] = _attn_bwd(dy_attn, lse, proj_main, proj_tail, p["bias_tbl"], p["attn_sinks"], dmain)
    g["rel_bias"] = _rel_bias_grad(dbias.reshape(ATTN_HEADS, WINDOW * 2 * WINDOW), p["bucket_onehot"]).T
    dmain, dtail, dxbc, g["ssm_norm_w"], dd, dalog, ddtb = _ssd_bwd(
        dy_ssm, ypre, xbc, hs, proj_main, proj_tail, p["dtb_col"], p["alog_col"], p["d_exp"], p["ssm_norm_w"],
        p["ehead_t"], dmain, dtail)
    g["ssm_d"], g["ssm_a_log"], g["ssm_dt_bias"] = (a.reshape(1, SSM_HEADS) for a in (dd, dalog, ddtb))
    dmain, g["ssm_conv_w"], g["ssm_conv_b"] = _conv_bwd(
        dxbc, proj_main, XBC_OFF, XBC_COLS, p["ssm_conv_w"], p["ssm_conv_b"], SSM_CONV, "silu", dmain, XBC_OFF,
        "ssm_conv_bwd")
    g["in_main"] = _matmul(x, dmain, "tn", F32, "dw_in_main", tm=1024, tn=1024, tk=512)
    g["in_tail"] = _matmul(x, dtail, "tn", F32, "dw_in_tail", tm=1024, tn=TAIL_COLS, tk=512)
    return sq, (dmain, dtail, dr1), g


def _grad_x(dproj, w, exchange=None):
    dmain, dtail, dr1 = dproj
    dx = _matmul(dtail, w["in_tail"], "nt", F32, "dx_tail", tk=TAIL_COLS, addend=dr1, addend_scale=ALPHA)
    return _matmul(dmain, w["in_main"], "nt", F32, "dx_main", addend=dx, exchange=exchange)


def _split_w_in(w):
    seg = lambda off, n: w[:, off:off + n]
    main = jnp.concatenate([seg(O_Z, 2048), seg(O_XBC, XBC_COLS), seg(O_Q, D_MODEL), seg(O_GATE, 2 * D_MODEL)], axis=1)
    tail = jnp.concatenate([seg(O_K, 128), seg(O_V, 128), seg(O_DT, SSM_HEADS),
                            jnp.zeros((w.shape[0], 128 - SSM_HEADS), w.dtype)], axis=1)
    return main, tail


def _join_w_in(main, tail):
    return jnp.concatenate([main[:, Z_OFF:Z_OFF + 2048], main[:, XBC_OFF:XBC_OFF + XBC_COLS],
                            tail[:, DT_OFF:DT_OFF + SSM_HEADS], main[:, Q_OFF:Q_OFF + D_MODEL],
                            tail[:, K_OFF:K_OFF + 128], tail[:, V_OFF:V_OFF + 128],
                            main[:, GATE_OFF:GATE_OFF + 2 * D_MODEL]], axis=1)


def _prep_weights(w_in, w_bs, w_ba, w_mo, w_up, w_down):
    main, tail = _split_w_in(w_in.astype(BF16))
    return {"in_main": main, "in_tail": tail, "bs": w_bs.astype(BF16), "ba": w_ba.astype(BF16),
            "mo": w_mo.astype(BF16), "up": w_up.astype(BF16), "down": w_down.astype(BF16)}


def _prep_params(rel_bias, b_gate, ssm_conv_w, ssm_conv_b, ssm_dt_bias, ssm_a_log, ssm_d, ssm_norm_w, attn_sinks,
                 ln1_g, ln1_b, ffn_conv_w, ffn_conv_b, ln2_g, ln2_b):
    bucket, in_window = _band_geometry()
    bucket, in_window = bucket.T, in_window.T
    bias_tbl = jnp.where(in_window[None], jnp.transpose(rel_bias[bucket], (2, 0, 1)), NEG)
    onehot = jnp.logical_and(bucket.reshape(-1, 1) == jnp.arange(REL_BUCKETS)[None, :],
                             in_window.reshape(-1, 1)).astype(F32)
    ehead_t = (jnp.arange(SSM_INNER)[None, :] // SSMD == jnp.arange(SSM_HEADS)[:, None]).astype(F32)
    return {"bias_tbl": bias_tbl, "bucket_onehot": onehot, "b_gate": b_gate, "ssm_conv_w": ssm_conv_w,
            "ssm_conv_b": ssm_conv_b, "dtb_col": ssm_dt_bias.reshape(SSM_HEADS, 1),
            "alog_col": ssm_a_log.reshape(SSM_HEADS, 1), "ehead_t": ehead_t,
            "d_exp": jnp.repeat(ssm_d, SSMD, axis=1), "ssm_norm_w": ssm_norm_w, "attn_sinks": attn_sinks,
            "ln1_g": ln1_g, "ln1_b": ln1_b, "ffn_conv_w": ffn_conv_w, "ffn_conv_b": ffn_conv_b,
            "ln2_g": ln2_g, "ln2_b": ln2_b}


def _all_gather(shard, name):
    R, C = shard.shape

    def body(x_ref, out_ref, send_sems, recv_sems, local_sem):
        x, y, c = _mesh_pos()
        me, sibling = (x, y, c), (x, y, 1 - c)
        chips = [(1 - x, y), (x, 1 - y), (1 - x, 1 - y)]

        def slot(px, py, pc):
            return out_ref.at[4 * px + 2 * py + pc]

        def copy(k, block, to, src=None):
            return pltpu.make_async_remote_copy(
                src_ref=slot(*block) if src is None else src, dst_ref=slot(*block),
                send_sem=send_sems.at[k], recv_sem=recv_sems.at[k], device_id=to, device_id_type=MESH_ID)

        mine = pltpu.make_async_copy(x_ref, slot(*me), local_sem)
        mine.start()
        first = [copy(0, me, sibling, src=x_ref)]
        first += [copy(1 + j, me, (*chip, c), src=x_ref) for j, chip in enumerate(chips)]
        for cp in first:
            cp.start()
        passed = [copy(4 + j, (*chip, c), sibling) for j, chip in enumerate(chips)]
        for j, chip in enumerate(chips):
            copy(1 + j, (*chip, c), me).wait_recv()
            passed[j].start()
        copy(0, sibling, me).wait_recv()
        for j, chip in enumerate(chips):
            copy(4 + j, (*chip, 1 - c), me).wait_recv()
        for cp in first + passed:
            cp.wait_send()
        mine.wait()

    return pl.pallas_call(
        body, name=name, out_shape=jax.ShapeDtypeStruct((N_DEV, R, C), shard.dtype),
        in_specs=[pl.BlockSpec(memory_space=pl.ANY)], out_specs=pl.BlockSpec(memory_space=pl.ANY),
        scratch_shapes=[pltpu.SemaphoreType.DMA((7,)), pltpu.SemaphoreType.DMA((7,)), pltpu.SemaphoreType.DMA],
    )(shard)


def _slot_sum(slots, tr=512):
    _, R, C = slots.shape
    assert R % tr == 0

    def body(s_ref, o_ref):
        acc = s_ref[0].astype(F32)
        for i in range(1, N_DEV):
            acc = acc + s_ref[i].astype(F32)
        o_ref[...] = acc

    return pl.pallas_call(
        body, name="slot_sum", grid=(R // tr,),
        in_specs=[pl.BlockSpec((N_DEV, tr, C), lambda i: (0, i, 0))], out_specs=pl.BlockSpec((tr, C), lambda i: (i, 0)),
        out_shape=jax.ShapeDtypeStruct((R, C), F32), compiler_params=_cp(("parallel",)))(slots)


def _adamw_math(w, g, m, v):
    m = ADAM_B1 * m + (1.0 - ADAM_B1) * g
    v = ADAM_B2 * v + (1.0 - ADAM_B2) * (g * g)
    m_hat = m / (1.0 - ADAM_B1 ** ADAM_STEP)
    v_hat = v / (1.0 - ADAM_B2 ** ADAM_STEP)
    return -ADAM_LR * (m_hat / (jnp.sqrt(v_hat) + ADAM_EPS) + ADAM_WD * w), m, v


def _adamw(w, g, m, v, name):
    R, C = w.shape
    tr = 256 if R % 256 == 0 and R > 256 else R

    def body(w_ref, g_ref, m_ref, v_ref, d_ref, nm_ref, nv_ref):
        d_ref[...], nm_ref[...], nv_ref[...] = _adamw_math(w_ref[...], g_ref[...], m_ref[...], v_ref[...])

    spec = pl.BlockSpec((tr, C), lambda i: (i, 0))
    return pl.pallas_call(
        body, name=name, grid=(R // tr,), in_specs=[spec] * 4, out_specs=[spec] * 3,
        out_shape=[jax.ShapeDtypeStruct((R, C), F32)] * 3, compiler_params=_cp(("parallel",)))(w, g, m, v)


def _small_update(gathered, w, m, v):
    def body(s_ref, w_ref, m_ref, v_ref, g_ref, d_ref, nm_ref, nv_ref):
        g = s_ref[0]
        for i in range(1, N_DEV):
            g = g + s_ref[i]
        g_ref[...] = g
        d_ref[...], nm_ref[...], nv_ref[...] = _adamw_math(w_ref[...], g, m_ref[...], v_ref[...])

    return pl.pallas_call(body, name="small_update", out_shape=[jax.ShapeDtypeStruct(w.shape, F32)] * 4)(gathered, w, m, v)


LANES = 128
BIG = (("w_in", 8480, "cols"), ("w_branch_ssm", 2048, "rows"), ("w_branch_attn", 1024, "rows"),
       ("w_mix_out", 1024, "rows"), ("w_up", 5632, "cols"), ("w_down", 2816, "rows"))
CONVW = (("ssm_conv_w", 16), ("ffn_conv_w", 24))
SMALL = ("rel_bias", "b_gate", "ssm_conv_b", "ssm_dt_bias", "ssm_a_log", "ssm_d", "ssm_norm_w", "attn_sinks",
         "ln1_g", "ln1_b", "ffn_conv_b", "ln2_g", "ln2_b")
WEIGHTS = ("rel_bias", "w_in", "b_gate", "ssm_conv_w", "ssm_conv_b", "ssm_dt_bias", "ssm_a_log", "ssm_d", "ssm_norm_w",
           "attn_sinks", "w_branch_ssm", "w_branch_attn", "w_mix_out", "ln1_g", "ln1_b", "w_up", "ffn_conv_w",
           "ffn_conv_b", "w_down", "ln2_g", "ln2_b")
EXCHANGE_ROWS = 21504
SMALL_ROWS = 144


def _rows(a, rows):
    flat = a.reshape(-1)
    return jnp.pad(flat, (0, rows * LANES - flat.shape[0])).reshape(rows, LANES)


def _rows8(a, rows):
    return jnp.pad(a, ((0, 0), (0, rows * LANES - a.shape[1]))).reshape(N_DEV, rows, LANES)


def _by_device(full, how):
    r, c = full.shape
    if how == "rows":
        return full.reshape(N_DEV, -1)
    return full.reshape(r, N_DEV, c // N_DEV).transpose(1, 0, 2).reshape(N_DEV, -1)


def _from_devices(slots, r, c, how):
    if how == "rows":
        return slots.reshape(r, c)
    return slots.reshape(N_DEV, r, c // N_DEV).transpose(1, 0, 2).reshape(r, c)


def kernel(x, rel_bias, w_in, b_gate, ssm_conv_w, ssm_conv_b, ssm_dt_bias, ssm_a_log, ssm_d, ssm_norm_w, attn_sinks, w_branch_ssm, w_branch_attn, w_mix_out, ln1_g, ln1_b, w_up, ffn_conv_w, ffn_conv_b, w_down, ln2_g, ln2_b, loss_target, m_rel_bias, m_w_in, m_b_gate, m_ssm_conv_w, m_ssm_conv_b, m_ssm_dt_bias, m_ssm_a_log, m_ssm_d, m_ssm_norm_w, m_attn_sinks, m_w_branch_ssm, m_w_branch_attn, m_w_mix_out, m_ln1_g, m_ln1_b, m_w_up, m_ffn_conv_w, m_ffn_conv_b, m_w_down, m_ln2_g, m_ln2_b, v_rel_bias, v_w_in, v_b_gate, v_ssm_conv_w, v_ssm_conv_b, v_ssm_dt_bias, v_ssm_a_log, v_ssm_d, v_ssm_norm_w, v_attn_sinks, v_w_branch_ssm, v_w_branch_attn, v_w_mix_out, v_ln1_g, v_ln1_b, v_w_up, v_ffn_conv_w, v_ffn_conv_b, v_w_down, v_ln2_g, v_ln2_b):
    W = dict(zip(WEIGHTS, (rel_bias, w_in, b_gate, ssm_conv_w, ssm_conv_b, ssm_dt_bias, ssm_a_log, ssm_d, ssm_norm_w,
                           attn_sinks, w_branch_ssm, w_branch_attn, w_mix_out, ln1_g, ln1_b, w_up, ffn_conv_w,
                           ffn_conv_b, w_down, ln2_g, ln2_b)))
    M = dict(zip(WEIGHTS, (m_rel_bias, m_w_in, m_b_gate, m_ssm_conv_w, m_ssm_conv_b, m_ssm_dt_bias, m_ssm_a_log, m_ssm_d,
                           m_ssm_norm_w, m_attn_sinks, m_w_branch_ssm, m_w_branch_attn, m_w_mix_out, m_ln1_g, m_ln1_b,
                           m_w_up, m_ffn_conv_w, m_ffn_conv_b, m_w_down, m_ln2_g, m_ln2_b)))
    V = dict(zip(WEIGHTS, (v_rel_bias, v_w_in, v_b_gate, v_ssm_conv_w, v_ssm_conv_b, v_ssm_dt_bias, v_ssm_a_log, v_ssm_d,
                           v_ssm_norm_w, v_attn_sinks, v_w_branch_ssm, v_w_branch_attn, v_w_mix_out, v_ln1_g, v_ln1_b,
                           v_w_up, v_ffn_conv_w, v_ffn_conv_b, v_w_down, v_ln2_g, v_ln2_b)))
    shard2d = lambda a: a.reshape(a.shape[-2], a.shape[-1])

    big_shard = jnp.concatenate([_rows(shard2d(W[n]).astype(BF16), rows) for n, rows, _ in BIG], axis=0)
    conv_shard = jnp.concatenate([_rows(shard2d(W[n]), rows) for n, rows in CONVW], axis=0)
    big_all = _all_gather(big_shard, "gather_matmul_weights")
    conv_all = _all_gather(conv_shard, "gather_conv_weights")
    full = {}
    off = 0
    for n, rows, how in BIG:
        r, c = shard2d(W[n]).shape
        fr, fc = (r * N_DEV, c) if how == "rows" else (r, c * N_DEV)
        full[n] = _from_devices(big_all[:, off:off + rows].reshape(N_DEV, -1), fr, fc, how)
        off += rows
    off = 0
    for n, rows in CONVW:
        r, c = shard2d(W[n]).shape
        full[n] = _from_devices(conv_all[:, off:off + rows].reshape(N_DEV, -1)[:, :r * c], r, c * N_DEV, "cols")
        off += rows

    w = _prep_weights(full["w_in"], full["w_branch_ssm"], full["w_branch_attn"], full["w_mix_out"], full["w_up"],
                      full["w_down"])
    p = _prep_params(rel_bias, b_gate, full["ssm_conv_w"], ssm_conv_b, ssm_dt_bias, ssm_a_log, ssm_d, ssm_norm_w,
                     attn_sinks, ln1_g, ln1_b, full["ffn_conv_w"], ffn_conv_b, ln2_g, ln2_b)
    sq, dproj, g = _local_step(x[0], loss_target[0], w, p)
    g["w_in"] = _join_w_in(g.pop("in_main"), g.pop("in_tail"))
    loss = (0.5 / D_MODEL) * lax.psum(sq[0, 0], ("x", "y", "c"))

    parts = [_rows8(_by_device(g[n], how), rows) for n, rows, how in BIG]
    parts += [_rows8(_by_device(g[n], "cols"), rows) for n, rows in CONVW]
    used = sum(rows for _, rows, _ in BIG) + sum(rows for _, rows in CONVW)
    parts.append(jnp.zeros((N_DEV, EXCHANGE_ROWS - used, LANES), F32))
    dx, received = _grad_x(dproj, w, exchange=jnp.concatenate(parts, axis=1).astype(BF16))
    g_rows = _slot_sum(received)
    grads, deltas, new_m, new_v = {}, {}, {}, {}
    off = 0
    for n, rows in [(n, rows) for n, rows, _ in BIG] + list(CONVW):
        r, c = shard2d(W[n]).shape
        gn = g_rows[off:off + rows].reshape(-1)[:r * c].reshape(r, c)
        off += rows
        d, nm, nv = _adamw(shard2d(W[n]), gn, shard2d(M[n]), shard2d(V[n]), "adamw_" + n)
        grads[n], deltas[n], new_m[n], new_v[n] = (a.reshape(W[n].shape) for a in (gn, d, nm, nv))

    pack = lambda src: _rows(jnp.concatenate([src[n].reshape(-1) for n in SMALL]), SMALL_ROWS)
    small_all = _all_gather(pack(g), "gather_small_grads")
    outs = _small_update(small_all, pack(W), pack(M), pack(V))
    off = 0
    for n in SMALL:
        size = W[n].size
        grads[n], deltas[n], new_m[n], new_v[n] = (a.reshape(-1)[off:off + size].reshape(W[n].shape) for a in outs)
        off += size

    return (loss, dx[None], *[grads[n] for n in WEIGHTS], *[deltas[n] for n in WEIGHTS],
            *[new_m[n] for n in WEIGHTS], *[new_v[n] for n in WEIGHTS])
```

```python
import functools
import math

import jax
import jax.numpy as jnp
from jax import lax
from jax.experimental import pallas as pl
from jax.experimental.pallas import tpu as pltpu

F32, BF16 = jnp.float32, jnp.bfloat16
HIGHEST = lax.Precision.HIGHEST
MESH_ID = pl.DeviceIdType.MESH

N_DEV = 8
D_MODEL = 1024
SSM_INNER = 2048
SSM_HEADS = 32
SSM_HEADDIM = 64
SSMD = SSM_HEADDIM
SSM_GROUPS = 4
SSM_GROUP_COLS = SSM_INNER // SSM_GROUPS
SSM_STATE = 128
SSM_CONV = 4
CHUNK = 128
XBC_COLS = SSM_INNER + 2 * SSM_GROUPS * SSM_STATE
B_OFF = SSM_INNER
C_OFF = SSM_INNER + SSM_GROUPS * SSM_STATE
ATTN_HEADS = 16
ATTN_KV = 2
ATTN_GROUP = 8
HEADDIM = 64
WINDOW = 128
REL_BUCKETS = 32
REL_MAX_DIST = 128
D_FF = 2816
FFN_CONV = 3
ALPHA = 2.0 ** 0.25
LN_EPS = 1e-5
RMS_EPS = 1e-5
IN_COLS = 8480
Z_OFF, XBC_OFF, Q_OFF, GATE_OFF, MAIN_COLS = 0, 2048, 5120, 6144, 8192
K_OFF, V_OFF, DT_OFF, TAIL_COLS = 0, 128, 256, 384
O_Z, O_XBC, O_DT, O_Q, O_K, O_V, O_GATE = 0, 2048, 5120, 5152, 6176, 6304, 6432

ADAM_LR, ADAM_B1, ADAM_B2, ADAM_EPS, ADAM_WD, ADAM_STEP = 0.001, 0.9, 0.999, 1e-08, 0.01, 10
NEG = -1e30
HALO = 8
VMEM_LIMIT = 56 * 1024 * 1024


def _cp(sem):
    return pltpu.CompilerParams(dimension_semantics=sem, vmem_limit_bytes=VMEM_LIMIT)


def _const_spec(shape):
    nd = len(shape)
    return pl.BlockSpec(shape, lambda *_: (0,) * nd)


def _sigmoid(x):
    return 0.5 * jnp.tanh(0.5 * x) + 0.5


def _softplus(x):
    return jnp.maximum(x, 0.0) + jnp.log1p(jnp.exp(-jnp.abs(x)))


def _dot(a, b, dims=(((1,), (0,)), ((), ())), precision=None):
    return lax.dot_general(a, b, dims, preferred_element_type=F32, precision=precision)


NN = (((1,), (0,)), ((), ()))
NT = (((1,), (1,)), ((), ()))
TN = (((0,), (0,)), ((), ()))


def _mesh_pos():
    return lax.axis_index("x"), lax.axis_index("y"), lax.axis_index("c")


def _exchange_phases(in_ref, out_ref, send_sems, recv_sems, local_sem):
    def copies():
        x, y, c = _mesh_pos()
        me = 4 * x + 2 * y + c
        cps = [pltpu.make_async_copy(in_ref.at[me], out_ref.at[me], local_sem)]
        for r in range(1, N_DEV):
            px = 1 - x if r & 4 else x
            py = 1 - y if r & 2 else y
            pc = 1 - c if r & 1 else c
            cps.append(pltpu.make_async_remote_copy(
                src_ref=in_ref.at[4 * px + 2 * py + pc], dst_ref=out_ref.at[me],
                send_sem=send_sems.at[r - 1], recv_sem=recv_sems.at[r - 1],
                device_id=(px, py, pc), device_id_type=MESH_ID))
        return cps

    def start():
        for cp in copies():
            cp.start()

    def finish():
        for cp in copies():
            cp.wait()

    return [start, finish]


def _gather_phases(x_ref, out_ref, send_sems, recv_sems, local_sem):
    def parts(which):
        x, y, c = _mesh_pos()
        me, sibling = (x, y, c), (x, y, 1 - c)
        chips = [(1 - x, y), (x, 1 - y), (1 - x, 1 - y)]

        def slot(px, py, pc):
            return out_ref.at[4 * px + 2 * py + pc]

        def copy(k, block, to, src=None):
            return pltpu.make_async_remote_copy(
                src_ref=slot(*block) if src is None else src, dst_ref=slot(*block),
                send_sem=send_sems.at[k], recv_sem=recv_sems.at[k], device_id=to, device_id_type=MESH_ID)

        if which == "mine":
            return pltpu.make_async_copy(x_ref, slot(*me), local_sem)
        if which == "first":
            return [copy(0, me, sibling, src=x_ref)] + [copy(1 + j, me, (*chip, c), src=x_ref) for j, chip in enumerate(chips)]
        if which == "passed":
            return [copy(4 + j, (*chip, c), sibling) for j, chip in enumerate(chips)]
        if which == "arrived":
            return [copy(1 + j, (*chip, c), me) for j, chip in enumerate(chips)]
        return [copy(0, sibling, me)] + [copy(4 + j, (*chip, 1 - c), me) for j, chip in enumerate(chips)]

    def start():
        parts("mine").start()
        for cp in parts("first"):
            cp.start()

    def forward():
        for a, p in zip(parts("arrived"), parts("passed")):
            a.wait_recv()
            p.start()

    def finish():
        for cp in parts("late"):
            cp.wait_recv()
        for cp in parts("first") + parts("passed"):
            cp.wait_send()
        parts("mine").wait()

    return [start, forward, finish]


COMM = {"exchange": _exchange_phases, "gather": _gather_phases}


def _matmul(a, b, mode, out_dtype, name, tm=512, tn=1024, tk=1024, addend=None, addend_scale=1.0, comm=None):
    exchange = None if comm is None else comm[1]
    if mode == "nn":
        (M, K), (K2, N) = a.shape, b.shape
    elif mode == "nt":
        (M, K), (N, K2) = a.shape, b.shape
    else:
        (K, M), (K2, N) = a.shape, b.shape
    assert K == K2, (a.shape, b.shape, mode)
    tm, tn, tk = min(tm, M), min(tn, N), min(tk, K)
    assert M % tm == 0 and N % tn == 0 and K % tk == 0, (M, N, K, tm, tn, tk)
    nk = K // tk
    dims = {"nn": NN, "nt": NT, "tn": TN}[mode]
    a_spec = pl.BlockSpec((tk, tm), lambda i, j, k: (k, i)) if mode == "tn" else pl.BlockSpec((tm, tk), lambda i, j, k: (i, k))
    b_spec = pl.BlockSpec((tn, tk), lambda i, j, k: (j, k)) if mode == "nt" else pl.BlockSpec((tk, tn), lambda i, j, k: (k, j))
    o_spec = pl.BlockSpec((tm, tn), lambda i, j, k: (i, j))

    ni, nj = M // tm, N // tn

    def body(*refs):
        refs = list(refs)
        a_ref, b_ref = refs[:2]
        c_ref = refs[2] if addend is not None else None
        n_in = 2 + (addend is not None) + (exchange is not None)
        o_ref, acc = refs[n_in], refs[n_in + 1 + (exchange is not None)]
        i, j, k = pl.program_id(0), pl.program_id(1), pl.program_id(2)
        step = (i * nj + j) * nk + k
        if comm is not None:
            phases = COMM[comm[0]](refs[n_in - 1], refs[n_in + 1], *refs[n_in + 3:])
            at = [(ni * nj * nk - 1) * p // (len(phases) - 1) for p in range(len(phases))]
            for when, phase in zip(at[:-1], phases[:-1]):
                pl.when(step == when)(phase)

        @pl.when(k == 0)
        def _():
            acc[...] = jnp.zeros_like(acc)

        acc[...] += _dot(a_ref[...].astype(BF16), b_ref[...].astype(BF16), dims)

        @pl.when(k == nk - 1)
        def _():
            r = acc[...]
            if addend is not None:
                r = r + addend_scale * c_ref[...].astype(F32)
            o_ref[...] = r.astype(out_dtype)

        if comm is not None:
            pl.when(step == at[-1])(phases[-1])

    in_specs = [a_spec, b_spec] + ([o_spec] if addend is not None else [])
    args = (a, b) + ((addend,) if addend is not None else ())
    out_specs, out_shape = o_spec, jax.ShapeDtypeStruct((M, N), out_dtype)
    scratch = [pltpu.VMEM((tm, tn), F32)]
    sem = ("parallel", "parallel", "arbitrary")
    if exchange is not None:
        any_spec = pl.BlockSpec(memory_space=pl.ANY)
        in_specs, args = in_specs + [any_spec], args + (exchange,)
        landed = exchange.shape if comm[0] == "exchange" else (N_DEV,) + exchange.shape
        out_specs, out_shape = [o_spec, any_spec], [out_shape, jax.ShapeDtypeStruct(landed, exchange.dtype)]
        scratch += [pltpu.SemaphoreType.DMA((N_DEV - 1,)), pltpu.SemaphoreType.DMA((N_DEV - 1,)), pltpu.SemaphoreType.DMA]
        sem = ("arbitrary", "arbitrary", "arbitrary")
    return pl.pallas_call(
        body, name=name, grid=(ni, nj, nk), in_specs=in_specs, out_specs=out_specs, out_shape=out_shape,
        scratch_shapes=scratch, compiler_params=_cp(sem))(*args)


def _taps(w_ref, K, tc):
    return [jnp.broadcast_to(w_ref[k:k + 1, :], (HALO, tc)) for k in range(K)]


def _conv_fwd(pre, pre_col_off, C, w, b, K, act, name, tr=512, tc=512):
    T = pre.shape[0]
    tc = min(tc, C)
    assert T % tr == 0 and C % tc == 0 and pre_col_off % tc == 0
    joff = pre_col_off // tc
    hb = tr // HALO

    def body(x_ref, xp_ref, w_ref, b_ref, o_ref, head):
        i = pl.program_id(1)
        head[0:HALO, :] = jnp.where(i > 0, xp_ref[...], 0.0)
        head[HALO:, :] = x_ref[0:HALO, :]
        wk = _taps(w_ref, K, tc)
        bias = jnp.broadcast_to(b_ref[...], (HALO, tc))
        for r in range(tr // HALO):
            lo = r * HALO
            acc = bias + wk[K - 1] * x_ref[lo:lo + HALO, :]
            for k in range(K - 1):
                s = K - 1 - k
                acc = acc + wk[k] * (head[HALO - s:2 * HALO - s, :] if r == 0 else x_ref[lo - s:lo + HALO - s, :])
            if act == "silu":
                acc = acc * _sigmoid(acc)
            o_ref[lo:lo + HALO, :] = acc

    return pl.pallas_call(
        body, name=name, grid=(C // tc, T // tr),
        in_specs=[pl.BlockSpec((tr, tc), lambda j, i: (i, joff + j)),
                  pl.BlockSpec((HALO, tc), lambda j, i: (jnp.maximum(i * hb - 1, 0), joff + j)),
                  pl.BlockSpec((K, tc), lambda j, i: (0, j)),
                  pl.BlockSpec((1, tc), lambda j, i: (0, j))],
        out_specs=pl.BlockSpec((tr, tc), lambda j, i: (i, j)),
        out_shape=jax.ShapeDtypeStruct((T, C), F32),
        scratch_shapes=[pltpu.VMEM((2 * HALO, tc), F32)],
        compiler_params=_cp(("parallel", "arbitrary")))(pre, pre, w, b)


def _conv_bwd(dout, pre, pre_col_off, C, w, b, K, act, dst, dst_col_off, name, tr=512, tc=512):
    T = pre.shape[0]
    tc = min(tc, C)
    assert T % tr == 0 and C % tc == 0 and pre_col_off % tc == 0 and dst_col_off % tc == 0
    joff, doff = pre_col_off // tc, dst_col_off // tc
    hb = tr // HALO
    nt = T // tr
    n = tr // HALO
    last_hblock = T // HALO - 1

    def body(g_ref, gn_ref, x_ref, xp_ref, xn_ref, w_ref, b_ref, *rest):
        o_ref, dw_ref, db_ref, head, tail, gext = rest[-6:]
        i = pl.program_id(1)
        not_last = i < nt - 1
        head[0:HALO, :] = jnp.where(i > 0, xp_ref[...], 0.0)
        head[HALO:, :] = x_ref[0:HALO, :]
        tail[0:HALO, :] = x_ref[tr - HALO:tr, :]
        tail[HALO:, :] = jnp.where(not_last, xn_ref[...], 0.0)
        wk = _taps(w_ref, K, tc)

        def x_at(r, s):
            if r == 0:
                return head[HALO - s:2 * HALO - s, :]
            if r == n:
                return tail[HALO - s:2 * HALO - s, :]
            return x_ref[r * HALO - s:(r + 1) * HALO - s, :]

        acc_w = [jnp.zeros((HALO, tc), F32) for _ in range(K)]
        acc_b = jnp.zeros((HALO, tc), F32)
        for r in range(n + 1 if act == "silu" else n):
            g = g_ref[r * HALO:(r + 1) * HALO, :] if r < n else jnp.where(not_last, gn_ref[...], 0.0)
            xs = [x_at(r, K - 1 - k) for k in range(K)] if (act == "silu" or r < n) else None
            if act == "silu":
                co = jnp.broadcast_to(b_ref[...], (HALO, tc))
                for k in range(K):
                    co = co + wk[k] * xs[k]
                sg = _sigmoid(co)
                g = g * (sg * (1.0 + co * (1.0 - sg)))
            gext[r * HALO:(r + 1) * HALO, :] = g
            if r < n:
                acc_b = acc_b + g
                for k in range(K):
                    acc_w[k] = acc_w[k] + g * xs[k]
        if act != "silu":
            gext[n * HALO:, :] = jnp.where(not_last, gn_ref[...], 0.0)
        for r in range(n):
            lo = r * HALO
            dpre = wk[K - 1] * gext[lo:lo + HALO, :]
            for k in range(K - 1):
                s = K - 1 - k
                dpre = dpre + wk[k] * gext[lo + s:lo + HALO + s, :]
            o_ref[lo:lo + HALO, :] = dpre.astype(o_ref.dtype)

        @pl.when(i == 0)
        def _():
            dw_ref[...] = jnp.zeros_like(dw_ref)
            db_ref[...] = jnp.zeros_like(db_ref)

        db_ref[...] += jnp.sum(acc_b, axis=0, keepdims=True)
        dw_ref[...] += jnp.concatenate([jnp.sum(a, axis=0, keepdims=True) for a in acc_w], axis=0)

    tile = lambda off: pl.BlockSpec((tr, tc), lambda j, i: (i, off + j))
    nxt = lambda off: pl.BlockSpec((HALO, tc), lambda j, i: (jnp.minimum((i + 1) * hb, last_hblock), off + j))
    in_specs = [tile(0), nxt(0), tile(joff),
                pl.BlockSpec((HALO, tc), lambda j, i: (jnp.maximum(i * hb - 1, 0), joff + j)), nxt(joff),
                pl.BlockSpec((K, tc), lambda j, i: (0, j)), pl.BlockSpec((1, tc), lambda j, i: (0, j))]
    args = (dout, dout, pre, pre, pre, w, b)
    if isinstance(dst, jax.ShapeDtypeStruct):
        aliases = {}
    else:
        in_specs.append(pl.BlockSpec(memory_space=pl.ANY))
        args += (dst,)
        aliases = {7: 0}
    return pl.pallas_call(
        body, name=name, grid=(C // tc, nt), in_specs=in_specs,
        out_specs=[tile(doff), pl.BlockSpec((K, tc), lambda j, i: (0, j)), pl.BlockSpec((1, tc), lambda j, i: (0, j))],
        out_shape=[jax.ShapeDtypeStruct(dst.shape, dst.dtype), jax.ShapeDtypeStruct((K, C), F32),
                   jax.ShapeDtypeStruct((1, C), F32)],
        scratch_shapes=[pltpu.VMEM((2 * HALO, tc), F32), pltpu.VMEM((2 * HALO, tc), F32),
                        pltpu.VMEM((tr + HALO, tc), F32)],
        input_output_aliases=aliases,
        compiler_params=_cp(("parallel", "arbitrary")))(*args)


PAIR = 2 * SSMD
PAIRS_PER_GROUP = SSM_GROUP_COLS // PAIR


def _dot3(x, onehot):
    h1 = x.astype(BF16)
    r = x - h1.astype(F32)
    h2 = r.astype(BF16)
    h3 = (r - h2.astype(F32)).astype(BF16)
    return _dot(h1, onehot) + _dot(h2, onehot) + _dot(h3, onehot)


def _chunk_rows(dt_raw, dtb_col, alog_col):
    row = lax.broadcasted_iota(jnp.int32, (CHUNK, CHUNK), 0)
    col = lax.broadcasted_iota(jnp.int32, (CHUNK, CHUNK), 1)
    dt_rawT = dt_raw.T
    dtT = _softplus(dt_rawT + dtb_col)
    a_col = -jnp.exp(alog_col)
    acsT = _dot3(dtT * a_col, (row <= col).astype(BF16))
    return dt_rawT, dtT, a_col, acsT, row, col


def _block_diag(x, left):
    return jnp.concatenate([jnp.where(left, x, 0.0), jnp.where(left, 0.0, x)], axis=0).astype(BF16)


def _lane_bcast(v, h):
    return jnp.broadcast_to(v[:, h:h + 1], (CHUNK, CHUNK))


def _ssd_fwd(xbc, proj_main, proj_tail, dtb_col, alog_col, d_exp, norm_w):
    T = xbc.shape[0]
    nc = T // CHUNK

    def body(xbc_ref, dt_ref, z_ref, dtb_ref, alog_ref, d_ref, nw_ref, y_ref, ypre_ref, hs_ref, H):
        c = pl.program_id(0)

        @pl.when(c == 0)
        def _():
            H[...] = jnp.zeros_like(H)

        hs_ref[0] = H[...]
        _, dtT, _, acsT, row, col = _chunk_rows(dt_ref[:, 0:SSM_HEADS], dtb_ref[...], alog_ref[...])
        tril, left = row >= col, col < SSMD
        acs = acsT.T
        w = (dtT * jnp.exp(acsT[:, CHUNK - 1:CHUNK] - acsT)).T
        cd = jnp.exp(acs[CHUNK - 1:CHUNK, :])
        for g in range(SSM_GROUPS):
            gs = slice(g * SSM_GROUP_COLS, (g + 1) * SSM_GROUP_COLS)
            Bb = xbc_ref[:, B_OFF + g * SSM_STATE:B_OFF + (g + 1) * SSM_STATE].astype(BF16)
            Cb = xbc_ref[:, C_OFF + g * SSM_STATE:C_OFF + (g + 1) * SSM_STATE].astype(BF16)
            Hg = H[:, gs]
            CH = _dot(Cb, Hg.astype(BF16))
            CB = _dot(Cb, Bb, NT)
            ys, xws = [], []
            for kk in range(PAIRS_PER_GROUP):
                k = g * PAIRS_PER_GROUP + kk
                xs_p = xbc_ref[:, k * PAIR:(k + 1) * PAIR]
                mps, ecols, wcols = [], [], []
                for j in range(2):
                    h = 2 * k + j
                    colb = _lane_bcast(acs, h)
                    L = jnp.exp(jnp.where(tril, colb - acsT[h:h + 1, :], -jnp.inf))
                    mps.append((CB * L * dtT[h:h + 1, :]).astype(BF16))
                    ecols.append(jnp.exp(colb))
                    wcols.append(_lane_bcast(w, h))
                yd = _dot(jnp.concatenate(mps, axis=1), _block_diag(xs_p, left))
                ys.append(yd + CH[:, kk * PAIR:(kk + 1) * PAIR] * jnp.where(left, ecols[0], ecols[1]))
                xws.append((xs_p * jnp.where(left, wcols[0], wcols[1])).astype(BF16))
            cd_e = jnp.concatenate([jnp.broadcast_to(cd[:, g * 8 + e:g * 8 + e + 1], (1, SSMD)) for e in range(8)], axis=1)
            H[:, gs] = Hg * cd_e + _dot(Bb, jnp.concatenate(xws, axis=1), TN)
            ypre = jnp.concatenate(ys, axis=1) + xbc_ref[:, gs] * d_ref[:, gs]
            ypre_ref[:, gs] = ypre
            z = z_ref[:, gs]
            yg = ypre * (z * _sigmoid(z))
            r = lax.rsqrt(jnp.mean(yg * yg, axis=1, keepdims=True) + RMS_EPS)
            y_ref[:, gs] = (yg * r * nw_ref[:, gs]).astype(BF16)

    vec = lambda n: _const_spec((1, n))
    colv = _const_spec((SSM_HEADS, 1))
    return pl.pallas_call(
        body, name="ssd_fwd", grid=(nc,),
        in_specs=[pl.BlockSpec((CHUNK, XBC_COLS), lambda c: (c, 0)),
                  pl.BlockSpec((CHUNK, 128), lambda c: (c, DT_OFF // 128)),
                  pl.BlockSpec((CHUNK, SSM_INNER), lambda c: (c, Z_OFF // SSM_INNER)),
                  colv, colv, vec(SSM_INNER), vec(SSM_INNER)],
        out_specs=[pl.BlockSpec((CHUNK, SSM_INNER), lambda c: (c, 0)),
                   pl.BlockSpec((CHUNK, SSM_INNER), lambda c: (c, 0)),
                   pl.BlockSpec((1, SSM_STATE, SSM_INNER), lambda c: (c, 0, 0))],
        out_shape=[jax.ShapeDtypeStruct((T, SSM_INNER), BF16), jax.ShapeDtypeStruct((T, SSM_INNER), F32),
                   jax.ShapeDtypeStruct((nc, SSM_STATE, SSM_INNER), F32)],
        scratch_shapes=[pltpu.VMEM((SSM_STATE, SSM_INNER), F32)],
        compiler_params=_cp(("arbitrary",)))(xbc, proj_tail, proj_main, dtb_col, alog_col, d_exp, norm_w)


def _ssd_bwd(dyo, ypre, xbc, hs, proj_main, proj_tail, dtb_col, alog_col, d_exp, norm_w, ehead_t, dmain, dtail):
    T = xbc.shape[0]
    nc = T // CHUNK

    def body(dyo_ref, ypre_ref, xbc_ref, hs_ref, dt_ref, z_ref, dtb_ref, alog_ref, d_ref, nw_ref, eh_ref,
             dmain_in, dtail_in, dz_ref, ddt_ref, dxbc_ref, dnw_ref, dd_ref, dalog_ref, ddtb_ref, G):
        del dmain_in, dtail_in
        c = pl.program_id(0)

        @pl.when(c == 0)
        def _():
            G[...] = jnp.zeros_like(G)
            dnw_ref[...] = jnp.zeros_like(dnw_ref)
            dd_ref[...] = jnp.zeros_like(dd_ref)
            dalog_ref[...] = jnp.zeros_like(dalog_ref)
            ddtb_ref[...] = jnp.zeros_like(ddtb_ref)

        dt_rawT, dtT, a_col, acsT, row, col = _chunk_rows(dt_ref[:, 0:SSM_HEADS], dtb_ref[...], alog_ref[...])
        tril, triu, left = row >= col, col >= row, col < SSMD
        acs = acsT.T
        dt = dtT.T
        lastT = acsT[:, CHUNK - 1:CHUNK]
        dstT = jnp.exp(lastT - acsT)
        wT = dtT * dstT
        cd = jnp.exp(acs[CHUNK - 1:CHUNK, :])
        ddt_rows, rs_rows, deo_rows, dw_rows = [], [], [], []
        dd_cols, gh_cols, dnw_cols = [], [], []
        for g in range(SSM_GROUPS):
            gs = slice(g * SSM_GROUP_COLS, (g + 1) * SSM_GROUP_COLS)
            z = z_ref[:, gs]
            sz = _sigmoid(z)
            silu_z = z * sz
            ypre = ypre_ref[:, gs]
            yg = ypre * silu_z
            r = lax.rsqrt(jnp.mean(yg * yg, axis=1, keepdims=True) + RMS_EPS)
            ygn = yg * r
            dyo = dyo_ref[:, gs]
            dyn = dyo * nw_ref[:, gs]
            dnw_cols.append(jnp.sum(dyo * ygn, axis=0, keepdims=True))
            dyg = r * (dyn - ygn * jnp.mean(dyn * ygn, axis=1, keepdims=True))
            dz_ref[:, gs] = (dyg * ypre * (sz * (1.0 + z * (1.0 - sz)))).astype(dz_ref.dtype)
            dY = dyg * silu_z
            xs = xbc_ref[:, gs]
            dd_cols.append(jnp.sum(dY * xs, axis=0, keepdims=True))
            Bf = xbc_ref[:, B_OFF + g * SSM_STATE:B_OFF + (g + 1) * SSM_STATE]
            Cf = xbc_ref[:, C_OFF + g * SSM_STATE:C_OFF + (g + 1) * SSM_STATE]
            Bb, Cb = Bf.astype(BF16), Cf.astype(BF16)
            BT, CT = Bf.T, Cf.T
            CB = _dot(Cb, Bb, NT)
            CBT = _dot(Bb, Cb, NT)
            Hg = hs_ref[0, :, gs]
            Gg = G[:, gs]
            gh_cols.append(jnp.sum(Gg * Hg, axis=0, keepdims=True))
            dCB = jnp.zeros((CHUNK, CHUNK), F32)
            dxs_d, dyes, xws, wsels = [], [], [], []
            for kk in range(PAIRS_PER_GROUP):
                k = g * PAIRS_PER_GROUP + kk
                ps = slice(kk * PAIR, (kk + 1) * PAIR)
                xs_p, dY_p = xs[:, ps], dY[:, ps]
                Ls, LTs, dtcols, ecols, wcols = [], [], [], [], []
                for j in range(2):
                    h = 2 * k + j
                    colb = _lane_bcast(acs, h)
                    seg = colb - acsT[h:h + 1, :]
                    Ls.append(jnp.exp(jnp.where(tril, seg, -jnp.inf)))
                    LTs.append(jnp.exp(jnp.where(triu, -seg, -jnp.inf)))
                    dtcol = _lane_bcast(dt, h)
                    dtcols.append(dtcol)
                    ecols.append(jnp.exp(colb))
                    wcols.append(dtcol * jnp.exp(acs[CHUNK - 1:CHUNK, h:h + 1] - colb))
                wsel = jnp.where(left, wcols[0], wcols[1])
                dYe_p = dY_p * jnp.where(left, ecols[0], ecols[1])
                bdx = _block_diag(xs_p, left)
                bddy = _block_diag(dY_p, left)
                dMx2 = _dot(dY_p.astype(BF16), bdx, NT)
                dMxT2 = _dot(xs_p.astype(BF16), bddy, NT)
                Q1 = _dot(Hg[:, ps].astype(BF16), _block_diag(dYe_p, left), NT)
                Q2 = _dot(Gg[:, ps].astype(BF16), bdx, NT)
                mts = []
                for j in range(2):
                    h = 2 * k + j
                    js = slice(j * CHUNK, (j + 1) * CHUNK)
                    dMx = dMx2[:, js]
                    A = CB * Ls[j]
                    AT = CBT * LTs[j]
                    ddt_rows.append(jnp.sum(A * dMx, axis=0, keepdims=True))
                    ATd = AT * dtcols[j]
                    rs_rows.append(jnp.sum(ATd * dMxT2[:, js], axis=0, keepdims=True))
                    dCB = dCB + dMx * Ls[j] * dtT[h:h + 1, :]
                    mts.append(ATd.astype(BF16))
                    deo_rows.append(jnp.sum(CT * Q1[:, js], axis=0, keepdims=True))
                    dw_rows.append(jnp.sum(BT * Q2[:, js], axis=0, keepdims=True))
                dxs_d.append(_dot(jnp.concatenate(mts, axis=1), bddy))
                dyes.append(dYe_p.astype(BF16))
                xws.append((xs_p * wsel).astype(BF16))
                wsels.append(wsel)
            dYe_g = jnp.concatenate(dyes, axis=1)
            xw_g = jnp.concatenate(xws, axis=1)
            Hgb, Ggb, dCBb = Hg.astype(BF16), Gg.astype(BF16), dCB.astype(BF16)
            dxbc_ref[:, C_OFF + g * SSM_STATE:C_OFF + (g + 1) * SSM_STATE] = _dot(dYe_g, Hgb, NT) + _dot(dCBb, Bb)
            dxbc_ref[:, B_OFF + g * SSM_STATE:B_OFF + (g + 1) * SSM_STATE] = _dot(xw_g, Ggb, NT) + _dot(dCBb, Cb, TN)
            BG = _dot(Bb, Ggb)
            dxbc_ref[:, gs] = (jnp.concatenate(dxs_d, axis=1) + BG * jnp.concatenate(wsels, axis=1)
                               + dY * d_ref[:, gs])
            cd_e = jnp.concatenate([jnp.broadcast_to(cd[:, g * 8 + e:g * 8 + e + 1], (1, SSMD)) for e in range(8)], axis=1)
            G[:, gs] = Gg * cd_e + _dot(Cb, dYe_g, TN)
        dnw_ref[...] += jnp.concatenate(dnw_cols, axis=1)
        eh = eh_ref[...]
        dd_ref[...] += jnp.sum(eh * jnp.concatenate(dd_cols, axis=1), axis=1, keepdims=True)
        dcd = jnp.sum(eh * jnp.concatenate(gh_cols, axis=1), axis=1, keepdims=True)
        DDT = jnp.concatenate(ddt_rows, axis=0)
        DW = jnp.concatenate(dw_rows, axis=0)
        DWw = DW * wT
        dacsT = jnp.concatenate(rs_rows, axis=0) - DDT * dtT + jnp.concatenate(deo_rows, axis=0) - DWw
        end = jnp.sum(DWw, axis=1, keepdims=True) + dcd * jnp.exp(lastT)
        lane = lax.broadcasted_iota(jnp.int32, (SSM_HEADS, CHUNK), 1)
        dacsT = dacsT + jnp.where(lane == CHUNK - 1, end, 0.0)
        dadtT = _dot3(dacsT, tril.astype(BF16))
        ddtT = dadtT * a_col + DDT + DW * dstT
        dalog_ref[...] += jnp.sum(dadtT * dtT, axis=1, keepdims=True) * a_col
        ddt_rawT = ddtT * _sigmoid(dt_rawT + dtb_ref[...])
        ddtb_ref[...] += jnp.sum(ddt_rawT, axis=1, keepdims=True)
        ddt_ref[...] = jnp.concatenate([ddt_rawT.T, jnp.zeros((CHUNK, 128 - SSM_HEADS), F32)], axis=1).astype(ddt_ref.dtype)

    rev = lambda c: nc - 1 - c
    vec = lambda n: _const_spec((1, n))
    colv = _const_spec((SSM_HEADS, 1))
    any_spec = pl.BlockSpec(memory_space=pl.ANY)
    return pl.pallas_call(
        body, name="ssd_bwd", grid=(nc,),
        in_specs=[pl.BlockSpec((CHUNK, SSM_INNER), lambda c: (rev(c), 0)),
                  pl.BlockSpec((CHUNK, SSM_INNER), lambda c: (rev(c), 0)),
                  pl.BlockSpec((CHUNK, XBC_COLS), lambda c: (rev(c), 0)),
                  pl.BlockSpec((1, SSM_STATE, SSM_INNER), lambda c: (rev(c), 0, 0)),
                  pl.BlockSpec((CHUNK, 128), lambda c: (rev(c), DT_OFF // 128)),
                  pl.BlockSpec((CHUNK, SSM_INNER), lambda c: (rev(c), Z_OFF // SSM_INNER)),
                  colv, colv, vec(SSM_INNER), vec(SSM_INNER), _const_spec((SSM_HEADS, SSM_INNER)), any_spec, any_spec],
        out_specs=[pl.BlockSpec((CHUNK, SSM_INNER), lambda c: (rev(c), Z_OFF // SSM_INNER)),
                   pl.BlockSpec((CHUNK, 128), lambda c: (rev(c), DT_OFF // 128)),
                   pl.BlockSpec((CHUNK, XBC_COLS), lambda c: (rev(c), 0)),
                   vec(SSM_INNER), colv, colv, colv],
        out_shape=[jax.ShapeDtypeStruct(dmain.shape, dmain.dtype), jax.ShapeDtypeStruct(dtail.shape, dtail.dtype),
                   jax.ShapeDtypeStruct((T, XBC_COLS), F32), jax.ShapeDtypeStruct((1, SSM_INNER), F32),
                   jax.ShapeDtypeStruct((SSM_HEADS, 1), F32), jax.ShapeDtypeStruct((SSM_HEADS, 1), F32),
                   jax.ShapeDtypeStruct((SSM_HEADS, 1), F32)],
        scratch_shapes=[pltpu.VMEM((SSM_STATE, SSM_INNER), F32)],
        input_output_aliases={11: 0, 12: 1},
        compiler_params=_cp(("arbitrary",)))(dyo, ypre, xbc, hs, proj_tail, proj_main, dtb_col, alog_col, d_exp, norm_w,
                                             ehead_t, dmain, dtail)


def _rel_bucket(rel):
    n = jnp.maximum(rel, 0)
    max_exact = REL_BUCKETS // 2
    nf = jnp.maximum(n, 1).astype(F32)
    large = max_exact + (jnp.log(nf / max_exact) / math.log(REL_MAX_DIST / max_exact)
                         * (REL_BUCKETS - max_exact)).astype(jnp.int32)
    large = jnp.minimum(large, REL_BUCKETS - 1)
    return jnp.where(n < max_exact, n, large)


def _band_geometry():
    qi = jnp.arange(WINDOW)[:, None] + WINDOW
    kj = jnp.arange(2 * WINDOW)[None, :]
    rel = qi - kj
    return _rel_bucket(rel), (rel >= 0) & (rel < WINDOW)


def _attn_logits(kband, qh, bias_h, first):
    s = _dot(kband, qh, NT) * (HEADDIM ** -0.5) + bias_h
    rowk = lax.broadcasted_iota(jnp.int32, (2 * WINDOW, WINDOW), 0)
    return jnp.where(jnp.logical_and(first, rowk < WINDOW), NEG, s)


def _attn_fwd(proj_main, proj_tail, bias_tbl, sinks):
    T = proj_main.shape[0]
    nb = T // WINDOW

    def body(q_ref, kv_ref, kvp_ref, bias_ref, sink_ref, o_ref, lse_ref):
        i = pl.program_id(0)
        first = i == 0
        outs, lses = [], []
        for kvh in range(ATTN_KV):
            ks = slice(K_OFF + kvh * HEADDIM, K_OFF + (kvh + 1) * HEADDIM)
            vs = slice(V_OFF + kvh * HEADDIM, V_OFF + (kvh + 1) * HEADDIM)
            kband = jnp.concatenate([kvp_ref[:, ks], kv_ref[:, ks]], axis=0).astype(BF16)
            vband = jnp.concatenate([kvp_ref[:, vs], kv_ref[:, vs]], axis=0).astype(BF16)
            heads = range(kvh * ATTN_GROUP, (kvh + 1) * ATTN_GROUP)
            logits = [_attn_logits(kband, q_ref[:, h * HEADDIM:(h + 1) * HEADDIM].astype(BF16), bias_ref[h], first)
                      for h in heads]
            probs = []
            for h, s in zip(heads, logits):
                sink = sink_ref[:, h:h + 1]
                m = jnp.maximum(jnp.max(s, axis=0, keepdims=True), sink)
                p = jnp.exp(s - m)
                den = jnp.sum(p, axis=0, keepdims=True) + jnp.exp(sink - m)
                probs.append((p * (1.0 / den)).astype(BF16))
                lses.append(m + jnp.log(den))
            outs += [_dot(pt, vband, TN) for pt in probs]
        o_ref[...] = jnp.concatenate(outs, axis=1).astype(BF16)
        lse_ref[...] = jnp.concatenate(lses, axis=0)

    return pl.pallas_call(
        body, name="attn_fwd", grid=(nb,),
        in_specs=[pl.BlockSpec((WINDOW, D_MODEL), lambda i: (i, Q_OFF // D_MODEL)),
                  pl.BlockSpec((WINDOW, 256), lambda i: (i, 0)),
                  pl.BlockSpec((WINDOW, 256), lambda i: (jnp.maximum(i - 1, 0), 0)),
                  _const_spec((ATTN_HEADS, 2 * WINDOW, WINDOW)), _const_spec((1, ATTN_HEADS))],
        out_specs=[pl.BlockSpec((WINDOW, D_MODEL), lambda i: (i, 0)),
                   pl.BlockSpec((ATTN_HEADS, WINDOW), lambda i: (0, i))],
        out_shape=[jax.ShapeDtypeStruct((T, D_MODEL), BF16), jax.ShapeDtypeStruct((ATTN_HEADS, T), F32)],
        compiler_params=_cp(("arbitrary",)))(proj_main, proj_tail, proj_tail, bias_tbl, sinks)


def _attn_bwd(dy, lse, proj_main, proj_tail, bias_tbl, sinks, dmain):
    T = proj_main.shape[0]
    nb = T // WINDOW

    def body(dy_ref, lse_ref, q_ref, kv_ref, kvp_ref, bias_ref, sink_ref, dmain_in,
             dq_ref, dkv_ref, dbias_ref, dsink_ref, carry):
        del dmain_in
        i = pl.program_id(0)
        first = i == 0

        @pl.when(first)
        def _():
            carry[...] = jnp.zeros_like(carry)
            dbias_ref[...] = jnp.zeros_like(dbias_ref)
            dsink_ref[...] = jnp.zeros_like(dsink_ref)

        @pl.when(i < nb)
        def _():
            scale = HEADDIM ** -0.5
            dqs, dsinks, dks, dvs = [], [], [], []
            for kvh in range(ATTN_KV):
                ks = slice(K_OFF + kvh * HEADDIM, K_OFF + (kvh + 1) * HEADDIM)
                vs = slice(V_OFF + kvh * HEADDIM, V_OFF + (kvh + 1) * HEADDIM)
                kband = jnp.concatenate([kvp_ref[:, ks], kv_ref[:, ks]], axis=0).astype(BF16)
                vband = jnp.concatenate([kvp_ref[:, vs], kv_ref[:, vs]], axis=0).astype(BF16)
                heads = range(kvh * ATTN_GROUP, (kvh + 1) * ATTN_GROUP)
                qs = [q_ref[:, h * HEADDIM:(h + 1) * HEADDIM].astype(BF16) for h in heads]
                dos = [dy_ref[:, h * HEADDIM:(h + 1) * HEADDIM] for h in heads]
                logits = [_attn_logits(kband, qh, bias_ref[h], first) for h, qh in zip(heads, qs)]
                dps = [_dot(vband, do, NT) for do in dos]
                pbs, dsbs = [], []
                for h, s, dp in zip(heads, logits, dps):
                    lse_h = lse_ref[h:h + 1, :]
                    p = jnp.exp(s - lse_h)
                    delta = jnp.sum(p * dp, axis=0, keepdims=True)
                    ds = p * (dp - delta)
                    psink = jnp.exp(sink_ref[:, h:h + 1] - lse_h)
                    dsinks.append(-jnp.sum(psink * delta, axis=1, keepdims=True))
                    dbias_ref[h] += ds
                    pbs.append(p.astype(BF16))
                    dsbs.append((ds * scale).astype(BF16))
                dqs += [_dot(dsb, kband, TN) for dsb in dsbs]
                dks.append(_dot(jnp.concatenate(dsbs, axis=1), jnp.concatenate(qs, axis=0)))
                dvs.append(_dot(jnp.concatenate(pbs, axis=1), jnp.concatenate(dos, axis=0)))
            dq_ref[...] = jnp.concatenate(dqs, axis=1).astype(dq_ref.dtype)
            dsink_ref[...] += jnp.concatenate(dsinks, axis=1)
            dkv = jnp.concatenate(dks + dvs, axis=1)
            dkv_ref[...] = (carry[...] + dkv[0:WINDOW, :]).astype(dkv_ref.dtype)
            carry[...] = dkv[WINDOW:, :]

        @pl.when(i == nb)
        def _():
            dkv_ref[...] = carry[...].astype(dkv_ref.dtype)

    cur = lambda i: jnp.minimum(i, nb - 1)
    return pl.pallas_call(
        body, name="attn_bwd", grid=(nb + 1,),
        in_specs=[pl.BlockSpec((WINDOW, D_MODEL), lambda i: (cur(i), 0)),
                  pl.BlockSpec((ATTN_HEADS, WINDOW), lambda i: (0, cur(i))),
                  pl.BlockSpec((WINDOW, D_MODEL), lambda i: (cur(i), Q_OFF // D_MODEL)),
                  pl.BlockSpec((WINDOW, 256), lambda i: (cur(i), 0)),
                  pl.BlockSpec((WINDOW, 256), lambda i: (jnp.maximum(cur(i) - 1, 0), 0)),
                  _const_spec((ATTN_HEADS, 2 * WINDOW, WINDOW)), _const_spec((1, ATTN_HEADS)),
                  pl.BlockSpec(memory_space=pl.ANY)],
        out_specs=[pl.BlockSpec((WINDOW, D_MODEL), lambda i: (cur(i), Q_OFF // D_MODEL)),
                   pl.BlockSpec((WINDOW, 256), lambda i: (jnp.maximum(i - 1, 0), 0)),
                   _const_spec((ATTN_HEADS, 2 * WINDOW, WINDOW)), _const_spec((1, ATTN_HEADS))],
        out_shape=[jax.ShapeDtypeStruct(dmain.shape, dmain.dtype), jax.ShapeDtypeStruct((T, TAIL_COLS), BF16),
                   jax.ShapeDtypeStruct((ATTN_HEADS, 2 * WINDOW, WINDOW), F32),
                   jax.ShapeDtypeStruct((1, ATTN_HEADS), F32)],
        scratch_shapes=[pltpu.VMEM((WINDOW, 256), F32)],
        input_output_aliases={7: 0},
        compiler_params=_cp(("arbitrary",)))(dy, lse, proj_main, proj_tail, proj_tail, bias_tbl, sinks, dmain)


def _bias_table(rel_bias_t, onehot_t, mask):
    def body(rb_ref, oh_ref, m_ref, o_ref):
        o_ref[...] = _dot3(rb_ref[...], oh_ref[...]) + m_ref[...]

    flat = pl.pallas_call(body, name="bias_table",
                          out_shape=jax.ShapeDtypeStruct((ATTN_HEADS, 2 * WINDOW * WINDOW), F32))(rel_bias_t, onehot_t, mask)
    return flat.reshape(ATTN_HEADS, 2 * WINDOW, WINDOW)


def _rel_bias_grad(dbias, onehot):
    def body(d_ref, oh_ref, o_ref):
        o_ref[...] = _dot(d_ref[...], oh_ref[...], NN, HIGHEST)

    return pl.pallas_call(body, name="rel_bias_grad",
                          out_shape=jax.ShapeDtypeStruct((ATTN_HEADS, REL_BUCKETS), F32))(dbias, onehot)


def _ln_fwd(r, g, b):
    mu = jnp.mean(r, axis=1, keepdims=True)
    xc = r - mu
    rstd = lax.rsqrt(jnp.mean(xc * xc, axis=1, keepdims=True) + LN_EPS)
    xhat = xc * rstd
    return xhat * g + b, xhat, rstd


def _ln_bwd(dy, xhat, rstd, g):
    dxh = dy * g
    return rstd * (dxh - jnp.mean(dxh, axis=1, keepdims=True) - xhat * jnp.mean(dxh * xhat, axis=1, keepdims=True))


def _merge_fwd(y_ssm, y_attn, proj_main, b_gate, w_bs, w_ba, tm=512):
    T = y_ssm.shape[0]

    def body(ys_ref, ya_ref, gs_ref, ga_ref, bg_ref, wbs_ref, wba_ref, m_ref, bs_ref, ba_ref):
        bs = _dot(ys_ref[...], wbs_ref[...])
        ba = _dot(ya_ref[...], wba_ref[...])
        g_s = _sigmoid(gs_ref[...] + bg_ref[:, 0:D_MODEL])
        g_a = _sigmoid(ga_ref[...] + bg_ref[:, D_MODEL:])
        m_ref[...] = (g_s * bs + g_a * ba).astype(BF16)
        bs_ref[...] = bs
        ba_ref[...] = ba

    row = lambda w, off=0: pl.BlockSpec((tm, w), lambda i: (i, off))
    return pl.pallas_call(
        body, name="merge_fwd", grid=(T // tm,),
        in_specs=[row(SSM_INNER), row(D_MODEL), row(D_MODEL, GATE_OFF // D_MODEL), row(D_MODEL, GATE_OFF // D_MODEL + 1),
                  _const_spec((1, 2 * D_MODEL)), _const_spec((SSM_INNER, D_MODEL)), _const_spec((D_MODEL, D_MODEL))],
        out_specs=[row(D_MODEL), row(D_MODEL), row(D_MODEL)],
        out_shape=[jax.ShapeDtypeStruct((T, D_MODEL), BF16), jax.ShapeDtypeStruct((T, D_MODEL), F32),
                   jax.ShapeDtypeStruct((T, D_MODEL), F32)],
        compiler_params=_cp(("parallel",)))(y_ssm, y_attn, proj_main, proj_main, b_gate, w_bs, w_ba)


def _mix_ln1(merged, w_mo, x, g1, b1, tm=512):
    T = x.shape[0]

    def body(m_ref, w_ref, x_ref, g_ref, b_ref, r_ref, h_ref):
        r = ALPHA * x_ref[...] + _dot(m_ref[...], w_ref[...])
        r_ref[...] = r
        h_ref[...] = _ln_fwd(r, g_ref[...], b_ref[...])[0]

    row = pl.BlockSpec((tm, D_MODEL), lambda i: (i, 0))
    return pl.pallas_call(
        body, name="mix_ln1", grid=(T // tm,),
        in_specs=[row, _const_spec((D_MODEL, D_MODEL)), row, _const_spec((1, D_MODEL)), _const_spec((1, D_MODEL))],
        out_specs=[row, row],
        out_shape=[jax.ShapeDtypeStruct((T, D_MODEL), F32), jax.ShapeDtypeStruct((T, D_MODEL), F32)],
        compiler_params=_cp(("parallel",)))(merged, w_mo, x, g1, b1)


def _glu(u, tr=512, tc=1408):
    T = u.shape[0]
    nj = D_FF // tc

    def body(g_ref, v_ref, o_ref):
        g = g_ref[...]
        o_ref[...] = (g * _sigmoid(g) * v_ref[...]).astype(BF16)

    return pl.pallas_call(
        body, name="glu", grid=(T // tr, nj),
        in_specs=[pl.BlockSpec((tr, tc), lambda i, j: (i, j)), pl.BlockSpec((tr, tc), lambda i, j: (i, nj + j))],
        out_specs=pl.BlockSpec((tr, tc), lambda i, j: (i, j)),
        out_shape=jax.ShapeDtypeStruct((T, D_FF), BF16),
        compiler_params=_cp(("parallel", "parallel")))(u, u)


def _down_ln2_loss(act, w_down, h1, target, g2, b2, tm=512):
    T = h1.shape[0]

    def body(a_ref, w_ref, h_ref, t_ref, g_ref, b_ref, dr_ref, dg_ref, db_ref, l_ref):
        @pl.when(pl.program_id(0) == 0)
        def _():
            dg_ref[...] = jnp.zeros_like(dg_ref)
            db_ref[...] = jnp.zeros_like(db_ref)
            l_ref[...] = jnp.zeros_like(l_ref)

        r = ALPHA * h_ref[...] + _dot(a_ref[...], w_ref[...])
        y, xhat, rstd = _ln_fwd(r, g_ref[...], b_ref[...])
        err = y - t_ref[...]
        l_ref[...] += jnp.sum(err * err, keepdims=True)
        dy = err * (1.0 / D_MODEL)
        dg_ref[...] += jnp.sum(dy * xhat, axis=0, keepdims=True)
        db_ref[...] += jnp.sum(dy, axis=0, keepdims=True)
        dr_ref[...] = _ln_bwd(dy, xhat, rstd, g_ref[...])

    row = pl.BlockSpec((tm, D_MODEL), lambda i: (i, 0))
    vec = _const_spec((1, D_MODEL))
    return pl.pallas_call(
        body, name="down_ln2_loss", grid=(T // tm,),
        in_specs=[pl.BlockSpec((tm, D_FF), lambda i: (i, 0)), _const_spec((D_FF, D_MODEL)), row, row, vec, vec],
        out_specs=[row, vec, vec, _const_spec((1, 1))],
        out_shape=[jax.ShapeDtypeStruct((T, D_MODEL), F32), jax.ShapeDtypeStruct((1, D_MODEL), F32),
                   jax.ShapeDtypeStruct((1, D_MODEL), F32), jax.ShapeDtypeStruct((1, 1), F32)],
        compiler_params=_cp(("arbitrary",)))(act, w_down, h1, target, g2, b2)


def _ffn_bwd_act(dr2, w_down, u, tm=512, tn=1408):
    T = dr2.shape[0]
    nj = D_FF // tn

    def body(d_ref, w_ref, g_ref, v_ref, o_ref, dact):
        half = pl.program_id(2)

        @pl.when(half == 0)
        def _():
            dact[...] = _dot(d_ref[...].astype(BF16), w_ref[...], NT)
            g = g_ref[...]
            sg = _sigmoid(g)
            o_ref[...] = dact[...] * v_ref[...] * (sg * (1.0 + g * (1.0 - sg)))

        @pl.when(half == 1)
        def _():
            g = g_ref[...]
            o_ref[...] = dact[...] * (g * _sigmoid(g))

    return pl.pallas_call(
        body, name="ffn_bwd_act", grid=(T // tm, nj, 2),
        in_specs=[pl.BlockSpec((tm, D_MODEL), lambda i, j, h: (i, 0)),
                  pl.BlockSpec((tn, D_MODEL), lambda i, j, h: (j, 0)),
                  pl.BlockSpec((tm, tn), lambda i, j, h: (i, j)),
                  pl.BlockSpec((tm, tn), lambda i, j, h: (i, nj + j))],
        out_specs=pl.BlockSpec((tm, tn), lambda i, j, h: (i, h * nj + j)),
        out_shape=jax.ShapeDtypeStruct((T, 2 * D_FF), F32),
        scratch_shapes=[pltpu.VMEM((tm, tn), F32)],
        compiler_params=_cp(("parallel", "arbitrary", "arbitrary")))(dr2, w_down, u, u)


def _ffn_bwd_in(du_pre, w_up, dr2, r1, g1, b1, tm=1024, tk=1408):
    T = dr2.shape[0]
    tm = min(tm, T)
    assert T % tm == 0
    nk = 2 * D_FF // tk

    def body(d_ref, w_ref, dr2_ref, r_ref, g_ref, b_ref, dr1_ref, dg_ref, db_ref, acc):
        i, k = pl.program_id(0), pl.program_id(1)

        @pl.when(jnp.logical_and(i == 0, k == 0))
        def _():
            dg_ref[...] = jnp.zeros_like(dg_ref)
            db_ref[...] = jnp.zeros_like(db_ref)

        @pl.when(k == 0)
        def _():
            acc[...] = ALPHA * dr2_ref[...]

        acc[...] += _dot(d_ref[...], w_ref[...], NT)

        @pl.when(k == nk - 1)
        def _():
            _, xhat, rstd = _ln_fwd(r_ref[...], g_ref[...], b_ref[...])
            dy = acc[...]
            dg_ref[...] += jnp.sum(dy * xhat, axis=0, keepdims=True)
            db_ref[...] += jnp.sum(dy, axis=0, keepdims=True)
            dr1_ref[...] = _ln_bwd(dy, xhat, rstd, g_ref[...])

    row = pl.BlockSpec((tm, D_MODEL), lambda i, k: (i, 0))
    vec = _const_spec((1, D_MODEL))
    return pl.pallas_call(
        body, name="ffn_bwd_in", grid=(T // tm, nk),
        in_specs=[pl.BlockSpec((tm, tk), lambda i, k: (i, k)), pl.BlockSpec((D_MODEL, tk), lambda i, k: (0, k)),
                  row, row, vec, vec],
        out_specs=[row, vec, vec],
        out_shape=[jax.ShapeDtypeStruct((T, D_MODEL), F32), jax.ShapeDtypeStruct((1, D_MODEL), F32),
                   jax.ShapeDtypeStruct((1, D_MODEL), F32)],
        scratch_shapes=[pltpu.VMEM((tm, D_MODEL), F32)],
        compiler_params=_cp(("arbitrary", "arbitrary")))(du_pre, w_up, dr2, r1, g1, b1)


def _mix_bwd(dr1, w_mo, w_bs, w_ba, bs, ba, proj_main, b_gate, tm=512):
    T = dr1.shape[0]

    def body(d_ref, wmo_ref, wbs_ref, wba_ref, bs_ref, ba_ref, gs_ref, ga_ref, bg_ref,
             dg_ref, dbs_ref, dba_ref, dys_ref, dya_ref, dbg_ref):
        @pl.when(pl.program_id(0) == 0)
        def _():
            dbg_ref[...] = jnp.zeros_like(dbg_ref)

        dm = _dot(d_ref[...].astype(BF16), wmo_ref[...], NT)
        g_s = _sigmoid(gs_ref[...] + bg_ref[:, 0:D_MODEL])
        g_a = _sigmoid(ga_ref[...] + bg_ref[:, D_MODEL:])
        dgs = dm * bs_ref[...] * g_s * (1.0 - g_s)
        dga = dm * ba_ref[...] * g_a * (1.0 - g_a)
        dg_ref[:, 0:D_MODEL] = dgs.astype(BF16)
        dg_ref[:, D_MODEL:] = dga.astype(BF16)
        dbg_ref[:, 0:D_MODEL] += jnp.sum(dgs, axis=0, keepdims=True)
        dbg_ref[:, D_MODEL:] += jnp.sum(dga, axis=0, keepdims=True)
        dbs = (dm * g_s).astype(BF16)
        dba = (dm * g_a).astype(BF16)
        dbs_ref[...] = dbs
        dba_ref[...] = dba
        dys_ref[...] = _dot(dbs, wbs_ref[...], NT)
        dya_ref[...] = _dot(dba, wba_ref[...], NT).astype(BF16)

    row = lambda w, off=0: pl.BlockSpec((tm, w), lambda i: (i, off))
    return pl.pallas_call(
        body, name="mix_bwd", grid=(T // tm,),
        in_specs=[row(D_MODEL), _const_spec((D_MODEL, D_MODEL)), _const_spec((SSM_INNER, D_MODEL)),
                  _const_spec((D_MODEL, D_MODEL)), row(D_MODEL), row(D_MODEL),
                  row(D_MODEL, GATE_OFF // D_MODEL), row(D_MODEL, GATE_OFF // D_MODEL + 1), _const_spec((1, 2 * D_MODEL))],
        out_specs=[row(2 * D_MODEL, GATE_OFF // (2 * D_MODEL)), row(D_MODEL), row(D_MODEL), row(SSM_INNER), row(D_MODEL),
                   _const_spec((1, 2 * D_MODEL))],
        out_shape=[jax.ShapeDtypeStruct((T, MAIN_COLS), BF16), jax.ShapeDtypeStruct((T, D_MODEL), BF16),
                   jax.ShapeDtypeStruct((T, D_MODEL), BF16), jax.ShapeDtypeStruct((T, SSM_INNER), F32),
                   jax.ShapeDtypeStruct((T, D_MODEL), BF16), jax.ShapeDtypeStruct((1, 2 * D_MODEL), F32)],
        compiler_params=_cp(("arbitrary",)))(dr1, w_mo, w_bs, w_ba, bs, ba, proj_main, proj_main, b_gate)


def _local_step(x, target, w, p, late_weights=None):
    xb = x.astype(BF16)
    if late_weights is None:
        proj_main = _matmul(xb, w["in_main"], "nn", F32, "in_proj_main", tm=1024, tn=2048)
    else:
        proj_main, landed = _matmul(xb, w["in_main"], "nn", F32, "in_proj_main", tm=1024, tn=2048,
                                    comm=("gather", late_weights[0]))
        w = {**w, **late_weights[1](landed)}
    proj_tail = _matmul(xb, w["in_tail"], "nn", F32, "in_proj_tail", tn=TAIL_COLS)
    xbc = _conv_fwd(proj_main, XBC_OFF, XBC_COLS, p["ssm_conv_w"], p["ssm_conv_b"], SSM_CONV, "silu", "ssm_conv_fwd")
    y_ssm, ypre, hs = _ssd_fwd(xbc, proj_main, proj_tail, p["dtb_col"], p["alog_col"], p["d_exp"], p["ssm_norm_w"])
    y_attn, lse = _attn_fwd(proj_main, proj_tail, p["bias_tbl"], p["attn_sinks"])
    merged, bs, ba = _merge_fwd(y_ssm, y_attn, proj_main, p["b_gate"], w["bs"], w["ba"])
    r1, h1 = _mix_ln1(merged, w["mo"], x, p["ln1_g"], p["ln1_b"])
    u_pre = _matmul(h1, w["up"], "nn", F32, "ffn_up", tm=1024, tn=1408)
    u = _conv_fwd(u_pre, 0, 2 * D_FF, p["ffn_conv_w"], p["ffn_conv_b"], FFN_CONV, None, "ffn_conv_fwd")
    act = _glu(u)
    dr2, dg2, db2, sq = _down_ln2_loss(act, w["down"], h1, target, p["ln2_g"], p["ln2_b"])
    g = {"ln2_g": dg2, "ln2_b": db2}
    g["w_down"] = _matmul(act, dr2, "tn", F32, "dw_down", tm=1408, tn=1024, tk=512)
    du = _ffn_bwd_act(dr2, w["down"], u)
    du_pre, g["ffn_conv_w"], g["ffn_conv_b"] = _conv_bwd(
        du, u_pre, 0, 2 * D_FF, p["ffn_conv_w"], p["ffn_conv_b"], FFN_CONV, None,
        jax.ShapeDtypeStruct((x.shape[0], 2 * D_FF), BF16), 0, "ffn_conv_bwd")
    g["w_up"] = _matmul(h1, du_pre, "tn", F32, "dw_up", tm=1024, tn=1408, tk=512)
    dr1, g["ln1_g"], g["ln1_b"] = _ffn_bwd_in(du_pre, w["up"], dr2, r1, p["ln1_g"], p["ln1_b"])
    g["w_mix_out"] = _matmul(merged, dr1, "tn", F32, "dw_mix_out", tm=1024, tn=1024, tk=512)
    dmain, dbs, dba, dy_ssm, dy_attn, g["b_gate"] = _mix_bwd(dr1, w["mo"], w["bs"], w["ba"], bs, ba, proj_main, p["b_gate"])
    g["w_branch_ssm"] = _matmul(y_ssm, dbs, "tn", F32, "dw_branch_ssm", tm=1024, tn=1024, tk=512)
    g["w_branch_attn"] = _matmul(y_attn, dba, "tn", F32, "dw_branch_attn", tm=1024, tn=1024, tk=512)
    dmain, dtail, dbias, g["attn_sinks"] = _attn_bwd(dy_attn, lse, proj_main, proj_tail, p["bias_tbl"], p["attn_sinks"], dmain)
    g["rel_bias"] = _rel_bias_grad(dbias.reshape(ATTN_HEADS, WINDOW * 2 * WINDOW), p["bucket_onehot"]).T
    dmain, dtail, dxbc, g["ssm_norm_w"], dd, dalog, ddtb = _ssd_bwd(
        dy_ssm, ypre, xbc, hs, proj_main, proj_tail, p["dtb_col"], p["alog_col"], p["d_exp"], p["ssm_norm_w"],
        p["ehead_t"], dmain, dtail)
    g["ssm_d"], g["ssm_a_log"], g["ssm_dt_bias"] = (a.reshape(1, SSM_HEADS) for a in (dd, dalog, ddtb))
    dmain, g["ssm_conv_w"], g["ssm_conv_b"] = _conv_bwd(
        dxbc, proj_main, XBC_OFF, XBC_COLS, p["ssm_conv_w"], p["ssm_conv_b"], SSM_CONV, "silu", dmain, XBC_OFF,
        "ssm_conv_bwd")
    g["in_main"] = _matmul(xb, dmain, "tn", F32, "dw_in_main", tm=1024, tn=1024, tk=512)
    g["in_tail"] = _matmul(xb, dtail, "tn", F32, "dw_in_tail", tm=1024, tn=TAIL_COLS, tk=512)
    return sq, (dmain, dtail, dr1), w, g


def _grad_x(dproj, w, exchange=None):
    dmain, dtail, dr1 = dproj
    dx = _matmul(dtail, w["in_tail"], "nt", F32, "dx_tail", tk=TAIL_COLS, addend=dr1, addend_scale=ALPHA)
    return _matmul(dmain, w["in_main"], "nt", F32, "dx_main", tm=1024, addend=dx,
                   comm=None if exchange is None else ("exchange", exchange))


def _split_w_in(w):
    seg = lambda off, n: w[:, off:off + n]
    main = jnp.concatenate([seg(O_Z, 2048), seg(O_XBC, XBC_COLS), seg(O_Q, D_MODEL), seg(O_GATE, 2 * D_MODEL)], axis=1)
    tail = jnp.concatenate([seg(O_K, 128), seg(O_V, 128), seg(O_DT, SSM_HEADS),
                            jnp.zeros((w.shape[0], 128 - SSM_HEADS), w.dtype)], axis=1)
    return main, tail


def _join_w_in(main, tail):
    return jnp.concatenate([main[:, Z_OFF:Z_OFF + 2048], main[:, XBC_OFF:XBC_OFF + XBC_COLS],
                            tail[:, DT_OFF:DT_OFF + SSM_HEADS], main[:, Q_OFF:Q_OFF + D_MODEL],
                            tail[:, K_OFF:K_OFF + 128], tail[:, V_OFF:V_OFF + 128],
                            main[:, GATE_OFF:GATE_OFF + 2 * D_MODEL]], axis=1)


def _prep_params(rel_bias, b_gate, ssm_conv_w, ssm_conv_b, ssm_dt_bias, ssm_a_log, ssm_d, ssm_norm_w, attn_sinks,
                 ln1_g, ln1_b, ffn_conv_w, ffn_conv_b, ln2_g, ln2_b):
    bucket, in_window = _band_geometry()
    bucket, in_window = bucket.T, in_window.T
    onehot = jnp.logical_and(bucket.reshape(-1, 1) == jnp.arange(REL_BUCKETS)[None, :],
                             in_window.reshape(-1, 1)).astype(F32)
    onehot_t = jnp.logical_and(bucket.reshape(1, -1) == jnp.arange(REL_BUCKETS)[:, None],
                               in_window.reshape(1, -1)).astype(BF16)
    bias_tbl = _bias_table(rel_bias.T, onehot_t, jnp.where(in_window.reshape(1, -1), 0.0, NEG))
    ehead_t = (jnp.arange(SSM_INNER)[None, :] // SSMD == jnp.arange(SSM_HEADS)[:, None]).astype(F32)
    return {"bias_tbl": bias_tbl, "bucket_onehot": onehot, "b_gate": b_gate, "ssm_conv_w": ssm_conv_w,
            "ssm_conv_b": ssm_conv_b, "dtb_col": ssm_dt_bias.reshape(SSM_HEADS, 1),
            "alog_col": ssm_a_log.reshape(SSM_HEADS, 1), "ehead_t": ehead_t,
            "d_exp": jnp.repeat(ssm_d, SSMD, axis=1), "ssm_norm_w": ssm_norm_w, "attn_sinks": attn_sinks,
            "ln1_g": ln1_g, "ln1_b": ln1_b, "ffn_conv_w": ffn_conv_w, "ffn_conv_b": ffn_conv_b,
            "ln2_g": ln2_g, "ln2_b": ln2_b}


def _all_gather(shard, name):
    R, C = shard.shape

    def body(x_ref, out_ref, send_sems, recv_sems, local_sem):
        for phase in _gather_phases(x_ref, out_ref, send_sems, recv_sems, local_sem):
            phase()

    return pl.pallas_call(
        body, name=name, out_shape=jax.ShapeDtypeStruct((N_DEV, R, C), shard.dtype),
        in_specs=[pl.BlockSpec(memory_space=pl.ANY)], out_specs=pl.BlockSpec(memory_space=pl.ANY),
        scratch_shapes=[pltpu.SemaphoreType.DMA((7,)), pltpu.SemaphoreType.DMA((7,)), pltpu.SemaphoreType.DMA],
    )(shard)


def _slot_sum(slots, tr=512):
    _, R, C = slots.shape
    assert R % tr == 0

    def body(s_ref, o_ref):
        acc = s_ref[0].astype(F32)
        for i in range(1, N_DEV):
            acc = acc + s_ref[i].astype(F32)
        o_ref[...] = acc

    return pl.pallas_call(
        body, name="slot_sum", grid=(R // tr,),
        in_specs=[pl.BlockSpec((N_DEV, tr, C), lambda i: (0, i, 0))], out_specs=pl.BlockSpec((tr, C), lambda i: (i, 0)),
        out_shape=jax.ShapeDtypeStruct((R, C), F32), compiler_params=_cp(("parallel",)))(slots)


def _adamw_math(w, g, m, v):
    m = ADAM_B1 * m + (1.0 - ADAM_B1) * g
    v = ADAM_B2 * v + (1.0 - ADAM_B2) * (g * g)
    m_hat = m / (1.0 - ADAM_B1 ** ADAM_STEP)
    v_hat = v / (1.0 - ADAM_B2 ** ADAM_STEP)
    return -ADAM_LR * (m_hat / (jnp.sqrt(v_hat) + ADAM_EPS) + ADAM_WD * w), m, v


def _adamw(w, g, m, v, name):
    R, C = w.shape
    tr = 256 if R % 256 == 0 and R > 256 else R

    def body(w_ref, g_ref, m_ref, v_ref, d_ref, nm_ref, nv_ref):
        d_ref[...], nm_ref[...], nv_ref[...] = _adamw_math(w_ref[...], g_ref[...], m_ref[...], v_ref[...])

    spec = pl.BlockSpec((tr, C), lambda i: (i, 0))
    return pl.pallas_call(
        body, name=name, grid=(R // tr,), in_specs=[spec] * 4, out_specs=[spec] * 3,
        out_shape=[jax.ShapeDtypeStruct((R, C), F32)] * 3, compiler_params=_cp(("parallel",)))(w, g, m, v)


def _small_update(gathered, w, m, v):
    def body(s_ref, w_ref, m_ref, v_ref, g_ref, d_ref, nm_ref, nv_ref):
        g = s_ref[0]
        for i in range(1, N_DEV):
            g = g + s_ref[i]
        g_ref[...] = g
        d_ref[...], nm_ref[...], nv_ref[...] = _adamw_math(w_ref[...], g, m_ref[...], v_ref[...])

    return pl.pallas_call(body, name="small_update", out_shape=[jax.ShapeDtypeStruct(w.shape, F32)] * 4)(gathered, w, m, v)


LANES = 128
BIG = (("w_in", 8480, "cols"), ("w_branch_ssm", 2048, "rows"), ("w_branch_attn", 1024, "rows"),
       ("w_mix_out", 1024, "rows"), ("w_up", 5632, "cols"), ("w_down", 2816, "rows"))
CONVW = (("ssm_conv_w", 16), ("ffn_conv_w", 24))
SMALL = ("rel_bias", "b_gate", "ssm_conv_b", "ssm_dt_bias", "ssm_a_log", "ssm_d", "ssm_norm_w", "attn_sinks",
         "ln1_g", "ln1_b", "ffn_conv_b", "ln2_g", "ln2_b")
WEIGHTS = ("rel_bias", "w_in", "b_gate", "ssm_conv_w", "ssm_conv_b", "ssm_dt_bias", "ssm_a_log", "ssm_d", "ssm_norm_w",
           "attn_sinks", "w_branch_ssm", "w_branch_attn", "w_mix_out", "ln1_g", "ln1_b", "w_up", "ffn_conv_w",
           "ffn_conv_b", "w_down", "ln2_g", "ln2_b")
EXCHANGE_ROWS = 21504
SMALL_ROWS = 144


def _rows(a, rows):
    flat = a.reshape(-1)
    return jnp.pad(flat, (0, rows * LANES - flat.shape[0])).reshape(rows, LANES)


def _rows8(a, rows):
    return jnp.pad(a, ((0, 0), (0, rows * LANES - a.shape[1]))).reshape(N_DEV, rows, LANES)


def _by_device(full, how):
    r, c = full.shape
    if how == "rows":
        return full.reshape(N_DEV, -1)
    return full.reshape(r, N_DEV, c // N_DEV).transpose(1, 0, 2).reshape(N_DEV, -1)


def _from_devices(slots, r, c, how):
    if how == "rows":
        return slots.reshape(r, c)
    return slots.reshape(N_DEV, r, c // N_DEV).transpose(1, 0, 2).reshape(r, c)


def kernel(x, rel_bias, w_in, b_gate, ssm_conv_w, ssm_conv_b, ssm_dt_bias, ssm_a_log, ssm_d, ssm_norm_w, attn_sinks, w_branch_ssm, w_branch_attn, w_mix_out, ln1_g, ln1_b, w_up, ffn_conv_w, ffn_conv_b, w_down, ln2_g, ln2_b, loss_target, m_rel_bias, m_w_in, m_b_gate, m_ssm_conv_w, m_ssm_conv_b, m_ssm_dt_bias, m_ssm_a_log, m_ssm_d, m_ssm_norm_w, m_attn_sinks, m_w_branch_ssm, m_w_branch_attn, m_w_mix_out, m_ln1_g, m_ln1_b, m_w_up, m_ffn_conv_w, m_ffn_conv_b, m_w_down, m_ln2_g, m_ln2_b, v_rel_bias, v_w_in, v_b_gate, v_ssm_conv_w, v_ssm_conv_b, v_ssm_dt_bias, v_ssm_a_log, v_ssm_d, v_ssm_norm_w, v_attn_sinks, v_w_branch_ssm, v_w_branch_attn, v_w_mix_out, v_ln1_g, v_ln1_b, v_w_up, v_ffn_conv_w, v_ffn_conv_b, v_w_down, v_ln2_g, v_ln2_b):
    W = dict(zip(WEIGHTS, (rel_bias, w_in, b_gate, ssm_conv_w, ssm_conv_b, ssm_dt_bias, ssm_a_log, ssm_d, ssm_norm_w,
                           attn_sinks, w_branch_ssm, w_branch_attn, w_mix_out, ln1_g, ln1_b, w_up, ffn_conv_w,
                           ffn_conv_b, w_down, ln2_g, ln2_b)))
    M = dict(zip(WEIGHTS, (m_rel_bias, m_w_in, m_b_gate, m_ssm_conv_w, m_ssm_conv_b, m_ssm_dt_bias, m_ssm_a_log, m_ssm_d,
                           m_ssm_norm_w, m_attn_sinks, m_w_branch_ssm, m_w_branch_attn, m_w_mix_out, m_ln1_g, m_ln1_b,
                           m_w_up, m_ffn_conv_w, m_ffn_conv_b, m_w_down, m_ln2_g, m_ln2_b)))
    V = dict(zip(WEIGHTS, (v_rel_bias, v_w_in, v_b_gate, v_ssm_conv_w, v_ssm_conv_b, v_ssm_dt_bias, v_ssm_a_log, v_ssm_d,
                           v_ssm_norm_w, v_attn_sinks, v_w_branch_ssm, v_w_branch_attn, v_w_mix_out, v_ln1_g, v_ln1_b,
                           v_w_up, v_ffn_conv_w, v_ffn_conv_b, v_w_down, v_ln2_g, v_ln2_b)))
    shard2d = lambda a: a.reshape(a.shape[-2], a.shape[-1])

    def unpack(landed, entries):
        full, off = {}, 0
        for n, rows, how in entries:
            r, c = shard2d(W[n]).shape
            fr, fc = (r * N_DEV, c) if how == "rows" else (r, c * N_DEV)
            full[n] = _from_devices(landed[:, off:off + rows].reshape(N_DEV, -1)[:, :r * c], fr, fc, how)
            off += rows
        return full

    conv_entries = tuple((n, rows, "cols") for n, rows in CONVW)
    win_shard = _rows(shard2d(w_in).astype(BF16), BIG[0][1])
    late_shard = jnp.concatenate([_rows(shard2d(W[n]).astype(BF16), rows) for n, rows, _ in BIG[1:]], axis=0)
    conv_shard = jnp.concatenate([_rows(shard2d(W[n]), rows) for n, rows in CONVW], axis=0)
    main, tail = _split_w_in(unpack(_all_gather(win_shard, "gather_w_in"), BIG[:1])["w_in"])
    conv_full = unpack(_all_gather(conv_shard, "gather_conv_weights"), conv_entries)
    short = {"w_branch_ssm": "bs", "w_branch_attn": "ba", "w_mix_out": "mo", "w_up": "up", "w_down": "down"}
    late = lambda landed: {short[n]: v for n, v in unpack(landed, BIG[1:]).items()}
    p = _prep_params(rel_bias, b_gate, conv_full["ssm_conv_w"], ssm_conv_b, ssm_dt_bias, ssm_a_log, ssm_d, ssm_norm_w,
                     attn_sinks, ln1_g, ln1_b, conv_full["ffn_conv_w"], ffn_conv_b, ln2_g, ln2_b)
    sq, dproj, w, g = _local_step(x[0], loss_target[0], {"in_main": main, "in_tail": tail}, p, (late_shard, late))
    g["w_in"] = _join_w_in(g.pop("in_main"), g.pop("in_tail"))
    loss = (0.5 / D_MODEL) * lax.psum(sq[0, 0], ("x", "y", "c"))

    parts = [_rows8(_by_device(g[n], how), rows) for n, rows, how in BIG]
    parts += [_rows8(_by_device(g[n], "cols"), rows) for n, rows in CONVW]
    used = sum(rows for _, rows, _ in BIG) + sum(rows for _, rows in CONVW)
    parts.append(jnp.zeros((N_DEV, EXCHANGE_ROWS - used, LANES), F32))
    dx, received = _grad_x(dproj, w, exchange=jnp.concatenate(parts, axis=1).astype(BF16))
    g_rows = _slot_sum(received)
    grads, deltas, new_m, new_v = {}, {}, {}, {}
    off = 0
    for n, rows in [(n, rows) for n, rows, _ in BIG] + list(CONVW):
        r, c = shard2d(W[n]).shape
        gn = g_rows[off:off + rows].reshape(-1)[:r * c].reshape(r, c)
        off += rows
        d, nm, nv = _adamw(shard2d(W[n]), gn, shard2d(M[n]), shard2d(V[n]), "adamw_" + n)
        grads[n], deltas[n], new_m[n], new_v[n] = (a.reshape(W[n].shape) for a in (gn, d, nm, nv))

    pack = lambda src: _rows(jnp.concatenate([src[n].reshape(-1) for n in SMALL]), SMALL_ROWS)
    small_all = _all_gather(pack(g), "gather_small_grads")
    outs = _small_update(small_all, pack(W), pack(M), pack(V))
    off = 0
    for n in SMALL:
        size = W[n].size
        grads[n], deltas[n], new_m[n], new_v[n] = (a.reshape(-1)[off:off + size].reshape(W[n].shape) for a in outs)
        off += size

    return (loss, dx[None], *[grads[n] for n in WEIGHTS], *[deltas[n] for n in WEIGHTS],
            *[new_m[n] for n in WEIGHTS], *[new_v[n] for n in WEIGHTS])
```

```python
import functools
import math

import jax
import jax.numpy as jnp
from jax import lax
from jax.experimental import pallas as pl
from jax.experimental.pallas import tpu as pltpu

F32, BF16 = jnp.float32, jnp.bfloat16
HIGHEST = lax.Precision.HIGHEST
MESH_ID = pl.DeviceIdType.MESH

N_DEV = 8
D_MODEL = 1024
SSM_INNER = 2048
SSM_HEADS = 32
SSM_HEADDIM = 64
SSMD = SSM_HEADDIM
SSM_GROUPS = 4
SSM_GROUP_COLS = SSM_INNER // SSM_GROUPS
SSM_STATE = 128
SSM_CONV = 4
CHUNK = 128
XBC_COLS = SSM_INNER + 2 * SSM_GROUPS * SSM_STATE
B_OFF = SSM_INNER
C_OFF = SSM_INNER + SSM_GROUPS * SSM_STATE
ATTN_HEADS = 16
ATTN_KV = 2
ATTN_GROUP = 8
HEADDIM = 64
WINDOW = 128
REL_BUCKETS = 32
REL_MAX_DIST = 128
D_FF = 2816
FFN_CONV = 3
ALPHA = 2.0 ** 0.25
LN_EPS = 1e-5
RMS_EPS = 1e-5
IN_COLS = 8480
Z_OFF, XBC_OFF, Q_OFF, GATE_OFF, MAIN_COLS = 0, 2048, 5120, 6144, 8192
K_OFF, V_OFF, DT_OFF, TAIL_COLS = 0, 128, 256, 384
O_Z, O_XBC, O_DT, O_Q, O_K, O_V, O_GATE = 0, 2048, 5120, 5152, 6176, 6304, 6432

ADAM_LR, ADAM_B1, ADAM_B2, ADAM_EPS, ADAM_WD, ADAM_STEP = 0.001, 0.9, 0.999, 1e-08, 0.01, 10
NEG = -1e30
HALO = 8
VMEM_LIMIT = 56 * 1024 * 1024


def _cp(sem):
    return pltpu.CompilerParams(dimension_semantics=sem, vmem_limit_bytes=VMEM_LIMIT)


def _const_spec(shape):
    nd = len(shape)
    return pl.BlockSpec(shape, lambda *_: (0,) * nd)


def _sigmoid(x):
    return 0.5 * jnp.tanh(0.5 * x) + 0.5


def _softplus(x):
    return jnp.maximum(x, 0.0) + jnp.log1p(jnp.exp(-jnp.abs(x)))


def _dot(a, b, dims=(((1,), (0,)), ((), ())), precision=None):
    return lax.dot_general(a, b, dims, preferred_element_type=F32, precision=precision)


NN = (((1,), (0,)), ((), ()))
NT = (((1,), (1,)), ((), ()))
TN = (((0,), (0,)), ((), ()))


def _mesh_pos():
    return lax.axis_index("x"), lax.axis_index("y"), lax.axis_index("c")


def _exchange_phases(in_ref, out_ref, send_sems, recv_sems, local_sem):
    def copies():
        x, y, c = _mesh_pos()
        me = 4 * x + 2 * y + c
        cps = [pltpu.make_async_copy(in_ref.at[me], out_ref.at[me], local_sem)]
        for r in range(1, N_DEV):
            px = 1 - x if r & 4 else x
            py = 1 - y if r & 2 else y
            pc = 1 - c if r & 1 else c
            cps.append(pltpu.make_async_remote_copy(
                src_ref=in_ref.at[4 * px + 2 * py + pc], dst_ref=out_ref.at[me],
                send_sem=send_sems.at[r - 1], recv_sem=recv_sems.at[r - 1],
                device_id=(px, py, pc), device_id_type=MESH_ID))
        return cps

    def start():
        for cp in copies():
            cp.start()

    def finish():
        for cp in copies():
            cp.wait()

    return [start, finish]


def _gather_phases(x_ref, out_ref, send_sems, recv_sems, local_sem):
    def parts(which):
        x, y, c = _mesh_pos()
        me, sibling = (x, y, c), (x, y, 1 - c)
        chips = [(1 - x, y), (x, 1 - y), (1 - x, 1 - y)]

        def slot(px, py, pc):
            return out_ref.at[4 * px + 2 * py + pc]

        def copy(k, block, to, src=None):
            return pltpu.make_async_remote_copy(
                src_ref=slot(*block) if src is None else src, dst_ref=slot(*block),
                send_sem=send_sems.at[k], recv_sem=recv_sems.at[k], device_id=to, device_id_type=MESH_ID)

        if which == "mine":
            return pltpu.make_async_copy(x_ref, slot(*me), local_sem)
        if which == "first":
            return [copy(0, me, sibling, src=x_ref)] + [copy(1 + j, me, (*chip, c), src=x_ref) for j, chip in enumerate(chips)]
        if which == "passed":
            return [copy(4 + j, (*chip, c), sibling) for j, chip in enumerate(chips)]
        if which == "arrived":
            return [copy(1 + j, (*chip, c), me) for j, chip in enumerate(chips)]
        return [copy(0, sibling, me)] + [copy(4 + j, (*chip, 1 - c), me) for j, chip in enumerate(chips)]

    def start():
        parts("mine").start()
        for cp in parts("first"):
            cp.start()

    def forward():
        for a, p in zip(parts("arrived"), parts("passed")):
            a.wait_recv()
            p.start()

    def finish():
        for cp in parts("late"):
            cp.wait_recv()
        for cp in parts("first") + parts("passed"):
            cp.wait_send()
        parts("mine").wait()

    return [start, forward, finish]


COMM = {"exchange": _exchange_phases, "gather": _gather_phases}


def _matmul(a, b, mode, out_dtype, name, tm=512, tn=1024, tk=1024, addend=None, addend_scale=1.0, comm=None):
    exchange = None if comm is None else comm[1]
    if mode == "nn":
        (M, K), (K2, N) = a.shape, b.shape
    elif mode == "nt":
        (M, K), (N, K2) = a.shape, b.shape
    else:
        (K, M), (K2, N) = a.shape, b.shape
    assert K == K2, (a.shape, b.shape, mode)
    tm, tn, tk = min(tm, M), min(tn, N), min(tk, K)
    assert M % tm == 0 and N % tn == 0 and K % tk == 0, (M, N, K, tm, tn, tk)
    nk = K // tk
    dims = {"nn": NN, "nt": NT, "tn": TN}[mode]
    a_spec = pl.BlockSpec((tk, tm), lambda i, j, k: (k, i)) if mode == "tn" else pl.BlockSpec((tm, tk), lambda i, j, k: (i, k))
    b_spec = pl.BlockSpec((tn, tk), lambda i, j, k: (j, k)) if mode == "nt" else pl.BlockSpec((tk, tn), lambda i, j, k: (k, j))
    o_spec = pl.BlockSpec((tm, tn), lambda i, j, k: (i, j))

    ni, nj = M // tm, N // tn

    def body(*refs):
        refs = list(refs)
        a_ref, b_ref = refs[:2]
        c_ref = refs[2] if addend is not None else None
        n_in = 2 + (addend is not None) + (exchange is not None)
        o_ref, acc = refs[n_in], refs[n_in + 1 + (exchange is not None)]
        i, j, k = pl.program_id(0), pl.program_id(1), pl.program_id(2)
        step = (i * nj + j) * nk + k
        if comm is not None:
            phases = COMM[comm[0]](refs[n_in - 1], refs[n_in + 1], *refs[n_in + 3:])
            at = [(ni * nj * nk - 1) * p // (len(phases) - 1) for p in range(len(phases))]
            for when, phase in zip(at[:-1], phases[:-1]):
                pl.when(step == when)(phase)

        d = _dot(a_ref[...].astype(BF16), b_ref[...].astype(BF16), dims)

        def finish(r):
            if addend is not None:
                r = r + addend_scale * c_ref[...].astype(F32)
            o_ref[...] = r.astype(out_dtype)

        if nk == 1:
            finish(d)
        else:
            @pl.when(k == 0)
            def _():
                acc[...] = d

            @pl.when(jnp.logical_and(k > 0, k < nk - 1))
            def _():
                acc[...] += d

            @pl.when(k == nk - 1)
            def _():
                finish(acc[...] + d)

        if comm is not None:
            pl.when(step == at[-1])(phases[-1])

    in_specs = [a_spec, b_spec] + ([o_spec] if addend is not None else [])
    args = (a, b) + ((addend,) if addend is not None else ())
    out_specs, out_shape = o_spec, jax.ShapeDtypeStruct((M, N), out_dtype)
    scratch = [pltpu.VMEM((tm, tn), F32)]
    sem = ("parallel", "parallel", "arbitrary")
    if exchange is not None:
        any_spec = pl.BlockSpec(memory_space=pl.ANY)
        in_specs, args = in_specs + [any_spec], args + (exchange,)
        landed = exchange.shape if comm[0] == "exchange" else (N_DEV,) + exchange.shape
        out_specs, out_shape = [o_spec, any_spec], [out_shape, jax.ShapeDtypeStruct(landed, exchange.dtype)]
        scratch += [pltpu.SemaphoreType.DMA((N_DEV - 1,)), pltpu.SemaphoreType.DMA((N_DEV - 1,)), pltpu.SemaphoreType.DMA]
        sem = ("arbitrary", "arbitrary", "arbitrary")
    return pl.pallas_call(
        body, name=name, grid=(ni, nj, nk), in_specs=in_specs, out_specs=out_specs, out_shape=out_shape,
        scratch_shapes=scratch, compiler_params=_cp(sem))(*args)


def _taps(w_ref, K, tc):
    return [jnp.broadcast_to(w_ref[k:k + 1, :], (HALO, tc)) for k in range(K)]


def _conv_fwd(pre, pre_col_off, C, w, b, K, act, name, tr=512, tc=512):
    T = pre.shape[0]
    tc = min(tc, C)
    assert T % tr == 0 and C % tc == 0 and pre_col_off % tc == 0
    joff = pre_col_off // tc
    hb = tr // HALO

    def body(x_ref, xp_ref, w_ref, b_ref, o_ref, head):
        i = pl.program_id(1)
        head[0:HALO, :] = jnp.where(i > 0, xp_ref[...], 0.0)
        head[HALO:, :] = x_ref[0:HALO, :]
        wk = _taps(w_ref, K, tc)
        bias = jnp.broadcast_to(b_ref[...], (HALO, tc))
        for r in range(tr // HALO):
            lo = r * HALO
            acc = bias + wk[K - 1] * x_ref[lo:lo + HALO, :]
            for k in range(K - 1):
                s = K - 1 - k
                acc = acc + wk[k] * (head[HALO - s:2 * HALO - s, :] if r == 0 else x_ref[lo - s:lo + HALO - s, :])
            if act == "silu":
                acc = acc * _sigmoid(acc)
            o_ref[lo:lo + HALO, :] = acc

    return pl.pallas_call(
        body, name=name, grid=(C // tc, T // tr),
        in_specs=[pl.BlockSpec((tr, tc), lambda j, i: (i, joff + j)),
                  pl.BlockSpec((HALO, tc), lambda j, i: (jnp.maximum(i * hb - 1, 0), joff + j)),
                  pl.BlockSpec((K, tc), lambda j, i: (0, j)),
                  pl.BlockSpec((1, tc), lambda j, i: (0, j))],
        out_specs=pl.BlockSpec((tr, tc), lambda j, i: (i, j)),
        out_shape=jax.ShapeDtypeStruct((T, C), F32),
        scratch_shapes=[pltpu.VMEM((2 * HALO, tc), F32)],
        compiler_params=_cp(("parallel", "arbitrary")))(pre, pre, w, b)


def _conv_bwd(dout, pre, pre_col_off, C, w, b, K, act, dst, dst_col_off, name, tr=512, tc=512):
    T = pre.shape[0]
    tc = min(tc, C)
    assert T % tr == 0 and C % tc == 0 and pre_col_off % tc == 0 and dst_col_off % tc == 0
    joff, doff = pre_col_off // tc, dst_col_off // tc
    hb = tr // HALO
    nt = T // tr
    n = tr // HALO
    last_hblock = T // HALO - 1

    def body(g_ref, gn_ref, x_ref, xp_ref, xn_ref, w_ref, b_ref, *rest):
        o_ref, dw_ref, db_ref, head, tail, gext = rest[-6:]
        i = pl.program_id(1)
        not_last = i < nt - 1
        head[0:HALO, :] = jnp.where(i > 0, xp_ref[...], 0.0)
        head[HALO:, :] = x_ref[0:HALO, :]
        tail[0:HALO, :] = x_ref[tr - HALO:tr, :]
        tail[HALO:, :] = jnp.where(not_last, xn_ref[...], 0.0)
        wk = _taps(w_ref, K, tc)

        def x_at(r, s):
            if r == 0:
                return head[HALO - s:2 * HALO - s, :]
            if r == n:
                return tail[HALO - s:2 * HALO - s, :]
            return x_ref[r * HALO - s:(r + 1) * HALO - s, :]

        acc_w = [jnp.zeros((HALO, tc), F32) for _ in range(K)]
        acc_b = jnp.zeros((HALO, tc), F32)
        gn = jnp.where(not_last, gn_ref[...], 0.0)
        if act == "silu":
            for r in range(n + 1):
                xs = [x_at(r, K - 1 - k) for k in range(K)]
                co = jnp.broadcast_to(b_ref[...], (HALO, tc))
                for k in range(K):
                    co = co + wk[k] * xs[k]
                sg = _sigmoid(co)
                gext[r * HALO:(r + 1) * HALO, :] = (g_ref[r * HALO:(r + 1) * HALO, :] if r < n else gn) * (
                    sg * (1.0 + co * (1.0 - sg)))
            src = gext
        else:
            gext[0:HALO, :] = g_ref[tr - HALO:tr, :]
            gext[HALO:2 * HALO, :] = gn
            src = g_ref
        for r in range(n):
            lo = r * HALO
            x = x_ref[lo:lo + HALO, :]
            dpre = None
            for s in range(K):
                if act != "silu" and r == n - 1 and s > 0:
                    gs = gext[s:HALO + s, :]
                else:
                    gs = src[lo + s:lo + HALO + s, :]
                dpre = wk[K - 1 - s] * gs if dpre is None else dpre + wk[K - 1 - s] * gs
                acc_w[K - 1 - s] = acc_w[K - 1 - s] + gs * x
                if s == 0:
                    acc_b = acc_b + gs
            o_ref[lo:lo + HALO, :] = dpre.astype(o_ref.dtype)

        @pl.when(i == 0)
        def _():
            dw_ref[...] = jnp.zeros_like(dw_ref)
            db_ref[...] = jnp.zeros_like(db_ref)

        db_ref[...] += jnp.sum(acc_b, axis=0, keepdims=True)
        dw_ref[...] += jnp.concatenate([jnp.sum(a, axis=0, keepdims=True) for a in acc_w], axis=0)

    tile = lambda off: pl.BlockSpec((tr, tc), lambda j, i: (i, off + j))
    nxt = lambda off: pl.BlockSpec((HALO, tc), lambda j, i: (jnp.minimum((i + 1) * hb, last_hblock), off + j))
    in_specs = [tile(0), nxt(0), tile(joff),
                pl.BlockSpec((HALO, tc), lambda j, i: (jnp.maximum(i * hb - 1, 0), joff + j)), nxt(joff),
                pl.BlockSpec((K, tc), lambda j, i: (0, j)), pl.BlockSpec((1, tc), lambda j, i: (0, j))]
    args = (dout, dout, pre, pre, pre, w, b)
    if isinstance(dst, jax.ShapeDtypeStruct):
        aliases = {}
    else:
        in_specs.append(pl.BlockSpec(memory_space=pl.ANY))
        args += (dst,)
        aliases = {7: 0}
    return pl.pallas_call(
        body, name=name, grid=(C // tc, nt), in_specs=in_specs,
        out_specs=[tile(doff), pl.BlockSpec((K, tc), lambda j, i: (0, j)), pl.BlockSpec((1, tc), lambda j, i: (0, j))],
        out_shape=[jax.ShapeDtypeStruct(dst.shape, dst.dtype), jax.ShapeDtypeStruct((K, C), F32),
                   jax.ShapeDtypeStruct((1, C), F32)],
        scratch_shapes=[pltpu.VMEM((2 * HALO, tc), F32), pltpu.VMEM((2 * HALO, tc), F32),
                        pltpu.VMEM((tr + HALO, tc), F32)],
        input_output_aliases=aliases,
        compiler_params=_cp(("parallel", "arbitrary")))(*args)


PAIR = 2 * SSMD
PAIRS_PER_GROUP = SSM_GROUP_COLS // PAIR


def _dot3(x, onehot):
    h1 = x.astype(BF16)
    r = x - h1.astype(F32)
    h2 = r.astype(BF16)
    h3 = (r - h2.astype(F32)).astype(BF16)
    return _dot(h1, onehot) + _dot(h2, onehot) + _dot(h3, onehot)


def _chunk_rows(dt_raw, dtb_col, alog_col):
    row = lax.broadcasted_iota(jnp.int32, (CHUNK, CHUNK), 0)
    col = lax.broadcasted_iota(jnp.int32, (CHUNK, CHUNK), 1)
    dt_rawT = dt_raw.T
    dtT = _softplus(dt_rawT + dtb_col)
    a_col = -jnp.exp(alog_col)
    acsT = _dot3(dtT * a_col, (row <= col).astype(BF16))
    return dt_rawT, dtT, a_col, acsT, row, col


def _block_diag(x, left):
    return jnp.concatenate([jnp.where(left, x, 0.0), jnp.where(left, 0.0, x)], axis=0).astype(BF16)


def _lane_bcast(v, h):
    return jnp.broadcast_to(v[:, h:h + 1], (CHUNK, CHUNK))


def _ssd_fwd(xbc, proj_main, proj_tail, dtb_col, alog_col, d_exp, norm_w):
    T = xbc.shape[0]
    nc = T // CHUNK

    def body(xbc_ref, dt_ref, z_ref, dtb_ref, alog_ref, d_ref, nw_ref, y_ref, ypre_ref, hs_ref, H):
        c = pl.program_id(0)

        @pl.when(c == 0)
        def _():
            H[...] = jnp.zeros_like(H)

        hs_ref[0] = H[...]
        _, dtT, _, acsT, row, col = _chunk_rows(dt_ref[:, 0:SSM_HEADS], dtb_ref[...], alog_ref[...])
        tril, left = row >= col, col < SSMD
        acs = acsT.T
        w = (dtT * jnp.exp(acsT[:, CHUNK - 1:CHUNK] - acsT)).T
        cd = jnp.exp(acs[CHUNK - 1:CHUNK, :])
        for g in range(SSM_GROUPS):
            gs = slice(g * SSM_GROUP_COLS, (g + 1) * SSM_GROUP_COLS)
            Bb = xbc_ref[:, B_OFF + g * SSM_STATE:B_OFF + (g + 1) * SSM_STATE].astype(BF16)
            Cb = xbc_ref[:, C_OFF + g * SSM_STATE:C_OFF + (g + 1) * SSM_STATE].astype(BF16)
            Hg = H[:, gs]
            CH = _dot(Cb, Hg.astype(BF16))
            CB = _dot(Cb, Bb, NT)
            ys, xws = [], []
            for kk in range(PAIRS_PER_GROUP):
                k = g * PAIRS_PER_GROUP + kk
                xs_p = xbc_ref[:, k * PAIR:(k + 1) * PAIR]
                mps, ecols, wcols = [], [], []
                for j in range(2):
                    h = 2 * k + j
                    colb = _lane_bcast(acs, h)
                    L = jnp.exp(jnp.where(tril, colb - acsT[h:h + 1, :], -jnp.inf))
                    mps.append((CB * L * dtT[h:h + 1, :]).astype(BF16))
                    ecols.append(jnp.exp(colb))
                    wcols.append(_lane_bcast(w, h))
                yd = _dot(jnp.concatenate(mps, axis=1), _block_diag(xs_p, left))
                ys.append(yd + CH[:, kk * PAIR:(kk + 1) * PAIR] * jnp.where(left, ecols[0], ecols[1]))
                xws.append((xs_p * jnp.where(left, wcols[0], wcols[1])).astype(BF16))
            cd_e = jnp.concatenate([jnp.broadcast_to(cd[:, g * 8 + e:g * 8 + e + 1], (1, SSMD)) for e in range(8)], axis=1)
            H[:, gs] = Hg * cd_e + _dot(Bb, jnp.concatenate(xws, axis=1), TN)
            ypre = jnp.concatenate(ys, axis=1) + xbc_ref[:, gs] * d_ref[:, gs]
            ypre_ref[:, gs] = ypre
            z = z_ref[:, gs]
            yg = ypre * (z * _sigmoid(z))
            r = lax.rsqrt(jnp.mean(yg * yg, axis=1, keepdims=True) + RMS_EPS)
            y_ref[:, gs] = (yg * r * nw_ref[:, gs]).astype(BF16)

    vec = lambda n: _const_spec((1, n))
    colv = _const_spec((SSM_HEADS, 1))
    return pl.pallas_call(
        body, name="ssd_fwd", grid=(nc,),
        in_specs=[pl.BlockSpec((CHUNK, XBC_COLS), lambda c: (c, 0)),
                  pl.BlockSpec((CHUNK, 128), lambda c: (c, DT_OFF // 128)),
                  pl.BlockSpec((CHUNK, SSM_INNER), lambda c: (c, Z_OFF // SSM_INNER)),
                  colv, colv, vec(SSM_INNER), vec(SSM_INNER)],
        out_specs=[pl.BlockSpec((CHUNK, SSM_INNER), lambda c: (c, 0)),
                   pl.BlockSpec((CHUNK, SSM_INNER), lambda c: (c, 0)),
                   pl.BlockSpec((1, SSM_STATE, SSM_INNER), lambda c: (c, 0, 0))],
        out_shape=[jax.ShapeDtypeStruct((T, SSM_INNER), BF16), jax.ShapeDtypeStruct((T, SSM_INNER), F32),
                   jax.ShapeDtypeStruct((nc, SSM_STATE, SSM_INNER), F32)],
        scratch_shapes=[pltpu.VMEM((SSM_STATE, SSM_INNER), F32)],
        compiler_params=_cp(("arbitrary",)))(xbc, proj_tail, proj_main, dtb_col, alog_col, d_exp, norm_w)


def _ssd_bwd(dyo, ypre, xbc, hs, proj_main, proj_tail, dtb_col, alog_col, d_exp, norm_w, ehead_t, dmain, dtail):
    T = xbc.shape[0]
    nc = T // CHUNK

    def body(dyo_ref, ypre_ref, xbc_ref, hs_ref, dt_ref, z_ref, dtb_ref, alog_ref, d_ref, nw_ref, eh_ref,
             dmain_in, dtail_in, dz_ref, ddt_ref, dxbc_ref, dnw_ref, dd_ref, dalog_ref, ddtb_ref, G):
        del dmain_in, dtail_in
        c = pl.program_id(0)

        @pl.when(c == 0)
        def _():
            G[...] = jnp.zeros_like(G)
            dnw_ref[...] = jnp.zeros_like(dnw_ref)
            dd_ref[...] = jnp.zeros_like(dd_ref)
            dalog_ref[...] = jnp.zeros_like(dalog_ref)
            ddtb_ref[...] = jnp.zeros_like(ddtb_ref)

        dt_rawT, dtT, a_col, acsT, row, col = _chunk_rows(dt_ref[:, 0:SSM_HEADS], dtb_ref[...], alog_ref[...])
        tril, triu, left = row >= col, col >= row, col < SSMD
        acs = acsT.T
        dt = dtT.T
        lastT = acsT[:, CHUNK - 1:CHUNK]
        dstT = jnp.exp(lastT - acsT)
        wT = dtT * dstT
        cd = jnp.exp(acs[CHUNK - 1:CHUNK, :])
        ddt_rows, rs_rows, deo_rows, dw_rows = [], [], [], []
        dd_cols, gh_cols, dnw_cols = [], [], []
        for g in range(SSM_GROUPS):
            gs = slice(g * SSM_GROUP_COLS, (g + 1) * SSM_GROUP_COLS)
            z = z_ref[:, gs]
            sz = _sigmoid(z)
            silu_z = z * sz
            ypre = ypre_ref[:, gs]
            yg = ypre * silu_z
            r = lax.rsqrt(jnp.mean(yg * yg, axis=1, keepdims=True) + RMS_EPS)
            ygn = yg * r
            dyo = dyo_ref[:, gs]
            dyn = dyo * nw_ref[:, gs]
            dnw_cols.append(jnp.sum(dyo * ygn, axis=0, keepdims=True))
            dyg = r * (dyn - ygn * jnp.mean(dyn * ygn, axis=1, keepdims=True))
            dz_ref[:, gs] = (dyg * ypre * (sz * (1.0 + z * (1.0 - sz)))).astype(dz_ref.dtype)
            dY = dyg * silu_z
            xs = xbc_ref[:, gs]
            dd_cols.append(jnp.sum(dY * xs, axis=0, keepdims=True))
            Bf = xbc_ref[:, B_OFF + g * SSM_STATE:B_OFF + (g + 1) * SSM_STATE]
            Cf = xbc_ref[:, C_OFF + g * SSM_STATE:C_OFF + (g + 1) * SSM_STATE]
            Bb, Cb = Bf.astype(BF16), Cf.astype(BF16)
            BT, CT = Bf.T, Cf.T
            CB = _dot(Cb, Bb, NT)
            CBT = _dot(Bb, Cb, NT)
            Hg = hs_ref[0, :, gs]
            Gg = G[:, gs]
            gh_cols.append(jnp.sum(Gg * Hg, axis=0, keepdims=True))
            dCB = jnp.zeros((CHUNK, CHUNK), F32)
            dxs_d, dyes, xws, wsels = [], [], [], []
            for kk in range(PAIRS_PER_GROUP):
                k = g * PAIRS_PER_GROUP + kk
                ps = slice(kk * PAIR, (kk + 1) * PAIR)
                xs_p, dY_p = xs[:, ps], dY[:, ps]
                Ls, LTs, dtcols, ecols, wcols = [], [], [], [], []
                for j in range(2):
                    h = 2 * k + j
                    colb = _lane_bcast(acs, h)
                    seg = colb - acsT[h:h + 1, :]
                    Ls.append(jnp.exp(jnp.where(tril, seg, -jnp.inf)))
                    LTs.append(jnp.exp(jnp.where(triu, -seg, -jnp.inf)))
                    dtcol = _lane_bcast(dt, h)
                    dtcols.append(dtcol)
                    ecols.append(jnp.exp(colb))
                    wcols.append(dtcol * jnp.exp(acs[CHUNK - 1:CHUNK, h:h + 1] - colb))
                wsel = jnp.where(left, wcols[0], wcols[1])
                dYe_p = dY_p * jnp.where(left, ecols[0], ecols[1])
                bdx = _block_diag(xs_p, left)
                bddy = _block_diag(dY_p, left)
                dMx2 = _dot(dY_p.astype(BF16), bdx, NT)
                dMxT2 = _dot(xs_p.astype(BF16), bddy, NT)
                Q1 = _dot(Hg[:, ps].astype(BF16), _block_diag(dYe_p, left), NT)
                Q2 = _dot(Gg[:, ps].astype(BF16), bdx, NT)
                mts = []
                for j in range(2):
                    h = 2 * k + j
                    js = slice(j * CHUNK, (j + 1) * CHUNK)
                    dMx = dMx2[:, js]
                    A = CB * Ls[j]
                    AT = CBT * LTs[j]
                    ddt_rows.append(jnp.sum(A * dMx, axis=0, keepdims=True))
                    ATd = AT * dtcols[j]
                    rs_rows.append(jnp.sum(ATd * dMxT2[:, js], axis=0, keepdims=True))
                    dCB = dCB + dMx * Ls[j] * dtT[h:h + 1, :]
                    mts.append(ATd.astype(BF16))
                    deo_rows.append(jnp.sum(CT * Q1[:, js], axis=0, keepdims=True))
                    dw_rows.append(jnp.sum(BT * Q2[:, js], axis=0, keepdims=True))
                dxs_d.append(_dot(jnp.concatenate(mts, axis=1), bddy))
                dyes.append(dYe_p.astype(BF16))
                xws.append((xs_p * wsel).astype(BF16))
                wsels.append(wsel)
            dYe_g = jnp.concatenate(dyes, axis=1)
            xw_g = jnp.concatenate(xws, axis=1)
            Hgb, Ggb, dCBb = Hg.astype(BF16), Gg.astype(BF16), dCB.astype(BF16)
            dxbc_ref[:, C_OFF + g * SSM_STATE:C_OFF + (g + 1) * SSM_STATE] = _dot(dYe_g, Hgb, NT) + _dot(dCBb, Bb)
            dxbc_ref[:, B_OFF + g * SSM_STATE:B_OFF + (g + 1) * SSM_STATE] = _dot(xw_g, Ggb, NT) + _dot(dCBb, Cb, TN)
            BG = _dot(Bb, Ggb)
            dxbc_ref[:, gs] = (jnp.concatenate(dxs_d, axis=1) + BG * jnp.concatenate(wsels, axis=1)
                               + dY * d_ref[:, gs])
            cd_e = jnp.concatenate([jnp.broadcast_to(cd[:, g * 8 + e:g * 8 + e + 1], (1, SSMD)) for e in range(8)], axis=1)
            G[:, gs] = Gg * cd_e + _dot(Cb, dYe_g, TN)
        dnw_ref[...] += jnp.concatenate(dnw_cols, axis=1)
        eh = eh_ref[...]
        dd_ref[...] += jnp.sum(eh * jnp.concatenate(dd_cols, axis=1), axis=1, keepdims=True)
        dcd = jnp.sum(eh * jnp.concatenate(gh_cols, axis=1), axis=1, keepdims=True)
        DDT = jnp.concatenate(ddt_rows, axis=0)
        DW = jnp.concatenate(dw_rows, axis=0)
        DWw = DW * wT
        dacsT = jnp.concatenate(rs_rows, axis=0) - DDT * dtT + jnp.concatenate(deo_rows, axis=0) - DWw
        end = jnp.sum(DWw, axis=1, keepdims=True) + dcd * jnp.exp(lastT)
        lane = lax.broadcasted_iota(jnp.int32, (SSM_HEADS, CHUNK), 1)
        dacsT = dacsT + jnp.where(lane == CHUNK - 1, end, 0.0)
        dadtT = _dot3(dacsT, tril.astype(BF16))
        ddtT = dadtT * a_col + DDT + DW * dstT
        dalog_ref[...] += jnp.sum(dadtT * dtT, axis=1, keepdims=True) * a_col
        ddt_rawT = ddtT * _sigmoid(dt_rawT + dtb_ref[...])
        ddtb_ref[...] += jnp.sum(ddt_rawT, axis=1, keepdims=True)
        ddt_ref[...] = jnp.concatenate([ddt_rawT.T, jnp.zeros((CHUNK, 128 - SSM_HEADS), F32)], axis=1).astype(ddt_ref.dtype)

    rev = lambda c: nc - 1 - c
    vec = lambda n: _const_spec((1, n))
    colv = _const_spec((SSM_HEADS, 1))
    any_spec = pl.BlockSpec(memory_space=pl.ANY)
    return pl.pallas_call(
        body, name="ssd_bwd", grid=(nc,),
        in_specs=[pl.BlockSpec((CHUNK, SSM_INNER), lambda c: (rev(c), 0)),
                  pl.BlockSpec((CHUNK, SSM_INNER), lambda c: (rev(c), 0)),
                  pl.BlockSpec((CHUNK, XBC_COLS), lambda c: (rev(c), 0)),
                  pl.BlockSpec((1, SSM_STATE, SSM_INNER), lambda c: (rev(c), 0, 0)),
                  pl.BlockSpec((CHUNK, 128), lambda c: (rev(c), DT_OFF // 128)),
                  pl.BlockSpec((CHUNK, SSM_INNER), lambda c: (rev(c), Z_OFF // SSM_INNER)),
                  colv, colv, vec(SSM_INNER), vec(SSM_INNER), _const_spec((SSM_HEADS, SSM_INNER)), any_spec, any_spec],
        out_specs=[pl.BlockSpec((CHUNK, SSM_INNER), lambda c: (rev(c), Z_OFF // SSM_INNER)),
                   pl.BlockSpec((CHUNK, 128), lambda c: (rev(c), DT_OFF // 128)),
                   pl.BlockSpec((CHUNK, XBC_COLS), lambda c: (rev(c), 0)),
                   vec(SSM_INNER), colv, colv, colv],
        out_shape=[jax.ShapeDtypeStruct(dmain.shape, dmain.dtype), jax.ShapeDtypeStruct(dtail.shape, dtail.dtype),
                   jax.ShapeDtypeStruct((T, XBC_COLS), F32), jax.ShapeDtypeStruct((1, SSM_INNER), F32),
                   jax.ShapeDtypeStruct((SSM_HEADS, 1), F32), jax.ShapeDtypeStruct((SSM_HEADS, 1), F32),
                   jax.ShapeDtypeStruct((SSM_HEADS, 1), F32)],
        scratch_shapes=[pltpu.VMEM((SSM_STATE, SSM_INNER), F32)],
        input_output_aliases={11: 0, 12: 1},
        compiler_params=_cp(("arbitrary",)))(dyo, ypre, xbc, hs, proj_tail, proj_main, dtb_col, alog_col, d_exp, norm_w,
                                             ehead_t, dmain, dtail)


def _rel_bucket(rel):
    n = jnp.maximum(rel, 0)
    max_exact = REL_BUCKETS // 2
    nf = jnp.maximum(n, 1).astype(F32)
    large = max_exact + (jnp.log(nf / max_exact) / math.log(REL_MAX_DIST / max_exact)
                         * (REL_BUCKETS - max_exact)).astype(jnp.int32)
    large = jnp.minimum(large, REL_BUCKETS - 1)
    return jnp.where(n < max_exact, n, large)


def _band_geometry():
    qi = jnp.arange(WINDOW)[:, None] + WINDOW
    kj = jnp.arange(2 * WINDOW)[None, :]
    rel = qi - kj
    return _rel_bucket(rel), (rel >= 0) & (rel < WINDOW)


def _attn_logits(kband, qh, bias_h, first):
    s = _dot(kband, qh, NT) * (HEADDIM ** -0.5) + bias_h
    rowk = lax.broadcasted_iota(jnp.int32, (2 * WINDOW, WINDOW), 0)
    return jnp.where(jnp.logical_and(first, rowk < WINDOW), NEG, s)


def _attn_fwd(proj_main, proj_tail, bias_tbl, sinks):
    T = proj_main.shape[0]
    nb = T // WINDOW

    def body(q_ref, kv_ref, kvp_ref, bias_ref, sink_ref, o_ref, lse_ref):
        i = pl.program_id(0)
        first = i == 0
        outs, lses = [], []
        for kvh in range(ATTN_KV):
            ks = slice(K_OFF + kvh * HEADDIM, K_OFF + (kvh + 1) * HEADDIM)
            vs = slice(V_OFF + kvh * HEADDIM, V_OFF + (kvh + 1) * HEADDIM)
            kband = jnp.concatenate([kvp_ref[:, ks], kv_ref[:, ks]], axis=0).astype(BF16)
            vband = jnp.concatenate([kvp_ref[:, vs], kv_ref[:, vs]], axis=0).astype(BF16)
            heads = range(kvh * ATTN_GROUP, (kvh + 1) * ATTN_GROUP)
            logits = [_attn_logits(kband, q_ref[:, h * HEADDIM:(h + 1) * HEADDIM].astype(BF16), bias_ref[h], first)
                      for h in heads]
            probs = []
            for h, s in zip(heads, logits):
                sink = sink_ref[:, h:h + 1]
                m = jnp.maximum(jnp.max(s, axis=0, keepdims=True), sink)
                p = jnp.exp(s - m)
                den = jnp.sum(p, axis=0, keepdims=True) + jnp.exp(sink - m)
                probs.append((p * (1.0 / den)).astype(BF16))
                lses.append(m + jnp.log(den))
            outs += [_dot(pt, vband, TN) for pt in probs]
        o_ref[...] = jnp.concatenate(outs, axis=1).astype(BF16)
        lse_ref[...] = jnp.concatenate(lses, axis=0)

    return pl.pallas_call(
        body, name="attn_fwd", grid=(nb,),
        in_specs=[pl.BlockSpec((WINDOW, D_MODEL), lambda i: (i, Q_OFF // D_MODEL)),
                  pl.BlockSpec((WINDOW, 256), lambda i: (i, 0)),
                  pl.BlockSpec((WINDOW, 256), lambda i: (jnp.maximum(i - 1, 0), 0)),
                  _const_spec((ATTN_HEADS, 2 * WINDOW, WINDOW)), _const_spec((1, ATTN_HEADS))],
        out_specs=[pl.BlockSpec((WINDOW, D_MODEL), lambda i: (i, 0)),
                   pl.BlockSpec((ATTN_HEADS, WINDOW), lambda i: (0, i))],
        out_shape=[jax.ShapeDtypeStruct((T, D_MODEL), BF16), jax.ShapeDtypeStruct((ATTN_HEADS, T), F32)],
        compiler_params=_cp(("arbitrary",)))(proj_main, proj_tail, proj_tail, bias_tbl, sinks)


def _attn_bwd(dy, lse, proj_main, proj_tail, bias_tbl, sinks, dmain):
    T = proj_main.shape[0]
    nb = T // WINDOW

    def body(dy_ref, lse_ref, q_ref, kv_ref, kvp_ref, bias_ref, sink_ref, dmain_in,
             dq_ref, dkv_ref, dbias_ref, dsink_ref, carry):
        del dmain_in
        i = pl.program_id(0)
        first = i == 0

        @pl.when(first)
        def _():
            carry[...] = jnp.zeros_like(carry)
            dbias_ref[...] = jnp.zeros_like(dbias_ref)
            dsink_ref[...] = jnp.zeros_like(dsink_ref)

        @pl.when(i < nb)
        def _():
            scale = HEADDIM ** -0.5
            dqs, dsinks, dks, dvs = [], [], [], []
            for kvh in range(ATTN_KV):
                ks = slice(K_OFF + kvh * HEADDIM, K_OFF + (kvh + 1) * HEADDIM)
                vs = slice(V_OFF + kvh * HEADDIM, V_OFF + (kvh + 1) * HEADDIM)
                kband = jnp.concatenate([kvp_ref[:, ks], kv_ref[:, ks]], axis=0).astype(BF16)
                vband = jnp.concatenate([kvp_ref[:, vs], kv_ref[:, vs]], axis=0).astype(BF16)
                heads = range(kvh * ATTN_GROUP, (kvh + 1) * ATTN_GROUP)
                qs = [q_ref[:, h * HEADDIM:(h + 1) * HEADDIM].astype(BF16) for h in heads]
                dos = [dy_ref[:, h * HEADDIM:(h + 1) * HEADDIM] for h in heads]
                logits = [_attn_logits(kband, qh, bias_ref[h], first) for h, qh in zip(heads, qs)]
                dps = [_dot(vband, do, NT) for do in dos]
                pbs, dsbs = [], []
                for h, s, dp in zip(heads, logits, dps):
                    lse_h = lse_ref[h:h + 1, :]
                    p = jnp.exp(s - lse_h)
                    delta = jnp.sum(p * dp, axis=0, keepdims=True)
                    ds = p * (dp - delta)
                    psink = jnp.exp(sink_ref[:, h:h + 1] - lse_h)
                    dsinks.append(-jnp.sum(psink * delta, axis=1, keepdims=True))
                    dbias_ref[h] += ds
                    pbs.append(p.astype(BF16))
                    dsbs.append((ds * scale).astype(BF16))
                dqs += [_dot(dsb, kband, TN) for dsb in dsbs]
                dks.append(_dot(jnp.concatenate(dsbs, axis=1), jnp.concatenate(qs, axis=0)))
                dvs.append(_dot(jnp.concatenate(pbs, axis=1), jnp.concatenate(dos, axis=0)))
            dq_ref[...] = jnp.concatenate(dqs, axis=1).astype(dq_ref.dtype)
            dsink_ref[...] += jnp.concatenate(dsinks, axis=1)
            dkv = jnp.concatenate(dks + dvs, axis=1)
            dkv_ref[...] = (carry[...] + dkv[0:WINDOW, :]).astype(dkv_ref.dtype)
            carry[...] = dkv[WINDOW:, :]

        @pl.when(i == nb)
        def _():
            dkv_ref[...] = carry[...].astype(dkv_ref.dtype)

    cur = lambda i: jnp.minimum(i, nb - 1)
    return pl.pallas_call(
        body, name="attn_bwd", grid=(nb + 1,),
        in_specs=[pl.BlockSpec((WINDOW, D_MODEL), lambda i: (cur(i), 0)),
                  pl.BlockSpec((ATTN_HEADS, WINDOW), lambda i: (0, cur(i))),
                  pl.BlockSpec((WINDOW, D_MODEL), lambda i: (cur(i), Q_OFF // D_MODEL)),
                  pl.BlockSpec((WINDOW, 256), lambda i: (cur(i), 0)),
                  pl.BlockSpec((WINDOW, 256), lambda i: (jnp.maximum(cur(i) - 1, 0), 0)),
                  _const_spec((ATTN_HEADS, 2 * WINDOW, WINDOW)), _const_spec((1, ATTN_HEADS)),
                  pl.BlockSpec(memory_space=pl.ANY)],
        out_specs=[pl.BlockSpec((WINDOW, D_MODEL), lambda i: (cur(i), Q_OFF // D_MODEL)),
                   pl.BlockSpec((WINDOW, 256), lambda i: (jnp.maximum(i - 1, 0), 0)),
                   _const_spec((ATTN_HEADS, 2 * WINDOW, WINDOW)), _const_spec((1, ATTN_HEADS))],
        out_shape=[jax.ShapeDtypeStruct(dmain.shape, dmain.dtype), jax.ShapeDtypeStruct((T, TAIL_COLS), BF16),
                   jax.ShapeDtypeStruct((ATTN_HEADS, 2 * WINDOW, WINDOW), F32),
                   jax.ShapeDtypeStruct((1, ATTN_HEADS), F32)],
        scratch_shapes=[pltpu.VMEM((WINDOW, 256), F32)],
        input_output_aliases={7: 0},
        compiler_params=_cp(("arbitrary",)))(dy, lse, proj_main, proj_tail, proj_tail, bias_tbl, sinks, dmain)


def _bias_table(rel_bias_t, onehot_t, mask):
    def body(rb_ref, oh_ref, m_ref, o_ref):
        o_ref[...] = _dot3(rb_ref[...], oh_ref[...]) + m_ref[...]

    flat = pl.pallas_call(body, name="bias_table",
                          out_shape=jax.ShapeDtypeStruct((ATTN_HEADS, 2 * WINDOW * WINDOW), F32))(rel_bias_t, onehot_t, mask)
    return flat.reshape(ATTN_HEADS, 2 * WINDOW, WINDOW)


def _rel_bias_grad(dbias, onehot):
    def body(d_ref, oh_ref, o_ref):
        o_ref[...] = _dot(d_ref[...], oh_ref[...], NN, HIGHEST)

    return pl.pallas_call(body, name="rel_bias_grad",
                          out_shape=jax.ShapeDtypeStruct((ATTN_HEADS, REL_BUCKETS), F32))(dbias, onehot)


def _ln_fwd(r, g, b):
    mu = jnp.mean(r, axis=1, keepdims=True)
    xc = r - mu
    rstd = lax.rsqrt(jnp.mean(xc * xc, axis=1, keepdims=True) + LN_EPS)
    xhat = xc * rstd
    return xhat * g + b, xhat, rstd


def _ln_bwd(dy, xhat, rstd, g):
    dxh = dy * g
    return rstd * (dxh - jnp.mean(dxh, axis=1, keepdims=True) - xhat * jnp.mean(dxh * xhat, axis=1, keepdims=True))


def _merge_fwd(y_ssm, y_attn, proj_main, b_gate, w_bs, w_ba, tm=512):
    T = y_ssm.shape[0]

    def body(ys_ref, ya_ref, gs_ref, ga_ref, bg_ref, wbs_ref, wba_ref, m_ref, bs_ref, ba_ref):
        bs = _dot(ys_ref[...], wbs_ref[...])
        ba = _dot(ya_ref[...], wba_ref[...])
        g_s = _sigmoid(gs_ref[...] + bg_ref[:, 0:D_MODEL])
        g_a = _sigmoid(ga_ref[...] + bg_ref[:, D_MODEL:])
        m_ref[...] = (g_s * bs + g_a * ba).astype(BF16)
        bs_ref[...] = bs
        ba_ref[...] = ba

    row = lambda w, off=0: pl.BlockSpec((tm, w), lambda i: (i, off))
    return pl.pallas_call(
        body, name="merge_fwd", grid=(T // tm,),
        in_specs=[row(SSM_INNER), row(D_MODEL), row(D_MODEL, GATE_OFF // D_MODEL), row(D_MODEL, GATE_OFF // D_MODEL + 1),
                  _const_spec((1, 2 * D_MODEL)), _const_spec((SSM_INNER, D_MODEL)), _const_spec((D_MODEL, D_MODEL))],
        out_specs=[row(D_MODEL), row(D_MODEL), row(D_MODEL)],
        out_shape=[jax.ShapeDtypeStruct((T, D_MODEL), BF16), jax.ShapeDtypeStruct((T, D_MODEL), F32),
                   jax.ShapeDtypeStruct((T, D_MODEL), F32)],
        compiler_params=_cp(("parallel",)))(y_ssm, y_attn, proj_main, proj_main, b_gate, w_bs, w_ba)


def _mix_ln1(merged, w_mo, x, g1, b1, tm=512):
    T = x.shape[0]

    def body(m_ref, w_ref, x_ref, g_ref, b_ref, r_ref, h_ref):
        r = ALPHA * x_ref[...] + _dot(m_ref[...], w_ref[...])
        r_ref[...] = r
        h_ref[...] = _ln_fwd(r, g_ref[...], b_ref[...])[0]

    row = pl.BlockSpec((tm, D_MODEL), lambda i: (i, 0))
    return pl.pallas_call(
        body, name="mix_ln1", grid=(T // tm,),
        in_specs=[row, _const_spec((D_MODEL, D_MODEL)), row, _const_spec((1, D_MODEL)), _const_spec((1, D_MODEL))],
        out_specs=[row, row],
        out_shape=[jax.ShapeDtypeStruct((T, D_MODEL), F32), jax.ShapeDtypeStruct((T, D_MODEL), F32)],
        compiler_params=_cp(("parallel",)))(merged, w_mo, x, g1, b1)


def _ffn_conv_glu(u_pre, w, b, tr=512, tc=256):
    T = u_pre.shape[0]
    K = FFN_CONV
    nj = D_FF // tc
    hb = tr // HALO
    assert T % tr == 0 and D_FF % tc == 0

    def body(xg_ref, xgp_ref, xv_ref, xvp_ref, wg_ref, wv_ref, bg_ref, bv_ref, u_ref, a_ref, head_g, head_v):
        i = pl.program_id(1)
        halves = []
        for x_ref, xp_ref, w_ref, b_ref, head in ((xg_ref, xgp_ref, wg_ref, bg_ref, head_g),
                                                  (xv_ref, xvp_ref, wv_ref, bv_ref, head_v)):
            head[0:HALO, :] = jnp.where(i > 0, xp_ref[...], 0.0)
            head[HALO:, :] = x_ref[0:HALO, :]
            halves.append((x_ref, head, _taps(w_ref, K, tc), jnp.broadcast_to(b_ref[...], (HALO, tc))))

        def conv(half, r):
            x_ref, head, wk, bias = halves[half]
            lo = r * HALO
            acc = bias + wk[K - 1] * x_ref[lo:lo + HALO, :]
            for k in range(K - 1):
                s = K - 1 - k
                acc = acc + wk[k] * (head[HALO - s:2 * HALO - s, :] if r == 0 else x_ref[lo - s:lo + HALO - s, :])
            return acc

        for r2 in range(tr // (2 * HALO)):
            acts = []
            for r in (2 * r2, 2 * r2 + 1):
                lo = r * HALO
                ug, uv = conv(0, r), conv(1, r)
                u_ref[0, lo:lo + HALO, :] = ug
                u_ref[1, lo:lo + HALO, :] = uv
                acts.append(ug * _sigmoid(ug) * uv)
            a_ref[2 * r2 * HALO:(2 * r2 + 2) * HALO, :] = jnp.concatenate(acts, axis=0).astype(BF16)

    tile = lambda off: pl.BlockSpec((tr, tc), lambda j, i: (i, off + j))
    prev = lambda off: pl.BlockSpec((HALO, tc), lambda j, i: (jnp.maximum(i * hb - 1, 0), off + j))
    row = lambda rows, off: pl.BlockSpec((rows, tc), lambda j, i: (0, off + j))
    return pl.pallas_call(
        body, name="ffn_conv_glu", grid=(nj, T // tr),
        in_specs=[tile(0), prev(0), tile(nj), prev(nj), row(K, 0), row(K, nj), row(1, 0), row(1, nj)],
        out_specs=[pl.BlockSpec((2, tr, tc), lambda j, i: (0, i, j)), pl.BlockSpec((tr, tc), lambda j, i: (i, j))],
        out_shape=[jax.ShapeDtypeStruct((2, T, D_FF), F32), jax.ShapeDtypeStruct((T, D_FF), BF16)],
        scratch_shapes=[pltpu.VMEM((2 * HALO, tc), F32), pltpu.VMEM((2 * HALO, tc), F32)],
        compiler_params=_cp(("parallel", "arbitrary")))(u_pre, u_pre, u_pre, u_pre, w, w, b, b)


def _down_ln2_loss(act, w_down, h1, target, g2, b2, tm=512):
    T = h1.shape[0]

    def body(a_ref, w_ref, h_ref, t_ref, g_ref, b_ref, dr_ref, dg_ref, db_ref, l_ref):
        @pl.when(pl.program_id(0) == 0)
        def _():
            dg_ref[...] = jnp.zeros_like(dg_ref)
            db_ref[...] = jnp.zeros_like(db_ref)
            l_ref[...] = jnp.zeros_like(l_ref)

        r = ALPHA * h_ref[...] + _dot(a_ref[...], w_ref[...])
        y, xhat, rstd = _ln_fwd(r, g_ref[...], b_ref[...])
        err = y - t_ref[...]
        l_ref[...] += jnp.sum(err * err, keepdims=True)
        dy = err * (1.0 / D_MODEL)
        dg_ref[...] += jnp.sum(dy * xhat, axis=0, keepdims=True)
        db_ref[...] += jnp.sum(dy, axis=0, keepdims=True)
        dr_ref[...] = _ln_bwd(dy, xhat, rstd, g_ref[...])

    row = pl.BlockSpec((tm, D_MODEL), lambda i: (i, 0))
    vec = _const_spec((1, D_MODEL))
    return pl.pallas_call(
        body, name="down_ln2_loss", grid=(T // tm,),
        in_specs=[pl.BlockSpec((tm, D_FF), lambda i: (i, 0)), _const_spec((D_FF, D_MODEL)), row, row, vec, vec],
        out_specs=[row, vec, vec, _const_spec((1, 1))],
        out_shape=[jax.ShapeDtypeStruct((T, D_MODEL), F32), jax.ShapeDtypeStruct((1, D_MODEL), F32),
                   jax.ShapeDtypeStruct((1, D_MODEL), F32), jax.ShapeDtypeStruct((1, 1), F32)],
        compiler_params=_cp(("arbitrary",)))(act, w_down, h1, target, g2, b2)


def _ffn_bwd_act(dr2, w_down, u, tm=512, tn=1408):
    T = dr2.shape[0]
    nj = D_FF // tn

    def body(d_ref, w_ref, g_ref, v_ref, o_ref, dact):
        half = pl.program_id(2)

        @pl.when(half == 0)
        def _():
            dact[...] = _dot(d_ref[...].astype(BF16), w_ref[...], NT)
            g = g_ref[...]
            sg = _sigmoid(g)
            o_ref[...] = dact[...] * v_ref[...] * (sg * (1.0 + g * (1.0 - sg)))

        @pl.when(half == 1)
        def _():
            g = g_ref[...]
            o_ref[...] = dact[...] * (g * _sigmoid(g))

    return pl.pallas_call(
        body, name="ffn_bwd_act", grid=(T // tm, nj, 2),
        in_specs=[pl.BlockSpec((tm, D_MODEL), lambda i, j, h: (i, 0)),
                  pl.BlockSpec((tn, D_MODEL), lambda i, j, h: (j, 0)),
                  pl.BlockSpec((None, tm, tn), lambda i, j, h: (0, i, j)),
                  pl.BlockSpec((None, tm, tn), lambda i, j, h: (1, i, j))],
        out_specs=pl.BlockSpec((tm, tn), lambda i, j, h: (i, h * nj + j)),
        out_shape=jax.ShapeDtypeStruct((T, 2 * D_FF), F32),
        scratch_shapes=[pltpu.VMEM((tm, tn), F32)],
        compiler_params=_cp(("parallel", "arbitrary", "arbitrary")))(dr2, w_down, u, u)


def _ffn_bwd_in(du_pre, w_up, dr2, r1, g1, b1, tm=1024, tk=1408):
    T = dr2.shape[0]
    tm = min(tm, T)
    assert T % tm == 0
    nk = 2 * D_FF // tk

    def body(d_ref, w_ref, dr2_ref, r_ref, g_ref, b_ref, dr1_ref, dg_ref, db_ref, acc):
        i, k = pl.program_id(0), pl.program_id(1)

        @pl.when(jnp.logical_and(i == 0, k == 0))
        def _():
            dg_ref[...] = jnp.zeros_like(dg_ref)
            db_ref[...] = jnp.zeros_like(db_ref)

        @pl.when(k == 0)
        def _():
            acc[...] = ALPHA * dr2_ref[...]

        acc[...] += _dot(d_ref[...], w_ref[...], NT)

        @pl.when(k == nk - 1)
        def _():
            _, xhat, rstd = _ln_fwd(r_ref[...], g_ref[...], b_ref[...])
            dy = acc[...]
            dg_ref[...] += jnp.sum(dy * xhat, axis=0, keepdims=True)
            db_ref[...] += jnp.sum(dy, axis=0, keepdims=True)
            dr1_ref[...] = _ln_bwd(dy, xhat, rstd, g_ref[...])

    row = pl.BlockSpec((tm, D_MODEL), lambda i, k: (i, 0))
    vec = _const_spec((1, D_MODEL))
    return pl.pallas_call(
        body, name="ffn_bwd_in", grid=(T // tm, nk),
        in_specs=[pl.BlockSpec((tm, tk), lambda i, k: (i, k)), pl.BlockSpec((D_MODEL, tk), lambda i, k: (0, k)),
                  row, row, vec, vec],
        out_specs=[row, vec, vec],
        out_shape=[jax.ShapeDtypeStruct((T, D_MODEL), F32), jax.ShapeDtypeStruct((1, D_MODEL), F32),
                   jax.ShapeDtypeStruct((1, D_MODEL), F32)],
        scratch_shapes=[pltpu.VMEM((tm, D_MODEL), F32)],
        compiler_params=_cp(("arbitrary", "arbitrary")))(du_pre, w_up, dr2, r1, g1, b1)


def _mix_bwd(dr1, w_mo, w_bs, w_ba, bs, ba, proj_main, b_gate, tm=512):
    T = dr1.shape[0]

    def body(d_ref, wmo_ref, wbs_ref, wba_ref, bs_ref, ba_ref, gs_ref, ga_ref, bg_ref,
             dg_ref, dbs_ref, dba_ref, dys_ref, dya_ref, dbg_ref):
        @pl.when(pl.program_id(0) == 0)
        def _():
            dbg_ref[...] = jnp.zeros_like(dbg_ref)

        dm = _dot(d_ref[...].astype(BF16), wmo_ref[...], NT)
        g_s = _sigmoid(gs_ref[...] + bg_ref[:, 0:D_MODEL])
        g_a = _sigmoid(ga_ref[...] + bg_ref[:, D_MODEL:])
        dgs = dm * bs_ref[...] * g_s * (1.0 - g_s)
        dga = dm * ba_ref[...] * g_a * (1.0 - g_a)
        dg_ref[:, 0:D_MODEL] = dgs.astype(BF16)
        dg_ref[:, D_MODEL:] = dga.astype(BF16)
        dbg_ref[:, 0:D_MODEL] += jnp.sum(dgs, axis=0, keepdims=True)
        dbg_ref[:, D_MODEL:] += jnp.sum(dga, axis=0, keepdims=True)
        dbs = (dm * g_s).astype(BF16)
        dba = (dm * g_a).astype(BF16)
        dbs_ref[...] = dbs
        dba_ref[...] = dba
        dys_ref[...] = _dot(dbs, wbs_ref[...], NT)
        dya_ref[...] = _dot(dba, wba_ref[...], NT).astype(BF16)

    row = lambda w, off=0: pl.BlockSpec((tm, w), lambda i: (i, off))
    return pl.pallas_call(
        body, name="mix_bwd", grid=(T // tm,),
        in_specs=[row(D_MODEL), _const_spec((D_MODEL, D_MODEL)), _const_spec((SSM_INNER, D_MODEL)),
                  _const_spec((D_MODEL, D_MODEL)), row(D_MODEL), row(D_MODEL),
                  row(D_MODEL, GATE_OFF // D_MODEL), row(D_MODEL, GATE_OFF // D_MODEL + 1), _const_spec((1, 2 * D_MODEL))],
        out_specs=[row(2 * D_MODEL, GATE_OFF // (2 * D_MODEL)), row(D_MODEL), row(D_MODEL), row(SSM_INNER), row(D_MODEL),
                   _const_spec((1, 2 * D_MODEL))],
        out_shape=[jax.ShapeDtypeStruct((T, MAIN_COLS), BF16), jax.ShapeDtypeStruct((T, D_MODEL), BF16),
                   jax.ShapeDtypeStruct((T, D_MODEL), BF16), jax.ShapeDtypeStruct((T, SSM_INNER), F32),
                   jax.ShapeDtypeStruct((T, D_MODEL), BF16), jax.ShapeDtypeStruct((1, 2 * D_MODEL), F32)],
        compiler_params=_cp(("arbitrary",)))(dr1, w_mo, w_bs, w_ba, bs, ba, proj_main, proj_main, b_gate)


def _local_step(x, target, w, p, late_weights=None):
    xb = x.astype(BF16)
    if late_weights is None:
        proj_main = _matmul(xb, w["in_main"], "nn", F32, "in_proj_main", tm=1024, tn=2048)
    else:
        proj_main, landed = _matmul(xb, w["in_main"], "nn", F32, "in_proj_main", tm=1024, tn=2048,
                                    comm=("gather", late_weights[0]))
        w = {**w, **late_weights[1](landed)}
    proj_tail = _matmul(xb, w["in_tail"], "nn", F32, "in_proj_tail", tn=TAIL_COLS)
    xbc = _conv_fwd(proj_main, XBC_OFF, XBC_COLS, p["ssm_conv_w"], p["ssm_conv_b"], SSM_CONV, "silu", "ssm_conv_fwd")
    y_ssm, ypre, hs = _ssd_fwd(xbc, proj_main, proj_tail, p["dtb_col"], p["alog_col"], p["d_exp"], p["ssm_norm_w"])
    y_attn, lse = _attn_fwd(proj_main, proj_tail, p["bias_tbl"], p["attn_sinks"])
    merged, bs, ba = _merge_fwd(y_ssm, y_attn, proj_main, p["b_gate"], w["bs"], w["ba"])
    r1, h1 = _mix_ln1(merged, w["mo"], x, p["ln1_g"], p["ln1_b"])
    u_pre = _matmul(h1, w["up"], "nn", F32, "ffn_up", tm=1024, tn=1408)
    u, act = _ffn_conv_glu(u_pre, p["ffn_conv_w"], p["ffn_conv_b"])
    dr2, dg2, db2, sq = _down_ln2_loss(act, w["down"], h1, target, p["ln2_g"], p["ln2_b"])
    g = {"ln2_g": dg2, "ln2_b": db2}
    g["w_down"] = _matmul(act, dr2, "tn", F32, "dw_down", tm=1408, tn=1024, tk=1024)
    du = _ffn_bwd_act(dr2, w["down"], u)
    du_pre, g["ffn_conv_w"], g["ffn_conv_b"] = _conv_bwd(
        du, u_pre, 0, 2 * D_FF, p["ffn_conv_w"], p["ffn_conv_b"], FFN_CONV, None,
        jax.ShapeDtypeStruct((x.shape[0], 2 * D_FF), BF16), 0, "ffn_conv_bwd")
    g["w_up"] = _matmul(h1, du_pre, "tn", F32, "dw_up", tm=1024, tn=1408, tk=1024)
    dr1, g["ln1_g"], g["ln1_b"] = _ffn_bwd_in(du_pre, w["up"], dr2, r1, p["ln1_g"], p["ln1_b"])
    g["w_mix_out"] = _matmul(merged, dr1, "tn", F32, "dw_mix_out", tm=1024, tn=1024, tk=2048)
    dmain, dbs, dba, dy_ssm, dy_attn, g["b_gate"] = _mix_bwd(dr1, w["mo"], w["bs"], w["ba"], bs, ba, proj_main, p["b_gate"])
    g["w_branch_ssm"] = _matmul(y_ssm, dbs, "tn", F32, "dw_branch_ssm", tm=1024, tn=1024, tk=2048)
    g["w_branch_attn"] = _matmul(y_attn, dba, "tn", F32, "dw_branch_attn", tm=1024, tn=1024, tk=2048)
    dmain, dtail, dbias, g["attn_sinks"] = _attn_bwd(dy_attn, lse, proj_main, proj_tail, p["bias_tbl"], p["attn_sinks"], dmain)
    g["rel_bias"] = _rel_bias_grad(dbias.reshape(ATTN_HEADS, WINDOW * 2 * WINDOW), p["bucket_onehot"]).T
    dmain, dtail, dxbc, g["ssm_norm_w"], dd, dalog, ddtb = _ssd_bwd(
        dy_ssm, ypre, xbc, hs, proj_main, proj_tail, p["dtb_col"], p["alog_col"], p["d_exp"], p["ssm_norm_w"],
        p["ehead_t"], dmain, dtail)
    g["ssm_d"], g["ssm_a_log"], g["ssm_dt_bias"] = (a.reshape(1, SSM_HEADS) for a in (dd, dalog, ddtb))
    dmain, g["ssm_conv_w"], g["ssm_conv_b"] = _conv_bwd(
        dxbc, proj_main, XBC_OFF, XBC_COLS, p["ssm_conv_w"], p["ssm_conv_b"], SSM_CONV, "silu", dmain, XBC_OFF,
        "ssm_conv_bwd")
    g["in_main"] = _matmul(xb, dmain, "tn", F32, "dw_in_main", tm=1024, tn=1024, tk=2048)
    g["in_tail"] = _matmul(xb, dtail, "tn", F32, "dw_in_tail", tm=1024, tn=TAIL_COLS, tk=2048)
    return sq, (dmain, dtail, dr1), w, g


def _grad_x(dproj, w, exchange=None):
    dmain, dtail, dr1 = dproj
    dx = _matmul(dtail, w["in_tail"], "nt", F32, "dx_tail", tk=TAIL_COLS, addend=dr1, addend_scale=ALPHA)
    return _matmul(dmain, w["in_main"], "nt", F32, "dx_main", tm=1024, tk=2048, addend=dx,
                   comm=None if exchange is None else ("exchange", exchange))


def _split_w_in(w):
    seg = lambda off, n: w[:, off:off + n]
    main = jnp.concatenate([seg(O_Z, 2048), seg(O_XBC, XBC_COLS), seg(O_Q, D_MODEL), seg(O_GATE, 2 * D_MODEL)], axis=1)
    tail = jnp.concatenate([seg(O_K, 128), seg(O_V, 128), seg(O_DT, SSM_HEADS),
                            jnp.zeros((w.shape[0], 128 - SSM_HEADS), w.dtype)], axis=1)
    return main, tail


def _join_w_in(main, tail):
    return jnp.concatenate([main[:, Z_OFF:Z_OFF + 2048], main[:, XBC_OFF:XBC_OFF + XBC_COLS],
                            tail[:, DT_OFF:DT_OFF + SSM_HEADS], main[:, Q_OFF:Q_OFF + D_MODEL],
                            tail[:, K_OFF:K_OFF + 128], tail[:, V_OFF:V_OFF + 128],
                            main[:, GATE_OFF:GATE_OFF + 2 * D_MODEL]], axis=1)


def _prep_params(rel_bias, b_gate, ssm_conv_w, ssm_conv_b, ssm_dt_bias, ssm_a_log, ssm_d, ssm_norm_w, attn_sinks,
                 ln1_g, ln1_b, ffn_conv_w, ffn_conv_b, ln2_g, ln2_b):
    bucket, in_window = _band_geometry()
    bucket, in_window = bucket.T, in_window.T
    onehot = jnp.logical_and(bucket.reshape(-1, 1) == jnp.arange(REL_BUCKETS)[None, :],
                             in_window.reshape(-1, 1)).astype(F32)
    onehot_t = jnp.logical_and(bucket.reshape(1, -1) == jnp.arange(REL_BUCKETS)[:, None],
                               in_window.reshape(1, -1)).astype(BF16)
    bias_tbl = _bias_table(rel_bias.T, onehot_t, jnp.where(in_window.reshape(1, -1), 0.0, NEG))
    ehead_t = (jnp.arange(SSM_INNER)[None, :] // SSMD == jnp.arange(SSM_HEADS)[:, None]).astype(F32)
    return {"bias_tbl": bias_tbl, "bucket_onehot": onehot, "b_gate": b_gate, "ssm_conv_w": ssm_conv_w,
            "ssm_conv_b": ssm_conv_b, "dtb_col": ssm_dt_bias.reshape(SSM_HEADS, 1),
            "alog_col": ssm_a_log.reshape(SSM_HEADS, 1), "ehead_t": ehead_t,
            "d_exp": jnp.repeat(ssm_d, SSMD, axis=1), "ssm_norm_w": ssm_norm_w, "attn_sinks": attn_sinks,
            "ln1_g": ln1_g, "ln1_b": ln1_b, "ffn_conv_w": ffn_conv_w, "ffn_conv_b": ffn_conv_b,
            "ln2_g": ln2_g, "ln2_b": ln2_b}


def _all_gather(shard, name):
    R, C = shard.shape

    def body(x_ref, out_ref, send_sems, recv_sems, local_sem):
        for phase in _gather_phases(x_ref, out_ref, send_sems, recv_sems, local_sem):
            phase()

    return pl.pallas_call(
        body, name=name, out_shape=jax.ShapeDtypeStruct((N_DEV, R, C), shard.dtype),
        in_specs=[pl.BlockSpec(memory_space=pl.ANY)], out_specs=pl.BlockSpec(memory_space=pl.ANY),
        scratch_shapes=[pltpu.SemaphoreType.DMA((7,)), pltpu.SemaphoreType.DMA((7,)), pltpu.SemaphoreType.DMA],
    )(shard)


def _slot_sum(slots, tr=512):
    _, R, C = slots.shape
    assert R % tr == 0

    def body(s_ref, o_ref):
        acc = s_ref[0].astype(F32)
        for i in range(1, N_DEV):
            acc = acc + s_ref[i].astype(F32)
        o_ref[...] = acc

    return pl.pallas_call(
        body, name="slot_sum", grid=(R // tr,),
        in_specs=[pl.BlockSpec((N_DEV, tr, C), lambda i: (0, i, 0))], out_specs=pl.BlockSpec((tr, C), lambda i: (i, 0)),
        out_shape=jax.ShapeDtypeStruct((R, C), F32), compiler_params=_cp(("parallel",)))(slots)


def _adamw_math(w, g, m, v):
    m = ADAM_B1 * m + (1.0 - ADAM_B1) * g
    v = ADAM_B2 * v + (1.0 - ADAM_B2) * (g * g)
    m_hat = m / (1.0 - ADAM_B1 ** ADAM_STEP)
    v_hat = v / (1.0 - ADAM_B2 ** ADAM_STEP)
    return -ADAM_LR * (m_hat / (jnp.sqrt(v_hat) + ADAM_EPS) + ADAM_WD * w), m, v


def _adamw(w, g, m, v, name):
    R, C = w.shape
    tr = 256 if R % 256 == 0 and R > 256 else R

    def body(w_ref, g_ref, m_ref, v_ref, d_ref, nm_ref, nv_ref):
        d_ref[...], nm_ref[...], nv_ref[...] = _adamw_math(w_ref[...], g_ref[...], m_ref[...], v_ref[...])

    spec = pl.BlockSpec((tr, C), lambda i: (i, 0))
    return pl.pallas_call(
        body, name=name, grid=(R // tr,), in_specs=[spec] * 4, out_specs=[spec] * 3,
        out_shape=[jax.ShapeDtypeStruct((R, C), F32)] * 3, compiler_params=_cp(("parallel",)))(w, g, m, v)


def _small_update(gathered, w, m, v):
    def body(s_ref, w_ref, m_ref, v_ref, g_ref, d_ref, nm_ref, nv_ref):
        g = s_ref[0]
        for i in range(1, N_DEV):
            g = g + s_ref[i]
        g_ref[...] = g
        d_ref[...], nm_ref[...], nv_ref[...] = _adamw_math(w_ref[...], g, m_ref[...], v_ref[...])

    return pl.pallas_call(body, name="small_update", out_shape=[jax.ShapeDtypeStruct(w.shape, F32)] * 4)(gathered, w, m, v)


LANES = 128
BIG = (("w_in", 8480, "cols"), ("w_branch_ssm", 2048, "rows"), ("w_branch_attn", 1024, "rows"),
       ("w_mix_out", 1024, "rows"), ("w_up", 5632, "cols"), ("w_down", 2816, "rows"))
CONVW = (("ssm_conv_w", 16), ("ffn_conv_w", 24))
SMALL = ("rel_bias", "b_gate", "ssm_conv_b", "ssm_dt_bias", "ssm_a_log", "ssm_d", "ssm_norm_w", "attn_sinks",
         "ln1_g", "ln1_b", "ffn_conv_b", "ln2_g", "ln2_b")
WEIGHTS = ("rel_bias", "w_in", "b_gate", "ssm_conv_w", "ssm_conv_b", "ssm_dt_bias", "ssm_a_log", "ssm_d", "ssm_norm_w",
           "attn_sinks", "w_branch_ssm", "w_branch_attn", "w_mix_out", "ln1_g", "ln1_b", "w_up", "ffn_conv_w",
           "ffn_conv_b", "w_down", "ln2_g", "ln2_b")
EXCHANGE_ROWS = 21504
SMALL_ROWS = 144


def _rows(a, rows):
    flat = a.reshape(-1)
    return jnp.pad(flat, (0, rows * LANES - flat.shape[0])).reshape(rows, LANES)


def _rows8(a, rows):
    return jnp.pad(a, ((0, 0), (0, rows * LANES - a.shape[1]))).reshape(N_DEV, rows, LANES)


def _by_device(full, how):
    r, c = full.shape
    if how == "rows":
        return full.reshape(N_DEV, -1)
    return full.reshape(r, N_DEV, c // N_DEV).transpose(1, 0, 2).reshape(N_DEV, -1)


def _from_devices(slots, r, c, how):
    if how == "rows":
        return slots.reshape(r, c)
    return slots.reshape(N_DEV, r, c // N_DEV).transpose(1, 0, 2).reshape(r, c)


def kernel(x, rel_bias, w_in, b_gate, ssm_conv_w, ssm_conv_b, ssm_dt_bias, ssm_a_log, ssm_d, ssm_norm_w, attn_sinks, w_branch_ssm, w_branch_attn, w_mix_out, ln1_g, ln1_b, w_up, ffn_conv_w, ffn_conv_b, w_down, ln2_g, ln2_b, loss_target, m_rel_bias, m_w_in, m_b_gate, m_ssm_conv_w, m_ssm_conv_b, m_ssm_dt_bias, m_ssm_a_log, m_ssm_d, m_ssm_norm_w, m_attn_sinks, m_w_branch_ssm, m_w_branch_attn, m_w_mix_out, m_ln1_g, m_ln1_b, m_w_up, m_ffn_conv_w, m_ffn_conv_b, m_w_down, m_ln2_g, m_ln2_b, v_rel_bias, v_w_in, v_b_gate, v_ssm_conv_w, v_ssm_conv_b, v_ssm_dt_bias, v_ssm_a_log, v_ssm_d, v_ssm_norm_w, v_attn_sinks, v_w_branch_ssm, v_w_branch_attn, v_w_mix_out, v_ln1_g, v_ln1_b, v_w_up, v_ffn_conv_w, v_ffn_conv_b, v_w_down, v_ln2_g, v_ln2_b):
    W = dict(zip(WEIGHTS, (rel_bias, w_in, b_gate, ssm_conv_w, ssm_conv_b, ssm_dt_bias, ssm_a_log, ssm_d, ssm_norm_w,
                           attn_sinks, w_branch_ssm, w_branch_attn, w_mix_out, ln1_g, ln1_b, w_up, ffn_conv_w,
                           ffn_conv_b, w_down, ln2_g, ln2_b)))
    M = dict(zip(WEIGHTS, (m_rel_bias, m_w_in, m_b_gate, m_ssm_conv_w, m_ssm_conv_b, m_ssm_dt_bias, m_ssm_a_log, m_ssm_d,
                           m_ssm_norm_w, m_attn_sinks, m_w_branch_ssm, m_w_branch_attn, m_w_mix_out, m_ln1_g, m_ln1_b,
                           m_w_up, m_ffn_conv_w, m_ffn_conv_b, m_w_down, m_ln2_g, m_ln2_b)))
    V = dict(zip(WEIGHTS, (v_rel_bias, v_w_in, v_b_gate, v_ssm_conv_w, v_ssm_conv_b, v_ssm_dt_bias, v_ssm_a_log, v_ssm_d,
                           v_ssm_norm_w, v_attn_sinks, v_w_branch_ssm, v_w_branch_attn, v_w_mix_out, v_ln1_g, v_ln1_b,
                           v_w_up, v_ffn_conv_w, v_ffn_conv_b, v_w_down, v_ln2_g, v_ln2_b)))
    shard2d = lambda a: a.reshape(a.shape[-2], a.shape[-1])

    def unpack(landed, entries):
        full, off = {}, 0
        for n, rows, how in entries:
            r, c = shard2d(W[n]).shape
            fr, fc = (r * N_DEV, c) if how == "rows" else (r, c * N_DEV)
            full[n] = _from_devices(landed[:, off:off + rows].reshape(N_DEV, -1)[:, :r * c], fr, fc, how)
            off += rows
        return full

    conv_entries = tuple((n, rows, "cols") for n, rows in CONVW)
    win_shard = _rows(shard2d(w_in).astype(BF16), BIG[0][1])
    late_shard = jnp.concatenate([_rows(shard2d(W[n]).astype(BF16), rows) for n, rows, _ in BIG[1:]], axis=0)
    conv_shard = jnp.concatenate([_rows(shard2d(W[n]), rows) for n, rows in CONVW], axis=0)
    main, tail = _split_w_in(unpack(_all_gather(win_shard, "gather_w_in"), BIG[:1])["w_in"])
    conv_full = unpack(_all_gather(conv_shard, "gather_conv_weights"), conv_entries)
    short = {"w_branch_ssm": "bs", "w_branch_attn": "ba", "w_mix_out": "mo", "w_up": "up", "w_down": "down"}
    late = lambda landed: {short[n]: v for n, v in unpack(landed, BIG[1:]).items()}
    p = _prep_params(rel_bias, b_gate, conv_full["ssm_conv_w"], ssm_conv_b, ssm_dt_bias, ssm_a_log, ssm_d, ssm_norm_w,
                     attn_sinks, ln1_g, ln1_b, conv_full["ffn_conv_w"], ffn_conv_b, ln2_g, ln2_b)
    sq, dproj, w, g = _local_step(x[0], loss_target[0], {"in_main": main, "in_tail": tail}, p, (late_shard, late))
    g["w_in"] = _join_w_in(g.pop("in_main"), g.pop("in_tail"))
    loss = (0.5 / D_MODEL) * lax.psum(sq[0, 0], ("x", "y", "c"))

    parts = [_rows8(_by_device(g[n], how), rows) for n, rows, how in BIG]
    parts += [_rows8(_by_device(g[n], "cols"), rows) for n, rows in CONVW]
    used = sum(rows for _, rows, _ in BIG) + sum(rows for _, rows in CONVW)
    parts.append(jnp.zeros((N_DEV, EXCHANGE_ROWS - used, LANES), F32))
    dx, received = _grad_x(dproj, w, exchange=jnp.concatenate(parts, axis=1).astype(BF16))
    g_rows = _slot_sum(received)
    grads, deltas, new_m, new_v = {}, {}, {}, {}
    off = 0
    for n, rows in [(n, rows) for n, rows, _ in BIG] + list(CONVW):
        r, c = shard2d(W[n]).shape
        gn = g_rows[off:off + rows].reshape(-1)[:r * c].reshape(r, c)
        off += rows
        d, nm, nv = _adamw(shard2d(W[n]), gn, shard2d(M[n]), shard2d(V[n]), "adamw_" + n)
        grads[n], deltas[n], new_m[n], new_v[n] = (a.reshape(W[n].shape) for a in (gn, d, nm, nv))

    pack = lambda src: _rows(jnp.concatenate([src[n].reshape(-1) for n in SMALL]), SMALL_ROWS)
    small_all = _all_gather(pack(g), "gather_small_grads")
    outs = _small_update(small_all, pack(W), pack(M), pack(V))
    off = 0
    for n in SMALL:
        size = W[n].size
        grads[n], deltas[n], new_m[n], new_v[n] = (a.reshape(-1)[off:off + size].reshape(W[n].shape) for a in outs)
        off += size

    return (loss, dx[None], *[grads[n] for n in WEIGHTS], *[deltas[n] for n in WEIGHTS],
            *[new_m[n] for n in WEIGHTS], *[new_v[n] for n in WEIGHTS])
```

```python
import functools
import math

import jax
import jax.numpy as jnp
from jax import lax
from jax.experimental import pallas as pl
from jax.experimental.pallas import tpu as pltpu

F32, BF16 = jnp.float32, jnp.bfloat16
HIGHEST = lax.Precision.HIGHEST
MESH_ID = pl.DeviceIdType.MESH

N_DEV = 8
D_MODEL = 1024
SSM_INNER = 2048
SSM_HEADS = 32
SSM_HEADDIM = 64
SSMD = SSM_HEADDIM
SSM_GROUPS = 4
SSM_GROUP_COLS = SSM_INNER // SSM_GROUPS
SSM_STATE = 128
SSM_CONV = 4
CHUNK = 128
XBC_COLS = SSM_INNER + 2 * SSM_GROUPS * SSM_STATE
B_OFF = SSM_INNER
C_OFF = SSM_INNER + SSM_GROUPS * SSM_STATE
ATTN_HEADS = 16
ATTN_KV = 2
ATTN_GROUP = 8
HEADDIM = 64
WINDOW = 128
REL_BUCKETS = 32
REL_MAX_DIST = 128
D_FF = 2816
FFN_CONV = 3
ALPHA = 2.0 ** 0.25
LN_EPS = 1e-5
RMS_EPS = 1e-5
IN_COLS = 8480
Z_OFF, XBC_OFF, Q_OFF, GATE_OFF, MAIN_COLS = 0, 2048, 5120, 6144, 8192
K_OFF, V_OFF, DT_OFF, TAIL_COLS = 0, 128, 256, 384
O_Z, O_XBC, O_DT, O_Q, O_K, O_V, O_GATE = 0, 2048, 5120, 5152, 6176, 6304, 6432

ADAM_LR, ADAM_B1, ADAM_B2, ADAM_EPS, ADAM_WD, ADAM_STEP = 0.001, 0.9, 0.999, 1e-08, 0.01, 10
NEG = -1e30
HALO = 8
VMEM_LIMIT = 56 * 1024 * 1024


def _cp(sem):
    return pltpu.CompilerParams(dimension_semantics=sem, vmem_limit_bytes=VMEM_LIMIT)


def _const_spec(shape):
    nd = len(shape)
    return pl.BlockSpec(shape, lambda *_: (0,) * nd)


def _sigmoid(x):
    return 0.5 * jnp.tanh(0.5 * x) + 0.5


def _softplus(x):
    return jnp.maximum(x, 0.0) + jnp.log1p(jnp.exp(-jnp.abs(x)))


def _dot(a, b, dims=(((1,), (0,)), ((), ())), precision=None):
    return lax.dot_general(a, b, dims, preferred_element_type=F32, precision=precision)


NN = (((1,), (0,)), ((), ()))
NT = (((1,), (1,)), ((), ()))
TN = (((0,), (0,)), ((), ()))


def _mesh_pos():
    return lax.axis_index("x"), lax.axis_index("y"), lax.axis_index("c")


PEERS = N_DEV - 1


def _exchange_phases(in_refs, out_refs, send_sems, recv_sems, local_sems):
    def copies():
        x, y, c = _mesh_pos()
        me = 4 * x + 2 * y + c
        cps = []
        for b, (in_ref, out_ref) in enumerate(zip(in_refs, out_refs)):
            cps.append(pltpu.make_async_copy(in_ref.at[me], out_ref.at[me], local_sems.at[b]))
            for r in range(1, N_DEV):
                px = 1 - x if r & 4 else x
                py = 1 - y if r & 2 else y
                pc = 1 - c if r & 1 else c
                cps.append(pltpu.make_async_remote_copy(
                    src_ref=in_ref.at[4 * px + 2 * py + pc], dst_ref=out_ref.at[me],
                    send_sem=send_sems.at[b * PEERS + r - 1], recv_sem=recv_sems.at[b * PEERS + r - 1],
                    device_id=(px, py, pc), device_id_type=MESH_ID))
        return cps

    def start():
        for cp in copies():
            cp.start()

    def finish():
        for cp in copies():
            cp.wait()

    return [start, finish]


def _gather_phases(x_refs, out_refs, send_sems, recv_sems, local_sems):
    def parts(which):
        x, y, c = _mesh_pos()
        me, sibling = (x, y, c), (x, y, 1 - c)
        chips = [(1 - x, y), (x, 1 - y), (1 - x, 1 - y)]
        found = []
        for b, (x_ref, out_ref) in enumerate(zip(x_refs, out_refs)):
            def slot(px, py, pc):
                return out_ref.at[4 * px + 2 * py + pc]

            def copy(k, block, to, src=None):
                return pltpu.make_async_remote_copy(
                    src_ref=slot(*block) if src is None else src, dst_ref=slot(*block),
                    send_sem=send_sems.at[b * PEERS + k], recv_sem=recv_sems.at[b * PEERS + k],
                    device_id=to, device_id_type=MESH_ID)

            if which == "mine":
                found.append(pltpu.make_async_copy(x_ref, slot(*me), local_sems.at[b]))
            elif which == "first":
                found.append(copy(0, me, sibling, src=x_ref))
                found += [copy(1 + j, me, (*chip, c), src=x_ref) for j, chip in enumerate(chips)]
            elif which == "passed":
                found += [copy(4 + j, (*chip, c), sibling) for j, chip in enumerate(chips)]
            elif which == "arrived":
                found += [copy(1 + j, (*chip, c), me) for j, chip in enumerate(chips)]
            else:
                found.append(copy(0, sibling, me))
                found += [copy(4 + j, (*chip, 1 - c), me) for j, chip in enumerate(chips)]
        return found

    def start():
        for cp in parts("mine") + parts("first"):
            cp.start()

    def forward():
        for a, p in zip(parts("arrived"), parts("passed")):
            a.wait_recv()
            p.start()

    def finish():
        for cp in parts("late"):
            cp.wait_recv()
        for cp in parts("first") + parts("passed"):
            cp.wait_send()
        for cp in parts("mine"):
            cp.wait()

    return [start, forward, finish]


COMM = {"exchange": _exchange_phases, "gather": _gather_phases}


def _comm_sems(nb):
    return [pltpu.SemaphoreType.DMA((nb * PEERS,)), pltpu.SemaphoreType.DMA((nb * PEERS,)), pltpu.SemaphoreType.DMA((nb,))]


def _matmul(a, b, mode, out_dtype, name, tm=512, tn=1024, tk=1024, addend=None, addend_scale=1.0, comm=None):
    bufs = [] if comm is None else list(comm[1])
    nb = len(bufs)
    if mode == "nn":
        (M, K), (K2, N) = a.shape, b.shape
    elif mode == "nt":
        (M, K), (N, K2) = a.shape, b.shape
    else:
        (K, M), (K2, N) = a.shape, b.shape
    assert K == K2, (a.shape, b.shape, mode)
    tm, tn, tk = min(tm, M), min(tn, N), min(tk, K)
    assert M % tm == 0 and N % tn == 0 and K % tk == 0, (M, N, K, tm, tn, tk)
    nk = K // tk
    dims = {"nn": NN, "nt": NT, "tn": TN}[mode]
    a_spec = pl.BlockSpec((tk, tm), lambda i, j, k: (k, i)) if mode == "tn" else pl.BlockSpec((tm, tk), lambda i, j, k: (i, k))
    b_spec = pl.BlockSpec((tn, tk), lambda i, j, k: (j, k)) if mode == "nt" else pl.BlockSpec((tk, tn), lambda i, j, k: (k, j))
    o_spec = pl.BlockSpec((tm, tn), lambda i, j, k: (i, j))

    ni, nj = M // tm, N // tn

    def body(*refs):
        refs = list(refs)
        a_ref, b_ref = refs[:2]
        c_ref = refs[2] if addend is not None else None
        n_in = 2 + (addend is not None) + nb
        o_ref, acc = refs[n_in], refs[n_in + 1 + nb]
        i, j, k = pl.program_id(0), pl.program_id(1), pl.program_id(2)
        step = (i * nj + j) * nk + k
        if comm is not None:
            phases = COMM[comm[0]](refs[n_in - nb:n_in], refs[n_in + 1:n_in + 1 + nb], *refs[n_in + 2 + nb:])
            at = [(ni * nj * nk - 1) * p // (len(phases) - 1) for p in range(len(phases))]
            for when, phase in zip(at[:-1], phases[:-1]):
                pl.when(step == when)(phase)

        d = _dot(a_ref[...].astype(BF16), b_ref[...].astype(BF16), dims)

        def finish(r):
            if addend is not None:
                r = r + addend_scale * c_ref[...].astype(F32)
            o_ref[...] = r.astype(out_dtype)

        if nk == 1:
            finish(d)
        else:
            @pl.when(k == 0)
            def _():
                acc[...] = d

            @pl.when(jnp.logical_and(k > 0, k < nk - 1))
            def _():
                acc[...] += d

            @pl.when(k == nk - 1)
            def _():
                finish(acc[...] + d)

        if comm is not None:
            pl.when(step == at[-1])(phases[-1])

    in_specs = [a_spec, b_spec] + ([o_spec] if addend is not None else [])
    args = (a, b) + ((addend,) if addend is not None else ())
    out_specs, out_shape = o_spec, jax.ShapeDtypeStruct((M, N), out_dtype)
    scratch = [pltpu.VMEM((tm, tn), F32)]
    sem = ("parallel", "parallel", "arbitrary")
    if comm is not None:
        any_spec = pl.BlockSpec(memory_space=pl.ANY)
        in_specs, args = in_specs + [any_spec] * nb, args + tuple(bufs)
        landed = [x.shape if comm[0] == "exchange" else (N_DEV,) + x.shape for x in bufs]
        out_specs = [o_spec] + [any_spec] * nb
        out_shape = [out_shape] + [jax.ShapeDtypeStruct(s, x.dtype) for s, x in zip(landed, bufs)]
        scratch += _comm_sems(nb)
        sem = ("arbitrary", "arbitrary", "arbitrary")
    return pl.pallas_call(
        body, name=name, grid=(ni, nj, nk), in_specs=in_specs, out_specs=out_specs, out_shape=out_shape,
        scratch_shapes=scratch, compiler_params=_cp(sem))(*args)


def _taps(w_ref, K, tc):
    return [jnp.broadcast_to(w_ref[k:k + 1, :], (HALO, tc)) for k in range(K)]


def _conv_fwd(pre, pre_col_off, C, w, b, K, act, name, tr=1024, tc=512):
    T = pre.shape[0]
    tr, tc = min(tr, T), min(tc, C)
    assert T % tr == 0 and C % tc == 0 and pre_col_off % tc == 0
    joff = pre_col_off // tc
    hb = tr // HALO

    def body(x_ref, xp_ref, w_ref, b_ref, o_ref, head):
        i = pl.program_id(1)
        head[0:HALO, :] = jnp.where(i > 0, xp_ref[...], 0.0)
        head[HALO:, :] = x_ref[0:HALO, :]
        wk = _taps(w_ref, K, tc)
        bias = jnp.broadcast_to(b_ref[...], (HALO, tc))
        for r in range(tr // HALO):
            lo = r * HALO
            acc = bias + wk[K - 1] * x_ref[lo:lo + HALO, :]
            for k in range(K - 1):
                s = K - 1 - k
                acc = acc + wk[k] * (head[HALO - s:2 * HALO - s, :] if r == 0 else x_ref[lo - s:lo + HALO - s, :])
            if act == "silu":
                acc = acc * _sigmoid(acc)
            o_ref[lo:lo + HALO, :] = acc

    return pl.pallas_call(
        body, name=name, grid=(C // tc, T // tr),
        in_specs=[pl.BlockSpec((tr, tc), lambda j, i: (i, joff + j)),
                  pl.BlockSpec((HALO, tc), lambda j, i: (jnp.maximum(i * hb - 1, 0), joff + j)),
                  pl.BlockSpec((K, tc), lambda j, i: (0, j)),
                  pl.BlockSpec((1, tc), lambda j, i: (0, j))],
        out_specs=pl.BlockSpec((tr, tc), lambda j, i: (i, j)),
        out_shape=jax.ShapeDtypeStruct((T, C), F32),
        scratch_shapes=[pltpu.VMEM((2 * HALO, tc), F32)],
        compiler_params=_cp(("parallel", "arbitrary")))(pre, pre, w, b)


def _conv_bwd(dout, pre, pre_col_off, C, w, b, K, act, dst, dst_col_off, name, tr=1024, tc=512):
    T = pre.shape[0]
    tr, tc = min(tr, T), min(tc, C)
    assert T % tr == 0 and C % tc == 0 and pre_col_off % tc == 0 and dst_col_off % tc == 0
    joff, doff = pre_col_off // tc, dst_col_off // tc
    hb = tr // HALO
    nt = T // tr
    n = tr // HALO
    last_hblock = T // HALO - 1

    def body(g_ref, gn_ref, x_ref, xp_ref, xn_ref, w_ref, b_ref, *rest):
        o_ref, dw_ref, db_ref, head, tail, gext = rest[-6:]
        i = pl.program_id(1)
        not_last = i < nt - 1
        head[0:HALO, :] = jnp.where(i > 0, xp_ref[...], 0.0)
        head[HALO:, :] = x_ref[0:HALO, :]
        tail[0:HALO, :] = x_ref[tr - HALO:tr, :]
        tail[HALO:, :] = jnp.where(not_last, xn_ref[...], 0.0)
        wk = _taps(w_ref, K, tc)

        def x_at(r, s):
            if r == 0:
                return head[HALO - s:2 * HALO - s, :]
            if r == n:
                return tail[HALO - s:2 * HALO - s, :]
            return x_ref[r * HALO - s:(r + 1) * HALO - s, :]

        acc_w = [jnp.zeros((HALO, tc), F32) for _ in range(K)]
        acc_b = jnp.zeros((HALO, tc), F32)
        gn = jnp.where(not_last, gn_ref[...], 0.0)
        if act == "silu":
            for r in range(n + 1):
                xs = [x_at(r, K - 1 - k) for k in range(K)]
                co = jnp.broadcast_to(b_ref[...], (HALO, tc))
                for k in range(K):
                    co = co + wk[k] * xs[k]
                sg = _sigmoid(co)
                gext[r * HALO:(r + 1) * HALO, :] = (g_ref[r * HALO:(r + 1) * HALO, :] if r < n else gn) * (
                    sg * (1.0 + co * (1.0 - sg)))
            src = gext
        else:
            gext[0:HALO, :] = g_ref[tr - HALO:tr, :]
            gext[HALO:2 * HALO, :] = gn
            src = g_ref
        for r in range(n):
            lo = r * HALO
            x = x_ref[lo:lo + HALO, :]
            dpre = None
            for s in range(K):
                if act != "silu" and r == n - 1 and s > 0:
                    gs = gext[s:HALO + s, :]
                else:
                    gs = src[lo + s:lo + HALO + s, :]
                dpre = wk[K - 1 - s] * gs if dpre is None else dpre + wk[K - 1 - s] * gs
                acc_w[K - 1 - s] = acc_w[K - 1 - s] + gs * x
                if s == 0:
                    acc_b = acc_b + gs
            o_ref[lo:lo + HALO, :] = dpre.astype(o_ref.dtype)

        @pl.when(i == 0)
        def _():
            dw_ref[...] = jnp.zeros_like(dw_ref)
            db_ref[...] = jnp.zeros_like(db_ref)

        db_ref[...] += jnp.sum(acc_b, axis=0, keepdims=True)
        dw_ref[...] += jnp.concatenate([jnp.sum(a, axis=0, keepdims=True) for a in acc_w], axis=0)

    tile = lambda off: pl.BlockSpec((tr, tc), lambda j, i: (i, off + j))
    nxt = lambda off: pl.BlockSpec((HALO, tc), lambda j, i: (jnp.minimum((i + 1) * hb, last_hblock), off + j))
    in_specs = [tile(0), nxt(0), tile(joff),
                pl.BlockSpec((HALO, tc), lambda j, i: (jnp.maximum(i * hb - 1, 0), joff + j)), nxt(joff),
                pl.BlockSpec((K, tc), lambda j, i: (0, j)), pl.BlockSpec((1, tc), lambda j, i: (0, j))]
    args = (dout, dout, pre, pre, pre, w, b)
    if isinstance(dst, jax.ShapeDtypeStruct):
        aliases = {}
    else:
        in_specs.append(pl.BlockSpec(memory_space=pl.ANY))
        args += (dst,)
        aliases = {7: 0}
    return pl.pallas_call(
        body, name=name, grid=(C // tc, nt), in_specs=in_specs,
        out_specs=[tile(doff), pl.BlockSpec((K, tc), lambda j, i: (0, j)), pl.BlockSpec((1, tc), lambda j, i: (0, j))],
        out_shape=[jax.ShapeDtypeStruct(dst.shape, dst.dtype), jax.ShapeDtypeStruct((K, C), F32),
                   jax.ShapeDtypeStruct((1, C), F32)],
        scratch_shapes=[pltpu.VMEM((2 * HALO, tc), F32), pltpu.VMEM((2 * HALO, tc), F32),
                        pltpu.VMEM((tr + HALO, tc), F32)],
        input_output_aliases=aliases,
        compiler_params=_cp(("parallel", "arbitrary")))(*args)


PAIR = 2 * SSMD
PAIRS_PER_GROUP = SSM_GROUP_COLS // PAIR


def _dot3(x, onehot):
    h1 = x.astype(BF16)
    r = x - h1.astype(F32)
    h2 = r.astype(BF16)
    h3 = (r - h2.astype(F32)).astype(BF16)
    return _dot(h1, onehot) + _dot(h2, onehot) + _dot(h3, onehot)


def _chunk_rows(dt_raw, dtb_col, alog_col):
    row = lax.broadcasted_iota(jnp.int32, (CHUNK, CHUNK), 0)
    col = lax.broadcasted_iota(jnp.int32, (CHUNK, CHUNK), 1)
    dt_rawT = dt_raw.T
    dtT = _softplus(dt_rawT + dtb_col)
    a_col = -jnp.exp(alog_col)
    acsT = _dot3(dtT * a_col, (row <= col).astype(BF16))
    return dt_rawT, dtT, a_col, acsT, row, col


def _block_diag(x, left):
    return jnp.concatenate([jnp.where(left, x, 0.0), jnp.where(left, 0.0, x)], axis=0).astype(BF16)


def _lane_bcast(v, h):
    return jnp.broadcast_to(v[:, h:h + 1], (CHUNK, CHUNK))


def _ssd_fwd(xbc, proj_main, proj_tail, dtb_col, alog_col, d_exp, norm_w):
    T = xbc.shape[0]
    nc = T // CHUNK

    def body(xbc_ref, dt_ref, z_ref, dtb_ref, alog_ref, d_ref, nw_ref, y_ref, ypre_ref, hs_ref, H):
        c = pl.program_id(0)

        @pl.when(c == 0)
        def _():
            H[...] = jnp.zeros_like(H)

        hs_ref[0] = H[...]
        _, dtT, _, acsT, row, col = _chunk_rows(dt_ref[:, 0:SSM_HEADS], dtb_ref[...], alog_ref[...])
        tril, left = row >= col, col < SSMD
        acs = acsT.T
        w = (dtT * jnp.exp(acsT[:, CHUNK - 1:CHUNK] - acsT)).T
        cd = jnp.exp(acs[CHUNK - 1:CHUNK, :])
        for g in range(SSM_GROUPS):
            gs = slice(g * SSM_GROUP_COLS, (g + 1) * SSM_GROUP_COLS)
            Bb = xbc_ref[:, B_OFF + g * SSM_STATE:B_OFF + (g + 1) * SSM_STATE].astype(BF16)
            Cb = xbc_ref[:, C_OFF + g * SSM_STATE:C_OFF + (g + 1) * SSM_STATE].astype(BF16)
            Hg = H[:, gs]
            CH = _dot(Cb, Hg.astype(BF16))
            CB = _dot(Cb, Bb, NT)
            ys, xws = [], []
            for kk in range(PAIRS_PER_GROUP):
                k = g * PAIRS_PER_GROUP + kk
                xs_p = xbc_ref[:, k * PAIR:(k + 1) * PAIR]
                mps, ecols, wcols = [], [], []
                for j in range(2):
                    h = 2 * k + j
                    colb = _lane_bcast(acs, h)
                    L = jnp.exp(jnp.where(tril, colb - acsT[h:h + 1, :], -jnp.inf))
                    mps.append((CB * L * dtT[h:h + 1, :]).astype(BF16))
                    ecols.append(jnp.exp(colb))
                    wcols.append(_lane_bcast(w, h))
                yd = _dot(jnp.concatenate(mps, axis=1), _block_diag(xs_p, left))
                ys.append(yd + CH[:, kk * PAIR:(kk + 1) * PAIR] * jnp.where(left, ecols[0], ecols[1]))
                xws.append((xs_p * jnp.where(left, wcols[0], wcols[1])).astype(BF16))
            cd_e = jnp.concatenate([jnp.broadcast_to(cd[:, g * 8 + e:g * 8 + e + 1], (1, SSMD)) for e in range(8)], axis=1)
            H[:, gs] = Hg * cd_e + _dot(Bb, jnp.concatenate(xws, axis=1), TN)
            ypre = jnp.concatenate(ys, axis=1) + xbc_ref[:, gs] * d_ref[:, gs]
            ypre_ref[:, gs] = ypre
            z = z_ref[:, gs]
            yg = ypre * (z * _sigmoid(z))
            r = lax.rsqrt(jnp.mean(yg * yg, axis=1, keepdims=True) + RMS_EPS)
            y_ref[:, gs] = (yg * r * nw_ref[:, gs]).astype(BF16)

    vec = lambda n: _const_spec((1, n))
    colv = _const_spec((SSM_HEADS, 1))
    return pl.pallas_call(
        body, name="ssd_fwd", grid=(nc,),
        in_specs=[pl.BlockSpec((CHUNK, XBC_COLS), lambda c: (c, 0)),
                  pl.BlockSpec((CHUNK, 128), lambda c: (c, DT_OFF // 128)),
                  pl.BlockSpec((CHUNK, SSM_INNER), lambda c: (c, Z_OFF // SSM_INNER)),
                  colv, colv, vec(SSM_INNER), vec(SSM_INNER)],
        out_specs=[pl.BlockSpec((CHUNK, SSM_INNER), lambda c: (c, 0)),
                   pl.BlockSpec((CHUNK, SSM_INNER), lambda c: (c, 0)),
                   pl.BlockSpec((1, SSM_STATE, SSM_INNER), lambda c: (c, 0, 0))],
        out_shape=[jax.ShapeDtypeStruct((T, SSM_INNER), BF16), jax.ShapeDtypeStruct((T, SSM_INNER), F32),
                   jax.ShapeDtypeStruct((nc, SSM_STATE, SSM_INNER), F32)],
        scratch_shapes=[pltpu.VMEM((SSM_STATE, SSM_INNER), F32)],
        compiler_params=_cp(("arbitrary",)))(xbc, proj_tail, proj_main, dtb_col, alog_col, d_exp, norm_w)


def _ssd_bwd(dyo, ypre, xbc, hs, proj_main, proj_tail, dtb_col, alog_col, d_exp, norm_w, ehead_t, dmain, dtail):
    T = xbc.shape[0]
    nc = T // CHUNK

    def body(dyo_ref, ypre_ref, xbc_ref, hs_ref, dt_ref, z_ref, dtb_ref, alog_ref, d_ref, nw_ref, eh_ref,
             dmain_in, dtail_in, dz_ref, ddt_ref, dxbc_ref, dnw_ref, dd_ref, dalog_ref, ddtb_ref, G):
        del dmain_in, dtail_in
        c = pl.program_id(0)

        @pl.when(c == 0)
        def _():
            G[...] = jnp.zeros_like(G)
            dnw_ref[...] = jnp.zeros_like(dnw_ref)
            dd_ref[...] = jnp.zeros_like(dd_ref)
            dalog_ref[...] = jnp.zeros_like(dalog_ref)
            ddtb_ref[...] = jnp.zeros_like(ddtb_ref)

        dt_rawT, dtT, a_col, acsT, row, col = _chunk_rows(dt_ref[:, 0:SSM_HEADS], dtb_ref[...], alog_ref[...])
        tril, triu, left = row >= col, col >= row, col < SSMD
        acs = acsT.T
        dt = dtT.T
        lastT = acsT[:, CHUNK - 1:CHUNK]
        dstT = jnp.exp(lastT - acsT)
        wT = dtT * dstT
        cd = jnp.exp(acs[CHUNK - 1:CHUNK, :])
        ddt_rows, rs_rows, deo_rows, dw_rows = [], [], [], []
        dd_cols, gh_cols, dnw_cols = [], [], []
        for g in range(SSM_GROUPS):
            gs = slice(g * SSM_GROUP_COLS, (g + 1) * SSM_GROUP_COLS)
            z = z_ref[:, gs]
            sz = _sigmoid(z)
            silu_z = z * sz
            ypre = ypre_ref[:, gs]
            yg = ypre * silu_z
            r = lax.rsqrt(jnp.mean(yg * yg, axis=1, keepdims=True) + RMS_EPS)
            ygn = yg * r
            dyo = dyo_ref[:, gs]
            dyn = dyo * nw_ref[:, gs]
            dnw_cols.append(jnp.sum(dyo * ygn, axis=0, keepdims=True))
            dyg = r * (dyn - ygn * jnp.mean(dyn * ygn, axis=1, keepdims=True))
            dz_ref[:, gs] = (dyg * ypre * (sz * (1.0 + z * (1.0 - sz)))).astype(dz_ref.dtype)
            dY = dyg * silu_z
            xs = xbc_ref[:, gs]
            dd_cols.append(jnp.sum(dY * xs, axis=0, keepdims=True))
            Bf = xbc_ref[:, B_OFF + g * SSM_STATE:B_OFF + (g + 1) * SSM_STATE]
            Cf = xbc_ref[:, C_OFF + g * SSM_STATE:C_OFF + (g + 1) * SSM_STATE]
            Bb, Cb = Bf.astype(BF16), Cf.astype(BF16)
            BT, CT = Bf.T, Cf.T
            CB = _dot(Cb, Bb, NT)
            CBT = _dot(Bb, Cb, NT)
            Hg = hs_ref[0, :, gs]
            Gg = G[:, gs]
            gh_cols.append(jnp.sum(Gg * Hg, axis=0, keepdims=True))
            dCB = jnp.zeros((CHUNK, CHUNK), F32)
            dxs_d, dyes, xws, wsels = [], [], [], []
            for kk in range(PAIRS_PER_GROUP):
                k = g * PAIRS_PER_GROUP + kk
                ps = slice(kk * PAIR, (kk + 1) * PAIR)
                xs_p, dY_p = xs[:, ps], dY[:, ps]
                Ls, LTs, dtcols, ecols, wcols = [], [], [], [], []
                for j in range(2):
                    h = 2 * k + j
                    colb = _lane_bcast(acs, h)
                    seg = colb - acsT[h:h + 1, :]
                    Ls.append(jnp.exp(jnp.where(tril, seg, -jnp.inf)))
                    LTs.append(jnp.exp(jnp.where(triu, -seg, -jnp.inf)))
                    dtcol = _lane_bcast(dt, h)
                    dtcols.append(dtcol)
                    ecols.append(jnp.exp(colb))
                    wcols.append(dtcol * jnp.exp(acs[CHUNK - 1:CHUNK, h:h + 1] - colb))
                wsel = jnp.where(left, wcols[0], wcols[1])
                dYe_p = dY_p * jnp.where(left, ecols[0], ecols[1])
                bdx = _block_diag(xs_p, left)
                bddy = _block_diag(dY_p, left)
                dMx2 = _dot(dY_p.astype(BF16), bdx, NT)
                dMxT2 = _dot(xs_p.astype(BF16), bddy, NT)
                Q1 = _dot(Hg[:, ps].astype(BF16), _block_diag(dYe_p, left), NT)
                Q2 = _dot(Gg[:, ps].astype(BF16), bdx, NT)
                mts = []
                for j in range(2):
                    h = 2 * k + j
                    js = slice(j * CHUNK, (j + 1) * CHUNK)
                    dMx = dMx2[:, js]
                    A = CB * Ls[j]
                    AT = CBT * LTs[j]
                    ddt_rows.append(jnp.sum(A * dMx, axis=0, keepdims=True))
                    ATd = AT * dtcols[j]
                    rs_rows.append(jnp.sum(ATd * dMxT2[:, js], axis=0, keepdims=True))
                    dCB = dCB + dMx * Ls[j] * dtT[h:h + 1, :]
                    mts.append(ATd.astype(BF16))
                    deo_rows.append(jnp.sum(CT * Q1[:, js], axis=0, keepdims=True))
                    dw_rows.append(jnp.sum(BT * Q2[:, js], axis=0, keepdims=True))
                dxs_d.append(_dot(jnp.concatenate(mts, axis=1), bddy))
                dyes.append(dYe_p.astype(BF16))
                xws.append((xs_p * wsel).astype(BF16))
                wsels.append(wsel)
            dYe_g = jnp.concatenate(dyes, axis=1)
            xw_g = jnp.concatenate(xws, axis=1)
            Hgb, Ggb, dCBb = Hg.astype(BF16), Gg.astype(BF16), dCB.astype(BF16)
            dxbc_ref[:, C_OFF + g * SSM_STATE:C_OFF + (g + 1) * SSM_STATE] = _dot(dYe_g, Hgb, NT) + _dot(dCBb, Bb)
            dxbc_ref[:, B_OFF + g * SSM_STATE:B_OFF + (g + 1) * SSM_STATE] = _dot(xw_g, Ggb, NT) + _dot(dCBb, Cb, TN)
            BG = _dot(Bb, Ggb)
            dxbc_ref[:, gs] = (jnp.concatenate(dxs_d, axis=1) + BG * jnp.concatenate(wsels, axis=1)
                               + dY * d_ref[:, gs])
            cd_e = jnp.concatenate([jnp.broadcast_to(cd[:, g * 8 + e:g * 8 + e + 1], (1, SSMD)) for e in range(8)], axis=1)
            G[:, gs] = Gg * cd_e + _dot(Cb, dYe_g, TN)
        dnw_ref[...] += jnp.concatenate(dnw_cols, axis=1)
        eh = eh_ref[...]
        dd_ref[...] += jnp.sum(eh * jnp.concatenate(dd_cols, axis=1), axis=1, keepdims=True)
        dcd = jnp.sum(eh * jnp.concatenate(gh_cols, axis=1), axis=1, keepdims=True)
        DDT = jnp.concatenate(ddt_rows, axis=0)
        DW = jnp.concatenate(dw_rows, axis=0)
        DWw = DW * wT
        dacsT = jnp.concatenate(rs_rows, axis=0) - DDT * dtT + jnp.concatenate(deo_rows, axis=0) - DWw
        end = jnp.sum(DWw, axis=1, keepdims=True) + dcd * jnp.exp(lastT)
        lane = lax.broadcasted_iota(jnp.int32, (SSM_HEADS, CHUNK), 1)
        dacsT = dacsT + jnp.where(lane == CHUNK - 1, end, 0.0)
        dadtT = _dot3(dacsT, tril.astype(BF16))
        ddtT = dadtT * a_col + DDT + DW * dstT
        dalog_ref[...] += jnp.sum(dadtT * dtT, axis=1, keepdims=True) * a_col
        ddt_rawT = ddtT * _sigmoid(dt_rawT + dtb_ref[...])
        ddtb_ref[...] += jnp.sum(ddt_rawT, axis=1, keepdims=True)
        ddt_ref[...] = jnp.concatenate([ddt_rawT.T, jnp.zeros((CHUNK, 128 - SSM_HEADS), F32)], axis=1).astype(ddt_ref.dtype)

    rev = lambda c: nc - 1 - c
    vec = lambda n: _const_spec((1, n))
    colv = _const_spec((SSM_HEADS, 1))
    any_spec = pl.BlockSpec(memory_space=pl.ANY)
    return pl.pallas_call(
        body, name="ssd_bwd", grid=(nc,),
        in_specs=[pl.BlockSpec((CHUNK, SSM_INNER), lambda c: (rev(c), 0)),
                  pl.BlockSpec((CHUNK, SSM_INNER), lambda c: (rev(c), 0)),
                  pl.BlockSpec((CHUNK, XBC_COLS), lambda c: (rev(c), 0)),
                  pl.BlockSpec((1, SSM_STATE, SSM_INNER), lambda c: (rev(c), 0, 0)),
                  pl.BlockSpec((CHUNK, 128), lambda c: (rev(c), DT_OFF // 128)),
                  pl.BlockSpec((CHUNK, SSM_INNER), lambda c: (rev(c), Z_OFF // SSM_INNER)),
                  colv, colv, vec(SSM_INNER), vec(SSM_INNER), _const_spec((SSM_HEADS, SSM_INNER)), any_spec, any_spec],
        out_specs=[pl.BlockSpec((CHUNK, SSM_INNER), lambda c: (rev(c), Z_OFF // SSM_INNER)),
                   pl.BlockSpec((CHUNK, 128), lambda c: (rev(c), DT_OFF // 128)),
                   pl.BlockSpec((CHUNK, XBC_COLS), lambda c: (rev(c), 0)),
                   vec(SSM_INNER), colv, colv, colv],
        out_shape=[jax.ShapeDtypeStruct(dmain.shape, dmain.dtype), jax.ShapeDtypeStruct(dtail.shape, dtail.dtype),
                   jax.ShapeDtypeStruct((T, XBC_COLS), F32), jax.ShapeDtypeStruct((1, SSM_INNER), F32),
                   jax.ShapeDtypeStruct((SSM_HEADS, 1), F32), jax.ShapeDtypeStruct((SSM_HEADS, 1), F32),
                   jax.ShapeDtypeStruct((SSM_HEADS, 1), F32)],
        scratch_shapes=[pltpu.VMEM((SSM_STATE, SSM_INNER), F32)],
        input_output_aliases={11: 0, 12: 1},
        compiler_params=_cp(("arbitrary",)))(dyo, ypre, xbc, hs, proj_tail, proj_main, dtb_col, alog_col, d_exp, norm_w,
                                             ehead_t, dmain, dtail)


def _rel_bucket(rel):
    n = jnp.maximum(rel, 0)
    max_exact = REL_BUCKETS // 2
    nf = jnp.maximum(n, 1).astype(F32)
    large = max_exact + (jnp.log(nf / max_exact) / math.log(REL_MAX_DIST / max_exact)
                         * (REL_BUCKETS - max_exact)).astype(jnp.int32)
    large = jnp.minimum(large, REL_BUCKETS - 1)
    return jnp.where(n < max_exact, n, large)


def _band_geometry():
    qi = jnp.arange(WINDOW)[:, None] + WINDOW
    kj = jnp.arange(2 * WINDOW)[None, :]
    rel = qi - kj
    return _rel_bucket(rel), (rel >= 0) & (rel < WINDOW)


def _attn_logits(kband, qh, bias_h, first):
    s = _dot(kband, qh, NT) * (HEADDIM ** -0.5) + bias_h
    rowk = lax.broadcasted_iota(jnp.int32, (2 * WINDOW, WINDOW), 0)
    return jnp.where(jnp.logical_and(first, rowk < WINDOW), NEG, s)


def _attn_fwd(proj_main, proj_tail, bias_tbl, sinks):
    T = proj_main.shape[0]
    nb = T // WINDOW

    def body(q_ref, kv_ref, kvp_ref, bias_ref, sink_ref, o_ref, lse_ref):
        i = pl.program_id(0)
        first = i == 0
        outs, lses = [], []
        for kvh in range(ATTN_KV):
            ks = slice(K_OFF + kvh * HEADDIM, K_OFF + (kvh + 1) * HEADDIM)
            vs = slice(V_OFF + kvh * HEADDIM, V_OFF + (kvh + 1) * HEADDIM)
            kband = jnp.concatenate([kvp_ref[:, ks], kv_ref[:, ks]], axis=0).astype(BF16)
            vband = jnp.concatenate([kvp_ref[:, vs], kv_ref[:, vs]], axis=0).astype(BF16)
            heads = range(kvh * ATTN_GROUP, (kvh + 1) * ATTN_GROUP)
            logits = [_attn_logits(kband, q_ref[:, h * HEADDIM:(h + 1) * HEADDIM].astype(BF16), bias_ref[h], first)
                      for h in heads]
            probs = []
            for h, s in zip(heads, logits):
                sink = sink_ref[:, h:h + 1]
                m = jnp.maximum(jnp.max(s, axis=0, keepdims=True), sink)
                p = jnp.exp(s - m)
                den = jnp.sum(p, axis=0, keepdims=True) + jnp.exp(sink - m)
                probs.append((p * (1.0 / den)).astype(BF16))
                lses.append(m + jnp.log(den))
            outs += [_dot(pt, vband, TN) for pt in probs]
        o_ref[...] = jnp.concatenate(outs, axis=1).astype(BF16)
        lse_ref[...] = jnp.concatenate(lses, axis=0)

    return pl.pallas_call(
        body, name="attn_fwd", grid=(nb,),
        in_specs=[pl.BlockSpec((WINDOW, D_MODEL), lambda i: (i, Q_OFF // D_MODEL)),
                  pl.BlockSpec((WINDOW, 256), lambda i: (i, 0)),
                  pl.BlockSpec((WINDOW, 256), lambda i: (jnp.maximum(i - 1, 0), 0)),
                  _const_spec((ATTN_HEADS, 2 * WINDOW, WINDOW)), _const_spec((1, ATTN_HEADS))],
        out_specs=[pl.BlockSpec((WINDOW, D_MODEL), lambda i: (i, 0)),
                   pl.BlockSpec((ATTN_HEADS, WINDOW), lambda i: (0, i))],
        out_shape=[jax.ShapeDtypeStruct((T, D_MODEL), BF16), jax.ShapeDtypeStruct((ATTN_HEADS, T), F32)],
        compiler_params=_cp(("arbitrary",)))(proj_main, proj_tail, proj_tail, bias_tbl, sinks)


def _attn_bwd(dy, lse, proj_main, proj_tail, bias_tbl, sinks, dmain):
    T = proj_main.shape[0]
    nb = T // WINDOW

    def body(dy_ref, lse_ref, q_ref, kv_ref, kvp_ref, bias_ref, sink_ref, dmain_in,
             dq_ref, dkv_ref, dbias_ref, dsink_ref, carry):
        del dmain_in
        i = pl.program_id(0)
        first = i == 0

        @pl.when(first)
        def _():
            carry[...] = jnp.zeros_like(carry)
            dbias_ref[...] = jnp.zeros_like(dbias_ref)
            dsink_ref[...] = jnp.zeros_like(dsink_ref)

        @pl.when(i < nb)
        def _():
            scale = HEADDIM ** -0.5
            dqs, dsinks, dks, dvs = [], [], [], []
            for kvh in range(ATTN_KV):
                ks = slice(K_OFF + kvh * HEADDIM, K_OFF + (kvh + 1) * HEADDIM)
                vs = slice(V_OFF + kvh * HEADDIM, V_OFF + (kvh + 1) * HEADDIM)
                kband = jnp.concatenate([kvp_ref[:, ks], kv_ref[:, ks]], axis=0).astype(BF16)
                vband = jnp.concatenate([kvp_ref[:, vs], kv_ref[:, vs]], axis=0).astype(BF16)
                heads = range(kvh * ATTN_GROUP, (kvh + 1) * ATTN_GROUP)
                qs = [q_ref[:, h * HEADDIM:(h + 1) * HEADDIM].astype(BF16) for h in heads]
                dos = [dy_ref[:, h * HEADDIM:(h + 1) * HEADDIM] for h in heads]
                logits = [_attn_logits(kband, qh, bias_ref[h], first) for h, qh in zip(heads, qs)]
                dps = [_dot(vband, do, NT) for do in dos]
                pbs, dsbs = [], []
                for h, s, dp in zip(heads, logits, dps):
                    lse_h = lse_ref[h:h + 1, :]
                    p = jnp.exp(s - lse_h)
                    delta = jnp.sum(p * dp, axis=0, keepdims=True)
                    ds = p * (dp - delta)
                    psink = jnp.exp(sink_ref[:, h:h + 1] - lse_h)
                    dsinks.append(-jnp.sum(psink * delta, axis=1, keepdims=True))
                    dbias_ref[h] += ds
                    pbs.append(p.astype(BF16))
                    dsbs.append((ds * scale).astype(BF16))
                dqs += [_dot(dsb, kband, TN) for dsb in dsbs]
                dks.append(_dot(jnp.concatenate(dsbs, axis=1), jnp.concatenate(qs, axis=0)))
                dvs.append(_dot(jnp.concatenate(pbs, axis=1), jnp.concatenate(dos, axis=0)))
            dq_ref[...] = jnp.concatenate(dqs, axis=1).astype(dq_ref.dtype)
            dsink_ref[...] += jnp.concatenate(dsinks, axis=1)
            dkv = jnp.concatenate(dks + dvs, axis=1)
            dkv_ref[...] = (carry[...] + dkv[0:WINDOW, :]).astype(dkv_ref.dtype)
            carry[...] = dkv[WINDOW:, :]

        @pl.when(i == nb)
        def _():
            dkv_ref[...] = carry[...].astype(dkv_ref.dtype)

    cur = lambda i: jnp.minimum(i, nb - 1)
    return pl.pallas_call(
        body, name="attn_bwd", grid=(nb + 1,),
        in_specs=[pl.BlockSpec((WINDOW, D_MODEL), lambda i: (cur(i), 0)),
                  pl.BlockSpec((ATTN_HEADS, WINDOW), lambda i: (0, cur(i))),
                  pl.BlockSpec((WINDOW, D_MODEL), lambda i: (cur(i), Q_OFF // D_MODEL)),
                  pl.BlockSpec((WINDOW, 256), lambda i: (cur(i), 0)),
                  pl.BlockSpec((WINDOW, 256), lambda i: (jnp.maximum(cur(i) - 1, 0), 0)),
                  _const_spec((ATTN_HEADS, 2 * WINDOW, WINDOW)), _const_spec((1, ATTN_HEADS)),
                  pl.BlockSpec(memory_space=pl.ANY)],
        out_specs=[pl.BlockSpec((WINDOW, D_MODEL), lambda i: (cur(i), Q_OFF // D_MODEL)),
                   pl.BlockSpec((WINDOW, 256), lambda i: (jnp.maximum(i - 1, 0), 0)),
                   _const_spec((ATTN_HEADS, 2 * WINDOW, WINDOW)), _const_spec((1, ATTN_HEADS))],
        out_shape=[jax.ShapeDtypeStruct(dmain.shape, dmain.dtype), jax.ShapeDtypeStruct((T, TAIL_COLS), BF16),
                   jax.ShapeDtypeStruct((ATTN_HEADS, 2 * WINDOW, WINDOW), F32),
                   jax.ShapeDtypeStruct((1, ATTN_HEADS), F32)],
        scratch_shapes=[pltpu.VMEM((WINDOW, 256), F32)],
        input_output_aliases={7: 0},
        compiler_params=_cp(("arbitrary",)))(dy, lse, proj_main, proj_tail, proj_tail, bias_tbl, sinks, dmain)


def _bias_table(rel_bias_t, onehot_t, mask):
    def body(rb_ref, oh_ref, m_ref, o_ref):
        o_ref[...] = _dot3(rb_ref[...], oh_ref[...]) + m_ref[...]

    flat = pl.pallas_call(body, name="bias_table",
                          out_shape=jax.ShapeDtypeStruct((ATTN_HEADS, 2 * WINDOW * WINDOW), F32))(rel_bias_t, onehot_t, mask)
    return flat.reshape(ATTN_HEADS, 2 * WINDOW, WINDOW)


def _rel_bias_grad(dbias, onehot):
    def body(d_ref, oh_ref, o_ref):
        o_ref[...] = _dot(d_ref[...], oh_ref[...], NN, HIGHEST)

    return pl.pallas_call(body, name="rel_bias_grad",
                          out_shape=jax.ShapeDtypeStruct((ATTN_HEADS, REL_BUCKETS), F32))(dbias, onehot)


def _ln_fwd(r, g, b):
    mu = jnp.mean(r, axis=1, keepdims=True)
    xc = r - mu
    rstd = lax.rsqrt(jnp.mean(xc * xc, axis=1, keepdims=True) + LN_EPS)
    xhat = xc * rstd
    return xhat * g + b, xhat, rstd


def _ln_bwd(dy, xhat, rstd, g):
    dxh = dy * g
    return rstd * (dxh - jnp.mean(dxh, axis=1, keepdims=True) - xhat * jnp.mean(dxh * xhat, axis=1, keepdims=True))


def _merge_fwd(y_ssm, y_attn, proj_main, b_gate, w_bs, w_ba, tm=512):
    T = y_ssm.shape[0]

    def body(ys_ref, ya_ref, gs_ref, ga_ref, bg_ref, wbs_ref, wba_ref, m_ref, bs_ref, ba_ref):
        bs = _dot(ys_ref[...], wbs_ref[...])
        ba = _dot(ya_ref[...], wba_ref[...])
        g_s = _sigmoid(gs_ref[...] + bg_ref[:, 0:D_MODEL])
        g_a = _sigmoid(ga_ref[...] + bg_ref[:, D_MODEL:])
        m_ref[...] = (g_s * bs + g_a * ba).astype(BF16)
        bs_ref[...] = bs
        ba_ref[...] = ba

    row = lambda w, off=0: pl.BlockSpec((tm, w), lambda i: (i, off))
    return pl.pallas_call(
        body, name="merge_fwd", grid=(T // tm,),
        in_specs=[row(SSM_INNER), row(D_MODEL), row(D_MODEL, GATE_OFF // D_MODEL), row(D_MODEL, GATE_OFF // D_MODEL + 1),
                  _const_spec((1, 2 * D_MODEL)), _const_spec((SSM_INNER, D_MODEL)), _const_spec((D_MODEL, D_MODEL))],
        out_specs=[row(D_MODEL), row(D_MODEL), row(D_MODEL)],
        out_shape=[jax.ShapeDtypeStruct((T, D_MODEL), BF16), jax.ShapeDtypeStruct((T, D_MODEL), F32),
                   jax.ShapeDtypeStruct((T, D_MODEL), F32)],
        compiler_params=_cp(("parallel",)))(y_ssm, y_attn, proj_main, proj_main, b_gate, w_bs, w_ba)


def _mix_ln1(merged, w_mo, x, g1, b1, tm=512):
    T = x.shape[0]

    def body(m_ref, w_ref, x_ref, g_ref, b_ref, r_ref, h_ref):
        r = ALPHA * x_ref[...] + _dot(m_ref[...], w_ref[...])
        r_ref[...] = r
        h_ref[...] = _ln_fwd(r, g_ref[...], b_ref[...])[0]

    row = pl.BlockSpec((tm, D_MODEL), lambda i: (i, 0))
    return pl.pallas_call(
        body, name="mix_ln1", grid=(T // tm,),
        in_specs=[row, _const_spec((D_MODEL, D_MODEL)), row, _const_spec((1, D_MODEL)), _const_spec((1, D_MODEL))],
        out_specs=[row, row],
        out_shape=[jax.ShapeDtypeStruct((T, D_MODEL), F32), jax.ShapeDtypeStruct((T, D_MODEL), F32)],
        compiler_params=_cp(("parallel",)))(merged, w_mo, x, g1, b1)


def _ffn_conv_glu(u_pre, w, b, tr=1024, tc=256):
    T = u_pre.shape[0]
    tr = min(tr, T)
    K = FFN_CONV
    nj = D_FF // tc
    hb = tr // HALO
    assert T % tr == 0 and D_FF % tc == 0

    def body(xg_ref, xgp_ref, xv_ref, xvp_ref, wg_ref, wv_ref, bg_ref, bv_ref, u_ref, a_ref, head_g, head_v):
        i = pl.program_id(1)
        halves = []
        for x_ref, xp_ref, w_ref, b_ref, head in ((xg_ref, xgp_ref, wg_ref, bg_ref, head_g),
                                                  (xv_ref, xvp_ref, wv_ref, bv_ref, head_v)):
            head[0:HALO, :] = jnp.where(i > 0, xp_ref[...], 0.0)
            head[HALO:, :] = x_ref[0:HALO, :]
            halves.append((x_ref, head, _taps(w_ref, K, tc), jnp.broadcast_to(b_ref[...], (HALO, tc))))

        def conv(half, r):
            x_ref, head, wk, bias = halves[half]
            lo = r * HALO
            acc = bias + wk[K - 1] * x_ref[lo:lo + HALO, :]
            for k in range(K - 1):
                s = K - 1 - k
                acc = acc + wk[k] * (head[HALO - s:2 * HALO - s, :] if r == 0 else x_ref[lo - s:lo + HALO - s, :])
            return acc

        for r2 in range(tr // (2 * HALO)):
            acts = []
            for r in (2 * r2, 2 * r2 + 1):
                lo = r * HALO
                ug, uv = conv(0, r), conv(1, r)
                u_ref[0, lo:lo + HALO, :] = ug
                u_ref[1, lo:lo + HALO, :] = uv
                acts.append(ug * _sigmoid(ug) * uv)
            a_ref[2 * r2 * HALO:(2 * r2 + 2) * HALO, :] = jnp.concatenate(acts, axis=0).astype(BF16)

    tile = lambda off: pl.BlockSpec((tr, tc), lambda j, i: (i, off + j))
    prev = lambda off: pl.BlockSpec((HALO, tc), lambda j, i: (jnp.maximum(i * hb - 1, 0), off + j))
    row = lambda rows, off: pl.BlockSpec((rows, tc), lambda j, i: (0, off + j))
    return pl.pallas_call(
        body, name="ffn_conv_glu", grid=(nj, T // tr),
        in_specs=[tile(0), prev(0), tile(nj), prev(nj), row(K, 0), row(K, nj), row(1, 0), row(1, nj)],
        out_specs=[pl.BlockSpec((2, tr, tc), lambda j, i: (0, i, j)), pl.BlockSpec((tr, tc), lambda j, i: (i, j))],
        out_shape=[jax.ShapeDtypeStruct((2, T, D_FF), F32), jax.ShapeDtypeStruct((T, D_FF), BF16)],
        scratch_shapes=[pltpu.VMEM((2 * HALO, tc), F32), pltpu.VMEM((2 * HALO, tc), F32)],
        compiler_params=_cp(("parallel", "arbitrary")))(u_pre, u_pre, u_pre, u_pre, w, w, b, b)


def _down_ln2_loss(act, w_down, h1, target, g2, b2, tm=512):
    T = h1.shape[0]

    def body(a_ref, w_ref, h_ref, t_ref, g_ref, b_ref, dr_ref, dg_ref, db_ref, l_ref):
        @pl.when(pl.program_id(0) == 0)
        def _():
            dg_ref[...] = jnp.zeros_like(dg_ref)
            db_ref[...] = jnp.zeros_like(db_ref)
            l_ref[...] = jnp.zeros_like(l_ref)

        r = ALPHA * h_ref[...] + _dot(a_ref[...], w_ref[...])
        y, xhat, rstd = _ln_fwd(r, g_ref[...], b_ref[...])
        err = y - t_ref[...]
        l_ref[...] += jnp.sum(err * err, keepdims=True)
        dy = err * (1.0 / D_MODEL)
        dg_ref[...] += jnp.sum(dy * xhat, axis=0, keepdims=True)
        db_ref[...] += jnp.sum(dy, axis=0, keepdims=True)
        dr_ref[...] = _ln_bwd(dy, xhat, rstd, g_ref[...])

    row = pl.BlockSpec((tm, D_MODEL), lambda i: (i, 0))
    vec = _const_spec((1, D_MODEL))
    return pl.pallas_call(
        body, name="down_ln2_loss", grid=(T // tm,),
        in_specs=[pl.BlockSpec((tm, D_FF), lambda i: (i, 0)), _const_spec((D_FF, D_MODEL)), row, row, vec, vec],
        out_specs=[row, vec, vec, _const_spec((1, 1))],
        out_shape=[jax.ShapeDtypeStruct((T, D_MODEL), F32), jax.ShapeDtypeStruct((1, D_MODEL), F32),
                   jax.ShapeDtypeStruct((1, D_MODEL), F32), jax.ShapeDtypeStruct((1, 1), F32)],
        compiler_params=_cp(("arbitrary",)))(act, w_down, h1, target, g2, b2)


def _ffn_bwd_act(dr2, w_down, u, tm=512, tn=1408):
    T = dr2.shape[0]
    nj = D_FF // tn

    def body(d_ref, w_ref, g_ref, v_ref, o_ref, dact):
        half = pl.program_id(2)

        @pl.when(half == 0)
        def _():
            dact[...] = _dot(d_ref[...].astype(BF16), w_ref[...], NT)
            g = g_ref[...]
            sg = _sigmoid(g)
            o_ref[...] = dact[...] * v_ref[...] * (sg * (1.0 + g * (1.0 - sg)))

        @pl.when(half == 1)
        def _():
            g = g_ref[...]
            o_ref[...] = dact[...] * (g * _sigmoid(g))

    return pl.pallas_call(
        body, name="ffn_bwd_act", grid=(T // tm, nj, 2),
        in_specs=[pl.BlockSpec((tm, D_MODEL), lambda i, j, h: (i, 0)),
                  pl.BlockSpec((tn, D_MODEL), lambda i, j, h: (j, 0)),
                  pl.BlockSpec((None, tm, tn), lambda i, j, h: (0, i, j)),
                  pl.BlockSpec((None, tm, tn), lambda i, j, h: (1, i, j))],
        out_specs=pl.BlockSpec((tm, tn), lambda i, j, h: (i, h * nj + j)),
        out_shape=jax.ShapeDtypeStruct((T, 2 * D_FF), F32),
        scratch_shapes=[pltpu.VMEM((tm, tn), F32)],
        compiler_params=_cp(("parallel", "arbitrary", "arbitrary")))(dr2, w_down, u, u)


def _ffn_bwd_in(du_pre, w_up, dr2, r1, g1, b1, tm=1024, tk=1408):
    T = dr2.shape[0]
    tm = min(tm, T)
    assert T % tm == 0
    nk = 2 * D_FF // tk

    def body(d_ref, w_ref, dr2_ref, r_ref, g_ref, b_ref, dr1_ref, dg_ref, db_ref, acc):
        i, k = pl.program_id(0), pl.program_id(1)

        @pl.when(jnp.logical_and(i == 0, k == 0))
        def _():
            dg_ref[...] = jnp.zeros_like(dg_ref)
            db_ref[...] = jnp.zeros_like(db_ref)

        @pl.when(k == 0)
        def _():
            acc[...] = ALPHA * dr2_ref[...]

        acc[...] += _dot(d_ref[...], w_ref[...], NT)

        @pl.when(k == nk - 1)
        def _():
            _, xhat, rstd = _ln_fwd(r_ref[...], g_ref[...], b_ref[...])
            dy = acc[...]
            dg_ref[...] += jnp.sum(dy * xhat, axis=0, keepdims=True)
            db_ref[...] += jnp.sum(dy, axis=0, keepdims=True)
            dr1_ref[...] = _ln_bwd(dy, xhat, rstd, g_ref[...])

    row = pl.BlockSpec((tm, D_MODEL), lambda i, k: (i, 0))
    vec = _const_spec((1, D_MODEL))
    return pl.pallas_call(
        body, name="ffn_bwd_in", grid=(T // tm, nk),
        in_specs=[pl.BlockSpec((tm, tk), lambda i, k: (i, k)), pl.BlockSpec((D_MODEL, tk), lambda i, k: (0, k)),
                  row, row, vec, vec],
        out_specs=[row, vec, vec],
        out_shape=[jax.ShapeDtypeStruct((T, D_MODEL), F32), jax.ShapeDtypeStruct((1, D_MODEL), F32),
                   jax.ShapeDtypeStruct((1, D_MODEL), F32)],
        scratch_shapes=[pltpu.VMEM((tm, D_MODEL), F32)],
        compiler_params=_cp(("arbitrary", "arbitrary")))(du_pre, w_up, dr2, r1, g1, b1)


def _mix_bwd(dr1, w_mo, w_bs, w_ba, bs, ba, proj_main, b_gate, tm=512):
    T = dr1.shape[0]

    def body(d_ref, wmo_ref, wbs_ref, wba_ref, bs_ref, ba_ref, gs_ref, ga_ref, bg_ref,
             dg_ref, dbs_ref, dba_ref, dys_ref, dya_ref, dbg_ref):
        @pl.when(pl.program_id(0) == 0)
        def _():
            dbg_ref[...] = jnp.zeros_like(dbg_ref)

        dm = _dot(d_ref[...].astype(BF16), wmo_ref[...], NT)
        g_s = _sigmoid(gs_ref[...] + bg_ref[:, 0:D_MODEL])
        g_a = _sigmoid(ga_ref[...] + bg_ref[:, D_MODEL:])
        dgs = dm * bs_ref[...] * g_s * (1.0 - g_s)
        dga = dm * ba_ref[...] * g_a * (1.0 - g_a)
        dg_ref[:, 0:D_MODEL] = dgs.astype(BF16)
        dg_ref[:, D_MODEL:] = dga.astype(BF16)
        dbg_ref[:, 0:D_MODEL] += jnp.sum(dgs, axis=0, keepdims=True)
        dbg_ref[:, D_MODEL:] += jnp.sum(dga, axis=0, keepdims=True)
        dbs = (dm * g_s).astype(BF16)
        dba = (dm * g_a).astype(BF16)
        dbs_ref[...] = dbs
        dba_ref[...] = dba
        dys_ref[...] = _dot(dbs, wbs_ref[...], NT)
        dya_ref[...] = _dot(dba, wba_ref[...], NT).astype(BF16)

    row = lambda w, off=0: pl.BlockSpec((tm, w), lambda i: (i, off))
    return pl.pallas_call(
        body, name="mix_bwd", grid=(T // tm,),
        in_specs=[row(D_MODEL), _const_spec((D_MODEL, D_MODEL)), _const_spec((SSM_INNER, D_MODEL)),
                  _const_spec((D_MODEL, D_MODEL)), row(D_MODEL), row(D_MODEL),
                  row(D_MODEL, GATE_OFF // D_MODEL), row(D_MODEL, GATE_OFF // D_MODEL + 1), _const_spec((1, 2 * D_MODEL))],
        out_specs=[row(2 * D_MODEL, GATE_OFF // (2 * D_MODEL)), row(D_MODEL), row(D_MODEL), row(SSM_INNER), row(D_MODEL),
                   _const_spec((1, 2 * D_MODEL))],
        out_shape=[jax.ShapeDtypeStruct((T, MAIN_COLS), BF16), jax.ShapeDtypeStruct((T, D_MODEL), BF16),
                   jax.ShapeDtypeStruct((T, D_MODEL), BF16), jax.ShapeDtypeStruct((T, SSM_INNER), F32),
                   jax.ShapeDtypeStruct((T, D_MODEL), BF16), jax.ShapeDtypeStruct((1, 2 * D_MODEL), F32)],
        compiler_params=_cp(("arbitrary",)))(dr1, w_mo, w_bs, w_ba, bs, ba, proj_main, proj_main, b_gate)


def _local_step(x, target, w, p, late_weights=None, early_grads=None):
    xb = x.astype(BF16)
    if late_weights is None:
        proj_main = _matmul(xb, w["in_main"], "nn", F32, "in_proj_main", tm=1024, tn=2048)
    else:
        proj_main, *landed = _matmul(xb, w["in_main"], "nn", F32, "in_proj_main", tm=1024, tn=2048,
                                     comm=("gather", late_weights[0]))
        w = {**w, **late_weights[1](landed)}
    proj_tail = _matmul(xb, w["in_tail"], "nn", F32, "in_proj_tail", tn=TAIL_COLS)
    xbc = _conv_fwd(proj_main, XBC_OFF, XBC_COLS, p["ssm_conv_w"], p["ssm_conv_b"], SSM_CONV, "silu", "ssm_conv_fwd")
    y_ssm, ypre, hs = _ssd_fwd(xbc, proj_main, proj_tail, p["dtb_col"], p["alog_col"], p["d_exp"], p["ssm_norm_w"])
    y_attn, lse = _attn_fwd(proj_main, proj_tail, p["bias_tbl"], p["attn_sinks"])
    merged, bs, ba = _merge_fwd(y_ssm, y_attn, proj_main, p["b_gate"], w["bs"], w["ba"])
    r1, h1 = _mix_ln1(merged, w["mo"], x, p["ln1_g"], p["ln1_b"])
    u_pre = _matmul(h1, w["up"], "nn", F32, "ffn_up", tm=1024, tn=1408)
    u, act = _ffn_conv_glu(u_pre, p["ffn_conv_w"], p["ffn_conv_b"])
    dr2, dg2, db2, sq = _down_ln2_loss(act, w["down"], h1, target, p["ln2_g"], p["ln2_b"])
    g = {"ln2_g": dg2, "ln2_b": db2}
    g["w_down"] = _matmul(act, dr2, "tn", F32, "dw_down", tm=1408, tn=1024, tk=1024)
    du = _ffn_bwd_act(dr2, w["down"], u)
    du_pre, g["ffn_conv_w"], g["ffn_conv_b"] = _conv_bwd(
        du, u_pre, 0, 2 * D_FF, p["ffn_conv_w"], p["ffn_conv_b"], FFN_CONV, None,
        jax.ShapeDtypeStruct((x.shape[0], 2 * D_FF), BF16), 0, "ffn_conv_bwd")
    g["w_up"] = _matmul(h1, du_pre, "tn", F32, "dw_up", tm=1024, tn=1408, tk=1024)
    dr1, g["ln1_g"], g["ln1_b"] = _ffn_bwd_in(du_pre, w["up"], dr2, r1, p["ln1_g"], p["ln1_b"])
    g["w_mix_out"] = _matmul(merged, dr1, "tn", F32, "dw_mix_out", tm=1024, tn=1024, tk=2048)
    dmain, dbs, dba, dy_ssm, dy_attn, g["b_gate"] = _mix_bwd(dr1, w["mo"], w["bs"], w["ba"], bs, ba, proj_main, p["b_gate"])
    g["w_branch_ssm"] = _matmul(y_ssm, dbs, "tn", F32, "dw_branch_ssm", tm=1024, tn=1024, tk=2048)
    g["w_branch_attn"] = _matmul(y_attn, dba, "tn", F32, "dw_branch_attn", tm=1024, tn=1024, tk=2048)
    dmain, dtail, dbias, g["attn_sinks"] = _attn_bwd(dy_attn, lse, proj_main, proj_tail, p["bias_tbl"], p["attn_sinks"], dmain)
    g["rel_bias"] = _rel_bias_grad(dbias.reshape(ATTN_HEADS, WINDOW * 2 * WINDOW), p["bucket_onehot"]).T
    dmain, dtail, dxbc, g["ssm_norm_w"], dd, dalog, ddtb = _ssd_bwd(
        dy_ssm, ypre, xbc, hs, proj_main, proj_tail, p["dtb_col"], p["alog_col"], p["d_exp"], p["ssm_norm_w"],
        p["ehead_t"], dmain, dtail)
    g["ssm_d"], g["ssm_a_log"], g["ssm_dt_bias"] = (a.reshape(1, SSM_HEADS) for a in (dd, dalog, ddtb))
    dmain, g["ssm_conv_w"], g["ssm_conv_b"] = _conv_bwd(
        dxbc, proj_main, XBC_OFF, XBC_COLS, p["ssm_conv_w"], p["ssm_conv_b"], SSM_CONV, "silu", dmain, XBC_OFF,
        "ssm_conv_bwd")
    g["in_tail"] = _matmul(xb, dtail, "tn", F32, "dw_in_tail", tm=1024, tn=TAIL_COLS, tk=2048)
    landed = []
    if early_grads is None:
        g["in_main"] = _matmul(xb, dmain, "tn", F32, "dw_in_main", tm=1024, tn=1024, tk=2048)
    else:
        g["in_main"], *landed = _matmul(xb, dmain, "tn", F32, "dw_in_main", tm=1024, tn=1024, tk=2048,
                                        comm=("exchange", early_grads(g)))
    return sq, (dmain, dtail, dr1), w, g, landed


def _grad_x(dproj, w, exchange=None):
    dmain, dtail, dr1 = dproj
    dx = _matmul(dtail, w["in_tail"], "nt", F32, "dx_tail", tk=TAIL_COLS, addend=dr1, addend_scale=ALPHA)
    if exchange is None:
        return _matmul(dmain, w["in_main"], "nt", F32, "dx_main", tm=1024, tk=2048, addend=dx)
    dx, landed = _matmul(dmain, w["in_main"], "nt", F32, "dx_main", tm=1024, tk=2048, addend=dx,
                         comm=("exchange", exchange))
    return dx, landed


def _split_w_in(w):
    seg = lambda off, n: w[:, off:off + n]
    main = jnp.concatenate([seg(O_Z, 2048), seg(O_XBC, XBC_COLS), seg(O_Q, D_MODEL), seg(O_GATE, 2 * D_MODEL)], axis=1)
    tail = jnp.concatenate([seg(O_K, 128), seg(O_V, 128), seg(O_DT, SSM_HEADS),
                            jnp.zeros((w.shape[0], 128 - SSM_HEADS), w.dtype)], axis=1)
    return main, tail


def _join_w_in(main, tail):
    return jnp.concatenate([main[:, Z_OFF:Z_OFF + 2048], main[:, XBC_OFF:XBC_OFF + XBC_COLS],
                            tail[:, DT_OFF:DT_OFF + SSM_HEADS], main[:, Q_OFF:Q_OFF + D_MODEL],
                            tail[:, K_OFF:K_OFF + 128], tail[:, V_OFF:V_OFF + 128],
                            main[:, GATE_OFF:GATE_OFF + 2 * D_MODEL]], axis=1)


def _prep_params(rel_bias, b_gate, ssm_conv_w, ssm_conv_b, ssm_dt_bias, ssm_a_log, ssm_d, ssm_norm_w, attn_sinks,
                 ln1_g, ln1_b, ffn_conv_w, ffn_conv_b, ln2_g, ln2_b):
    bucket, in_window = _band_geometry()
    bucket, in_window = bucket.T, in_window.T
    onehot = jnp.logical_and(bucket.reshape(-1, 1) == jnp.arange(REL_BUCKETS)[None, :],
                             in_window.reshape(-1, 1)).astype(F32)
    onehot_t = jnp.logical_and(bucket.reshape(1, -1) == jnp.arange(REL_BUCKETS)[:, None],
                               in_window.reshape(1, -1)).astype(BF16)
    bias_tbl = _bias_table(rel_bias.T, onehot_t, jnp.where(in_window.reshape(1, -1), 0.0, NEG))
    ehead_t = (jnp.arange(SSM_INNER)[None, :] // SSMD == jnp.arange(SSM_HEADS)[:, None]).astype(F32)
    return {"bias_tbl": bias_tbl, "bucket_onehot": onehot, "b_gate": b_gate, "ssm_conv_w": ssm_conv_w,
            "ssm_conv_b": ssm_conv_b, "dtb_col": ssm_dt_bias.reshape(SSM_HEADS, 1),
            "alog_col": ssm_a_log.reshape(SSM_HEADS, 1), "ehead_t": ehead_t,
            "d_exp": jnp.repeat(ssm_d, SSMD, axis=1), "ssm_norm_w": ssm_norm_w, "attn_sinks": attn_sinks,
            "ln1_g": ln1_g, "ln1_b": ln1_b, "ffn_conv_w": ffn_conv_w, "ffn_conv_b": ffn_conv_b,
            "ln2_g": ln2_g, "ln2_b": ln2_b}


def _all_gather(shards, name):
    nb = len(shards)

    def body(*refs):
        for phase in _gather_phases(refs[:nb], refs[nb:2 * nb], *refs[2 * nb:]):
            phase()

    any_spec = pl.BlockSpec(memory_space=pl.ANY)
    return pl.pallas_call(
        body, name=name, out_shape=[jax.ShapeDtypeStruct((N_DEV,) + s.shape, s.dtype) for s in shards],
        in_specs=[any_spec] * nb, out_specs=[any_spec] * nb, scratch_shapes=_comm_sems(nb))(*shards)


def _adamw_math(w, g, m, v):
    m = ADAM_B1 * m + (1.0 - ADAM_B1) * g
    v = ADAM_B2 * v + (1.0 - ADAM_B2) * (g * g)
    m_hat = m / (1.0 - ADAM_B1 ** ADAM_STEP)
    v_hat = v / (1.0 - ADAM_B2 ** ADAM_STEP)
    return -ADAM_LR * (m_hat / (jnp.sqrt(v_hat) + ADAM_EPS) + ADAM_WD * w), m, v


def _slot_total(s_ref):
    g = s_ref[0].astype(F32)
    for i in range(1, N_DEV):
        g = g + s_ref[i].astype(F32)
    return g


def _adamw(landed, w, m, v, name):
    R, C = w.shape
    tr = 256 if R % 256 == 0 and R > 256 else R

    def body(s_ref, w_ref, m_ref, v_ref, g_ref, d_ref, nm_ref, nv_ref):
        g = _slot_total(s_ref)
        g_ref[...] = g
        d_ref[...], nm_ref[...], nv_ref[...] = _adamw_math(w_ref[...], g, m_ref[...], v_ref[...])

    spec = pl.BlockSpec((tr, C), lambda i: (i, 0))
    return pl.pallas_call(
        body, name=name, grid=(R // tr,), in_specs=[pl.BlockSpec((N_DEV, tr, C), lambda i: (0, i, 0))] + [spec] * 3,
        out_specs=[spec] * 4, out_shape=[jax.ShapeDtypeStruct((R, C), F32)] * 4,
        compiler_params=_cp(("parallel",)))(landed, w, m, v)


def _small_update(landed, ws, ms, vs):
    k = len(ws)

    def body(*refs):
        s_ref, w_refs, m_refs, v_refs = refs[0], refs[1:1 + k], refs[1 + k:1 + 2 * k], refs[1 + 2 * k:1 + 3 * k]
        outs = refs[1 + 3 * k:]
        g_all = _slot_total(s_ref)
        for i in range(k):
            n = w_refs[i].shape[1]
            g = g_all[i:i + 1, 0:n]
            outs[i][...] = g
            outs[k + i][...], outs[2 * k + i][...], outs[3 * k + i][...] = _adamw_math(
                w_refs[i][...], g, m_refs[i][...], v_refs[i][...])

    return pl.pallas_call(body, name="small_update",
                          out_shape=[jax.ShapeDtypeStruct(w.shape, F32) for w in ws] * 4)(landed, *ws, *ms, *vs)


SHARDED = {"w_in": "cols", "w_branch_ssm": "rows", "w_branch_attn": "rows", "w_mix_out": "rows", "w_up": "cols",
           "w_down": "rows", "ssm_conv_w": "cols", "ffn_conv_w": "cols"}
LATE = ("w_branch_ssm", "w_branch_attn", "w_mix_out", "w_up", "w_down")
SHORT = {"w_branch_ssm": "bs", "w_branch_attn": "ba", "w_mix_out": "mo", "w_up": "up", "w_down": "down"}
SMALL = ("rel_bias", "b_gate", "ssm_conv_b", "ssm_dt_bias", "ssm_a_log", "ssm_d", "ssm_norm_w", "attn_sinks",
         "ln1_g", "ln1_b", "ffn_conv_b", "ln2_g", "ln2_b")
WEIGHTS = ("rel_bias", "w_in", "b_gate", "ssm_conv_w", "ssm_conv_b", "ssm_dt_bias", "ssm_a_log", "ssm_d", "ssm_norm_w",
           "attn_sinks", "w_branch_ssm", "w_branch_attn", "w_mix_out", "ln1_g", "ln1_b", "w_up", "ffn_conv_w",
           "ffn_conv_b", "w_down", "ln2_g", "ln2_b")
SMALL_ROWS, SMALL_COLS = 16, 2 * D_FF


def _by_device(full, how):
    r, c = full.shape
    if how == "rows":
        return full.reshape(N_DEV, r // N_DEV, c)
    return full.reshape(r, N_DEV, c // N_DEV).transpose(1, 0, 2)


def _from_devices(slots, how):
    _, r, c = slots.shape
    if how == "rows":
        return slots.reshape(N_DEV * r, c)
    return slots.transpose(1, 0, 2).reshape(r, N_DEV * c)


def kernel(x, rel_bias, w_in, b_gate, ssm_conv_w, ssm_conv_b, ssm_dt_bias, ssm_a_log, ssm_d, ssm_norm_w, attn_sinks, w_branch_ssm, w_branch_attn, w_mix_out, ln1_g, ln1_b, w_up, ffn_conv_w, ffn_conv_b, w_down, ln2_g, ln2_b, loss_target, m_rel_bias, m_w_in, m_b_gate, m_ssm_conv_w, m_ssm_conv_b, m_ssm_dt_bias, m_ssm_a_log, m_ssm_d, m_ssm_norm_w, m_attn_sinks, m_w_branch_ssm, m_w_branch_attn, m_w_mix_out, m_ln1_g, m_ln1_b, m_w_up, m_ffn_conv_w, m_ffn_conv_b, m_w_down, m_ln2_g, m_ln2_b, v_rel_bias, v_w_in, v_b_gate, v_ssm_conv_w, v_ssm_conv_b, v_ssm_dt_bias, v_ssm_a_log, v_ssm_d, v_ssm_norm_w, v_attn_sinks, v_w_branch_ssm, v_w_branch_attn, v_w_mix_out, v_ln1_g, v_ln1_b, v_w_up, v_ffn_conv_w, v_ffn_conv_b, v_w_down, v_ln2_g, v_ln2_b):
    W = dict(zip(WEIGHTS, (rel_bias, w_in, b_gate, ssm_conv_w, ssm_conv_b, ssm_dt_bias, ssm_a_log, ssm_d, ssm_norm_w,
                           attn_sinks, w_branch_ssm, w_branch_attn, w_mix_out, ln1_g, ln1_b, w_up, ffn_conv_w,
                           ffn_conv_b, w_down, ln2_g, ln2_b)))
    M = dict(zip(WEIGHTS, (m_rel_bias, m_w_in, m_b_gate, m_ssm_conv_w, m_ssm_conv_b, m_ssm_dt_bias, m_ssm_a_log, m_ssm_d,
                           m_ssm_norm_w, m_attn_sinks, m_w_branch_ssm, m_w_branch_attn, m_w_mix_out, m_ln1_g, m_ln1_b,
                           m_w_up, m_ffn_conv_w, m_ffn_conv_b, m_w_down, m_ln2_g, m_ln2_b)))
    V = dict(zip(WEIGHTS, (v_rel_bias, v_w_in, v_b_gate, v_ssm_conv_w, v_ssm_conv_b, v_ssm_dt_bias, v_ssm_a_log, v_ssm_d,
                           v_ssm_norm_w, v_attn_sinks, v_w_branch_ssm, v_w_branch_attn, v_w_mix_out, v_ln1_g, v_ln1_b,
                           v_w_up, v_ffn_conv_w, v_ffn_conv_b, v_w_down, v_ln2_g, v_ln2_b)))
    shard2d = lambda a: a.reshape(a.shape[-2], a.shape[-1])

    (win_all,) = _all_gather([shard2d(w_in).astype(BF16)], "gather_w_in")
    main, tail = _split_w_in(_from_devices(win_all, "cols"))
    conv_all = _all_gather([shard2d(ssm_conv_w), shard2d(ffn_conv_w)], "gather_conv_weights")
    late_shards = [shard2d(W[n]).astype(BF16) for n in LATE]
    late = lambda landed: {SHORT[n]: _from_devices(a, SHARDED[n]) for n, a in zip(LATE, landed)}
    p = _prep_params(rel_bias, b_gate, _from_devices(conv_all[0], "cols"), ssm_conv_b, ssm_dt_bias, ssm_a_log, ssm_d,
                     ssm_norm_w, attn_sinks, ln1_g, ln1_b, _from_devices(conv_all[1], "cols"), ffn_conv_b, ln2_g, ln2_b)

    early_names = LATE + ("ssm_conv_w", "ffn_conv_w")
    early = lambda g: [_by_device(g[n], SHARDED[n]).astype(BF16 if n in LATE else F32) for n in early_names]
    sq, dproj, w, g, landed = _local_step(x[0], loss_target[0], {"in_main": main, "in_tail": tail}, p,
                                          (late_shards, late), early)
    landed = dict(zip(early_names, landed))
    g_w_in = _join_w_in(g.pop("in_main"), g.pop("in_tail"))
    dx, landed["w_in"] = _grad_x(dproj, w, exchange=[_by_device(g_w_in, "cols").astype(BF16)])
    loss = (0.5 / D_MODEL) * lax.psum(sq[0, 0], ("x", "y", "c"))
    grads, deltas, new_m, new_v = {}, {}, {}, {}
    for n in SHARDED:
        outs = _adamw(landed[n], shard2d(W[n]), shard2d(M[n]), shard2d(V[n]), "adamw_" + n)
        grads[n], deltas[n], new_m[n], new_v[n] = (a.reshape(W[n].shape) for a in outs)

    row = lambda a: a.reshape(1, -1)
    packed = jnp.concatenate([jnp.pad(row(g[n]), ((0, 0), (0, SMALL_COLS - g[n].size))) for n in SMALL]
                             + [jnp.zeros((SMALL_ROWS - len(SMALL), SMALL_COLS), F32)], axis=0)
    (small_all,) = _all_gather([packed], "gather_small_grads")
    outs = _small_update(small_all, *[[row(src[n]) for n in SMALL] for src in (W, M, V)])
    for i, n in enumerate(SMALL):
        grads[n], deltas[n], new_m[n], new_v[n] = (outs[j * len(SMALL) + i].reshape(W[n].shape) for j in range(4))

    return (loss, dx[None], *[grads[n] for n in WEIGHTS], *[deltas[n] for n in WEIGHTS],
            *[new_m[n] for n in WEIGHTS], *[new_v[n] for n in WEIGHTS])
```

```python
import functools
import math

import jax
import jax.numpy as jnp
from jax import lax
from jax.experimental import pallas as pl
from jax.experimental.pallas import tpu as pltpu

F32, BF16 = jnp.float32, jnp.bfloat16
HIGHEST = lax.Precision.HIGHEST
MESH_ID = pl.DeviceIdType.MESH

N_DEV = 8
D_MODEL = 1024
SSM_INNER = 2048
SSM_HEADS = 32
SSM_HEADDIM = 64
SSMD = SSM_HEADDIM
SSM_GROUPS = 4
SSM_GROUP_COLS = SSM_INNER // SSM_GROUPS
SSM_STATE = 128
SSM_CONV = 4
CHUNK = 128
XBC_COLS = SSM_INNER + 2 * SSM_GROUPS * SSM_STATE
B_OFF = SSM_INNER
C_OFF = SSM_INNER + SSM_GROUPS * SSM_STATE
ATTN_HEADS = 16
ATTN_KV = 2
ATTN_GROUP = 8
HEADDIM = 64
WINDOW = 128
REL_BUCKETS = 32
REL_MAX_DIST = 128
D_FF = 2816
FFN_CONV = 3
ALPHA = 2.0 ** 0.25
LN_EPS = 1e-5
RMS_EPS = 1e-5
IN_COLS = 8480
Z_OFF, XBC_OFF, Q_OFF, GATE_OFF, MAIN_COLS = 0, 2048, 5120, 6144, 8192
K_OFF, V_OFF, DT_OFF, TAIL_COLS = 0, 128, 256, 384
O_Z, O_XBC, O_DT, O_Q, O_K, O_V, O_GATE = 0, 2048, 5120, 5152, 6176, 6304, 6432

ADAM_LR, ADAM_B1, ADAM_B2, ADAM_EPS, ADAM_WD, ADAM_STEP = 0.001, 0.9, 0.999, 1e-08, 0.01, 10
NEG = -1e30
HALO = 8
VMEM_LIMIT = 56 * 1024 * 1024


def _cp(sem):
    return pltpu.CompilerParams(dimension_semantics=sem, vmem_limit_bytes=VMEM_LIMIT)


def _const_spec(shape):
    nd = len(shape)
    return pl.BlockSpec(shape, lambda *_: (0,) * nd)


def _sigmoid(x):
    return 0.5 * jnp.tanh(0.5 * x) + 0.5


def _softplus(x):
    return jnp.maximum(x, 0.0) + jnp.log1p(jnp.exp(-jnp.abs(x)))


def _dot(a, b, dims=(((1,), (0,)), ((), ())), precision=None):
    return lax.dot_general(a, b, dims, preferred_element_type=F32, precision=precision)


NN = (((1,), (0,)), ((), ()))
NT = (((1,), (1,)), ((), ()))
TN = (((0,), (0,)), ((), ()))


def _mesh_pos():
    return lax.axis_index("x"), lax.axis_index("y"), lax.axis_index("c")


PEERS = N_DEV - 1


def _exchange_phases(in_refs, out_refs, send_sems, recv_sems, local_sems):
    def copies():
        x, y, c = _mesh_pos()
        me = 4 * x + 2 * y + c
        cps = []
        for b, (in_ref, out_ref) in enumerate(zip(in_refs, out_refs)):
            cps.append(pltpu.make_async_copy(in_ref.at[me], out_ref.at[me], local_sems.at[b]))
            for r in range(1, N_DEV):
                px = 1 - x if r & 4 else x
                py = 1 - y if r & 2 else y
                pc = 1 - c if r & 1 else c
                cps.append(pltpu.make_async_remote_copy(
                    src_ref=in_ref.at[4 * px + 2 * py + pc], dst_ref=out_ref.at[me],
                    send_sem=send_sems.at[b * PEERS + r - 1], recv_sem=recv_sems.at[b * PEERS + r - 1],
                    device_id=(px, py, pc), device_id_type=MESH_ID))
        return cps

    def start():
        for cp in copies():
            cp.start()

    def finish():
        for cp in copies():
            cp.wait()

    return [start, finish]


def _gather_phases(x_refs, out_refs, send_sems, recv_sems, local_sems):
    def parts(which):
        x, y, c = _mesh_pos()
        me, sibling = (x, y, c), (x, y, 1 - c)
        chips = [(1 - x, y), (x, 1 - y), (1 - x, 1 - y)]
        found = []
        for b, (x_ref, out_ref) in enumerate(zip(x_refs, out_refs)):
            def slot(px, py, pc):
                return out_ref.at[4 * px + 2 * py + pc]

            def copy(k, block, to, src=None):
                return pltpu.make_async_remote_copy(
                    src_ref=slot(*block) if src is None else src, dst_ref=slot(*block),
                    send_sem=send_sems.at[b * PEERS + k], recv_sem=recv_sems.at[b * PEERS + k],
                    device_id=to, device_id_type=MESH_ID)

            if which == "mine":
                found.append(pltpu.make_async_copy(x_ref, slot(*me), local_sems.at[b]))
            elif which == "first":
                found.append(copy(0, me, sibling, src=x_ref))
                found += [copy(1 + j, me, (*chip, c), src=x_ref) for j, chip in enumerate(chips)]
            elif which == "passed":
                found += [copy(4 + j, (*chip, c), sibling) for j, chip in enumerate(chips)]
            elif which == "arrived":
                found += [copy(1 + j, (*chip, c), me) for j, chip in enumerate(chips)]
            else:
                found.append(copy(0, sibling, me))
                found += [copy(4 + j, (*chip, 1 - c), me) for j, chip in enumerate(chips)]
        return found

    def start():
        for cp in parts("mine") + parts("first"):
            cp.start()

    def forward():
        for a, p in zip(parts("arrived"), parts("passed")):
            a.wait_recv()
            p.start()

    def finish():
        for cp in parts("late"):
            cp.wait_recv()
        for cp in parts("first") + parts("passed"):
            cp.wait_send()
        for cp in parts("mine"):
            cp.wait()

    return [start, forward, finish]


COMM = {"exchange": _exchange_phases, "gather": _gather_phases}


def _comm_sems(nb):
    return [pltpu.SemaphoreType.DMA((nb * PEERS,)), pltpu.SemaphoreType.DMA((nb * PEERS,)), pltpu.SemaphoreType.DMA((nb,))]


def _matmul(a, b, mode, out_dtype, name, tm=512, tn=1024, tk=1024, addend=None, addend_scale=1.0, comm=None):
    bufs = [] if comm is None else list(comm[1])
    nb = len(bufs)
    if mode == "nn":
        (M, K), (K2, N) = a.shape, b.shape
    elif mode == "nt":
        (M, K), (N, K2) = a.shape, b.shape
    else:
        (K, M), (K2, N) = a.shape, b.shape
    assert K == K2, (a.shape, b.shape, mode)
    tm, tn, tk = min(tm, M), min(tn, N), min(tk, K)
    assert M % tm == 0 and N % tn == 0 and K % tk == 0, (M, N, K, tm, tn, tk)
    nk = K // tk
    dims = {"nn": NN, "nt": NT, "tn": TN}[mode]
    a_spec = pl.BlockSpec((tk, tm), lambda i, j, k: (k, i)) if mode == "tn" else pl.BlockSpec((tm, tk), lambda i, j, k: (i, k))
    b_spec = pl.BlockSpec((tn, tk), lambda i, j, k: (j, k)) if mode == "nt" else pl.BlockSpec((tk, tn), lambda i, j, k: (k, j))
    o_spec = pl.BlockSpec((tm, tn), lambda i, j, k: (i, j))

    ni, nj = M // tm, N // tn

    def body(*refs):
        refs = list(refs)
        a_ref, b_ref = refs[:2]
        c_ref = refs[2] if addend is not None else None
        n_in = 2 + (addend is not None) + nb
        o_ref, acc = refs[n_in], refs[n_in + 1 + nb]
        i, j, k = pl.program_id(0), pl.program_id(1), pl.program_id(2)
        step = (i * nj + j) * nk + k
        if comm is not None:
            phases = COMM[comm[0]](refs[n_in - nb:n_in], refs[n_in + 1:n_in + 1 + nb], *refs[n_in + 2 + nb:])
            at = [(ni * nj * nk - 1) * p // (len(phases) - 1) for p in range(len(phases))]
            for when, phase in zip(at[:-1], phases[:-1]):
                pl.when(step == when)(phase)

        d = _dot(a_ref[...].astype(BF16), b_ref[...].astype(BF16), dims)

        def finish(r):
            if addend is not None:
                r = r + addend_scale * c_ref[...].astype(F32)
            o_ref[...] = r.astype(out_dtype)

        if nk == 1:
            finish(d)
        else:
            @pl.when(k == 0)
            def _():
                acc[...] = d

            @pl.when(jnp.logical_and(k > 0, k < nk - 1))
            def _():
                acc[...] += d

            @pl.when(k == nk - 1)
            def _():
                finish(acc[...] + d)

        if comm is not None:
            pl.when(step == at[-1])(phases[-1])

    in_specs = [a_spec, b_spec] + ([o_spec] if addend is not None else [])
    args = (a, b) + ((addend,) if addend is not None else ())
    out_specs, out_shape = o_spec, jax.ShapeDtypeStruct((M, N), out_dtype)
    scratch = [pltpu.VMEM((tm, tn), F32)]
    sem = ("parallel", "parallel", "arbitrary")
    if comm is not None:
        any_spec = pl.BlockSpec(memory_space=pl.ANY)
        in_specs, args = in_specs + [any_spec] * nb, args + tuple(bufs)
        landed = [x.shape if comm[0] == "exchange" else (N_DEV,) + x.shape for x in bufs]
        out_specs = [o_spec] + [any_spec] * nb
        out_shape = [out_shape] + [jax.ShapeDtypeStruct(s, x.dtype) for s, x in zip(landed, bufs)]
        scratch += _comm_sems(nb)
        sem = ("arbitrary", "arbitrary", "arbitrary")
    return pl.pallas_call(
        body, name=name, grid=(ni, nj, nk), in_specs=in_specs, out_specs=out_specs, out_shape=out_shape,
        scratch_shapes=scratch, compiler_params=_cp(sem))(*args)


def _taps(w_ref, K, tc):
    return [jnp.broadcast_to(w_ref[k:k + 1, :], (HALO, tc)) for k in range(K)]


def _conv_silu_fwd(pre, pre_col_off, C, w, b, K, name, tr=1024, tc=512):
    T = pre.shape[0]
    tr, tc = min(tr, T), min(tc, C)
    assert T % tr == 0 and C % tc == 0 and pre_col_off % tc == 0
    joff = pre_col_off // tc
    hb = tr // HALO

    def body(x_ref, xp_ref, w_ref, b_ref, o_ref, d_ref, head):
        i = pl.program_id(1)
        head[0:HALO, :] = jnp.where(i > 0, xp_ref[...], 0.0)
        head[HALO:, :] = x_ref[0:HALO, :]
        wk = _taps(w_ref, K, tc)
        bias = jnp.broadcast_to(b_ref[...], (HALO, tc))
        for r in range(tr // HALO):
            lo = r * HALO
            co = bias + wk[K - 1] * x_ref[lo:lo + HALO, :]
            for k in range(K - 1):
                s = K - 1 - k
                co = co + wk[k] * (head[HALO - s:2 * HALO - s, :] if r == 0 else x_ref[lo - s:lo + HALO - s, :])
            sg = _sigmoid(co)
            y = co * sg
            o_ref[lo:lo + HALO, :] = y
            d_ref[lo:lo + HALO, :] = sg + y * (1.0 - sg)

    out = pl.BlockSpec((tr, tc), lambda j, i: (i, j))
    return pl.pallas_call(
        body, name=name, grid=(C // tc, T // tr),
        in_specs=[pl.BlockSpec((tr, tc), lambda j, i: (i, joff + j)),
                  pl.BlockSpec((HALO, tc), lambda j, i: (jnp.maximum(i * hb - 1, 0), joff + j)),
                  pl.BlockSpec((K, tc), lambda j, i: (0, j)),
                  pl.BlockSpec((1, tc), lambda j, i: (0, j))],
        out_specs=[out, out], out_shape=[jax.ShapeDtypeStruct((T, C), F32)] * 2,
        scratch_shapes=[pltpu.VMEM((2 * HALO, tc), F32)],
        compiler_params=_cp(("parallel", "arbitrary")))(pre, pre, w, b)


def _conv_bwd(dout, pre, pre_col_off, C, w, K, dst, dst_col_off, name, tr=1024, tc=512):
    T = pre.shape[0]
    tr, tc = min(tr, T), min(tc, C)
    assert T % tr == 0 and C % tc == 0 and pre_col_off % tc == 0 and dst_col_off % tc == 0
    joff, doff = pre_col_off // tc, dst_col_off // tc
    hb = tr // HALO
    nt = T // tr
    n = tr // HALO
    last_hblock = T // HALO - 1

    def body(g_ref, gn_ref, x_ref, w_ref, *rest):
        o_ref, dw_ref, db_ref, edge = rest[-4:]
        i = pl.program_id(1)
        wk = _taps(w_ref, K, tc)
        edge[0:HALO, :] = g_ref[tr - HALO:tr, :]
        edge[HALO:, :] = jnp.where(i < nt - 1, gn_ref[...], 0.0)
        acc_w = [jnp.zeros((HALO, tc), F32) for _ in range(K)]
        acc_b = jnp.zeros((HALO, tc), F32)
        for r in range(n):
            lo = r * HALO
            x = x_ref[lo:lo + HALO, :]
            dpre = None
            for s in range(K):
                gs = edge[s:HALO + s, :] if (r == n - 1 and s > 0) else g_ref[lo + s:lo + HALO + s, :]
                dpre = wk[K - 1 - s] * gs if dpre is None else dpre + wk[K - 1 - s] * gs
                acc_w[K - 1 - s] = acc_w[K - 1 - s] + gs * x
                if s == 0:
                    acc_b = acc_b + gs
            o_ref[lo:lo + HALO, :] = dpre.astype(o_ref.dtype)

        @pl.when(i == 0)
        def _():
            dw_ref[...] = jnp.zeros_like(dw_ref)
            db_ref[...] = jnp.zeros_like(db_ref)

        db_ref[...] += jnp.sum(acc_b, axis=0, keepdims=True)
        dw_ref[...] += jnp.concatenate([jnp.sum(a, axis=0, keepdims=True) for a in acc_w], axis=0)

    tile = lambda off: pl.BlockSpec((tr, tc), lambda j, i: (i, off + j))
    nxt_row = lambda i: jnp.minimum((i + 1) * hb, last_hblock)
    if dout.ndim == 3:
        nh = C // 2 // tc
        g_specs = [pl.BlockSpec((None, tr, tc), lambda j, i: (j // nh, i, j % nh)),
                   pl.BlockSpec((None, HALO, tc), lambda j, i: (j // nh, nxt_row(i), j % nh))]
    else:
        g_specs = [tile(0), pl.BlockSpec((HALO, tc), lambda j, i: (nxt_row(i), j))]
    in_specs = g_specs + [tile(joff), pl.BlockSpec((K, tc), lambda j, i: (0, j))]
    args = (dout, dout, pre, w)
    if isinstance(dst, jax.ShapeDtypeStruct):
        aliases = {}
    else:
        in_specs.append(pl.BlockSpec(memory_space=pl.ANY))
        args += (dst,)
        aliases = {4: 0}
    return pl.pallas_call(
        body, name=name, grid=(C // tc, nt), in_specs=in_specs,
        out_specs=[tile(doff), pl.BlockSpec((K, tc), lambda j, i: (0, j)), pl.BlockSpec((1, tc), lambda j, i: (0, j))],
        out_shape=[jax.ShapeDtypeStruct(dst.shape, dst.dtype), jax.ShapeDtypeStruct((K, C), F32),
                   jax.ShapeDtypeStruct((1, C), F32)],
        scratch_shapes=[pltpu.VMEM((2 * HALO, tc), F32)],
        input_output_aliases=aliases,
        compiler_params=_cp(("parallel", "arbitrary")))(*args)


PAIR = 2 * SSMD
PAIRS_PER_GROUP = SSM_GROUP_COLS // PAIR


def _dot3(x, onehot):
    h1 = x.astype(BF16)
    r = x - h1.astype(F32)
    h2 = r.astype(BF16)
    h3 = (r - h2.astype(F32)).astype(BF16)
    return _dot(h1, onehot) + _dot(h2, onehot) + _dot(h3, onehot)


def _chunk_rows(dt_raw, dtb_col, alog_col):
    row = lax.broadcasted_iota(jnp.int32, (CHUNK, CHUNK), 0)
    col = lax.broadcasted_iota(jnp.int32, (CHUNK, CHUNK), 1)
    dt_rawT = dt_raw.T
    dtT = _softplus(dt_rawT + dtb_col)
    a_col = -jnp.exp(alog_col)
    acsT = _dot3(dtT * a_col, (row <= col).astype(BF16))
    return dt_rawT, dtT, a_col, acsT, row, col


def _block_diag(x, left):
    return jnp.concatenate([jnp.where(left, x, 0.0), jnp.where(left, 0.0, x)], axis=0).astype(BF16)


def _lane_bcast(v, h):
    return jnp.broadcast_to(v[:, h:h + 1], (CHUNK, CHUNK))


def _ssd_fwd(xbc, proj_main, proj_tail, dtb_col, alog_col, d_exp, norm_w):
    T = xbc.shape[0]
    nc = T // CHUNK

    def body(xbc_ref, dt_ref, z_ref, dtb_ref, alog_ref, d_ref, nw_ref, y_ref, ypre_ref, hs_ref, H):
        c = pl.program_id(0)

        @pl.when(c == 0)
        def _():
            H[...] = jnp.zeros_like(H)

        hs_ref[0] = H[...]
        _, dtT, _, acsT, row, col = _chunk_rows(dt_ref[:, 0:SSM_HEADS], dtb_ref[...], alog_ref[...])
        tril, left = row >= col, col < SSMD
        acs = acsT.T
        w = (dtT * jnp.exp(acsT[:, CHUNK - 1:CHUNK] - acsT)).T
        cd = jnp.exp(acs[CHUNK - 1:CHUNK, :])
        for g in range(SSM_GROUPS):
            gs = slice(g * SSM_GROUP_COLS, (g + 1) * SSM_GROUP_COLS)
            Bb = xbc_ref[:, B_OFF + g * SSM_STATE:B_OFF + (g + 1) * SSM_STATE].astype(BF16)
            Cb = xbc_ref[:, C_OFF + g * SSM_STATE:C_OFF + (g + 1) * SSM_STATE].astype(BF16)
            Hg = H[:, gs]
            CH = _dot(Cb, Hg.astype(BF16))
            CB = _dot(Cb, Bb, NT)
            ys, xws = [], []
            for kk in range(PAIRS_PER_GROUP):
                k = g * PAIRS_PER_GROUP + kk
                xs_p = xbc_ref[:, k * PAIR:(k + 1) * PAIR]
                mps, ecols, wcols = [], [], []
                for j in range(2):
                    h = 2 * k + j
                    colb = _lane_bcast(acs, h)
                    L = jnp.exp(jnp.where(tril, colb - acsT[h:h + 1, :], -jnp.inf))
                    mps.append((CB * L * dtT[h:h + 1, :]).astype(BF16))
                    ecols.append(jnp.exp(colb))
                    wcols.append(_lane_bcast(w, h))
                yd = _dot(jnp.concatenate(mps, axis=1), _block_diag(xs_p, left))
                ys.append(yd + CH[:, kk * PAIR:(kk + 1) * PAIR] * jnp.where(left, ecols[0], ecols[1]))
                xws.append((xs_p * jnp.where(left, wcols[0], wcols[1])).astype(BF16))
            cd_e = jnp.concatenate([jnp.broadcast_to(cd[:, g * 8 + e:g * 8 + e + 1], (1, SSMD)) for e in range(8)], axis=1)
            H[:, gs] = Hg * cd_e + _dot(Bb, jnp.concatenate(xws, axis=1), TN)
            ypre = jnp.concatenate(ys, axis=1) + xbc_ref[:, gs] * d_ref[:, gs]
            ypre_ref[:, gs] = ypre
            z = z_ref[:, gs]
            yg = ypre * (z * _sigmoid(z))
            r = lax.rsqrt(jnp.mean(yg * yg, axis=1, keepdims=True) + RMS_EPS)
            y_ref[:, gs] = (yg * r * nw_ref[:, gs]).astype(BF16)

    vec = lambda n: _const_spec((1, n))
    colv = _const_spec((SSM_HEADS, 1))
    return pl.pallas_call(
        body, name="ssd_fwd", grid=(nc,),
        in_specs=[pl.BlockSpec((CHUNK, XBC_COLS), lambda c: (c, 0)),
                  pl.BlockSpec((CHUNK, 128), lambda c: (c, DT_OFF // 128)),
                  pl.BlockSpec((CHUNK, SSM_INNER), lambda c: (c, Z_OFF // SSM_INNER)),
                  colv, colv, vec(SSM_INNER), vec(SSM_INNER)],
        out_specs=[pl.BlockSpec((CHUNK, SSM_INNER), lambda c: (c, 0)),
                   pl.BlockSpec((CHUNK, SSM_INNER), lambda c: (c, 0)),
                   pl.BlockSpec((1, SSM_STATE, SSM_INNER), lambda c: (c, 0, 0))],
        out_shape=[jax.ShapeDtypeStruct((T, SSM_INNER), BF16), jax.ShapeDtypeStruct((T, SSM_INNER), F32),
                   jax.ShapeDtypeStruct((nc, SSM_STATE, SSM_INNER), F32)],
        scratch_shapes=[pltpu.VMEM((SSM_STATE, SSM_INNER), F32)],
        compiler_params=_cp(("arbitrary",)))(xbc, proj_tail, proj_main, dtb_col, alog_col, d_exp, norm_w)


def _ssd_bwd(dyo, ypre, xbc, dsil, hs, proj_main, proj_tail, dtb_col, alog_col, d_exp, norm_w, ehead_t, dmain, dtail):
    T = xbc.shape[0]
    nc = T // CHUNK

    def body(dyo_ref, ypre_ref, xbc_ref, dsil_ref, hs_ref, dt_ref, z_ref, dtb_ref, alog_ref, d_ref, nw_ref, eh_ref,
             dmain_in, dtail_in, dz_ref, ddt_ref, dxbc_ref, dnw_ref, dd_ref, dalog_ref, ddtb_ref, G):
        del dmain_in, dtail_in
        c = pl.program_id(0)

        @pl.when(c == 0)
        def _():
            G[...] = jnp.zeros_like(G)
            dnw_ref[...] = jnp.zeros_like(dnw_ref)
            dd_ref[...] = jnp.zeros_like(dd_ref)
            dalog_ref[...] = jnp.zeros_like(dalog_ref)
            ddtb_ref[...] = jnp.zeros_like(ddtb_ref)

        dt_rawT, dtT, a_col, acsT, row, col = _chunk_rows(dt_ref[:, 0:SSM_HEADS], dtb_ref[...], alog_ref[...])
        tril, triu, left = row >= col, col >= row, col < SSMD
        acs = acsT.T
        dt = dtT.T
        lastT = acsT[:, CHUNK - 1:CHUNK]
        dstT = jnp.exp(lastT - acsT)
        wT = dtT * dstT
        cd = jnp.exp(acs[CHUNK - 1:CHUNK, :])
        ddt_rows, rs_rows, deo_rows, dw_rows = [], [], [], []
        dd_cols, gh_cols, dnw_cols = [], [], []
        for g in range(SSM_GROUPS):
            gs = slice(g * SSM_GROUP_COLS, (g + 1) * SSM_GROUP_COLS)
            z = z_ref[:, gs]
            sz = _sigmoid(z)
            silu_z = z * sz
            ypre = ypre_ref[:, gs]
            yg = ypre * silu_z
            r = lax.rsqrt(jnp.mean(yg * yg, axis=1, keepdims=True) + RMS_EPS)
            ygn = yg * r
            dyo = dyo_ref[:, gs]
            dyn = dyo * nw_ref[:, gs]
            dnw_cols.append(jnp.sum(dyo * ygn, axis=0, keepdims=True))
            dyg = r * (dyn - ygn * jnp.mean(dyn * ygn, axis=1, keepdims=True))
            dz_ref[:, gs] = (dyg * ypre * (sz * (1.0 + z * (1.0 - sz)))).astype(dz_ref.dtype)
            dY = dyg * silu_z
            xs = xbc_ref[:, gs]
            dd_cols.append(jnp.sum(dY * xs, axis=0, keepdims=True))
            Bf = xbc_ref[:, B_OFF + g * SSM_STATE:B_OFF + (g + 1) * SSM_STATE]
            Cf = xbc_ref[:, C_OFF + g * SSM_STATE:C_OFF + (g + 1) * SSM_STATE]
            Bb, Cb = Bf.astype(BF16), Cf.astype(BF16)
            BT, CT = Bf.T, Cf.T
            CB = _dot(Cb, Bb, NT)
            CBT = _dot(Bb, Cb, NT)
            Hg = hs_ref[0, :, gs]
            Gg = G[:, gs]
            gh_cols.append(jnp.sum(Gg * Hg, axis=0, keepdims=True))
            dCB = jnp.zeros((CHUNK, CHUNK), F32)
            dxs_d, dyes, xws, wsels = [], [], [], []
            for kk in range(PAIRS_PER_GROUP):
                k = g * PAIRS_PER_GROUP + kk
                ps = slice(kk * PAIR, (kk + 1) * PAIR)
                xs_p, dY_p = xs[:, ps], dY[:, ps]
                Ls, LTs, dtcols, ecols, wcols = [], [], [], [], []
                for j in range(2):
                    h = 2 * k + j
                    colb = _lane_bcast(acs, h)
                    seg = colb - acsT[h:h + 1, :]
                    Ls.append(jnp.exp(jnp.where(tril, seg, -jnp.inf)))
                    LTs.append(jnp.exp(jnp.where(triu, -seg, -jnp.inf)))
                    dtcol = _lane_bcast(dt, h)
                    dtcols.append(dtcol)
                    ecols.append(jnp.exp(colb))
                    wcols.append(dtcol * jnp.exp(acs[CHUNK - 1:CHUNK, h:h + 1] - colb))
                wsel = jnp.where(left, wcols[0], wcols[1])
                dYe_p = dY_p * jnp.where(left, ecols[0], ecols[1])
                bdx = _block_diag(xs_p, left)
                bddy = _block_diag(dY_p, left)
                dMx2 = _dot(dY_p.astype(BF16), bdx, NT)
                dMxT2 = _dot(xs_p.astype(BF16), bddy, NT)
                Q1 = _dot(Hg[:, ps].astype(BF16), _block_diag(dYe_p, left), NT)
                Q2 = _dot(Gg[:, ps].astype(BF16), bdx, NT)
                mts = []
                for j in range(2):
                    h = 2 * k + j
                    js = slice(j * CHUNK, (j + 1) * CHUNK)
                    dMx = dMx2[:, js]
                    A = CB * Ls[j]
                    AT = CBT * LTs[j]
                    ddt_rows.append(jnp.sum(A * dMx, axis=0, keepdims=True))
                    ATd = AT * dtcols[j]
                    rs_rows.append(jnp.sum(ATd * dMxT2[:, js], axis=0, keepdims=True))
                    dCB = dCB + dMx * Ls[j] * dtT[h:h + 1, :]
                    mts.append(ATd.astype(BF16))
                    deo_rows.append(jnp.sum(CT * Q1[:, js], axis=0, keepdims=True))
                    dw_rows.append(jnp.sum(BT * Q2[:, js], axis=0, keepdims=True))
                dxs_d.append(_dot(jnp.concatenate(mts, axis=1), bddy))
                dyes.append(dYe_p.astype(BF16))
                xws.append((xs_p * wsel).astype(BF16))
                wsels.append(wsel)
            dYe_g = jnp.concatenate(dyes, axis=1)
            xw_g = jnp.concatenate(xws, axis=1)
            Hgb, Ggb, dCBb = Hg.astype(BF16), Gg.astype(BF16), dCB.astype(BF16)
            cs = slice(C_OFF + g * SSM_STATE, C_OFF + (g + 1) * SSM_STATE)
            bs = slice(B_OFF + g * SSM_STATE, B_OFF + (g + 1) * SSM_STATE)
            dxbc_ref[:, cs] = (_dot(dYe_g, Hgb, NT) + _dot(dCBb, Bb)) * dsil_ref[:, cs]
            dxbc_ref[:, bs] = (_dot(xw_g, Ggb, NT) + _dot(dCBb, Cb, TN)) * dsil_ref[:, bs]
            BG = _dot(Bb, Ggb)
            dxbc_ref[:, gs] = (jnp.concatenate(dxs_d, axis=1) + BG * jnp.concatenate(wsels, axis=1)
                               + dY * d_ref[:, gs]) * dsil_ref[:, gs]
            cd_e = jnp.concatenate([jnp.broadcast_to(cd[:, g * 8 + e:g * 8 + e + 1], (1, SSMD)) for e in range(8)], axis=1)
            G[:, gs] = Gg * cd_e + _dot(Cb, dYe_g, TN)
        dnw_ref[...] += jnp.concatenate(dnw_cols, axis=1)
        eh = eh_ref[...]
        dd_ref[...] += jnp.sum(eh * jnp.concatenate(dd_cols, axis=1), axis=1, keepdims=True)
        dcd = jnp.sum(eh * jnp.concatenate(gh_cols, axis=1), axis=1, keepdims=True)
        DDT = jnp.concatenate(ddt_rows, axis=0)
        DW = jnp.concatenate(dw_rows, axis=0)
        DWw = DW * wT
        dacsT = jnp.concatenate(rs_rows, axis=0) - DDT * dtT + jnp.concatenate(deo_rows, axis=0) - DWw
        end = jnp.sum(DWw, axis=1, keepdims=True) + dcd * jnp.exp(lastT)
        lane = lax.broadcasted_iota(jnp.int32, (SSM_HEADS, CHUNK), 1)
        dacsT = dacsT + jnp.where(lane == CHUNK - 1, end, 0.0)
        dadtT = _dot3(dacsT, tril.astype(BF16))
        ddtT = dadtT * a_col + DDT + DW * dstT
        dalog_ref[...] += jnp.sum(dadtT * dtT, axis=1, keepdims=True) * a_col
        ddt_rawT = ddtT * _sigmoid(dt_rawT + dtb_ref[...])
        ddtb_ref[...] += jnp.sum(ddt_rawT, axis=1, keepdims=True)
        ddt_ref[...] = jnp.concatenate([ddt_rawT.T, jnp.zeros((CHUNK, 128 - SSM_HEADS), F32)], axis=1).astype(ddt_ref.dtype)

    rev = lambda c: nc - 1 - c
    vec = lambda n: _const_spec((1, n))
    colv = _const_spec((SSM_HEADS, 1))
    any_spec = pl.BlockSpec(memory_space=pl.ANY)
    return pl.pallas_call(
        body, name="ssd_bwd", grid=(nc,),
        in_specs=[pl.BlockSpec((CHUNK, SSM_INNER), lambda c: (rev(c), 0)),
                  pl.BlockSpec((CHUNK, SSM_INNER), lambda c: (rev(c), 0)),
                  pl.BlockSpec((CHUNK, XBC_COLS), lambda c: (rev(c), 0)),
                  pl.BlockSpec((CHUNK, XBC_COLS), lambda c: (rev(c), 0)),
                  pl.BlockSpec((1, SSM_STATE, SSM_INNER), lambda c: (rev(c), 0, 0)),
                  pl.BlockSpec((CHUNK, 128), lambda c: (rev(c), DT_OFF // 128)),
                  pl.BlockSpec((CHUNK, SSM_INNER), lambda c: (rev(c), Z_OFF // SSM_INNER)),
                  colv, colv, vec(SSM_INNER), vec(SSM_INNER), _const_spec((SSM_HEADS, SSM_INNER)), any_spec, any_spec],
        out_specs=[pl.BlockSpec((CHUNK, SSM_INNER), lambda c: (rev(c), Z_OFF // SSM_INNER)),
                   pl.BlockSpec((CHUNK, 128), lambda c: (rev(c), DT_OFF // 128)),
                   pl.BlockSpec((CHUNK, XBC_COLS), lambda c: (rev(c), 0)),
                   vec(SSM_INNER), colv, colv, colv],
        out_shape=[jax.ShapeDtypeStruct(dmain.shape, dmain.dtype), jax.ShapeDtypeStruct(dtail.shape, dtail.dtype),
                   jax.ShapeDtypeStruct((T, XBC_COLS), F32), jax.ShapeDtypeStruct((1, SSM_INNER), F32),
                   jax.ShapeDtypeStruct((SSM_HEADS, 1), F32), jax.ShapeDtypeStruct((SSM_HEADS, 1), F32),
                   jax.ShapeDtypeStruct((SSM_HEADS, 1), F32)],
        scratch_shapes=[pltpu.VMEM((SSM_STATE, SSM_INNER), F32)],
        input_output_aliases={12: 0, 13: 1},
        compiler_params=_cp(("arbitrary",)))(dyo, ypre, xbc, dsil, hs, proj_tail, proj_main, dtb_col, alog_col, d_exp,
                                             norm_w, ehead_t, dmain, dtail)


def _rel_bucket(rel):
    n = jnp.maximum(rel, 0)
    max_exact = REL_BUCKETS // 2
    nf = jnp.maximum(n, 1).astype(F32)
    large = max_exact + (jnp.log(nf / max_exact) / math.log(REL_MAX_DIST / max_exact)
                         * (REL_BUCKETS - max_exact)).astype(jnp.int32)
    large = jnp.minimum(large, REL_BUCKETS - 1)
    return jnp.where(n < max_exact, n, large)


def _band_geometry():
    qi = jnp.arange(WINDOW)[:, None] + WINDOW
    kj = jnp.arange(2 * WINDOW)[None, :]
    rel = qi - kj
    return _rel_bucket(rel), (rel >= 0) & (rel < WINDOW)


def _attn_logits(kband, qh, bias_h, first):
    s = _dot(kband, qh, NT) * (HEADDIM ** -0.5) + bias_h
    rowk = lax.broadcasted_iota(jnp.int32, (2 * WINDOW, WINDOW), 0)
    return jnp.where(jnp.logical_and(first, rowk < WINDOW), NEG, s)


def _attn_fwd(proj_main, proj_tail, bias_tbl, sinks):
    T = proj_main.shape[0]
    nb = T // WINDOW

    def body(q_ref, kv_ref, kvp_ref, bias_ref, sink_ref, o_ref, lse_ref):
        i = pl.program_id(0)
        first = i == 0
        outs, lses = [], []
        for kvh in range(ATTN_KV):
            ks = slice(K_OFF + kvh * HEADDIM, K_OFF + (kvh + 1) * HEADDIM)
            vs = slice(V_OFF + kvh * HEADDIM, V_OFF + (kvh + 1) * HEADDIM)
            kband = jnp.concatenate([kvp_ref[:, ks], kv_ref[:, ks]], axis=0).astype(BF16)
            vband = jnp.concatenate([kvp_ref[:, vs], kv_ref[:, vs]], axis=0).astype(BF16)
            heads = range(kvh * ATTN_GROUP, (kvh + 1) * ATTN_GROUP)
            logits = [_attn_logits(kband, q_ref[:, h * HEADDIM:(h + 1) * HEADDIM].astype(BF16), bias_ref[h], first)
                      for h in heads]
            probs = []
            for h, s in zip(heads, logits):
                sink = sink_ref[:, h:h + 1]
                m = jnp.maximum(jnp.max(s, axis=0, keepdims=True), sink)
                p = jnp.exp(s - m)
                den = jnp.sum(p, axis=0, keepdims=True) + jnp.exp(sink - m)
                probs.append((p * (1.0 / den)).astype(BF16))
                lses.append(m + jnp.log(den))
            outs += [_dot(pt, vband, TN) for pt in probs]
        o_ref[...] = jnp.concatenate(outs, axis=1).astype(BF16)
        lse_ref[...] = jnp.concatenate(lses, axis=0)

    return pl.pallas_call(
        body, name="attn_fwd", grid=(nb,),
        in_specs=[pl.BlockSpec((WINDOW, D_MODEL), lambda i: (i, Q_OFF // D_MODEL)),
                  pl.BlockSpec((WINDOW, 256), lambda i: (i, 0)),
                  pl.BlockSpec((WINDOW, 256), lambda i: (jnp.maximum(i - 1, 0), 0)),
                  _const_spec((ATTN_HEADS, 2 * WINDOW, WINDOW)), _const_spec((1, ATTN_HEADS))],
        out_specs=[pl.BlockSpec((WINDOW, D_MODEL), lambda i: (i, 0)),
                   pl.BlockSpec((ATTN_HEADS, WINDOW), lambda i: (0, i))],
        out_shape=[jax.ShapeDtypeStruct((T, D_MODEL), BF16), jax.ShapeDtypeStruct((ATTN_HEADS, T), F32)],
        compiler_params=_cp(("arbitrary",)))(proj_main, proj_tail, proj_tail, bias_tbl, sinks)


def _attn_bwd(dy, lse, proj_main, proj_tail, bias_tbl, sinks, dmain):
    T = proj_main.shape[0]
    nb = T // WINDOW

    def body(dy_ref, lse_ref, q_ref, kv_ref, kvp_ref, bias_ref, sink_ref, dmain_in,
             dq_ref, dkv_ref, dbias_ref, dsink_ref, carry):
        del dmain_in
        i = pl.program_id(0)
        first = i == 0

        @pl.when(first)
        def _():
            carry[...] = jnp.zeros_like(carry)
            dbias_ref[...] = jnp.zeros_like(dbias_ref)
            dsink_ref[...] = jnp.zeros_like(dsink_ref)

        @pl.when(i < nb)
        def _():
            scale = HEADDIM ** -0.5
            dqs, dsinks, dks, dvs = [], [], [], []
            for kvh in range(ATTN_KV):
                ks = slice(K_OFF + kvh * HEADDIM, K_OFF + (kvh + 1) * HEADDIM)
                vs = slice(V_OFF + kvh * HEADDIM, V_OFF + (kvh + 1) * HEADDIM)
                kband = jnp.concatenate([kvp_ref[:, ks], kv_ref[:, ks]], axis=0).astype(BF16)
                vband = jnp.concatenate([kvp_ref[:, vs], kv_ref[:, vs]], axis=0).astype(BF16)
                heads = range(kvh * ATTN_GROUP, (kvh + 1) * ATTN_GROUP)
                qs = [q_ref[:, h * HEADDIM:(h + 1) * HEADDIM].astype(BF16) for h in heads]
                dos = [dy_ref[:, h * HEADDIM:(h + 1) * HEADDIM] for h in heads]
                logits = [_attn_logits(kband, qh, bias_ref[h], first) for h, qh in zip(heads, qs)]
                dps = [_dot(vband, do, NT) for do in dos]
                pbs, dsbs = [], []
                for h, s, dp in zip(heads, logits, dps):
                    lse_h = lse_ref[h:h + 1, :]
                    p = jnp.exp(s - lse_h)
                    delta = jnp.sum(p * dp, axis=0, keepdims=True)
                    ds = p * (dp - delta)
                    psink = jnp.exp(sink_ref[:, h:h + 1] - lse_h)
                    dsinks.append(-jnp.sum(psink * delta, axis=1, keepdims=True))
                    dbias_ref[h] += ds
                    pbs.append(p.astype(BF16))
                    dsbs.append((ds * scale).astype(BF16))
                dqs += [_dot(dsb, kband, TN) for dsb in dsbs]
                dks.append(_dot(jnp.concatenate(dsbs, axis=1), jnp.concatenate(qs, axis=0)))
                dvs.append(_dot(jnp.concatenate(pbs, axis=1), jnp.concatenate(dos, axis=0)))
            dq_ref[...] = jnp.concatenate(dqs, axis=1).astype(dq_ref.dtype)
            dsink_ref[...] += jnp.concatenate(dsinks, axis=1)
            dkv = jnp.concatenate(dks + dvs, axis=1)
            dkv_ref[...] = (carry[...] + dkv[0:WINDOW, :]).astype(dkv_ref.dtype)
            carry[...] = dkv[WINDOW:, :]

        @pl.when(i == nb)
        def _():
            dkv_ref[...] = carry[...].astype(dkv_ref.dtype)

    cur = lambda i: jnp.minimum(i, nb - 1)
    return pl.pallas_call(
        body, name="attn_bwd", grid=(nb + 1,),
        in_specs=[pl.BlockSpec((WINDOW, D_MODEL), lambda i: (cur(i), 0)),
                  pl.BlockSpec((ATTN_HEADS, WINDOW), lambda i: (0, cur(i))),
                  pl.BlockSpec((WINDOW, D_MODEL), lambda i: (cur(i), Q_OFF // D_MODEL)),
                  pl.BlockSpec((WINDOW, 256), lambda i: (cur(i), 0)),
                  pl.BlockSpec((WINDOW, 256), lambda i: (jnp.maximum(cur(i) - 1, 0), 0)),
                  _const_spec((ATTN_HEADS, 2 * WINDOW, WINDOW)), _const_spec((1, ATTN_HEADS)),
                  pl.BlockSpec(memory_space=pl.ANY)],
        out_specs=[pl.BlockSpec((WINDOW, D_MODEL), lambda i: (cur(i), Q_OFF // D_MODEL)),
                   pl.BlockSpec((WINDOW, 256), lambda i: (jnp.maximum(i - 1, 0), 0)),
                   _const_spec((ATTN_HEADS, 2 * WINDOW, WINDOW)), _const_spec((1, ATTN_HEADS))],
        out_shape=[jax.ShapeDtypeStruct(dmain.shape, dmain.dtype), jax.ShapeDtypeStruct((T, TAIL_COLS), BF16),
                   jax.ShapeDtypeStruct((ATTN_HEADS, 2 * WINDOW, WINDOW), F32),
                   jax.ShapeDtypeStruct((1, ATTN_HEADS), F32)],
        scratch_shapes=[pltpu.VMEM((WINDOW, 256), F32)],
        input_output_aliases={7: 0},
        compiler_params=_cp(("arbitrary",)))(dy, lse, proj_main, proj_tail, proj_tail, bias_tbl, sinks, dmain)


def _bias_table(rel_bias_t, onehot_t, mask):
    def body(rb_ref, oh_ref, m_ref, o_ref):
        o_ref[...] = _dot3(rb_ref[...], oh_ref[...]) + m_ref[...]

    flat = pl.pallas_call(body, name="bias_table",
                          out_shape=jax.ShapeDtypeStruct((ATTN_HEADS, 2 * WINDOW * WINDOW), F32))(rel_bias_t, onehot_t, mask)
    return flat.reshape(ATTN_HEADS, 2 * WINDOW, WINDOW)


def _rel_bias_grad(dbias, onehot):
    def body(d_ref, oh_ref, o_ref):
        o_ref[...] = _dot(d_ref[...], oh_ref[...], NN, HIGHEST)

    return pl.pallas_call(body, name="rel_bias_grad",
                          out_shape=jax.ShapeDtypeStruct((ATTN_HEADS, REL_BUCKETS), F32))(dbias, onehot)


def _ln_fwd(r, g, b):
    mu = jnp.mean(r, axis=1, keepdims=True)
    xc = r - mu
    rstd = lax.rsqrt(jnp.mean(xc * xc, axis=1, keepdims=True) + LN_EPS)
    xhat = xc * rstd
    return xhat * g + b, xhat, rstd


def _ln_bwd(dy, xhat, rstd, g):
    dxh = dy * g
    return rstd * (dxh - jnp.mean(dxh, axis=1, keepdims=True) - xhat * jnp.mean(dxh * xhat, axis=1, keepdims=True))


def _merge_fwd(y_ssm, y_attn, proj_main, b_gate, w_bs, w_ba, tm=512):
    T = y_ssm.shape[0]

    def body(ys_ref, ya_ref, gs_ref, ga_ref, bg_ref, wbs_ref, wba_ref, m_ref, bs_ref, ba_ref):
        bs = _dot(ys_ref[...], wbs_ref[...])
        ba = _dot(ya_ref[...], wba_ref[...])
        g_s = _sigmoid(gs_ref[...] + bg_ref[:, 0:D_MODEL])
        g_a = _sigmoid(ga_ref[...] + bg_ref[:, D_MODEL:])
        m_ref[...] = (g_s * bs + g_a * ba).astype(BF16)
        bs_ref[...] = bs
        ba_ref[...] = ba

    row = lambda w, off=0: pl.BlockSpec((tm, w), lambda i: (i, off))
    return pl.pallas_call(
        body, name="merge_fwd", grid=(T // tm,),
        in_specs=[row(SSM_INNER), row(D_MODEL), row(D_MODEL, GATE_OFF // D_MODEL), row(D_MODEL, GATE_OFF // D_MODEL + 1),
                  _const_spec((1, 2 * D_MODEL)), _const_spec((SSM_INNER, D_MODEL)), _const_spec((D_MODEL, D_MODEL))],
        out_specs=[row(D_MODEL), row(D_MODEL), row(D_MODEL)],
        out_shape=[jax.ShapeDtypeStruct((T, D_MODEL), BF16), jax.ShapeDtypeStruct((T, D_MODEL), F32),
                   jax.ShapeDtypeStruct((T, D_MODEL), F32)],
        compiler_params=_cp(("parallel",)))(y_ssm, y_attn, proj_main, proj_main, b_gate, w_bs, w_ba)


def _mix_ln1(merged, w_mo, x, g1, b1, tm=512):
    T = x.shape[0]

    def body(m_ref, w_ref, x_ref, g_ref, b_ref, r_ref, h_ref):
        r = ALPHA * x_ref[...] + _dot(m_ref[...], w_ref[...])
        r_ref[...] = r
        h_ref[...] = _ln_fwd(r, g_ref[...], b_ref[...])[0]

    row = pl.BlockSpec((tm, D_MODEL), lambda i: (i, 0))
    return pl.pallas_call(
        body, name="mix_ln1", grid=(T // tm,),
        in_specs=[row, _const_spec((D_MODEL, D_MODEL)), row, _const_spec((1, D_MODEL)), _const_spec((1, D_MODEL))],
        out_specs=[row, row],
        out_shape=[jax.ShapeDtypeStruct((T, D_MODEL), F32), jax.ShapeDtypeStruct((T, D_MODEL), F32)],
        compiler_params=_cp(("parallel",)))(merged, w_mo, x, g1, b1)


def _ffn_conv_glu(u_pre, w, b, tr=1024, tc=256):
    T = u_pre.shape[0]
    tr = min(tr, T)
    K = FFN_CONV
    nj = D_FF // tc
    hb = tr // HALO
    assert T % tr == 0 and D_FF % tc == 0

    def body(xg_ref, xgp_ref, xv_ref, xvp_ref, wg_ref, wv_ref, bg_ref, bv_ref, u_ref, a_ref, head_g, head_v):
        i = pl.program_id(1)
        halves = []
        for x_ref, xp_ref, w_ref, b_ref, head in ((xg_ref, xgp_ref, wg_ref, bg_ref, head_g),
                                                  (xv_ref, xvp_ref, wv_ref, bv_ref, head_v)):
            head[0:HALO, :] = jnp.where(i > 0, xp_ref[...], 0.0)
            head[HALO:, :] = x_ref[0:HALO, :]
            halves.append((x_ref, head, _taps(w_ref, K, tc), jnp.broadcast_to(b_ref[...], (HALO, tc))))

        def conv(half, r):
            x_ref, head, wk, bias = halves[half]
            lo = r * HALO
            acc = bias + wk[K - 1] * x_ref[lo:lo + HALO, :]
            for k in range(K - 1):
                s = K - 1 - k
                acc = acc + wk[k] * (head[HALO - s:2 * HALO - s, :] if r == 0 else x_ref[lo - s:lo + HALO - s, :])
            return acc

        for r2 in range(tr // (2 * HALO)):
            acts = []
            for r in (2 * r2, 2 * r2 + 1):
                lo = r * HALO
                ug, uv = conv(0, r), conv(1, r)
                u_ref[0, lo:lo + HALO, :] = ug
                u_ref[1, lo:lo + HALO, :] = uv
                acts.append(ug * _sigmoid(ug) * uv)
            a_ref[2 * r2 * HALO:(2 * r2 + 2) * HALO, :] = jnp.concatenate(acts, axis=0).astype(BF16)

    tile = lambda off: pl.BlockSpec((tr, tc), lambda j, i: (i, off + j))
    prev = lambda off: pl.BlockSpec((HALO, tc), lambda j, i: (jnp.maximum(i * hb - 1, 0), off + j))
    row = lambda rows, off: pl.BlockSpec((rows, tc), lambda j, i: (0, off + j))
    return pl.pallas_call(
        body, name="ffn_conv_glu", grid=(nj, T // tr),
        in_specs=[tile(0), prev(0), tile(nj), prev(nj), row(K, 0), row(K, nj), row(1, 0), row(1, nj)],
        out_specs=[pl.BlockSpec((2, tr, tc), lambda j, i: (0, i, j)), pl.BlockSpec((tr, tc), lambda j, i: (i, j))],
        out_shape=[jax.ShapeDtypeStruct((2, T, D_FF), F32), jax.ShapeDtypeStruct((T, D_FF), BF16)],
        scratch_shapes=[pltpu.VMEM((2 * HALO, tc), F32), pltpu.VMEM((2 * HALO, tc), F32)],
        compiler_params=_cp(("parallel", "arbitrary")))(u_pre, u_pre, u_pre, u_pre, w, w, b, b)


def _down_ln2_loss(act, w_down, h1, target, g2, b2, tm=512):
    T = h1.shape[0]

    def body(a_ref, w_ref, h_ref, t_ref, g_ref, b_ref, dr_ref, dg_ref, db_ref, l_ref):
        @pl.when(pl.program_id(0) == 0)
        def _():
            dg_ref[...] = jnp.zeros_like(dg_ref)
            db_ref[...] = jnp.zeros_like(db_ref)
            l_ref[...] = jnp.zeros_like(l_ref)

        r = ALPHA * h_ref[...] + _dot(a_ref[...], w_ref[...])
        y, xhat, rstd = _ln_fwd(r, g_ref[...], b_ref[...])
        err = y - t_ref[...]
        l_ref[...] += jnp.sum(err * err, keepdims=True)
        dy = err * (1.0 / D_MODEL)
        dg_ref[...] += jnp.sum(dy * xhat, axis=0, keepdims=True)
        db_ref[...] += jnp.sum(dy, axis=0, keepdims=True)
        dr_ref[...] = _ln_bwd(dy, xhat, rstd, g_ref[...])

    row = pl.BlockSpec((tm, D_MODEL), lambda i: (i, 0))
    vec = _const_spec((1, D_MODEL))
    return pl.pallas_call(
        body, name="down_ln2_loss", grid=(T // tm,),
        in_specs=[pl.BlockSpec((tm, D_FF), lambda i: (i, 0)), _const_spec((D_FF, D_MODEL)), row, row, vec, vec],
        out_specs=[row, vec, vec, _const_spec((1, 1))],
        out_shape=[jax.ShapeDtypeStruct((T, D_MODEL), F32), jax.ShapeDtypeStruct((1, D_MODEL), F32),
                   jax.ShapeDtypeStruct((1, D_MODEL), F32), jax.ShapeDtypeStruct((1, 1), F32)],
        compiler_params=_cp(("arbitrary",)))(act, w_down, h1, target, g2, b2)


def _ffn_bwd_act(dr2, w_down, u, tm=512, tn=1408):
    T = dr2.shape[0]
    tm = min(tm, T)
    nj = D_FF // tn

    def body(d_ref, w_ref, u_ref, o_ref, dact):
        dact[...] = _dot(d_ref[...].astype(BF16), w_ref[...], NT)
        for r in range(tm // HALO):
            rows = slice(r * HALO, (r + 1) * HALO)
            da, g, v = dact[rows, :], u_ref[0, rows, :], u_ref[1, rows, :]
            sg = _sigmoid(g)
            o_ref[0, rows, :] = da * v * (sg * (1.0 + g * (1.0 - sg)))
            o_ref[1, rows, :] = da * (g * sg)

    both = pl.BlockSpec((2, tm, tn), lambda i, j: (0, i, j))
    return pl.pallas_call(
        body, name="ffn_bwd_act", grid=(T // tm, nj),
        in_specs=[pl.BlockSpec((tm, D_MODEL), lambda i, j: (i, 0)), pl.BlockSpec((tn, D_MODEL), lambda i, j: (j, 0)), both],
        out_specs=both, out_shape=jax.ShapeDtypeStruct((2, T, D_FF), F32),
        scratch_shapes=[pltpu.VMEM((tm, tn), F32)],
        compiler_params=_cp(("parallel", "parallel")))(dr2, w_down, u)


def _ffn_bwd_in(du_pre, w_up, dr2, r1, g1, b1, tm=1024, tk=1408):
    T = dr2.shape[0]
    tm = min(tm, T)
    assert T % tm == 0
    nk = 2 * D_FF // tk

    def body(d_ref, w_ref, dr2_ref, r_ref, g_ref, b_ref, dr1_ref, dg_ref, db_ref, acc):
        i, k = pl.program_id(0), pl.program_id(1)

        @pl.when(jnp.logical_and(i == 0, k == 0))
        def _():
            dg_ref[...] = jnp.zeros_like(dg_ref)
            db_ref[...] = jnp.zeros_like(db_ref)

        @pl.when(k == 0)
        def _():
            acc[...] = ALPHA * dr2_ref[...]

        acc[...] += _dot(d_ref[...], w_ref[...], NT)

        @pl.when(k == nk - 1)
        def _():
            _, xhat, rstd = _ln_fwd(r_ref[...], g_ref[...], b_ref[...])
            dy = acc[...]
            dg_ref[...] += jnp.sum(dy * xhat, axis=0, keepdims=True)
            db_ref[...] += jnp.sum(dy, axis=0, keepdims=True)
            dr1_ref[...] = _ln_bwd(dy, xhat, rstd, g_ref[...])

    row = pl.BlockSpec((tm, D_MODEL), lambda i, k: (i, 0))
    vec = _const_spec((1, D_MODEL))
    return pl.pallas_call(
        body, name="ffn_bwd_in", grid=(T // tm, nk),
        in_specs=[pl.BlockSpec((tm, tk), lambda i, k: (i, k)), pl.BlockSpec((D_MODEL, tk), lambda i, k: (0, k)),
                  row, row, vec, vec],
        out_specs=[row, vec, vec],
        out_shape=[jax.ShapeDtypeStruct((T, D_MODEL), F32), jax.ShapeDtypeStruct((1, D_MODEL), F32),
                   jax.ShapeDtypeStruct((1, D_MODEL), F32)],
        scratch_shapes=[pltpu.VMEM((tm, D_MODEL), F32)],
        compiler_params=_cp(("arbitrary", "arbitrary")))(du_pre, w_up, dr2, r1, g1, b1)


def _mix_bwd(dr1, w_mo, w_bs, w_ba, bs, ba, proj_main, b_gate, tm=512):
    T = dr1.shape[0]

    def body(d_ref, wmo_ref, wbs_ref, wba_ref, bs_ref, ba_ref, gs_ref, ga_ref, bg_ref,
             dg_ref, dbs_ref, dba_ref, dys_ref, dya_ref, dbg_ref):
        @pl.when(pl.program_id(0) == 0)
        def _():
            dbg_ref[...] = jnp.zeros_like(dbg_ref)

        dm = _dot(d_ref[...].astype(BF16), wmo_ref[...], NT)
        g_s = _sigmoid(gs_ref[...] + bg_ref[:, 0:D_MODEL])
        g_a = _sigmoid(ga_ref[...] + bg_ref[:, D_MODEL:])
        dgs = dm * bs_ref[...] * g_s * (1.0 - g_s)
        dga = dm * ba_ref[...] * g_a * (1.0 - g_a)
        dg_ref[:, 0:D_MODEL] = dgs.astype(BF16)
        dg_ref[:, D_MODEL:] = dga.astype(BF16)
        dbg_ref[:, 0:D_MODEL] += jnp.sum(dgs, axis=0, keepdims=True)
        dbg_ref[:, D_MODEL:] += jnp.sum(dga, axis=0, keepdims=True)
        dbs = (dm * g_s).astype(BF16)
        dba = (dm * g_a).astype(BF16)
        dbs_ref[...] = dbs
        dba_ref[...] = dba
        dys_ref[...] = _dot(dbs, wbs_ref[...], NT)
        dya_ref[...] = _dot(dba, wba_ref[...], NT).astype(BF16)

    row = lambda w, off=0: pl.BlockSpec((tm, w), lambda i: (i, off))
    return pl.pallas_call(
        body, name="mix_bwd", grid=(T // tm,),
        in_specs=[row(D_MODEL), _const_spec((D_MODEL, D_MODEL)), _const_spec((SSM_INNER, D_MODEL)),
                  _const_spec((D_MODEL, D_MODEL)), row(D_MODEL), row(D_MODEL),
                  row(D_MODEL, GATE_OFF // D_MODEL), row(D_MODEL, GATE_OFF // D_MODEL + 1), _const_spec((1, 2 * D_MODEL))],
        out_specs=[row(2 * D_MODEL, GATE_OFF // (2 * D_MODEL)), row(D_MODEL), row(D_MODEL), row(SSM_INNER), row(D_MODEL),
                   _const_spec((1, 2 * D_MODEL))],
        out_shape=[jax.ShapeDtypeStruct((T, MAIN_COLS), BF16), jax.ShapeDtypeStruct((T, D_MODEL), BF16),
                   jax.ShapeDtypeStruct((T, D_MODEL), BF16), jax.ShapeDtypeStruct((T, SSM_INNER), F32),
                   jax.ShapeDtypeStruct((T, D_MODEL), BF16), jax.ShapeDtypeStruct((1, 2 * D_MODEL), F32)],
        compiler_params=_cp(("arbitrary",)))(dr1, w_mo, w_bs, w_ba, bs, ba, proj_main, proj_main, b_gate)


def _local_step(x, target, w, p, late_weights=None, early_grads=None):
    xb = x.astype(BF16)
    if late_weights is None:
        proj_main = _matmul(xb, w["in_main"], "nn", F32, "in_proj_main", tm=1024, tn=2048)
    else:
        proj_main, *landed = _matmul(xb, w["in_main"], "nn", F32, "in_proj_main", tm=1024, tn=2048,
                                     comm=("gather", late_weights[0]))
        w = {**w, **late_weights[1](landed)}
    proj_tail = _matmul(xb, w["in_tail"], "nn", F32, "in_proj_tail", tn=TAIL_COLS)
    xbc, dsil = _conv_silu_fwd(proj_main, XBC_OFF, XBC_COLS, p["ssm_conv_w"], p["ssm_conv_b"], SSM_CONV, "ssm_conv_fwd")
    y_ssm, ypre, hs = _ssd_fwd(xbc, proj_main, proj_tail, p["dtb_col"], p["alog_col"], p["d_exp"], p["ssm_norm_w"])
    y_attn, lse = _attn_fwd(proj_main, proj_tail, p["bias_tbl"], p["attn_sinks"])
    merged, bs, ba = _merge_fwd(y_ssm, y_attn, proj_main, p["b_gate"], w["bs"], w["ba"])
    r1, h1 = _mix_ln1(merged, w["mo"], x, p["ln1_g"], p["ln1_b"])
    u_pre = _matmul(h1, w["up"], "nn", F32, "ffn_up", tm=1024, tn=1408)
    u, act = _ffn_conv_glu(u_pre, p["ffn_conv_w"], p["ffn_conv_b"])
    dr2, dg2, db2, sq = _down_ln2_loss(act, w["down"], h1, target, p["ln2_g"], p["ln2_b"])
    g = {"ln2_g": dg2, "ln2_b": db2}
    g["w_down"] = _matmul(act, dr2, "tn", F32, "dw_down", tm=1408, tn=1024, tk=1024)
    du = _ffn_bwd_act(dr2, w["down"], u)
    du_pre, g["ffn_conv_w"], g["ffn_conv_b"] = _conv_bwd(
        du, u_pre, 0, 2 * D_FF, p["ffn_conv_w"], FFN_CONV, jax.ShapeDtypeStruct((x.shape[0], 2 * D_FF), BF16), 0,
        "ffn_conv_bwd", tc=256)
    g["w_up"] = _matmul(h1, du_pre, "tn", F32, "dw_up", tm=1024, tn=1408, tk=1024)
    dr1, g["ln1_g"], g["ln1_b"] = _ffn_bwd_in(du_pre, w["up"], dr2, r1, p["ln1_g"], p["ln1_b"])
    g["w_mix_out"] = _matmul(merged, dr1, "tn", F32, "dw_mix_out", tm=1024, tn=1024, tk=2048)
    dmain, dbs, dba, dy_ssm, dy_attn, g["b_gate"] = _mix_bwd(dr1, w["mo"], w["bs"], w["ba"], bs, ba, proj_main, p["b_gate"])
    g["w_branch_ssm"] = _matmul(y_ssm, dbs, "tn", F32, "dw_branch_ssm", tm=1024, tn=1024, tk=2048)
    g["w_branch_attn"] = _matmul(y_attn, dba, "tn", F32, "dw_branch_attn", tm=1024, tn=1024, tk=2048)
    dmain, dtail, dbias, g["attn_sinks"] = _attn_bwd(dy_attn, lse, proj_main, proj_tail, p["bias_tbl"], p["attn_sinks"], dmain)
    g["rel_bias"] = _rel_bias_grad(dbias.reshape(ATTN_HEADS, WINDOW * 2 * WINDOW), p["bucket_onehot"]).T
    dmain, dtail, dco, g["ssm_norm_w"], dd, dalog, ddtb = _ssd_bwd(
        dy_ssm, ypre, xbc, dsil, hs, proj_main, proj_tail, p["dtb_col"], p["alog_col"], p["d_exp"], p["ssm_norm_w"],
        p["ehead_t"], dmain, dtail)
    g["ssm_d"], g["ssm_a_log"], g["ssm_dt_bias"] = (a.reshape(1, SSM_HEADS) for a in (dd, dalog, ddtb))
    dmain, g["ssm_conv_w"], g["ssm_conv_b"] = _conv_bwd(
        dco, proj_main, XBC_OFF, XBC_COLS, p["ssm_conv_w"], SSM_CONV, dmain, XBC_OFF, "ssm_conv_bwd")
    g["in_tail"] = _matmul(xb, dtail, "tn", F32, "dw_in_tail", tm=1024, tn=TAIL_COLS, tk=2048)
    landed = []
    if early_grads is None:
        g["in_main"] = _matmul(xb, dmain, "tn", F32, "dw_in_main", tm=1024, tn=1024, tk=2048)
    else:
        g["in_main"], *landed = _matmul(xb, dmain, "tn", F32, "dw_in_main", tm=1024, tn=1024, tk=2048,
                                        comm=("exchange", early_grads(g)))
    return sq, (dmain, dtail, dr1), w, g, landed


def _grad_x(dproj, w, exchange=None):
    dmain, dtail, dr1 = dproj
    dx = _matmul(dtail, w["in_tail"], "nt", F32, "dx_tail", tk=TAIL_COLS, addend=dr1, addend_scale=ALPHA)
    if exchange is None:
        return _matmul(dmain, w["in_main"], "nt", F32, "dx_main", tm=1024, tk=2048, addend=dx)
    dx, landed = _matmul(dmain, w["in_main"], "nt", F32, "dx_main", tm=1024, tk=2048, addend=dx,
                         comm=("exchange", exchange))
    return dx, landed


def _split_w_in(w):
    seg = lambda off, n: w[:, off:off + n]
    main = jnp.concatenate([seg(O_Z, 2048), seg(O_XBC, XBC_COLS), seg(O_Q, D_MODEL), seg(O_GATE, 2 * D_MODEL)], axis=1)
    tail = jnp.concatenate([seg(O_K, 128), seg(O_V, 128), seg(O_DT, SSM_HEADS),
                            jnp.zeros((w.shape[0], 128 - SSM_HEADS), w.dtype)], axis=1)
    return main, tail


def _join_w_in(main, tail):
    return jnp.concatenate([main[:, Z_OFF:Z_OFF + 2048], main[:, XBC_OFF:XBC_OFF + XBC_COLS],
                            tail[:, DT_OFF:DT_OFF + SSM_HEADS], main[:, Q_OFF:Q_OFF + D_MODEL],
                            tail[:, K_OFF:K_OFF + 128], tail[:, V_OFF:V_OFF + 128],
                            main[:, GATE_OFF:GATE_OFF + 2 * D_MODEL]], axis=1)


def _prep_params(rel_bias, b_gate, ssm_conv_w, ssm_conv_b, ssm_dt_bias, ssm_a_log, ssm_d, ssm_norm_w, attn_sinks,
                 ln1_g, ln1_b, ffn_conv_w, ffn_conv_b, ln2_g, ln2_b):
    bucket, in_window = _band_geometry()
    bucket, in_window = bucket.T, in_window.T
    onehot = jnp.logical_and(bucket.reshape(-1, 1) == jnp.arange(REL_BUCKETS)[None, :],
                             in_window.reshape(-1, 1)).astype(F32)
    onehot_t = jnp.logical_and(bucket.reshape(1, -1) == jnp.arange(REL_BUCKETS)[:, None],
                               in_window.reshape(1, -1)).astype(BF16)
    bias_tbl = _bias_table(rel_bias.T, onehot_t, jnp.where(in_window.reshape(1, -1), 0.0, NEG))
    ehead_t = (jnp.arange(SSM_INNER)[None, :] // SSMD == jnp.arange(SSM_HEADS)[:, None]).astype(F32)
    return {"bias_tbl": bias_tbl, "bucket_onehot": onehot, "b_gate": b_gate, "ssm_conv_w": ssm_conv_w,
            "ssm_conv_b": ssm_conv_b, "dtb_col": ssm_dt_bias.reshape(SSM_HEADS, 1),
            "alog_col": ssm_a_log.reshape(SSM_HEADS, 1), "ehead_t": ehead_t,
            "d_exp": jnp.repeat(ssm_d, SSMD, axis=1), "ssm_norm_w": ssm_norm_w, "attn_sinks": attn_sinks,
            "ln1_g": ln1_g, "ln1_b": ln1_b, "ffn_conv_w": ffn_conv_w, "ffn_conv_b": ffn_conv_b,
            "ln2_g": ln2_g, "ln2_b": ln2_b}


def _all_gather(shards, name):
    nb = len(shards)

    def body(*refs):
        for phase in _gather_phases(refs[:nb], refs[nb:2 * nb], *refs[2 * nb:]):
            phase()

    any_spec = pl.BlockSpec(memory_space=pl.ANY)
    return pl.pallas_call(
        body, name=name, out_shape=[jax.ShapeDtypeStruct((N_DEV,) + s.shape, s.dtype) for s in shards],
        in_specs=[any_spec] * nb, out_specs=[any_spec] * nb, scratch_shapes=_comm_sems(nb))(*shards)


def _adamw_math(w, g, m, v):
    m = ADAM_B1 * m + (1.0 - ADAM_B1) * g
    v = ADAM_B2 * v + (1.0 - ADAM_B2) * (g * g)
    m_hat = m / (1.0 - ADAM_B1 ** ADAM_STEP)
    v_hat = v / (1.0 - ADAM_B2 ** ADAM_STEP)
    return -ADAM_LR * (m_hat / (jnp.sqrt(v_hat) + ADAM_EPS) + ADAM_WD * w), m, v


def _slot_total(s_ref):
    g = s_ref[0].astype(F32)
    for i in range(1, N_DEV):
        g = g + s_ref[i].astype(F32)
    return g


def _adamw(landed, w, m, v, name):
    R, C = w.shape
    tr = 256 if R % 256 == 0 and R > 256 else R

    def body(s_ref, w_ref, m_ref, v_ref, g_ref, d_ref, nm_ref, nv_ref):
        g = _slot_total(s_ref)
        g_ref[...] = g
        d_ref[...], nm_ref[...], nv_ref[...] = _adamw_math(w_ref[...], g, m_ref[...], v_ref[...])

    spec = pl.BlockSpec((tr, C), lambda i: (i, 0))
    return pl.pallas_call(
        body, name=name, grid=(R // tr,), in_specs=[pl.BlockSpec((N_DEV, tr, C), lambda i: (0, i, 0))] + [spec] * 3,
        out_specs=[spec] * 4, out_shape=[jax.ShapeDtypeStruct((R, C), F32)] * 4,
        compiler_params=_cp(("parallel",)))(landed, w, m, v)


def _small_update(landed, ws, ms, vs):
    k = len(ws)

    def body(*refs):
        s_ref, w_refs, m_refs, v_refs = refs[0], refs[1:1 + k], refs[1 + k:1 + 2 * k], refs[1 + 2 * k:1 + 3 * k]
        outs = refs[1 + 3 * k:]
        g_all = _slot_total(s_ref)
        for i in range(k):
            n = w_refs[i].shape[1]
            g = g_all[i:i + 1, 0:n]
            outs[i][...] = g
            outs[k + i][...], outs[2 * k + i][...], outs[3 * k + i][...] = _adamw_math(
                w_refs[i][...], g, m_refs[i][...], v_refs[i][...])

    return pl.pallas_call(body, name="small_update",
                          out_shape=[jax.ShapeDtypeStruct(w.shape, F32) for w in ws] * 4)(landed, *ws, *ms, *vs)


SHARDED = {"w_in": "cols", "w_branch_ssm": "rows", "w_branch_attn": "rows", "w_mix_out": "rows", "w_up": "cols",
           "w_down": "rows", "ssm_conv_w": "cols", "ffn_conv_w": "cols"}
LATE = ("w_branch_ssm", "w_branch_attn", "w_mix_out", "w_up", "w_down")
SHORT = {"w_branch_ssm": "bs", "w_branch_attn": "ba", "w_mix_out": "mo", "w_up": "up", "w_down": "down"}
SMALL = ("rel_bias", "b_gate", "ssm_conv_b", "ssm_dt_bias", "ssm_a_log", "ssm_d", "ssm_norm_w", "attn_sinks",
         "ln1_g", "ln1_b", "ffn_conv_b", "ln2_g", "ln2_b")
WEIGHTS = ("rel_bias", "w_in", "b_gate", "ssm_conv_w", "ssm_conv_b", "ssm_dt_bias", "ssm_a_log", "ssm_d", "ssm_norm_w",
           "attn_sinks", "w_branch_ssm", "w_branch_attn", "w_mix_out", "ln1_g", "ln1_b", "w_up", "ffn_conv_w",
           "ffn_conv_b", "w_down", "ln2_g", "ln2_b")
SMALL_ROWS, SMALL_COLS = 16, 2 * D_FF


def _by_device(full, how):
    r, c = full.shape
    if how == "rows":
        return full.reshape(N_DEV, r // N_DEV, c)
    return full.reshape(r, N_DEV, c // N_DEV).transpose(1, 0, 2)


def _from_devices(slots, how):
    _, r, c = slots.shape
    if how == "rows":
        return slots.reshape(N_DEV * r, c)
    return slots.transpose(1, 0, 2).reshape(r, N_DEV * c)


def kernel(x, rel_bias, w_in, b_gate, ssm_conv_w, ssm_conv_b, ssm_dt_bias, ssm_a_log, ssm_d, ssm_norm_w, attn_sinks, w_branch_ssm, w_branch_attn, w_mix_out, ln1_g, ln1_b, w_up, ffn_conv_w, ffn_conv_b, w_down, ln2_g, ln2_b, loss_target, m_rel_bias, m_w_in, m_b_gate, m_ssm_conv_w, m_ssm_conv_b, m_ssm_dt_bias, m_ssm_a_log, m_ssm_d, m_ssm_norm_w, m_attn_sinks, m_w_branch_ssm, m_w_branch_attn, m_w_mix_out, m_ln1_g, m_ln1_b, m_w_up, m_ffn_conv_w, m_ffn_conv_b, m_w_down, m_ln2_g, m_ln2_b, v_rel_bias, v_w_in, v_b_gate, v_ssm_conv_w, v_ssm_conv_b, v_ssm_dt_bias, v_ssm_a_log, v_ssm_d, v_ssm_norm_w, v_attn_sinks, v_w_branch_ssm, v_w_branch_attn, v_w_mix_out, v_ln1_g, v_ln1_b, v_w_up, v_ffn_conv_w, v_ffn_conv_b, v_w_down, v_ln2_g, v_ln2_b):
    W = dict(zip(WEIGHTS, (rel_bias, w_in, b_gate, ssm_conv_w, ssm_conv_b, ssm_dt_bias, ssm_a_log, ssm_d, ssm_norm_w,
                           attn_sinks, w_branch_ssm, w_branch_attn, w_mix_out, ln1_g, ln1_b, w_up, ffn_conv_w,
                           ffn_conv_b, w_down, ln2_g, ln2_b)))
    M = dict(zip(WEIGHTS, (m_rel_bias, m_w_in, m_b_gate, m_ssm_conv_w, m_ssm_conv_b, m_ssm_dt_bias, m_ssm_a_log, m_ssm_d,
                           m_ssm_norm_w, m_attn_sinks, m_w_branch_ssm, m_w_branch_attn, m_w_mix_out, m_ln1_g, m_ln1_b,
                           m_w_up, m_ffn_conv_w, m_ffn_conv_b, m_w_down, m_ln2_g, m_ln2_b)))
    V = dict(zip(WEIGHTS, (v_rel_bias, v_w_in, v_b_gate, v_ssm_conv_w, v_ssm_conv_b, v_ssm_dt_bias, v_ssm_a_log, v_ssm_d,
                           v_ssm_norm_w, v_attn_sinks, v_w_branch_ssm, v_w_branch_attn, v_w_mix_out, v_ln1_g, v_ln1_b,
                           v_w_up, v_ffn_conv_w, v_ffn_conv_b, v_w_down, v_ln2_g, v_ln2_b)))
    shard2d = lambda a: a.reshape(a.shape[-2], a.shape[-1])

    (win_all,) = _all_gather([shard2d(w_in).astype(BF16)], "gather_w_in")
    main, tail = _split_w_in(_from_devices(win_all, "cols"))
    conv_all = _all_gather([shard2d(ssm_conv_w), shard2d(ffn_conv_w)], "gather_conv_weights")
    late_shards = [shard2d(W[n]).astype(BF16) for n in LATE]
    late = lambda landed: {SHORT[n]: _from_devices(a, SHARDED[n]) for n, a in zip(LATE, landed)}
    p = _prep_params(rel_bias, b_gate, _from_devices(conv_all[0], "cols"), ssm_conv_b, ssm_dt_bias, ssm_a_log, ssm_d,
                     ssm_norm_w, attn_sinks, ln1_g, ln1_b, _from_devices(conv_all[1], "cols"), ffn_conv_b, ln2_g, ln2_b)

    early_names = LATE + ("ssm_conv_w", "ffn_conv_w")
    early = lambda g: [_by_device(g[n], SHARDED[n]).astype(BF16 if n in LATE else F32) for n in early_names]
    sq, dproj, w, g, landed = _local_step(x[0], loss_target[0], {"in_main": main, "in_tail": tail}, p,
                                          (late_shards, late), early)
    landed = dict(zip(early_names, landed))
    g_w_in = _join_w_in(g.pop("in_main"), g.pop("in_tail"))
    dx, landed["w_in"] = _grad_x(dproj, w, exchange=[_by_device(g_w_in, "cols").astype(BF16)])
    loss = (0.5 / D_MODEL) * lax.psum(sq[0, 0], ("x", "y", "c"))
    grads, deltas, new_m, new_v = {}, {}, {}, {}
    for n in SHARDED:
        outs = _adamw(landed[n], shard2d(W[n]), shard2d(M[n]), shard2d(V[n]), "adamw_" + n)
        grads[n], deltas[n], new_m[n], new_v[n] = (a.reshape(W[n].shape) for a in outs)

    row = lambda a: a.reshape(1, -1)
    packed = jnp.concatenate([jnp.pad(row(g[n]), ((0, 0), (0, SMALL_COLS - g[n].size))) for n in SMALL]
                             + [jnp.zeros((SMALL_ROWS - len(SMALL), SMALL_COLS), F32)], axis=0)
    (small_all,) = _all_gather([packed], "gather_small_grads")
    outs = _small_update(small_all, *[[row(src[n]) for n in SMALL] for src in (W, M, V)])
    for i, n in enumerate(SMALL):
        grads[n], deltas[n], new_m[n], new_v[n] = (outs[j * len(SMALL) + i].reshape(W[n].shape) for j in range(4))

    return (loss, dx[None], *[grads[n] for n in WEIGHTS], *[deltas[n] for n in WEIGHTS],
            *[new_m[n] for n in WEIGHTS], *[new_v[n] for n in WEIGHTS])
```

```python
import functools
import math

import jax
import jax.numpy as jnp
from jax import lax
from jax.experimental import pallas as pl
from jax.experimental.pallas import tpu as pltpu

F32, BF16 = jnp.float32, jnp.bfloat16
HIGHEST = lax.Precision.HIGHEST
MESH_ID = pl.DeviceIdType.MESH

N_DEV = 8
D_MODEL = 1024
SSM_INNER = 2048
SSM_HEADS = 32
SSM_HEADDIM = 64
SSMD = SSM_HEADDIM
SSM_GROUPS = 4
SSM_GROUP_COLS = SSM_INNER // SSM_GROUPS
SSM_STATE = 128
SSM_CONV = 4
CHUNK = 128
XBC_COLS = SSM_INNER + 2 * SSM_GROUPS * SSM_STATE
B_OFF = SSM_INNER
C_OFF = SSM_INNER + SSM_GROUPS * SSM_STATE
ATTN_HEADS = 16
ATTN_KV = 2
ATTN_GROUP = 8
HEADDIM = 64
WINDOW = 128
REL_BUCKETS = 32
REL_MAX_DIST = 128
D_FF = 2816
FFN_CONV = 3
ALPHA = 2.0 ** 0.25
LN_EPS = 1e-5
RMS_EPS = 1e-5
IN_COLS = 8480
Z_OFF, XBC_OFF, Q_OFF, GATE_OFF, MAIN_COLS = 0, 2048, 5120, 6144, 8192
K_OFF, V_OFF, DT_OFF, TAIL_COLS = 0, 128, 256, 384
O_Z, O_XBC, O_DT, O_Q, O_K, O_V, O_GATE = 0, 2048, 5120, 5152, 6176, 6304, 6432

ADAM_LR, ADAM_B1, ADAM_B2, ADAM_EPS, ADAM_WD, ADAM_STEP = 0.001, 0.9, 0.999, 1e-08, 0.01, 10
NEG = -1e30
HALO = 8
VMEM_LIMIT = 56 * 1024 * 1024


def _cp(sem):
    return pltpu.CompilerParams(dimension_semantics=sem, vmem_limit_bytes=VMEM_LIMIT)


def _const_spec(shape):
    nd = len(shape)
    return pl.BlockSpec(shape, lambda *_: (0,) * nd)


def _sigmoid(x):
    return 0.5 * jnp.tanh(0.5 * x) + 0.5


def _softplus(x):
    return jnp.maximum(x, 0.0) + jnp.log1p(jnp.exp(-jnp.abs(x)))


def _dot(a, b, dims=(((1,), (0,)), ((), ())), precision=None):
    return lax.dot_general(a, b, dims, preferred_element_type=F32, precision=precision)


NN = (((1,), (0,)), ((), ()))
NT = (((1,), (1,)), ((), ()))
TN = (((0,), (0,)), ((), ()))


def _mesh_pos():
    return lax.axis_index("x"), lax.axis_index("y"), lax.axis_index("c")


PEERS = N_DEV - 1


def _exchange_phases(in_refs, out_refs, send_sems, recv_sems, local_sems):
    def copies():
        x, y, c = _mesh_pos()
        me = 4 * x + 2 * y + c
        cps = []
        for b, (in_ref, out_ref) in enumerate(zip(in_refs, out_refs)):
            cps.append(pltpu.make_async_copy(in_ref.at[me], out_ref.at[me], local_sems.at[b]))
            for r in range(1, N_DEV):
                px = 1 - x if r & 4 else x
                py = 1 - y if r & 2 else y
                pc = 1 - c if r & 1 else c
                cps.append(pltpu.make_async_remote_copy(
                    src_ref=in_ref.at[4 * px + 2 * py + pc], dst_ref=out_ref.at[me],
                    send_sem=send_sems.at[b * PEERS + r - 1], recv_sem=recv_sems.at[b * PEERS + r - 1],
                    device_id=(px, py, pc), device_id_type=MESH_ID))
        return cps

    def start():
        for cp in copies():
            cp.start()

    def finish():
        for cp in copies():
            cp.wait()

    return [start, finish]


def _gather_phases(x_refs, out_refs, send_sems, recv_sems, local_sems):
    def parts(which):
        x, y, c = _mesh_pos()
        me, sibling = (x, y, c), (x, y, 1 - c)
        chips = [(1 - x, y), (x, 1 - y), (1 - x, 1 - y)]
        found = []
        for b, (x_ref, out_ref) in enumerate(zip(x_refs, out_refs)):
            def slot(px, py, pc):
                return out_ref.at[4 * px + 2 * py + pc]

            def copy(k, block, to, src=None):
                return pltpu.make_async_remote_copy(
                    src_ref=slot(*block) if src is None else src, dst_ref=slot(*block),
                    send_sem=send_sems.at[b * PEERS + k], recv_sem=recv_sems.at[b * PEERS + k],
                    device_id=to, device_id_type=MESH_ID)

            if which == "mine":
                found.append(pltpu.make_async_copy(x_ref, slot(*me), local_sems.at[b]))
            elif which == "first":
                found.append(copy(0, me, sibling, src=x_ref))
                found += [copy(1 + j, me, (*chip, c), src=x_ref) for j, chip in enumerate(chips)]
            elif which == "passed":
                found += [copy(4 + j, (*chip, c), sibling) for j, chip in enumerate(chips)]
            elif which == "arrived":
                found += [copy(1 + j, (*chip, c), me) for j, chip in enumerate(chips)]
            else:
                found.append(copy(0, sibling, me))
                found += [copy(4 + j, (*chip, 1 - c), me) for j, chip in enumerate(chips)]
        return found

    def start():
        for cp in parts("mine") + parts("first"):
            cp.start()

    def forward():
        for a, p in zip(parts("arrived"), parts("passed")):
            a.wait_recv()
            p.start()

    def finish():
        for cp in parts("late"):
            cp.wait_recv()
        for cp in parts("first") + parts("passed"):
            cp.wait_send()
        for cp in parts("mine"):
            cp.wait()

    return [start, forward, finish]


COMM = {"exchange": _exchange_phases, "gather": _gather_phases}


def _comm_sems(nb):
    return [pltpu.SemaphoreType.DMA((nb * PEERS,)), pltpu.SemaphoreType.DMA((nb * PEERS,)), pltpu.SemaphoreType.DMA((nb,))]


def _matmul(a, b, mode, out_dtype, name, tm=512, tn=1024, tk=1024, addend=None, addend_scale=1.0, comm=None):
    bufs = [] if comm is None else list(comm[1])
    nb = len(bufs)
    if mode == "nn":
        (M, K), (K2, N) = a.shape, b.shape
    elif mode == "nt":
        (M, K), (N, K2) = a.shape, b.shape
    else:
        (K, M), (K2, N) = a.shape, b.shape
    assert K == K2, (a.shape, b.shape, mode)
    tm, tn, tk = min(tm, M), min(tn, N), min(tk, K)
    assert M % tm == 0 and N % tn == 0 and K % tk == 0, (M, N, K, tm, tn, tk)
    nk = K // tk
    dims = {"nn": NN, "nt": NT, "tn": TN}[mode]
    a_spec = pl.BlockSpec((tk, tm), lambda i, j, k: (k, i)) if mode == "tn" else pl.BlockSpec((tm, tk), lambda i, j, k: (i, k))
    b_spec = pl.BlockSpec((tn, tk), lambda i, j, k: (j, k)) if mode == "nt" else pl.BlockSpec((tk, tn), lambda i, j, k: (k, j))
    o_spec = pl.BlockSpec((tm, tn), lambda i, j, k: (i, j))

    ni, nj = M // tm, N // tn

    def body(*refs):
        refs = list(refs)
        a_ref, b_ref = refs[:2]
        c_ref = refs[2] if addend is not None else None
        n_in = 2 + (addend is not None) + nb
        o_ref, acc = refs[n_in], refs[n_in + 1 + nb]
        i, j, k = pl.program_id(0), pl.program_id(1), pl.program_id(2)
        step = (i * nj + j) * nk + k
        if comm is not None:
            phases = COMM[comm[0]](refs[n_in - nb:n_in], refs[n_in + 1:n_in + 1 + nb], *refs[n_in + 2 + nb:])
            at = [(ni * nj * nk - 1) * p // (len(phases) - 1) for p in range(len(phases))]
            for when, phase in zip(at[:-1], phases[:-1]):
                pl.when(step == when)(phase)

        d = _dot(a_ref[...].astype(BF16), b_ref[...].astype(BF16), dims)

        def finish(r):
            if addend is not None:
                r = r + addend_scale * c_ref[...].astype(F32)
            o_ref[...] = r.astype(out_dtype)

        if nk == 1:
            finish(d)
        else:
            @pl.when(k == 0)
            def _():
                acc[...] = d

            @pl.when(jnp.logical_and(k > 0, k < nk - 1))
            def _():
                acc[...] += d

            @pl.when(k == nk - 1)
            def _():
                finish(acc[...] + d)

        if comm is not None:
            pl.when(step == at[-1])(phases[-1])

    in_specs = [a_spec, b_spec] + ([o_spec] if addend is not None else [])
    args = (a, b) + ((addend,) if addend is not None else ())
    out_specs, out_shape = o_spec, jax.ShapeDtypeStruct((M, N), out_dtype)
    scratch = [pltpu.VMEM((tm, tn), F32)]
    sem = ("parallel", "parallel", "arbitrary")
    if comm is not None:
        any_spec = pl.BlockSpec(memory_space=pl.ANY)
        in_specs, args = in_specs + [any_spec] * nb, args + tuple(bufs)
        landed = [x.shape if comm[0] == "exchange" else (N_DEV,) + x.shape for x in bufs]
        out_specs = [o_spec] + [any_spec] * nb
        out_shape = [out_shape] + [jax.ShapeDtypeStruct(s, x.dtype) for s, x in zip(landed, bufs)]
        scratch += _comm_sems(nb)
        sem = ("arbitrary", "arbitrary", "arbitrary")
    return pl.pallas_call(
        body, name=name, grid=(ni, nj, nk), in_specs=in_specs, out_specs=out_specs, out_shape=out_shape,
        scratch_shapes=scratch, compiler_params=_cp(sem))(*args)


def _taps(w_ref, K, tc):
    return [jnp.broadcast_to(w_ref[k:k + 1, :], (HALO, tc)) for k in range(K)]


def _conv_silu_fwd(pre, pre_col_off, C, w, b, K, name, tr=1024, tc=512):
    T = pre.shape[0]
    tr, tc = min(tr, T), min(tc, C)
    assert T % tr == 0 and C % tc == 0 and pre_col_off % tc == 0
    joff = pre_col_off // tc
    hb = tr // HALO

    def body(x_ref, xp_ref, w_ref, b_ref, o_ref, d_ref, head):
        i = pl.program_id(1)
        head[0:HALO, :] = jnp.where(i > 0, xp_ref[...], 0.0)
        head[HALO:, :] = x_ref[0:HALO, :]
        wk = _taps(w_ref, K, tc)
        bias = jnp.broadcast_to(b_ref[...], (HALO, tc))
        for r in range(tr // HALO):
            lo = r * HALO
            co = bias + wk[K - 1] * x_ref[lo:lo + HALO, :]
            for k in range(K - 1):
                s = K - 1 - k
                co = co + wk[k] * (head[HALO - s:2 * HALO - s, :] if r == 0 else x_ref[lo - s:lo + HALO - s, :])
            sg = _sigmoid(co)
            y = co * sg
            o_ref[lo:lo + HALO, :] = y
            d_ref[lo:lo + HALO, :] = sg + y * (1.0 - sg)

    out = pl.BlockSpec((tr, tc), lambda j, i: (i, j))
    return pl.pallas_call(
        body, name=name, grid=(C // tc, T // tr),
        in_specs=[pl.BlockSpec((tr, tc), lambda j, i: (i, joff + j)),
                  pl.BlockSpec((HALO, tc), lambda j, i: (jnp.maximum(i * hb - 1, 0), joff + j)),
                  pl.BlockSpec((K, tc), lambda j, i: (0, j)),
                  pl.BlockSpec((1, tc), lambda j, i: (0, j))],
        out_specs=[out, out], out_shape=[jax.ShapeDtypeStruct((T, C), F32)] * 2,
        scratch_shapes=[pltpu.VMEM((2 * HALO, tc), F32)],
        compiler_params=_cp(("parallel", "arbitrary")))(pre, pre, w, b)


def _conv_bwd(dout, pre, pre_col_off, C, w, K, dst, dst_col_off, name, tr=1024, tc=512):
    T = pre.shape[0]
    tr, tc = min(tr, T), min(tc, C)
    assert T % tr == 0 and C % tc == 0 and pre_col_off % tc == 0 and dst_col_off % tc == 0
    joff, doff = pre_col_off // tc, dst_col_off // tc
    hb = tr // HALO
    nt = T // tr
    n = tr // HALO
    last_hblock = T // HALO - 1

    def body(g_ref, gn_ref, x_ref, w_ref, *rest):
        o_ref, dw_ref, db_ref, edge = rest[-4:]
        i = pl.program_id(1)
        wk = _taps(w_ref, K, tc)
        edge[0:HALO, :] = g_ref[tr - HALO:tr, :]
        edge[HALO:, :] = jnp.where(i < nt - 1, gn_ref[...], 0.0)
        acc_w = [jnp.zeros((HALO, tc), F32) for _ in range(K)]
        acc_b = jnp.zeros((HALO, tc), F32)
        for r in range(n):
            lo = r * HALO
            x = x_ref[lo:lo + HALO, :]
            dpre = None
            for s in range(K):
                gs = edge[s:HALO + s, :] if (r == n - 1 and s > 0) else g_ref[lo + s:lo + HALO + s, :]
                dpre = wk[K - 1 - s] * gs if dpre is None else dpre + wk[K - 1 - s] * gs
                acc_w[K - 1 - s] = acc_w[K - 1 - s] + gs * x
                if s == 0:
                    acc_b = acc_b + gs
            o_ref[lo:lo + HALO, :] = dpre.astype(o_ref.dtype)

        @pl.when(i == 0)
        def _():
            dw_ref[...] = jnp.zeros_like(dw_ref)
            db_ref[...] = jnp.zeros_like(db_ref)

        db_ref[...] += jnp.sum(acc_b, axis=0, keepdims=True)
        dw_ref[...] += jnp.concatenate([jnp.sum(a, axis=0, keepdims=True) for a in acc_w], axis=0)

    tile = lambda off: pl.BlockSpec((tr, tc), lambda j, i: (i, off + j))
    nxt_row = lambda i: jnp.minimum((i + 1) * hb, last_hblock)
    if dout.ndim == 3:
        nh = C // 2 // tc
        g_specs = [pl.BlockSpec((None, tr, tc), lambda j, i: (j // nh, i, j % nh)),
                   pl.BlockSpec((None, HALO, tc), lambda j, i: (j // nh, nxt_row(i), j % nh))]
    else:
        g_specs = [tile(0), pl.BlockSpec((HALO, tc), lambda j, i: (nxt_row(i), j))]
    in_specs = g_specs + [tile(joff), pl.BlockSpec((K, tc), lambda j, i: (0, j))]
    args = (dout, dout, pre, w)
    if isinstance(dst, jax.ShapeDtypeStruct):
        aliases = {}
    else:
        in_specs.append(pl.BlockSpec(memory_space=pl.ANY))
        args += (dst,)
        aliases = {4: 0}
    return pl.pallas_call(
        body, name=name, grid=(C // tc, nt), in_specs=in_specs,
        out_specs=[tile(doff), pl.BlockSpec((K, tc), lambda j, i: (0, j)), pl.BlockSpec((1, tc), lambda j, i: (0, j))],
        out_shape=[jax.ShapeDtypeStruct(dst.shape, dst.dtype), jax.ShapeDtypeStruct((K, C), F32),
                   jax.ShapeDtypeStruct((1, C), F32)],
        scratch_shapes=[pltpu.VMEM((2 * HALO, tc), F32)],
        input_output_aliases=aliases,
        compiler_params=_cp(("parallel", "arbitrary")))(*args)


PAIR = 2 * SSMD
PAIRS_PER_GROUP = SSM_GROUP_COLS // PAIR


def _dot3(x, onehot):
    h1 = x.astype(BF16)
    r = x - h1.astype(F32)
    h2 = r.astype(BF16)
    h3 = (r - h2.astype(F32)).astype(BF16)
    return _dot(h1, onehot) + _dot(h2, onehot) + _dot(h3, onehot)


def _chunk_rows(dt_raw, dtb_col, alog_col):
    row = lax.broadcasted_iota(jnp.int32, (CHUNK, CHUNK), 0)
    col = lax.broadcasted_iota(jnp.int32, (CHUNK, CHUNK), 1)
    dt_rawT = dt_raw.T
    dtT = _softplus(dt_rawT + dtb_col)
    a_col = -jnp.exp(alog_col)
    acsT = _dot3(dtT * a_col, (row <= col).astype(BF16))
    return dt_rawT, dtT, a_col, acsT, row, col


def _block_diag(x, left):
    return jnp.concatenate([jnp.where(left, x, 0.0), jnp.where(left, 0.0, x)], axis=0).astype(BF16)


def _lane_bcast(v, h):
    return jnp.broadcast_to(v[:, h:h + 1], (CHUNK, CHUNK))


def _ssd_fwd(xbc, proj_main, proj_tail, dtb_col, alog_col, d_exp, norm_w):
    T = xbc.shape[0]
    nc = T // CHUNK

    def body(xbc_ref, dt_ref, z_ref, dtb_ref, alog_ref, d_ref, nw_ref, y_ref, ypre_ref, hs_ref, H):
        c = pl.program_id(0)

        @pl.when(c == 0)
        def _():
            H[...] = jnp.zeros_like(H)

        hs_ref[0] = H[...]
        _, dtT, _, acsT, row, col = _chunk_rows(dt_ref[:, 0:SSM_HEADS], dtb_ref[...], alog_ref[...])
        tril, left = row >= col, col < SSMD
        acs = acsT.T
        w = (dtT * jnp.exp(acsT[:, CHUNK - 1:CHUNK] - acsT)).T
        cd = jnp.exp(acs[CHUNK - 1:CHUNK, :])
        for g in range(SSM_GROUPS):
            gs = slice(g * SSM_GROUP_COLS, (g + 1) * SSM_GROUP_COLS)
            Bb = xbc_ref[:, B_OFF + g * SSM_STATE:B_OFF + (g + 1) * SSM_STATE].astype(BF16)
            Cb = xbc_ref[:, C_OFF + g * SSM_STATE:C_OFF + (g + 1) * SSM_STATE].astype(BF16)
            Hg = H[:, gs]
            CH = _dot(Cb, Hg.astype(BF16))
            CB = _dot(Cb, Bb, NT)
            ys, xws = [], []
            for kk in range(PAIRS_PER_GROUP):
                k = g * PAIRS_PER_GROUP + kk
                xs_p = xbc_ref[:, k * PAIR:(k + 1) * PAIR]
                mps, ecols, wcols = [], [], []
                for j in range(2):
                    h = 2 * k + j
                    colb = _lane_bcast(acs, h)
                    L = jnp.exp(jnp.where(tril, colb - acsT[h:h + 1, :], -jnp.inf))
                    mps.append((CB * L * dtT[h:h + 1, :]).astype(BF16))
                    ecols.append(jnp.exp(colb))
                    wcols.append(_lane_bcast(w, h))
                yd = _dot(jnp.concatenate(mps, axis=1), _block_diag(xs_p, left))
                ys.append(yd + CH[:, kk * PAIR:(kk + 1) * PAIR] * jnp.where(left, ecols[0], ecols[1]))
                xws.append((xs_p * jnp.where(left, wcols[0], wcols[1])).astype(BF16))
            cd_e = jnp.concatenate([jnp.broadcast_to(cd[:, g * 8 + e:g * 8 + e + 1], (1, SSMD)) for e in range(8)], axis=1)
            H[:, gs] = Hg * cd_e + _dot(Bb, jnp.concatenate(xws, axis=1), TN)
            ypre = jnp.concatenate(ys, axis=1) + xbc_ref[:, gs] * d_ref[:, gs]
            ypre_ref[:, gs] = ypre
            z = z_ref[:, gs]
            yg = ypre * (z * _sigmoid(z))
            r = lax.rsqrt(jnp.mean(yg * yg, axis=1, keepdims=True) + RMS_EPS)
            y_ref[:, gs] = (yg * r * nw_ref[:, gs]).astype(BF16)

    vec = lambda n: _const_spec((1, n))
    colv = _const_spec((SSM_HEADS, 1))
    return pl.pallas_call(
        body, name="ssd_fwd", grid=(nc,),
        in_specs=[pl.BlockSpec((CHUNK, XBC_COLS), lambda c: (c, 0)),
                  pl.BlockSpec((CHUNK, 128), lambda c: (c, DT_OFF // 128)),
                  pl.BlockSpec((CHUNK, SSM_INNER), lambda c: (c, Z_OFF // SSM_INNER)),
                  colv, colv, vec(SSM_INNER), vec(SSM_INNER)],
        out_specs=[pl.BlockSpec((CHUNK, SSM_INNER), lambda c: (c, 0)),
                   pl.BlockSpec((CHUNK, SSM_INNER), lambda c: (c, 0)),
                   pl.BlockSpec((1, SSM_STATE, SSM_INNER), lambda c: (c, 0, 0))],
        out_shape=[jax.ShapeDtypeStruct((T, SSM_INNER), BF16), jax.ShapeDtypeStruct((T, SSM_INNER), F32),
                   jax.ShapeDtypeStruct((nc, SSM_STATE, SSM_INNER), F32)],
        scratch_shapes=[pltpu.VMEM((SSM_STATE, SSM_INNER), F32)],
        compiler_params=_cp(("arbitrary",)))(xbc, proj_tail, proj_main, dtb_col, alog_col, d_exp, norm_w)


def _ssd_bwd(dyo, ypre, xbc, dsil, hs, proj_main, proj_tail, dtb_col, alog_col, d_exp, norm_w, ehead_t, dmain, dtail):
    T = xbc.shape[0]
    nc = T // CHUNK

    def body(dyo_ref, ypre_ref, xbc_ref, dsil_ref, hs_ref, dt_ref, z_ref, dtb_ref, alog_ref, d_ref, nw_ref, eh_ref,
             dmain_in, dtail_in, dz_ref, ddt_ref, dxbc_ref, dnw_ref, dd_ref, dalog_ref, ddtb_ref, G):
        del dmain_in, dtail_in
        c = pl.program_id(0)

        @pl.when(c == 0)
        def _():
            G[...] = jnp.zeros_like(G)
            dnw_ref[...] = jnp.zeros_like(dnw_ref)
            dd_ref[...] = jnp.zeros_like(dd_ref)
            dalog_ref[...] = jnp.zeros_like(dalog_ref)
            ddtb_ref[...] = jnp.zeros_like(ddtb_ref)

        dt_rawT, dtT, a_col, acsT, row, col = _chunk_rows(dt_ref[:, 0:SSM_HEADS], dtb_ref[...], alog_ref[...])
        tril, triu, left = row >= col, col >= row, col < SSMD
        acs = acsT.T
        dt = dtT.T
        lastT = acsT[:, CHUNK - 1:CHUNK]
        dstT = jnp.exp(lastT - acsT)
        wT = dtT * dstT
        cd = jnp.exp(acs[CHUNK - 1:CHUNK, :])
        ddt_rows, rs_rows, deo_rows, dw_rows = [], [], [], []
        dd_cols, gh_cols, dnw_cols = [], [], []
        for g in range(SSM_GROUPS):
            gs = slice(g * SSM_GROUP_COLS, (g + 1) * SSM_GROUP_COLS)
            z = z_ref[:, gs]
            sz = _sigmoid(z)
            silu_z = z * sz
            ypre = ypre_ref[:, gs]
            yg = ypre * silu_z
            r = lax.rsqrt(jnp.mean(yg * yg, axis=1, keepdims=True) + RMS_EPS)
            ygn = yg * r
            dyo = dyo_ref[:, gs]
            dyn = dyo * nw_ref[:, gs]
            dnw_cols.append(jnp.sum(dyo * ygn, axis=0, keepdims=True))
            dyg = r * (dyn - ygn * jnp.mean(dyn * ygn, axis=1, keepdims=True))
            dz_ref[:, gs] = (dyg * ypre * (sz * (1.0 + z * (1.0 - sz)))).astype(dz_ref.dtype)
            dY = dyg * silu_z
            xs = xbc_ref[:, gs]
            dd_cols.append(jnp.sum(dY * xs, axis=0, keepdims=True))
            Bf = xbc_ref[:, B_OFF + g * SSM_STATE:B_OFF + (g + 1) * SSM_STATE]
            Cf = xbc_ref[:, C_OFF + g * SSM_STATE:C_OFF + (g + 1) * SSM_STATE]
            Bb, Cb = Bf.astype(BF16), Cf.astype(BF16)
            BT, CT = Bf.T, Cf.T
            CB = _dot(Cb, Bb, NT)
            CBT = _dot(Bb, Cb, NT)
            Hg = hs_ref[0, :, gs]
            Gg = G[:, gs]
            gh_cols.append(jnp.sum(Gg * Hg, axis=0, keepdims=True))
            dCB = jnp.zeros((CHUNK, CHUNK), F32)
            dxs_d, dyes, xws, wsels = [], [], [], []
            for kk in range(PAIRS_PER_GROUP):
                k = g * PAIRS_PER_GROUP + kk
                ps = slice(kk * PAIR, (kk + 1) * PAIR)
                xs_p, dY_p = xs[:, ps], dY[:, ps]
                Ls, LTs, dtcols, ecols, wcols = [], [], [], [], []
                for j in range(2):
                    h = 2 * k + j
                    colb = _lane_bcast(acs, h)
                    seg = colb - acsT[h:h + 1, :]
                    Ls.append(jnp.exp(jnp.where(tril, seg, -jnp.inf)))
                    LTs.append(jnp.exp(jnp.where(triu, -seg, -jnp.inf)))
                    dtcol = _lane_bcast(dt, h)
                    dtcols.append(dtcol)
                    ecols.append(jnp.exp(colb))
                    wcols.append(dtcol * jnp.exp(acs[CHUNK - 1:CHUNK, h:h + 1] - colb))
                wsel = jnp.where(left, wcols[0], wcols[1])
                dYe_p = dY_p * jnp.where(left, ecols[0], ecols[1])
                bdx = _block_diag(xs_p, left)
                bddy = _block_diag(dY_p, left)
                dMx2 = _dot(dY_p.astype(BF16), bdx, NT)
                dMxT2 = _dot(xs_p.astype(BF16), bddy, NT)
                Q1 = _dot(Hg[:, ps].astype(BF16), _block_diag(dYe_p, left), NT)
                Q2 = _dot(Gg[:, ps].astype(BF16), bdx, NT)
                mts = []
                for j in range(2):
                    h = 2 * k + j
                    js = slice(j * CHUNK, (j + 1) * CHUNK)
                    dMx = dMx2[:, js]
                    A = CB * Ls[j]
                    AT = CBT * LTs[j]
                    ddt_rows.append(jnp.sum(A * dMx, axis=0, keepdims=True))
                    ATd = AT * dtcols[j]
                    rs_rows.append(jnp.sum(ATd * dMxT2[:, js], axis=0, keepdims=True))
                    dCB = dCB + dMx * Ls[j] * dtT[h:h + 1, :]
                    mts.append(ATd.astype(BF16))
                    deo_rows.append(jnp.sum(CT * Q1[:, js], axis=0, keepdims=True))
                    dw_rows.append(jnp.sum(BT * Q2[:, js], axis=0, keepdims=True))
                dxs_d.append(_dot(jnp.concatenate(mts, axis=1), bddy))
                dyes.append(dYe_p.astype(BF16))
                xws.append((xs_p * wsel).astype(BF16))
                wsels.append(wsel)
            dYe_g = jnp.concatenate(dyes, axis=1)
            xw_g = jnp.concatenate(xws, axis=1)
            Hgb, Ggb, dCBb = Hg.astype(BF16), Gg.astype(BF16), dCB.astype(BF16)
            cs = slice(C_OFF + g * SSM_STATE, C_OFF + (g + 1) * SSM_STATE)
            bs = slice(B_OFF + g * SSM_STATE, B_OFF + (g + 1) * SSM_STATE)
            dxbc_ref[:, cs] = (_dot(dYe_g, Hgb, NT) + _dot(dCBb, Bb)) * dsil_ref[:, cs]
            dxbc_ref[:, bs] = (_dot(xw_g, Ggb, NT) + _dot(dCBb, Cb, TN)) * dsil_ref[:, bs]
            BG = _dot(Bb, Ggb)
            dxbc_ref[:, gs] = (jnp.concatenate(dxs_d, axis=1) + BG * jnp.concatenate(wsels, axis=1)
                               + dY * d_ref[:, gs]) * dsil_ref[:, gs]
            cd_e = jnp.concatenate([jnp.broadcast_to(cd[:, g * 8 + e:g * 8 + e + 1], (1, SSMD)) for e in range(8)], axis=1)
            G[:, gs] = Gg * cd_e + _dot(Cb, dYe_g, TN)
        dnw_ref[...] += jnp.concatenate(dnw_cols, axis=1)
        eh = eh_ref[...]
        dd_ref[...] += jnp.sum(eh * jnp.concatenate(dd_cols, axis=1), axis=1, keepdims=True)
        dcd = jnp.sum(eh * jnp.concatenate(gh_cols, axis=1), axis=1, keepdims=True)
        DDT = jnp.concatenate(ddt_rows, axis=0)
        DW = jnp.concatenate(dw_rows, axis=0)
        DWw = DW * wT
        dacsT = jnp.concatenate(rs_rows, axis=0) - DDT * dtT + jnp.concatenate(deo_rows, axis=0) - DWw
        end = jnp.sum(DWw, axis=1, keepdims=True) + dcd * jnp.exp(lastT)
        lane = lax.broadcasted_iota(jnp.int32, (SSM_HEADS, CHUNK), 1)
        dacsT = dacsT + jnp.where(lane == CHUNK - 1, end, 0.0)
        dadtT = _dot3(dacsT, tril.astype(BF16))
        ddtT = dadtT * a_col + DDT + DW * dstT
        dalog_ref[...] += jnp.sum(dadtT * dtT, axis=1, keepdims=True) * a_col
        ddt_rawT = ddtT * _sigmoid(dt_rawT + dtb_ref[...])
        ddtb_ref[...] += jnp.sum(ddt_rawT, axis=1, keepdims=True)
        ddt_ref[...] = jnp.concatenate([ddt_rawT.T, jnp.zeros((CHUNK, 128 - SSM_HEADS), F32)], axis=1).astype(ddt_ref.dtype)

    rev = lambda c: nc - 1 - c
    vec = lambda n: _const_spec((1, n))
    colv = _const_spec((SSM_HEADS, 1))
    any_spec = pl.BlockSpec(memory_space=pl.ANY)
    return pl.pallas_call(
        body, name="ssd_bwd", grid=(nc,),
        in_specs=[pl.BlockSpec((CHUNK, SSM_INNER), lambda c: (rev(c), 0)),
                  pl.BlockSpec((CHUNK, SSM_INNER), lambda c: (rev(c), 0)),
                  pl.BlockSpec((CHUNK, XBC_COLS), lambda c: (rev(c), 0)),
                  pl.BlockSpec((CHUNK, XBC_COLS), lambda c: (rev(c), 0)),
                  pl.BlockSpec((1, SSM_STATE, SSM_INNER), lambda c: (rev(c), 0, 0)),
                  pl.BlockSpec((CHUNK, 128), lambda c: (rev(c), DT_OFF // 128)),
                  pl.BlockSpec((CHUNK, SSM_INNER), lambda c: (rev(c), Z_OFF // SSM_INNER)),
                  colv, colv, vec(SSM_INNER), vec(SSM_INNER), _const_spec((SSM_HEADS, SSM_INNER)), any_spec, any_spec],
        out_specs=[pl.BlockSpec((CHUNK, SSM_INNER), lambda c: (rev(c), Z_OFF // SSM_INNER)),
                   pl.BlockSpec((CHUNK, 128), lambda c: (rev(c), DT_OFF // 128)),
                   pl.BlockSpec((CHUNK, XBC_COLS), lambda c: (rev(c), 0)),
                   vec(SSM_INNER), colv, colv, colv],
        out_shape=[jax.ShapeDtypeStruct(dmain.shape, dmain.dtype), jax.ShapeDtypeStruct(dtail.shape, dtail.dtype),
                   jax.ShapeDtypeStruct((T, XBC_COLS), F32), jax.ShapeDtypeStruct((1, SSM_INNER), F32),
                   jax.ShapeDtypeStruct((SSM_HEADS, 1), F32), jax.ShapeDtypeStruct((SSM_HEADS, 1), F32),
                   jax.ShapeDtypeStruct((SSM_HEADS, 1), F32)],
        scratch_shapes=[pltpu.VMEM((SSM_STATE, SSM_INNER), F32)],
        input_output_aliases={12: 0, 13: 1},
        compiler_params=_cp(("arbitrary",)))(dyo, ypre, xbc, dsil, hs, proj_tail, proj_main, dtb_col, alog_col, d_exp,
                                             norm_w, ehead_t, dmain, dtail)


def _rel_bucket(rel):
    n = jnp.maximum(rel, 0)
    max_exact = REL_BUCKETS // 2
    nf = jnp.maximum(n, 1).astype(F32)
    large = max_exact + (jnp.log(nf / max_exact) / math.log(REL_MAX_DIST / max_exact)
                         * (REL_BUCKETS - max_exact)).astype(jnp.int32)
    large = jnp.minimum(large, REL_BUCKETS - 1)
    return jnp.where(n < max_exact, n, large)


def _band_geometry():
    qi = jnp.arange(WINDOW)[:, None] + WINDOW
    kj = jnp.arange(2 * WINDOW)[None, :]
    rel = qi - kj
    return _rel_bucket(rel), (rel >= 0) & (rel < WINDOW)


def _attn_logits(kband, qh, bias_h, first):
    s = _dot(kband, qh, NT) * (HEADDIM ** -0.5) + bias_h
    rowk = lax.broadcasted_iota(jnp.int32, (2 * WINDOW, WINDOW), 0)
    return jnp.where(jnp.logical_and(first, rowk < WINDOW), NEG, s)


def _attn_fwd(proj_main, proj_tail, bias_tbl, sinks):
    T = proj_main.shape[0]
    nb = T // WINDOW

    def body(q_ref, kv_ref, kvp_ref, bias_ref, sink_ref, o_ref, lse_ref):
        i = pl.program_id(0)
        first = i == 0
        outs, lses = [], []
        for kvh in range(ATTN_KV):
            ks = slice(K_OFF + kvh * HEADDIM, K_OFF + (kvh + 1) * HEADDIM)
            vs = slice(V_OFF + kvh * HEADDIM, V_OFF + (kvh + 1) * HEADDIM)
            kband = jnp.concatenate([kvp_ref[:, ks], kv_ref[:, ks]], axis=0).astype(BF16)
            vband = jnp.concatenate([kvp_ref[:, vs], kv_ref[:, vs]], axis=0).astype(BF16)
            heads = range(kvh * ATTN_GROUP, (kvh + 1) * ATTN_GROUP)
            logits = [_attn_logits(kband, q_ref[:, h * HEADDIM:(h + 1) * HEADDIM].astype(BF16), bias_ref[h], first)
                      for h in heads]
            probs = []
            for h, s in zip(heads, logits):
                sink = sink_ref[:, h:h + 1]
                m = jnp.maximum(jnp.max(s, axis=0, keepdims=True), sink)
                p = jnp.exp(s - m)
                den = jnp.sum(p, axis=0, keepdims=True) + jnp.exp(sink - m)
                probs.append((p * (1.0 / den)).astype(BF16))
                lses.append(m + jnp.log(den))
            outs += [_dot(pt, vband, TN) for pt in probs]
        o_ref[...] = jnp.concatenate(outs, axis=1).astype(BF16)
        lse_ref[...] = jnp.concatenate(lses, axis=0)

    return pl.pallas_call(
        body, name="attn_fwd", grid=(nb,),
        in_specs=[pl.BlockSpec((WINDOW, D_MODEL), lambda i: (i, Q_OFF // D_MODEL)),
                  pl.BlockSpec((WINDOW, 256), lambda i: (i, 0)),
                  pl.BlockSpec((WINDOW, 256), lambda i: (jnp.maximum(i - 1, 0), 0)),
                  _const_spec((ATTN_HEADS, 2 * WINDOW, WINDOW)), _const_spec((1, ATTN_HEADS))],
        out_specs=[pl.BlockSpec((WINDOW, D_MODEL), lambda i: (i, 0)),
                   pl.BlockSpec((ATTN_HEADS, WINDOW), lambda i: (0, i))],
        out_shape=[jax.ShapeDtypeStruct((T, D_MODEL), BF16), jax.ShapeDtypeStruct((ATTN_HEADS, T), F32)],
        compiler_params=_cp(("arbitrary",)))(proj_main, proj_tail, proj_tail, bias_tbl, sinks)


def _attn_bwd(dy, lse, proj_main, proj_tail, bias_tbl, sinks, dmain):
    T = proj_main.shape[0]
    nb = T // WINDOW

    def body(dy_ref, lse_ref, q_ref, kv_ref, kvp_ref, bias_ref, sink_ref, dmain_in,
             dq_ref, dkv_ref, dbias_ref, dsink_ref, carry):
        del dmain_in
        i = pl.program_id(0)
        first = i == 0

        @pl.when(first)
        def _():
            carry[...] = jnp.zeros_like(carry)
            dbias_ref[...] = jnp.zeros_like(dbias_ref)
            dsink_ref[...] = jnp.zeros_like(dsink_ref)

        @pl.when(i < nb)
        def _():
            scale = HEADDIM ** -0.5
            dqs, dsinks, dks, dvs = [], [], [], []
            for kvh in range(ATTN_KV):
                ks = slice(K_OFF + kvh * HEADDIM, K_OFF + (kvh + 1) * HEADDIM)
                vs = slice(V_OFF + kvh * HEADDIM, V_OFF + (kvh + 1) * HEADDIM)
                kband = jnp.concatenate([kvp_ref[:, ks], kv_ref[:, ks]], axis=0).astype(BF16)
                vband = jnp.concatenate([kvp_ref[:, vs], kv_ref[:, vs]], axis=0).astype(BF16)
                heads = range(kvh * ATTN_GROUP, (kvh + 1) * ATTN_GROUP)
                qs = [q_ref[:, h * HEADDIM:(h + 1) * HEADDIM].astype(BF16) for h in heads]
                dos = [dy_ref[:, h * HEADDIM:(h + 1) * HEADDIM] for h in heads]
                logits = [_attn_logits(kband, qh, bias_ref[h], first) for h, qh in zip(heads, qs)]
                dps = [_dot(vband, do, NT) for do in dos]
                pbs, dsbs = [], []
                for h, s, dp in zip(heads, logits, dps):
                    lse_h = lse_ref[h:h + 1, :]
                    p = jnp.exp(s - lse_h)
                    delta = jnp.sum(p * dp, axis=0, keepdims=True)
                    ds = p * (dp - delta)
                    psink = jnp.exp(sink_ref[:, h:h + 1] - lse_h)
                    dsinks.append(-jnp.sum(psink * delta, axis=1, keepdims=True))
                    dbias_ref[h] += ds
                    pbs.append(p.astype(BF16))
                    dsbs.append((ds * scale).astype(BF16))
                dqs += [_dot(dsb, kband, TN) for dsb in dsbs]
                dks.append(_dot(jnp.concatenate(dsbs, axis=1), jnp.concatenate(qs, axis=0)))
                dvs.append(_dot(jnp.concatenate(pbs, axis=1), jnp.concatenate(dos, axis=0)))
            dq_ref[...] = jnp.concatenate(dqs, axis=1).astype(dq_ref.dtype)
            dsink_ref[...] += jnp.concatenate(dsinks, axis=1)
            dkv = jnp.concatenate(dks + dvs, axis=1)
            dkv_ref[...] = (carry[...] + dkv[0:WINDOW, :]).astype(dkv_ref.dtype)
            carry[...] = dkv[WINDOW:, :]

        @pl.when(i == nb)
        def _():
            dkv_ref[...] = carry[...].astype(dkv_ref.dtype)

    cur = lambda i: jnp.minimum(i, nb - 1)
    return pl.pallas_call(
        body, name="attn_bwd", grid=(nb + 1,),
        in_specs=[pl.BlockSpec((WINDOW, D_MODEL), lambda i: (cur(i), 0)),
                  pl.BlockSpec((ATTN_HEADS, WINDOW), lambda i: (0, cur(i))),
                  pl.BlockSpec((WINDOW, D_MODEL), lambda i: (cur(i), Q_OFF // D_MODEL)),
                  pl.BlockSpec((WINDOW, 256), lambda i: (cur(i), 0)),
                  pl.BlockSpec((WINDOW, 256), lambda i: (jnp.maximum(cur(i) - 1, 0), 0)),
                  _const_spec((ATTN_HEADS, 2 * WINDOW, WINDOW)), _const_spec((1, ATTN_HEADS)),
                  pl.BlockSpec(memory_space=pl.ANY)],
        out_specs=[pl.BlockSpec((WINDOW, D_MODEL), lambda i: (cur(i), Q_OFF // D_MODEL)),
                   pl.BlockSpec((WINDOW, 256), lambda i: (jnp.maximum(i - 1, 0), 0)),
                   _const_spec((ATTN_HEADS, 2 * WINDOW, WINDOW)), _const_spec((1, ATTN_HEADS))],
        out_shape=[jax.ShapeDtypeStruct(dmain.shape, dmain.dtype), jax.ShapeDtypeStruct((T, TAIL_COLS), BF16),
                   jax.ShapeDtypeStruct((ATTN_HEADS, 2 * WINDOW, WINDOW), F32),
                   jax.ShapeDtypeStruct((1, ATTN_HEADS), F32)],
        scratch_shapes=[pltpu.VMEM((WINDOW, 256), F32)],
        input_output_aliases={7: 0},
        compiler_params=_cp(("arbitrary",)))(dy, lse, proj_main, proj_tail, proj_tail, bias_tbl, sinks, dmain)


def _bias_table(rel_bias_t, onehot_t, mask):
    def body(rb_ref, oh_ref, m_ref, o_ref):
        o_ref[...] = _dot3(rb_ref[...], oh_ref[...]) + m_ref[...]

    flat = pl.pallas_call(body, name="bias_table",
                          out_shape=jax.ShapeDtypeStruct((ATTN_HEADS, 2 * WINDOW * WINDOW), F32))(rel_bias_t, onehot_t, mask)
    return flat.reshape(ATTN_HEADS, 2 * WINDOW, WINDOW)


def _rel_bias_grad(dbias, onehot):
    def body(d_ref, oh_ref, o_ref):
        o_ref[...] = _dot(d_ref[...], oh_ref[...], NN, HIGHEST)

    return pl.pallas_call(body, name="rel_bias_grad",
                          out_shape=jax.ShapeDtypeStruct((ATTN_HEADS, REL_BUCKETS), F32))(dbias, onehot)


def _ln_fwd(r, g, b):
    mu = jnp.mean(r, axis=1, keepdims=True)
    xc = r - mu
    rstd = lax.rsqrt(jnp.mean(xc * xc, axis=1, keepdims=True) + LN_EPS)
    xhat = xc * rstd
    return xhat * g + b, xhat, rstd


def _ln_bwd(dy, xhat, rstd, g):
    dxh = dy * g
    return rstd * (dxh - jnp.mean(dxh, axis=1, keepdims=True) - xhat * jnp.mean(dxh * xhat, axis=1, keepdims=True))


def _merge_fwd(y_ssm, y_attn, proj_main, b_gate, w_bs, w_ba, tm=512):
    T = y_ssm.shape[0]

    def body(ys_ref, ya_ref, gs_ref, ga_ref, bg_ref, wbs_ref, wba_ref, m_ref, bs_ref, ba_ref):
        bs = _dot(ys_ref[...], wbs_ref[...])
        ba = _dot(ya_ref[...], wba_ref[...])
        g_s = _sigmoid(gs_ref[...] + bg_ref[:, 0:D_MODEL])
        g_a = _sigmoid(ga_ref[...] + bg_ref[:, D_MODEL:])
        m_ref[...] = (g_s * bs + g_a * ba).astype(BF16)
        bs_ref[...] = bs
        ba_ref[...] = ba

    row = lambda w, off=0: pl.BlockSpec((tm, w), lambda i: (i, off))
    return pl.pallas_call(
        body, name="merge_fwd", grid=(T // tm,),
        in_specs=[row(SSM_INNER), row(D_MODEL), row(D_MODEL, GATE_OFF // D_MODEL), row(D_MODEL, GATE_OFF // D_MODEL + 1),
                  _const_spec((1, 2 * D_MODEL)), _const_spec((SSM_INNER, D_MODEL)), _const_spec((D_MODEL, D_MODEL))],
        out_specs=[row(D_MODEL), row(D_MODEL), row(D_MODEL)],
        out_shape=[jax.ShapeDtypeStruct((T, D_MODEL), BF16), jax.ShapeDtypeStruct((T, D_MODEL), F32),
                   jax.ShapeDtypeStruct((T, D_MODEL), F32)],
        compiler_params=_cp(("parallel",)))(y_ssm, y_attn, proj_main, proj_main, b_gate, w_bs, w_ba)


def _mix_ln1(merged, w_mo, x, g1, b1, tm=512):
    T = x.shape[0]

    def body(m_ref, w_ref, x_ref, g_ref, b_ref, r_ref, h_ref, hb_ref):
        r = ALPHA * x_ref[...] + _dot(m_ref[...], w_ref[...])
        r_ref[...] = r
        h = _ln_fwd(r, g_ref[...], b_ref[...])[0]
        h_ref[...] = h
        hb_ref[...] = h.astype(BF16)

    row = pl.BlockSpec((tm, D_MODEL), lambda i: (i, 0))
    return pl.pallas_call(
        body, name="mix_ln1", grid=(T // tm,),
        in_specs=[row, _const_spec((D_MODEL, D_MODEL)), row, _const_spec((1, D_MODEL)), _const_spec((1, D_MODEL))],
        out_specs=[row, row, row],
        out_shape=[jax.ShapeDtypeStruct((T, D_MODEL), F32), jax.ShapeDtypeStruct((T, D_MODEL), F32),
                   jax.ShapeDtypeStruct((T, D_MODEL), BF16)],
        compiler_params=_cp(("parallel",)))(merged, w_mo, x, g1, b1)


def _ffn_conv_glu(u_pre, w, b, tr=2048, tc=256):
    T = u_pre.shape[0]
    tr = min(tr, T)
    K = FFN_CONV
    nj = D_FF // tc
    hb = tr // HALO
    assert T % tr == 0 and D_FF % tc == 0

    def body(xg_ref, xgp_ref, xv_ref, xvp_ref, wg_ref, wv_ref, bg_ref, bv_ref, u_ref, a_ref, head_g, head_v):
        i = pl.program_id(1)
        halves = []
        for x_ref, xp_ref, w_ref, b_ref, head in ((xg_ref, xgp_ref, wg_ref, bg_ref, head_g),
                                                  (xv_ref, xvp_ref, wv_ref, bv_ref, head_v)):
            head[0:HALO, :] = jnp.where(i > 0, xp_ref[...], 0.0)
            head[HALO:, :] = x_ref[0:HALO, :]
            halves.append((x_ref, head, _taps(w_ref, K, tc), jnp.broadcast_to(b_ref[...], (HALO, tc))))

        def conv(half, r):
            x_ref, head, wk, bias = halves[half]
            lo = r * HALO
            acc = bias + wk[K - 1] * x_ref[lo:lo + HALO, :]
            for k in range(K - 1):
                s = K - 1 - k
                acc = acc + wk[k] * (head[HALO - s:2 * HALO - s, :] if r == 0 else x_ref[lo - s:lo + HALO - s, :])
            return acc

        for r2 in range(tr // (2 * HALO)):
            acts = []
            for r in (2 * r2, 2 * r2 + 1):
                lo = r * HALO
                ug, uv = conv(0, r), conv(1, r)
                u_ref[0, lo:lo + HALO, :] = ug
                u_ref[1, lo:lo + HALO, :] = uv
                acts.append(ug * _sigmoid(ug) * uv)
            a_ref[2 * r2 * HALO:(2 * r2 + 2) * HALO, :] = jnp.concatenate(acts, axis=0).astype(BF16)

    tile = lambda off: pl.BlockSpec((tr, tc), lambda j, i: (i, off + j))
    prev = lambda off: pl.BlockSpec((HALO, tc), lambda j, i: (jnp.maximum(i * hb - 1, 0), off + j))
    row = lambda rows, off: pl.BlockSpec((rows, tc), lambda j, i: (0, off + j))
    return pl.pallas_call(
        body, name="ffn_conv_glu", grid=(nj, T // tr),
        in_specs=[tile(0), prev(0), tile(nj), prev(nj), row(K, 0), row(K, nj), row(1, 0), row(1, nj)],
        out_specs=[pl.BlockSpec((2, tr, tc), lambda j, i: (0, i, j)), pl.BlockSpec((tr, tc), lambda j, i: (i, j))],
        out_shape=[jax.ShapeDtypeStruct((2, T, D_FF), F32), jax.ShapeDtypeStruct((T, D_FF), BF16)],
        scratch_shapes=[pltpu.VMEM((2 * HALO, tc), F32), pltpu.VMEM((2 * HALO, tc), F32)],
        compiler_params=_cp(("parallel", "arbitrary")))(u_pre, u_pre, u_pre, u_pre, w, w, b, b)


def _down_ln2_loss(act, w_down, h1, target, g2, b2, tm=512):
    T = h1.shape[0]

    def body(a_ref, w_ref, h_ref, t_ref, g_ref, b_ref, dr_ref, dg_ref, db_ref, l_ref):
        @pl.when(pl.program_id(0) == 0)
        def _():
            dg_ref[...] = jnp.zeros_like(dg_ref)
            db_ref[...] = jnp.zeros_like(db_ref)
            l_ref[...] = jnp.zeros_like(l_ref)

        r = ALPHA * h_ref[...] + _dot(a_ref[...], w_ref[...])
        y, xhat, rstd = _ln_fwd(r, g_ref[...], b_ref[...])
        err = y - t_ref[...]
        l_ref[...] += jnp.sum(err * err, keepdims=True)
        dy = err * (1.0 / D_MODEL)
        dg_ref[...] += jnp.sum(dy * xhat, axis=0, keepdims=True)
        db_ref[...] += jnp.sum(dy, axis=0, keepdims=True)
        dr_ref[...] = _ln_bwd(dy, xhat, rstd, g_ref[...])

    row = pl.BlockSpec((tm, D_MODEL), lambda i: (i, 0))
    vec = _const_spec((1, D_MODEL))
    return pl.pallas_call(
        body, name="down_ln2_loss", grid=(T // tm,),
        in_specs=[pl.BlockSpec((tm, D_FF), lambda i: (i, 0)), _const_spec((D_FF, D_MODEL)), row, row, vec, vec],
        out_specs=[row, vec, vec, _const_spec((1, 1))],
        out_shape=[jax.ShapeDtypeStruct((T, D_MODEL), F32), jax.ShapeDtypeStruct((1, D_MODEL), F32),
                   jax.ShapeDtypeStruct((1, D_MODEL), F32), jax.ShapeDtypeStruct((1, 1), F32)],
        compiler_params=_cp(("arbitrary",)))(act, w_down, h1, target, g2, b2)


def _ffn_bwd_act(dr2, w_down, u, tm=512, tn=1408):
    T = dr2.shape[0]
    tm = min(tm, T)
    nj = D_FF // tn

    def body(d_ref, w_ref, u_ref, o_ref, dact):
        dact[...] = _dot(d_ref[...].astype(BF16), w_ref[...], NT)
        for r in range(tm // HALO):
            rows = slice(r * HALO, (r + 1) * HALO)
            da, g, v = dact[rows, :], u_ref[0, rows, :], u_ref[1, rows, :]
            sg = _sigmoid(g)
            o_ref[0, rows, :] = da * v * (sg * (1.0 + g * (1.0 - sg)))
            o_ref[1, rows, :] = da * (g * sg)

    both = pl.BlockSpec((2, tm, tn), lambda i, j: (0, i, j))
    return pl.pallas_call(
        body, name="ffn_bwd_act", grid=(T // tm, nj),
        in_specs=[pl.BlockSpec((tm, D_MODEL), lambda i, j: (i, 0)), pl.BlockSpec((tn, D_MODEL), lambda i, j: (j, 0)), both],
        out_specs=both, out_shape=jax.ShapeDtypeStruct((2, T, D_FF), F32),
        scratch_shapes=[pltpu.VMEM((tm, tn), F32)],
        compiler_params=_cp(("parallel", "parallel")))(dr2, w_down, u)


def _ffn_bwd_in(du_pre, w_up, dr2, r1, g1, b1, tm=1024, tk=1408):
    T = dr2.shape[0]
    tm = min(tm, T)
    assert T % tm == 0
    nk = 2 * D_FF // tk

    def body(d_ref, w_ref, dr2_ref, r_ref, g_ref, b_ref, dr1_ref, dg_ref, db_ref, acc):
        i, k = pl.program_id(0), pl.program_id(1)

        @pl.when(jnp.logical_and(i == 0, k == 0))
        def _():
            dg_ref[...] = jnp.zeros_like(dg_ref)
            db_ref[...] = jnp.zeros_like(db_ref)

        @pl.when(k == 0)
        def _():
            acc[...] = ALPHA * dr2_ref[...]

        acc[...] += _dot(d_ref[...], w_ref[...], NT)

        @pl.when(k == nk - 1)
        def _():
            _, xhat, rstd = _ln_fwd(r_ref[...], g_ref[...], b_ref[...])
            dy = acc[...]
            dg_ref[...] += jnp.sum(dy * xhat, axis=0, keepdims=True)
            db_ref[...] += jnp.sum(dy, axis=0, keepdims=True)
            dr1_ref[...] = _ln_bwd(dy, xhat, rstd, g_ref[...])

    row = pl.BlockSpec((tm, D_MODEL), lambda i, k: (i, 0))
    vec = _const_spec((1, D_MODEL))
    return pl.pallas_call(
        body, name="ffn_bwd_in", grid=(T // tm, nk),
        in_specs=[pl.BlockSpec((tm, tk), lambda i, k: (i, k)), pl.BlockSpec((D_MODEL, tk), lambda i, k: (0, k)),
                  row, row, vec, vec],
        out_specs=[row, vec, vec],
        out_shape=[jax.ShapeDtypeStruct((T, D_MODEL), F32), jax.ShapeDtypeStruct((1, D_MODEL), F32),
                   jax.ShapeDtypeStruct((1, D_MODEL), F32)],
        scratch_shapes=[pltpu.VMEM((tm, D_MODEL), F32)],
        compiler_params=_cp(("arbitrary", "arbitrary")))(du_pre, w_up, dr2, r1, g1, b1)


def _mix_bwd(dr1, w_mo, w_bs, w_ba, bs, ba, proj_main, b_gate, tm=512):
    T = dr1.shape[0]

    def body(d_ref, wmo_ref, wbs_ref, wba_ref, bs_ref, ba_ref, gs_ref, ga_ref, bg_ref,
             dg_ref, dbs_ref, dba_ref, dys_ref, dya_ref, dbg_ref):
        @pl.when(pl.program_id(0) == 0)
        def _():
            dbg_ref[...] = jnp.zeros_like(dbg_ref)

        dm = _dot(d_ref[...].astype(BF16), wmo_ref[...], NT)
        g_s = _sigmoid(gs_ref[...] + bg_ref[:, 0:D_MODEL])
        g_a = _sigmoid(ga_ref[...] + bg_ref[:, D_MODEL:])
        dgs = dm * bs_ref[...] * g_s * (1.0 - g_s)
        dga = dm * ba_ref[...] * g_a * (1.0 - g_a)
        dg_ref[:, 0:D_MODEL] = dgs.astype(BF16)
        dg_ref[:, D_MODEL:] = dga.astype(BF16)
        dbg_ref[:, 0:D_MODEL] += jnp.sum(dgs, axis=0, keepdims=True)
        dbg_ref[:, D_MODEL:] += jnp.sum(dga, axis=0, keepdims=True)
        dbs = (dm * g_s).astype(BF16)
        dba = (dm * g_a).astype(BF16)
        dbs_ref[...] = dbs
        dba_ref[...] = dba
        dys_ref[...] = _dot(dbs, wbs_ref[...], NT)
        dya_ref[...] = _dot(dba, wba_ref[...], NT).astype(BF16)

    row = lambda w, off=0: pl.BlockSpec((tm, w), lambda i: (i, off))
    return pl.pallas_call(
        body, name="mix_bwd", grid=(T // tm,),
        in_specs=[row(D_MODEL), _const_spec((D_MODEL, D_MODEL)), _const_spec((SSM_INNER, D_MODEL)),
                  _const_spec((D_MODEL, D_MODEL)), row(D_MODEL), row(D_MODEL),
                  row(D_MODEL, GATE_OFF // D_MODEL), row(D_MODEL, GATE_OFF // D_MODEL + 1), _const_spec((1, 2 * D_MODEL))],
        out_specs=[row(2 * D_MODEL, GATE_OFF // (2 * D_MODEL)), row(D_MODEL), row(D_MODEL), row(SSM_INNER), row(D_MODEL),
                   _const_spec((1, 2 * D_MODEL))],
        out_shape=[jax.ShapeDtypeStruct((T, MAIN_COLS), BF16), jax.ShapeDtypeStruct((T, D_MODEL), BF16),
                   jax.ShapeDtypeStruct((T, D_MODEL), BF16), jax.ShapeDtypeStruct((T, SSM_INNER), F32),
                   jax.ShapeDtypeStruct((T, D_MODEL), BF16), jax.ShapeDtypeStruct((1, 2 * D_MODEL), F32)],
        compiler_params=_cp(("arbitrary",)))(dr1, w_mo, w_bs, w_ba, bs, ba, proj_main, proj_main, b_gate)


def _local_step(x, target, w, p, late_weights=None, early_grads=None):
    xb = x.astype(BF16)
    if late_weights is None:
        proj_main = _matmul(xb, w["in_main"], "nn", F32, "in_proj_main", tm=1024, tn=2048)
    else:
        proj_main, *landed = _matmul(xb, w["in_main"], "nn", F32, "in_proj_main", tm=1024, tn=2048,
                                     comm=("gather", late_weights[0]))
        w = {**w, **late_weights[1](landed)}
    proj_tail = _matmul(xb, w["in_tail"], "nn", F32, "in_proj_tail", tn=TAIL_COLS)
    xbc, dsil = _conv_silu_fwd(proj_main, XBC_OFF, XBC_COLS, p["ssm_conv_w"], p["ssm_conv_b"], SSM_CONV, "ssm_conv_fwd")
    y_ssm, ypre, hs = _ssd_fwd(xbc, proj_main, proj_tail, p["dtb_col"], p["alog_col"], p["d_exp"], p["ssm_norm_w"])
    y_attn, lse = _attn_fwd(proj_main, proj_tail, p["bias_tbl"], p["attn_sinks"])
    merged, bs, ba = _merge_fwd(y_ssm, y_attn, proj_main, p["b_gate"], w["bs"], w["ba"])
    r1, h1, h1b = _mix_ln1(merged, w["mo"], x, p["ln1_g"], p["ln1_b"])
    u_pre = _matmul(h1b, w["up"], "nn", F32, "ffn_up", tm=1024, tn=1408)
    u, act = _ffn_conv_glu(u_pre, p["ffn_conv_w"], p["ffn_conv_b"])
    dr2, dg2, db2, sq = _down_ln2_loss(act, w["down"], h1, target, p["ln2_g"], p["ln2_b"])
    g = {"ln2_g": dg2, "ln2_b": db2}
    g["w_down"] = _matmul(act, dr2, "tn", BF16, "dw_down", tm=1408, tn=1024, tk=1024)
    du = _ffn_bwd_act(dr2, w["down"], u)
    du_pre, g["ffn_conv_w"], g["ffn_conv_b"] = _conv_bwd(
        du, u_pre, 0, 2 * D_FF, p["ffn_conv_w"], FFN_CONV, jax.ShapeDtypeStruct((x.shape[0], 2 * D_FF), BF16), 0,
        "ffn_conv_bwd", tr=2048, tc=256)
    g["w_up"] = _matmul(h1b, du_pre, "tn", BF16, "dw_up", tm=1024, tn=1408, tk=2048)
    dr1, g["ln1_g"], g["ln1_b"] = _ffn_bwd_in(du_pre, w["up"], dr2, r1, p["ln1_g"], p["ln1_b"])
    g["w_mix_out"] = _matmul(merged, dr1, "tn", BF16, "dw_mix_out", tm=1024, tn=1024, tk=2048)
    dmain, dbs, dba, dy_ssm, dy_attn, g["b_gate"] = _mix_bwd(dr1, w["mo"], w["bs"], w["ba"], bs, ba, proj_main, p["b_gate"])
    g["w_branch_ssm"] = _matmul(y_ssm, dbs, "tn", BF16, "dw_branch_ssm", tm=1024, tn=1024, tk=2048)
    g["w_branch_attn"] = _matmul(y_attn, dba, "tn", BF16, "dw_branch_attn", tm=1024, tn=1024, tk=2048)
    dmain, dtail, dbias, g["attn_sinks"] = _attn_bwd(dy_attn, lse, proj_main, proj_tail, p["bias_tbl"], p["attn_sinks"], dmain)
    g["rel_bias"] = _rel_bias_grad(dbias.reshape(ATTN_HEADS, WINDOW * 2 * WINDOW), p["bucket_onehot"]).T
    dmain, dtail, dco, g["ssm_norm_w"], dd, dalog, ddtb = _ssd_bwd(
        dy_ssm, ypre, xbc, dsil, hs, proj_main, proj_tail, p["dtb_col"], p["alog_col"], p["d_exp"], p["ssm_norm_w"],
        p["ehead_t"], dmain, dtail)
    g["ssm_d"], g["ssm_a_log"], g["ssm_dt_bias"] = (a.reshape(1, SSM_HEADS) for a in (dd, dalog, ddtb))
    dmain, g["ssm_conv_w"], g["ssm_conv_b"] = _conv_bwd(
        dco, proj_main, XBC_OFF, XBC_COLS, p["ssm_conv_w"], SSM_CONV, dmain, XBC_OFF, "ssm_conv_bwd")
    g["in_tail"] = _matmul(xb, dtail, "tn", BF16, "dw_in_tail", tm=1024, tn=TAIL_COLS, tk=2048)
    landed = []
    if early_grads is None:
        g["in_main"] = _matmul(xb, dmain, "tn", BF16, "dw_in_main", tm=1024, tn=1024, tk=2048)
    else:
        g["in_main"], *landed = _matmul(xb, dmain, "tn", BF16, "dw_in_main", tm=1024, tn=1024, tk=2048,
                                        comm=("exchange", early_grads(g)))
    return sq, (dmain, dtail, dr1), w, g, landed


def _grad_x(dproj, w, exchange=None):
    dmain, dtail, dr1 = dproj
    landed = None
    if exchange is None:
        dx = _matmul(dmain, w["in_main"], "nt", F32, "dx_main", tm=1024, tk=2048, addend=dr1, addend_scale=ALPHA)
    else:
        dx, landed = _matmul(dmain, w["in_main"], "nt", F32, "dx_main", tm=1024, tk=2048, addend=dr1,
                             addend_scale=ALPHA, comm=("exchange", exchange))
    dx = _matmul(dtail, w["in_tail"], "nt", F32, "dx_tail", tk=TAIL_COLS, addend=dx)
    return dx if exchange is None else (dx, landed)


def _split_w_in(w):
    seg = lambda off, n: w[:, off:off + n]
    main = jnp.concatenate([seg(O_Z, 2048), seg(O_XBC, XBC_COLS), seg(O_Q, D_MODEL), seg(O_GATE, 2 * D_MODEL)], axis=1)
    tail = jnp.concatenate([seg(O_K, 128), seg(O_V, 128), seg(O_DT, SSM_HEADS),
                            jnp.zeros((w.shape[0], 128 - SSM_HEADS), w.dtype)], axis=1)
    return main, tail


def _join_w_in(main, tail):
    return jnp.concatenate([main[:, Z_OFF:Z_OFF + 2048], main[:, XBC_OFF:XBC_OFF + XBC_COLS],
                            tail[:, DT_OFF:DT_OFF + SSM_HEADS], main[:, Q_OFF:Q_OFF + D_MODEL],
                            tail[:, K_OFF:K_OFF + 128], tail[:, V_OFF:V_OFF + 128],
                            main[:, GATE_OFF:GATE_OFF + 2 * D_MODEL]], axis=1)


def _prep_params(rel_bias, b_gate, ssm_conv_w, ssm_conv_b, ssm_dt_bias, ssm_a_log, ssm_d, ssm_norm_w, attn_sinks,
                 ln1_g, ln1_b, ffn_conv_w, ffn_conv_b, ln2_g, ln2_b):
    bucket, in_window = _band_geometry()
    bucket, in_window = bucket.T, in_window.T
    onehot = jnp.logical_and(bucket.reshape(-1, 1) == jnp.arange(REL_BUCKETS)[None, :],
                             in_window.reshape(-1, 1)).astype(F32)
    onehot_t = jnp.logical_and(bucket.reshape(1, -1) == jnp.arange(REL_BUCKETS)[:, None],
                               in_window.reshape(1, -1)).astype(BF16)
    bias_tbl = _bias_table(rel_bias.T, onehot_t, jnp.where(in_window.reshape(1, -1), 0.0, NEG))
    ehead_t = (jnp.arange(SSM_INNER)[None, :] // SSMD == jnp.arange(SSM_HEADS)[:, None]).astype(F32)
    return {"bias_tbl": bias_tbl, "bucket_onehot": onehot, "b_gate": b_gate, "ssm_conv_w": ssm_conv_w,
            "ssm_conv_b": ssm_conv_b, "dtb_col": ssm_dt_bias.reshape(SSM_HEADS, 1),
            "alog_col": ssm_a_log.reshape(SSM_HEADS, 1), "ehead_t": ehead_t,
            "d_exp": jnp.repeat(ssm_d, SSMD, axis=1), "ssm_norm_w": ssm_norm_w, "attn_sinks": attn_sinks,
            "ln1_g": ln1_g, "ln1_b": ln1_b, "ffn_conv_w": ffn_conv_w, "ffn_conv_b": ffn_conv_b,
            "ln2_g": ln2_g, "ln2_b": ln2_b}


def _all_gather(shards, name):
    nb = len(shards)

    def body(*refs):
        for phase in _gather_phases(refs[:nb], refs[nb:2 * nb], *refs[2 * nb:]):
            phase()

    any_spec = pl.BlockSpec(memory_space=pl.ANY)
    return pl.pallas_call(
        body, name=name, out_shape=[jax.ShapeDtypeStruct((N_DEV,) + s.shape, s.dtype) for s in shards],
        in_specs=[any_spec] * nb, out_specs=[any_spec] * nb, scratch_shapes=_comm_sems(nb))(*shards)


def _adamw_math(w, g, m, v):
    m = ADAM_B1 * m + (1.0 - ADAM_B1) * g
    v = ADAM_B2 * v + (1.0 - ADAM_B2) * (g * g)
    m_hat = m / (1.0 - ADAM_B1 ** ADAM_STEP)
    v_hat = v / (1.0 - ADAM_B2 ** ADAM_STEP)
    return -ADAM_LR * (m_hat / (jnp.sqrt(v_hat) + ADAM_EPS) + ADAM_WD * w), m, v


def _slot_total(s_ref):
    g = s_ref[0].astype(F32)
    for i in range(1, N_DEV):
        g = g + s_ref[i].astype(F32)
    return g


def _adamw(landed, w, m, v, name):
    R, C = w.shape
    tr = 256 if R % 256 == 0 and R > 256 else R

    def body(s_ref, w_ref, m_ref, v_ref, g_ref, d_ref, nm_ref, nv_ref):
        g = _slot_total(s_ref)
        g_ref[...] = g
        d_ref[...], nm_ref[...], nv_ref[...] = _adamw_math(w_ref[...], g, m_ref[...], v_ref[...])

    spec = pl.BlockSpec((tr, C), lambda i: (i, 0))
    return pl.pallas_call(
        body, name=name, grid=(R // tr,), in_specs=[pl.BlockSpec((N_DEV, tr, C), lambda i: (0, i, 0))] + [spec] * 3,
        out_specs=[spec] * 4, out_shape=[jax.ShapeDtypeStruct((R, C), F32)] * 4,
        compiler_params=_cp(("parallel",)))(landed, w, m, v)


def _small_update(landed, ws, ms, vs):
    k = len(ws)

    def body(*refs):
        s_ref, w_refs, m_refs, v_refs = refs[0], refs[1:1 + k], refs[1 + k:1 + 2 * k], refs[1 + 2 * k:1 + 3 * k]
        outs = refs[1 + 3 * k:]
        g_all = _slot_total(s_ref)
        for i in range(k):
            n = w_refs[i].shape[1]
            g = g_all[i:i + 1, 0:n]
            outs[i][...] = g
            outs[k + i][...], outs[2 * k + i][...], outs[3 * k + i][...] = _adamw_math(
                w_refs[i][...], g, m_refs[i][...], v_refs[i][...])

    return pl.pallas_call(body, name="small_update",
                          out_shape=[jax.ShapeDtypeStruct(w.shape, F32) for w in ws] * 4)(landed, *ws, *ms, *vs)


SHARDED = {"w_in": "cols", "w_branch_ssm": "rows", "w_branch_attn": "rows", "w_mix_out": "rows", "w_up": "cols",
           "w_down": "rows", "ssm_conv_w": "cols", "ffn_conv_w": "cols"}
LATE = ("w_branch_ssm", "w_branch_attn", "w_mix_out", "w_up", "w_down")
SHORT = {"w_branch_ssm": "bs", "w_branch_attn": "ba", "w_mix_out": "mo", "w_up": "up", "w_down": "down"}
SMALL = ("rel_bias", "b_gate", "ssm_conv_b", "ssm_dt_bias", "ssm_a_log", "ssm_d", "ssm_norm_w", "attn_sinks",
         "ln1_g", "ln1_b", "ffn_conv_b", "ln2_g", "ln2_b")
WEIGHTS = ("rel_bias", "w_in", "b_gate", "ssm_conv_w", "ssm_conv_b", "ssm_dt_bias", "ssm_a_log", "ssm_d", "ssm_norm_w",
           "attn_sinks", "w_branch_ssm", "w_branch_attn", "w_mix_out", "ln1_g", "ln1_b", "w_up", "ffn_conv_w",
           "ffn_conv_b", "w_down", "ln2_g", "ln2_b")
SMALL_ROWS, SMALL_COLS = 16, 2 * D_FF


def _by_device(full, how):
    r, c = full.shape
    if how == "rows":
        return full.reshape(N_DEV, r // N_DEV, c)
    return full.reshape(r, N_DEV, c // N_DEV).transpose(1, 0, 2)


def _from_devices(slots, how):
    _, r, c = slots.shape
    if how == "rows":
        return slots.reshape(N_DEV * r, c)
    return slots.transpose(1, 0, 2).reshape(r, N_DEV * c)


def kernel(x, rel_bias, w_in, b_gate, ssm_conv_w, ssm_conv_b, ssm_dt_bias, ssm_a_log, ssm_d, ssm_norm_w, attn_sinks, w_branch_ssm, w_branch_attn, w_mix_out, ln1_g, ln1_b, w_up, ffn_conv_w, ffn_conv_b, w_down, ln2_g, ln2_b, loss_target, m_rel_bias, m_w_in, m_b_gate, m_ssm_conv_w, m_ssm_conv_b, m_ssm_dt_bias, m_ssm_a_log, m_ssm_d, m_ssm_norm_w, m_attn_sinks, m_w_branch_ssm, m_w_branch_attn, m_w_mix_out, m_ln1_g, m_ln1_b, m_w_up, m_ffn_conv_w, m_ffn_conv_b, m_w_down, m_ln2_g, m_ln2_b, v_rel_bias, v_w_in, v_b_gate, v_ssm_conv_w, v_ssm_conv_b, v_ssm_dt_bias, v_ssm_a_log, v_ssm_d, v_ssm_norm_w, v_attn_sinks, v_w_branch_ssm, v_w_branch_attn, v_w_mix_out, v_ln1_g, v_ln1_b, v_w_up, v_ffn_conv_w, v_ffn_conv_b, v_w_down, v_ln2_g, v_ln2_b):
    W = dict(zip(WEIGHTS, (rel_bias, w_in, b_gate, ssm_conv_w, ssm_conv_b, ssm_dt_bias, ssm_a_log, ssm_d, ssm_norm_w,
                           attn_sinks, w_branch_ssm, w_branch_attn, w_mix_out, ln1_g, ln1_b, w_up, ffn_conv_w,
                           ffn_conv_b, w_down, ln2_g, ln2_b)))
    M = dict(zip(WEIGHTS, (m_rel_bias, m_w_in, m_b_gate, m_ssm_conv_w, m_ssm_conv_b, m_ssm_dt_bias, m_ssm_a_log, m_ssm_d,
                           m_ssm_norm_w, m_attn_sinks, m_w_branch_ssm, m_w_branch_attn, m_w_mix_out, m_ln1_g, m_ln1_b,
                           m_w_up, m_ffn_conv_w, m_ffn_conv_b, m_w_down, m_ln2_g, m_ln2_b)))
    V = dict(zip(WEIGHTS, (v_rel_bias, v_w_in, v_b_gate, v_ssm_conv_w, v_ssm_conv_b, v_ssm_dt_bias, v_ssm_a_log, v_ssm_d,
                           v_ssm_norm_w, v_attn_sinks, v_w_branch_ssm, v_w_branch_attn, v_w_mix_out, v_ln1_g, v_ln1_b,
                           v_w_up, v_ffn_conv_w, v_ffn_conv_b, v_w_down, v_ln2_g, v_ln2_b)))
    shard2d = lambda a: a.reshape(a.shape[-2], a.shape[-1])

    (win_all,) = _all_gather([shard2d(w_in).astype(BF16)], "gather_w_in")
    main, tail = _split_w_in(_from_devices(win_all, "cols"))
    conv_all = _all_gather([shard2d(ssm_conv_w), shard2d(ffn_conv_w)], "gather_conv_weights")
    late_shards = [shard2d(W[n]).astype(BF16) for n in LATE]
    late = lambda landed: {SHORT[n]: _from_devices(a, SHARDED[n]) for n, a in zip(LATE, landed)}
    p = _prep_params(rel_bias, b_gate, _from_devices(conv_all[0], "cols"), ssm_conv_b, ssm_dt_bias, ssm_a_log, ssm_d,
                     ssm_norm_w, attn_sinks, ln1_g, ln1_b, _from_devices(conv_all[1], "cols"), ffn_conv_b, ln2_g, ln2_b)

    early_names = LATE + ("ssm_conv_w", "ffn_conv_w")
    early = lambda g: [_by_device(g[n], SHARDED[n]).astype(BF16 if n in LATE else F32) for n in early_names]
    sq, dproj, w, g, landed = _local_step(x[0], loss_target[0], {"in_main": main, "in_tail": tail}, p,
                                          (late_shards, late), early)
    landed = dict(zip(early_names, landed))
    g_w_in = _join_w_in(g.pop("in_main"), g.pop("in_tail"))
    dx, landed["w_in"] = _grad_x(dproj, w, exchange=[_by_device(g_w_in, "cols").astype(BF16)])
    loss = (0.5 / D_MODEL) * lax.psum(sq[0, 0], ("x", "y", "c"))
    grads, deltas, new_m, new_v = {}, {}, {}, {}
    for n in SHARDED:
        outs = _adamw(landed[n], shard2d(W[n]), shard2d(M[n]), shard2d(V[n]), "adamw_" + n)
        grads[n], deltas[n], new_m[n], new_v[n] = (a.reshape(W[n].shape) for a in outs)

    row = lambda a: a.reshape(1, -1)
    packed = jnp.concatenate([jnp.pad(row(g[n]), ((0, 0), (0, SMALL_COLS - g[n].size))) for n in SMALL]
                             + [jnp.zeros((SMALL_ROWS - len(SMALL), SMALL_COLS), F32)], axis=0)
    (small_all,) = _all_gather([packed], "gather_small_grads")
    outs = _small_update(small_all, *[[row(src[n]) for n in SMALL] for src in (W, M, V)])
    for i, n in enumerate(SMALL):
        grads[n], deltas[n], new_m[n], new_v[n] = (outs[j * len(SMALL) + i].reshape(W[n].shape) for j in range(4))

    return (loss, dx[None], *[grads[n] for n in WEIGHTS], *[deltas[n] for n in WEIGHTS],
            *[new_m[n] for n in WEIGHTS], *[new_v[n] for n in WEIGHTS])
```

```python
import functools
import math

import jax
import jax.numpy as jnp
from jax import lax
from jax.experimental import pallas as pl
from jax.experimental.pallas import tpu as pltpu

F32, BF16 = jnp.float32, jnp.bfloat16
HIGHEST = lax.Precision.HIGHEST
MESH_ID = pl.DeviceIdType.MESH

N_DEV = 8
D_MODEL = 1024
SSM_INNER = 2048
SSM_HEADS = 32
SSM_HEADDIM = 64
SSMD = SSM_HEADDIM
SSM_GROUPS = 4
SSM_GROUP_COLS = SSM_INNER // SSM_GROUPS
SSM_STATE = 128
SSM_CONV = 4
CHUNK = 128
XBC_COLS = SSM_INNER + 2 * SSM_GROUPS * SSM_STATE
B_OFF = SSM_INNER
C_OFF = SSM_INNER + SSM_GROUPS * SSM_STATE
ATTN_HEADS = 16
ATTN_KV = 2
ATTN_GROUP = 8
HEADDIM = 64
WINDOW = 128
REL_BUCKETS = 32
REL_MAX_DIST = 128
D_FF = 2816
FFN_CONV = 3
ALPHA = 2.0 ** 0.25
LN_EPS = 1e-5
RMS_EPS = 1e-5
IN_COLS = 8480
Z_OFF, XBC_OFF, Q_OFF, GATE_OFF, MAIN_COLS = 0, 2048, 5120, 6144, 8192
K_OFF, V_OFF, DT_OFF, TAIL_COLS = 0, 128, 256, 384
O_Z, O_XBC, O_DT, O_Q, O_K, O_V, O_GATE = 0, 2048, 5120, 5152, 6176, 6304, 6432

ADAM_LR, ADAM_B1, ADAM_B2, ADAM_EPS, ADAM_WD, ADAM_STEP = 0.001, 0.9, 0.999, 1e-08, 0.01, 10
NEG = -1e30
HALO = 8
VMEM_LIMIT = 56 * 1024 * 1024


def _cp(sem):
    return pltpu.CompilerParams(dimension_semantics=sem, vmem_limit_bytes=VMEM_LIMIT)


def _const_spec(shape):
    nd = len(shape)
    return pl.BlockSpec(shape, lambda *_: (0,) * nd)


def _sigmoid(x):
    return 0.5 * jnp.tanh(0.5 * x) + 0.5


def _softplus(x):
    return jnp.maximum(x, 0.0) + jnp.log1p(jnp.exp(-jnp.abs(x)))


def _dot(a, b, dims=(((1,), (0,)), ((), ())), precision=None):
    return lax.dot_general(a, b, dims, preferred_element_type=F32, precision=precision)


NN = (((1,), (0,)), ((), ()))
NT = (((1,), (1,)), ((), ()))
TN = (((0,), (0,)), ((), ()))


def _mesh_pos():
    return lax.axis_index("x"), lax.axis_index("y"), lax.axis_index("c")


PEERS = N_DEV - 1


def _exchange_phases(in_refs, out_refs, send_sems, recv_sems, local_sems):
    def copies():
        x, y, c = _mesh_pos()
        me = 4 * x + 2 * y + c
        cps = []
        for b, (in_ref, out_ref) in enumerate(zip(in_refs, out_refs)):
            cps.append(pltpu.make_async_copy(in_ref.at[me], out_ref.at[me], local_sems.at[b]))
            for r in range(1, N_DEV):
                px = 1 - x if r & 4 else x
                py = 1 - y if r & 2 else y
                pc = 1 - c if r & 1 else c
                cps.append(pltpu.make_async_remote_copy(
                    src_ref=in_ref.at[4 * px + 2 * py + pc], dst_ref=out_ref.at[me],
                    send_sem=send_sems.at[b * PEERS + r - 1], recv_sem=recv_sems.at[b * PEERS + r - 1],
                    device_id=(px, py, pc), device_id_type=MESH_ID))
        return cps

    def start():
        for cp in copies():
            cp.start()

    def finish():
        for cp in copies():
            cp.wait()

    return [start, finish]


def _gather_phases(x_refs, out_refs, send_sems, recv_sems, local_sems):
    def parts(which):
        x, y, c = _mesh_pos()
        me, sibling = (x, y, c), (x, y, 1 - c)
        chips = [(1 - x, y), (x, 1 - y), (1 - x, 1 - y)]
        found = []
        for b, (x_ref, out_ref) in enumerate(zip(x_refs, out_refs)):
            def slot(px, py, pc):
                return out_ref.at[4 * px + 2 * py + pc]

            def copy(k, block, to, src=None):
                return pltpu.make_async_remote_copy(
                    src_ref=slot(*block) if src is None else src, dst_ref=slot(*block),
                    send_sem=send_sems.at[b * PEERS + k], recv_sem=recv_sems.at[b * PEERS + k],
                    device_id=to, device_id_type=MESH_ID)

            if which == "mine":
                found.append(pltpu.make_async_copy(x_ref, slot(*me), local_sems.at[b]))
            elif which == "first":
                found.append(copy(0, me, sibling, src=x_ref))
                found += [copy(1 + j, me, (*chip, c), src=x_ref) for j, chip in enumerate(chips)]
            elif which == "passed":
                found += [copy(4 + j, (*chip, c), sibling) for j, chip in enumerate(chips)]
            elif which == "arrived":
                found += [copy(1 + j, (*chip, c), me) for j, chip in enumerate(chips)]
            else:
                found.append(copy(0, sibling, me))
                found += [copy(4 + j, (*chip, 1 - c), me) for j, chip in enumerate(chips)]
        return found

    def start():
        for cp in parts("mine") + parts("first"):
            cp.start()

    def forward():
        for a, p in zip(parts("arrived"), parts("passed")):
            a.wait_recv()
            p.start()

    def finish():
        for cp in parts("late"):
            cp.wait_recv()
        for cp in parts("first") + parts("passed"):
            cp.wait_send()
        for cp in parts("mine"):
            cp.wait()

    return [start, forward, finish]


COMM = {"exchange": _exchange_phases, "gather": _gather_phases}


def _comm_sems(nb):
    return [pltpu.SemaphoreType.DMA((nb * PEERS,)), pltpu.SemaphoreType.DMA((nb * PEERS,)), pltpu.SemaphoreType.DMA((nb,))]


def _matmul(a, b, mode, out_dtype, name, tm=512, tn=1024, tk=1024, addend=None, addend_scale=1.0, comm=None):
    bufs = [] if comm is None else list(comm[1])
    nb = len(bufs)
    halves = b.ndim == 3
    if mode == "nn":
        (M, K), (K2, N) = a.shape, b.shape
    elif mode == "nt":
        (M, K), (N, K2) = a.shape, b.shape
    elif halves:
        (K, M), (K2, N) = a.shape, (b.shape[1], 2 * b.shape[2])
    else:
        (K, M), (K2, N) = a.shape, b.shape
    assert K == K2, (a.shape, b.shape, mode)
    tm, tn, tk = min(tm, M), min(tn, N), min(tk, K)
    assert M % tm == 0 and N % tn == 0 and K % tk == 0, (M, N, K, tm, tn, tk)
    nk = K // tk
    dims = {"nn": NN, "nt": NT, "tn": TN}[mode]
    a_spec = pl.BlockSpec((tk, tm), lambda i, j, k: (k, i)) if mode == "tn" else pl.BlockSpec((tm, tk), lambda i, j, k: (i, k))
    b_spec = pl.BlockSpec((tn, tk), lambda i, j, k: (j, k)) if mode == "nt" else pl.BlockSpec((tk, tn), lambda i, j, k: (k, j))
    o_spec = pl.BlockSpec((tm, tn), lambda i, j, k: (i, j))

    ni, nj = M // tm, N // tn
    if halves:
        assert mode == "tn" and nj % 2 == 0
        b_spec = pl.BlockSpec((None, tk, tn), lambda i, j, k: (j // (nj // 2), k, j % (nj // 2)))

    def body(*refs):
        refs = list(refs)
        a_ref, b_ref = refs[:2]
        c_ref = refs[2] if addend is not None else None
        n_in = 2 + (addend is not None) + nb
        o_ref, acc = refs[n_in], refs[n_in + 1 + nb]
        i, j, k = pl.program_id(0), pl.program_id(1), pl.program_id(2)
        step = (i * nj + j) * nk + k
        if comm is not None:
            phases = COMM[comm[0]](refs[n_in - nb:n_in], refs[n_in + 1:n_in + 1 + nb], *refs[n_in + 2 + nb:])
            at = [(ni * nj * nk - 1) * p // (len(phases) - 1) for p in range(len(phases))]
            for when, phase in zip(at[:-1], phases[:-1]):
                pl.when(step == when)(phase)

        d = _dot(a_ref[...].astype(BF16), b_ref[...].astype(BF16), dims)

        def finish(r):
            if addend is not None:
                r = r + addend_scale * c_ref[...].astype(F32)
            o_ref[...] = r.astype(out_dtype)

        if nk == 1:
            finish(d)
        else:
            @pl.when(k == 0)
            def _():
                acc[...] = d

            @pl.when(jnp.logical_and(k > 0, k < nk - 1))
            def _():
                acc[...] += d

            @pl.when(k == nk - 1)
            def _():
                finish(acc[...] + d)

        if comm is not None:
            pl.when(step == at[-1])(phases[-1])

    in_specs = [a_spec, b_spec] + ([o_spec] if addend is not None else [])
    args = (a, b) + ((addend,) if addend is not None else ())
    out_specs, out_shape = o_spec, jax.ShapeDtypeStruct((M, N), out_dtype)
    scratch = [pltpu.VMEM((tm, tn), F32)]
    sem = ("parallel", "parallel", "arbitrary")
    if comm is not None:
        any_spec = pl.BlockSpec(memory_space=pl.ANY)
        in_specs, args = in_specs + [any_spec] * nb, args + tuple(bufs)
        landed = [x.shape if comm[0] == "exchange" else (N_DEV,) + x.shape for x in bufs]
        out_specs = [o_spec] + [any_spec] * nb
        out_shape = [out_shape] + [jax.ShapeDtypeStruct(s, x.dtype) for s, x in zip(landed, bufs)]
        scratch += _comm_sems(nb)
        sem = ("arbitrary", "arbitrary", "arbitrary")
    return pl.pallas_call(
        body, name=name, grid=(ni, nj, nk), in_specs=in_specs, out_specs=out_specs, out_shape=out_shape,
        scratch_shapes=scratch, compiler_params=_cp(sem))(*args)


def _taps(w_ref, K, tc):
    return [jnp.broadcast_to(w_ref[k:k + 1, :], (HALO, tc)) for k in range(K)]


def _conv_silu_fwd(pre, pre_col_off, C, w, b, K, name, tr=1024, tc=512):
    T = pre.shape[0]
    tr, tc = min(tr, T), min(tc, C)
    assert T % tr == 0 and C % tc == 0 and pre_col_off % tc == 0
    joff = pre_col_off // tc
    hb = tr // HALO

    def body(x_ref, xp_ref, w_ref, b_ref, o_ref, d_ref, head):
        i = pl.program_id(1)
        head[0:HALO, :] = jnp.where(i > 0, xp_ref[...], 0.0)
        head[HALO:, :] = x_ref[0:HALO, :]
        wk = _taps(w_ref, K, tc)
        bias = jnp.broadcast_to(b_ref[...], (HALO, tc))
        for r in range(tr // HALO):
            lo = r * HALO
            co = bias + wk[K - 1] * x_ref[lo:lo + HALO, :]
            for k in range(K - 1):
                s = K - 1 - k
                co = co + wk[k] * (head[HALO - s:2 * HALO - s, :] if r == 0 else x_ref[lo - s:lo + HALO - s, :])
            sg = _sigmoid(co)
            y = co * sg
            o_ref[lo:lo + HALO, :] = y
            d_ref[lo:lo + HALO, :] = sg + y * (1.0 - sg)

    out = pl.BlockSpec((tr, tc), lambda j, i: (i, j))
    return pl.pallas_call(
        body, name=name, grid=(C // tc, T // tr),
        in_specs=[pl.BlockSpec((tr, tc), lambda j, i: (i, joff + j)),
                  pl.BlockSpec((HALO, tc), lambda j, i: (jnp.maximum(i * hb - 1, 0), joff + j)),
                  pl.BlockSpec((K, tc), lambda j, i: (0, j)),
                  pl.BlockSpec((1, tc), lambda j, i: (0, j))],
        out_specs=[out, out], out_shape=[jax.ShapeDtypeStruct((T, C), F32)] * 2,
        scratch_shapes=[pltpu.VMEM((2 * HALO, tc), F32)],
        compiler_params=_cp(("parallel", "arbitrary")))(pre, pre, w, b)


def _conv_bwd(dout, pre, pre_col_off, C, w, K, dst, dst_col_off, name, tr=1024, tc=512):
    T = pre.shape[0]
    tr, tc = min(tr, T), min(tc, C)
    assert T % tr == 0 and C % tc == 0 and pre_col_off % tc == 0 and dst_col_off % tc == 0
    joff, doff = pre_col_off // tc, dst_col_off // tc
    hb = tr // HALO
    nt = T // tr
    n = tr // HALO
    last_hblock = T // HALO - 1

    def body(g_ref, gn_ref, x_ref, w_ref, *rest):
        o_ref, dw_ref, db_ref, edge = rest[-4:]
        i = pl.program_id(1)
        wk = _taps(w_ref, K, tc)
        edge[0:HALO, :] = g_ref[tr - HALO:tr, :]
        edge[HALO:, :] = jnp.where(i < nt - 1, gn_ref[...], 0.0)
        acc_w = [jnp.zeros((HALO, tc), F32) for _ in range(K)]
        acc_b = jnp.zeros((HALO, tc), F32)
        for r in range(n):
            lo = r * HALO
            x = x_ref[lo:lo + HALO, :]
            dpre = None
            for s in range(K):
                gs = edge[s:HALO + s, :] if (r == n - 1 and s > 0) else g_ref[lo + s:lo + HALO + s, :]
                dpre = wk[K - 1 - s] * gs if dpre is None else dpre + wk[K - 1 - s] * gs
                acc_w[K - 1 - s] = acc_w[K - 1 - s] + gs * x
                if s == 0:
                    acc_b = acc_b + gs
            o_ref[lo:lo + HALO, :] = dpre.astype(o_ref.dtype)

        @pl.when(i == 0)
        def _():
            dw_ref[...] = jnp.zeros_like(dw_ref)
            db_ref[...] = jnp.zeros_like(db_ref)

        db_ref[...] += jnp.sum(acc_b, axis=0, keepdims=True)
        dw_ref[...] += jnp.concatenate([jnp.sum(a, axis=0, keepdims=True) for a in acc_w], axis=0)

    tile = lambda off: pl.BlockSpec((tr, tc), lambda j, i: (i, off + j))
    nxt_row = lambda i: jnp.minimum((i + 1) * hb, last_hblock)
    if dout.ndim == 3:
        nh = C // 2 // tc
        g_specs = [pl.BlockSpec((None, tr, tc), lambda j, i: (j // nh, i, j % nh)),
                   pl.BlockSpec((None, HALO, tc), lambda j, i: (j // nh, nxt_row(i), j % nh))]
    else:
        g_specs = [tile(0), pl.BlockSpec((HALO, tc), lambda j, i: (nxt_row(i), j))]
    in_specs = g_specs + [tile(joff), pl.BlockSpec((K, tc), lambda j, i: (0, j))]
    args = (dout, dout, pre, w)
    if isinstance(dst, jax.ShapeDtypeStruct):
        aliases = {}
    else:
        in_specs.append(pl.BlockSpec(memory_space=pl.ANY))
        args += (dst,)
        aliases = {4: 0}
    return pl.pallas_call(
        body, name=name, grid=(C // tc, nt), in_specs=in_specs,
        out_specs=[tile(doff), pl.BlockSpec((K, tc), lambda j, i: (0, j)), pl.BlockSpec((1, tc), lambda j, i: (0, j))],
        out_shape=[jax.ShapeDtypeStruct(dst.shape, dst.dtype), jax.ShapeDtypeStruct((K, C), F32),
                   jax.ShapeDtypeStruct((1, C), F32)],
        scratch_shapes=[pltpu.VMEM((2 * HALO, tc), F32)],
        input_output_aliases=aliases,
        compiler_params=_cp(("parallel", "arbitrary")))(*args)


PAIR = 2 * SSMD
PAIRS_PER_GROUP = SSM_GROUP_COLS // PAIR


def _dot3(x, onehot):
    h1 = x.astype(BF16)
    r = x - h1.astype(F32)
    h2 = r.astype(BF16)
    h3 = (r - h2.astype(F32)).astype(BF16)
    return _dot(h1, onehot) + _dot(h2, onehot) + _dot(h3, onehot)


def _chunk_rows(dt_raw, dtb_col, alog_col):
    row = lax.broadcasted_iota(jnp.int32, (CHUNK, CHUNK), 0)
    col = lax.broadcasted_iota(jnp.int32, (CHUNK, CHUNK), 1)
    dt_rawT = dt_raw.T
    dtT = _softplus(dt_rawT + dtb_col)
    a_col = -jnp.exp(alog_col)
    acsT = _dot3(dtT * a_col, (row <= col).astype(BF16))
    return dt_rawT, dtT, a_col, acsT, row, col


def _block_diag(x, left):
    return jnp.concatenate([jnp.where(left, x, 0.0), jnp.where(left, 0.0, x)], axis=0).astype(BF16)


def _lane_bcast(v, h):
    return jnp.broadcast_to(v[:, h:h + 1], (CHUNK, CHUNK))


def _ssd_fwd(xbc, proj_main, proj_tail, dtb_col, alog_col, d_exp, norm_w):
    T = xbc.shape[0]
    nc = T // CHUNK

    def body(xbc_ref, dt_ref, z_ref, dtb_ref, alog_ref, d_ref, nw_ref, y_ref, ypre_ref, hs_ref, H):
        c = pl.program_id(0)

        @pl.when(c == 0)
        def _():
            H[...] = jnp.zeros_like(H)

        hs_ref[0] = H[...]
        _, dtT, _, acsT, row, col = _chunk_rows(dt_ref[:, 0:SSM_HEADS], dtb_ref[...], alog_ref[...])
        tril, left = row >= col, col < SSMD
        acs = acsT.T
        w = (dtT * jnp.exp(acsT[:, CHUNK - 1:CHUNK] - acsT)).T
        cd = jnp.exp(acs[CHUNK - 1:CHUNK, :])
        for g in range(SSM_GROUPS):
            gs = slice(g * SSM_GROUP_COLS, (g + 1) * SSM_GROUP_COLS)
            Bb = xbc_ref[:, B_OFF + g * SSM_STATE:B_OFF + (g + 1) * SSM_STATE].astype(BF16)
            Cb = xbc_ref[:, C_OFF + g * SSM_STATE:C_OFF + (g + 1) * SSM_STATE].astype(BF16)
            Hg = H[:, gs]
            CH = _dot(Cb, Hg.astype(BF16))
            CB = _dot(Cb, Bb, NT)
            ys, xws = [], []
            for kk in range(PAIRS_PER_GROUP):
                k = g * PAIRS_PER_GROUP + kk
                xs_p = xbc_ref[:, k * PAIR:(k + 1) * PAIR]
                mps, ecols, wcols = [], [], []
                for j in range(2):
                    h = 2 * k + j
                    colb = _lane_bcast(acs, h)
                    L = jnp.exp(jnp.where(tril, colb - acsT[h:h + 1, :], -jnp.inf))
                    mps.append((CB * L * dtT[h:h + 1, :]).astype(BF16))
                    ecols.append(jnp.exp(colb))
                    wcols.append(_lane_bcast(w, h))
                yd = _dot(jnp.concatenate(mps, axis=1), _block_diag(xs_p, left))
                ys.append(yd + CH[:, kk * PAIR:(kk + 1) * PAIR] * jnp.where(left, ecols[0], ecols[1]))
                xws.append((xs_p * jnp.where(left, wcols[0], wcols[1])).astype(BF16))
            cd_e = jnp.concatenate([jnp.broadcast_to(cd[:, g * 8 + e:g * 8 + e + 1], (1, SSMD)) for e in range(8)], axis=1)
            H[:, gs] = Hg * cd_e + _dot(Bb, jnp.concatenate(xws, axis=1), TN)
            ypre = jnp.concatenate(ys, axis=1) + xbc_ref[:, gs] * d_ref[:, gs]
            ypre_ref[:, gs] = ypre
            z = z_ref[:, gs]
            yg = ypre * (z * _sigmoid(z))
            r = lax.rsqrt(jnp.mean(yg * yg, axis=1, keepdims=True) + RMS_EPS)
            y_ref[:, gs] = (yg * r * nw_ref[:, gs]).astype(BF16)

    vec = lambda n: _const_spec((1, n))
    colv = _const_spec((SSM_HEADS, 1))
    return pl.pallas_call(
        body, name="ssd_fwd", grid=(nc,),
        in_specs=[pl.BlockSpec((CHUNK, XBC_COLS), lambda c: (c, 0)),
                  pl.BlockSpec((CHUNK, 128), lambda c: (c, DT_OFF // 128)),
                  pl.BlockSpec((CHUNK, SSM_INNER), lambda c: (c, Z_OFF // SSM_INNER)),
                  colv, colv, vec(SSM_INNER), vec(SSM_INNER)],
        out_specs=[pl.BlockSpec((CHUNK, SSM_INNER), lambda c: (c, 0)),
                   pl.BlockSpec((CHUNK, SSM_INNER), lambda c: (c, 0)),
                   pl.BlockSpec((1, SSM_STATE, SSM_INNER), lambda c: (c, 0, 0))],
        out_shape=[jax.ShapeDtypeStruct((T, SSM_INNER), BF16), jax.ShapeDtypeStruct((T, SSM_INNER), F32),
                   jax.ShapeDtypeStruct((nc, SSM_STATE, SSM_INNER), F32)],
        scratch_shapes=[pltpu.VMEM((SSM_STATE, SSM_INNER), F32)],
        compiler_params=_cp(("arbitrary",)))(xbc, proj_tail, proj_main, dtb_col, alog_col, d_exp, norm_w)


def _ssd_bwd(dyo, ypre, xbc, dsil, hs, proj_main, proj_tail, dtb_col, alog_col, d_exp, norm_w, ehead_t, dmain, dtail):
    T = xbc.shape[0]
    nc = T // CHUNK

    def body(dyo_ref, ypre_ref, xbc_ref, dsil_ref, hs_ref, dt_ref, z_ref, dtb_ref, alog_ref, d_ref, nw_ref, eh_ref,
             dmain_in, dtail_in, dz_ref, ddt_ref, dxbc_ref, dnw_ref, dd_ref, dalog_ref, ddtb_ref, G):
        del dmain_in, dtail_in
        c = pl.program_id(0)

        @pl.when(c == 0)
        def _():
            G[...] = jnp.zeros_like(G)
            dnw_ref[...] = jnp.zeros_like(dnw_ref)
            dd_ref[...] = jnp.zeros_like(dd_ref)
            dalog_ref[...] = jnp.zeros_like(dalog_ref)
            ddtb_ref[...] = jnp.zeros_like(ddtb_ref)

        dt_rawT, dtT, a_col, acsT, row, col = _chunk_rows(dt_ref[:, 0:SSM_HEADS], dtb_ref[...], alog_ref[...])
        tril, triu, left = row >= col, col >= row, col < SSMD
        acs = acsT.T
        dt = dtT.T
        lastT = acsT[:, CHUNK - 1:CHUNK]
        dstT = jnp.exp(lastT - acsT)
        wT = dtT * dstT
        cd = jnp.exp(acs[CHUNK - 1:CHUNK, :])
        ddt_rows, rs_rows, deo_rows, dw_rows = [], [], [], []
        dd_cols, gh_cols, dnw_cols = [], [], []
        for g in range(SSM_GROUPS):
            gs = slice(g * SSM_GROUP_COLS, (g + 1) * SSM_GROUP_COLS)
            z = z_ref[:, gs]
            sz = _sigmoid(z)
            silu_z = z * sz
            ypre = ypre_ref[:, gs]
            yg = ypre * silu_z
            r = lax.rsqrt(jnp.mean(yg * yg, axis=1, keepdims=True) + RMS_EPS)
            ygn = yg * r
            dyo = dyo_ref[:, gs]
            dyn = dyo * nw_ref[:, gs]
            dnw_cols.append(jnp.sum(dyo * ygn, axis=0, keepdims=True))
            dyg = r * (dyn - ygn * jnp.mean(dyn * ygn, axis=1, keepdims=True))
            dz_ref[:, gs] = (dyg * ypre * (sz * (1.0 + z * (1.0 - sz)))).astype(dz_ref.dtype)
            dY = dyg * silu_z
            xs = xbc_ref[:, gs]
            dd_cols.append(jnp.sum(dY * xs, axis=0, keepdims=True))
            Bf = xbc_ref[:, B_OFF + g * SSM_STATE:B_OFF + (g + 1) * SSM_STATE]
            Cf = xbc_ref[:, C_OFF + g * SSM_STATE:C_OFF + (g + 1) * SSM_STATE]
            Bb, Cb = Bf.astype(BF16), Cf.astype(BF16)
            BT, CT = Bf.T, Cf.T
            CB = _dot(Cb, Bb, NT)
            CBT = _dot(Bb, Cb, NT)
            Hg = hs_ref[0, :, gs]
            Gg = G[:, gs]
            gh_cols.append(jnp.sum(Gg * Hg, axis=0, keepdims=True))
            dCB = jnp.zeros((CHUNK, CHUNK), F32)
            dxs_d, dyes, xws, wsels = [], [], [], []
            for kk in range(PAIRS_PER_GROUP):
                k = g * PAIRS_PER_GROUP + kk
                ps = slice(kk * PAIR, (kk + 1) * PAIR)
                xs_p, dY_p = xs[:, ps], dY[:, ps]
                Ls, LTs, dtcols, ecols, wcols = [], [], [], [], []
                for j in range(2):
                    h = 2 * k + j
                    colb = _lane_bcast(acs, h)
                    seg = colb - acsT[h:h + 1, :]
                    Ls.append(jnp.exp(jnp.where(tril, seg, -jnp.inf)))
                    LTs.append(jnp.exp(jnp.where(triu, -seg, -jnp.inf)))
                    dtcol = _lane_bcast(dt, h)
                    dtcols.append(dtcol)
                    ecols.append(jnp.exp(colb))
                    wcols.append(dtcol * jnp.exp(acs[CHUNK - 1:CHUNK, h:h + 1] - colb))
                wsel = jnp.where(left, wcols[0], wcols[1])
                dYe_p = dY_p * jnp.where(left, ecols[0], ecols[1])
                bdx = _block_diag(xs_p, left)
                bddy = _block_diag(dY_p, left)
                dMx2 = _dot(dY_p.astype(BF16), bdx, NT)
                dMxT2 = _dot(xs_p.astype(BF16), bddy, NT)
                Q1 = _dot(Hg[:, ps].astype(BF16), _block_diag(dYe_p, left), NT)
                Q2 = _dot(Gg[:, ps].astype(BF16), bdx, NT)
                mts = []
                for j in range(2):
                    h = 2 * k + j
                    js = slice(j * CHUNK, (j + 1) * CHUNK)
                    dMx = dMx2[:, js]
                    A = CB * Ls[j]
                    AT = CBT * LTs[j]
                    ddt_rows.append(jnp.sum(A * dMx, axis=0, keepdims=True))
                    ATd = AT * dtcols[j]
                    rs_rows.append(jnp.sum(ATd * dMxT2[:, js], axis=0, keepdims=True))
                    dCB = dCB + dMx * Ls[j] * dtT[h:h + 1, :]
                    mts.append(ATd.astype(BF16))
                    deo_rows.append(jnp.sum(CT * Q1[:, js], axis=0, keepdims=True))
                    dw_rows.append(jnp.sum(BT * Q2[:, js], axis=0, keepdims=True))
                dxs_d.append(_dot(jnp.concatenate(mts, axis=1), bddy))
                dyes.append(dYe_p.astype(BF16))
                xws.append((xs_p * wsel).astype(BF16))
                wsels.append(wsel)
            dYe_g = jnp.concatenate(dyes, axis=1)
            xw_g = jnp.concatenate(xws, axis=1)
            Hgb, Ggb, dCBb = Hg.astype(BF16), Gg.astype(BF16), dCB.astype(BF16)
            cs = slice(C_OFF + g * SSM_STATE, C_OFF + (g + 1) * SSM_STATE)
            bs = slice(B_OFF + g * SSM_STATE, B_OFF + (g + 1) * SSM_STATE)
            dxbc_ref[:, cs] = (_dot(dYe_g, Hgb, NT) + _dot(dCBb, Bb)) * dsil_ref[:, cs]
            dxbc_ref[:, bs] = (_dot(xw_g, Ggb, NT) + _dot(dCBb, Cb, TN)) * dsil_ref[:, bs]
            BG = _dot(Bb, Ggb)
            dxbc_ref[:, gs] = (jnp.concatenate(dxs_d, axis=1) + BG * jnp.concatenate(wsels, axis=1)
                               + dY * d_ref[:, gs]) * dsil_ref[:, gs]
            cd_e = jnp.concatenate([jnp.broadcast_to(cd[:, g * 8 + e:g * 8 + e + 1], (1, SSMD)) for e in range(8)], axis=1)
            G[:, gs] = Gg * cd_e + _dot(Cb, dYe_g, TN)
        dnw_ref[...] += jnp.concatenate(dnw_cols, axis=1)
        eh = eh_ref[...]
        dd_ref[...] += jnp.sum(eh * jnp.concatenate(dd_cols, axis=1), axis=1, keepdims=True)
        dcd = jnp.sum(eh * jnp.concatenate(gh_cols, axis=1), axis=1, keepdims=True)
        DDT = jnp.concatenate(ddt_rows, axis=0)
        DW = jnp.concatenate(dw_rows, axis=0)
        DWw = DW * wT
        dacsT = jnp.concatenate(rs_rows, axis=0) - DDT * dtT + jnp.concatenate(deo_rows, axis=0) - DWw
        end = jnp.sum(DWw, axis=1, keepdims=True) + dcd * jnp.exp(lastT)
        lane = lax.broadcasted_iota(jnp.int32, (SSM_HEADS, CHUNK), 1)
        dacsT = dacsT + jnp.where(lane == CHUNK - 1, end, 0.0)
        dadtT = _dot3(dacsT, tril.astype(BF16))
        ddtT = dadtT * a_col + DDT + DW * dstT
        dalog_ref[...] += jnp.sum(dadtT * dtT, axis=1, keepdims=True) * a_col
        ddt_rawT = ddtT * _sigmoid(dt_rawT + dtb_ref[...])
        ddtb_ref[...] += jnp.sum(ddt_rawT, axis=1, keepdims=True)
        ddt_ref[...] = jnp.concatenate([ddt_rawT.T, jnp.zeros((CHUNK, 128 - SSM_HEADS), F32)], axis=1).astype(ddt_ref.dtype)

    rev = lambda c: nc - 1 - c
    vec = lambda n: _const_spec((1, n))
    colv = _const_spec((SSM_HEADS, 1))
    any_spec = pl.BlockSpec(memory_space=pl.ANY)
    return pl.pallas_call(
        body, name="ssd_bwd", grid=(nc,),
        in_specs=[pl.BlockSpec((CHUNK, SSM_INNER), lambda c: (rev(c), 0)),
                  pl.BlockSpec((CHUNK, SSM_INNER), lambda c: (rev(c), 0)),
                  pl.BlockSpec((CHUNK, XBC_COLS), lambda c: (rev(c), 0)),
                  pl.BlockSpec((CHUNK, XBC_COLS), lambda c: (rev(c), 0)),
                  pl.BlockSpec((1, SSM_STATE, SSM_INNER), lambda c: (rev(c), 0, 0)),
                  pl.BlockSpec((CHUNK, 128), lambda c: (rev(c), DT_OFF // 128)),
                  pl.BlockSpec((CHUNK, SSM_INNER), lambda c: (rev(c), Z_OFF // SSM_INNER)),
                  colv, colv, vec(SSM_INNER), vec(SSM_INNER), _const_spec((SSM_HEADS, SSM_INNER)), any_spec, any_spec],
        out_specs=[pl.BlockSpec((CHUNK, SSM_INNER), lambda c: (rev(c), Z_OFF // SSM_INNER)),
                   pl.BlockSpec((CHUNK, 128), lambda c: (rev(c), DT_OFF // 128)),
                   pl.BlockSpec((CHUNK, XBC_COLS), lambda c: (rev(c), 0)),
                   vec(SSM_INNER), colv, colv, colv],
        out_shape=[jax.ShapeDtypeStruct(dmain.shape, dmain.dtype), jax.ShapeDtypeStruct(dtail.shape, dtail.dtype),
                   jax.ShapeDtypeStruct((T, XBC_COLS), F32), jax.ShapeDtypeStruct((1, SSM_INNER), F32),
                   jax.ShapeDtypeStruct((SSM_HEADS, 1), F32), jax.ShapeDtypeStruct((SSM_HEADS, 1), F32),
                   jax.ShapeDtypeStruct((SSM_HEADS, 1), F32)],
        scratch_shapes=[pltpu.VMEM((SSM_STATE, SSM_INNER), F32)],
        input_output_aliases={12: 0, 13: 1},
        compiler_params=_cp(("arbitrary",)))(dyo, ypre, xbc, dsil, hs, proj_tail, proj_main, dtb_col, alog_col, d_exp,
                                             norm_w, ehead_t, dmain, dtail)


def _rel_bucket(rel):
    n = jnp.maximum(rel, 0)
    max_exact = REL_BUCKETS // 2
    nf = jnp.maximum(n, 1).astype(F32)
    large = max_exact + (jnp.log(nf / max_exact) / math.log(REL_MAX_DIST / max_exact)
                         * (REL_BUCKETS - max_exact)).astype(jnp.int32)
    large = jnp.minimum(large, REL_BUCKETS - 1)
    return jnp.where(n < max_exact, n, large)


def _band_geometry():
    qi = jnp.arange(WINDOW)[:, None] + WINDOW
    kj = jnp.arange(2 * WINDOW)[None, :]
    rel = qi - kj
    return _rel_bucket(rel), (rel >= 0) & (rel < WINDOW)


def _attn_logits(kband, qh, bias_h, first):
    s = _dot(kband, qh, NT) * (HEADDIM ** -0.5) + bias_h
    rowk = lax.broadcasted_iota(jnp.int32, (2 * WINDOW, WINDOW), 0)
    return jnp.where(jnp.logical_and(first, rowk < WINDOW), NEG, s)


def _attn_fwd(proj_main, proj_tail, bias_tbl, sinks):
    T = proj_main.shape[0]
    nb = T // WINDOW

    def body(q_ref, kv_ref, kvp_ref, bias_ref, sink_ref, o_ref, lse_ref):
        i = pl.program_id(0)
        first = i == 0
        outs, lses = [], []
        for kvh in range(ATTN_KV):
            ks = slice(K_OFF + kvh * HEADDIM, K_OFF + (kvh + 1) * HEADDIM)
            vs = slice(V_OFF + kvh * HEADDIM, V_OFF + (kvh + 1) * HEADDIM)
            kband = jnp.concatenate([kvp_ref[:, ks], kv_ref[:, ks]], axis=0).astype(BF16)
            vband = jnp.concatenate([kvp_ref[:, vs], kv_ref[:, vs]], axis=0).astype(BF16)
            heads = range(kvh * ATTN_GROUP, (kvh + 1) * ATTN_GROUP)
            logits = [_attn_logits(kband, q_ref[:, h * HEADDIM:(h + 1) * HEADDIM].astype(BF16), bias_ref[h], first)
                      for h in heads]
            probs = []
            for h, s in zip(heads, logits):
                sink = sink_ref[:, h:h + 1]
                m = jnp.maximum(jnp.max(s, axis=0, keepdims=True), sink)
                p = jnp.exp(s - m)
                den = jnp.sum(p, axis=0, keepdims=True) + jnp.exp(sink - m)
                probs.append((p * (1.0 / den)).astype(BF16))
                lses.append(m + jnp.log(den))
            outs += [_dot(pt, vband, TN) for pt in probs]
        o_ref[...] = jnp.concatenate(outs, axis=1).astype(BF16)
        lse_ref[...] = jnp.concatenate(lses, axis=0)

    return pl.pallas_call(
        body, name="attn_fwd", grid=(nb,),
        in_specs=[pl.BlockSpec((WINDOW, D_MODEL), lambda i: (i, Q_OFF // D_MODEL)),
                  pl.BlockSpec((WINDOW, 256), lambda i: (i, 0)),
                  pl.BlockSpec((WINDOW, 256), lambda i: (jnp.maximum(i - 1, 0), 0)),
                  _const_spec((ATTN_HEADS, 2 * WINDOW, WINDOW)), _const_spec((1, ATTN_HEADS))],
        out_specs=[pl.BlockSpec((WINDOW, D_MODEL), lambda i: (i, 0)),
                   pl.BlockSpec((ATTN_HEADS, WINDOW), lambda i: (0, i))],
        out_shape=[jax.ShapeDtypeStruct((T, D_MODEL), BF16), jax.ShapeDtypeStruct((ATTN_HEADS, T), F32)],
        compiler_params=_cp(("arbitrary",)))(proj_main, proj_tail, proj_tail, bias_tbl, sinks)


def _attn_bwd(dy, lse, proj_main, proj_tail, bias_tbl, sinks, dmain):
    T = proj_main.shape[0]
    nb = T // WINDOW

    def body(dy_ref, lse_ref, q_ref, kv_ref, kvp_ref, bias_ref, sink_ref, dmain_in,
             dq_ref, dkv_ref, dbias_ref, dsink_ref, carry):
        del dmain_in
        i = pl.program_id(0)
        first = i == 0

        @pl.when(first)
        def _():
            carry[...] = jnp.zeros_like(carry)
            dbias_ref[...] = jnp.zeros_like(dbias_ref)
            dsink_ref[...] = jnp.zeros_like(dsink_ref)

        @pl.when(i < nb)
        def _():
            scale = HEADDIM ** -0.5
            dqs, dsinks, dks, dvs = [], [], [], []
            for kvh in range(ATTN_KV):
                ks = slice(K_OFF + kvh * HEADDIM, K_OFF + (kvh + 1) * HEADDIM)
                vs = slice(V_OFF + kvh * HEADDIM, V_OFF + (kvh + 1) * HEADDIM)
                kband = jnp.concatenate([kvp_ref[:, ks], kv_ref[:, ks]], axis=0).astype(BF16)
                vband = jnp.concatenate([kvp_ref[:, vs], kv_ref[:, vs]], axis=0).astype(BF16)
                heads = range(kvh * ATTN_GROUP, (kvh + 1) * ATTN_GROUP)
                qs = [q_ref[:, h * HEADDIM:(h + 1) * HEADDIM].astype(BF16) for h in heads]
                dos = [dy_ref[:, h * HEADDIM:(h + 1) * HEADDIM] for h in heads]
                logits = [_attn_logits(kband, qh, bias_ref[h], first) for h, qh in zip(heads, qs)]
                dps = [_dot(vband, do, NT) for do in dos]
                pbs, dsbs = [], []
                for h, s, dp in zip(heads, logits, dps):
                    lse_h = lse_ref[h:h + 1, :]
                    p = jnp.exp(s - lse_h)
                    delta = jnp.sum(p * dp, axis=0, keepdims=True)
                    ds = p * (dp - delta)
                    psink = jnp.exp(sink_ref[:, h:h + 1] - lse_h)
                    dsinks.append(-jnp.sum(psink * delta, axis=1, keepdims=True))
                    dbias_ref[h] += ds
                    pbs.append(p.astype(BF16))
                    dsbs.append((ds * scale).astype(BF16))
                dqs += [_dot(dsb, kband, TN) for dsb in dsbs]
                dks.append(_dot(jnp.concatenate(dsbs, axis=1), jnp.concatenate(qs, axis=0)))
                dvs.append(_dot(jnp.concatenate(pbs, axis=1), jnp.concatenate(dos, axis=0)))
            dq_ref[...] = jnp.concatenate(dqs, axis=1).astype(dq_ref.dtype)
            dsink_ref[...] += jnp.concatenate(dsinks, axis=1)
            dkv = jnp.concatenate(dks + dvs, axis=1)
            dkv_ref[...] = (carry[...] + dkv[0:WINDOW, :]).astype(dkv_ref.dtype)
            carry[...] = dkv[WINDOW:, :]

        @pl.when(i == nb)
        def _():
            dkv_ref[...] = carry[...].astype(dkv_ref.dtype)

    cur = lambda i: jnp.minimum(i, nb - 1)
    return pl.pallas_call(
        body, name="attn_bwd", grid=(nb + 1,),
        in_specs=[pl.BlockSpec((WINDOW, D_MODEL), lambda i: (cur(i), 0)),
                  pl.BlockSpec((ATTN_HEADS, WINDOW), lambda i: (0, cur(i))),
                  pl.BlockSpec((WINDOW, D_MODEL), lambda i: (cur(i), Q_OFF // D_MODEL)),
                  pl.BlockSpec((WINDOW, 256), lambda i: (cur(i), 0)),
                  pl.BlockSpec((WINDOW, 256), lambda i: (jnp.maximum(cur(i) - 1, 0), 0)),
                  _const_spec((ATTN_HEADS, 2 * WINDOW, WINDOW)), _const_spec((1, ATTN_HEADS)),
                  pl.BlockSpec(memory_space=pl.ANY)],
        out_specs=[pl.BlockSpec((WINDOW, D_MODEL), lambda i: (cur(i), Q_OFF // D_MODEL)),
                   pl.BlockSpec((WINDOW, 256), lambda i: (jnp.maximum(i - 1, 0), 0)),
                   _const_spec((ATTN_HEADS, 2 * WINDOW, WINDOW)), _const_spec((1, ATTN_HEADS))],
        out_shape=[jax.ShapeDtypeStruct(dmain.shape, dmain.dtype), jax.ShapeDtypeStruct((T, TAIL_COLS), BF16),
                   jax.ShapeDtypeStruct((ATTN_HEADS, 2 * WINDOW, WINDOW), F32),
                   jax.ShapeDtypeStruct((1, ATTN_HEADS), F32)],
        scratch_shapes=[pltpu.VMEM((WINDOW, 256), F32)],
        input_output_aliases={7: 0},
        compiler_params=_cp(("arbitrary",)))(dy, lse, proj_main, proj_tail, proj_tail, bias_tbl, sinks, dmain)


def _bias_table(rel_bias_t, onehot_t, mask):
    def body(rb_ref, oh_ref, m_ref, o_ref):
        o_ref[...] = _dot3(rb_ref[...], oh_ref[...]) + m_ref[...]

    flat = pl.pallas_call(body, name="bias_table",
                          out_shape=jax.ShapeDtypeStruct((ATTN_HEADS, 2 * WINDOW * WINDOW), F32))(rel_bias_t, onehot_t, mask)
    return flat.reshape(ATTN_HEADS, 2 * WINDOW, WINDOW)


def _rel_bias_grad(dbias, onehot):
    def body(d_ref, oh_ref, o_ref):
        o_ref[...] = _dot(d_ref[...], oh_ref[...], NN, HIGHEST)

    return pl.pallas_call(body, name="rel_bias_grad",
                          out_shape=jax.ShapeDtypeStruct((ATTN_HEADS, REL_BUCKETS), F32))(dbias, onehot)


def _ln_fwd(r, g, b):
    mu = jnp.mean(r, axis=1, keepdims=True)
    xc = r - mu
    rstd = lax.rsqrt(jnp.mean(xc * xc, axis=1, keepdims=True) + LN_EPS)
    xhat = xc * rstd
    return xhat * g + b, xhat, rstd


def _ln_bwd(dy, xhat, rstd, g):
    dxh = dy * g
    return rstd * (dxh - jnp.mean(dxh, axis=1, keepdims=True) - xhat * jnp.mean(dxh * xhat, axis=1, keepdims=True))


def _merge_fwd(y_ssm, y_attn, proj_main, b_gate, w_bs, w_ba, tm=512):
    T = y_ssm.shape[0]

    def body(ys_ref, ya_ref, gs_ref, ga_ref, bg_ref, wbs_ref, wba_ref, m_ref, bs_ref, ba_ref):
        bs = _dot(ys_ref[...], wbs_ref[...])
        ba = _dot(ya_ref[...], wba_ref[...])
        g_s = _sigmoid(gs_ref[...] + bg_ref[:, 0:D_MODEL])
        g_a = _sigmoid(ga_ref[...] + bg_ref[:, D_MODEL:])
        m_ref[...] = (g_s * bs + g_a * ba).astype(BF16)
        bs_ref[...] = bs
        ba_ref[...] = ba

    row = lambda w, off=0: pl.BlockSpec((tm, w), lambda i: (i, off))
    return pl.pallas_call(
        body, name="merge_fwd", grid=(T // tm,),
        in_specs=[row(SSM_INNER), row(D_MODEL), row(D_MODEL, GATE_OFF // D_MODEL), row(D_MODEL, GATE_OFF // D_MODEL + 1),
                  _const_spec((1, 2 * D_MODEL)), _const_spec((SSM_INNER, D_MODEL)), _const_spec((D_MODEL, D_MODEL))],
        out_specs=[row(D_MODEL), row(D_MODEL), row(D_MODEL)],
        out_shape=[jax.ShapeDtypeStruct((T, D_MODEL), BF16), jax.ShapeDtypeStruct((T, D_MODEL), F32),
                   jax.ShapeDtypeStruct((T, D_MODEL), F32)],
        compiler_params=_cp(("parallel",)))(y_ssm, y_attn, proj_main, proj_main, b_gate, w_bs, w_ba)


def _mix_ln1(merged, w_mo, x, g1, b1, tm=512):
    T = x.shape[0]

    def body(m_ref, w_ref, x_ref, g_ref, b_ref, r_ref, h_ref, hb_ref):
        r = ALPHA * x_ref[...] + _dot(m_ref[...], w_ref[...])
        r_ref[...] = r
        h = _ln_fwd(r, g_ref[...], b_ref[...])[0]
        h_ref[...] = h
        hb_ref[...] = h.astype(BF16)

    row = pl.BlockSpec((tm, D_MODEL), lambda i: (i, 0))
    return pl.pallas_call(
        body, name="mix_ln1", grid=(T // tm,),
        in_specs=[row, _const_spec((D_MODEL, D_MODEL)), row, _const_spec((1, D_MODEL)), _const_spec((1, D_MODEL))],
        out_specs=[row, row, row],
        out_shape=[jax.ShapeDtypeStruct((T, D_MODEL), F32), jax.ShapeDtypeStruct((T, D_MODEL), F32),
                   jax.ShapeDtypeStruct((T, D_MODEL), BF16)],
        compiler_params=_cp(("parallel",)))(merged, w_mo, x, g1, b1)


def _ffn_conv_glu(u_pre, w, b, tr=2048, tc=256):
    T = u_pre.shape[0]
    tr = min(tr, T)
    K = FFN_CONV
    nj = D_FF // tc
    hb = tr // HALO
    assert T % tr == 0 and D_FF % tc == 0

    def body(xg_ref, xgp_ref, xv_ref, xvp_ref, wg_ref, wv_ref, bg_ref, bv_ref, u_ref, a_ref, head_g, head_v):
        i = pl.program_id(1)
        halves = []
        for x_ref, xp_ref, w_ref, b_ref, head in ((xg_ref, xgp_ref, wg_ref, bg_ref, head_g),
                                                  (xv_ref, xvp_ref, wv_ref, bv_ref, head_v)):
            head[0:HALO, :] = jnp.where(i > 0, xp_ref[...], 0.0)
            head[HALO:, :] = x_ref[0:HALO, :]
            halves.append((x_ref, head, _taps(w_ref, K, tc), jnp.broadcast_to(b_ref[...], (HALO, tc))))

        def conv(half, r):
            x_ref, head, wk, bias = halves[half]
            lo = r * HALO
            acc = bias + wk[K - 1] * x_ref[lo:lo + HALO, :]
            for k in range(K - 1):
                s = K - 1 - k
                acc = acc + wk[k] * (head[HALO - s:2 * HALO - s, :] if r == 0 else x_ref[lo - s:lo + HALO - s, :])
            return acc

        for r2 in range(tr // (2 * HALO)):
            acts = []
            for r in (2 * r2, 2 * r2 + 1):
                lo = r * HALO
                ug, uv = conv(0, r), conv(1, r)
                u_ref[0, lo:lo + HALO, :] = ug
                u_ref[1, lo:lo + HALO, :] = uv
                acts.append(ug * _sigmoid(ug) * uv)
            a_ref[2 * r2 * HALO:(2 * r2 + 2) * HALO, :] = jnp.concatenate(acts, axis=0).astype(BF16)

    tile = lambda off: pl.BlockSpec((tr, tc), lambda j, i: (i, off + j))
    prev = lambda off: pl.BlockSpec((HALO, tc), lambda j, i: (jnp.maximum(i * hb - 1, 0), off + j))
    row = lambda rows, off: pl.BlockSpec((rows, tc), lambda j, i: (0, off + j))
    return pl.pallas_call(
        body, name="ffn_conv_glu", grid=(nj, T // tr),
        in_specs=[tile(0), prev(0), tile(nj), prev(nj), row(K, 0), row(K, nj), row(1, 0), row(1, nj)],
        out_specs=[pl.BlockSpec((2, tr, tc), lambda j, i: (0, i, j)), pl.BlockSpec((tr, tc), lambda j, i: (i, j))],
        out_shape=[jax.ShapeDtypeStruct((2, T, D_FF), F32), jax.ShapeDtypeStruct((T, D_FF), BF16)],
        scratch_shapes=[pltpu.VMEM((2 * HALO, tc), F32), pltpu.VMEM((2 * HALO, tc), F32)],
        compiler_params=_cp(("parallel", "arbitrary")))(u_pre, u_pre, u_pre, u_pre, w, w, b, b)


def _down_ln2_loss(act, w_down, h1, target, g2, b2, tm=512):
    T = h1.shape[0]

    def body(a_ref, w_ref, h_ref, t_ref, g_ref, b_ref, dr_ref, dg_ref, db_ref, l_ref):
        @pl.when(pl.program_id(0) == 0)
        def _():
            dg_ref[...] = jnp.zeros_like(dg_ref)
            db_ref[...] = jnp.zeros_like(db_ref)
            l_ref[...] = jnp.zeros_like(l_ref)

        r = ALPHA * h_ref[...] + _dot(a_ref[...], w_ref[...])
        y, xhat, rstd = _ln_fwd(r, g_ref[...], b_ref[...])
        err = y - t_ref[...]
        l_ref[...] += jnp.sum(err * err, keepdims=True)
        dy = err * (1.0 / D_MODEL)
        dg_ref[...] += jnp.sum(dy * xhat, axis=0, keepdims=True)
        db_ref[...] += jnp.sum(dy, axis=0, keepdims=True)
        dr_ref[...] = _ln_bwd(dy, xhat, rstd, g_ref[...])

    row = pl.BlockSpec((tm, D_MODEL), lambda i: (i, 0))
    vec = _const_spec((1, D_MODEL))
    return pl.pallas_call(
        body, name="down_ln2_loss", grid=(T // tm,),
        in_specs=[pl.BlockSpec((tm, D_FF), lambda i: (i, 0)), _const_spec((D_FF, D_MODEL)), row, row, vec, vec],
        out_specs=[row, vec, vec, _const_spec((1, 1))],
        out_shape=[jax.ShapeDtypeStruct((T, D_MODEL), F32), jax.ShapeDtypeStruct((1, D_MODEL), F32),
                   jax.ShapeDtypeStruct((1, D_MODEL), F32), jax.ShapeDtypeStruct((1, 1), F32)],
        compiler_params=_cp(("arbitrary",)))(act, w_down, h1, target, g2, b2)


def _ffn_gate_conv_bwd(dact, u, u_pre, w, tr=2048, tc=256):
    T = dact.shape[0]
    tr = min(tr, T)
    K = FFN_CONV
    nj, nt, n, hb = D_FF // tc, T // tr, tr // HALO, tr // HALO
    last_hblock = T // HALO - 1
    assert T % tr == 0 and D_FF % tc == 0 and n % 2 == 0

    def body(da_ref, dan_ref, u_ref, un_ref, xg_ref, xv_ref, wg_ref, wv_ref,
             o_ref, dwg_ref, dwv_ref, dbg_ref, dbv_ref, gext_g, gext_v):
        i = pl.program_id(1)
        for r in range(n + 1):
            rows = slice(r * HALO, (r + 1) * HALO)
            if r < n:
                da, g, v = da_ref[rows, :], u_ref[0, rows, :], u_ref[1, rows, :]
            else:
                da, g, v = jnp.where(i < nt - 1, dan_ref[...], 0.0), un_ref[0], un_ref[1]
            sg = _sigmoid(g)
            gext_g[rows, :] = da * v * (sg * (1.0 + g * (1.0 - sg)))
            gext_v[rows, :] = da * (g * sg)
        for half, (gext, x_ref, w_ref, dw_ref, db_ref) in enumerate(((gext_g, xg_ref, wg_ref, dwg_ref, dbg_ref),
                                                                      (gext_v, xv_ref, wv_ref, dwv_ref, dbv_ref))):
            wk = _taps(w_ref, K, tc)
            acc_w = [jnp.zeros((HALO, tc), F32) for _ in range(K)]
            acc_b = jnp.zeros((HALO, tc), F32)
            for r2 in range(n // 2):
                pair = []
                for r in (2 * r2, 2 * r2 + 1):
                    lo = r * HALO
                    x = x_ref[lo:lo + HALO, :]
                    dpre = None
                    for s in range(K):
                        gs = gext[lo + s:lo + HALO + s, :]
                        dpre = wk[K - 1 - s] * gs if dpre is None else dpre + wk[K - 1 - s] * gs
                        acc_w[K - 1 - s] = acc_w[K - 1 - s] + gs * x
                        if s == 0:
                            acc_b = acc_b + gs
                    pair.append(dpre)
                o_ref[half, 2 * r2 * HALO:(2 * r2 + 2) * HALO, :] = jnp.concatenate(pair, axis=0).astype(o_ref.dtype)

            @pl.when(i == 0)
            def _():
                dw_ref[...] = jnp.zeros_like(dw_ref)
                db_ref[...] = jnp.zeros_like(db_ref)

            db_ref[...] += jnp.sum(acc_b, axis=0, keepdims=True)
            dw_ref[...] += jnp.concatenate([jnp.sum(a, axis=0, keepdims=True) for a in acc_w], axis=0)

    nxt = lambda i: jnp.minimum((i + 1) * hb, last_hblock)
    taps = lambda off: pl.BlockSpec((K, tc), lambda j, i: (0, off + j))
    dw_spec, db_spec = pl.BlockSpec((K, tc), lambda j, i: (0, j)), pl.BlockSpec((1, tc), lambda j, i: (0, j))
    du_pre, dwg, dwv, dbg, dbv = pl.pallas_call(
        body, name="ffn_gate_conv_bwd", grid=(nj, nt),
        in_specs=[pl.BlockSpec((tr, tc), lambda j, i: (i, j)), pl.BlockSpec((HALO, tc), lambda j, i: (nxt(i), j)),
                  pl.BlockSpec((2, tr, tc), lambda j, i: (0, i, j)), pl.BlockSpec((2, HALO, tc), lambda j, i: (0, nxt(i), j)),
                  pl.BlockSpec((tr, tc), lambda j, i: (i, j)), pl.BlockSpec((tr, tc), lambda j, i: (i, nj + j)),
                  taps(0), taps(nj)],
        out_specs=[pl.BlockSpec((2, tr, tc), lambda j, i: (0, i, j)), dw_spec, dw_spec, db_spec, db_spec],
        out_shape=[jax.ShapeDtypeStruct((2, T, D_FF), BF16), jax.ShapeDtypeStruct((K, D_FF), F32),
                   jax.ShapeDtypeStruct((K, D_FF), F32), jax.ShapeDtypeStruct((1, D_FF), F32),
                   jax.ShapeDtypeStruct((1, D_FF), F32)],
        scratch_shapes=[pltpu.VMEM((tr + HALO, tc), F32), pltpu.VMEM((tr + HALO, tc), F32)],
        compiler_params=_cp(("parallel", "arbitrary")))(dact, dact, u, u, u_pre, u_pre, w, w)
    return du_pre, jnp.concatenate([dwg, dwv], axis=1), jnp.concatenate([dbg, dbv], axis=1)


def _ffn_bwd_in(du_pre, w_up, dr2, r1, g1, b1, tm=1024, tk=1408):
    T = dr2.shape[0]
    tm = min(tm, T)
    assert T % tm == 0 and D_FF % tk == 0
    nk = 2 * D_FF // tk
    kh = D_FF // tk

    def body(d_ref, w_ref, dr2_ref, r_ref, g_ref, b_ref, dr1_ref, dg_ref, db_ref, acc):
        i, k = pl.program_id(0), pl.program_id(1)

        @pl.when(jnp.logical_and(i == 0, k == 0))
        def _():
            dg_ref[...] = jnp.zeros_like(dg_ref)
            db_ref[...] = jnp.zeros_like(db_ref)

        @pl.when(k == 0)
        def _():
            acc[...] = ALPHA * dr2_ref[...]

        acc[...] += _dot(d_ref[...], w_ref[...], NT)

        @pl.when(k == nk - 1)
        def _():
            _, xhat, rstd = _ln_fwd(r_ref[...], g_ref[...], b_ref[...])
            dy = acc[...]
            dg_ref[...] += jnp.sum(dy * xhat, axis=0, keepdims=True)
            db_ref[...] += jnp.sum(dy, axis=0, keepdims=True)
            dr1_ref[...] = _ln_bwd(dy, xhat, rstd, g_ref[...])

    row = pl.BlockSpec((tm, D_MODEL), lambda i, k: (i, 0))
    vec = _const_spec((1, D_MODEL))
    return pl.pallas_call(
        body, name="ffn_bwd_in", grid=(T // tm, nk),
        in_specs=[pl.BlockSpec((None, tm, tk), lambda i, k: (k // kh, i, k % kh)),
                  pl.BlockSpec((D_MODEL, tk), lambda i, k: (0, k)), row, row, vec, vec],
        out_specs=[row, vec, vec],
        out_shape=[jax.ShapeDtypeStruct((T, D_MODEL), F32), jax.ShapeDtypeStruct((1, D_MODEL), F32),
                   jax.ShapeDtypeStruct((1, D_MODEL), F32)],
        scratch_shapes=[pltpu.VMEM((tm, D_MODEL), F32)],
        compiler_params=_cp(("arbitrary", "arbitrary")))(du_pre, w_up, dr2, r1, g1, b1)


def _mix_bwd(dr1, w_mo, w_bs, w_ba, bs, ba, proj_main, b_gate, tm=512):
    T = dr1.shape[0]

    def body(d_ref, wmo_ref, wbs_ref, wba_ref, bs_ref, ba_ref, gs_ref, ga_ref, bg_ref,
             dg_ref, dbs_ref, dba_ref, dys_ref, dya_ref, dbg_ref):
        @pl.when(pl.program_id(0) == 0)
        def _():
            dbg_ref[...] = jnp.zeros_like(dbg_ref)

        dm = _dot(d_ref[...].astype(BF16), wmo_ref[...], NT)
        g_s = _sigmoid(gs_ref[...] + bg_ref[:, 0:D_MODEL])
        g_a = _sigmoid(ga_ref[...] + bg_ref[:, D_MODEL:])
        dgs = dm * bs_ref[...] * g_s * (1.0 - g_s)
        dga = dm * ba_ref[...] * g_a * (1.0 - g_a)
        dg_ref[:, 0:D_MODEL] = dgs.astype(BF16)
        dg_ref[:, D_MODEL:] = dga.astype(BF16)
        dbg_ref[:, 0:D_MODEL] += jnp.sum(dgs, axis=0, keepdims=True)
        dbg_ref[:, D_MODEL:] += jnp.sum(dga, axis=0, keepdims=True)
        dbs = (dm * g_s).astype(BF16)
        dba = (dm * g_a).astype(BF16)
        dbs_ref[...] = dbs
        dba_ref[...] = dba
        dys_ref[...] = _dot(dbs, wbs_ref[...], NT)
        dya_ref[...] = _dot(dba, wba_ref[...], NT).astype(BF16)

    row = lambda w, off=0: pl.BlockSpec((tm, w), lambda i: (i, off))
    return pl.pallas_call(
        body, name="mix_bwd", grid=(T // tm,),
        in_specs=[row(D_MODEL), _const_spec((D_MODEL, D_MODEL)), _const_spec((SSM_INNER, D_MODEL)),
                  _const_spec((D_MODEL, D_MODEL)), row(D_MODEL), row(D_MODEL),
                  row(D_MODEL, GATE_OFF // D_MODEL), row(D_MODEL, GATE_OFF // D_MODEL + 1), _const_spec((1, 2 * D_MODEL))],
        out_specs=[row(2 * D_MODEL, GATE_OFF // (2 * D_MODEL)), row(D_MODEL), row(D_MODEL), row(SSM_INNER), row(D_MODEL),
                   _const_spec((1, 2 * D_MODEL))],
        out_shape=[jax.ShapeDtypeStruct((T, MAIN_COLS), BF16), jax.ShapeDtypeStruct((T, D_MODEL), BF16),
                   jax.ShapeDtypeStruct((T, D_MODEL), BF16), jax.ShapeDtypeStruct((T, SSM_INNER), F32),
                   jax.ShapeDtypeStruct((T, D_MODEL), BF16), jax.ShapeDtypeStruct((1, 2 * D_MODEL), F32)],
        compiler_params=_cp(("arbitrary",)))(dr1, w_mo, w_bs, w_ba, bs, ba, proj_main, proj_main, b_gate)


def _local_step(x, target, w, p, late_weights=None, early_grads=None):
    xb = x.astype(BF16)
    if late_weights is None:
        proj_main = _matmul(xb, w["in_main"], "nn", F32, "in_proj_main", tm=1024, tn=2048)
    else:
        proj_main, *landed = _matmul(xb, w["in_main"], "nn", F32, "in_proj_main", tm=1024, tn=2048,
                                     comm=("gather", late_weights[0]))
        w = {**w, **late_weights[1](landed)}
    proj_tail = _matmul(xb, w["in_tail"], "nn", F32, "in_proj_tail", tn=TAIL_COLS)
    xbc, dsil = _conv_silu_fwd(proj_main, XBC_OFF, XBC_COLS, p["ssm_conv_w"], p["ssm_conv_b"], SSM_CONV, "ssm_conv_fwd")
    y_ssm, ypre, hs = _ssd_fwd(xbc, proj_main, proj_tail, p["dtb_col"], p["alog_col"], p["d_exp"], p["ssm_norm_w"])
    y_attn, lse = _attn_fwd(proj_main, proj_tail, p["bias_tbl"], p["attn_sinks"])
    merged, bs, ba = _merge_fwd(y_ssm, y_attn, proj_main, p["b_gate"], w["bs"], w["ba"])
    r1, h1, h1b = _mix_ln1(merged, w["mo"], x, p["ln1_g"], p["ln1_b"])
    u_pre = _matmul(h1b, w["up"], "nn", F32, "ffn_up", tm=1024, tn=1408)
    u, act = _ffn_conv_glu(u_pre, p["ffn_conv_w"], p["ffn_conv_b"])
    dr2, dg2, db2, sq = _down_ln2_loss(act, w["down"], h1, target, p["ln2_g"], p["ln2_b"])
    g = {"ln2_g": dg2, "ln2_b": db2}
    g["w_down"] = _matmul(act, dr2, "tn", BF16, "dw_down", tm=1408, tn=1024, tk=1024)
    dact = _matmul(dr2, w["down"], "nt", F32, "ffn_dact", tm=1024, tn=1408, tk=1024)
    du_pre, g["ffn_conv_w"], g["ffn_conv_b"] = _ffn_gate_conv_bwd(dact, u, u_pre, p["ffn_conv_w"])
    g["w_up"] = _matmul(h1b, du_pre, "tn", BF16, "dw_up", tm=1024, tn=1408, tk=2048)
    dr1, g["ln1_g"], g["ln1_b"] = _ffn_bwd_in(du_pre, w["up"], dr2, r1, p["ln1_g"], p["ln1_b"])
    g["w_mix_out"] = _matmul(merged, dr1, "tn", BF16, "dw_mix_out", tm=1024, tn=1024, tk=2048)
    dmain, dbs, dba, dy_ssm, dy_attn, g["b_gate"] = _mix_bwd(dr1, w["mo"], w["bs"], w["ba"], bs, ba, proj_main, p["b_gate"])
    g["w_branch_ssm"] = _matmul(y_ssm, dbs, "tn", BF16, "dw_branch_ssm", tm=1024, tn=1024, tk=2048)
    g["w_branch_attn"] = _matmul(y_attn, dba, "tn", BF16, "dw_branch_attn", tm=1024, tn=1024, tk=2048)
    dmain, dtail, dbias, g["attn_sinks"] = _attn_bwd(dy_attn, lse, proj_main, proj_tail, p["bias_tbl"], p["attn_sinks"], dmain)
    g["rel_bias"] = _rel_bias_grad(dbias.reshape(ATTN_HEADS, WINDOW * 2 * WINDOW), p["bucket_onehot"]).T
    dmain, dtail, dco, g["ssm_norm_w"], dd, dalog, ddtb = _ssd_bwd(
        dy_ssm, ypre, xbc, dsil, hs, proj_main, proj_tail, p["dtb_col"], p["alog_col"], p["d_exp"], p["ssm_norm_w"],
        p["ehead_t"], dmain, dtail)
    g["ssm_d"], g["ssm_a_log"], g["ssm_dt_bias"] = (a.reshape(1, SSM_HEADS) for a in (dd, dalog, ddtb))
    dmain, g["ssm_conv_w"], g["ssm_conv_b"] = _conv_bwd(
        dco, proj_main, XBC_OFF, XBC_COLS, p["ssm_conv_w"], SSM_CONV, dmain, XBC_OFF, "ssm_conv_bwd")
    g["in_tail"] = _matmul(xb, dtail, "tn", BF16, "dw_in_tail", tm=1024, tn=TAIL_COLS, tk=2048)
    landed = []
    if early_grads is None:
        g["in_main"] = _matmul(xb, dmain, "tn", BF16, "dw_in_main", tm=1024, tn=1024, tk=2048)
    else:
        g["in_main"], *landed = _matmul(xb, dmain, "tn", BF16, "dw_in_main", tm=1024, tn=1024, tk=2048,
                                        comm=("exchange", early_grads(g)))
    return sq, (dmain, dtail, dr1), w, g, landed


def _grad_x(dproj, w, exchange=None):
    dmain, dtail, dr1 = dproj
    landed = None
    if exchange is None:
        dx = _matmul(dmain, w["in_main"], "nt", F32, "dx_main", tm=1024, tk=2048, addend=dr1, addend_scale=ALPHA)
    else:
        dx, landed = _matmul(dmain, w["in_main"], "nt", F32, "dx_main", tm=1024, tk=2048, addend=dr1,
                             addend_scale=ALPHA, comm=("exchange", exchange))
    dx = _matmul(dtail, w["in_tail"], "nt", F32, "dx_tail", tk=TAIL_COLS, addend=dx)
    return dx if exchange is None else (dx, landed)


SHARD_COLS = IN_COLS // N_DEV
W_IN_SEGMENTS = ((O_Z, 2048, "main", Z_OFF), (O_XBC, XBC_COLS, "main", XBC_OFF), (O_DT, SSM_HEADS, "tail", DT_OFF),
                 (O_Q, D_MODEL, "main", Q_OFF), (O_K, 128, "tail", K_OFF), (O_V, 128, "tail", V_OFF),
                 (O_GATE, 2 * D_MODEL, "main", GATE_OFF))


def _w_in_from_shards(shards):
    def seg(off, n):
        pieces = []
        for j in range(off // SHARD_COLS, (off + n - 1) // SHARD_COLS + 1):
            lo, hi = max(off, j * SHARD_COLS), min(off + n, (j + 1) * SHARD_COLS)
            pieces.append(shards[j, :, lo - j * SHARD_COLS:hi - j * SHARD_COLS])
        return pieces

    by_name = {(where, koff): seg(off, n) for off, n, where, koff in W_IN_SEGMENTS}
    main = jnp.concatenate(by_name["main", Z_OFF] + by_name["main", XBC_OFF] + by_name["main", Q_OFF]
                           + by_name["main", GATE_OFF], axis=1)
    tail = jnp.concatenate(by_name["tail", K_OFF] + by_name["tail", V_OFF] + by_name["tail", DT_OFF]
                           + [jnp.zeros((shards.shape[1], 128 - SSM_HEADS), shards.dtype)], axis=1)
    return main, tail


def _w_in_grad_by_device(g_main, g_tail):
    slots = []
    for j in range(N_DEV):
        a, b = j * SHARD_COLS, (j + 1) * SHARD_COLS
        pieces = []
        for off, n, where, koff in W_IN_SEGMENTS:
            lo, hi = max(a, off), min(b, off + n)
            if lo < hi:
                pieces.append((g_main if where == "main" else g_tail)[:, koff + lo - off:koff + hi - off])
        slots.append(jnp.concatenate(pieces, axis=1))
    return jnp.stack(slots)


def _prep_params(rel_bias, b_gate, ssm_conv_w, ssm_conv_b, ssm_dt_bias, ssm_a_log, ssm_d, ssm_norm_w, attn_sinks,
                 ln1_g, ln1_b, ffn_conv_w, ffn_conv_b, ln2_g, ln2_b):
    bucket, in_window = _band_geometry()
    bucket, in_window = bucket.T, in_window.T
    onehot = jnp.logical_and(bucket.reshape(-1, 1) == jnp.arange(REL_BUCKETS)[None, :],
                             in_window.reshape(-1, 1)).astype(F32)
    onehot_t = jnp.logical_and(bucket.reshape(1, -1) == jnp.arange(REL_BUCKETS)[:, None],
                               in_window.reshape(1, -1)).astype(BF16)
    bias_tbl = _bias_table(rel_bias.T, onehot_t, jnp.where(in_window.reshape(1, -1), 0.0, NEG))
    ehead_t = (jnp.arange(SSM_INNER)[None, :] // SSMD == jnp.arange(SSM_HEADS)[:, None]).astype(F32)
    return {"bias_tbl": bias_tbl, "bucket_onehot": onehot, "b_gate": b_gate, "ssm_conv_w": ssm_conv_w,
            "ssm_conv_b": ssm_conv_b, "dtb_col": ssm_dt_bias.reshape(SSM_HEADS, 1),
            "alog_col": ssm_a_log.reshape(SSM_HEADS, 1), "ehead_t": ehead_t,
            "d_exp": jnp.repeat(ssm_d, SSMD, axis=1), "ssm_norm_w": ssm_norm_w, "attn_sinks": attn_sinks,
            "ln1_g": ln1_g, "ln1_b": ln1_b, "ffn_conv_w": ffn_conv_w, "ffn_conv_b": ffn_conv_b,
            "ln2_g": ln2_g, "ln2_b": ln2_b}


def _all_gather(shards, name):
    nb = len(shards)

    def body(*refs):
        for phase in _gather_phases(refs[:nb], refs[nb:2 * nb], *refs[2 * nb:]):
            phase()

    any_spec = pl.BlockSpec(memory_space=pl.ANY)
    return pl.pallas_call(
        body, name=name, out_shape=[jax.ShapeDtypeStruct((N_DEV,) + s.shape, s.dtype) for s in shards],
        in_specs=[any_spec] * nb, out_specs=[any_spec] * nb, scratch_shapes=_comm_sems(nb))(*shards)


def _adamw_math(w, g, m, v):
    m = ADAM_B1 * m + (1.0 - ADAM_B1) * g
    v = ADAM_B2 * v + (1.0 - ADAM_B2) * (g * g)
    m_hat = m / (1.0 - ADAM_B1 ** ADAM_STEP)
    v_hat = v / (1.0 - ADAM_B2 ** ADAM_STEP)
    return -ADAM_LR * (m_hat / (jnp.sqrt(v_hat) + ADAM_EPS) + ADAM_WD * w), m, v


def _slot_total(s_ref):
    g = s_ref[0].astype(F32)
    for i in range(1, N_DEV):
        g = g + s_ref[i].astype(F32)
    return g


def _adamw(landed, w, m, v, name):
    R, C = w.shape
    tr = 256 if R % 256 == 0 and R > 256 else R

    def body(s_ref, w_ref, m_ref, v_ref, g_ref, d_ref, nm_ref, nv_ref):
        g = _slot_total(s_ref)
        g_ref[...] = g
        d_ref[...], nm_ref[...], nv_ref[...] = _adamw_math(w_ref[...], g, m_ref[...], v_ref[...])

    spec = pl.BlockSpec((tr, C), lambda i: (i, 0))
    return pl.pallas_call(
        body, name=name, grid=(R // tr,), in_specs=[pl.BlockSpec((N_DEV, tr, C), lambda i: (0, i, 0))] + [spec] * 3,
        out_specs=[spec] * 4, out_shape=[jax.ShapeDtypeStruct((R, C), F32)] * 4,
        compiler_params=_cp(("parallel",)))(landed, w, m, v)


def _small_update(landed, ws, ms, vs):
    k = len(ws)

    def body(*refs):
        s_ref, w_refs, m_refs, v_refs = refs[0], refs[1:1 + k], refs[1 + k:1 + 2 * k], refs[1 + 2 * k:1 + 3 * k]
        outs = refs[1 + 3 * k:]
        g_all = _slot_total(s_ref)
        for i in range(k):
            n = w_refs[i].shape[1]
            g = g_all[i:i + 1, 0:n]
            outs[i][...] = g
            outs[k + i][...], outs[2 * k + i][...], outs[3 * k + i][...] = _adamw_math(
                w_refs[i][...], g, m_refs[i][...], v_refs[i][...])

    return pl.pallas_call(body, name="small_update",
                          out_shape=[jax.ShapeDtypeStruct(w.shape, F32) for w in ws] * 4)(landed, *ws, *ms, *vs)


SHARDED = {"w_in": "cols", "w_branch_ssm": "rows", "w_branch_attn": "rows", "w_mix_out": "rows", "w_up": "cols",
           "w_down": "rows", "ssm_conv_w": "cols", "ffn_conv_w": "cols"}
LATE = ("w_branch_ssm", "w_branch_attn", "w_mix_out", "w_up", "w_down")
SHORT = {"w_branch_ssm": "bs", "w_branch_attn": "ba", "w_mix_out": "mo", "w_up": "up", "w_down": "down"}
SMALL = ("rel_bias", "b_gate", "ssm_conv_b", "ssm_dt_bias", "ssm_a_log", "ssm_d", "ssm_norm_w", "attn_sinks",
         "ln1_g", "ln1_b", "ffn_conv_b", "ln2_g", "ln2_b")
WEIGHTS = ("rel_bias", "w_in", "b_gate", "ssm_conv_w", "ssm_conv_b", "ssm_dt_bias", "ssm_a_log", "ssm_d", "ssm_norm_w",
           "attn_sinks", "w_branch_ssm", "w_branch_attn", "w_mix_out", "ln1_g", "ln1_b", "w_up", "ffn_conv_w",
           "ffn_conv_b", "w_down", "ln2_g", "ln2_b")
SMALL_ROWS, SMALL_COLS = 16, 2 * D_FF


def _by_device(full, how):
    r, c = full.shape
    if how == "rows":
        return full.reshape(N_DEV, r // N_DEV, c)
    return full.reshape(r, N_DEV, c // N_DEV).transpose(1, 0, 2)


def _from_devices(slots, how):
    _, r, c = slots.shape
    if how == "rows":
        return slots.reshape(N_DEV * r, c)
    return slots.transpose(1, 0, 2).reshape(r, N_DEV * c)


def kernel(x, rel_bias, w_in, b_gate, ssm_conv_w, ssm_conv_b, ssm_dt_bias, ssm_a_log, ssm_d, ssm_norm_w, attn_sinks, w_branch_ssm, w_branch_attn, w_mix_out, ln1_g, ln1_b, w_up, ffn_conv_w, ffn_conv_b, w_down, ln2_g, ln2_b, loss_target, m_rel_bias, m_w_in, m_b_gate, m_ssm_conv_w, m_ssm_conv_b, m_ssm_dt_bias, m_ssm_a_log, m_ssm_d, m_ssm_norm_w, m_attn_sinks, m_w_branch_ssm, m_w_branch_attn, m_w_mix_out, m_ln1_g, m_ln1_b, m_w_up, m_ffn_conv_w, m_ffn_conv_b, m_w_down, m_ln2_g, m_ln2_b, v_rel_bias, v_w_in, v_b_gate, v_ssm_conv_w, v_ssm_conv_b, v_ssm_dt_bias, v_ssm_a_log, v_ssm_d, v_ssm_norm_w, v_attn_sinks, v_w_branch_ssm, v_w_branch_attn, v_w_mix_out, v_ln1_g, v_ln1_b, v_w_up, v_ffn_conv_w, v_ffn_conv_b, v_w_down, v_ln2_g, v_ln2_b):
    W = dict(zip(WEIGHTS, (rel_bias, w_in, b_gate, ssm_conv_w, ssm_conv_b, ssm_dt_bias, ssm_a_log, ssm_d, ssm_norm_w,
                           attn_sinks, w_branch_ssm, w_branch_attn, w_mix_out, ln1_g, ln1_b, w_up, ffn_conv_w,
                           ffn_conv_b, w_down, ln2_g, ln2_b)))
    M = dict(zip(WEIGHTS, (m_rel_bias, m_w_in, m_b_gate, m_ssm_conv_w, m_ssm_conv_b, m_ssm_dt_bias, m_ssm_a_log, m_ssm_d,
                           m_ssm_norm_w, m_attn_sinks, m_w_branch_ssm, m_w_branch_attn, m_w_mix_out, m_ln1_g, m_ln1_b,
                           m_w_up, m_ffn_conv_w, m_ffn_conv_b, m_w_down, m_ln2_g, m_ln2_b)))
    V = dict(zip(WEIGHTS, (v_rel_bias, v_w_in, v_b_gate, v_ssm_conv_w, v_ssm_conv_b, v_ssm_dt_bias, v_ssm_a_log, v_ssm_d,
                           v_ssm_norm_w, v_attn_sinks, v_w_branch_ssm, v_w_branch_attn, v_w_mix_out, v_ln1_g, v_ln1_b,
                           v_w_up, v_ffn_conv_w, v_ffn_conv_b, v_w_down, v_ln2_g, v_ln2_b)))
    shard2d = lambda a: a.reshape(a.shape[-2], a.shape[-1])

    (win_all,) = _all_gather([shard2d(w_in).astype(BF16)], "gather_w_in")
    main, tail = _w_in_from_shards(win_all)
    conv_all = _all_gather([shard2d(ssm_conv_w), shard2d(ffn_conv_w)], "gather_conv_weights")
    late_shards = [shard2d(W[n]).astype(BF16) for n in LATE]
    late = lambda landed: {SHORT[n]: _from_devices(a, SHARDED[n]) for n, a in zip(LATE, landed)}
    p = _prep_params(rel_bias, b_gate, _from_devices(conv_all[0], "cols"), ssm_conv_b, ssm_dt_bias, ssm_a_log, ssm_d,
                     ssm_norm_w, attn_sinks, ln1_g, ln1_b, _from_devices(conv_all[1], "cols"), ffn_conv_b, ln2_g, ln2_b)

    early_names = LATE + ("ssm_conv_w", "ffn_conv_w")
    early = lambda g: [_by_device(g[n], SHARDED[n]).astype(BF16 if n in LATE else F32) for n in early_names]
    sq, dproj, w, g, landed = _local_step(x[0], loss_target[0], {"in_main": main, "in_tail": tail}, p,
                                          (late_shards, late), early)
    landed = dict(zip(early_names, landed))
    dx, landed["w_in"] = _grad_x(dproj, w, exchange=[_w_in_grad_by_device(g.pop("in_main"), g.pop("in_tail"))])
    loss = (0.5 / D_MODEL) * lax.psum(sq[0, 0], ("x", "y", "c"))
    grads, deltas, new_m, new_v = {}, {}, {}, {}
    for n in SHARDED:
        outs = _adamw(landed[n], shard2d(W[n]), shard2d(M[n]), shard2d(V[n]), "adamw_" + n)
        grads[n], deltas[n], new_m[n], new_v[n] = (a.reshape(W[n].shape) for a in outs)

    row = lambda a: a.reshape(1, -1)
    packed = jnp.concatenate([jnp.pad(row(g[n]), ((0, 0), (0, SMALL_COLS - g[n].size))) for n in SMALL]
                             + [jnp.zeros((SMALL_ROWS - len(SMALL), SMALL_COLS), F32)], axis=0)
    (small_all,) = _all_gather([packed], "gather_small_grads")
    outs = _small_update(small_all, *[[row(src[n]) for n in SMALL] for src in (W, M, V)])
    for i, n in enumerate(SMALL):
        grads[n], deltas[n], new_m[n], new_v[n] = (outs[j * len(SMALL) + i].reshape(W[n].shape) for j in range(4))

    return (loss, dx[None], *[grads[n] for n in WEIGHTS], *[deltas[n] for n in WEIGHTS],
            *[new_m[n] for n in WEIGHTS], *[new_v[n] for n in WEIGHTS])
```

```python
import math

import jax
import jax.numpy as jnp
from jax import lax
from jax.experimental import pallas as pl
from jax.experimental.pallas import tpu as pltpu

F32, BF16 = jnp.float32, jnp.bfloat16
HIGHEST = lax.Precision.HIGHEST
MESH_ID = pl.DeviceIdType.MESH

N_DEV = 8
D_MODEL = 1024
SSM_INNER = 2048
SSM_HEADS = 32
SSM_HEADDIM = 64
SSMD = SSM_HEADDIM
SSM_GROUPS = 4
SSM_GROUP_COLS = SSM_INNER // SSM_GROUPS
SSM_STATE = 128
SSM_CONV = 4
CHUNK = 128
XBC_COLS = SSM_INNER + 2 * SSM_GROUPS * SSM_STATE
B_OFF = SSM_INNER
C_OFF = SSM_INNER + SSM_GROUPS * SSM_STATE
ATTN_HEADS = 16
ATTN_KV = 2
ATTN_GROUP = 8
HEADDIM = 64
WINDOW = 128
REL_BUCKETS = 32
REL_MAX_DIST = 128
D_FF = 2816
FFN_CONV = 3
ALPHA = 2.0 ** 0.25
LN_EPS = 1e-5
RMS_EPS = 1e-5
IN_COLS = 8480
Z_OFF, XBC_OFF, Q_OFF, GATE_OFF, MAIN_COLS = 0, 2048, 5120, 6144, 8192
K_OFF, V_OFF, DT_OFF, TAIL_COLS = 0, 128, 256, 384
O_Z, O_XBC, O_DT, O_Q, O_K, O_V, O_GATE = 0, 2048, 5120, 5152, 6176, 6304, 6432

ADAM_LR, ADAM_B1, ADAM_B2, ADAM_EPS, ADAM_WD, ADAM_STEP = 0.001, 0.9, 0.999, 1e-08, 0.01, 10
NEG = -1e30
HALO = 8
VMEM_LIMIT = 56 * 1024 * 1024


def _cp(sem):
    return pltpu.CompilerParams(dimension_semantics=sem, vmem_limit_bytes=VMEM_LIMIT)


def _const_spec(shape):
    nd = len(shape)
    return pl.BlockSpec(shape, lambda *_: (0,) * nd)


def _sigmoid(x):
    return 0.5 * jnp.tanh(0.5 * x) + 0.5


def _softplus(x):
    return jnp.maximum(x, 0.0) + jnp.log1p(jnp.exp(-jnp.abs(x)))


def _dot(a, b, dims=(((1,), (0,)), ((), ())), precision=None):
    return lax.dot_general(a, b, dims, preferred_element_type=F32, precision=precision)


NN = (((1,), (0,)), ((), ()))
NT = (((1,), (1,)), ((), ()))
TN = (((0,), (0,)), ((), ()))


def _mesh_pos():
    return lax.axis_index("x"), lax.axis_index("y"), lax.axis_index("c")


PEERS = N_DEV - 1


def _exchange_phases(in_refs, out_refs, send_sems, recv_sems, local_sems):
    def copies():
        x, y, c = _mesh_pos()
        me = 4 * x + 2 * y + c
        cps = []
        for b, (in_ref, out_ref) in enumerate(zip(in_refs, out_refs)):
            cps.append(pltpu.make_async_copy(in_ref.at[me], out_ref.at[me], local_sems.at[b]))
            for r in range(1, N_DEV):
                px = 1 - x if r & 4 else x
                py = 1 - y if r & 2 else y
                pc = 1 - c if r & 1 else c
                cps.append(pltpu.make_async_remote_copy(
                    src_ref=in_ref.at[4 * px + 2 * py + pc], dst_ref=out_ref.at[me],
                    send_sem=send_sems.at[b * PEERS + r - 1], recv_sem=recv_sems.at[b * PEERS + r - 1],
                    device_id=(px, py, pc), device_id_type=MESH_ID))
        return cps

    def start():
        for cp in copies():
            cp.start()

    def finish():
        for cp in copies():
            cp.wait()

    return [start, finish]


def _gather_phases(x_refs, out_refs, send_sems, recv_sems, local_sems):
    def parts(which):
        x, y, c = _mesh_pos()
        me, sibling = (x, y, c), (x, y, 1 - c)
        chips = [(1 - x, y), (x, 1 - y), (1 - x, 1 - y)]
        found = []
        for b, (x_ref, out_ref) in enumerate(zip(x_refs, out_refs)):
            def slot(px, py, pc):
                return out_ref.at[4 * px + 2 * py + pc]

            def copy(k, block, to, src=None):
                return pltpu.make_async_remote_copy(
                    src_ref=slot(*block) if src is None else src, dst_ref=slot(*block),
                    send_sem=send_sems.at[b * PEERS + k], recv_sem=recv_sems.at[b * PEERS + k],
                    device_id=to, device_id_type=MESH_ID)

            if which == "mine":
                found.append(pltpu.make_async_copy(x_ref, slot(*me), local_sems.at[b]))
            elif which == "first":
                found.append(copy(0, me, sibling, src=x_ref))
                found += [copy(1 + j, me, (*chip, c), src=x_ref) for j, chip in enumerate(chips)]
            elif which == "passed":
                found += [copy(4 + j, (*chip, c), sibling) for j, chip in enumerate(chips)]
            elif which == "arrived":
                found += [copy(1 + j, (*chip, c), me) for j, chip in enumerate(chips)]
            else:
                found.append(copy(0, sibling, me))
                found += [copy(4 + j, (*chip, 1 - c), me) for j, chip in enumerate(chips)]
        return found

    def start():
        for cp in parts("mine") + parts("first"):
            cp.start()

    def forward():
        for a, p in zip(parts("arrived"), parts("passed")):
            a.wait_recv()
            p.start()

    def finish():
        for cp in parts("late"):
            cp.wait_recv()
        for cp in parts("first") + parts("passed"):
            cp.wait_send()
        for cp in parts("mine"):
            cp.wait()

    return [start, forward, finish]


COMM = {"exchange": _exchange_phases, "gather": _gather_phases}


def _comm_sems(nb):
    return [pltpu.SemaphoreType.DMA((nb * PEERS,)), pltpu.SemaphoreType.DMA((nb * PEERS,)), pltpu.SemaphoreType.DMA((nb,))]


def _matmul(a, b, mode, out_dtype, name, tm=512, tn=1024, tk=1024, addend=None, addend_scale=1.0, comm=None):
    bufs = [] if comm is None else list(comm[1])
    nb = len(bufs)
    halves = b.ndim == 3
    if mode == "nn":
        (M, K), (K2, N) = a.shape, b.shape
    elif mode == "nt":
        (M, K), (N, K2) = a.shape, b.shape
    elif halves:
        (K, M), (K2, N) = a.shape, (b.shape[1], 2 * b.shape[2])
    else:
        (K, M), (K2, N) = a.shape, b.shape
    assert K == K2, (a.shape, b.shape, mode)
    tm, tn, tk = min(tm, M), min(tn, N), min(tk, K)
    assert M % tm == 0 and N % tn == 0 and K % tk == 0, (M, N, K, tm, tn, tk)
    nk = K // tk
    dims = {"nn": NN, "nt": NT, "tn": TN}[mode]
    a_spec = pl.BlockSpec((tk, tm), lambda i, j, k: (k, i)) if mode == "tn" else pl.BlockSpec((tm, tk), lambda i, j, k: (i, k))
    b_spec = pl.BlockSpec((tn, tk), lambda i, j, k: (j, k)) if mode == "nt" else pl.BlockSpec((tk, tn), lambda i, j, k: (k, j))
    o_spec = pl.BlockSpec((tm, tn), lambda i, j, k: (i, j))

    ni, nj = M // tm, N // tn
    if halves:
        assert mode == "tn" and nj % 2 == 0
        b_spec = pl.BlockSpec((None, tk, tn), lambda i, j, k: (j // (nj // 2), k, j % (nj // 2)))

    def body(*refs):
        refs = list(refs)
        a_ref, b_ref = refs[:2]
        c_ref = refs[2] if addend is not None else None
        n_in = 2 + (addend is not None) + nb
        o_ref, acc = refs[n_in], refs[n_in + 1 + nb]
        i, j, k = pl.program_id(0), pl.program_id(1), pl.program_id(2)
        step = (i * nj + j) * nk + k
        if comm is not None:
            phases = COMM[comm[0]](refs[n_in - nb:n_in], refs[n_in + 1:n_in + 1 + nb], *refs[n_in + 2 + nb:])
            at = [(ni * nj * nk - 1) * p // (len(phases) - 1) for p in range(len(phases))]
            for when, phase in zip(at[:-1], phases[:-1]):
                pl.when(step == when)(phase)

        d = _dot(a_ref[...].astype(BF16), b_ref[...].astype(BF16), dims)

        def finish(r):
            if addend is not None:
                r = r + addend_scale * c_ref[...].astype(F32)
            o_ref[...] = r.astype(out_dtype)

        if nk == 1:
            finish(d)
        else:
            @pl.when(k == 0)
            def _():
                acc[...] = d

            @pl.when(jnp.logical_and(k > 0, k < nk - 1))
            def _():
                acc[...] += d

            @pl.when(k == nk - 1)
            def _():
                finish(acc[...] + d)

        if comm is not None:
            pl.when(step == at[-1])(phases[-1])

    in_specs = [a_spec, b_spec] + ([o_spec] if addend is not None else [])
    args = (a, b) + ((addend,) if addend is not None else ())
    out_specs, out_shape = o_spec, jax.ShapeDtypeStruct((M, N), out_dtype)
    scratch = [pltpu.VMEM((tm, tn), F32)]
    sem = ("parallel", "parallel", "arbitrary")
    if comm is not None:
        any_spec = pl.BlockSpec(memory_space=pl.ANY)
        in_specs, args = in_specs + [any_spec] * nb, args + tuple(bufs)
        landed = [x.shape if comm[0] == "exchange" else (N_DEV,) + x.shape for x in bufs]
        out_specs = [o_spec] + [any_spec] * nb
        out_shape = [out_shape] + [jax.ShapeDtypeStruct(s, x.dtype) for s, x in zip(landed, bufs)]
        scratch += _comm_sems(nb)
        sem = ("arbitrary", "arbitrary", "arbitrary")
    return pl.pallas_call(
        body, name=name, grid=(ni, nj, nk), in_specs=in_specs, out_specs=out_specs, out_shape=out_shape,
        scratch_shapes=scratch, compiler_params=_cp(sem))(*args)


def _taps(w_ref, K, tc):
    return [jnp.broadcast_to(w_ref[k:k + 1, :], (HALO, tc)) for k in range(K)]


def _conv_silu_fwd(pre, pre_col_off, C, w, b, K, name, tr=1024, tc=512):
    T = pre.shape[0]
    tr, tc = min(tr, T), min(tc, C)
    assert T % tr == 0 and C % tc == 0 and pre_col_off % tc == 0
    joff = pre_col_off // tc
    hb = tr // HALO

    def body(x_ref, xp_ref, w_ref, b_ref, o_ref, d_ref, head):
        i = pl.program_id(1)
        head[0:HALO, :] = jnp.where(i > 0, xp_ref[...], 0.0)
        head[HALO:, :] = x_ref[0:HALO, :]
        wk = _taps(w_ref, K, tc)
        bias = jnp.broadcast_to(b_ref[...], (HALO, tc))
        for r in range(tr // HALO):
            lo = r * HALO
            co = bias + wk[K - 1] * x_ref[lo:lo + HALO, :]
            for k in range(K - 1):
                s = K - 1 - k
                co = co + wk[k] * (head[HALO - s:2 * HALO - s, :] if r == 0 else x_ref[lo - s:lo + HALO - s, :])
            sg = _sigmoid(co)
            y = co * sg
            o_ref[lo:lo + HALO, :] = y
            d_ref[lo:lo + HALO, :] = sg + y * (1.0 - sg)

    out = pl.BlockSpec((tr, tc), lambda j, i: (i, j))
    return pl.pallas_call(
        body, name=name, grid=(C // tc, T // tr),
        in_specs=[pl.BlockSpec((tr, tc), lambda j, i: (i, joff + j)),
                  pl.BlockSpec((HALO, tc), lambda j, i: (jnp.maximum(i * hb - 1, 0), joff + j)),
                  pl.BlockSpec((K, tc), lambda j, i: (0, j)),
                  pl.BlockSpec((1, tc), lambda j, i: (0, j))],
        out_specs=[out, out], out_shape=[jax.ShapeDtypeStruct((T, C), F32)] * 2,
        scratch_shapes=[pltpu.VMEM((2 * HALO, tc), F32)],
        compiler_params=_cp(("parallel", "arbitrary")))(pre, pre, w, b)


def _conv_bwd(dout, pre, pre_col_off, C, w, K, dst, dst_col_off, name, tr=1024, tc=512):
    T = pre.shape[0]
    tr, tc = min(tr, T), min(tc, C)
    assert T % tr == 0 and C % tc == 0 and pre_col_off % tc == 0 and dst_col_off % tc == 0
    joff, doff = pre_col_off // tc, dst_col_off // tc
    hb = tr // HALO
    nt = T // tr
    n = tr // HALO
    last_hblock = T // HALO - 1

    def body(g_ref, gn_ref, x_ref, w_ref, *rest):
        o_ref, dw_ref, db_ref, edge = rest[-4:]
        i = pl.program_id(1)
        wk = _taps(w_ref, K, tc)
        edge[0:HALO, :] = g_ref[tr - HALO:tr, :]
        edge[HALO:, :] = jnp.where(i < nt - 1, gn_ref[...], 0.0)
        acc_w = [jnp.zeros((HALO, tc), F32) for _ in range(K)]
        acc_b = jnp.zeros((HALO, tc), F32)
        for r in range(n):
            lo = r * HALO
            x = x_ref[lo:lo + HALO, :]
            dpre = None
            for s in range(K):
                gs = edge[s:HALO + s, :] if (r == n - 1 and s > 0) else g_ref[lo + s:lo + HALO + s, :]
                dpre = wk[K - 1 - s] * gs if dpre is None else dpre + wk[K - 1 - s] * gs
                acc_w[K - 1 - s] = acc_w[K - 1 - s] + gs * x
                if s == 0:
                    acc_b = acc_b + gs
            o_ref[lo:lo + HALO, :] = dpre.astype(o_ref.dtype)

        @pl.when(i == 0)
        def _():
            dw_ref[...] = jnp.zeros_like(dw_ref)
            db_ref[...] = jnp.zeros_like(db_ref)

        db_ref[...] += jnp.sum(acc_b, axis=0, keepdims=True)
        dw_ref[...] += jnp.concatenate([jnp.sum(a, axis=0, keepdims=True) for a in acc_w], axis=0)

    tile = lambda off: pl.BlockSpec((tr, tc), lambda j, i: (i, off + j))
    in_specs = [tile(0), pl.BlockSpec((HALO, tc), lambda j, i: (jnp.minimum((i + 1) * hb, last_hblock), j)),
                tile(joff), pl.BlockSpec((K, tc), lambda j, i: (0, j))]
    args = (dout, dout, pre, w)
    if isinstance(dst, jax.ShapeDtypeStruct):
        aliases = {}
    else:
        in_specs.append(pl.BlockSpec(memory_space=pl.ANY))
        args += (dst,)
        aliases = {4: 0}
    return pl.pallas_call(
        body, name=name, grid=(C // tc, nt), in_specs=in_specs,
        out_specs=[tile(doff), pl.BlockSpec((K, tc), lambda j, i: (0, j)), pl.BlockSpec((1, tc), lambda j, i: (0, j))],
        out_shape=[jax.ShapeDtypeStruct(dst.shape, dst.dtype), jax.ShapeDtypeStruct((K, C), F32),
                   jax.ShapeDtypeStruct((1, C), F32)],
        scratch_shapes=[pltpu.VMEM((2 * HALO, tc), F32)],
        input_output_aliases=aliases,
        compiler_params=_cp(("parallel", "arbitrary")))(*args)


PAIR = 2 * SSMD
PAIRS_PER_GROUP = SSM_GROUP_COLS // PAIR


def _dot3(x, onehot):
    h1 = x.astype(BF16)
    r = x - h1.astype(F32)
    h2 = r.astype(BF16)
    h3 = (r - h2.astype(F32)).astype(BF16)
    return _dot(h1, onehot) + _dot(h2, onehot) + _dot(h3, onehot)


def _chunk_rows(dt_raw, dtb_col, alog_col):
    row = lax.broadcasted_iota(jnp.int32, (CHUNK, CHUNK), 0)
    col = lax.broadcasted_iota(jnp.int32, (CHUNK, CHUNK), 1)
    dt_rawT = dt_raw.T
    dtT = _softplus(dt_rawT + dtb_col)
    a_col = -jnp.exp(alog_col)
    acsT = _dot3(dtT * a_col, (row <= col).astype(BF16))
    return dt_rawT, dtT, a_col, acsT, row, col


def _block_diag(x, left):
    return jnp.concatenate([jnp.where(left, x, 0.0), jnp.where(left, 0.0, x)], axis=0).astype(BF16)


def _lane_bcast(v, h):
    return jnp.broadcast_to(v[:, h:h + 1], (CHUNK, CHUNK))


def _ssd_fwd(xbc, proj_main, proj_tail, dtb_col, alog_col, d_exp, norm_w):
    T = xbc.shape[0]
    nc = T // CHUNK

    def body(xbc_ref, dt_ref, z_ref, dtb_ref, alog_ref, d_ref, nw_ref, y_ref, ypre_ref, hs_ref, H):
        c = pl.program_id(0)

        @pl.when(c == 0)
        def _():
            H[...] = jnp.zeros_like(H)

        hs_ref[0] = H[...]
        _, dtT, _, acsT, row, col = _chunk_rows(dt_ref[:, 0:SSM_HEADS], dtb_ref[...], alog_ref[...])
        tril, left = row >= col, col < SSMD
        acs = acsT.T
        w = (dtT * jnp.exp(acsT[:, CHUNK - 1:CHUNK] - acsT)).T
        cd = jnp.exp(acs[CHUNK - 1:CHUNK, :])
        for g in range(SSM_GROUPS):
            gs = slice(g * SSM_GROUP_COLS, (g + 1) * SSM_GROUP_COLS)
            Bb = xbc_ref[:, B_OFF + g * SSM_STATE:B_OFF + (g + 1) * SSM_STATE].astype(BF16)
            Cb = xbc_ref[:, C_OFF + g * SSM_STATE:C_OFF + (g + 1) * SSM_STATE].astype(BF16)
            Hg = H[:, gs]
            CH = _dot(Cb, Hg.astype(BF16))
            CB = _dot(Cb, Bb, NT)
            ys, xws = [], []
            for kk in range(PAIRS_PER_GROUP):
                k = g * PAIRS_PER_GROUP + kk
                xs_p = xbc_ref[:, k * PAIR:(k + 1) * PAIR]
                mps, ecols, wcols = [], [], []
                for j in range(2):
                    h = 2 * k + j
                    colb = _lane_bcast(acs, h)
                    L = jnp.exp(jnp.where(tril, colb - acsT[h:h + 1, :], -jnp.inf))
                    mps.append((CB * L * dtT[h:h + 1, :]).astype(BF16))
                    ecols.append(jnp.exp(colb))
                    wcols.append(_lane_bcast(w, h))
                yd = _dot(jnp.concatenate(mps, axis=1), _block_diag(xs_p, left))
                ys.append(yd + CH[:, kk * PAIR:(kk + 1) * PAIR] * jnp.where(left, ecols[0], ecols[1]))
                xws.append((xs_p * jnp.where(left, wcols[0], wcols[1])).astype(BF16))
            cd_e = jnp.concatenate([jnp.broadcast_to(cd[:, g * 8 + e:g * 8 + e + 1], (1, SSMD)) for e in range(8)], axis=1)
            H[:, gs] = Hg * cd_e + _dot(Bb, jnp.concatenate(xws, axis=1), TN)
            ypre = jnp.concatenate(ys, axis=1) + xbc_ref[:, gs] * d_ref[:, gs]
            ypre_ref[:, gs] = ypre
            z = z_ref[:, gs]
            yg = ypre * (z * _sigmoid(z))
            r = lax.rsqrt(jnp.mean(yg * yg, axis=1, keepdims=True) + RMS_EPS)
            y_ref[:, gs] = (yg * r * nw_ref[:, gs]).astype(BF16)

    vec = lambda n: _const_spec((1, n))
    colv = _const_spec((SSM_HEADS, 1))
    return pl.pallas_call(
        body, name="ssd_fwd", grid=(nc,),
        in_specs=[pl.BlockSpec((CHUNK, XBC_COLS), lambda c: (c, 0)),
                  pl.BlockSpec((CHUNK, 128), lambda c: (c, DT_OFF // 128)),
                  pl.BlockSpec((CHUNK, SSM_INNER), lambda c: (c, Z_OFF // SSM_INNER)),
                  colv, colv, vec(SSM_INNER), vec(SSM_INNER)],
        out_specs=[pl.BlockSpec((CHUNK, SSM_INNER), lambda c: (c, 0)),
                   pl.BlockSpec((CHUNK, SSM_INNER), lambda c: (c, 0)),
                   pl.BlockSpec((1, SSM_STATE, SSM_INNER), lambda c: (c, 0, 0))],
        out_shape=[jax.ShapeDtypeStruct((T, SSM_INNER), BF16), jax.ShapeDtypeStruct((T, SSM_INNER), F32),
                   jax.ShapeDtypeStruct((nc, SSM_STATE, SSM_INNER), F32)],
        scratch_shapes=[pltpu.VMEM((SSM_STATE, SSM_INNER), F32)],
        compiler_params=_cp(("arbitrary",)))(xbc, proj_tail, proj_main, dtb_col, alog_col, d_exp, norm_w)


def _ssd_bwd(dyo, ypre, xbc, dsil, hs, proj_main, proj_tail, dtb_col, alog_col, d_exp, norm_w, ehead_t, dmain, dtail):
    T = xbc.shape[0]
    nc = T // CHUNK

    def body(dyo_ref, ypre_ref, xbc_ref, dsil_ref, hs_ref, dt_ref, z_ref, dtb_ref, alog_ref, d_ref, nw_ref, eh_ref,
             dmain_in, dtail_in, dz_ref, ddt_ref, dxbc_ref, dnw_ref, dd_ref, dalog_ref, ddtb_ref, G):
        del dmain_in, dtail_in
        c = pl.program_id(0)

        @pl.when(c == 0)
        def _():
            G[...] = jnp.zeros_like(G)
            dnw_ref[...] = jnp.zeros_like(dnw_ref)
            dd_ref[...] = jnp.zeros_like(dd_ref)
            dalog_ref[...] = jnp.zeros_like(dalog_ref)
            ddtb_ref[...] = jnp.zeros_like(ddtb_ref)

        dt_rawT, dtT, a_col, acsT, row, col = _chunk_rows(dt_ref[:, 0:SSM_HEADS], dtb_ref[...], alog_ref[...])
        tril, triu, left = row >= col, col >= row, col < SSMD
        acs = acsT.T
        dt = dtT.T
        lastT = acsT[:, CHUNK - 1:CHUNK]
        dstT = jnp.exp(lastT - acsT)
        wT = dtT * dstT
        cd = jnp.exp(acs[CHUNK - 1:CHUNK, :])
        ddt_rows, rs_rows, deo_rows, dw_rows = [], [], [], []
        dd_cols, gh_cols, dnw_cols = [], [], []
        for g in range(SSM_GROUPS):
            gs = slice(g * SSM_GROUP_COLS, (g + 1) * SSM_GROUP_COLS)
            z = z_ref[:, gs]
            sz = _sigmoid(z)
            silu_z = z * sz
            ypre = ypre_ref[:, gs]
            yg = ypre * silu_z
            r = lax.rsqrt(jnp.mean(yg * yg, axis=1, keepdims=True) + RMS_EPS)
            ygn = yg * r
            dyo = dyo_ref[:, gs]
            dyn = dyo * nw_ref[:, gs]
            dnw_cols.append(jnp.sum(dyo * ygn, axis=0, keepdims=True))
            dyg = r * (dyn - ygn * jnp.mean(dyn * ygn, axis=1, keepdims=True))
            dz_ref[:, gs] = (dyg * ypre * (sz * (1.0 + z * (1.0 - sz)))).astype(dz_ref.dtype)
            dY = dyg * silu_z
            xs = xbc_ref[:, gs]
            dd_cols.append(jnp.sum(dY * xs, axis=0, keepdims=True))
            Bf = xbc_ref[:, B_OFF + g * SSM_STATE:B_OFF + (g + 1) * SSM_STATE]
            Cf = xbc_ref[:, C_OFF + g * SSM_STATE:C_OFF + (g + 1) * SSM_STATE]
            Bb, Cb = Bf.astype(BF16), Cf.astype(BF16)
            BT, CT = Bf.T, Cf.T
            CB = _dot(Cb, Bb, NT)
            CBT = _dot(Bb, Cb, NT)
            Hg = hs_ref[0, :, gs]
            Gg = G[:, gs]
            gh_cols.append(jnp.sum(Gg * Hg, axis=0, keepdims=True))
            dCB = jnp.zeros((CHUNK, CHUNK), F32)
            dxs_d, dyes, xws, wsels = [], [], [], []
            for kk in range(PAIRS_PER_GROUP):
                k = g * PAIRS_PER_GROUP + kk
                ps = slice(kk * PAIR, (kk + 1) * PAIR)
                xs_p, dY_p = xs[:, ps], dY[:, ps]
                Ls, LTs, dtcols, ecols, wcols = [], [], [], [], []
                for j in range(2):
                    h = 2 * k + j
                    colb = _lane_bcast(acs, h)
                    seg = colb - acsT[h:h + 1, :]
                    Ls.append(jnp.exp(jnp.where(tril, seg, -jnp.inf)))
                    LTs.append(jnp.exp(jnp.where(triu, -seg, -jnp.inf)))
                    dtcol = _lane_bcast(dt, h)
                    dtcols.append(dtcol)
                    ecols.append(jnp.exp(colb))
                    wcols.append(dtcol * jnp.exp(acs[CHUNK - 1:CHUNK, h:h + 1] - colb))
                wsel = jnp.where(left, wcols[0], wcols[1])
                dYe_p = dY_p * jnp.where(left, ecols[0], ecols[1])
                bdx = _block_diag(xs_p, left)
                bddy = _block_diag(dY_p, left)
                dMx2 = _dot(dY_p.astype(BF16), bdx, NT)
                dMxT2 = _dot(xs_p.astype(BF16), bddy, NT)
                Q1 = _dot(Hg[:, ps].astype(BF16), _block_diag(dYe_p, left), NT)
                Q2 = _dot(Gg[:, ps].astype(BF16), bdx, NT)
                mts = []
                for j in range(2):
                    h = 2 * k + j
                    js = slice(j * CHUNK, (j + 1) * CHUNK)
                    dMx = dMx2[:, js]
                    A = CB * Ls[j]
                    AT = CBT * LTs[j]
                    ddt_rows.append(jnp.sum(A * dMx, axis=0, keepdims=True))
                    ATd = AT * dtcols[j]
                    rs_rows.append(jnp.sum(ATd * dMxT2[:, js], axis=0, keepdims=True))
                    dCB = dCB + dMx * Ls[j] * dtT[h:h + 1, :]
                    mts.append(ATd.astype(BF16))
                    deo_rows.append(jnp.sum(CT * Q1[:, js], axis=0, keepdims=True))
                    dw_rows.append(jnp.sum(BT * Q2[:, js], axis=0, keepdims=True))
                dxs_d.append(_dot(jnp.concatenate(mts, axis=1), bddy))
                dyes.append(dYe_p.astype(BF16))
                xws.append((xs_p * wsel).astype(BF16))
                wsels.append(wsel)
            dYe_g = jnp.concatenate(dyes, axis=1)
            xw_g = jnp.concatenate(xws, axis=1)
            Hgb, Ggb, dCBb = Hg.astype(BF16), Gg.astype(BF16), dCB.astype(BF16)
            cs = slice(C_OFF + g * SSM_STATE, C_OFF + (g + 1) * SSM_STATE)
            bs = slice(B_OFF + g * SSM_STATE, B_OFF + (g + 1) * SSM_STATE)
            dxbc_ref[:, cs] = (_dot(dYe_g, Hgb, NT) + _dot(dCBb, Bb)) * dsil_ref[:, cs]
            dxbc_ref[:, bs] = (_dot(xw_g, Ggb, NT) + _dot(dCBb, Cb, TN)) * dsil_ref[:, bs]
            BG = _dot(Bb, Ggb)
            dxbc_ref[:, gs] = (jnp.concatenate(dxs_d, axis=1) + BG * jnp.concatenate(wsels, axis=1)
                               + dY * d_ref[:, gs]) * dsil_ref[:, gs]
            cd_e = jnp.concatenate([jnp.broadcast_to(cd[:, g * 8 + e:g * 8 + e + 1], (1, SSMD)) for e in range(8)], axis=1)
            G[:, gs] = Gg * cd_e + _dot(Cb, dYe_g, TN)
        dnw_ref[...] += jnp.concatenate(dnw_cols, axis=1)
        eh = eh_ref[...]
        dd_ref[...] += jnp.sum(eh * jnp.concatenate(dd_cols, axis=1), axis=1, keepdims=True)
        dcd = jnp.sum(eh * jnp.concatenate(gh_cols, axis=1), axis=1, keepdims=True)
        DDT = jnp.concatenate(ddt_rows, axis=0)
        DW = jnp.concatenate(dw_rows, axis=0)
        DWw = DW * wT
        dacsT = jnp.concatenate(rs_rows, axis=0) - DDT * dtT + jnp.concatenate(deo_rows, axis=0) - DWw
        end = jnp.sum(DWw, axis=1, keepdims=True) + dcd * jnp.exp(lastT)
        lane = lax.broadcasted_iota(jnp.int32, (SSM_HEADS, CHUNK), 1)
        dacsT = dacsT + jnp.where(lane == CHUNK - 1, end, 0.0)
        dadtT = _dot3(dacsT, tril.astype(BF16))
        ddtT = dadtT * a_col + DDT + DW * dstT
        dalog_ref[...] += jnp.sum(dadtT * dtT, axis=1, keepdims=True) * a_col
        ddt_rawT = ddtT * _sigmoid(dt_rawT + dtb_ref[...])
        ddtb_ref[...] += jnp.sum(ddt_rawT, axis=1, keepdims=True)
        ddt_ref[...] = jnp.concatenate([ddt_rawT.T, jnp.zeros((CHUNK, 128 - SSM_HEADS), F32)], axis=1).astype(ddt_ref.dtype)

    rev = lambda c: nc - 1 - c
    vec = lambda n: _const_spec((1, n))
    colv = _const_spec((SSM_HEADS, 1))
    any_spec = pl.BlockSpec(memory_space=pl.ANY)
    return pl.pallas_call(
        body, name="ssd_bwd", grid=(nc,),
        in_specs=[pl.BlockSpec((CHUNK, SSM_INNER), lambda c: (rev(c), 0)),
                  pl.BlockSpec((CHUNK, SSM_INNER), lambda c: (rev(c), 0)),
                  pl.BlockSpec((CHUNK, XBC_COLS), lambda c: (rev(c), 0)),
                  pl.BlockSpec((CHUNK, XBC_COLS), lambda c: (rev(c), 0)),
                  pl.BlockSpec((1, SSM_STATE, SSM_INNER), lambda c: (rev(c), 0, 0)),
                  pl.BlockSpec((CHUNK, 128), lambda c: (rev(c), DT_OFF // 128)),
                  pl.BlockSpec((CHUNK, SSM_INNER), lambda c: (rev(c), Z_OFF // SSM_INNER)),
                  colv, colv, vec(SSM_INNER), vec(SSM_INNER), _const_spec((SSM_HEADS, SSM_INNER)), any_spec, any_spec],
        out_specs=[pl.BlockSpec((CHUNK, SSM_INNER), lambda c: (rev(c), Z_OFF // SSM_INNER)),
                   pl.BlockSpec((CHUNK, 128), lambda c: (rev(c), DT_OFF // 128)),
                   pl.BlockSpec((CHUNK, XBC_COLS), lambda c: (rev(c), 0)),
                   vec(SSM_INNER), colv, colv, colv],
        out_shape=[jax.ShapeDtypeStruct(dmain.shape, dmain.dtype), jax.ShapeDtypeStruct(dtail.shape, dtail.dtype),
                   jax.ShapeDtypeStruct((T, XBC_COLS), F32), jax.ShapeDtypeStruct((1, SSM_INNER), F32),
                   jax.ShapeDtypeStruct((SSM_HEADS, 1), F32), jax.ShapeDtypeStruct((SSM_HEADS, 1), F32),
                   jax.ShapeDtypeStruct((SSM_HEADS, 1), F32)],
        scratch_shapes=[pltpu.VMEM((SSM_STATE, SSM_INNER), F32)],
        input_output_aliases={12: 0, 13: 1},
        compiler_params=_cp(("arbitrary",)))(dyo, ypre, xbc, dsil, hs, proj_tail, proj_main, dtb_col, alog_col, d_exp,
                                             norm_w, ehead_t, dmain, dtail)


def _rel_bucket(rel):
    n = jnp.maximum(rel, 0)
    max_exact = REL_BUCKETS // 2
    nf = jnp.maximum(n, 1).astype(F32)
    large = max_exact + (jnp.log(nf / max_exact) / math.log(REL_MAX_DIST / max_exact)
                         * (REL_BUCKETS - max_exact)).astype(jnp.int32)
    large = jnp.minimum(large, REL_BUCKETS - 1)
    return jnp.where(n < max_exact, n, large)


def _band_geometry():
    qi = jnp.arange(WINDOW)[:, None] + WINDOW
    kj = jnp.arange(2 * WINDOW)[None, :]
    rel = qi - kj
    return _rel_bucket(rel), (rel >= 0) & (rel < WINDOW)


def _attn_logits(kband, qh, bias_h, first):
    s = _dot(kband, qh, NT) * (HEADDIM ** -0.5) + bias_h
    rowk = lax.broadcasted_iota(jnp.int32, (2 * WINDOW, WINDOW), 0)
    return jnp.where(jnp.logical_and(first, rowk < WINDOW), NEG, s)


def _attn_fwd(proj_main, proj_tail, bias_tbl, sinks):
    T = proj_main.shape[0]
    nb = T // WINDOW

    def body(q_ref, kv_ref, kvp_ref, bias_ref, sink_ref, o_ref, lse_ref):
        i = pl.program_id(0)
        first = i == 0
        outs, lses = [], []
        for kvh in range(ATTN_KV):
            ks = slice(K_OFF + kvh * HEADDIM, K_OFF + (kvh + 1) * HEADDIM)
            vs = slice(V_OFF + kvh * HEADDIM, V_OFF + (kvh + 1) * HEADDIM)
            kband = jnp.concatenate([kvp_ref[:, ks], kv_ref[:, ks]], axis=0).astype(BF16)
            vband = jnp.concatenate([kvp_ref[:, vs], kv_ref[:, vs]], axis=0).astype(BF16)
            heads = range(kvh * ATTN_GROUP, (kvh + 1) * ATTN_GROUP)
            logits = [_attn_logits(kband, q_ref[:, h * HEADDIM:(h + 1) * HEADDIM].astype(BF16), bias_ref[h], first)
                      for h in heads]
            probs = []
            for h, s in zip(heads, logits):
                sink = sink_ref[:, h:h + 1]
                m = jnp.maximum(jnp.max(s, axis=0, keepdims=True), sink)
                p = jnp.exp(s - m)
                den = jnp.sum(p, axis=0, keepdims=True) + jnp.exp(sink - m)
                probs.append((p * (1.0 / den)).astype(BF16))
                lses.append(m + jnp.log(den))
            outs += [_dot(pt, vband, TN) for pt in probs]
        o_ref[...] = jnp.concatenate(outs, axis=1).astype(BF16)
        lse_ref[...] = jnp.concatenate(lses, axis=0)

    return pl.pallas_call(
        body, name="attn_fwd", grid=(nb,),
        in_specs=[pl.BlockSpec((WINDOW, D_MODEL), lambda i: (i, Q_OFF // D_MODEL)),
                  pl.BlockSpec((WINDOW, 256), lambda i: (i, 0)),
                  pl.BlockSpec((WINDOW, 256), lambda i: (jnp.maximum(i - 1, 0), 0)),
                  _const_spec((ATTN_HEADS, 2 * WINDOW, WINDOW)), _const_spec((1, ATTN_HEADS))],
        out_specs=[pl.BlockSpec((WINDOW, D_MODEL), lambda i: (i, 0)),
                   pl.BlockSpec((ATTN_HEADS, WINDOW), lambda i: (0, i))],
        out_shape=[jax.ShapeDtypeStruct((T, D_MODEL), BF16), jax.ShapeDtypeStruct((ATTN_HEADS, T), F32)],
        compiler_params=_cp(("arbitrary",)))(proj_main, proj_tail, proj_tail, bias_tbl, sinks)


def _attn_bwd(dy, lse, proj_main, proj_tail, bias_tbl, sinks, dmain):
    T = proj_main.shape[0]
    nb = T // WINDOW

    def body(dy_ref, lse_ref, q_ref, kv_ref, kvp_ref, bias_ref, sink_ref, dmain_in,
             dq_ref, dkv_ref, dbias_ref, dsink_ref, carry):
        del dmain_in
        i = pl.program_id(0)
        first = i == 0

        @pl.when(first)
        def _():
            carry[...] = jnp.zeros_like(carry)
            dbias_ref[...] = jnp.zeros_like(dbias_ref)
            dsink_ref[...] = jnp.zeros_like(dsink_ref)

        @pl.when(i < nb)
        def _():
            scale = HEADDIM ** -0.5
            dqs, dsinks, dks, dvs = [], [], [], []
            for kvh in range(ATTN_KV):
                ks = slice(K_OFF + kvh * HEADDIM, K_OFF + (kvh + 1) * HEADDIM)
                vs = slice(V_OFF + kvh * HEADDIM, V_OFF + (kvh + 1) * HEADDIM)
                kband = jnp.concatenate([kvp_ref[:, ks], kv_ref[:, ks]], axis=0).astype(BF16)
                vband = jnp.concatenate([kvp_ref[:, vs], kv_ref[:, vs]], axis=0).astype(BF16)
                heads = range(kvh * ATTN_GROUP, (kvh + 1) * ATTN_GROUP)
                qs = [q_ref[:, h * HEADDIM:(h + 1) * HEADDIM].astype(BF16) for h in heads]
                dos = [dy_ref[:, h * HEADDIM:(h + 1) * HEADDIM] for h in heads]
                logits = [_attn_logits(kband, qh, bias_ref[h], first) for h, qh in zip(heads, qs)]
                dps = [_dot(vband, do, NT) for do in dos]
                pbs, dsbs = [], []
                for h, s, dp in zip(heads, logits, dps):
                    lse_h = lse_ref[h:h + 1, :]
                    p = jnp.exp(s - lse_h)
                    delta = jnp.sum(p * dp, axis=0, keepdims=True)
                    ds = p * (dp - delta)
                    psink = jnp.exp(sink_ref[:, h:h + 1] - lse_h)
                    dsinks.append(-jnp.sum(psink * delta, axis=1, keepdims=True))
                    dbias_ref[h] += ds
                    pbs.append(p.astype(BF16))
                    dsbs.append((ds * scale).astype(BF16))
                dqs += [_dot(dsb, kband, TN) for dsb in dsbs]
                dks.append(_dot(jnp.concatenate(dsbs, axis=1), jnp.concatenate(qs, axis=0)))
                dvs.append(_dot(jnp.concatenate(pbs, axis=1), jnp.concatenate(dos, axis=0)))
            dq_ref[...] = jnp.concatenate(dqs, axis=1).astype(dq_ref.dtype)
            dsink_ref[...] += jnp.concatenate(dsinks, axis=1)
            dkv = jnp.concatenate(dks + dvs, axis=1)
            dkv_ref[...] = (carry[...] + dkv[0:WINDOW, :]).astype(dkv_ref.dtype)
            carry[...] = dkv[WINDOW:, :]

        @pl.when(i == nb)
        def _():
            dkv_ref[...] = carry[...].astype(dkv_ref.dtype)

    cur = lambda i: jnp.minimum(i, nb - 1)
    return pl.pallas_call(
        body, name="attn_bwd", grid=(nb + 1,),
        in_specs=[pl.BlockSpec((WINDOW, D_MODEL), lambda i: (cur(i), 0)),
                  pl.BlockSpec((ATTN_HEADS, WINDOW), lambda i: (0, cur(i))),
                  pl.BlockSpec((WINDOW, D_MODEL), lambda i: (cur(i), Q_OFF // D_MODEL)),
                  pl.BlockSpec((WINDOW, 256), lambda i: (cur(i), 0)),
                  pl.BlockSpec((WINDOW, 256), lambda i: (jnp.maximum(cur(i) - 1, 0), 0)),
                  _const_spec((ATTN_HEADS, 2 * WINDOW, WINDOW)), _const_spec((1, ATTN_HEADS)),
                  pl.BlockSpec(memory_space=pl.ANY)],
        out_specs=[pl.BlockSpec((WINDOW, D_MODEL), lambda i: (cur(i), Q_OFF // D_MODEL)),
                   pl.BlockSpec((WINDOW, 256), lambda i: (jnp.maximum(i - 1, 0), 0)),
                   _const_spec((ATTN_HEADS, 2 * WINDOW, WINDOW)), _const_spec((1, ATTN_HEADS))],
        out_shape=[jax.ShapeDtypeStruct(dmain.shape, dmain.dtype), jax.ShapeDtypeStruct((T, TAIL_COLS), BF16),
                   jax.ShapeDtypeStruct((ATTN_HEADS, 2 * WINDOW, WINDOW), F32),
                   jax.ShapeDtypeStruct((1, ATTN_HEADS), F32)],
        scratch_shapes=[pltpu.VMEM((WINDOW, 256), F32)],
        input_output_aliases={7: 0},
        compiler_params=_cp(("arbitrary",)))(dy, lse, proj_main, proj_tail, proj_tail, bias_tbl, sinks, dmain)


def _bias_table(rel_bias_t, onehot_t, mask):
    def body(rb_ref, oh_ref, m_ref, o_ref):
        o_ref[...] = _dot3(rb_ref[...], oh_ref[...]) + m_ref[...]

    flat = pl.pallas_call(body, name="bias_table",
                          out_shape=jax.ShapeDtypeStruct((ATTN_HEADS, 2 * WINDOW * WINDOW), F32))(rel_bias_t, onehot_t, mask)
    return flat.reshape(ATTN_HEADS, 2 * WINDOW, WINDOW)


def _rel_bias_grad(dbias, onehot):
    def body(d_ref, oh_ref, o_ref):
        o_ref[...] = _dot(d_ref[...], oh_ref[...], NN, HIGHEST)

    return pl.pallas_call(body, name="rel_bias_grad",
                          out_shape=jax.ShapeDtypeStruct((ATTN_HEADS, REL_BUCKETS), F32))(dbias, onehot)


def _ln_fwd(r, g, b):
    mu = jnp.mean(r, axis=1, keepdims=True)
    xc = r - mu
    rstd = lax.rsqrt(jnp.mean(xc * xc, axis=1, keepdims=True) + LN_EPS)
    xhat = xc * rstd
    return xhat * g + b, xhat, rstd


def _ln_bwd(dy, xhat, rstd, g):
    dxh = dy * g
    return rstd * (dxh - jnp.mean(dxh, axis=1, keepdims=True) - xhat * jnp.mean(dxh * xhat, axis=1, keepdims=True))


def _merge_fwd(y_ssm, y_attn, proj_main, b_gate, w_bs, w_ba, tm=512):
    T = y_ssm.shape[0]

    def body(ys_ref, ya_ref, gs_ref, ga_ref, bg_ref, wbs_ref, wba_ref, m_ref, bs_ref, ba_ref):
        bs = _dot(ys_ref[...], wbs_ref[...])
        ba = _dot(ya_ref[...], wba_ref[...])
        g_s = _sigmoid(gs_ref[...] + bg_ref[:, 0:D_MODEL])
        g_a = _sigmoid(ga_ref[...] + bg_ref[:, D_MODEL:])
        m_ref[...] = (g_s * bs + g_a * ba).astype(BF16)
        bs_ref[...] = bs
        ba_ref[...] = ba

    row = lambda w, off=0: pl.BlockSpec((tm, w), lambda i: (i, off))
    return pl.pallas_call(
        body, name="merge_fwd", grid=(T // tm,),
        in_specs=[row(SSM_INNER), row(D_MODEL), row(D_MODEL, GATE_OFF // D_MODEL), row(D_MODEL, GATE_OFF // D_MODEL + 1),
                  _const_spec((1, 2 * D_MODEL)), _const_spec((SSM_INNER, D_MODEL)), _const_spec((D_MODEL, D_MODEL))],
        out_specs=[row(D_MODEL), row(D_MODEL), row(D_MODEL)],
        out_shape=[jax.ShapeDtypeStruct((T, D_MODEL), BF16), jax.ShapeDtypeStruct((T, D_MODEL), F32),
                   jax.ShapeDtypeStruct((T, D_MODEL), F32)],
        compiler_params=_cp(("parallel",)))(y_ssm, y_attn, proj_main, proj_main, b_gate, w_bs, w_ba)


def _mix_ln1(merged, w_mo, x, g1, b1, tm=512):
    T = x.shape[0]

    def body(m_ref, w_ref, x_ref, g_ref, b_ref, r_ref, h_ref, hb_ref):
        r = ALPHA * x_ref[...] + _dot(m_ref[...], w_ref[...])
        r_ref[...] = r
        h = _ln_fwd(r, g_ref[...], b_ref[...])[0]
        h_ref[...] = h
        hb_ref[...] = h.astype(BF16)

    row = pl.BlockSpec((tm, D_MODEL), lambda i: (i, 0))
    return pl.pallas_call(
        body, name="mix_ln1", grid=(T // tm,),
        in_specs=[row, _const_spec((D_MODEL, D_MODEL)), row, _const_spec((1, D_MODEL)), _const_spec((1, D_MODEL))],
        out_specs=[row, row, row],
        out_shape=[jax.ShapeDtypeStruct((T, D_MODEL), F32), jax.ShapeDtypeStruct((T, D_MODEL), F32),
                   jax.ShapeDtypeStruct((T, D_MODEL), BF16)],
        compiler_params=_cp(("parallel",)))(merged, w_mo, x, g1, b1)


def _ffn_conv_glu(u_pre, w, b, tr=2048, tc=256):
    T = u_pre.shape[0]
    tr = min(tr, T)
    K = FFN_CONV
    nj = D_FF // tc
    hb = tr // HALO
    assert T % tr == 0 and D_FF % tc == 0

    def body(xg_ref, xgp_ref, xv_ref, xvp_ref, wg_ref, wv_ref, bg_ref, bv_ref, u_ref, a_ref, head_g, head_v):
        i = pl.program_id(1)
        halves = []
        for x_ref, xp_ref, w_ref, b_ref, head in ((xg_ref, xgp_ref, wg_ref, bg_ref, head_g),
                                                  (xv_ref, xvp_ref, wv_ref, bv_ref, head_v)):
            head[0:HALO, :] = jnp.where(i > 0, xp_ref[...], 0.0)
            head[HALO:, :] = x_ref[0:HALO, :]
            halves.append((x_ref, head, _taps(w_ref, K, tc), jnp.broadcast_to(b_ref[...], (HALO, tc))))

        def conv(half, r):
            x_ref, head, wk, bias = halves[half]
            lo = r * HALO
            acc = bias + wk[K - 1] * x_ref[lo:lo + HALO, :]
            for k in range(K - 1):
                s = K - 1 - k
                acc = acc + wk[k] * (head[HALO - s:2 * HALO - s, :] if r == 0 else x_ref[lo - s:lo + HALO - s, :])
            return acc

        for r2 in range(tr // (2 * HALO)):
            acts = []
            for r in (2 * r2, 2 * r2 + 1):
                lo = r * HALO
                ug, uv = conv(0, r), conv(1, r)
                u_ref[0, lo:lo + HALO, :] = ug
                u_ref[1, lo:lo + HALO, :] = uv
                acts.append(ug * _sigmoid(ug) * uv)
            a_ref[2 * r2 * HALO:(2 * r2 + 2) * HALO, :] = jnp.concatenate(acts, axis=0).astype(BF16)

    tile = lambda off: pl.BlockSpec((tr, tc), lambda j, i: (i, off + j))
    prev = lambda off: pl.BlockSpec((HALO, tc), lambda j, i: (jnp.maximum(i * hb - 1, 0), off + j))
    row = lambda rows, off: pl.BlockSpec((rows, tc), lambda j, i: (0, off + j))
    return pl.pallas_call(
        body, name="ffn_conv_glu", grid=(nj, T // tr),
        in_specs=[tile(0), prev(0), tile(nj), prev(nj), row(K, 0), row(K, nj), row(1, 0), row(1, nj)],
        out_specs=[pl.BlockSpec((2, tr, tc), lambda j, i: (0, i, j)), pl.BlockSpec((tr, tc), lambda j, i: (i, j))],
        out_shape=[jax.ShapeDtypeStruct((2, T, D_FF), F32), jax.ShapeDtypeStruct((T, D_FF), BF16)],
        scratch_shapes=[pltpu.VMEM((2 * HALO, tc), F32), pltpu.VMEM((2 * HALO, tc), F32)],
        compiler_params=_cp(("parallel", "arbitrary")))(u_pre, u_pre, u_pre, u_pre, w, w, b, b)


def _down_ln2_loss(act, w_down, h1, target, g2, b2, tm=512):
    T = h1.shape[0]

    def body(a_ref, w_ref, h_ref, t_ref, g_ref, b_ref, dr_ref, drb_ref, dg_ref, db_ref, l_ref):
        @pl.when(pl.program_id(0) == 0)
        def _():
            dg_ref[...] = jnp.zeros_like(dg_ref)
            db_ref[...] = jnp.zeros_like(db_ref)
            l_ref[...] = jnp.zeros_like(l_ref)

        r = ALPHA * h_ref[...] + _dot(a_ref[...], w_ref[...])
        y, xhat, rstd = _ln_fwd(r, g_ref[...], b_ref[...])
        err = y - t_ref[...]
        l_ref[...] += jnp.sum(err * err, keepdims=True)
        dy = err * (1.0 / D_MODEL)
        dg_ref[...] += jnp.sum(dy * xhat, axis=0, keepdims=True)
        db_ref[...] += jnp.sum(dy, axis=0, keepdims=True)
        dr = _ln_bwd(dy, xhat, rstd, g_ref[...])
        dr_ref[...] = dr
        drb_ref[...] = dr.astype(BF16)

    row = pl.BlockSpec((tm, D_MODEL), lambda i: (i, 0))
    vec = _const_spec((1, D_MODEL))
    return pl.pallas_call(
        body, name="down_ln2_loss", grid=(T // tm,),
        in_specs=[pl.BlockSpec((tm, D_FF), lambda i: (i, 0)), _const_spec((D_FF, D_MODEL)), row, row, vec, vec],
        out_specs=[row, row, vec, vec, _const_spec((1, 1))],
        out_shape=[jax.ShapeDtypeStruct((T, D_MODEL), F32), jax.ShapeDtypeStruct((T, D_MODEL), BF16),
                   jax.ShapeDtypeStruct((1, D_MODEL), F32), jax.ShapeDtypeStruct((1, D_MODEL), F32),
                   jax.ShapeDtypeStruct((1, 1), F32)],
        compiler_params=_cp(("arbitrary",)))(act, w_down, h1, target, g2, b2)


def _ffn_gate_conv_bwd(dact, u, u_pre, w, tr=2048, tc=256):
    T = dact.shape[0]
    tr = min(tr, T)
    K = FFN_CONV
    nj, nt, n, hb = D_FF // tc, T // tr, tr // HALO, tr // HALO
    last_hblock = T // HALO - 1
    assert T % tr == 0 and D_FF % tc == 0 and n % 2 == 0

    def body(da_ref, dan_ref, u_ref, un_ref, xg_ref, xv_ref, wg_ref, wv_ref,
             o_ref, dwg_ref, dwv_ref, dbg_ref, dbv_ref, gext_g, gext_v):
        i = pl.program_id(1)
        for r in range(n + 1):
            rows = slice(r * HALO, (r + 1) * HALO)
            if r < n:
                da, g, v = da_ref[rows, :], u_ref[0, rows, :], u_ref[1, rows, :]
            else:
                da, g, v = jnp.where(i < nt - 1, dan_ref[...], 0.0), un_ref[0], un_ref[1]
            sg = _sigmoid(g)
            gext_g[rows, :] = da * v * (sg * (1.0 + g * (1.0 - sg)))
            gext_v[rows, :] = da * (g * sg)
        for half, (gext, x_ref, w_ref, dw_ref, db_ref) in enumerate(((gext_g, xg_ref, wg_ref, dwg_ref, dbg_ref),
                                                                      (gext_v, xv_ref, wv_ref, dwv_ref, dbv_ref))):
            wk = _taps(w_ref, K, tc)
            acc_w = [jnp.zeros((HALO, tc), F32) for _ in range(K)]
            acc_b = jnp.zeros((HALO, tc), F32)
            for r2 in range(n // 2):
                pair = []
                for r in (2 * r2, 2 * r2 + 1):
                    lo = r * HALO
                    x = x_ref[lo:lo + HALO, :]
                    dpre = None
                    for s in range(K):
                        gs = gext[lo + s:lo + HALO + s, :]
                        dpre = wk[K - 1 - s] * gs if dpre is None else dpre + wk[K - 1 - s] * gs
                        acc_w[K - 1 - s] = acc_w[K - 1 - s] + gs * x
                        if s == 0:
                            acc_b = acc_b + gs
                    pair.append(dpre)
                o_ref[half, 2 * r2 * HALO:(2 * r2 + 2) * HALO, :] = jnp.concatenate(pair, axis=0).astype(o_ref.dtype)

            @pl.when(i == 0)
            def _():
                dw_ref[...] = jnp.zeros_like(dw_ref)
                db_ref[...] = jnp.zeros_like(db_ref)

            db_ref[...] += jnp.sum(acc_b, axis=0, keepdims=True)
            dw_ref[...] += jnp.concatenate([jnp.sum(a, axis=0, keepdims=True) for a in acc_w], axis=0)

    nxt = lambda i: jnp.minimum((i + 1) * hb, last_hblock)
    taps = lambda off: pl.BlockSpec((K, tc), lambda j, i: (0, off + j))
    dw_spec, db_spec = pl.BlockSpec((K, tc), lambda j, i: (0, j)), pl.BlockSpec((1, tc), lambda j, i: (0, j))
    du_pre, dwg, dwv, dbg, dbv = pl.pallas_call(
        body, name="ffn_gate_conv_bwd", grid=(nj, nt),
        in_specs=[pl.BlockSpec((tr, tc), lambda j, i: (i, j)), pl.BlockSpec((HALO, tc), lambda j, i: (nxt(i), j)),
                  pl.BlockSpec((2, tr, tc), lambda j, i: (0, i, j)), pl.BlockSpec((2, HALO, tc), lambda j, i: (0, nxt(i), j)),
                  pl.BlockSpec((tr, tc), lambda j, i: (i, j)), pl.BlockSpec((tr, tc), lambda j, i: (i, nj + j)),
                  taps(0), taps(nj)],
        out_specs=[pl.BlockSpec((2, tr, tc), lambda j, i: (0, i, j)), dw_spec, dw_spec, db_spec, db_spec],
        out_shape=[jax.ShapeDtypeStruct((2, T, D_FF), BF16), jax.ShapeDtypeStruct((K, D_FF), F32),
                   jax.ShapeDtypeStruct((K, D_FF), F32), jax.ShapeDtypeStruct((1, D_FF), F32),
                   jax.ShapeDtypeStruct((1, D_FF), F32)],
        scratch_shapes=[pltpu.VMEM((tr + HALO, tc), F32), pltpu.VMEM((tr + HALO, tc), F32)],
        compiler_params=_cp(("parallel", "arbitrary")))(dact, dact, u, u, u_pre, u_pre, w, w)
    return du_pre, jnp.concatenate([dwg, dwv], axis=1), jnp.concatenate([dbg, dbv], axis=1)


def _ffn_bwd_in(du_pre, w_up, dr2, r1, g1, b1, tm=1024, tk=1408):
    T = dr2.shape[0]
    tm = min(tm, T)
    assert T % tm == 0 and D_FF % tk == 0
    nk = 2 * D_FF // tk
    kh = D_FF // tk

    def body(d_ref, w_ref, dr2_ref, r_ref, g_ref, b_ref, dr1_ref, dr1b_ref, dg_ref, db_ref, acc):
        i, k = pl.program_id(0), pl.program_id(1)

        @pl.when(jnp.logical_and(i == 0, k == 0))
        def _():
            dg_ref[...] = jnp.zeros_like(dg_ref)
            db_ref[...] = jnp.zeros_like(db_ref)

        @pl.when(k == 0)
        def _():
            acc[...] = ALPHA * dr2_ref[...]

        acc[...] += _dot(d_ref[...], w_ref[...], NT)

        @pl.when(k == nk - 1)
        def _():
            _, xhat, rstd = _ln_fwd(r_ref[...], g_ref[...], b_ref[...])
            dy = acc[...]
            dg_ref[...] += jnp.sum(dy * xhat, axis=0, keepdims=True)
            db_ref[...] += jnp.sum(dy, axis=0, keepdims=True)
            dr1 = _ln_bwd(dy, xhat, rstd, g_ref[...])
            dr1_ref[...] = dr1
            dr1b_ref[...] = dr1.astype(BF16)

    row = pl.BlockSpec((tm, D_MODEL), lambda i, k: (i, 0))
    vec = _const_spec((1, D_MODEL))
    return pl.pallas_call(
        body, name="ffn_bwd_in", grid=(T // tm, nk),
        in_specs=[pl.BlockSpec((None, tm, tk), lambda i, k: (k // kh, i, k % kh)),
                  pl.BlockSpec((D_MODEL, tk), lambda i, k: (0, k)), row, row, vec, vec],
        out_specs=[row, row, vec, vec],
        out_shape=[jax.ShapeDtypeStruct((T, D_MODEL), F32), jax.ShapeDtypeStruct((T, D_MODEL), BF16),
                   jax.ShapeDtypeStruct((1, D_MODEL), F32), jax.ShapeDtypeStruct((1, D_MODEL), F32)],
        scratch_shapes=[pltpu.VMEM((tm, D_MODEL), F32)],
        compiler_params=_cp(("arbitrary", "arbitrary")))(du_pre, w_up, dr2, r1, g1, b1)


def _mix_bwd(dr1, w_mo, w_bs, w_ba, bs, ba, proj_main, b_gate, tm=512):
    T = dr1.shape[0]

    def body(d_ref, wmo_ref, wbs_ref, wba_ref, bs_ref, ba_ref, gs_ref, ga_ref, bg_ref,
             dg_ref, dbs_ref, dba_ref, dys_ref, dya_ref, dbg_ref):
        @pl.when(pl.program_id(0) == 0)
        def _():
            dbg_ref[...] = jnp.zeros_like(dbg_ref)

        dm = _dot(d_ref[...].astype(BF16), wmo_ref[...], NT)
        g_s = _sigmoid(gs_ref[...] + bg_ref[:, 0:D_MODEL])
        g_a = _sigmoid(ga_ref[...] + bg_ref[:, D_MODEL:])
        dgs = dm * bs_ref[...] * g_s * (1.0 - g_s)
        dga = dm * ba_ref[...] * g_a * (1.0 - g_a)
        dg_ref[:, 0:D_MODEL] = dgs.astype(BF16)
        dg_ref[:, D_MODEL:] = dga.astype(BF16)
        dbg_ref[:, 0:D_MODEL] += jnp.sum(dgs, axis=0, keepdims=True)
        dbg_ref[:, D_MODEL:] += jnp.sum(dga, axis=0, keepdims=True)
        dbs = (dm * g_s).astype(BF16)
        dba = (dm * g_a).astype(BF16)
        dbs_ref[...] = dbs
        dba_ref[...] = dba
        dys_ref[...] = _dot(dbs, wbs_ref[...], NT)
        dya_ref[...] = _dot(dba, wba_ref[...], NT).astype(BF16)

    row = lambda w, off=0: pl.BlockSpec((tm, w), lambda i: (i, off))
    return pl.pallas_call(
        body, name="mix_bwd", grid=(T // tm,),
        in_specs=[row(D_MODEL), _const_spec((D_MODEL, D_MODEL)), _const_spec((SSM_INNER, D_MODEL)),
                  _const_spec((D_MODEL, D_MODEL)), row(D_MODEL), row(D_MODEL),
                  row(D_MODEL, GATE_OFF // D_MODEL), row(D_MODEL, GATE_OFF // D_MODEL + 1), _const_spec((1, 2 * D_MODEL))],
        out_specs=[row(2 * D_MODEL, GATE_OFF // (2 * D_MODEL)), row(D_MODEL), row(D_MODEL), row(SSM_INNER), row(D_MODEL),
                   _const_spec((1, 2 * D_MODEL))],
        out_shape=[jax.ShapeDtypeStruct((T, MAIN_COLS), BF16), jax.ShapeDtypeStruct((T, D_MODEL), BF16),
                   jax.ShapeDtypeStruct((T, D_MODEL), BF16), jax.ShapeDtypeStruct((T, SSM_INNER), F32),
                   jax.ShapeDtypeStruct((T, D_MODEL), BF16), jax.ShapeDtypeStruct((1, 2 * D_MODEL), F32)],
        compiler_params=_cp(("arbitrary",)))(dr1, w_mo, w_bs, w_ba, bs, ba, proj_main, proj_main, b_gate)


def _local_step(x, target, w, p, late_weights=None, early_grads=None):
    xb = x.astype(BF16)
    if late_weights is None:
        proj_main = _matmul(xb, w["in_main"], "nn", F32, "in_proj_main", tm=1024, tn=2048)
    else:
        proj_main, *landed = _matmul(xb, w["in_main"], "nn", F32, "in_proj_main", tm=1024, tn=2048,
                                     comm=("gather", late_weights[0]))
        w = {**w, **late_weights[1](landed)}
    proj_tail = _matmul(xb, w["in_tail"], "nn", F32, "in_proj_tail", tn=TAIL_COLS)
    xbc, dsil = _conv_silu_fwd(proj_main, XBC_OFF, XBC_COLS, p["ssm_conv_w"], p["ssm_conv_b"], SSM_CONV, "ssm_conv_fwd")
    y_ssm, ypre, hs = _ssd_fwd(xbc, proj_main, proj_tail, p["dtb_col"], p["alog_col"], p["d_exp"], p["ssm_norm_w"])
    y_attn, lse = _attn_fwd(proj_main, proj_tail, p["bias_tbl"], p["attn_sinks"])
    merged, bs, ba = _merge_fwd(y_ssm, y_attn, proj_main, p["b_gate"], w["bs"], w["ba"])
    r1, h1, h1b = _mix_ln1(merged, w["mo"], x, p["ln1_g"], p["ln1_b"])
    u_pre = _matmul(h1b, w["up"], "nn", F32, "ffn_up", tm=1024, tn=1408)
    u, act = _ffn_conv_glu(u_pre, p["ffn_conv_w"], p["ffn_conv_b"])
    dr2, dr2b, dg2, db2, sq = _down_ln2_loss(act, w["down"], h1, target, p["ln2_g"], p["ln2_b"])
    g = {"ln2_g": dg2, "ln2_b": db2}
    g["w_down"] = _matmul(act, dr2b, "tn", BF16, "dw_down", tm=1408, tn=1024, tk=2048)
    dact = _matmul(dr2b, w["down"], "nt", F32, "ffn_dact", tm=1024, tn=1408, tk=1024)
    du_pre, g["ffn_conv_w"], g["ffn_conv_b"] = _ffn_gate_conv_bwd(dact, u, u_pre, p["ffn_conv_w"])
    g["w_up"] = _matmul(h1b, du_pre, "tn", BF16, "dw_up", tm=1024, tn=1408, tk=2048)
    dr1, dr1b, g["ln1_g"], g["ln1_b"] = _ffn_bwd_in(du_pre, w["up"], dr2, r1, p["ln1_g"], p["ln1_b"])
    g["w_mix_out"] = _matmul(merged, dr1b, "tn", BF16, "dw_mix_out", tm=1024, tn=1024, tk=2048)
    dmain, dbs, dba, dy_ssm, dy_attn, g["b_gate"] = _mix_bwd(dr1b, w["mo"], w["bs"], w["ba"], bs, ba, proj_main, p["b_gate"])
    g["w_branch_ssm"] = _matmul(y_ssm, dbs, "tn", BF16, "dw_branch_ssm", tm=1024, tn=1024, tk=2048)
    g["w_branch_attn"] = _matmul(y_attn, dba, "tn", BF16, "dw_branch_attn", tm=1024, tn=1024, tk=2048)
    dmain, dtail, dbias, g["attn_sinks"] = _attn_bwd(dy_attn, lse, proj_main, proj_tail, p["bias_tbl"], p["attn_sinks"], dmain)
    g["rel_bias"] = _rel_bias_grad(dbias.reshape(ATTN_HEADS, WINDOW * 2 * WINDOW), p["bucket_onehot"]).T
    dmain, dtail, dco, g["ssm_norm_w"], dd, dalog, ddtb = _ssd_bwd(
        dy_ssm, ypre, xbc, dsil, hs, proj_main, proj_tail, p["dtb_col"], p["alog_col"], p["d_exp"], p["ssm_norm_w"],
        p["ehead_t"], dmain, dtail)
    g["ssm_d"], g["ssm_a_log"], g["ssm_dt_bias"] = (a.reshape(1, SSM_HEADS) for a in (dd, dalog, ddtb))
    dmain, g["ssm_conv_w"], g["ssm_conv_b"] = _conv_bwd(
        dco, proj_main, XBC_OFF, XBC_COLS, p["ssm_conv_w"], SSM_CONV, dmain, XBC_OFF, "ssm_conv_bwd")
    g["in_tail"] = _matmul(xb, dtail, "tn", BF16, "dw_in_tail", tm=1024, tn=TAIL_COLS, tk=2048)
    landed = []
    if early_grads is None:
        g["in_main"] = _matmul(xb, dmain, "tn", BF16, "dw_in_main", tm=1024, tn=1024, tk=2048)
    else:
        g["in_main"], *landed = _matmul(xb, dmain, "tn", BF16, "dw_in_main", tm=1024, tn=1024, tk=2048,
                                        comm=("exchange", early_grads(g)))
    return sq, (dmain, dtail, dr1), w, g, landed


def _grad_x(dproj, w, exchange=None):
    dmain, dtail, dr1 = dproj
    landed = None
    if exchange is None:
        dx = _matmul(dmain, w["in_main"], "nt", F32, "dx_main", tm=1024, tk=2048, addend=dr1, addend_scale=ALPHA)
    else:
        dx, landed = _matmul(dmain, w["in_main"], "nt", F32, "dx_main", tm=1024, tk=2048, addend=dr1,
                             addend_scale=ALPHA, comm=("exchange", exchange))
    dx = _matmul(dtail, w["in_tail"], "nt", F32, "dx_tail", tk=TAIL_COLS, addend=dx)
    return dx if exchange is None else (dx, landed)


SHARD_COLS = IN_COLS // N_DEV
W_IN_SEGMENTS = ((O_Z, 2048, "main", Z_OFF), (O_XBC, XBC_COLS, "main", XBC_OFF), (O_DT, SSM_HEADS, "tail", DT_OFF),
                 (O_Q, D_MODEL, "main", Q_OFF), (O_K, 128, "tail", K_OFF), (O_V, 128, "tail", V_OFF),
                 (O_GATE, 2 * D_MODEL, "main", GATE_OFF))


def _w_in_from_shards(shards):
    def seg(off, n):
        pieces = []
        for j in range(off // SHARD_COLS, (off + n - 1) // SHARD_COLS + 1):
            lo, hi = max(off, j * SHARD_COLS), min(off + n, (j + 1) * SHARD_COLS)
            pieces.append(shards[j, :, lo - j * SHARD_COLS:hi - j * SHARD_COLS])
        return pieces

    by_name = {(where, koff): seg(off, n) for off, n, where, koff in W_IN_SEGMENTS}
    main = jnp.concatenate(by_name["main", Z_OFF] + by_name["main", XBC_OFF] + by_name["main", Q_OFF]
                           + by_name["main", GATE_OFF], axis=1)
    tail = jnp.concatenate(by_name["tail", K_OFF] + by_name["tail", V_OFF] + by_name["tail", DT_OFF]
                           + [jnp.zeros((shards.shape[1], 128 - SSM_HEADS), shards.dtype)], axis=1)
    return main, tail


def _w_in_grad_by_device(g_main, g_tail):
    slots = []
    for j in range(N_DEV):
        a, b = j * SHARD_COLS, (j + 1) * SHARD_COLS
        pieces = []
        for off, n, where, koff in W_IN_SEGMENTS:
            lo, hi = max(a, off), min(b, off + n)
            if lo < hi:
                pieces.append((g_main if where == "main" else g_tail)[:, koff + lo - off:koff + hi - off])
        slots.append(jnp.concatenate(pieces, axis=1))
    return jnp.stack(slots)


def _prep_params(rel_bias, b_gate, ssm_conv_w, ssm_conv_b, ssm_dt_bias, ssm_a_log, ssm_d, ssm_norm_w, attn_sinks,
                 ln1_g, ln1_b, ffn_conv_w, ffn_conv_b, ln2_g, ln2_b):
    bucket, in_window = _band_geometry()
    bucket, in_window = bucket.T, in_window.T
    onehot = jnp.logical_and(bucket.reshape(-1, 1) == jnp.arange(REL_BUCKETS)[None, :],
                             in_window.reshape(-1, 1)).astype(F32)
    onehot_t = jnp.logical_and(bucket.reshape(1, -1) == jnp.arange(REL_BUCKETS)[:, None],
                               in_window.reshape(1, -1)).astype(BF16)
    bias_tbl = _bias_table(rel_bias.T, onehot_t, jnp.where(in_window.reshape(1, -1), 0.0, NEG))
    ehead_t = (jnp.arange(SSM_INNER)[None, :] // SSMD == jnp.arange(SSM_HEADS)[:, None]).astype(F32)
    return {"bias_tbl": bias_tbl, "bucket_onehot": onehot, "b_gate": b_gate, "ssm_conv_w": ssm_conv_w,
            "ssm_conv_b": ssm_conv_b, "dtb_col": ssm_dt_bias.reshape(SSM_HEADS, 1),
            "alog_col": ssm_a_log.reshape(SSM_HEADS, 1), "ehead_t": ehead_t,
            "d_exp": jnp.repeat(ssm_d, SSMD, axis=1), "ssm_norm_w": ssm_norm_w, "attn_sinks": attn_sinks,
            "ln1_g": ln1_g, "ln1_b": ln1_b, "ffn_conv_w": ffn_conv_w, "ffn_conv_b": ffn_conv_b,
            "ln2_g": ln2_g, "ln2_b": ln2_b}


def _all_gather(shards, name):
    nb = len(shards)

    def body(*refs):
        for phase in _gather_phases(refs[:nb], refs[nb:2 * nb], *refs[2 * nb:]):
            phase()

    any_spec = pl.BlockSpec(memory_space=pl.ANY)
    return pl.pallas_call(
        body, name=name, out_shape=[jax.ShapeDtypeStruct((N_DEV,) + s.shape, s.dtype) for s in shards],
        in_specs=[any_spec] * nb, out_specs=[any_spec] * nb, scratch_shapes=_comm_sems(nb))(*shards)


def _adamw_math(w, g, m, v):
    m = ADAM_B1 * m + (1.0 - ADAM_B1) * g
    v = ADAM_B2 * v + (1.0 - ADAM_B2) * (g * g)
    m_hat = m / (1.0 - ADAM_B1 ** ADAM_STEP)
    v_hat = v / (1.0 - ADAM_B2 ** ADAM_STEP)
    return -ADAM_LR * (m_hat / (jnp.sqrt(v_hat) + ADAM_EPS) + ADAM_WD * w), m, v


def _slot_total(s_ref):
    g = s_ref[0].astype(F32)
    for i in range(1, N_DEV):
        g = g + s_ref[i].astype(F32)
    return g


def _adamw(landed, w, m, v, name):
    R, C = w.shape
    tr = 256 if R % 256 == 0 and R > 256 else R

    def body(s_ref, w_ref, m_ref, v_ref, g_ref, d_ref, nm_ref, nv_ref):
        g = _slot_total(s_ref)
        g_ref[...] = g
        d_ref[...], nm_ref[...], nv_ref[...] = _adamw_math(w_ref[...], g, m_ref[...], v_ref[...])

    spec = pl.BlockSpec((tr, C), lambda i: (i, 0))
    return pl.pallas_call(
        body, name=name, grid=(R // tr,), in_specs=[pl.BlockSpec((N_DEV, tr, C), lambda i: (0, i, 0))] + [spec] * 3,
        out_specs=[spec] * 4, out_shape=[jax.ShapeDtypeStruct((R, C), F32)] * 4,
        compiler_params=_cp(("parallel",)))(landed, w, m, v)


def _small_update(landed, ws, ms, vs):
    k = len(ws)

    def body(*refs):
        s_ref, w_refs, m_refs, v_refs = refs[0], refs[1:1 + k], refs[1 + k:1 + 2 * k], refs[1 + 2 * k:1 + 3 * k]
        outs = refs[1 + 3 * k:]
        g_all = _slot_total(s_ref)
        for i in range(k):
            n = w_refs[i].shape[1]
            g = g_all[i:i + 1, 0:n]
            outs[i][...] = g
            outs[k + i][...], outs[2 * k + i][...], outs[3 * k + i][...] = _adamw_math(
                w_refs[i][...], g, m_refs[i][...], v_refs[i][...])

    return pl.pallas_call(body, name="small_update",
                          out_shape=[jax.ShapeDtypeStruct(w.shape, F32) for w in ws] * 4)(landed, *ws, *ms, *vs)


SHARDED = {"w_in": "cols", "w_branch_ssm": "rows", "w_branch_attn": "rows", "w_mix_out": "rows", "w_up": "cols",
           "w_down": "rows", "ssm_conv_w": "cols", "ffn_conv_w": "cols"}
LATE = ("w_branch_ssm", "w_branch_attn", "w_mix_out", "w_up", "w_down")
SHORT = {"w_branch_ssm": "bs", "w_branch_attn": "ba", "w_mix_out": "mo", "w_up": "up", "w_down": "down"}
SMALL = ("rel_bias", "b_gate", "ssm_conv_b", "ssm_dt_bias", "ssm_a_log", "ssm_d", "ssm_norm_w", "attn_sinks",
         "ln1_g", "ln1_b", "ffn_conv_b", "ln2_g", "ln2_b")
WEIGHTS = ("rel_bias", "w_in", "b_gate", "ssm_conv_w", "ssm_conv_b", "ssm_dt_bias", "ssm_a_log", "ssm_d", "ssm_norm_w",
           "attn_sinks", "w_branch_ssm", "w_branch_attn", "w_mix_out", "ln1_g", "ln1_b", "w_up", "ffn_conv_w",
           "ffn_conv_b", "w_down", "ln2_g", "ln2_b")
SMALL_ROWS, SMALL_COLS = 16, 2 * D_FF


def _by_device(full, how):
    r, c = full.shape
    if how == "rows":
        return full.reshape(N_DEV, r // N_DEV, c)
    return full.reshape(r, N_DEV, c // N_DEV).transpose(1, 0, 2)


def _from_devices(slots, how):
    _, r, c = slots.shape
    if how == "rows":
        return slots.reshape(N_DEV * r, c)
    return slots.transpose(1, 0, 2).reshape(r, N_DEV * c)


def kernel(x, rel_bias, w_in, b_gate, ssm_conv_w, ssm_conv_b, ssm_dt_bias, ssm_a_log, ssm_d, ssm_norm_w, attn_sinks, w_branch_ssm, w_branch_attn, w_mix_out, ln1_g, ln1_b, w_up, ffn_conv_w, ffn_conv_b, w_down, ln2_g, ln2_b, loss_target, m_rel_bias, m_w_in, m_b_gate, m_ssm_conv_w, m_ssm_conv_b, m_ssm_dt_bias, m_ssm_a_log, m_ssm_d, m_ssm_norm_w, m_attn_sinks, m_w_branch_ssm, m_w_branch_attn, m_w_mix_out, m_ln1_g, m_ln1_b, m_w_up, m_ffn_conv_w, m_ffn_conv_b, m_w_down, m_ln2_g, m_ln2_b, v_rel_bias, v_w_in, v_b_gate, v_ssm_conv_w, v_ssm_conv_b, v_ssm_dt_bias, v_ssm_a_log, v_ssm_d, v_ssm_norm_w, v_attn_sinks, v_w_branch_ssm, v_w_branch_attn, v_w_mix_out, v_ln1_g, v_ln1_b, v_w_up, v_ffn_conv_w, v_ffn_conv_b, v_w_down, v_ln2_g, v_ln2_b):
    W = dict(zip(WEIGHTS, (rel_bias, w_in, b_gate, ssm_conv_w, ssm_conv_b, ssm_dt_bias, ssm_a_log, ssm_d, ssm_norm_w,
                           attn_sinks, w_branch_ssm, w_branch_attn, w_mix_out, ln1_g, ln1_b, w_up, ffn_conv_w,
                           ffn_conv_b, w_down, ln2_g, ln2_b)))
    M = dict(zip(WEIGHTS, (m_rel_bias, m_w_in, m_b_gate, m_ssm_conv_w, m_ssm_conv_b, m_ssm_dt_bias, m_ssm_a_log, m_ssm_d,
                           m_ssm_norm_w, m_attn_sinks, m_w_branch_ssm, m_w_branch_attn, m_w_mix_out, m_ln1_g, m_ln1_b,
                           m_w_up, m_ffn_conv_w, m_ffn_conv_b, m_w_down, m_ln2_g, m_ln2_b)))
    V = dict(zip(WEIGHTS, (v_rel_bias, v_w_in, v_b_gate, v_ssm_conv_w, v_ssm_conv_b, v_ssm_dt_bias, v_ssm_a_log, v_ssm_d,
                           v_ssm_norm_w, v_attn_sinks, v_w_branch_ssm, v_w_branch_attn, v_w_mix_out, v_ln1_g, v_ln1_b,
                           v_w_up, v_ffn_conv_w, v_ffn_conv_b, v_w_down, v_ln2_g, v_ln2_b)))
    shard2d = lambda a: a.reshape(a.shape[-2], a.shape[-1])

    (win_all,) = _all_gather([shard2d(w_in).astype(BF16)], "gather_w_in")
    main, tail = _w_in_from_shards(win_all)
    conv_all = _all_gather([shard2d(ssm_conv_w), shard2d(ffn_conv_w)], "gather_conv_weights")
    late_shards = [shard2d(W[n]).astype(BF16) for n in LATE]
    late = lambda landed: {SHORT[n]: _from_devices(a, SHARDED[n]) for n, a in zip(LATE, landed)}
    p = _prep_params(rel_bias, b_gate, _from_devices(conv_all[0], "cols"), ssm_conv_b, ssm_dt_bias, ssm_a_log, ssm_d,
                     ssm_norm_w, attn_sinks, ln1_g, ln1_b, _from_devices(conv_all[1], "cols"), ffn_conv_b, ln2_g, ln2_b)

    early_names = LATE + ("ssm_conv_w", "ffn_conv_w")
    early = lambda g: [_by_device(g[n], SHARDED[n]).astype(BF16 if n in LATE else F32) for n in early_names]
    sq, dproj, w, g, landed = _local_step(x[0], loss_target[0], {"in_main": main, "in_tail": tail}, p,
                                          (late_shards, late), early)
    landed = dict(zip(early_names, landed))
    dx, landed["w_in"] = _grad_x(dproj, w, exchange=[_w_in_grad_by_device(g.pop("in_main"), g.pop("in_tail"))])
    loss = (0.5 / D_MODEL) * lax.psum(sq[0, 0], ("x", "y", "c"))
    grads, deltas, new_m, new_v = {}, {}, {}, {}
    for n in SHARDED:
        outs = _adamw(landed[n], shard2d(W[n]), shard2d(M[n]), shard2d(V[n]), "adamw_" + n)
        grads[n], deltas[n], new_m[n], new_v[n] = (a.reshape(W[n].shape) for a in outs)

    row = lambda a: a.reshape(1, -1)
    packed = jnp.concatenate([jnp.pad(row(g[n]), ((0, 0), (0, SMALL_COLS - g[n].size))) for n in SMALL]
                             + [jnp.zeros((SMALL_ROWS - len(SMALL), SMALL_COLS), F32)], axis=0)
    (small_all,) = _all_gather([packed], "gather_small_grads")
    outs = _small_update(small_all, *[[row(src[n]) for n in SMALL] for src in (W, M, V)])
    for i, n in enumerate(SMALL):
        grads[n], deltas[n], new_m[n], new_v[n] = (outs[j * len(SMALL) + i].reshape(W[n].shape) for j in range(4))

    return (loss, dx[None], *[grads[n] for n in WEIGHTS], *[deltas[n] for n in WEIGHTS],
            *[new_m[n] for n in WEIGHTS], *[new_v[n] for n in WEIGHTS])
```

```python
import math

import jax
import jax.numpy as jnp
from jax import lax
from jax.experimental import pallas as pl
from jax.experimental.pallas import tpu as pltpu

F32, BF16 = jnp.float32, jnp.bfloat16
HIGHEST = lax.Precision.HIGHEST
MESH_ID = pl.DeviceIdType.MESH

N_DEV = 8
D_MODEL = 1024
SSM_INNER = 2048
SSM_HEADS = 32
SSM_HEADDIM = 64
SSMD = SSM_HEADDIM
SSM_GROUPS = 4
SSM_GROUP_COLS = SSM_INNER // SSM_GROUPS
SSM_STATE = 128
SSM_CONV = 4
CHUNK = 128
XBC_COLS = SSM_INNER + 2 * SSM_GROUPS * SSM_STATE
B_OFF = SSM_INNER
C_OFF = SSM_INNER + SSM_GROUPS * SSM_STATE
ATTN_HEADS = 16
ATTN_KV = 2
ATTN_GROUP = 8
HEADDIM = 64
WINDOW = 128
REL_BUCKETS = 32
REL_MAX_DIST = 128
D_FF = 2816
FFN_CONV = 3
ALPHA = 2.0 ** 0.25
LN_EPS = 1e-5
RMS_EPS = 1e-5
IN_COLS = 8480
Z_OFF, XBC_OFF, Q_OFF, GATE_OFF, MAIN_COLS = 0, 2048, 5120, 6144, 8192
K_OFF, V_OFF, DT_OFF, TAIL_COLS = 0, 128, 256, 384
O_Z, O_XBC, O_DT, O_Q, O_K, O_V, O_GATE = 0, 2048, 5120, 5152, 6176, 6304, 6432

ADAM_LR, ADAM_B1, ADAM_B2, ADAM_EPS, ADAM_WD, ADAM_STEP = 0.001, 0.9, 0.999, 1e-08, 0.01, 10
NEG = -1e30
HALO = 8
VMEM_LIMIT = 56 * 1024 * 1024


def _cp(sem):
    return pltpu.CompilerParams(dimension_semantics=sem, vmem_limit_bytes=VMEM_LIMIT)


def _const_spec(shape):
    nd = len(shape)
    return pl.BlockSpec(shape, lambda *_: (0,) * nd)


def _sigmoid(x):
    return 0.5 * jnp.tanh(0.5 * x) + 0.5


def _softplus(x):
    return jnp.maximum(x, 0.0) + jnp.log1p(jnp.exp(-jnp.abs(x)))


def _dot(a, b, dims=(((1,), (0,)), ((), ())), precision=None):
    return lax.dot_general(a, b, dims, preferred_element_type=F32, precision=precision)


NN = (((1,), (0,)), ((), ()))
NT = (((1,), (1,)), ((), ()))
TN = (((0,), (0,)), ((), ()))


def _mesh_pos():
    return lax.axis_index("x"), lax.axis_index("y"), lax.axis_index("c")


PEERS = N_DEV - 1


def _exchange_phases(in_refs, out_refs, send_sems, recv_sems, local_sems):
    def copies():
        x, y, c = _mesh_pos()
        me = 4 * x + 2 * y + c
        cps = []
        for b, (in_ref, out_ref) in enumerate(zip(in_refs, out_refs)):
            cps.append(pltpu.make_async_copy(in_ref.at[me], out_ref.at[me], local_sems.at[b]))
            for r in range(1, N_DEV):
                px = 1 - x if r & 4 else x
                py = 1 - y if r & 2 else y
                pc = 1 - c if r & 1 else c
                cps.append(pltpu.make_async_remote_copy(
                    src_ref=in_ref.at[4 * px + 2 * py + pc], dst_ref=out_ref.at[me],
                    send_sem=send_sems.at[b * PEERS + r - 1], recv_sem=recv_sems.at[b * PEERS + r - 1],
                    device_id=(px, py, pc), device_id_type=MESH_ID))
        return cps

    def start():
        for cp in copies():
            cp.start()

    def finish():
        for cp in copies():
            cp.wait()

    return [start, finish]


def _gather_phases(x_refs, out_refs, send_sems, recv_sems, local_sems):
    def parts(which):
        x, y, c = _mesh_pos()
        me, sibling = (x, y, c), (x, y, 1 - c)
        chips = [(1 - x, y), (x, 1 - y), (1 - x, 1 - y)]
        found = []
        for b, (x_ref, out_ref) in enumerate(zip(x_refs, out_refs)):
            def slot(px, py, pc):
                return out_ref.at[4 * px + 2 * py + pc]

            def copy(k, block, to, src=None):
                return pltpu.make_async_remote_copy(
                    src_ref=slot(*block) if src is None else src, dst_ref=slot(*block),
                    send_sem=send_sems.at[b * PEERS + k], recv_sem=recv_sems.at[b * PEERS + k],
                    device_id=to, device_id_type=MESH_ID)

            if which == "mine":
                found.append(pltpu.make_async_copy(x_ref, slot(*me), local_sems.at[b]))
            elif which == "first":
                found.append(copy(0, me, sibling, src=x_ref))
                found += [copy(1 + j, me, (*chip, c), src=x_ref) for j, chip in enumerate(chips)]
            elif which == "passed":
                found += [copy(4 + j, (*chip, c), sibling) for j, chip in enumerate(chips)]
            elif which == "arrived":
                found += [copy(1 + j, (*chip, c), me) for j, chip in enumerate(chips)]
            else:
                found.append(copy(0, sibling, me))
                found += [copy(4 + j, (*chip, 1 - c), me) for j, chip in enumerate(chips)]
        return found

    def start():
        for cp in parts("mine") + parts("first"):
            cp.start()

    def forward():
        for a, p in zip(parts("arrived"), parts("passed")):
            a.wait_recv()
            p.start()

    def finish():
        for cp in parts("late"):
            cp.wait_recv()
        for cp in parts("first") + parts("passed"):
            cp.wait_send()
        for cp in parts("mine"):
            cp.wait()

    return [start, forward, finish]


COMM = {"exchange": _exchange_phases, "gather": _gather_phases}


def _comm_sems(nb):
    return [pltpu.SemaphoreType.DMA((nb * PEERS,)), pltpu.SemaphoreType.DMA((nb * PEERS,)), pltpu.SemaphoreType.DMA((nb,))]


def _matmul(a, b, mode, out_dtype, name, tm=512, tn=1024, tk=1024, addend=None, addend_scale=1.0, comm=None):
    bufs = [] if comm is None else list(comm[1])
    nb = len(bufs)
    halves = b.ndim == 3
    if mode == "nn":
        (M, K), (K2, N) = a.shape, b.shape
    elif mode == "nt":
        (M, K), (N, K2) = a.shape, b.shape
    elif halves:
        (K, M), (K2, N) = a.shape, (b.shape[1], 2 * b.shape[2])
    else:
        (K, M), (K2, N) = a.shape, b.shape
    assert K == K2, (a.shape, b.shape, mode)
    tm, tn, tk = min(tm, M), min(tn, N), min(tk, K)
    assert M % tm == 0 and N % tn == 0 and K % tk == 0, (M, N, K, tm, tn, tk)
    nk = K // tk
    dims = {"nn": NN, "nt": NT, "tn": TN}[mode]
    a_spec = pl.BlockSpec((tk, tm), lambda i, j, k: (k, i)) if mode == "tn" else pl.BlockSpec((tm, tk), lambda i, j, k: (i, k))
    b_spec = pl.BlockSpec((tn, tk), lambda i, j, k: (j, k)) if mode == "nt" else pl.BlockSpec((tk, tn), lambda i, j, k: (k, j))
    o_spec = pl.BlockSpec((tm, tn), lambda i, j, k: (i, j))

    ni, nj = M // tm, N // tn
    if halves:
        assert mode == "tn" and nj % 2 == 0
        b_spec = pl.BlockSpec((None, tk, tn), lambda i, j, k: (j // (nj // 2), k, j % (nj // 2)))

    def body(*refs):
        refs = list(refs)
        a_ref, b_ref = refs[:2]
        c_ref = refs[2] if addend is not None else None
        n_in = 2 + (addend is not None) + nb
        o_ref = refs[n_in]
        acc = refs[n_in + 1 + nb] if nk > 1 else None
        i, j, k = pl.program_id(0), pl.program_id(1), pl.program_id(2)
        step = (i * nj + j) * nk + k
        if comm is not None:
            phases = COMM[comm[0]](refs[n_in - nb:n_in], refs[n_in + 1:n_in + 1 + nb], *refs[n_in + 1 + nb + (nk > 1):])
            at = [(ni * nj * nk - 1) * p // (len(phases) - 1) for p in range(len(phases))]
            for when, phase in zip(at[:-1], phases[:-1]):
                pl.when(step == when)(phase)

        d = _dot(a_ref[...].astype(BF16), b_ref[...].astype(BF16), dims)

        def finish(r):
            if addend is not None:
                r = r + addend_scale * c_ref[...].astype(F32)
            o_ref[...] = r.astype(out_dtype)

        if nk == 1:
            finish(d)
        else:
            @pl.when(k == 0)
            def _():
                acc[...] = d

            @pl.when(jnp.logical_and(k > 0, k < nk - 1))
            def _():
                acc[...] += d

            @pl.when(k == nk - 1)
            def _():
                finish(acc[...] + d)

        if comm is not None:
            pl.when(step == at[-1])(phases[-1])

    in_specs = [a_spec, b_spec] + ([o_spec] if addend is not None else [])
    args = (a, b) + ((addend,) if addend is not None else ())
    out_specs, out_shape = o_spec, jax.ShapeDtypeStruct((M, N), out_dtype)
    scratch = [pltpu.VMEM((tm, tn), F32)] if nk > 1 else []
    sem = ("parallel", "parallel", "arbitrary")
    if comm is not None:
        any_spec = pl.BlockSpec(memory_space=pl.ANY)
        in_specs, args = in_specs + [any_spec] * nb, args + tuple(bufs)
        landed = [x.shape if comm[0] == "exchange" else (N_DEV,) + x.shape for x in bufs]
        out_specs = [o_spec] + [any_spec] * nb
        out_shape = [out_shape] + [jax.ShapeDtypeStruct(s, x.dtype) for s, x in zip(landed, bufs)]
        scratch += _comm_sems(nb)
        sem = ("arbitrary", "arbitrary", "arbitrary")
    return pl.pallas_call(
        body, name=name, grid=(ni, nj, nk), in_specs=in_specs, out_specs=out_specs, out_shape=out_shape,
        scratch_shapes=scratch, compiler_params=_cp(sem))(*args)


def _taps(w_ref, K, tc):
    return [jnp.broadcast_to(w_ref[k:k + 1, :], (HALO, tc)) for k in range(K)]


def _conv_silu_fwd(pre, pre_col_off, C, w, b, K, name, tr=1024, tc=512):
    T = pre.shape[0]
    tr, tc = min(tr, T), min(tc, C)
    assert T % tr == 0 and C % tc == 0 and pre_col_off % tc == 0
    joff = pre_col_off // tc
    hb = tr // HALO

    def body(x_ref, xp_ref, w_ref, b_ref, o_ref, d_ref, head):
        i = pl.program_id(1)
        head[0:HALO, :] = jnp.where(i > 0, xp_ref[...], 0.0)
        head[HALO:, :] = x_ref[0:HALO, :]
        wk = _taps(w_ref, K, tc)
        bias = jnp.broadcast_to(b_ref[...], (HALO, tc))
        for r in range(tr // HALO):
            lo = r * HALO
            co = bias + wk[K - 1] * x_ref[lo:lo + HALO, :]
            for k in range(K - 1):
                s = K - 1 - k
                co = co + wk[k] * (head[HALO - s:2 * HALO - s, :] if r == 0 else x_ref[lo - s:lo + HALO - s, :])
            sg = _sigmoid(co)
            y = co * sg
            o_ref[lo:lo + HALO, :] = y
            d_ref[lo:lo + HALO, :] = sg + y * (1.0 - sg)

    out = pl.BlockSpec((tr, tc), lambda j, i: (i, j))
    return pl.pallas_call(
        body, name=name, grid=(C // tc, T // tr),
        in_specs=[pl.BlockSpec((tr, tc), lambda j, i: (i, joff + j)),
                  pl.BlockSpec((HALO, tc), lambda j, i: (jnp.maximum(i * hb - 1, 0), joff + j)),
                  pl.BlockSpec((K, tc), lambda j, i: (0, j)),
                  pl.BlockSpec((1, tc), lambda j, i: (0, j))],
        out_specs=[out, out], out_shape=[jax.ShapeDtypeStruct((T, C), F32)] * 2,
        scratch_shapes=[pltpu.VMEM((2 * HALO, tc), F32)],
        compiler_params=_cp(("parallel", "arbitrary")))(pre, pre, w, b)


def _conv_bwd(dout, pre, pre_col_off, C, w, K, dst, dst_col_off, name, tr=1024, tc=512):
    T = pre.shape[0]
    tr, tc = min(tr, T), min(tc, C)
    assert T % tr == 0 and C % tc == 0 and pre_col_off % tc == 0 and dst_col_off % tc == 0
    joff, doff = pre_col_off // tc, dst_col_off // tc
    hb = tr // HALO
    nt = T // tr
    n = tr // HALO
    last_hblock = T // HALO - 1

    def body(g_ref, gn_ref, x_ref, w_ref, *rest):
        o_ref, dw_ref, db_ref, edge = rest[-4:]
        i = pl.program_id(1)
        wk = _taps(w_ref, K, tc)
        edge[0:HALO, :] = g_ref[tr - HALO:tr, :]
        edge[HALO:, :] = jnp.where(i < nt - 1, gn_ref[...], 0.0)
        acc_w = [jnp.zeros((HALO, tc), F32) for _ in range(K)]
        acc_b = jnp.zeros((HALO, tc), F32)
        for r in range(n):
            lo = r * HALO
            x = x_ref[lo:lo + HALO, :]
            dpre = None
            for s in range(K):
                gs = edge[s:HALO + s, :] if (r == n - 1 and s > 0) else g_ref[lo + s:lo + HALO + s, :]
                dpre = wk[K - 1 - s] * gs if dpre is None else dpre + wk[K - 1 - s] * gs
                acc_w[K - 1 - s] = acc_w[K - 1 - s] + gs * x
                if s == 0:
                    acc_b = acc_b + gs
            o_ref[lo:lo + HALO, :] = dpre.astype(o_ref.dtype)

        @pl.when(i == 0)
        def _():
            dw_ref[...] = jnp.zeros_like(dw_ref)
            db_ref[...] = jnp.zeros_like(db_ref)

        db_ref[...] += jnp.sum(acc_b, axis=0, keepdims=True)
        dw_ref[...] += jnp.concatenate([jnp.sum(a, axis=0, keepdims=True) for a in acc_w], axis=0)

    tile = lambda off: pl.BlockSpec((tr, tc), lambda j, i: (i, off + j))
    in_specs = [tile(0), pl.BlockSpec((HALO, tc), lambda j, i: (jnp.minimum((i + 1) * hb, last_hblock), j)),
                tile(joff), pl.BlockSpec((K, tc), lambda j, i: (0, j))]
    args = (dout, dout, pre, w)
    if isinstance(dst, jax.ShapeDtypeStruct):
        aliases = {}
    else:
        in_specs.append(pl.BlockSpec(memory_space=pl.ANY))
        args += (dst,)
        aliases = {4: 0}
    return pl.pallas_call(
        body, name=name, grid=(C // tc, nt), in_specs=in_specs,
        out_specs=[tile(doff), pl.BlockSpec((K, tc), lambda j, i: (0, j)), pl.BlockSpec((1, tc), lambda j, i: (0, j))],
        out_shape=[jax.ShapeDtypeStruct(dst.shape, dst.dtype), jax.ShapeDtypeStruct((K, C), F32),
                   jax.ShapeDtypeStruct((1, C), F32)],
        scratch_shapes=[pltpu.VMEM((2 * HALO, tc), F32)],
        input_output_aliases=aliases,
        compiler_params=_cp(("parallel", "arbitrary")))(*args)


PAIR = 2 * SSMD
PAIRS_PER_GROUP = SSM_GROUP_COLS // PAIR


def _dot3(x, onehot):
    h1 = x.astype(BF16)
    r = x - h1.astype(F32)
    h2 = r.astype(BF16)
    h3 = (r - h2.astype(F32)).astype(BF16)
    return _dot(h1, onehot) + _dot(h2, onehot) + _dot(h3, onehot)


def _chunk_rows(dt_raw, dtb_col, alog_col):
    row = lax.broadcasted_iota(jnp.int32, (CHUNK, CHUNK), 0)
    col = lax.broadcasted_iota(jnp.int32, (CHUNK, CHUNK), 1)
    dt_rawT = dt_raw.T
    dtT = _softplus(dt_rawT + dtb_col)
    a_col = -jnp.exp(alog_col)
    acsT = _dot3(dtT * a_col, (row <= col).astype(BF16))
    return dt_rawT, dtT, a_col, acsT, row, col


def _block_diag(x, left):
    return jnp.concatenate([jnp.where(left, x, 0.0), jnp.where(left, 0.0, x)], axis=0).astype(BF16)


def _lane_bcast(v, h):
    return jnp.broadcast_to(v[:, h:h + 1], (CHUNK, CHUNK))


def _ssd_fwd(xbc, proj_main, proj_tail, dtb_col, alog_col, d_exp, norm_w):
    T = xbc.shape[0]
    nc = T // CHUNK

    def body(xbc_ref, dt_ref, z_ref, dtb_ref, alog_ref, d_ref, nw_ref, y_ref, ypre_ref, hs_ref, H):
        c = pl.program_id(0)

        @pl.when(c == 0)
        def _():
            H[...] = jnp.zeros_like(H)

        hs_ref[0] = H[...]
        _, dtT, _, acsT, row, col = _chunk_rows(dt_ref[:, 0:SSM_HEADS], dtb_ref[...], alog_ref[...])
        tril, left = row >= col, col < SSMD
        acs = acsT.T
        w = (dtT * jnp.exp(acsT[:, CHUNK - 1:CHUNK] - acsT)).T
        cd = jnp.exp(acs[CHUNK - 1:CHUNK, :])
        for g in range(SSM_GROUPS):
            gs = slice(g * SSM_GROUP_COLS, (g + 1) * SSM_GROUP_COLS)
            Bb = xbc_ref[:, B_OFF + g * SSM_STATE:B_OFF + (g + 1) * SSM_STATE].astype(BF16)
            Cb = xbc_ref[:, C_OFF + g * SSM_STATE:C_OFF + (g + 1) * SSM_STATE].astype(BF16)
            Hg = H[:, gs]
            CH = _dot(Cb, Hg.astype(BF16))
            CB = _dot(Cb, Bb, NT)
            ys, xws = [], []
            for kk in range(PAIRS_PER_GROUP):
                k = g * PAIRS_PER_GROUP + kk
                xs_p = xbc_ref[:, k * PAIR:(k + 1) * PAIR]
                mps, ecols, wcols = [], [], []
                for j in range(2):
                    h = 2 * k + j
                    colb = _lane_bcast(acs, h)
                    L = jnp.exp(jnp.where(tril, colb - acsT[h:h + 1, :], -jnp.inf))
                    mps.append((CB * L * dtT[h:h + 1, :]).astype(BF16))
                    ecols.append(jnp.exp(colb))
                    wcols.append(_lane_bcast(w, h))
                yd = _dot(jnp.concatenate(mps, axis=1), _block_diag(xs_p, left))
                ys.append(yd + CH[:, kk * PAIR:(kk + 1) * PAIR] * jnp.where(left, ecols[0], ecols[1]))
                xws.append((xs_p * jnp.where(left, wcols[0], wcols[1])).astype(BF16))
            cd_e = jnp.concatenate([jnp.broadcast_to(cd[:, g * 8 + e:g * 8 + e + 1], (1, SSMD)) for e in range(8)], axis=1)
            H[:, gs] = Hg * cd_e + _dot(Bb, jnp.concatenate(xws, axis=1), TN)
            ypre = jnp.concatenate(ys, axis=1) + xbc_ref[:, gs] * d_ref[:, gs]
            ypre_ref[:, gs] = ypre
            z = z_ref[:, gs]
            yg = ypre * (z * _sigmoid(z))
            r = lax.rsqrt(jnp.mean(yg * yg, axis=1, keepdims=True) + RMS_EPS)
            y_ref[:, gs] = (yg * r * nw_ref[:, gs]).astype(BF16)

    vec = lambda n: _const_spec((1, n))
    colv = _const_spec((SSM_HEADS, 1))
    return pl.pallas_call(
        body, name="ssd_fwd", grid=(nc,),
        in_specs=[pl.BlockSpec((CHUNK, XBC_COLS), lambda c: (c, 0)),
                  pl.BlockSpec((CHUNK, 128), lambda c: (c, DT_OFF // 128)),
                  pl.BlockSpec((CHUNK, SSM_INNER), lambda c: (c, Z_OFF // SSM_INNER)),
                  colv, colv, vec(SSM_INNER), vec(SSM_INNER)],
        out_specs=[pl.BlockSpec((CHUNK, SSM_INNER), lambda c: (c, 0)),
                   pl.BlockSpec((CHUNK, SSM_INNER), lambda c: (c, 0)),
                   pl.BlockSpec((1, SSM_STATE, SSM_INNER), lambda c: (c, 0, 0))],
        out_shape=[jax.ShapeDtypeStruct((T, SSM_INNER), BF16), jax.ShapeDtypeStruct((T, SSM_INNER), F32),
                   jax.ShapeDtypeStruct((nc, SSM_STATE, SSM_INNER), F32)],
        scratch_shapes=[pltpu.VMEM((SSM_STATE, SSM_INNER), F32)],
        compiler_params=_cp(("arbitrary",)))(xbc, proj_tail, proj_main, dtb_col, alog_col, d_exp, norm_w)


def _ssd_bwd(dyo, ypre, xbc, dsil, hs, proj_main, proj_tail, dtb_col, alog_col, d_exp, norm_w, ehead_t, dmain, dtail):
    T = xbc.shape[0]
    nc = T // CHUNK

    def body(dyo_ref, ypre_ref, xbc_ref, dsil_ref, hs_ref, dt_ref, z_ref, dtb_ref, alog_ref, d_ref, nw_ref, eh_ref,
             dmain_in, dtail_in, dz_ref, ddt_ref, dxbc_ref, dnw_ref, dd_ref, dalog_ref, ddtb_ref, G):
        del dmain_in, dtail_in
        c = pl.program_id(0)

        @pl.when(c == 0)
        def _():
            G[...] = jnp.zeros_like(G)
            dnw_ref[...] = jnp.zeros_like(dnw_ref)
            dd_ref[...] = jnp.zeros_like(dd_ref)
            dalog_ref[...] = jnp.zeros_like(dalog_ref)
            ddtb_ref[...] = jnp.zeros_like(ddtb_ref)

        dt_rawT, dtT, a_col, acsT, row, col = _chunk_rows(dt_ref[:, 0:SSM_HEADS], dtb_ref[...], alog_ref[...])
        tril, triu, left = row >= col, col >= row, col < SSMD
        acs = acsT.T
        dt = dtT.T
        lastT = acsT[:, CHUNK - 1:CHUNK]
        dstT = jnp.exp(lastT - acsT)
        wT = dtT * dstT
        cd = jnp.exp(acs[CHUNK - 1:CHUNK, :])
        ddt_rows, rs_rows, deo_rows, dw_rows = [], [], [], []
        dd_cols, gh_cols, dnw_cols = [], [], []
        for g in range(SSM_GROUPS):
            gs = slice(g * SSM_GROUP_COLS, (g + 1) * SSM_GROUP_COLS)
            z = z_ref[:, gs]
            sz = _sigmoid(z)
            silu_z = z * sz
            ypre = ypre_ref[:, gs]
            yg = ypre * silu_z
            r = lax.rsqrt(jnp.mean(yg * yg, axis=1, keepdims=True) + RMS_EPS)
            ygn = yg * r
            dyo = dyo_ref[:, gs]
            dyn = dyo * nw_ref[:, gs]
            dnw_cols.append(jnp.sum(dyo * ygn, axis=0, keepdims=True))
            dyg = r * (dyn - ygn * jnp.mean(dyn * ygn, axis=1, keepdims=True))
            dz_ref[:, gs] = (dyg * ypre * (sz * (1.0 + z * (1.0 - sz)))).astype(dz_ref.dtype)
            dY = dyg * silu_z
            xs = xbc_ref[:, gs]
            dd_cols.append(jnp.sum(dY * xs, axis=0, keepdims=True))
            Bf = xbc_ref[:, B_OFF + g * SSM_STATE:B_OFF + (g + 1) * SSM_STATE]
            Cf = xbc_ref[:, C_OFF + g * SSM_STATE:C_OFF + (g + 1) * SSM_STATE]
            Bb, Cb = Bf.astype(BF16), Cf.astype(BF16)
            BT, CT = Bf.T, Cf.T
            CB = _dot(Cb, Bb, NT)
            CBT = _dot(Bb, Cb, NT)
            Hg = hs_ref[0, :, gs]
            Gg = G[:, gs]
            gh_cols.append(jnp.sum(Gg * Hg, axis=0, keepdims=True))
            dCB = jnp.zeros((CHUNK, CHUNK), F32)
            dxs_d, dyes, xws, wsels = [], [], [], []
            for kk in range(PAIRS_PER_GROUP):
                k = g * PAIRS_PER_GROUP + kk
                ps = slice(kk * PAIR, (kk + 1) * PAIR)
                xs_p, dY_p = xs[:, ps], dY[:, ps]
                Ls, LTs, dtcols, ecols, wcols = [], [], [], [], []
                for j in range(2):
                    h = 2 * k + j
                    colb = _lane_bcast(acs, h)
                    seg = colb - acsT[h:h + 1, :]
                    Ls.append(jnp.exp(jnp.where(tril, seg, -jnp.inf)))
                    LTs.append(jnp.exp(jnp.where(triu, -seg, -jnp.inf)))
                    dtcol = _lane_bcast(dt, h)
                    dtcols.append(dtcol)
                    ecols.append(jnp.exp(colb))
                    wcols.append(dtcol * jnp.exp(acs[CHUNK - 1:CHUNK, h:h + 1] - colb))
                wsel = jnp.where(left, wcols[0], wcols[1])
                dYe_p = dY_p * jnp.where(left, ecols[0], ecols[1])
                bdx = _block_diag(xs_p, left)
                bddy = _block_diag(dY_p, left)
                dMx2 = _dot(dY_p.astype(BF16), bdx, NT)
                dMxT2 = _dot(xs_p.astype(BF16), bddy, NT)
                Q1 = _dot(Hg[:, ps].astype(BF16), _block_diag(dYe_p, left), NT)
                Q2 = _dot(Gg[:, ps].astype(BF16), bdx, NT)
                mts = []
                for j in range(2):
                    h = 2 * k + j
                    js = slice(j * CHUNK, (j + 1) * CHUNK)
                    dMx = dMx2[:, js]
                    A = CB * Ls[j]
                    AT = CBT * LTs[j]
                    ddt_rows.append(jnp.sum(A * dMx, axis=0, keepdims=True))
                    ATd = AT * dtcols[j]
                    rs_rows.append(jnp.sum(ATd * dMxT2[:, js], axis=0, keepdims=True))
                    dCB = dCB + dMx * Ls[j] * dtT[h:h + 1, :]
                    mts.append(ATd.astype(BF16))
                    deo_rows.append(jnp.sum(CT * Q1[:, js], axis=0, keepdims=True))
                    dw_rows.append(jnp.sum(BT * Q2[:, js], axis=0, keepdims=True))
                dxs_d.append(_dot(jnp.concatenate(mts, axis=1), bddy))
                dyes.append(dYe_p.astype(BF16))
                xws.append((xs_p * wsel).astype(BF16))
                wsels.append(wsel)
            dYe_g = jnp.concatenate(dyes, axis=1)
            xw_g = jnp.concatenate(xws, axis=1)
            Hgb, Ggb, dCBb = Hg.astype(BF16), Gg.astype(BF16), dCB.astype(BF16)
            cs = slice(C_OFF + g * SSM_STATE, C_OFF + (g + 1) * SSM_STATE)
            bs = slice(B_OFF + g * SSM_STATE, B_OFF + (g + 1) * SSM_STATE)
            dxbc_ref[:, cs] = (_dot(dYe_g, Hgb, NT) + _dot(dCBb, Bb)) * dsil_ref[:, cs]
            dxbc_ref[:, bs] = (_dot(xw_g, Ggb, NT) + _dot(dCBb, Cb, TN)) * dsil_ref[:, bs]
            BG = _dot(Bb, Ggb)
            dxbc_ref[:, gs] = (jnp.concatenate(dxs_d, axis=1) + BG * jnp.concatenate(wsels, axis=1)
                               + dY * d_ref[:, gs]) * dsil_ref[:, gs]
            cd_e = jnp.concatenate([jnp.broadcast_to(cd[:, g * 8 + e:g * 8 + e + 1], (1, SSMD)) for e in range(8)], axis=1)
            G[:, gs] = Gg * cd_e + _dot(Cb, dYe_g, TN)
        dnw_ref[...] += jnp.concatenate(dnw_cols, axis=1)
        eh = eh_ref[...]
        dd_ref[...] += jnp.sum(eh * jnp.concatenate(dd_cols, axis=1), axis=1, keepdims=True)
        dcd = jnp.sum(eh * jnp.concatenate(gh_cols, axis=1), axis=1, keepdims=True)
        DDT = jnp.concatenate(ddt_rows, axis=0)
        DW = jnp.concatenate(dw_rows, axis=0)
        DWw = DW * wT
        dacsT = jnp.concatenate(rs_rows, axis=0) - DDT * dtT + jnp.concatenate(deo_rows, axis=0) - DWw
        end = jnp.sum(DWw, axis=1, keepdims=True) + dcd * jnp.exp(lastT)
        lane = lax.broadcasted_iota(jnp.int32, (SSM_HEADS, CHUNK), 1)
        dacsT = dacsT + jnp.where(lane == CHUNK - 1, end, 0.0)
        dadtT = _dot3(dacsT, tril.astype(BF16))
        ddtT = dadtT * a_col + DDT + DW * dstT
        dalog_ref[...] += jnp.sum(dadtT * dtT, axis=1, keepdims=True) * a_col
        ddt_rawT = ddtT * _sigmoid(dt_rawT + dtb_ref[...])
        ddtb_ref[...] += jnp.sum(ddt_rawT, axis=1, keepdims=True)
        ddt_ref[...] = jnp.concatenate([ddt_rawT.T, jnp.zeros((CHUNK, 128 - SSM_HEADS), F32)], axis=1).astype(ddt_ref.dtype)

    rev = lambda c: nc - 1 - c
    vec = lambda n: _const_spec((1, n))
    colv = _const_spec((SSM_HEADS, 1))
    any_spec = pl.BlockSpec(memory_space=pl.ANY)
    return pl.pallas_call(
        body, name="ssd_bwd", grid=(nc,),
        in_specs=[pl.BlockSpec((CHUNK, SSM_INNER), lambda c: (rev(c), 0)),
                  pl.BlockSpec((CHUNK, SSM_INNER), lambda c: (rev(c), 0)),
                  pl.BlockSpec((CHUNK, XBC_COLS), lambda c: (rev(c), 0)),
                  pl.BlockSpec((CHUNK, XBC_COLS), lambda c: (rev(c), 0)),
                  pl.BlockSpec((1, SSM_STATE, SSM_INNER), lambda c: (rev(c), 0, 0)),
                  pl.BlockSpec((CHUNK, 128), lambda c: (rev(c), DT_OFF // 128)),
                  pl.BlockSpec((CHUNK, SSM_INNER), lambda c: (rev(c), Z_OFF // SSM_INNER)),
                  colv, colv, vec(SSM_INNER), vec(SSM_INNER), _const_spec((SSM_HEADS, SSM_INNER)), any_spec, any_spec],
        out_specs=[pl.BlockSpec((CHUNK, SSM_INNER), lambda c: (rev(c), Z_OFF // SSM_INNER)),
                   pl.BlockSpec((CHUNK, 128), lambda c: (rev(c), DT_OFF // 128)),
                   pl.BlockSpec((CHUNK, XBC_COLS), lambda c: (rev(c), 0)),
                   vec(SSM_INNER), colv, colv, colv],
        out_shape=[jax.ShapeDtypeStruct(dmain.shape, dmain.dtype), jax.ShapeDtypeStruct(dtail.shape, dtail.dtype),
                   jax.ShapeDtypeStruct((T, XBC_COLS), F32), jax.ShapeDtypeStruct((1, SSM_INNER), F32),
                   jax.ShapeDtypeStruct((SSM_HEADS, 1), F32), jax.ShapeDtypeStruct((SSM_HEADS, 1), F32),
                   jax.ShapeDtypeStruct((SSM_HEADS, 1), F32)],
        scratch_shapes=[pltpu.VMEM((SSM_STATE, SSM_INNER), F32)],
        input_output_aliases={12: 0, 13: 1},
        compiler_params=_cp(("arbitrary",)))(dyo, ypre, xbc, dsil, hs, proj_tail, proj_main, dtb_col, alog_col, d_exp,
                                             norm_w, ehead_t, dmain, dtail)


def _rel_bucket(rel):
    n = jnp.maximum(rel, 0)
    max_exact = REL_BUCKETS // 2
    nf = jnp.maximum(n, 1).astype(F32)
    large = max_exact + (jnp.log(nf / max_exact) / math.log(REL_MAX_DIST / max_exact)
                         * (REL_BUCKETS - max_exact)).astype(jnp.int32)
    large = jnp.minimum(large, REL_BUCKETS - 1)
    return jnp.where(n < max_exact, n, large)


def _band_geometry():
    qi = jnp.arange(WINDOW)[:, None] + WINDOW
    kj = jnp.arange(2 * WINDOW)[None, :]
    rel = qi - kj
    return _rel_bucket(rel), (rel >= 0) & (rel < WINDOW)


def _attn_logits(kband, qh, bias_h, first):
    s = _dot(kband, qh, NT) * (HEADDIM ** -0.5) + bias_h
    rowk = lax.broadcasted_iota(jnp.int32, (2 * WINDOW, WINDOW), 0)
    return jnp.where(jnp.logical_and(first, rowk < WINDOW), NEG, s)


def _attn_fwd(proj_main, proj_tail, bias_tbl, sinks):
    T = proj_main.shape[0]
    nb = T // WINDOW

    def body(q_ref, kv_ref, kvp_ref, bias_ref, sink_ref, o_ref, lse_ref):
        i = pl.program_id(0)
        first = i == 0
        outs, lses = [], []
        for kvh in range(ATTN_KV):
            ks = slice(K_OFF + kvh * HEADDIM, K_OFF + (kvh + 1) * HEADDIM)
            vs = slice(V_OFF + kvh * HEADDIM, V_OFF + (kvh + 1) * HEADDIM)
            kband = jnp.concatenate([kvp_ref[:, ks], kv_ref[:, ks]], axis=0).astype(BF16)
            vband = jnp.concatenate([kvp_ref[:, vs], kv_ref[:, vs]], axis=0).astype(BF16)
            heads = range(kvh * ATTN_GROUP, (kvh + 1) * ATTN_GROUP)
            logits = [_attn_logits(kband, q_ref[:, h * HEADDIM:(h + 1) * HEADDIM].astype(BF16), bias_ref[h], first)
                      for h in heads]
            probs = []
            for h, s in zip(heads, logits):
                sink = sink_ref[:, h:h + 1]
                m = jnp.maximum(jnp.max(s, axis=0, keepdims=True), sink)
                p = jnp.exp(s - m)
                den = jnp.sum(p, axis=0, keepdims=True) + jnp.exp(sink - m)
                probs.append((p * (1.0 / den)).astype(BF16))
                lses.append(m + jnp.log(den))
            outs += [_dot(pt, vband, TN) for pt in probs]
        o_ref[...] = jnp.concatenate(outs, axis=1).astype(BF16)
        lse_ref[...] = jnp.concatenate(lses, axis=0)

    return pl.pallas_call(
        body, name="attn_fwd", grid=(nb,),
        in_specs=[pl.BlockSpec((WINDOW, D_MODEL), lambda i: (i, Q_OFF // D_MODEL)),
                  pl.BlockSpec((WINDOW, 256), lambda i: (i, 0)),
                  pl.BlockSpec((WINDOW, 256), lambda i: (jnp.maximum(i - 1, 0), 0)),
                  _const_spec((ATTN_HEADS, 2 * WINDOW, WINDOW)), _const_spec((1, ATTN_HEADS))],
        out_specs=[pl.BlockSpec((WINDOW, D_MODEL), lambda i: (i, 0)),
                   pl.BlockSpec((ATTN_HEADS, WINDOW), lambda i: (0, i))],
        out_shape=[jax.ShapeDtypeStruct((T, D_MODEL), BF16), jax.ShapeDtypeStruct((ATTN_HEADS, T), F32)],
        compiler_params=_cp(("arbitrary",)))(proj_main, proj_tail, proj_tail, bias_tbl, sinks)


def _attn_bwd(dy, lse, proj_main, proj_tail, bias_tbl, sinks, dmain):
    T = proj_main.shape[0]
    nb = T // WINDOW

    def body(dy_ref, lse_ref, q_ref, kv_ref, kvp_ref, bias_ref, sink_ref, dmain_in,
             dq_ref, dkv_ref, dbias_ref, dsink_ref, carry):
        del dmain_in
        i = pl.program_id(0)
        first = i == 0

        @pl.when(first)
        def _():
            carry[...] = jnp.zeros_like(carry)
            dbias_ref[...] = jnp.zeros_like(dbias_ref)
            dsink_ref[...] = jnp.zeros_like(dsink_ref)

        @pl.when(i < nb)
        def _():
            scale = HEADDIM ** -0.5
            dqs, dsinks, dks, dvs = [], [], [], []
            for kvh in range(ATTN_KV):
                ks = slice(K_OFF + kvh * HEADDIM, K_OFF + (kvh + 1) * HEADDIM)
                vs = slice(V_OFF + kvh * HEADDIM, V_OFF + (kvh + 1) * HEADDIM)
                kband = jnp.concatenate([kvp_ref[:, ks], kv_ref[:, ks]], axis=0).astype(BF16)
                vband = jnp.concatenate([kvp_ref[:, vs], kv_ref[:, vs]], axis=0).astype(BF16)
                heads = range(kvh * ATTN_GROUP, (kvh + 1) * ATTN_GROUP)
                qs = [q_ref[:, h * HEADDIM:(h + 1) * HEADDIM].astype(BF16) for h in heads]
                dos = [dy_ref[:, h * HEADDIM:(h + 1) * HEADDIM] for h in heads]
                logits = [_attn_logits(kband, qh, bias_ref[h], first) for h, qh in zip(heads, qs)]
                dps = [_dot(vband, do, NT) for do in dos]
                pbs, dsbs = [], []
                for h, s, dp in zip(heads, logits, dps):
                    lse_h = lse_ref[h:h + 1, :]
                    p = jnp.exp(s - lse_h)
                    delta = jnp.sum(p * dp, axis=0, keepdims=True)
                    ds = p * (dp - delta)
                    psink = jnp.exp(sink_ref[:, h:h + 1] - lse_h)
                    dsinks.append(-jnp.sum(psink * delta, axis=1, keepdims=True))
                    dbias_ref[h] += ds
                    pbs.append(p.astype(BF16))
                    dsbs.append((ds * scale).astype(BF16))
                dqs += [_dot(dsb, kband, TN) for dsb in dsbs]
                dks.append(_dot(jnp.concatenate(dsbs, axis=1), jnp.concatenate(qs, axis=0)))
                dvs.append(_dot(jnp.concatenate(pbs, axis=1), jnp.concatenate(dos, axis=0)))
            dq_ref[...] = jnp.concatenate(dqs, axis=1).astype(dq_ref.dtype)
            dsink_ref[...] += jnp.concatenate(dsinks, axis=1)
            dkv = jnp.concatenate(dks + dvs, axis=1)
            dkv_ref[...] = (carry[...] + dkv[0:WINDOW, :]).astype(dkv_ref.dtype)
            carry[...] = dkv[WINDOW:, :]

        @pl.when(i == nb)
        def _():
            dkv_ref[...] = carry[...].astype(dkv_ref.dtype)

    cur = lambda i: jnp.minimum(i, nb - 1)
    return pl.pallas_call(
        body, name="attn_bwd", grid=(nb + 1,),
        in_specs=[pl.BlockSpec((WINDOW, D_MODEL), lambda i: (cur(i), 0)),
                  pl.BlockSpec((ATTN_HEADS, WINDOW), lambda i: (0, cur(i))),
                  pl.BlockSpec((WINDOW, D_MODEL), lambda i: (cur(i), Q_OFF // D_MODEL)),
                  pl.BlockSpec((WINDOW, 256), lambda i: (cur(i), 0)),
                  pl.BlockSpec((WINDOW, 256), lambda i: (jnp.maximum(cur(i) - 1, 0), 0)),
                  _const_spec((ATTN_HEADS, 2 * WINDOW, WINDOW)), _const_spec((1, ATTN_HEADS)),
                  pl.BlockSpec(memory_space=pl.ANY)],
        out_specs=[pl.BlockSpec((WINDOW, D_MODEL), lambda i: (cur(i), Q_OFF // D_MODEL)),
                   pl.BlockSpec((WINDOW, 256), lambda i: (jnp.maximum(i - 1, 0), 0)),
                   _const_spec((ATTN_HEADS, 2 * WINDOW, WINDOW)), _const_spec((1, ATTN_HEADS))],
        out_shape=[jax.ShapeDtypeStruct(dmain.shape, dmain.dtype), jax.ShapeDtypeStruct((T, TAIL_COLS), BF16),
                   jax.ShapeDtypeStruct((ATTN_HEADS, 2 * WINDOW, WINDOW), F32),
                   jax.ShapeDtypeStruct((1, ATTN_HEADS), F32)],
        scratch_shapes=[pltpu.VMEM((WINDOW, 256), F32)],
        input_output_aliases={7: 0},
        compiler_params=_cp(("arbitrary",)))(dy, lse, proj_main, proj_tail, proj_tail, bias_tbl, sinks, dmain)


def _bias_table(rel_bias_t, onehot_t, mask):
    def body(rb_ref, oh_ref, m_ref, o_ref):
        o_ref[...] = _dot3(rb_ref[...], oh_ref[...]) + m_ref[...]

    flat = pl.pallas_call(body, name="bias_table",
                          out_shape=jax.ShapeDtypeStruct((ATTN_HEADS, 2 * WINDOW * WINDOW), F32))(rel_bias_t, onehot_t, mask)
    return flat.reshape(ATTN_HEADS, 2 * WINDOW, WINDOW)


def _rel_bias_grad(dbias, onehot):
    def body(d_ref, oh_ref, o_ref):
        o_ref[...] = _dot(d_ref[...], oh_ref[...], NN, HIGHEST)

    return pl.pallas_call(body, name="rel_bias_grad",
                          out_shape=jax.ShapeDtypeStruct((ATTN_HEADS, REL_BUCKETS), F32))(dbias, onehot)


def _ln_fwd(r, g, b):
    mu = jnp.mean(r, axis=1, keepdims=True)
    xc = r - mu
    rstd = lax.rsqrt(jnp.mean(xc * xc, axis=1, keepdims=True) + LN_EPS)
    xhat = xc * rstd
    return xhat * g + b, xhat, rstd


def _ln_bwd(dy, xhat, rstd, g):
    dxh = dy * g
    return rstd * (dxh - jnp.mean(dxh, axis=1, keepdims=True) - xhat * jnp.mean(dxh * xhat, axis=1, keepdims=True))


def _merge_fwd(y_ssm, y_attn, proj_main, b_gate, w_bs, w_ba, tm=512):
    T = y_ssm.shape[0]

    def body(ys_ref, ya_ref, gs_ref, ga_ref, bg_ref, wbs_ref, wba_ref, m_ref, bs_ref, ba_ref):
        bs = _dot(ys_ref[...], wbs_ref[...])
        ba = _dot(ya_ref[...], wba_ref[...])
        g_s = _sigmoid(gs_ref[...] + bg_ref[:, 0:D_MODEL])
        g_a = _sigmoid(ga_ref[...] + bg_ref[:, D_MODEL:])
        m_ref[...] = (g_s * bs + g_a * ba).astype(BF16)
        bs_ref[...] = bs
        ba_ref[...] = ba

    row = lambda w, off=0: pl.BlockSpec((tm, w), lambda i: (i, off))
    return pl.pallas_call(
        body, name="merge_fwd", grid=(T // tm,),
        in_specs=[row(SSM_INNER), row(D_MODEL), row(D_MODEL, GATE_OFF // D_MODEL), row(D_MODEL, GATE_OFF // D_MODEL + 1),
                  _const_spec((1, 2 * D_MODEL)), _const_spec((SSM_INNER, D_MODEL)), _const_spec((D_MODEL, D_MODEL))],
        out_specs=[row(D_MODEL), row(D_MODEL), row(D_MODEL)],
        out_shape=[jax.ShapeDtypeStruct((T, D_MODEL), BF16), jax.ShapeDtypeStruct((T, D_MODEL), F32),
                   jax.ShapeDtypeStruct((T, D_MODEL), F32)],
        compiler_params=_cp(("parallel",)))(y_ssm, y_attn, proj_main, proj_main, b_gate, w_bs, w_ba)


def _mix_ln1(merged, w_mo, x, g1, b1, tm=1024):
    T = x.shape[0]
    tm = min(tm, T)

    def body(m_ref, w_ref, x_ref, g_ref, b_ref, r_ref, h_ref, hb_ref):
        r = ALPHA * x_ref[...] + _dot(m_ref[...], w_ref[...])
        r_ref[...] = r
        h = _ln_fwd(r, g_ref[...], b_ref[...])[0]
        h_ref[...] = h
        hb_ref[...] = h.astype(BF16)

    row = pl.BlockSpec((tm, D_MODEL), lambda i: (i, 0))
    return pl.pallas_call(
        body, name="mix_ln1", grid=(T // tm,),
        in_specs=[row, _const_spec((D_MODEL, D_MODEL)), row, _const_spec((1, D_MODEL)), _const_spec((1, D_MODEL))],
        out_specs=[row, row, row],
        out_shape=[jax.ShapeDtypeStruct((T, D_MODEL), F32), jax.ShapeDtypeStruct((T, D_MODEL), F32),
                   jax.ShapeDtypeStruct((T, D_MODEL), BF16)],
        compiler_params=_cp(("parallel",)))(merged, w_mo, x, g1, b1)


def _ffn_conv_glu(u_pre, w, b, tr=2048, tc=256):
    T = u_pre.shape[0]
    tr = min(tr, T)
    K = FFN_CONV
    nj = D_FF // tc
    hb = tr // HALO
    assert T % tr == 0 and D_FF % tc == 0

    def body(xg_ref, xgp_ref, xv_ref, xvp_ref, wg_ref, wv_ref, bg_ref, bv_ref, u_ref, a_ref, head_g, head_v):
        i = pl.program_id(1)
        halves = []
        for x_ref, xp_ref, w_ref, b_ref, head in ((xg_ref, xgp_ref, wg_ref, bg_ref, head_g),
                                                  (xv_ref, xvp_ref, wv_ref, bv_ref, head_v)):
            head[0:HALO, :] = jnp.where(i > 0, xp_ref[...], 0.0)
            head[HALO:, :] = x_ref[0:HALO, :]
            halves.append((x_ref, head, _taps(w_ref, K, tc), jnp.broadcast_to(b_ref[...], (HALO, tc))))

        def conv(half, r):
            x_ref, head, wk, bias = halves[half]
            lo = r * HALO
            acc = bias + wk[K - 1] * x_ref[lo:lo + HALO, :]
            for k in range(K - 1):
                s = K - 1 - k
                acc = acc + wk[k] * (head[HALO - s:2 * HALO - s, :] if r == 0 else x_ref[lo - s:lo + HALO - s, :])
            return acc

        for r2 in range(tr // (2 * HALO)):
            acts = []
            for r in (2 * r2, 2 * r2 + 1):
                lo = r * HALO
                ug, uv = conv(0, r), conv(1, r)
                u_ref[0, lo:lo + HALO, :] = ug
                u_ref[1, lo:lo + HALO, :] = uv
                acts.append(ug * _sigmoid(ug) * uv)
            a_ref[2 * r2 * HALO:(2 * r2 + 2) * HALO, :] = jnp.concatenate(acts, axis=0).astype(BF16)

    tile = lambda off: pl.BlockSpec((tr, tc), lambda j, i: (i, off + j))
    prev = lambda off: pl.BlockSpec((HALO, tc), lambda j, i: (jnp.maximum(i * hb - 1, 0), off + j))
    row = lambda rows, off: pl.BlockSpec((rows, tc), lambda j, i: (0, off + j))
    return pl.pallas_call(
        body, name="ffn_conv_glu", grid=(nj, T // tr),
        in_specs=[tile(0), prev(0), tile(nj), prev(nj), row(K, 0), row(K, nj), row(1, 0), row(1, nj)],
        out_specs=[pl.BlockSpec((2, tr, tc), lambda j, i: (0, i, j)), pl.BlockSpec((tr, tc), lambda j, i: (i, j))],
        out_shape=[jax.ShapeDtypeStruct((2, T, D_FF), F32), jax.ShapeDtypeStruct((T, D_FF), BF16)],
        scratch_shapes=[pltpu.VMEM((2 * HALO, tc), F32), pltpu.VMEM((2 * HALO, tc), F32)],
        compiler_params=_cp(("parallel", "arbitrary")))(u_pre, u_pre, u_pre, u_pre, w, w, b, b)


def _down_ln2_loss(act, w_down, h1, target, g2, b2, tm=512):
    T = h1.shape[0]

    def body(a_ref, w_ref, h_ref, t_ref, g_ref, b_ref, dr_ref, drb_ref, dg_ref, db_ref, l_ref):
        @pl.when(pl.program_id(0) == 0)
        def _():
            dg_ref[...] = jnp.zeros_like(dg_ref)
            db_ref[...] = jnp.zeros_like(db_ref)
            l_ref[...] = jnp.zeros_like(l_ref)

        r = ALPHA * h_ref[...] + _dot(a_ref[...], w_ref[...])
        y, xhat, rstd = _ln_fwd(r, g_ref[...], b_ref[...])
        err = y - t_ref[...]
        l_ref[...] += jnp.sum(err * err, keepdims=True)
        dy = err * (1.0 / D_MODEL)
        dg_ref[...] += jnp.sum(dy * xhat, axis=0, keepdims=True)
        db_ref[...] += jnp.sum(dy, axis=0, keepdims=True)
        dr = _ln_bwd(dy, xhat, rstd, g_ref[...])
        dr_ref[...] = dr
        drb_ref[...] = dr.astype(BF16)

    row = pl.BlockSpec((tm, D_MODEL), lambda i: (i, 0))
    vec = _const_spec((1, D_MODEL))
    return pl.pallas_call(
        body, name="down_ln2_loss", grid=(T // tm,),
        in_specs=[pl.BlockSpec((tm, D_FF), lambda i: (i, 0)), _const_spec((D_FF, D_MODEL)), row, row, vec, vec],
        out_specs=[row, row, vec, vec, _const_spec((1, 1))],
        out_shape=[jax.ShapeDtypeStruct((T, D_MODEL), F32), jax.ShapeDtypeStruct((T, D_MODEL), BF16),
                   jax.ShapeDtypeStruct((1, D_MODEL), F32), jax.ShapeDtypeStruct((1, D_MODEL), F32),
                   jax.ShapeDtypeStruct((1, 1), F32)],
        compiler_params=_cp(("arbitrary",)))(act, w_down, h1, target, g2, b2)


def _ffn_gate_conv_bwd(dact, u, u_pre, w, tr=2048, tc=256):
    T = dact.shape[0]
    tr = min(tr, T)
    K = FFN_CONV
    nj, nt, n, hb = D_FF // tc, T // tr, tr // HALO, tr // HALO
    last_hblock = T // HALO - 1
    assert T % tr == 0 and D_FF % tc == 0 and n % 2 == 0

    def body(da_ref, dan_ref, u_ref, un_ref, xg_ref, xv_ref, wg_ref, wv_ref,
             o_ref, dwg_ref, dwv_ref, dbg_ref, dbv_ref, gext_g, gext_v):
        i = pl.program_id(1)
        for r in range(n + 1):
            rows = slice(r * HALO, (r + 1) * HALO)
            if r < n:
                da, g, v = da_ref[rows, :], u_ref[0, rows, :], u_ref[1, rows, :]
            else:
                da, g, v = jnp.where(i < nt - 1, dan_ref[...], 0.0), un_ref[0], un_ref[1]
            sg = _sigmoid(g)
            gext_g[rows, :] = da * v * (sg * (1.0 + g * (1.0 - sg)))
            gext_v[rows, :] = da * (g * sg)
        for half, (gext, x_ref, w_ref, dw_ref, db_ref) in enumerate(((gext_g, xg_ref, wg_ref, dwg_ref, dbg_ref),
                                                                      (gext_v, xv_ref, wv_ref, dwv_ref, dbv_ref))):
            wk = _taps(w_ref, K, tc)
            acc_w = [jnp.zeros((HALO, tc), F32) for _ in range(K)]
            acc_b = jnp.zeros((HALO, tc), F32)
            for r2 in range(n // 2):
                pair = []
                for r in (2 * r2, 2 * r2 + 1):
                    lo = r * HALO
                    x = x_ref[lo:lo + HALO, :]
                    dpre = None
                    for s in range(K):
                        gs = gext[lo + s:lo + HALO + s, :]
                        dpre = wk[K - 1 - s] * gs if dpre is None else dpre + wk[K - 1 - s] * gs
                        acc_w[K - 1 - s] = acc_w[K - 1 - s] + gs * x
                        if s == 0:
                            acc_b = acc_b + gs
                    pair.append(dpre)
                o_ref[half, 2 * r2 * HALO:(2 * r2 + 2) * HALO, :] = jnp.concatenate(pair, axis=0).astype(o_ref.dtype)

            @pl.when(i == 0)
            def _():
                dw_ref[...] = jnp.zeros_like(dw_ref)
                db_ref[...] = jnp.zeros_like(db_ref)

            db_ref[...] += jnp.sum(acc_b, axis=0, keepdims=True)
            dw_ref[...] += jnp.concatenate([jnp.sum(a, axis=0, keepdims=True) for a in acc_w], axis=0)

    nxt = lambda i: jnp.minimum((i + 1) * hb, last_hblock)
    taps = lambda off: pl.BlockSpec((K, tc), lambda j, i: (0, off + j))
    dw_spec, db_spec = pl.BlockSpec((K, tc), lambda j, i: (0, j)), pl.BlockSpec((1, tc), lambda j, i: (0, j))
    du_pre, dwg, dwv, dbg, dbv = pl.pallas_call(
        body, name="ffn_gate_conv_bwd", grid=(nj, nt),
        in_specs=[pl.BlockSpec((tr, tc), lambda j, i: (i, j)), pl.BlockSpec((HALO, tc), lambda j, i: (nxt(i), j)),
                  pl.BlockSpec((2, tr, tc), lambda j, i: (0, i, j)), pl.BlockSpec((2, HALO, tc), lambda j, i: (0, nxt(i), j)),
                  pl.BlockSpec((tr, tc), lambda j, i: (i, j)), pl.BlockSpec((tr, tc), lambda j, i: (i, nj + j)),
                  taps(0), taps(nj)],
        out_specs=[pl.BlockSpec((2, tr, tc), lambda j, i: (0, i, j)), dw_spec, dw_spec, db_spec, db_spec],
        out_shape=[jax.ShapeDtypeStruct((2, T, D_FF), BF16), jax.ShapeDtypeStruct((K, D_FF), F32),
                   jax.ShapeDtypeStruct((K, D_FF), F32), jax.ShapeDtypeStruct((1, D_FF), F32),
                   jax.ShapeDtypeStruct((1, D_FF), F32)],
        scratch_shapes=[pltpu.VMEM((tr + HALO, tc), F32), pltpu.VMEM((tr + HALO, tc), F32)],
        compiler_params=_cp(("parallel", "arbitrary")))(dact, dact, u, u, u_pre, u_pre, w, w)
    return du_pre, jnp.concatenate([dwg, dwv], axis=1), jnp.concatenate([dbg, dbv], axis=1)


def _ffn_bwd_in(du_pre, w_up, dr2, r1, g1, b1, tm=1024, tk=1408):
    T = dr2.shape[0]
    tm = min(tm, T)
    assert T % tm == 0 and D_FF % tk == 0
    nk = 2 * D_FF // tk
    kh = D_FF // tk

    def body(d_ref, w_ref, dr2_ref, r_ref, g_ref, b_ref, dr1_ref, dr1b_ref, dg_ref, db_ref, acc):
        i, k = pl.program_id(0), pl.program_id(1)

        @pl.when(jnp.logical_and(i == 0, k == 0))
        def _():
            dg_ref[...] = jnp.zeros_like(dg_ref)
            db_ref[...] = jnp.zeros_like(db_ref)

        @pl.when(k == 0)
        def _():
            acc[...] = ALPHA * dr2_ref[...]

        acc[...] += _dot(d_ref[...], w_ref[...], NT)

        @pl.when(k == nk - 1)
        def _():
            _, xhat, rstd = _ln_fwd(r_ref[...], g_ref[...], b_ref[...])
            dy = acc[...]
            dg_ref[...] += jnp.sum(dy * xhat, axis=0, keepdims=True)
            db_ref[...] += jnp.sum(dy, axis=0, keepdims=True)
            dr1 = _ln_bwd(dy, xhat, rstd, g_ref[...])
            dr1_ref[...] = dr1
            dr1b_ref[...] = dr1.astype(BF16)

    row = pl.BlockSpec((tm, D_MODEL), lambda i, k: (i, 0))
    vec = _const_spec((1, D_MODEL))
    return pl.pallas_call(
        body, name="ffn_bwd_in", grid=(T // tm, nk),
        in_specs=[pl.BlockSpec((None, tm, tk), lambda i, k: (k // kh, i, k % kh)),
                  pl.BlockSpec((D_MODEL, tk), lambda i, k: (0, k)), row, row, vec, vec],
        out_specs=[row, row, vec, vec],
        out_shape=[jax.ShapeDtypeStruct((T, D_MODEL), F32), jax.ShapeDtypeStruct((T, D_MODEL), BF16),
                   jax.ShapeDtypeStruct((1, D_MODEL), F32), jax.ShapeDtypeStruct((1, D_MODEL), F32)],
        scratch_shapes=[pltpu.VMEM((tm, D_MODEL), F32)],
        compiler_params=_cp(("arbitrary", "arbitrary")))(du_pre, w_up, dr2, r1, g1, b1)


def _mix_bwd(dr1, w_mo, w_bs, w_ba, bs, ba, proj_main, b_gate, tm=512):
    T = dr1.shape[0]

    def body(d_ref, wmo_ref, wbs_ref, wba_ref, bs_ref, ba_ref, gs_ref, ga_ref, bg_ref,
             dg_ref, dbs_ref, dba_ref, dys_ref, dya_ref, dbg_ref):
        @pl.when(pl.program_id(0) == 0)
        def _():
            dbg_ref[...] = jnp.zeros_like(dbg_ref)

        dm = _dot(d_ref[...].astype(BF16), wmo_ref[...], NT)
        g_s = _sigmoid(gs_ref[...] + bg_ref[:, 0:D_MODEL])
        g_a = _sigmoid(ga_ref[...] + bg_ref[:, D_MODEL:])
        dgs = dm * bs_ref[...] * g_s * (1.0 - g_s)
        dga = dm * ba_ref[...] * g_a * (1.0 - g_a)
        dg_ref[:, 0:D_MODEL] = dgs.astype(BF16)
        dg_ref[:, D_MODEL:] = dga.astype(BF16)
        dbg_ref[:, 0:D_MODEL] += jnp.sum(dgs, axis=0, keepdims=True)
        dbg_ref[:, D_MODEL:] += jnp.sum(dga, axis=0, keepdims=True)
        dbs = (dm * g_s).astype(BF16)
        dba = (dm * g_a).astype(BF16)
        dbs_ref[...] = dbs
        dba_ref[...] = dba
        dys_ref[...] = _dot(dbs, wbs_ref[...], NT)
        dya_ref[...] = _dot(dba, wba_ref[...], NT).astype(BF16)

    row = lambda w, off=0: pl.BlockSpec((tm, w), lambda i: (i, off))
    return pl.pallas_call(
        body, name="mix_bwd", grid=(T // tm,),
        in_specs=[row(D_MODEL), _const_spec((D_MODEL, D_MODEL)), _const_spec((SSM_INNER, D_MODEL)),
                  _const_spec((D_MODEL, D_MODEL)), row(D_MODEL), row(D_MODEL),
                  row(D_MODEL, GATE_OFF // D_MODEL), row(D_MODEL, GATE_OFF // D_MODEL + 1), _const_spec((1, 2 * D_MODEL))],
        out_specs=[row(2 * D_MODEL, GATE_OFF // (2 * D_MODEL)), row(D_MODEL), row(D_MODEL), row(SSM_INNER), row(D_MODEL),
                   _const_spec((1, 2 * D_MODEL))],
        out_shape=[jax.ShapeDtypeStruct((T, MAIN_COLS), BF16), jax.ShapeDtypeStruct((T, D_MODEL), BF16),
                   jax.ShapeDtypeStruct((T, D_MODEL), BF16), jax.ShapeDtypeStruct((T, SSM_INNER), F32),
                   jax.ShapeDtypeStruct((T, D_MODEL), BF16), jax.ShapeDtypeStruct((1, 2 * D_MODEL), F32)],
        compiler_params=_cp(("arbitrary",)))(dr1, w_mo, w_bs, w_ba, bs, ba, proj_main, proj_main, b_gate)


def _local_step(x, target, w, p, late_weights=None, early_grads=None):
    xb = x.astype(BF16)
    if late_weights is None:
        proj_main = _matmul(xb, w["in_main"], "nn", F32, "in_proj_main", tm=1024, tn=2048)
    else:
        proj_main, *landed = _matmul(xb, w["in_main"], "nn", F32, "in_proj_main", tm=1024, tn=2048,
                                     comm=("gather", late_weights[0]))
        w = {**w, **late_weights[1](landed)}
    proj_tail = _matmul(xb, w["in_tail"], "nn", F32, "in_proj_tail", tm=2048, tn=TAIL_COLS)
    xbc, dsil = _conv_silu_fwd(proj_main, XBC_OFF, XBC_COLS, p["ssm_conv_w"], p["ssm_conv_b"], SSM_CONV, "ssm_conv_fwd",
                               tr=2048)
    y_ssm, ypre, hs = _ssd_fwd(xbc, proj_main, proj_tail, p["dtb_col"], p["alog_col"], p["d_exp"], p["ssm_norm_w"])
    y_attn, lse = _attn_fwd(proj_main, proj_tail, p["bias_tbl"], p["attn_sinks"])
    merged, bs, ba = _merge_fwd(y_ssm, y_attn, proj_main, p["b_gate"], w["bs"], w["ba"])
    r1, h1, h1b = _mix_ln1(merged, w["mo"], x, p["ln1_g"], p["ln1_b"])
    u_pre = _matmul(h1b, w["up"], "nn", F32, "ffn_up", tm=1024, tn=2816)
    u, act = _ffn_conv_glu(u_pre, p["ffn_conv_w"], p["ffn_conv_b"])
    dr2, dr2b, dg2, db2, sq = _down_ln2_loss(act, w["down"], h1, target, p["ln2_g"], p["ln2_b"])
    g = {"ln2_g": dg2, "ln2_b": db2}
    g["w_down"] = _matmul(act, dr2b, "tn", BF16, "dw_down", tm=1408, tn=1024, tk=2048)
    dact = _matmul(dr2b, w["down"], "nt", F32, "ffn_dact", tm=1024, tn=1408, tk=1024)
    du_pre, g["ffn_conv_w"], g["ffn_conv_b"] = _ffn_gate_conv_bwd(dact, u, u_pre, p["ffn_conv_w"])
    g["w_up"] = _matmul(h1b, du_pre, "tn", BF16, "dw_up", tm=1024, tn=1408, tk=2048)
    dr1, dr1b, g["ln1_g"], g["ln1_b"] = _ffn_bwd_in(du_pre, w["up"], dr2, r1, p["ln1_g"], p["ln1_b"])
    g["w_mix_out"] = _matmul(merged, dr1b, "tn", BF16, "dw_mix_out", tm=1024, tn=1024, tk=2048)
    dmain, dbs, dba, dy_ssm, dy_attn, g["b_gate"] = _mix_bwd(dr1b, w["mo"], w["bs"], w["ba"], bs, ba, proj_main, p["b_gate"])
    g["w_branch_ssm"] = _matmul(y_ssm, dbs, "tn", BF16, "dw_branch_ssm", tm=1024, tn=1024, tk=2048)
    g["w_branch_attn"] = _matmul(y_attn, dba, "tn", BF16, "dw_branch_attn", tm=1024, tn=1024, tk=2048)
    dmain, dtail, dbias, g["attn_sinks"] = _attn_bwd(dy_attn, lse, proj_main, proj_tail, p["bias_tbl"], p["attn_sinks"], dmain)
    g["rel_bias"] = _rel_bias_grad(dbias.reshape(ATTN_HEADS, WINDOW * 2 * WINDOW), p["bucket_onehot"]).T
    dmain, dtail, dco, g["ssm_norm_w"], dd, dalog, ddtb = _ssd_bwd(
        dy_ssm, ypre, xbc, dsil, hs, proj_main, proj_tail, p["dtb_col"], p["alog_col"], p["d_exp"], p["ssm_norm_w"],
        p["ehead_t"], dmain, dtail)
    g["ssm_d"], g["ssm_a_log"], g["ssm_dt_bias"] = (a.reshape(1, SSM_HEADS) for a in (dd, dalog, ddtb))
    dmain, g["ssm_conv_w"], g["ssm_conv_b"] = _conv_bwd(
        dco, proj_main, XBC_OFF, XBC_COLS, p["ssm_conv_w"], SSM_CONV, dmain, XBC_OFF, "ssm_conv_bwd")
    g["in_tail"] = _matmul(xb, dtail, "tn", BF16, "dw_in_tail", tm=1024, tn=TAIL_COLS, tk=2048)
    landed = []
    if early_grads is None:
        g["in_main"] = _matmul(xb, dmain, "tn", BF16, "dw_in_main", tm=1024, tn=1024, tk=2048)
    else:
        g["in_main"], *landed = _matmul(xb, dmain, "tn", BF16, "dw_in_main", tm=1024, tn=1024, tk=2048,
                                        comm=("exchange", early_grads(g)))
    return sq, (dmain, dtail, dr1), w, g, landed


def _grad_x(dproj, w, exchange=None):
    dmain, dtail, dr1 = dproj
    landed = None
    if exchange is None:
        dx = _matmul(dmain, w["in_main"], "nt", F32, "dx_main", tm=1024, tk=2048, addend=dr1, addend_scale=ALPHA)
    else:
        dx, landed = _matmul(dmain, w["in_main"], "nt", F32, "dx_main", tm=1024, tk=2048, addend=dr1,
                             addend_scale=ALPHA, comm=("exchange", exchange))
    dx = _matmul(dtail, w["in_tail"], "nt", F32, "dx_tail", tm=2048, tk=TAIL_COLS, addend=dx)
    return dx if exchange is None else (dx, landed)


SHARD_COLS = IN_COLS // N_DEV
W_IN_SEGMENTS = ((O_Z, 2048, "main", Z_OFF), (O_XBC, XBC_COLS, "main", XBC_OFF), (O_DT, SSM_HEADS, "tail", DT_OFF),
                 (O_Q, D_MODEL, "main", Q_OFF), (O_K, 128, "tail", K_OFF), (O_V, 128, "tail", V_OFF),
                 (O_GATE, 2 * D_MODEL, "main", GATE_OFF))


def _w_in_from_shards(shards):
    def seg(off, n):
        pieces = []
        for j in range(off // SHARD_COLS, (off + n - 1) // SHARD_COLS + 1):
            lo, hi = max(off, j * SHARD_COLS), min(off + n, (j + 1) * SHARD_COLS)
            pieces.append(shards[j, :, lo - j * SHARD_COLS:hi - j * SHARD_COLS])
        return pieces

    by_name = {(where, koff): seg(off, n) for off, n, where, koff in W_IN_SEGMENTS}
    main = jnp.concatenate(by_name["main", Z_OFF] + by_name["main", XBC_OFF] + by_name["main", Q_OFF]
                           + by_name["main", GATE_OFF], axis=1)
    tail = jnp.concatenate(by_name["tail", K_OFF] + by_name["tail", V_OFF] + by_name["tail", DT_OFF]
                           + [jnp.zeros((shards.shape[1], 128 - SSM_HEADS), shards.dtype)], axis=1)
    return main, tail


def _w_in_grad_by_device(g_main, g_tail):
    slots = []
    for j in range(N_DEV):
        a, b = j * SHARD_COLS, (j + 1) * SHARD_COLS
        pieces = []
        for off, n, where, koff in W_IN_SEGMENTS:
            lo, hi = max(a, off), min(b, off + n)
            if lo < hi:
                pieces.append((g_main if where == "main" else g_tail)[:, koff + lo - off:koff + hi - off])
        slots.append(jnp.concatenate(pieces, axis=1))
    return jnp.stack(slots)


def _prep_params(rel_bias, b_gate, ssm_conv_w, ssm_conv_b, ssm_dt_bias, ssm_a_log, ssm_d, ssm_norm_w, attn_sinks,
                 ln1_g, ln1_b, ffn_conv_w, ffn_conv_b, ln2_g, ln2_b):
    bucket, in_window = _band_geometry()
    bucket, in_window = bucket.T, in_window.T
    onehot = jnp.logical_and(bucket.reshape(-1, 1) == jnp.arange(REL_BUCKETS)[None, :],
                             in_window.reshape(-1, 1)).astype(F32)
    onehot_t = jnp.logical_and(bucket.reshape(1, -1) == jnp.arange(REL_BUCKETS)[:, None],
                               in_window.reshape(1, -1)).astype(BF16)
    bias_tbl = _bias_table(rel_bias.T, onehot_t, jnp.where(in_window.reshape(1, -1), 0.0, NEG))
    ehead_t = (jnp.arange(SSM_INNER)[None, :] // SSMD == jnp.arange(SSM_HEADS)[:, None]).astype(F32)
    return {"bias_tbl": bias_tbl, "bucket_onehot": onehot, "b_gate": b_gate, "ssm_conv_w": ssm_conv_w,
            "ssm_conv_b": ssm_conv_b, "dtb_col": ssm_dt_bias.reshape(SSM_HEADS, 1),
            "alog_col": ssm_a_log.reshape(SSM_HEADS, 1), "ehead_t": ehead_t,
            "d_exp": jnp.repeat(ssm_d, SSMD, axis=1), "ssm_norm_w": ssm_norm_w, "attn_sinks": attn_sinks,
            "ln1_g": ln1_g, "ln1_b": ln1_b, "ffn_conv_w": ffn_conv_w, "ffn_conv_b": ffn_conv_b,
            "ln2_g": ln2_g, "ln2_b": ln2_b}


def _all_gather(shards, name):
    nb = len(shards)

    def body(*refs):
        for phase in _gather_phases(refs[:nb], refs[nb:2 * nb], *refs[2 * nb:]):
            phase()

    any_spec = pl.BlockSpec(memory_space=pl.ANY)
    return pl.pallas_call(
        body, name=name, out_shape=[jax.ShapeDtypeStruct((N_DEV,) + s.shape, s.dtype) for s in shards],
        in_specs=[any_spec] * nb, out_specs=[any_spec] * nb, scratch_shapes=_comm_sems(nb))(*shards)


def _adamw_math(w, g, m, v):
    m = ADAM_B1 * m + (1.0 - ADAM_B1) * g
    v = ADAM_B2 * v + (1.0 - ADAM_B2) * (g * g)
    m_hat = m / (1.0 - ADAM_B1 ** ADAM_STEP)
    v_hat = v / (1.0 - ADAM_B2 ** ADAM_STEP)
    return -ADAM_LR * (m_hat / (jnp.sqrt(v_hat) + ADAM_EPS) + ADAM_WD * w), m, v


def _slot_total(s_ref):
    g = s_ref[0].astype(F32)
    for i in range(1, N_DEV):
        g = g + s_ref[i].astype(F32)
    return g


def _adamw(landed, w, m, v, name):
    R, C = w.shape
    tr = 256 if R % 256 == 0 and R > 256 else R

    def body(s_ref, w_ref, m_ref, v_ref, g_ref, d_ref, nm_ref, nv_ref):
        g = _slot_total(s_ref)
        g_ref[...] = g
        d_ref[...], nm_ref[...], nv_ref[...] = _adamw_math(w_ref[...], g, m_ref[...], v_ref[...])

    spec = pl.BlockSpec((tr, C), lambda i: (i, 0))
    return pl.pallas_call(
        body, name=name, grid=(R // tr,), in_specs=[pl.BlockSpec((N_DEV, tr, C), lambda i: (0, i, 0))] + [spec] * 3,
        out_specs=[spec] * 4, out_shape=[jax.ShapeDtypeStruct((R, C), F32)] * 4,
        compiler_params=_cp(("parallel",)))(landed, w, m, v)


def _small_update(landed, ws, ms, vs):
    k = len(ws)

    def body(*refs):
        s_ref, w_refs, m_refs, v_refs = refs[0], refs[1:1 + k], refs[1 + k:1 + 2 * k], refs[1 + 2 * k:1 + 3 * k]
        outs = refs[1 + 3 * k:]
        g_all = _slot_total(s_ref)
        for i in range(k):
            n = w_refs[i].shape[1]
            g = g_all[i:i + 1, 0:n]
            outs[i][...] = g
            outs[k + i][...], outs[2 * k + i][...], outs[3 * k + i][...] = _adamw_math(
                w_refs[i][...], g, m_refs[i][...], v_refs[i][...])

    return pl.pallas_call(body, name="small_update",
                          out_shape=[jax.ShapeDtypeStruct(w.shape, F32) for w in ws] * 4)(landed, *ws, *ms, *vs)


SHARDED = {"w_in": "cols", "w_branch_ssm": "rows", "w_branch_attn": "rows", "w_mix_out": "rows", "w_up": "cols",
           "w_down": "rows", "ssm_conv_w": "cols", "ffn_conv_w": "cols"}
LATE = ("w_branch_ssm", "w_branch_attn", "w_mix_out", "w_up", "w_down")
SHORT = {"w_branch_ssm": "bs", "w_branch_attn": "ba", "w_mix_out": "mo", "w_up": "up", "w_down": "down"}
SMALL = ("rel_bias", "b_gate", "ssm_conv_b", "ssm_dt_bias", "ssm_a_log", "ssm_d", "ssm_norm_w", "attn_sinks",
         "ln1_g", "ln1_b", "ffn_conv_b", "ln2_g", "ln2_b")
WEIGHTS = ("rel_bias", "w_in", "b_gate", "ssm_conv_w", "ssm_conv_b", "ssm_dt_bias", "ssm_a_log", "ssm_d", "ssm_norm_w",
           "attn_sinks", "w_branch_ssm", "w_branch_attn", "w_mix_out", "ln1_g", "ln1_b", "w_up", "ffn_conv_w",
           "ffn_conv_b", "w_down", "ln2_g", "ln2_b")
SMALL_ROWS, SMALL_COLS = 16, 2 * D_FF


def _by_device(full, how):
    r, c = full.shape
    if how == "rows":
        return full.reshape(N_DEV, r // N_DEV, c)
    return full.reshape(r, N_DEV, c // N_DEV).transpose(1, 0, 2)


def _from_devices(slots, how):
    _, r, c = slots.shape
    if how == "rows":
        return slots.reshape(N_DEV * r, c)
    return slots.transpose(1, 0, 2).reshape(r, N_DEV * c)


def kernel(x, rel_bias, w_in, b_gate, ssm_conv_w, ssm_conv_b, ssm_dt_bias, ssm_a_log, ssm_d, ssm_norm_w, attn_sinks, w_branch_ssm, w_branch_attn, w_mix_out, ln1_g, ln1_b, w_up, ffn_conv_w, ffn_conv_b, w_down, ln2_g, ln2_b, loss_target, m_rel_bias, m_w_in, m_b_gate, m_ssm_conv_w, m_ssm_conv_b, m_ssm_dt_bias, m_ssm_a_log, m_ssm_d, m_ssm_norm_w, m_attn_sinks, m_w_branch_ssm, m_w_branch_attn, m_w_mix_out, m_ln1_g, m_ln1_b, m_w_up, m_ffn_conv_w, m_ffn_conv_b, m_w_down, m_ln2_g, m_ln2_b, v_rel_bias, v_w_in, v_b_gate, v_ssm_conv_w, v_ssm_conv_b, v_ssm_dt_bias, v_ssm_a_log, v_ssm_d, v_ssm_norm_w, v_attn_sinks, v_w_branch_ssm, v_w_branch_attn, v_w_mix_out, v_ln1_g, v_ln1_b, v_w_up, v_ffn_conv_w, v_ffn_conv_b, v_w_down, v_ln2_g, v_ln2_b):
    W = dict(zip(WEIGHTS, (rel_bias, w_in, b_gate, ssm_conv_w, ssm_conv_b, ssm_dt_bias, ssm_a_log, ssm_d, ssm_norm_w,
                           attn_sinks, w_branch_ssm, w_branch_attn, w_mix_out, ln1_g, ln1_b, w_up, ffn_conv_w,
                           ffn_conv_b, w_down, ln2_g, ln2_b)))
    M = dict(zip(WEIGHTS, (m_rel_bias, m_w_in, m_b_gate, m_ssm_conv_w, m_ssm_conv_b, m_ssm_dt_bias, m_ssm_a_log, m_ssm_d,
                           m_ssm_norm_w, m_attn_sinks, m_w_branch_ssm, m_w_branch_attn, m_w_mix_out, m_ln1_g, m_ln1_b,
                           m_w_up, m_ffn_conv_w, m_ffn_conv_b, m_w_down, m_ln2_g, m_ln2_b)))
    V = dict(zip(WEIGHTS, (v_rel_bias, v_w_in, v_b_gate, v_ssm_conv_w, v_ssm_conv_b, v_ssm_dt_bias, v_ssm_a_log, v_ssm_d,
                           v_ssm_norm_w, v_attn_sinks, v_w_branch_ssm, v_w_branch_attn, v_w_mix_out, v_ln1_g, v_ln1_b,
                           v_w_up, v_ffn_conv_w, v_ffn_conv_b, v_w_down, v_ln2_g, v_ln2_b)))
    shard2d = lambda a: a.reshape(a.shape[-2], a.shape[-1])

    (win_all,) = _all_gather([shard2d(w_in).astype(BF16)], "gather_w_in")
    main, tail = _w_in_from_shards(win_all)
    conv_all = _all_gather([shard2d(ssm_conv_w), shard2d(ffn_conv_w)], "gather_conv_weights")
    late_shards = [shard2d(W[n]).astype(BF16) for n in LATE]
    late = lambda landed: {SHORT[n]: _from_devices(a, SHARDED[n]) for n, a in zip(LATE, landed)}
    p = _prep_params(rel_bias, b_gate, _from_devices(conv_all[0], "cols"), ssm_conv_b, ssm_dt_bias, ssm_a_log, ssm_d,
                     ssm_norm_w, attn_sinks, ln1_g, ln1_b, _from_devices(conv_all[1], "cols"), ffn_conv_b, ln2_g, ln2_b)

    early_names = LATE + ("ssm_conv_w", "ffn_conv_w")
    early = lambda g: [_by_device(g[n], SHARDED[n]).astype(BF16 if n in LATE else F32) for n in early_names]
    sq, dproj, w, g, landed = _local_step(x[0], loss_target[0], {"in_main": main, "in_tail": tail}, p,
                                          (late_shards, late), early)
    landed = dict(zip(early_names, landed))
    dx, landed["w_in"] = _grad_x(dproj, w, exchange=[_w_in_grad_by_device(g.pop("in_main"), g.pop("in_tail"))])
    loss = (0.5 / D_MODEL) * lax.psum(sq[0, 0], ("x", "y", "c"))
    grads, deltas, new_m, new_v = {}, {}, {}, {}
    for n in SHARDED:
        outs = _adamw(landed[n], shard2d(W[n]), shard2d(M[n]), shard2d(V[n]), "adamw_" + n)
        grads[n], deltas[n], new_m[n], new_v[n] = (a.reshape(W[n].shape) for a in outs)

    row = lambda a: a.reshape(1, -1)
    packed = jnp.concatenate([jnp.pad(row(g[n]), ((0, 0), (0, SMALL_COLS - g[n].size))) for n in SMALL]
                             + [jnp.zeros((SMALL_ROWS - len(SMALL), SMALL_COLS), F32)], axis=0)
    (small_all,) = _all_gather([packed], "gather_small_grads")
    outs = _small_update(small_all, *[[row(src[n]) for n in SMALL] for src in (W, M, V)])
    for i, n in enumerate(SMALL):
        grads[n], deltas[n], new_m[n], new_v[n] = (outs[j * len(SMALL) + i].reshape(W[n].shape) for j in range(4))

    return (loss, dx[None], *[grads[n] for n in WEIGHTS], *[deltas[n] for n in WEIGHTS],
            *[new_m[n] for n in WEIGHTS], *[new_v[n] for n in WEIGHTS])
```

```python
import math

import jax
import jax.numpy as jnp
from jax import lax
from jax.experimental import pallas as pl
from jax.experimental.pallas import tpu as pltpu

F32, BF16 = jnp.float32, jnp.bfloat16
HIGHEST = lax.Precision.HIGHEST
MESH_ID = pl.DeviceIdType.MESH

N_DEV = 8
D_MODEL = 1024
SSM_INNER = 2048
SSM_HEADS = 32
SSM_HEADDIM = 64
SSMD = SSM_HEADDIM
SSM_GROUPS = 4
SSM_GROUP_COLS = SSM_INNER // SSM_GROUPS
SSM_STATE = 128
SSM_CONV = 4
CHUNK = 128
XBC_COLS = SSM_INNER + 2 * SSM_GROUPS * SSM_STATE
B_OFF = SSM_INNER
C_OFF = SSM_INNER + SSM_GROUPS * SSM_STATE
ATTN_HEADS = 16
ATTN_KV = 2
ATTN_GROUP = 8
HEADDIM = 64
WINDOW = 128
REL_BUCKETS = 32
REL_MAX_DIST = 128
D_FF = 2816
FFN_CONV = 3
ALPHA = 2.0 ** 0.25
LN_EPS = 1e-5
RMS_EPS = 1e-5
IN_COLS = 8480
Z_OFF, XBC_OFF, Q_OFF, GATE_OFF, MAIN_COLS = 0, 2048, 5120, 6144, 8192
K_OFF, V_OFF, DT_OFF, TAIL_COLS = 0, 128, 256, 384
O_Z, O_XBC, O_DT, O_Q, O_K, O_V, O_GATE = 0, 2048, 5120, 5152, 6176, 6304, 6432

ADAM_LR, ADAM_B1, ADAM_B2, ADAM_EPS, ADAM_WD, ADAM_STEP = 0.001, 0.9, 0.999, 1e-08, 0.01, 10
NEG = -1e30
HALO = 8
VMEM_LIMIT = 56 * 1024 * 1024


def _cp(sem):
    return pltpu.CompilerParams(dimension_semantics=sem, vmem_limit_bytes=VMEM_LIMIT)


def _const_spec(shape):
    nd = len(shape)
    return pl.BlockSpec(shape, lambda *_: (0,) * nd)


def _sigmoid(x):
    return 0.5 * jnp.tanh(0.5 * x) + 0.5


def _softplus(x):
    return jnp.maximum(x, 0.0) + jnp.log1p(jnp.exp(-jnp.abs(x)))


def _dot(a, b, dims=(((1,), (0,)), ((), ())), precision=None):
    return lax.dot_general(a, b, dims, preferred_element_type=F32, precision=precision)


NN = (((1,), (0,)), ((), ()))
NT = (((1,), (1,)), ((), ()))
TN = (((0,), (0,)), ((), ()))


def _mesh_pos():
    return lax.axis_index("x"), lax.axis_index("y"), lax.axis_index("c")


PEERS = N_DEV - 1


def _exchange_phases(in_refs, out_refs, send_sems, recv_sems, local_sems):
    def copies():
        x, y, c = _mesh_pos()
        me = 4 * x + 2 * y + c
        cps = []
        for b, (in_ref, out_ref) in enumerate(zip(in_refs, out_refs)):
            cps.append(pltpu.make_async_copy(in_ref.at[me], out_ref.at[me], local_sems.at[b]))
            for r in range(1, N_DEV):
                px = 1 - x if r & 4 else x
                py = 1 - y if r & 2 else y
                pc = 1 - c if r & 1 else c
                cps.append(pltpu.make_async_remote_copy(
                    src_ref=in_ref.at[4 * px + 2 * py + pc], dst_ref=out_ref.at[me],
                    send_sem=send_sems.at[b * PEERS + r - 1], recv_sem=recv_sems.at[b * PEERS + r - 1],
                    device_id=(px, py, pc), device_id_type=MESH_ID))
        return cps

    def start():
        for cp in copies():
            cp.start()

    def finish():
        for cp in copies():
            cp.wait()

    return [start, finish]


def _gather_phases(x_refs, out_refs, send_sems, recv_sems, local_sems):
    def parts(which):
        x, y, c = _mesh_pos()
        me, sibling = (x, y, c), (x, y, 1 - c)
        chips = [(1 - x, y), (x, 1 - y), (1 - x, 1 - y)]
        found = []
        for b, (x_ref, out_ref) in enumerate(zip(x_refs, out_refs)):
            def slot(px, py, pc):
                return out_ref.at[4 * px + 2 * py + pc]

            def copy(k, block, to, src=None):
                return pltpu.make_async_remote_copy(
                    src_ref=slot(*block) if src is None else src, dst_ref=slot(*block),
                    send_sem=send_sems.at[b * PEERS + k], recv_sem=recv_sems.at[b * PEERS + k],
                    device_id=to, device_id_type=MESH_ID)

            if which == "mine":
                found.append(pltpu.make_async_copy(x_ref, slot(*me), local_sems.at[b]))
            elif which == "first":
                found.append(copy(0, me, sibling, src=x_ref))
                found += [copy(1 + j, me, (*chip, c), src=x_ref) for j, chip in enumerate(chips)]
            elif which == "passed":
                found += [copy(4 + j, (*chip, c), sibling) for j, chip in enumerate(chips)]
            elif which == "arrived":
                found += [copy(1 + j, (*chip, c), me) for j, chip in enumerate(chips)]
            else:
                found.append(copy(0, sibling, me))
                found += [copy(4 + j, (*chip, 1 - c), me) for j, chip in enumerate(chips)]
        return found

    def start():
        for cp in parts("mine") + parts("first"):
            cp.start()

    def forward():
        for a, p in zip(parts("arrived"), parts("passed")):
            a.wait_recv()
            p.start()

    def finish():
        for cp in parts("late"):
            cp.wait_recv()
        for cp in parts("first") + parts("passed"):
            cp.wait_send()
        for cp in parts("mine"):
            cp.wait()

    return [start, forward, finish]


COMM = {"exchange": _exchange_phases, "gather": _gather_phases}


def _comm_sems(nb):
    return [pltpu.SemaphoreType.DMA((nb * PEERS,)), pltpu.SemaphoreType.DMA((nb * PEERS,)), pltpu.SemaphoreType.DMA((nb,))]


def _matmul(a, b, mode, out_dtype, name, tm=512, tn=1024, tk=1024, addend=None, addend_scale=1.0, comm=None):
    bufs = [] if comm is None else list(comm[1])
    nb = len(bufs)
    halves = b.ndim == 3
    if mode == "nn":
        (M, K), (K2, N) = a.shape, b.shape
    elif mode == "nt":
        (M, K), (N, K2) = a.shape, b.shape
    elif halves:
        (K, M), (K2, N) = a.shape, (b.shape[1], 2 * b.shape[2])
    else:
        (K, M), (K2, N) = a.shape, b.shape
    assert K == K2, (a.shape, b.shape, mode)
    tm, tn, tk = min(tm, M), min(tn, N), min(tk, K)
    assert M % tm == 0 and N % tn == 0 and K % tk == 0, (M, N, K, tm, tn, tk)
    nk = K // tk
    dims = {"nn": NN, "nt": NT, "tn": TN}[mode]
    a_spec = pl.BlockSpec((tk, tm), lambda i, j, k: (k, i)) if mode == "tn" else pl.BlockSpec((tm, tk), lambda i, j, k: (i, k))
    b_spec = pl.BlockSpec((tn, tk), lambda i, j, k: (j, k)) if mode == "nt" else pl.BlockSpec((tk, tn), lambda i, j, k: (k, j))
    o_spec = pl.BlockSpec((tm, tn), lambda i, j, k: (i, j))

    ni, nj = M // tm, N // tn
    if halves:
        assert mode == "tn" and nj % 2 == 0
        b_spec = pl.BlockSpec((None, tk, tn), lambda i, j, k: (j // (nj // 2), k, j % (nj // 2)))

    def body(*refs):
        refs = list(refs)
        a_ref, b_ref = refs[:2]
        c_ref = refs[2] if addend is not None else None
        n_in = 2 + (addend is not None) + nb
        o_ref = refs[n_in]
        acc = refs[n_in + 1 + nb] if nk > 1 else None
        i, j, k = pl.program_id(0), pl.program_id(1), pl.program_id(2)
        step = (i * nj + j) * nk + k
        if comm is not None:
            phases = COMM[comm[0]](refs[n_in - nb:n_in], refs[n_in + 1:n_in + 1 + nb], *refs[n_in + 1 + nb + (nk > 1):])
            at = [(ni * nj * nk - 1) * p // (len(phases) - 1) for p in range(len(phases))]
            for when, phase in zip(at[:-1], phases[:-1]):
                pl.when(step == when)(phase)

        d = _dot(a_ref[...].astype(BF16), b_ref[...].astype(BF16), dims)

        def finish(r):
            if addend is not None:
                r = r + addend_scale * c_ref[...].astype(F32)
            o_ref[...] = r.astype(out_dtype)

        if nk == 1:
            finish(d)
        else:
            @pl.when(k == 0)
            def _():
                acc[...] = d

            @pl.when(jnp.logical_and(k > 0, k < nk - 1))
            def _():
                acc[...] += d

            @pl.when(k == nk - 1)
            def _():
                finish(acc[...] + d)

        if comm is not None:
            pl.when(step == at[-1])(phases[-1])

    in_specs = [a_spec, b_spec] + ([o_spec] if addend is not None else [])
    args = (a, b) + ((addend,) if addend is not None else ())
    out_specs, out_shape = o_spec, jax.ShapeDtypeStruct((M, N), out_dtype)
    scratch = [pltpu.VMEM((tm, tn), F32)] if nk > 1 else []
    sem = ("parallel", "parallel", "arbitrary")
    if comm is not None:
        any_spec = pl.BlockSpec(memory_space=pl.ANY)
        in_specs, args = in_specs + [any_spec] * nb, args + tuple(bufs)
        landed = [x.shape if comm[0] == "exchange" else (N_DEV,) + x.shape for x in bufs]
        out_specs = [o_spec] + [any_spec] * nb
        out_shape = [out_shape] + [jax.ShapeDtypeStruct(s, x.dtype) for s, x in zip(landed, bufs)]
        scratch += _comm_sems(nb)
        sem = ("arbitrary", "arbitrary", "arbitrary")
    return pl.pallas_call(
        body, name=name, grid=(ni, nj, nk), in_specs=in_specs, out_specs=out_specs, out_shape=out_shape,
        scratch_shapes=scratch, compiler_params=_cp(sem))(*args)


def _taps(w_ref, K, tc):
    return [jnp.broadcast_to(w_ref[k:k + 1, :], (HALO, tc)) for k in range(K)]


def _conv_silu_fwd(pre, pre_col_off, C, w, b, K, name, tr=1024, tc=512):
    T = pre.shape[0]
    tr, tc = min(tr, T), min(tc, C)
    assert T % tr == 0 and C % tc == 0 and pre_col_off % tc == 0
    joff = pre_col_off // tc
    hb = tr // HALO

    def body(x_ref, xp_ref, w_ref, b_ref, o_ref, d_ref, head):
        i = pl.program_id(1)
        head[0:HALO, :] = jnp.where(i > 0, xp_ref[...], 0.0)
        head[HALO:, :] = x_ref[0:HALO, :]
        wk = _taps(w_ref, K, tc)
        bias = jnp.broadcast_to(b_ref[...], (HALO, tc))
        for r in range(tr // HALO):
            lo = r * HALO
            co = bias + wk[K - 1] * x_ref[lo:lo + HALO, :]
            for k in range(K - 1):
                s = K - 1 - k
                co = co + wk[k] * (head[HALO - s:2 * HALO - s, :] if r == 0 else x_ref[lo - s:lo + HALO - s, :])
            sg = _sigmoid(co)
            y = co * sg
            o_ref[lo:lo + HALO, :] = y
            d_ref[lo:lo + HALO, :] = sg + y * (1.0 - sg)

    out = pl.BlockSpec((tr, tc), lambda j, i: (i, j))
    return pl.pallas_call(
        body, name=name, grid=(C // tc, T // tr),
        in_specs=[pl.BlockSpec((tr, tc), lambda j, i: (i, joff + j)),
                  pl.BlockSpec((HALO, tc), lambda j, i: (jnp.maximum(i * hb - 1, 0), joff + j)),
                  pl.BlockSpec((K, tc), lambda j, i: (0, j)),
                  pl.BlockSpec((1, tc), lambda j, i: (0, j))],
        out_specs=[out, out], out_shape=[jax.ShapeDtypeStruct((T, C), F32)] * 2,
        scratch_shapes=[pltpu.VMEM((2 * HALO, tc), F32)],
        compiler_params=_cp(("parallel", "arbitrary")))(pre, pre, w, b)


def _conv_bwd(dout, pre, pre_col_off, C, w, K, dst, dst_col_off, name, tr=1024, tc=512):
    T = pre.shape[0]
    tr, tc = min(tr, T), min(tc, C)
    assert T % tr == 0 and C % tc == 0 and pre_col_off % tc == 0 and dst_col_off % tc == 0
    joff, doff = pre_col_off // tc, dst_col_off // tc
    hb = tr // HALO
    nt = T // tr
    n = tr // HALO
    last_hblock = T // HALO - 1

    def body(g_ref, gn_ref, x_ref, w_ref, *rest):
        o_ref, dw_ref, db_ref, edge = rest[-4:]
        i = pl.program_id(1)
        wk = _taps(w_ref, K, tc)
        edge[0:HALO, :] = g_ref[tr - HALO:tr, :]
        edge[HALO:, :] = jnp.where(i < nt - 1, gn_ref[...], 0.0)
        acc_w = [jnp.zeros((HALO, tc), F32) for _ in range(K)]
        acc_b = jnp.zeros((HALO, tc), F32)
        for r in range(n):
            lo = r * HALO
            x = x_ref[lo:lo + HALO, :]
            dpre = None
            for s in range(K):
                gs = edge[s:HALO + s, :] if (r == n - 1 and s > 0) else g_ref[lo + s:lo + HALO + s, :]
                dpre = wk[K - 1 - s] * gs if dpre is None else dpre + wk[K - 1 - s] * gs
                acc_w[K - 1 - s] = acc_w[K - 1 - s] + gs * x
                if s == 0:
                    acc_b = acc_b + gs
            o_ref[lo:lo + HALO, :] = dpre.astype(o_ref.dtype)

        @pl.when(i == 0)
        def _():
            dw_ref[...] = jnp.zeros_like(dw_ref)
            db_ref[...] = jnp.zeros_like(db_ref)

        db_ref[...] += jnp.sum(acc_b, axis=0, keepdims=True)
        dw_ref[...] += jnp.concatenate([jnp.sum(a, axis=0, keepdims=True) for a in acc_w], axis=0)

    tile = lambda off: pl.BlockSpec((tr, tc), lambda j, i: (i, off + j))
    in_specs = [tile(0), pl.BlockSpec((HALO, tc), lambda j, i: (jnp.minimum((i + 1) * hb, last_hblock), j)),
                tile(joff), pl.BlockSpec((K, tc), lambda j, i: (0, j))]
    args = (dout, dout, pre, w)
    if isinstance(dst, jax.ShapeDtypeStruct):
        aliases = {}
    else:
        in_specs.append(pl.BlockSpec(memory_space=pl.ANY))
        args += (dst,)
        aliases = {4: 0}
    return pl.pallas_call(
        body, name=name, grid=(C // tc, nt), in_specs=in_specs,
        out_specs=[tile(doff), pl.BlockSpec((K, tc), lambda j, i: (0, j)), pl.BlockSpec((1, tc), lambda j, i: (0, j))],
        out_shape=[jax.ShapeDtypeStruct(dst.shape, dst.dtype), jax.ShapeDtypeStruct((K, C), F32),
                   jax.ShapeDtypeStruct((1, C), F32)],
        scratch_shapes=[pltpu.VMEM((2 * HALO, tc), F32)],
        input_output_aliases=aliases,
        compiler_params=_cp(("parallel", "arbitrary")))(*args)


PAIR = 2 * SSMD
PAIRS_PER_GROUP = SSM_GROUP_COLS // PAIR


def _dot3(x, onehot):
    h1 = x.astype(BF16)
    r = x - h1.astype(F32)
    h2 = r.astype(BF16)
    h3 = (r - h2.astype(F32)).astype(BF16)
    return _dot(h1, onehot) + _dot(h2, onehot) + _dot(h3, onehot)


def _chunk_rows(dt_raw, dtb_col, alog_col):
    row = lax.broadcasted_iota(jnp.int32, (CHUNK, CHUNK), 0)
    col = lax.broadcasted_iota(jnp.int32, (CHUNK, CHUNK), 1)
    dt_rawT = dt_raw.T
    dtT = _softplus(dt_rawT + dtb_col)
    a_col = -jnp.exp(alog_col)
    acsT = _dot3(dtT * a_col, (row <= col).astype(BF16))
    return dt_rawT, dtT, a_col, acsT, row, col


def _block_diag(x, left):
    return jnp.concatenate([jnp.where(left, x, 0.0), jnp.where(left, 0.0, x)], axis=0).astype(BF16)


def _lane_bcast(v, h):
    return jnp.broadcast_to(v[:, h:h + 1], (CHUNK, CHUNK))


def _ssd_fwd(xbc, proj_main, proj_tail, dtb_col, alog_col, d_exp, norm_w):
    T = xbc.shape[0]
    nc = T // CHUNK

    def body(xbc_ref, dt_ref, z_ref, dtb_ref, alog_ref, d_ref, nw_ref, y_ref, ypre_ref, hs_ref, H):
        c = pl.program_id(0)

        @pl.when(c == 0)
        def _():
            H[...] = jnp.zeros_like(H)

        hs_ref[0] = H[...]
        _, dtT, _, acsT, row, col = _chunk_rows(dt_ref[:, 0:SSM_HEADS], dtb_ref[...], alog_ref[...])
        tril, left = row >= col, col < SSMD
        acs = acsT.T
        w = (dtT * jnp.exp(acsT[:, CHUNK - 1:CHUNK] - acsT)).T
        cd = jnp.exp(acs[CHUNK - 1:CHUNK, :])
        for g in range(SSM_GROUPS):
            gs = slice(g * SSM_GROUP_COLS, (g + 1) * SSM_GROUP_COLS)
            Bb = xbc_ref[:, B_OFF + g * SSM_STATE:B_OFF + (g + 1) * SSM_STATE].astype(BF16)
            Cb = xbc_ref[:, C_OFF + g * SSM_STATE:C_OFF + (g + 1) * SSM_STATE].astype(BF16)
            Hg = H[:, gs]
            CH = _dot(Cb, Hg.astype(BF16))
            CB = _dot(Cb, Bb, NT)
            ys, xws = [], []
            for kk in range(PAIRS_PER_GROUP):
                k = g * PAIRS_PER_GROUP + kk
                xs_p = xbc_ref[:, k * PAIR:(k + 1) * PAIR]
                mps, ecols, wcols = [], [], []
                for j in range(2):
                    h = 2 * k + j
                    colb = _lane_bcast(acs, h)
                    L = jnp.exp(jnp.where(tril, colb - acsT[h:h + 1, :], -jnp.inf))
                    mps.append((CB * L * dtT[h:h + 1, :]).astype(BF16))
                    ecols.append(jnp.exp(colb))
                    wcols.append(_lane_bcast(w, h))
                yd = _dot(jnp.concatenate(mps, axis=1), _block_diag(xs_p, left))
                ys.append(yd + CH[:, kk * PAIR:(kk + 1) * PAIR] * jnp.where(left, ecols[0], ecols[1]))
                xws.append((xs_p * jnp.where(left, wcols[0], wcols[1])).astype(BF16))
            cd_e = jnp.concatenate([jnp.broadcast_to(cd[:, g * 8 + e:g * 8 + e + 1], (1, SSMD)) for e in range(8)], axis=1)
            H[:, gs] = Hg * cd_e + _dot(Bb, jnp.concatenate(xws, axis=1), TN)
            ypre = jnp.concatenate(ys, axis=1) + xbc_ref[:, gs] * d_ref[:, gs]
            ypre_ref[:, gs] = ypre
            z = z_ref[:, gs]
            yg = ypre * (z * _sigmoid(z))
            r = lax.rsqrt(jnp.mean(yg * yg, axis=1, keepdims=True) + RMS_EPS)
            y_ref[:, gs] = (yg * r * nw_ref[:, gs]).astype(BF16)

    vec = lambda n: _const_spec((1, n))
    colv = _const_spec((SSM_HEADS, 1))
    return pl.pallas_call(
        body, name="ssd_fwd", grid=(nc,),
        in_specs=[pl.BlockSpec((CHUNK, XBC_COLS), lambda c: (c, 0)),
                  pl.BlockSpec((CHUNK, 128), lambda c: (c, DT_OFF // 128)),
                  pl.BlockSpec((CHUNK, SSM_INNER), lambda c: (c, Z_OFF // SSM_INNER)),
                  colv, colv, vec(SSM_INNER), vec(SSM_INNER)],
        out_specs=[pl.BlockSpec((CHUNK, SSM_INNER), lambda c: (c, 0)),
                   pl.BlockSpec((CHUNK, SSM_INNER), lambda c: (c, 0)),
                   pl.BlockSpec((1, SSM_STATE, SSM_INNER), lambda c: (c, 0, 0))],
        out_shape=[jax.ShapeDtypeStruct((T, SSM_INNER), BF16), jax.ShapeDtypeStruct((T, SSM_INNER), F32),
                   jax.ShapeDtypeStruct((nc, SSM_STATE, SSM_INNER), F32)],
        scratch_shapes=[pltpu.VMEM((SSM_STATE, SSM_INNER), F32)],
        compiler_params=_cp(("arbitrary",)))(xbc, proj_tail, proj_main, dtb_col, alog_col, d_exp, norm_w)


def _ssd_bwd(dyo, ypre, xbc, dsil, hs, proj_main, proj_tail, dtb_col, alog_col, d_exp, norm_w, ehead_t, dmain, dtail):
    T = xbc.shape[0]
    nc = T // CHUNK

    def body(dyo_ref, ypre_ref, xbc_ref, dsil_ref, hs_ref, dt_ref, z_ref, dtb_ref, alog_ref, d_ref, nw_ref, eh_ref,
             dmain_in, dtail_in, dz_ref, ddt_ref, dxbc_ref, dnw_ref, dd_ref, dalog_ref, ddtb_ref, G):
        del dmain_in, dtail_in
        c = pl.program_id(0)

        @pl.when(c == 0)
        def _():
            G[...] = jnp.zeros_like(G)
            dnw_ref[...] = jnp.zeros_like(dnw_ref)
            dd_ref[...] = jnp.zeros_like(dd_ref)
            dalog_ref[...] = jnp.zeros_like(dalog_ref)
            ddtb_ref[...] = jnp.zeros_like(ddtb_ref)

        dt_rawT, dtT, a_col, acsT, row, col = _chunk_rows(dt_ref[:, 0:SSM_HEADS], dtb_ref[...], alog_ref[...])
        tril, triu, left = row >= col, col >= row, col < SSMD
        acs = acsT.T
        dt = dtT.T
        lastT = acsT[:, CHUNK - 1:CHUNK]
        dstT = jnp.exp(lastT - acsT)
        wT = dtT * dstT
        cd = jnp.exp(acs[CHUNK - 1:CHUNK, :])
        ddt_rows, rs_rows, deo_rows, dw_rows = [], [], [], []
        dd_cols, gh_cols, dnw_cols = [], [], []
        for g in range(SSM_GROUPS):
            gs = slice(g * SSM_GROUP_COLS, (g + 1) * SSM_GROUP_COLS)
            z = z_ref[:, gs]
            sz = _sigmoid(z)
            silu_z = z * sz
            ypre = ypre_ref[:, gs]
            yg = ypre * silu_z
            r = lax.rsqrt(jnp.mean(yg * yg, axis=1, keepdims=True) + RMS_EPS)
            ygn = yg * r
            dyo = dyo_ref[:, gs]
            dyn = dyo * nw_ref[:, gs]
            dnw_cols.append(jnp.sum(dyo * ygn, axis=0, keepdims=True))
            dyg = r * (dyn - ygn * jnp.mean(dyn * ygn, axis=1, keepdims=True))
            dz_ref[:, gs] = (dyg * ypre * (sz * (1.0 + z * (1.0 - sz)))).astype(dz_ref.dtype)
            dY = dyg * silu_z
            xs = xbc_ref[:, gs]
            dd_cols.append(jnp.sum(dY * xs, axis=0, keepdims=True))
            Bf = xbc_ref[:, B_OFF + g * SSM_STATE:B_OFF + (g + 1) * SSM_STATE]
            Cf = xbc_ref[:, C_OFF + g * SSM_STATE:C_OFF + (g + 1) * SSM_STATE]
            Bb, Cb = Bf.astype(BF16), Cf.astype(BF16)
            BT, CT = Bf.T, Cf.T
            CB = _dot(Cb, Bb, NT)
            CBT = _dot(Bb, Cb, NT)
            Hg = hs_ref[0, :, gs]
            Gg = G[:, gs]
            gh_cols.append(jnp.sum(Gg * Hg, axis=0, keepdims=True))
            dCB = jnp.zeros((CHUNK, CHUNK), F32)
            dxs_d, dyes, xws, wsels = [], [], [], []
            for kk in range(PAIRS_PER_GROUP):
                k = g * PAIRS_PER_GROUP + kk
                ps = slice(kk * PAIR, (kk + 1) * PAIR)
                xs_p, dY_p = xs[:, ps], dY[:, ps]
                Ls, LTs, dtcols, ecols, wcols = [], [], [], [], []
                for j in range(2):
                    h = 2 * k + j
                    colb = _lane_bcast(acs, h)
                    seg = colb - acsT[h:h + 1, :]
                    Ls.append(jnp.exp(jnp.where(tril, seg, -jnp.inf)))
                    LTs.append(jnp.exp(jnp.where(triu, -seg, -jnp.inf)))
                    dtcol = _lane_bcast(dt, h)
                    dtcols.append(dtcol)
                    ecols.append(jnp.exp(colb))
                    wcols.append(dtcol * jnp.exp(acs[CHUNK - 1:CHUNK, h:h + 1] - colb))
                wsel = jnp.where(left, wcols[0], wcols[1])
                dYe_p = dY_p * jnp.where(left, ecols[0], ecols[1])
                bdx = _block_diag(xs_p, left)
                bddy = _block_diag(dY_p, left)
                dMx2 = _dot(dY_p.astype(BF16), bdx, NT)
                dMxT2 = _dot(xs_p.astype(BF16), bddy, NT)
                Q1 = _dot(Hg[:, ps].astype(BF16), _block_diag(dYe_p, left), NT)
                Q2 = _dot(Gg[:, ps].astype(BF16), bdx, NT)
                mts = []
                for j in range(2):
                    h = 2 * k + j
                    js = slice(j * CHUNK, (j + 1) * CHUNK)
                    dMx = dMx2[:, js]
                    A = CB * Ls[j]
                    AT = CBT * LTs[j]
                    ddt_rows.append(jnp.sum(A * dMx, axis=0, keepdims=True))
                    ATd = AT * dtcols[j]
                    rs_rows.append(jnp.sum(ATd * dMxT2[:, js], axis=0, keepdims=True))
                    dCB = dCB + dMx * Ls[j] * dtT[h:h + 1, :]
                    mts.append(ATd.astype(BF16))
                    deo_rows.append(jnp.sum(CT * Q1[:, js], axis=0, keepdims=True))
                    dw_rows.append(jnp.sum(BT * Q2[:, js], axis=0, keepdims=True))
                dxs_d.append(_dot(jnp.concatenate(mts, axis=1), bddy))
                dyes.append(dYe_p.astype(BF16))
                xws.append((xs_p * wsel).astype(BF16))
                wsels.append(wsel)
            dYe_g = jnp.concatenate(dyes, axis=1)
            xw_g = jnp.concatenate(xws, axis=1)
            Hgb, Ggb, dCBb = Hg.astype(BF16), Gg.astype(BF16), dCB.astype(BF16)
            cs = slice(C_OFF + g * SSM_STATE, C_OFF + (g + 1) * SSM_STATE)
            bs = slice(B_OFF + g * SSM_STATE, B_OFF + (g + 1) * SSM_STATE)
            dxbc_ref[:, cs] = (_dot(dYe_g, Hgb, NT) + _dot(dCBb, Bb)) * dsil_ref[:, cs]
            dxbc_ref[:, bs] = (_dot(xw_g, Ggb, NT) + _dot(dCBb, Cb, TN)) * dsil_ref[:, bs]
            BG = _dot(Bb, Ggb)
            dxbc_ref[:, gs] = (jnp.concatenate(dxs_d, axis=1) + BG * jnp.concatenate(wsels, axis=1)
                               + dY * d_ref[:, gs]) * dsil_ref[:, gs]
            cd_e = jnp.concatenate([jnp.broadcast_to(cd[:, g * 8 + e:g * 8 + e + 1], (1, SSMD)) for e in range(8)], axis=1)
            G[:, gs] = Gg * cd_e + _dot(Cb, dYe_g, TN)
        dnw_ref[...] += jnp.concatenate(dnw_cols, axis=1)
        eh = eh_ref[...]
        dd_ref[...] += jnp.sum(eh * jnp.concatenate(dd_cols, axis=1), axis=1, keepdims=True)
        dcd = jnp.sum(eh * jnp.concatenate(gh_cols, axis=1), axis=1, keepdims=True)
        DDT = jnp.concatenate(ddt_rows, axis=0)
        DW = jnp.concatenate(dw_rows, axis=0)
        DWw = DW * wT
        dacsT = jnp.concatenate(rs_rows, axis=0) - DDT * dtT + jnp.concatenate(deo_rows, axis=0) - DWw
        end = jnp.sum(DWw, axis=1, keepdims=True) + dcd * jnp.exp(lastT)
        lane = lax.broadcasted_iota(jnp.int32, (SSM_HEADS, CHUNK), 1)
        dacsT = dacsT + jnp.where(lane == CHUNK - 1, end, 0.0)
        dadtT = _dot3(dacsT, tril.astype(BF16))
        ddtT = dadtT * a_col + DDT + DW * dstT
        dalog_ref[...] += jnp.sum(dadtT * dtT, axis=1, keepdims=True) * a_col
        ddt_rawT = ddtT * _sigmoid(dt_rawT + dtb_ref[...])
        ddtb_ref[...] += jnp.sum(ddt_rawT, axis=1, keepdims=True)
        ddt_ref[...] = jnp.concatenate([ddt_rawT.T, jnp.zeros((CHUNK, 128 - SSM_HEADS), F32)], axis=1).astype(ddt_ref.dtype)

    rev = lambda c: nc - 1 - c
    vec = lambda n: _const_spec((1, n))
    colv = _const_spec((SSM_HEADS, 1))
    any_spec = pl.BlockSpec(memory_space=pl.ANY)
    return pl.pallas_call(
        body, name="ssd_bwd", grid=(nc,),
        in_specs=[pl.BlockSpec((CHUNK, SSM_INNER), lambda c: (rev(c), 0)),
                  pl.BlockSpec((CHUNK, SSM_INNER), lambda c: (rev(c), 0)),
                  pl.BlockSpec((CHUNK, XBC_COLS), lambda c: (rev(c), 0)),
                  pl.BlockSpec((CHUNK, XBC_COLS), lambda c: (rev(c), 0)),
                  pl.BlockSpec((1, SSM_STATE, SSM_INNER), lambda c: (rev(c), 0, 0)),
                  pl.BlockSpec((CHUNK, 128), lambda c: (rev(c), DT_OFF // 128)),
                  pl.BlockSpec((CHUNK, SSM_INNER), lambda c: (rev(c), Z_OFF // SSM_INNER)),
                  colv, colv, vec(SSM_INNER), vec(SSM_INNER), _const_spec((SSM_HEADS, SSM_INNER)), any_spec, any_spec],
        out_specs=[pl.BlockSpec((CHUNK, SSM_INNER), lambda c: (rev(c), Z_OFF // SSM_INNER)),
                   pl.BlockSpec((CHUNK, 128), lambda c: (rev(c), DT_OFF // 128)),
                   pl.BlockSpec((CHUNK, XBC_COLS), lambda c: (rev(c), 0)),
                   vec(SSM_INNER), colv, colv, colv],
        out_shape=[jax.ShapeDtypeStruct(dmain.shape, dmain.dtype), jax.ShapeDtypeStruct(dtail.shape, dtail.dtype),
                   jax.ShapeDtypeStruct((T, XBC_COLS), F32), jax.ShapeDtypeStruct((1, SSM_INNER), F32),
                   jax.ShapeDtypeStruct((SSM_HEADS, 1), F32), jax.ShapeDtypeStruct((SSM_HEADS, 1), F32),
                   jax.ShapeDtypeStruct((SSM_HEADS, 1), F32)],
        scratch_shapes=[pltpu.VMEM((SSM_STATE, SSM_INNER), F32)],
        input_output_aliases={12: 0, 13: 1},
        compiler_params=_cp(("arbitrary",)))(dyo, ypre, xbc, dsil, hs, proj_tail, proj_main, dtb_col, alog_col, d_exp,
                                             norm_w, ehead_t, dmain, dtail)


def _rel_bucket(rel):
    n = jnp.maximum(rel, 0)
    max_exact = REL_BUCKETS // 2
    nf = jnp.maximum(n, 1).astype(F32)
    large = max_exact + (jnp.log(nf / max_exact) / math.log(REL_MAX_DIST / max_exact)
                         * (REL_BUCKETS - max_exact)).astype(jnp.int32)
    large = jnp.minimum(large, REL_BUCKETS - 1)
    return jnp.where(n < max_exact, n, large)


def _band_geometry():
    qi = jnp.arange(WINDOW)[:, None] + WINDOW
    kj = jnp.arange(2 * WINDOW)[None, :]
    rel = qi - kj
    return _rel_bucket(rel), (rel >= 0) & (rel < WINDOW)


def _attn_logits(kband, qh, bias_h, first):
    s = _dot(kband, qh, NT) * (HEADDIM ** -0.5) + bias_h
    rowk = lax.broadcasted_iota(jnp.int32, (2 * WINDOW, WINDOW), 0)
    return jnp.where(jnp.logical_and(first, rowk < WINDOW), NEG, s)


def _attn_fwd(proj_main, proj_tail, bias_tbl, sinks):
    T = proj_main.shape[0]
    nb = T // WINDOW

    def body(q_ref, kv_ref, kvp_ref, bias_ref, sink_ref, o_ref, lse_ref):
        i = pl.program_id(0)
        first = i == 0
        outs, lses = [], []
        for kvh in range(ATTN_KV):
            ks = slice(K_OFF + kvh * HEADDIM, K_OFF + (kvh + 1) * HEADDIM)
            vs = slice(V_OFF + kvh * HEADDIM, V_OFF + (kvh + 1) * HEADDIM)
            kband = jnp.concatenate([kvp_ref[:, ks], kv_ref[:, ks]], axis=0).astype(BF16)
            vband = jnp.concatenate([kvp_ref[:, vs], kv_ref[:, vs]], axis=0).astype(BF16)
            heads = range(kvh * ATTN_GROUP, (kvh + 1) * ATTN_GROUP)
            logits = [_attn_logits(kband, q_ref[:, h * HEADDIM:(h + 1) * HEADDIM].astype(BF16), bias_ref[h], first)
                      for h in heads]
            probs = []
            for h, s in zip(heads, logits):
                sink = sink_ref[:, h:h + 1]
                m = jnp.maximum(jnp.max(s, axis=0, keepdims=True), sink)
                p = jnp.exp(s - m)
                den = jnp.sum(p, axis=0, keepdims=True) + jnp.exp(sink - m)
                probs.append((p * (1.0 / den)).astype(BF16))
                lses.append(m + jnp.log(den))
            outs += [_dot(pt, vband, TN) for pt in probs]
        o_ref[...] = jnp.concatenate(outs, axis=1).astype(BF16)
        lse_ref[...] = jnp.concatenate(lses, axis=0)

    return pl.pallas_call(
        body, name="attn_fwd", grid=(nb,),
        in_specs=[pl.BlockSpec((WINDOW, D_MODEL), lambda i: (i, Q_OFF // D_MODEL)),
                  pl.BlockSpec((WINDOW, 256), lambda i: (i, 0)),
                  pl.BlockSpec((WINDOW, 256), lambda i: (jnp.maximum(i - 1, 0), 0)),
                  _const_spec((ATTN_HEADS, 2 * WINDOW, WINDOW)), _const_spec((1, ATTN_HEADS))],
        out_specs=[pl.BlockSpec((WINDOW, D_MODEL), lambda i: (i, 0)),
                   pl.BlockSpec((ATTN_HEADS, WINDOW), lambda i: (0, i))],
        out_shape=[jax.ShapeDtypeStruct((T, D_MODEL), BF16), jax.ShapeDtypeStruct((ATTN_HEADS, T), F32)],
        compiler_params=_cp(("arbitrary",)))(proj_main, proj_tail, proj_tail, bias_tbl, sinks)


def _attn_bwd(dy, lse, proj_main, proj_tail, bias_tbl, sinks, dmain):
    T = proj_main.shape[0]
    nb = T // WINDOW

    def body(dy_ref, lse_ref, q_ref, kv_ref, kvp_ref, bias_ref, sink_ref, dmain_in,
             dq_ref, dkv_ref, dbias_ref, dsink_ref, carry):
        del dmain_in
        i = pl.program_id(0)
        first = i == 0

        @pl.when(first)
        def _():
            carry[...] = jnp.zeros_like(carry)
            dbias_ref[...] = jnp.zeros_like(dbias_ref)
            dsink_ref[...] = jnp.zeros_like(dsink_ref)

        @pl.when(i < nb)
        def _():
            scale = HEADDIM ** -0.5
            dqs, dsinks, dks, dvs = [], [], [], []
            for kvh in range(ATTN_KV):
                ks = slice(K_OFF + kvh * HEADDIM, K_OFF + (kvh + 1) * HEADDIM)
                vs = slice(V_OFF + kvh * HEADDIM, V_OFF + (kvh + 1) * HEADDIM)
                kband = jnp.concatenate([kvp_ref[:, ks], kv_ref[:, ks]], axis=0).astype(BF16)
                vband = jnp.concatenate([kvp_ref[:, vs], kv_ref[:, vs]], axis=0).astype(BF16)
                heads = range(kvh * ATTN_GROUP, (kvh + 1) * ATTN_GROUP)
                qs = [q_ref[:, h * HEADDIM:(h + 1) * HEADDIM].astype(BF16) for h in heads]
                dos = [dy_ref[:, h * HEADDIM:(h + 1) * HEADDIM] for h in heads]
                logits = [_attn_logits(kband, qh, bias_ref[h], first) for h, qh in zip(heads, qs)]
                dps = [_dot(vband, do, NT) for do in dos]
                pbs, dsbs = [], []
                for h, s, dp in zip(heads, logits, dps):
                    lse_h = lse_ref[h:h + 1, :]
                    p = jnp.exp(s - lse_h)
                    delta = jnp.sum(p * dp, axis=0, keepdims=True)
                    ds = p * (dp - delta)
                    psink = jnp.exp(sink_ref[:, h:h + 1] - lse_h)
                    dsinks.append(-jnp.sum(psink * delta, axis=1, keepdims=True))
                    dbias_ref[h] += ds
                    pbs.append(p.astype(BF16))
                    dsbs.append((ds * scale).astype(BF16))
                dqs += [_dot(dsb, kband, TN) for dsb in dsbs]
                dks.append(_dot(jnp.concatenate(dsbs, axis=1), jnp.concatenate(qs, axis=0)))
                dvs.append(_dot(jnp.concatenate(pbs, axis=1), jnp.concatenate(dos, axis=0)))
            dq_ref[...] = jnp.concatenate(dqs, axis=1).astype(dq_ref.dtype)
            dsink_ref[...] += jnp.concatenate(dsinks, axis=1)
            dkv = jnp.concatenate(dks + dvs, axis=1)
            dkv_ref[...] = (carry[...] + dkv[0:WINDOW, :]).astype(dkv_ref.dtype)
            carry[...] = dkv[WINDOW:, :]

        @pl.when(i == nb)
        def _():
            dkv_ref[...] = carry[...].astype(dkv_ref.dtype)

    cur = lambda i: jnp.minimum(i, nb - 1)
    return pl.pallas_call(
        body, name="attn_bwd", grid=(nb + 1,),
        in_specs=[pl.BlockSpec((WINDOW, D_MODEL), lambda i: (cur(i), 0)),
                  pl.BlockSpec((ATTN_HEADS, WINDOW), lambda i: (0, cur(i))),
                  pl.BlockSpec((WINDOW, D_MODEL), lambda i: (cur(i), Q_OFF // D_MODEL)),
                  pl.BlockSpec((WINDOW, 256), lambda i: (cur(i), 0)),
                  pl.BlockSpec((WINDOW, 256), lambda i: (jnp.maximum(cur(i) - 1, 0), 0)),
                  _const_spec((ATTN_HEADS, 2 * WINDOW, WINDOW)), _const_spec((1, ATTN_HEADS)),
                  pl.BlockSpec(memory_space=pl.ANY)],
        out_specs=[pl.BlockSpec((WINDOW, D_MODEL), lambda i: (cur(i), Q_OFF // D_MODEL)),
                   pl.BlockSpec((WINDOW, 256), lambda i: (jnp.maximum(i - 1, 0), 0)),
                   _const_spec((ATTN_HEADS, 2 * WINDOW, WINDOW)), _const_spec((1, ATTN_HEADS))],
        out_shape=[jax.ShapeDtypeStruct(dmain.shape, dmain.dtype), jax.ShapeDtypeStruct((T, TAIL_COLS), BF16),
                   jax.ShapeDtypeStruct((ATTN_HEADS, 2 * WINDOW, WINDOW), F32),
                   jax.ShapeDtypeStruct((1, ATTN_HEADS), F32)],
        scratch_shapes=[pltpu.VMEM((WINDOW, 256), F32)],
        input_output_aliases={7: 0},
        compiler_params=_cp(("arbitrary",)))(dy, lse, proj_main, proj_tail, proj_tail, bias_tbl, sinks, dmain)


def _bias_table(rel_bias_t, onehot_t, mask):
    def body(rb_ref, oh_ref, m_ref, o_ref):
        o_ref[...] = _dot3(rb_ref[...], oh_ref[...]) + m_ref[...]

    flat = pl.pallas_call(body, name="bias_table",
                          out_shape=jax.ShapeDtypeStruct((ATTN_HEADS, 2 * WINDOW * WINDOW), F32))(rel_bias_t, onehot_t, mask)
    return flat.reshape(ATTN_HEADS, 2 * WINDOW, WINDOW)


def _rel_bias_grad(dbias, onehot):
    def body(d_ref, oh_ref, o_ref):
        o_ref[...] = _dot(d_ref[...], oh_ref[...], NN, HIGHEST)

    return pl.pallas_call(body, name="rel_bias_grad",
                          out_shape=jax.ShapeDtypeStruct((ATTN_HEADS, REL_BUCKETS), F32))(dbias, onehot)


def _ln_fwd(r, g, b):
    mu = jnp.mean(r, axis=1, keepdims=True)
    xc = r - mu
    rstd = lax.rsqrt(jnp.mean(xc * xc, axis=1, keepdims=True) + LN_EPS)
    xhat = xc * rstd
    return xhat * g + b, xhat, rstd


def _ln_bwd(dy, xhat, rstd, g):
    dxh = dy * g
    return rstd * (dxh - jnp.mean(dxh, axis=1, keepdims=True) - xhat * jnp.mean(dxh * xhat, axis=1, keepdims=True))


def _merge_fwd(y_ssm, y_attn, proj_main, b_gate, w_bs, w_ba, tm=512):
    T = y_ssm.shape[0]

    def body(ys_ref, ya_ref, gs_ref, ga_ref, bg_ref, wbs_ref, wba_ref, m_ref, bs_ref, ba_ref):
        bs = _dot(ys_ref[...], wbs_ref[...])
        ba = _dot(ya_ref[...], wba_ref[...])
        g_s = _sigmoid(gs_ref[...] + bg_ref[:, 0:D_MODEL])
        g_a = _sigmoid(ga_ref[...] + bg_ref[:, D_MODEL:])
        m_ref[...] = (g_s * bs + g_a * ba).astype(BF16)
        bs_ref[...] = bs
        ba_ref[...] = ba

    row = lambda w, off=0: pl.BlockSpec((tm, w), lambda i: (i, off))
    return pl.pallas_call(
        body, name="merge_fwd", grid=(T // tm,),
        in_specs=[row(SSM_INNER), row(D_MODEL), row(D_MODEL, GATE_OFF // D_MODEL), row(D_MODEL, GATE_OFF // D_MODEL + 1),
                  _const_spec((1, 2 * D_MODEL)), _const_spec((SSM_INNER, D_MODEL)), _const_spec((D_MODEL, D_MODEL))],
        out_specs=[row(D_MODEL), row(D_MODEL), row(D_MODEL)],
        out_shape=[jax.ShapeDtypeStruct((T, D_MODEL), BF16), jax.ShapeDtypeStruct((T, D_MODEL), F32),
                   jax.ShapeDtypeStruct((T, D_MODEL), F32)],
        compiler_params=_cp(("parallel",)))(y_ssm, y_attn, proj_main, proj_main, b_gate, w_bs, w_ba)


def _mix_ln1(merged, w_mo, x, g1, b1, tm=1024):
    T = x.shape[0]
    tm = min(tm, T)

    def body(m_ref, w_ref, x_ref, g_ref, b_ref, r_ref, h_ref, hb_ref):
        r = ALPHA * x_ref[...] + _dot(m_ref[...], w_ref[...])
        r_ref[...] = r
        h = _ln_fwd(r, g_ref[...], b_ref[...])[0]
        h_ref[...] = h
        hb_ref[...] = h.astype(BF16)

    row = pl.BlockSpec((tm, D_MODEL), lambda i: (i, 0))
    return pl.pallas_call(
        body, name="mix_ln1", grid=(T // tm,),
        in_specs=[row, _const_spec((D_MODEL, D_MODEL)), row, _const_spec((1, D_MODEL)), _const_spec((1, D_MODEL))],
        out_specs=[row, row, row],
        out_shape=[jax.ShapeDtypeStruct((T, D_MODEL), F32), jax.ShapeDtypeStruct((T, D_MODEL), F32),
                   jax.ShapeDtypeStruct((T, D_MODEL), BF16)],
        compiler_params=_cp(("parallel",)))(merged, w_mo, x, g1, b1)


def _ffn_conv_glu(u_pre, w, b, tr=2048, tc=256):
    T = u_pre.shape[0]
    tr = min(tr, T)
    K = FFN_CONV
    nj = D_FF // tc
    hb = tr // HALO
    assert T % tr == 0 and D_FF % tc == 0

    def body(xg_ref, xgp_ref, xv_ref, xvp_ref, wg_ref, wv_ref, bg_ref, bv_ref, u_ref, a_ref, head_g, head_v):
        i = pl.program_id(1)
        halves = []
        for x_ref, xp_ref, w_ref, b_ref, head in ((xg_ref, xgp_ref, wg_ref, bg_ref, head_g),
                                                  (xv_ref, xvp_ref, wv_ref, bv_ref, head_v)):
            head[0:HALO, :] = jnp.where(i > 0, xp_ref[...], 0.0)
            head[HALO:, :] = x_ref[0:HALO, :]
            halves.append((x_ref, head, _taps(w_ref, K, tc), jnp.broadcast_to(b_ref[...], (HALO, tc))))

        def conv(half, r):
            x_ref, head, wk, bias = halves[half]
            lo = r * HALO
            acc = bias + wk[K - 1] * x_ref[lo:lo + HALO, :]
            for k in range(K - 1):
                s = K - 1 - k
                acc = acc + wk[k] * (head[HALO - s:2 * HALO - s, :] if r == 0 else x_ref[lo - s:lo + HALO - s, :])
            return acc

        for r2 in range(tr // (2 * HALO)):
            acts = []
            for r in (2 * r2, 2 * r2 + 1):
                lo = r * HALO
                ug, uv = conv(0, r), conv(1, r)
                u_ref[0, lo:lo + HALO, :] = ug
                u_ref[1, lo:lo + HALO, :] = uv
                acts.append(ug * _sigmoid(ug) * uv)
            a_ref[2 * r2 * HALO:(2 * r2 + 2) * HALO, :] = jnp.concatenate(acts, axis=0).astype(BF16)

    tile = lambda off: pl.BlockSpec((tr, tc), lambda j, i: (i, off + j))
    prev = lambda off: pl.BlockSpec((HALO, tc), lambda j, i: (jnp.maximum(i * hb - 1, 0), off + j))
    row = lambda rows, off: pl.BlockSpec((rows, tc), lambda j, i: (0, off + j))
    return pl.pallas_call(
        body, name="ffn_conv_glu", grid=(nj, T // tr),
        in_specs=[tile(0), prev(0), tile(nj), prev(nj), row(K, 0), row(K, nj), row(1, 0), row(1, nj)],
        out_specs=[pl.BlockSpec((2, tr, tc), lambda j, i: (0, i, j)), pl.BlockSpec((tr, tc), lambda j, i: (i, j))],
        out_shape=[jax.ShapeDtypeStruct((2, T, D_FF), F32), jax.ShapeDtypeStruct((T, D_FF), BF16)],
        scratch_shapes=[pltpu.VMEM((2 * HALO, tc), F32), pltpu.VMEM((2 * HALO, tc), F32)],
        compiler_params=_cp(("parallel", "arbitrary")))(u_pre, u_pre, u_pre, u_pre, w, w, b, b)


def _down_ln2_loss(act, w_down, h1, target, g2, b2, tm=512):
    T = h1.shape[0]

    def body(a_ref, w_ref, h_ref, t_ref, g_ref, b_ref, dr_ref, drb_ref, dg_ref, db_ref, l_ref):
        @pl.when(pl.program_id(0) == 0)
        def _():
            dg_ref[...] = jnp.zeros_like(dg_ref)
            db_ref[...] = jnp.zeros_like(db_ref)
            l_ref[...] = jnp.zeros_like(l_ref)

        r = ALPHA * h_ref[...] + _dot(a_ref[...], w_ref[...])
        y, xhat, rstd = _ln_fwd(r, g_ref[...], b_ref[...])
        err = y - t_ref[...]
        l_ref[...] += jnp.sum(err * err, keepdims=True)
        dy = err * (1.0 / D_MODEL)
        dg_ref[...] += jnp.sum(dy * xhat, axis=0, keepdims=True)
        db_ref[...] += jnp.sum(dy, axis=0, keepdims=True)
        dr = _ln_bwd(dy, xhat, rstd, g_ref[...])
        dr_ref[...] = dr
        drb_ref[...] = dr.astype(BF16)

    row = pl.BlockSpec((tm, D_MODEL), lambda i: (i, 0))
    vec = _const_spec((1, D_MODEL))
    return pl.pallas_call(
        body, name="down_ln2_loss", grid=(T // tm,),
        in_specs=[pl.BlockSpec((tm, D_FF), lambda i: (i, 0)), _const_spec((D_FF, D_MODEL)), row, row, vec, vec],
        out_specs=[row, row, vec, vec, _const_spec((1, 1))],
        out_shape=[jax.ShapeDtypeStruct((T, D_MODEL), F32), jax.ShapeDtypeStruct((T, D_MODEL), BF16),
                   jax.ShapeDtypeStruct((1, D_MODEL), F32), jax.ShapeDtypeStruct((1, D_MODEL), F32),
                   jax.ShapeDtypeStruct((1, 1), F32)],
        compiler_params=_cp(("arbitrary",)))(act, w_down, h1, target, g2, b2)


def _ffn_gate_conv_bwd(dact, u, u_pre, w, tr=2048, tc=256):
    T = dact.shape[0]
    tr = min(tr, T)
    K = FFN_CONV
    nj, nt, n, hb = D_FF // tc, T // tr, tr // HALO, tr // HALO
    last_hblock = T // HALO - 1
    assert T % tr == 0 and D_FF % tc == 0 and n % 2 == 0

    def body(da_ref, dan_ref, u_ref, un_ref, xg_ref, xv_ref, wg_ref, wv_ref,
             o_ref, dwg_ref, dwv_ref, dbg_ref, dbv_ref, gext_g, gext_v):
        i = pl.program_id(1)
        for r in range(n + 1):
            rows = slice(r * HALO, (r + 1) * HALO)
            if r < n:
                da, g, v = da_ref[rows, :], u_ref[0, rows, :], u_ref[1, rows, :]
            else:
                da, g, v = jnp.where(i < nt - 1, dan_ref[...], 0.0), un_ref[0], un_ref[1]
            sg = _sigmoid(g)
            gext_g[rows, :] = da * v * (sg * (1.0 + g * (1.0 - sg)))
            gext_v[rows, :] = da * (g * sg)
        for half, (gext, x_ref, w_ref, dw_ref, db_ref) in enumerate(((gext_g, xg_ref, wg_ref, dwg_ref, dbg_ref),
                                                                      (gext_v, xv_ref, wv_ref, dwv_ref, dbv_ref))):
            wk = _taps(w_ref, K, tc)
            acc_w = [jnp.zeros((HALO, tc), F32) for _ in range(K)]
            acc_b = jnp.zeros((HALO, tc), F32)
            for r2 in range(n // 2):
                pair = []
                for r in (2 * r2, 2 * r2 + 1):
                    lo = r * HALO
                    x = x_ref[lo:lo + HALO, :]
                    dpre = None
                    for s in range(K):
                        gs = gext[lo + s:lo + HALO + s, :]
                        dpre = wk[K - 1 - s] * gs if dpre is None else dpre + wk[K - 1 - s] * gs
                        acc_w[K - 1 - s] = acc_w[K - 1 - s] + gs * x
                        if s == 0:
                            acc_b = acc_b + gs
                    pair.append(dpre)
                o_ref[half, 2 * r2 * HALO:(2 * r2 + 2) * HALO, :] = jnp.concatenate(pair, axis=0).astype(o_ref.dtype)

            @pl.when(i == 0)
            def _():
                dw_ref[...] = jnp.zeros_like(dw_ref)
                db_ref[...] = jnp.zeros_like(db_ref)

            db_ref[...] += jnp.sum(acc_b, axis=0, keepdims=True)
            dw_ref[...] += jnp.concatenate([jnp.sum(a, axis=0, keepdims=True) for a in acc_w], axis=0)

    nxt = lambda i: jnp.minimum((i + 1) * hb, last_hblock)
    taps = lambda off: pl.BlockSpec((K, tc), lambda j, i: (0, off + j))
    dw_spec, db_spec = pl.BlockSpec((K, tc), lambda j, i: (0, j)), pl.BlockSpec((1, tc), lambda j, i: (0, j))
    du_pre, dwg, dwv, dbg, dbv = pl.pallas_call(
        body, name="ffn_gate_conv_bwd", grid=(nj, nt),
        in_specs=[pl.BlockSpec((tr, tc), lambda j, i: (i, j)), pl.BlockSpec((HALO, tc), lambda j, i: (nxt(i), j)),
                  pl.BlockSpec((2, tr, tc), lambda j, i: (0, i, j)), pl.BlockSpec((2, HALO, tc), lambda j, i: (0, nxt(i), j)),
                  pl.BlockSpec((tr, tc), lambda j, i: (i, j)), pl.BlockSpec((tr, tc), lambda j, i: (i, nj + j)),
                  taps(0), taps(nj)],
        out_specs=[pl.BlockSpec((2, tr, tc), lambda j, i: (0, i, j)), dw_spec, dw_spec, db_spec, db_spec],
        out_shape=[jax.ShapeDtypeStruct((2, T, D_FF), BF16), jax.ShapeDtypeStruct((K, D_FF), F32),
                   jax.ShapeDtypeStruct((K, D_FF), F32), jax.ShapeDtypeStruct((1, D_FF), F32),
                   jax.ShapeDtypeStruct((1, D_FF), F32)],
        scratch_shapes=[pltpu.VMEM((tr + HALO, tc), F32), pltpu.VMEM((tr + HALO, tc), F32)],
        compiler_params=_cp(("parallel", "arbitrary")))(dact, dact, u, u, u_pre, u_pre, w, w)
    return du_pre, jnp.concatenate([dwg, dwv], axis=1), jnp.concatenate([dbg, dbv], axis=1)


def _ffn_bwd_in(du_pre, w_up, dr2, r1, g1, b1, tm=1024, tk=1408):
    T = dr2.shape[0]
    tm = min(tm, T)
    assert T % tm == 0 and D_FF % tk == 0
    nk = 2 * D_FF // tk
    kh = D_FF // tk

    def body(d_ref, w_ref, dr2_ref, r_ref, g_ref, b_ref, dr1_ref, dr1b_ref, dg_ref, db_ref, acc):
        i, k = pl.program_id(0), pl.program_id(1)

        @pl.when(jnp.logical_and(i == 0, k == 0))
        def _():
            dg_ref[...] = jnp.zeros_like(dg_ref)
            db_ref[...] = jnp.zeros_like(db_ref)

        @pl.when(k == 0)
        def _():
            acc[...] = ALPHA * dr2_ref[...]

        acc[...] += _dot(d_ref[...], w_ref[...], NT)

        @pl.when(k == nk - 1)
        def _():
            _, xhat, rstd = _ln_fwd(r_ref[...], g_ref[...], b_ref[...])
            dy = acc[...]
            dg_ref[...] += jnp.sum(dy * xhat, axis=0, keepdims=True)
            db_ref[...] += jnp.sum(dy, axis=0, keepdims=True)
            dr1 = _ln_bwd(dy, xhat, rstd, g_ref[...])
            dr1_ref[...] = dr1
            dr1b_ref[...] = dr1.astype(BF16)

    row = pl.BlockSpec((tm, D_MODEL), lambda i, k: (i, 0))
    vec = _const_spec((1, D_MODEL))
    return pl.pallas_call(
        body, name="ffn_bwd_in", grid=(T // tm, nk),
        in_specs=[pl.BlockSpec((None, tm, tk), lambda i, k: (k // kh, i, k % kh)),
                  pl.BlockSpec((D_MODEL, tk), lambda i, k: (0, k)), row, row, vec, vec],
        out_specs=[row, row, vec, vec],
        out_shape=[jax.ShapeDtypeStruct((T, D_MODEL), F32), jax.ShapeDtypeStruct((T, D_MODEL), BF16),
                   jax.ShapeDtypeStruct((1, D_MODEL), F32), jax.ShapeDtypeStruct((1, D_MODEL), F32)],
        scratch_shapes=[pltpu.VMEM((tm, D_MODEL), F32)],
        compiler_params=_cp(("arbitrary", "arbitrary")))(du_pre, w_up, dr2, r1, g1, b1)


def _mix_bwd(dr1, w_mo, w_bs, w_ba, bs, ba, proj_main, b_gate, tm=512):
    T = dr1.shape[0]

    def body(d_ref, wmo_ref, wbs_ref, wba_ref, bs_ref, ba_ref, gs_ref, ga_ref, bg_ref,
             dg_ref, dbs_ref, dba_ref, dys_ref, dya_ref, dbg_ref):
        @pl.when(pl.program_id(0) == 0)
        def _():
            dbg_ref[...] = jnp.zeros_like(dbg_ref)

        dm = _dot(d_ref[...].astype(BF16), wmo_ref[...], NT)
        g_s = _sigmoid(gs_ref[...] + bg_ref[:, 0:D_MODEL])
        g_a = _sigmoid(ga_ref[...] + bg_ref[:, D_MODEL:])
        dgs = dm * bs_ref[...] * g_s * (1.0 - g_s)
        dga = dm * ba_ref[...] * g_a * (1.0 - g_a)
        dg_ref[:, 0:D_MODEL] = dgs.astype(BF16)
        dg_ref[:, D_MODEL:] = dga.astype(BF16)
        dbg_ref[:, 0:D_MODEL] += jnp.sum(dgs, axis=0, keepdims=True)
        dbg_ref[:, D_MODEL:] += jnp.sum(dga, axis=0, keepdims=True)
        dbs = (dm * g_s).astype(BF16)
        dba = (dm * g_a).astype(BF16)
        dbs_ref[...] = dbs
        dba_ref[...] = dba
        dys_ref[...] = _dot(dbs, wbs_ref[...], NT)
        dya_ref[...] = _dot(dba, wba_ref[...], NT).astype(BF16)

    row = lambda w, off=0: pl.BlockSpec((tm, w), lambda i: (i, off))
    return pl.pallas_call(
        body, name="mix_bwd", grid=(T // tm,),
        in_specs=[row(D_MODEL), _const_spec((D_MODEL, D_MODEL)), _const_spec((SSM_INNER, D_MODEL)),
                  _const_spec((D_MODEL, D_MODEL)), row(D_MODEL), row(D_MODEL),
                  row(D_MODEL, GATE_OFF // D_MODEL), row(D_MODEL, GATE_OFF // D_MODEL + 1), _const_spec((1, 2 * D_MODEL))],
        out_specs=[row(2 * D_MODEL, GATE_OFF // (2 * D_MODEL)), row(D_MODEL), row(D_MODEL), row(SSM_INNER), row(D_MODEL),
                   _const_spec((1, 2 * D_MODEL))],
        out_shape=[jax.ShapeDtypeStruct((T, MAIN_COLS), BF16), jax.ShapeDtypeStruct((T, D_MODEL), BF16),
                   jax.ShapeDtypeStruct((T, D_MODEL), BF16), jax.ShapeDtypeStruct((T, SSM_INNER), F32),
                   jax.ShapeDtypeStruct((T, D_MODEL), BF16), jax.ShapeDtypeStruct((1, 2 * D_MODEL), F32)],
        compiler_params=_cp(("arbitrary",)))(dr1, w_mo, w_bs, w_ba, bs, ba, proj_main, proj_main, b_gate)


def _local_step(x, target, w, p, late_weights=None, early_grads=None):
    xb = x.astype(BF16)
    if late_weights is None:
        proj_main = _matmul(xb, w["in_main"], "nn", F32, "in_proj_main", tm=1024, tn=2048)
    else:
        proj_main, *landed = _matmul(xb, w["in_main"], "nn", F32, "in_proj_main", tm=1024, tn=2048,
                                     comm=("gather", late_weights[0]))
        w = {**w, **late_weights[1](landed)}
    proj_tail = _matmul(xb, w["in_tail"], "nn", F32, "in_proj_tail", tm=2048, tn=TAIL_COLS)
    xbc, dsil = _conv_silu_fwd(proj_main, XBC_OFF, XBC_COLS, p["ssm_conv_w"], p["ssm_conv_b"], SSM_CONV, "ssm_conv_fwd",
                               tr=2048)
    y_ssm, ypre, hs = _ssd_fwd(xbc, proj_main, proj_tail, p["dtb_col"], p["alog_col"], p["d_exp"], p["ssm_norm_w"])
    y_attn, lse = _attn_fwd(proj_main, proj_tail, p["bias_tbl"], p["attn_sinks"])
    merged, bs, ba = _merge_fwd(y_ssm, y_attn, proj_main, p["b_gate"], w["bs"], w["ba"])
    r1, h1, h1b = _mix_ln1(merged, w["mo"], x, p["ln1_g"], p["ln1_b"])
    u_pre = _matmul(h1b, w["up"], "nn", F32, "ffn_up", tm=1024, tn=2816)
    u, act = _ffn_conv_glu(u_pre, p["ffn_conv_w"], p["ffn_conv_b"])
    dr2, dr2b, dg2, db2, sq = _down_ln2_loss(act, w["down"], h1, target, p["ln2_g"], p["ln2_b"])
    g = {"ln2_g": dg2, "ln2_b": db2}
    g["w_down"] = _matmul(act, dr2b, "tn", BF16, "dw_down", tm=1408, tn=1024, tk=2048)
    dact = _matmul(dr2b, w["down"], "nt", F32, "ffn_dact", tm=1024, tn=2816, tk=1024)
    du_pre, g["ffn_conv_w"], g["ffn_conv_b"] = _ffn_gate_conv_bwd(dact, u, u_pre, p["ffn_conv_w"])
    g["w_up"] = _matmul(h1b, du_pre, "tn", BF16, "dw_up", tm=1024, tn=1408, tk=2048)
    dr1, dr1b, g["ln1_g"], g["ln1_b"] = _ffn_bwd_in(du_pre, w["up"], dr2, r1, p["ln1_g"], p["ln1_b"])
    g["w_mix_out"] = _matmul(merged, dr1b, "tn", BF16, "dw_mix_out", tm=1024, tn=1024, tk=2048)
    dmain, dbs, dba, dy_ssm, dy_attn, g["b_gate"] = _mix_bwd(dr1b, w["mo"], w["bs"], w["ba"], bs, ba, proj_main, p["b_gate"])
    g["w_branch_ssm"] = _matmul(y_ssm, dbs, "tn", BF16, "dw_branch_ssm", tm=1024, tn=1024, tk=2048)
    g["w_branch_attn"] = _matmul(y_attn, dba, "tn", BF16, "dw_branch_attn", tm=1024, tn=1024, tk=2048)
    dmain, dtail, dbias, g["attn_sinks"] = _attn_bwd(dy_attn, lse, proj_main, proj_tail, p["bias_tbl"], p["attn_sinks"], dmain)
    g["rel_bias"] = _rel_bias_grad(dbias.reshape(ATTN_HEADS, WINDOW * 2 * WINDOW), p["bucket_onehot"]).T
    dmain, dtail, dco, g["ssm_norm_w"], dd, dalog, ddtb = _ssd_bwd(
        dy_ssm, ypre, xbc, dsil, hs, proj_main, proj_tail, p["dtb_col"], p["alog_col"], p["d_exp"], p["ssm_norm_w"],
        p["ehead_t"], dmain, dtail)
    g["ssm_d"], g["ssm_a_log"], g["ssm_dt_bias"] = (a.reshape(1, SSM_HEADS) for a in (dd, dalog, ddtb))
    dmain, g["ssm_conv_w"], g["ssm_conv_b"] = _conv_bwd(
        dco, proj_main, XBC_OFF, XBC_COLS, p["ssm_conv_w"], SSM_CONV, dmain, XBC_OFF, "ssm_conv_bwd",
        tr=2048)
    g["in_tail"] = _matmul(xb, dtail, "tn", BF16, "dw_in_tail", tm=1024, tn=TAIL_COLS, tk=2048)
    landed = []
    if early_grads is None:
        g["in_main"] = _matmul(xb, dmain, "tn", BF16, "dw_in_main", tm=1024, tn=1024, tk=2048)
    else:
        g["in_main"], *landed = _matmul(xb, dmain, "tn", BF16, "dw_in_main", tm=1024, tn=1024, tk=2048,
                                        comm=("exchange", early_grads(g)))
    return sq, (dmain, dtail, dr1), w, g, landed


def _grad_x(dproj, w, exchange=None):
    dmain, dtail, dr1 = dproj
    landed = None
    if exchange is None:
        dx = _matmul(dmain, w["in_main"], "nt", F32, "dx_main", tm=1024, tk=2048, addend=dr1, addend_scale=ALPHA)
    else:
        dx, landed = _matmul(dmain, w["in_main"], "nt", F32, "dx_main", tm=1024, tk=2048, addend=dr1,
                             addend_scale=ALPHA, comm=("exchange", exchange))
    dx = _matmul(dtail, w["in_tail"], "nt", F32, "dx_tail", tm=2048, tk=TAIL_COLS, addend=dx)
    return dx if exchange is None else (dx, landed)


SHARD_COLS = IN_COLS // N_DEV
W_IN_SEGMENTS = ((O_Z, 2048, "main", Z_OFF), (O_XBC, XBC_COLS, "main", XBC_OFF), (O_DT, SSM_HEADS, "tail", DT_OFF),
                 (O_Q, D_MODEL, "main", Q_OFF), (O_K, 128, "tail", K_OFF), (O_V, 128, "tail", V_OFF),
                 (O_GATE, 2 * D_MODEL, "main", GATE_OFF))


def _w_in_from_shards(shards):
    def seg(off, n):
        pieces = []
        for j in range(off // SHARD_COLS, (off + n - 1) // SHARD_COLS + 1):
            lo, hi = max(off, j * SHARD_COLS), min(off + n, (j + 1) * SHARD_COLS)
            pieces.append(shards[j, :, lo - j * SHARD_COLS:hi - j * SHARD_COLS])
        return pieces

    by_name = {(where, koff): seg(off, n) for off, n, where, koff in W_IN_SEGMENTS}
    main = jnp.concatenate(by_name["main", Z_OFF] + by_name["main", XBC_OFF] + by_name["main", Q_OFF]
                           + by_name["main", GATE_OFF], axis=1)
    tail = jnp.concatenate(by_name["tail", K_OFF] + by_name["tail", V_OFF] + by_name["tail", DT_OFF]
                           + [jnp.zeros((shards.shape[1], 128 - SSM_HEADS), shards.dtype)], axis=1)
    return main, tail


def _w_in_grad_by_device(g_main, g_tail):
    slots = []
    for j in range(N_DEV):
        a, b = j * SHARD_COLS, (j + 1) * SHARD_COLS
        pieces = []
        for off, n, where, koff in W_IN_SEGMENTS:
            lo, hi = max(a, off), min(b, off + n)
            if lo < hi:
                pieces.append((g_main if where == "main" else g_tail)[:, koff + lo - off:koff + hi - off])
        slots.append(jnp.concatenate(pieces, axis=1))
    return jnp.stack(slots)


def _prep_params(rel_bias, b_gate, ssm_conv_w, ssm_conv_b, ssm_dt_bias, ssm_a_log, ssm_d, ssm_norm_w, attn_sinks,
                 ln1_g, ln1_b, ffn_conv_w, ffn_conv_b, ln2_g, ln2_b):
    bucket, in_window = _band_geometry()
    bucket, in_window = bucket.T, in_window.T
    onehot = jnp.logical_and(bucket.reshape(-1, 1) == jnp.arange(REL_BUCKETS)[None, :],
                             in_window.reshape(-1, 1)).astype(F32)
    onehot_t = jnp.logical_and(bucket.reshape(1, -1) == jnp.arange(REL_BUCKETS)[:, None],
                               in_window.reshape(1, -1)).astype(BF16)
    bias_tbl = _bias_table(rel_bias.T, onehot_t, jnp.where(in_window.reshape(1, -1), 0.0, NEG))
    ehead_t = (jnp.arange(SSM_INNER)[None, :] // SSMD == jnp.arange(SSM_HEADS)[:, None]).astype(F32)
    return {"bias_tbl": bias_tbl, "bucket_onehot": onehot, "b_gate": b_gate, "ssm_conv_w": ssm_conv_w,
            "ssm_conv_b": ssm_conv_b, "dtb_col": ssm_dt_bias.reshape(SSM_HEADS, 1),
            "alog_col": ssm_a_log.reshape(SSM_HEADS, 1), "ehead_t": ehead_t,
            "d_exp": jnp.repeat(ssm_d, SSMD, axis=1), "ssm_norm_w": ssm_norm_w, "attn_sinks": attn_sinks,
            "ln1_g": ln1_g, "ln1_b": ln1_b, "ffn_conv_w": ffn_conv_w, "ffn_conv_b": ffn_conv_b,
            "ln2_g": ln2_g, "ln2_b": ln2_b}


def _all_gather(shards, name):
    nb = len(shards)

    def body(*refs):
        for phase in _gather_phases(refs[:nb], refs[nb:2 * nb], *refs[2 * nb:]):
            phase()

    any_spec = pl.BlockSpec(memory_space=pl.ANY)
    return pl.pallas_call(
        body, name=name, out_shape=[jax.ShapeDtypeStruct((N_DEV,) + s.shape, s.dtype) for s in shards],
        in_specs=[any_spec] * nb, out_specs=[any_spec] * nb, scratch_shapes=_comm_sems(nb))(*shards)


def _adamw_math(w, g, m, v):
    m = ADAM_B1 * m + (1.0 - ADAM_B1) * g
    v = ADAM_B2 * v + (1.0 - ADAM_B2) * (g * g)
    m_hat = m / (1.0 - ADAM_B1 ** ADAM_STEP)
    v_hat = v / (1.0 - ADAM_B2 ** ADAM_STEP)
    return -ADAM_LR * (m_hat / (jnp.sqrt(v_hat) + ADAM_EPS) + ADAM_WD * w), m, v


def _slot_total(s_ref):
    g = s_ref[0].astype(F32)
    for i in range(1, N_DEV):
        g = g + s_ref[i].astype(F32)
    return g


def _adamw(landed, w, m, v, name):
    R, C = w.shape
    tr = 256 if R % 256 == 0 and R > 256 else R

    def body(s_ref, w_ref, m_ref, v_ref, g_ref, d_ref, nm_ref, nv_ref):
        g = _slot_total(s_ref)
        g_ref[...] = g
        d_ref[...], nm_ref[...], nv_ref[...] = _adamw_math(w_ref[...], g, m_ref[...], v_ref[...])

    spec = pl.BlockSpec((tr, C), lambda i: (i, 0))
    return pl.pallas_call(
        body, name=name, grid=(R // tr,), in_specs=[pl.BlockSpec((N_DEV, tr, C), lambda i: (0, i, 0))] + [spec] * 3,
        out_specs=[spec] * 4, out_shape=[jax.ShapeDtypeStruct((R, C), F32)] * 4,
        compiler_params=_cp(("parallel",)))(landed, w, m, v)


def _small_update(landed, ws, ms, vs):
    k = len(ws)

    def body(*refs):
        s_ref, w_refs, m_refs, v_refs = refs[0], refs[1:1 + k], refs[1 + k:1 + 2 * k], refs[1 + 2 * k:1 + 3 * k]
        outs = refs[1 + 3 * k:]
        g_all = _slot_total(s_ref)
        for i in range(k):
            n = w_refs[i].shape[1]
            g = g_all[i:i + 1, 0:n]
            outs[i][...] = g
            outs[k + i][...], outs[2 * k + i][...], outs[3 * k + i][...] = _adamw_math(
                w_refs[i][...], g, m_refs[i][...], v_refs[i][...])

    return pl.pallas_call(body, name="small_update",
                          out_shape=[jax.ShapeDtypeStruct(w.shape, F32) for w in ws] * 4)(landed, *ws, *ms, *vs)


SHARDED = {"w_in": "cols", "w_branch_ssm": "rows", "w_branch_attn": "rows", "w_mix_out": "rows", "w_up": "cols",
           "w_down": "rows", "ssm_conv_w": "cols", "ffn_conv_w": "cols"}
LATE = ("w_branch_ssm", "w_branch_attn", "w_mix_out", "w_up", "w_down")
SHORT = {"w_branch_ssm": "bs", "w_branch_attn": "ba", "w_mix_out": "mo", "w_up": "up", "w_down": "down"}
SMALL = ("rel_bias", "b_gate", "ssm_conv_b", "ssm_dt_bias", "ssm_a_log", "ssm_d", "ssm_norm_w", "attn_sinks",
         "ln1_g", "ln1_b", "ffn_conv_b", "ln2_g", "ln2_b")
WEIGHTS = ("rel_bias", "w_in", "b_gate", "ssm_conv_w", "ssm_conv_b", "ssm_dt_bias", "ssm_a_log", "ssm_d", "ssm_norm_w",
           "attn_sinks", "w_branch_ssm", "w_branch_attn", "w_mix_out", "ln1_g", "ln1_b", "w_up", "ffn_conv_w",
           "ffn_conv_b", "w_down", "ln2_g", "ln2_b")
SMALL_ROWS, SMALL_COLS = 16, 2 * D_FF


def _by_device(full, how):
    r, c = full.shape
    if how == "rows":
        return full.reshape(N_DEV, r // N_DEV, c)
    return full.reshape(r, N_DEV, c // N_DEV).transpose(1, 0, 2)


def _from_devices(slots, how):
    _, r, c = slots.shape
    if how == "rows":
        return slots.reshape(N_DEV * r, c)
    return slots.transpose(1, 0, 2).reshape(r, N_DEV * c)


def kernel(x, rel_bias, w_in, b_gate, ssm_conv_w, ssm_conv_b, ssm_dt_bias, ssm_a_log, ssm_d, ssm_norm_w, attn_sinks, w_branch_ssm, w_branch_attn, w_mix_out, ln1_g, ln1_b, w_up, ffn_conv_w, ffn_conv_b, w_down, ln2_g, ln2_b, loss_target, m_rel_bias, m_w_in, m_b_gate, m_ssm_conv_w, m_ssm_conv_b, m_ssm_dt_bias, m_ssm_a_log, m_ssm_d, m_ssm_norm_w, m_attn_sinks, m_w_branch_ssm, m_w_branch_attn, m_w_mix_out, m_ln1_g, m_ln1_b, m_w_up, m_ffn_conv_w, m_ffn_conv_b, m_w_down, m_ln2_g, m_ln2_b, v_rel_bias, v_w_in, v_b_gate, v_ssm_conv_w, v_ssm_conv_b, v_ssm_dt_bias, v_ssm_a_log, v_ssm_d, v_ssm_norm_w, v_attn_sinks, v_w_branch_ssm, v_w_branch_attn, v_w_mix_out, v_ln1_g, v_ln1_b, v_w_up, v_ffn_conv_w, v_ffn_conv_b, v_w_down, v_ln2_g, v_ln2_b):
    W = dict(zip(WEIGHTS, (rel_bias, w_in, b_gate, ssm_conv_w, ssm_conv_b, ssm_dt_bias, ssm_a_log, ssm_d, ssm_norm_w,
                           attn_sinks, w_branch_ssm, w_branch_attn, w_mix_out, ln1_g, ln1_b, w_up, ffn_conv_w,
                           ffn_conv_b, w_down, ln2_g, ln2_b)))
    M = dict(zip(WEIGHTS, (m_rel_bias, m_w_in, m_b_gate, m_ssm_conv_w, m_ssm_conv_b, m_ssm_dt_bias, m_ssm_a_log, m_ssm_d,
                           m_ssm_norm_w, m_attn_sinks, m_w_branch_ssm, m_w_branch_attn, m_w_mix_out, m_ln1_g, m_ln1_b,
                           m_w_up, m_ffn_conv_w, m_ffn_conv_b, m_w_down, m_ln2_g, m_ln2_b)))
    V = dict(zip(WEIGHTS, (v_rel_bias, v_w_in, v_b_gate, v_ssm_conv_w, v_ssm_conv_b, v_ssm_dt_bias, v_ssm_a_log, v_ssm_d,
                           v_ssm_norm_w, v_attn_sinks, v_w_branch_ssm, v_w_branch_attn, v_w_mix_out, v_ln1_g, v_ln1_b,
                           v_w_up, v_ffn_conv_w, v_ffn_conv_b, v_w_down, v_ln2_g, v_ln2_b)))
    shard2d = lambda a: a.reshape(a.shape[-2], a.shape[-1])

    (win_all,) = _all_gather([shard2d(w_in).astype(BF16)], "gather_w_in")
    main, tail = _w_in_from_shards(win_all)
    conv_all = _all_gather([shard2d(ssm_conv_w), shard2d(ffn_conv_w)], "gather_conv_weights")
    late_shards = [shard2d(W[n]).astype(BF16) for n in LATE]
    late = lambda landed: {SHORT[n]: _from_devices(a, SHARDED[n]) for n, a in zip(LATE, landed)}
    p = _prep_params(rel_bias, b_gate, _from_devices(conv_all[0], "cols"), ssm_conv_b, ssm_dt_bias, ssm_a_log, ssm_d,
                     ssm_norm_w, attn_sinks, ln1_g, ln1_b, _from_devices(conv_all[1], "cols"), ffn_conv_b, ln2_g, ln2_b)

    early_names = LATE + ("ssm_conv_w", "ffn_conv_w")
    early = lambda g: [_by_device(g[n], SHARDED[n]).astype(BF16 if n in LATE else F32) for n in early_names]
    sq, dproj, w, g, landed = _local_step(x[0], loss_target[0], {"in_main": main, "in_tail": tail}, p,
                                          (late_shards, late), early)
    landed = dict(zip(early_names, landed))
    dx, landed["w_in"] = _grad_x(dproj, w, exchange=[_w_in_grad_by_device(g.pop("in_main"), g.pop("in_tail"))])
    loss = (0.5 / D_MODEL) * lax.psum(sq[0, 0], ("x", "y", "c"))
    grads, deltas, new_m, new_v = {}, {}, {}, {}
    for n in SHARDED:
        outs = _adamw(landed[n], shard2d(W[n]), shard2d(M[n]), shard2d(V[n]), "adamw_" + n)
        grads[n], deltas[n], new_m[n], new_v[n] = (a.reshape(W[n].shape) for a in outs)

    row = lambda a: a.reshape(1, -1)
    packed = jnp.concatenate([jnp.pad(row(g[n]), ((0, 0), (0, SMALL_COLS - g[n].size))) for n in SMALL]
                             + [jnp.zeros((SMALL_ROWS - len(SMALL), SMALL_COLS), F32)], axis=0)
    (small_all,) = _all_gather([packed], "gather_small_grads")
    outs = _small_update(small_all, *[[row(src[n]) for n in SMALL] for src in (W, M, V)])
    for i, n in enumerate(SMALL):
        grads[n], deltas[n], new_m[n], new_v[n] = (outs[j * len(SMALL) + i].reshape(W[n].shape) for j in range(4))

    return (loss, dx[None], *[grads[n] for n in WEIGHTS], *[deltas[n] for n in WEIGHTS],
            *[new_m[n] for n in WEIGHTS], *[new_v[n] for n in WEIGHTS])
```

```python
import math

import jax
import jax.numpy as jnp
from jax import lax
from jax.experimental import pallas as pl
from jax.experimental.pallas import tpu as pltpu

F32, BF16 = jnp.float32, jnp.bfloat16
HIGHEST = lax.Precision.HIGHEST
MESH_ID = pl.DeviceIdType.MESH

N_DEV = 8
D_MODEL = 1024
SSM_INNER = 2048
SSM_HEADS = 32
SSM_HEADDIM = 64
SSMD = SSM_HEADDIM
SSM_GROUPS = 4
SSM_GROUP_COLS = SSM_INNER // SSM_GROUPS
SSM_STATE = 128
SSM_CONV = 4
CHUNK = 128
XBC_COLS = SSM_INNER + 2 * SSM_GROUPS * SSM_STATE
B_OFF = SSM_INNER
C_OFF = SSM_INNER + SSM_GROUPS * SSM_STATE
ATTN_HEADS = 16
ATTN_KV = 2
ATTN_GROUP = 8
HEADDIM = 64
WINDOW = 128
REL_BUCKETS = 32
REL_MAX_DIST = 128
D_FF = 2816
FFN_CONV = 3
ALPHA = 2.0 ** 0.25
LN_EPS = 1e-5
RMS_EPS = 1e-5
IN_COLS = 8480
Z_OFF, XBC_OFF, Q_OFF, GATE_OFF, MAIN_COLS = 0, 2048, 5120, 6144, 8192
K_OFF, V_OFF, DT_OFF, TAIL_COLS = 0, 128, 256, 384
O_Z, O_XBC, O_DT, O_Q, O_K, O_V, O_GATE = 0, 2048, 5120, 5152, 6176, 6304, 6432

ADAM_LR, ADAM_B1, ADAM_B2, ADAM_EPS, ADAM_WD, ADAM_STEP = 0.001, 0.9, 0.999, 1e-08, 0.01, 10
NEG = -1e30
HALO = 8
VMEM_LIMIT = 56 * 1024 * 1024


def _cp(sem):
    return pltpu.CompilerParams(dimension_semantics=sem, vmem_limit_bytes=VMEM_LIMIT)


def _const_spec(shape):
    nd = len(shape)
    return pl.BlockSpec(shape, lambda *_: (0,) * nd)


def _sigmoid(x):
    return 0.5 * jnp.tanh(0.5 * x) + 0.5


def _softplus(x):
    return jnp.maximum(x, 0.0) + jnp.log1p(jnp.exp(-jnp.abs(x)))


def _dot(a, b, dims=(((1,), (0,)), ((), ())), precision=None):
    return lax.dot_general(a, b, dims, preferred_element_type=F32, precision=precision)


NN = (((1,), (0,)), ((), ()))
NT = (((1,), (1,)), ((), ()))
TN = (((0,), (0,)), ((), ()))


def _mesh_pos():
    return lax.axis_index("x"), lax.axis_index("y"), lax.axis_index("c")


PEERS = N_DEV - 1


def _exchange_phases(in_refs, out_refs, send_sems, recv_sems, local_sems):
    def copies():
        x, y, c = _mesh_pos()
        me = 4 * x + 2 * y + c
        cps = []
        for b, (in_ref, out_ref) in enumerate(zip(in_refs, out_refs)):
            cps.append(pltpu.make_async_copy(in_ref.at[me], out_ref.at[me], local_sems.at[b]))
            for r in range(1, N_DEV):
                px = 1 - x if r & 4 else x
                py = 1 - y if r & 2 else y
                pc = 1 - c if r & 1 else c
                cps.append(pltpu.make_async_remote_copy(
                    src_ref=in_ref.at[4 * px + 2 * py + pc], dst_ref=out_ref.at[me],
                    send_sem=send_sems.at[b * PEERS + r - 1], recv_sem=recv_sems.at[b * PEERS + r - 1],
                    device_id=(px, py, pc), device_id_type=MESH_ID))
        return cps

    def start():
        for cp in copies():
            cp.start()

    def finish():
        for cp in copies():
            cp.wait()

    return [start, finish]


def _gather_phases(x_refs, out_refs, send_sems, recv_sems, local_sems):
    def parts(which):
        x, y, c = _mesh_pos()
        me, sibling = (x, y, c), (x, y, 1 - c)
        chips = [(1 - x, y), (x, 1 - y), (1 - x, 1 - y)]
        found = []
        for b, (x_ref, out_ref) in enumerate(zip(x_refs, out_refs)):
            def slot(px, py, pc):
                return out_ref.at[4 * px + 2 * py + pc]

            def copy(k, block, to, src=None):
                return pltpu.make_async_remote_copy(
                    src_ref=slot(*block) if src is None else src, dst_ref=slot(*block),
                    send_sem=send_sems.at[b * PEERS + k], recv_sem=recv_sems.at[b * PEERS + k],
                    device_id=to, device_id_type=MESH_ID)

            if which == "mine":
                found.append(pltpu.make_async_copy(x_ref, slot(*me), local_sems.at[b]))
            elif which == "first":
                found.append(copy(0, me, sibling, src=x_ref))
                found += [copy(1 + j, me, (*chip, c), src=x_ref) for j, chip in enumerate(chips)]
            elif which == "passed":
                found += [copy(4 + j, (*chip, c), sibling) for j, chip in enumerate(chips)]
            elif which == "arrived":
                found += [copy(1 + j, (*chip, c), me) for j, chip in enumerate(chips)]
            else:
                found.append(copy(0, sibling, me))
                found += [copy(4 + j, (*chip, 1 - c), me) for j, chip in enumerate(chips)]
        return found

    def start():
        for cp in parts("mine") + parts("first"):
            cp.start()

    def forward():
        for a, p in zip(parts("arrived"), parts("passed")):
            a.wait_recv()
            p.start()

    def finish():
        for cp in parts("late"):
            cp.wait_recv()
        for cp in parts("first") + parts("passed"):
            cp.wait_send()
        for cp in parts("mine"):
            cp.wait()

    return [start, forward, finish]


COMM = {"exchange": _exchange_phases, "gather": _gather_phases}


def _comm_sems(nb):
    return [pltpu.SemaphoreType.DMA((nb * PEERS,)), pltpu.SemaphoreType.DMA((nb * PEERS,)), pltpu.SemaphoreType.DMA((nb,))]


def _matmul(a, b, mode, out_dtype, name, tm=512, tn=1024, tk=1024, addend=None, addend_scale=1.0, comm=None):
    bufs = [] if comm is None else list(comm[1])
    nb = len(bufs)
    halves = b.ndim == 3
    if mode == "nn":
        (M, K), (K2, N) = a.shape, b.shape
    elif mode == "nt":
        (M, K), (N, K2) = a.shape, b.shape
    elif halves:
        (K, M), (K2, N) = a.shape, (b.shape[1], 2 * b.shape[2])
    else:
        (K, M), (K2, N) = a.shape, b.shape
    assert K == K2, (a.shape, b.shape, mode)
    tm, tn, tk = min(tm, M), min(tn, N), min(tk, K)
    assert M % tm == 0 and N % tn == 0 and K % tk == 0, (M, N, K, tm, tn, tk)
    nk = K // tk
    dims = {"nn": NN, "nt": NT, "tn": TN}[mode]
    a_spec = pl.BlockSpec((tk, tm), lambda i, j, k: (k, i)) if mode == "tn" else pl.BlockSpec((tm, tk), lambda i, j, k: (i, k))
    b_spec = pl.BlockSpec((tn, tk), lambda i, j, k: (j, k)) if mode == "nt" else pl.BlockSpec((tk, tn), lambda i, j, k: (k, j))
    o_spec = pl.BlockSpec((tm, tn), lambda i, j, k: (i, j))

    ni, nj = M // tm, N // tn
    if halves:
        assert mode == "tn" and nj % 2 == 0
        b_spec = pl.BlockSpec((None, tk, tn), lambda i, j, k: (j // (nj // 2), k, j % (nj // 2)))

    def body(*refs):
        refs = list(refs)
        a_ref, b_ref = refs[:2]
        c_ref = refs[2] if addend is not None else None
        n_in = 2 + (addend is not None) + nb
        o_ref = refs[n_in]
        acc = refs[n_in + 1 + nb] if nk > 1 else None
        i, j, k = pl.program_id(0), pl.program_id(1), pl.program_id(2)
        step = (i * nj + j) * nk + k
        if comm is not None:
            phases = COMM[comm[0]](refs[n_in - nb:n_in], refs[n_in + 1:n_in + 1 + nb], *refs[n_in + 1 + nb + (nk > 1):])
            at = [(ni * nj * nk - 1) * p // (len(phases) - 1) for p in range(len(phases))]
            for when, phase in zip(at[:-1], phases[:-1]):
                pl.when(step == when)(phase)

        d = _dot(a_ref[...].astype(BF16), b_ref[...].astype(BF16), dims)

        def finish(r):
            if addend is not None:
                r = r + addend_scale * c_ref[...].astype(F32)
            o_ref[...] = r.astype(out_dtype)

        if nk == 1:
            finish(d)
        else:
            @pl.when(k == 0)
            def _():
                acc[...] = d

            @pl.when(jnp.logical_and(k > 0, k < nk - 1))
            def _():
                acc[...] += d

            @pl.when(k == nk - 1)
            def _():
                finish(acc[...] + d)

        if comm is not None:
            pl.when(step == at[-1])(phases[-1])

    in_specs = [a_spec, b_spec] + ([o_spec] if addend is not None else [])
    args = (a, b) + ((addend,) if addend is not None else ())
    out_specs, out_shape = o_spec, jax.ShapeDtypeStruct((M, N), out_dtype)
    scratch = [pltpu.VMEM((tm, tn), F32)] if nk > 1 else []
    sem = ("parallel", "parallel", "arbitrary")
    if comm is not None:
        any_spec = pl.BlockSpec(memory_space=pl.ANY)
        in_specs, args = in_specs + [any_spec] * nb, args + tuple(bufs)
        landed = [x.shape if comm[0] == "exchange" else (N_DEV,) + x.shape for x in bufs]
        out_specs = [o_spec] + [any_spec] * nb
        out_shape = [out_shape] + [jax.ShapeDtypeStruct(s, x.dtype) for s, x in zip(landed, bufs)]
        scratch += _comm_sems(nb)
        sem = ("arbitrary", "arbitrary", "arbitrary")
    return pl.pallas_call(
        body, name=name, grid=(ni, nj, nk), in_specs=in_specs, out_specs=out_specs, out_shape=out_shape,
        scratch_shapes=scratch, compiler_params=_cp(sem))(*args)


def _taps(w_ref, K, tc):
    return [jnp.broadcast_to(w_ref[k:k + 1, :], (HALO, tc)) for k in range(K)]


def _conv_silu_fwd(pre, pre_col_off, C, w, b, K, name, tr=1024, tc=512):
    T = pre.shape[0]
    tr, tc = min(tr, T), min(tc, C)
    assert T % tr == 0 and C % tc == 0 and pre_col_off % tc == 0
    joff = pre_col_off // tc
    hb = tr // HALO

    def body(x_ref, xp_ref, w_ref, b_ref, o_ref, d_ref, head):
        i = pl.program_id(1)
        head[0:HALO, :] = jnp.where(i > 0, xp_ref[...], 0.0)
        head[HALO:, :] = x_ref[0:HALO, :]
        wk = _taps(w_ref, K, tc)
        bias = jnp.broadcast_to(b_ref[...], (HALO, tc))
        for r in range(tr // HALO):
            lo = r * HALO
            co = bias + wk[K - 1] * x_ref[lo:lo + HALO, :]
            for k in range(K - 1):
                s = K - 1 - k
                co = co + wk[k] * (head[HALO - s:2 * HALO - s, :] if r == 0 else x_ref[lo - s:lo + HALO - s, :])
            sg = _sigmoid(co)
            y = co * sg
            o_ref[lo:lo + HALO, :] = y
            d_ref[lo:lo + HALO, :] = sg + y * (1.0 - sg)

    out = pl.BlockSpec((tr, tc), lambda j, i: (i, j))
    return pl.pallas_call(
        body, name=name, grid=(C // tc, T // tr),
        in_specs=[pl.BlockSpec((tr, tc), lambda j, i: (i, joff + j)),
                  pl.BlockSpec((HALO, tc), lambda j, i: (jnp.maximum(i * hb - 1, 0), joff + j)),
                  pl.BlockSpec((K, tc), lambda j, i: (0, j)),
                  pl.BlockSpec((1, tc), lambda j, i: (0, j))],
        out_specs=[out, out], out_shape=[jax.ShapeDtypeStruct((T, C), F32)] * 2,
        scratch_shapes=[pltpu.VMEM((2 * HALO, tc), F32)],
        compiler_params=_cp(("parallel", "arbitrary")))(pre, pre, w, b)


def _conv_bwd(dout, pre, pre_col_off, C, w, K, dst, dst_col_off, name, tr=1024, tc=512):
    T = pre.shape[0]
    tr, tc = min(tr, T), min(tc, C)
    assert T % tr == 0 and C % tc == 0 and pre_col_off % tc == 0 and dst_col_off % tc == 0
    joff, doff = pre_col_off // tc, dst_col_off // tc
    hb = tr // HALO
    nt = T // tr
    n = tr // HALO
    last_hblock = T // HALO - 1

    def body(g_ref, gn_ref, x_ref, w_ref, *rest):
        o_ref, dw_ref, db_ref, edge = rest[-4:]
        i = pl.program_id(1)
        wk = _taps(w_ref, K, tc)
        edge[0:HALO, :] = g_ref[tr - HALO:tr, :]
        edge[HALO:, :] = jnp.where(i < nt - 1, gn_ref[...], 0.0)
        acc_w = [jnp.zeros((HALO, tc), F32) for _ in range(K)]
        acc_b = jnp.zeros((HALO, tc), F32)
        for r in range(n):
            lo = r * HALO
            x = x_ref[lo:lo + HALO, :]
            dpre = None
            for s in range(K):
                gs = edge[s:HALO + s, :] if (r == n - 1 and s > 0) else g_ref[lo + s:lo + HALO + s, :]
                dpre = wk[K - 1 - s] * gs if dpre is None else dpre + wk[K - 1 - s] * gs
                acc_w[K - 1 - s] = acc_w[K - 1 - s] + gs * x
                if s == 0:
                    acc_b = acc_b + gs
            o_ref[lo:lo + HALO, :] = dpre.astype(o_ref.dtype)

        @pl.when(i == 0)
        def _():
            dw_ref[...] = jnp.zeros_like(dw_ref)
            db_ref[...] = jnp.zeros_like(db_ref)

        db_ref[...] += jnp.sum(acc_b, axis=0, keepdims=True)
        dw_ref[...] += jnp.concatenate([jnp.sum(a, axis=0, keepdims=True) for a in acc_w], axis=0)

    tile = lambda off: pl.BlockSpec((tr, tc), lambda j, i: (i, off + j))
    in_specs = [tile(0), pl.BlockSpec((HALO, tc), lambda j, i: (jnp.minimum((i + 1) * hb, last_hblock), j)),
                tile(joff), pl.BlockSpec((K, tc), lambda j, i: (0, j))]
    args = (dout, dout, pre, w)
    if isinstance(dst, jax.ShapeDtypeStruct):
        aliases = {}
    else:
        in_specs.append(pl.BlockSpec(memory_space=pl.ANY))
        args += (dst,)
        aliases = {4: 0}
    return pl.pallas_call(
        body, name=name, grid=(C // tc, nt), in_specs=in_specs,
        out_specs=[tile(doff), pl.BlockSpec((K, tc), lambda j, i: (0, j)), pl.BlockSpec((1, tc), lambda j, i: (0, j))],
        out_shape=[jax.ShapeDtypeStruct(dst.shape, dst.dtype), jax.ShapeDtypeStruct((K, C), F32),
                   jax.ShapeDtypeStruct((1, C), F32)],
        scratch_shapes=[pltpu.VMEM((2 * HALO, tc), F32)],
        input_output_aliases=aliases,
        compiler_params=_cp(("parallel", "arbitrary")))(*args)


PAIR = 2 * SSMD
PAIRS_PER_GROUP = SSM_GROUP_COLS // PAIR


def _dot3(x, onehot):
    h1 = x.astype(BF16)
    r = x - h1.astype(F32)
    h2 = r.astype(BF16)
    h3 = (r - h2.astype(F32)).astype(BF16)
    return _dot(h1, onehot) + _dot(h2, onehot) + _dot(h3, onehot)


def _chunk_rows(dt_raw, dtb_col, alog_col):
    row = lax.broadcasted_iota(jnp.int32, (CHUNK, CHUNK), 0)
    col = lax.broadcasted_iota(jnp.int32, (CHUNK, CHUNK), 1)
    dt_rawT = dt_raw.T
    dtT = _softplus(dt_rawT + dtb_col)
    a_col = -jnp.exp(alog_col)
    acsT = _dot3(dtT * a_col, (row <= col).astype(BF16))
    return dt_rawT, dtT, a_col, acsT, row, col


def _block_diag(x, left):
    return jnp.concatenate([jnp.where(left, x, 0.0), jnp.where(left, 0.0, x)], axis=0).astype(BF16)


def _lane_bcast(v, h):
    return jnp.broadcast_to(v[:, h:h + 1], (CHUNK, CHUNK))


def _ssd_fwd(xbc, proj_main, proj_tail, dtb_col, alog_col, d_exp, norm_w):
    T = xbc.shape[0]
    nc = T // CHUNK

    def body(xbc_ref, dt_ref, z_ref, dtb_ref, alog_ref, d_ref, nw_ref, y_ref, ypre_ref, hs_ref, H):
        c = pl.program_id(0)

        @pl.when(c == 0)
        def _():
            H[...] = jnp.zeros_like(H)

        hs_ref[0] = H[...]
        _, dtT, _, acsT, row, col = _chunk_rows(dt_ref[:, 0:SSM_HEADS], dtb_ref[...], alog_ref[...])
        tril, left = row >= col, col < SSMD
        acs = acsT.T
        w = (dtT * jnp.exp(acsT[:, CHUNK - 1:CHUNK] - acsT)).T
        cd = jnp.exp(acs[CHUNK - 1:CHUNK, :])
        for g in range(SSM_GROUPS):
            gs = slice(g * SSM_GROUP_COLS, (g + 1) * SSM_GROUP_COLS)
            Bb = xbc_ref[:, B_OFF + g * SSM_STATE:B_OFF + (g + 1) * SSM_STATE].astype(BF16)
            Cb = xbc_ref[:, C_OFF + g * SSM_STATE:C_OFF + (g + 1) * SSM_STATE].astype(BF16)
            Hg = H[:, gs]
            CH = _dot(Cb, Hg.astype(BF16))
            CB = _dot(Cb, Bb, NT)
            ys, xws = [], []
            for kk in range(PAIRS_PER_GROUP):
                k = g * PAIRS_PER_GROUP + kk
                xs_p = xbc_ref[:, k * PAIR:(k + 1) * PAIR]
                mps, ecols, wcols = [], [], []
                for j in range(2):
                    h = 2 * k + j
                    colb = _lane_bcast(acs, h)
                    L = jnp.exp(jnp.where(tril, colb - acsT[h:h + 1, :], -jnp.inf))
                    mps.append((CB * L * dtT[h:h + 1, :]).astype(BF16))
                    ecols.append(jnp.exp(colb))
                    wcols.append(_lane_bcast(w, h))
                yd = _dot(jnp.concatenate(mps, axis=1), _block_diag(xs_p, left))
                ys.append(yd + CH[:, kk * PAIR:(kk + 1) * PAIR] * jnp.where(left, ecols[0], ecols[1]))
                xws.append((xs_p * jnp.where(left, wcols[0], wcols[1])).astype(BF16))
            cd_e = jnp.concatenate([jnp.broadcast_to(cd[:, g * 8 + e:g * 8 + e + 1], (1, SSMD)) for e in range(8)], axis=1)
            H[:, gs] = Hg * cd_e + _dot(Bb, jnp.concatenate(xws, axis=1), TN)
            ypre = jnp.concatenate(ys, axis=1) + xbc_ref[:, gs] * d_ref[:, gs]
            ypre_ref[:, gs] = ypre
            z = z_ref[:, gs]
            yg = ypre * (z * _sigmoid(z))
            r = lax.rsqrt(jnp.mean(yg * yg, axis=1, keepdims=True) + RMS_EPS)
            y_ref[:, gs] = (yg * r * nw_ref[:, gs]).astype(BF16)

    vec = lambda n: _const_spec((1, n))
    colv = _const_spec((SSM_HEADS, 1))
    return pl.pallas_call(
        body, name="ssd_fwd", grid=(nc,),
        in_specs=[pl.BlockSpec((CHUNK, XBC_COLS), lambda c: (c, 0)),
                  pl.BlockSpec((CHUNK, 128), lambda c: (c, DT_OFF // 128)),
                  pl.BlockSpec((CHUNK, SSM_INNER), lambda c: (c, Z_OFF // SSM_INNER)),
                  colv, colv, vec(SSM_INNER), vec(SSM_INNER)],
        out_specs=[pl.BlockSpec((CHUNK, SSM_INNER), lambda c: (c, 0)),
                   pl.BlockSpec((CHUNK, SSM_INNER), lambda c: (c, 0)),
                   pl.BlockSpec((1, SSM_STATE, SSM_INNER), lambda c: (c, 0, 0))],
        out_shape=[jax.ShapeDtypeStruct((T, SSM_INNER), BF16), jax.ShapeDtypeStruct((T, SSM_INNER), F32),
                   jax.ShapeDtypeStruct((nc, SSM_STATE, SSM_INNER), F32)],
        scratch_shapes=[pltpu.VMEM((SSM_STATE, SSM_INNER), F32)],
        compiler_params=_cp(("arbitrary",)))(xbc, proj_tail, proj_main, dtb_col, alog_col, d_exp, norm_w)


def _ssd_bwd(dyo, ypre, xbc, dsil, hs, proj_main, proj_tail, dtb_col, alog_col, d_exp, norm_w, ehead_t, dmain, dtail):
    T = xbc.shape[0]
    nc = T // CHUNK

    def body(dyo_ref, ypre_ref, xbc_ref, dsil_ref, hs_ref, dt_ref, z_ref, dtb_ref, alog_ref, d_ref, nw_ref, eh_ref,
             dmain_in, dtail_in, dz_ref, ddt_ref, dxbc_ref, dnw_ref, dd_ref, dalog_ref, ddtb_ref, G):
        del dmain_in, dtail_in
        c = pl.program_id(0)

        @pl.when(c == 0)
        def _():
            G[...] = jnp.zeros_like(G)
            dnw_ref[...] = jnp.zeros_like(dnw_ref)
            dd_ref[...] = jnp.zeros_like(dd_ref)
            dalog_ref[...] = jnp.zeros_like(dalog_ref)
            ddtb_ref[...] = jnp.zeros_like(ddtb_ref)

        dt_rawT, dtT, a_col, acsT, row, col = _chunk_rows(dt_ref[:, 0:SSM_HEADS], dtb_ref[...], alog_ref[...])
        tril, triu, left = row >= col, col >= row, col < SSMD
        acs = acsT.T
        dt = dtT.T
        lastT = acsT[:, CHUNK - 1:CHUNK]
        dstT = jnp.exp(lastT - acsT)
        wT = dtT * dstT
        cd = jnp.exp(acs[CHUNK - 1:CHUNK, :])
        ddt_rows, rs_rows, deo_rows, dw_rows = [], [], [], []
        dd_cols, gh_cols, dnw_cols = [], [], []
        for g in range(SSM_GROUPS):
            gs = slice(g * SSM_GROUP_COLS, (g + 1) * SSM_GROUP_COLS)
            z = z_ref[:, gs]
            sz = _sigmoid(z)
            silu_z = z * sz
            ypre = ypre_ref[:, gs]
            yg = ypre * silu_z
            r = lax.rsqrt(jnp.mean(yg * yg, axis=1, keepdims=True) + RMS_EPS)
            ygn = yg * r
            dyo = dyo_ref[:, gs]
            dyn = dyo * nw_ref[:, gs]
            dnw_cols.append(jnp.sum(dyo * ygn, axis=0, keepdims=True))
            dyg = r * (dyn - ygn * jnp.mean(dyn * ygn, axis=1, keepdims=True))
            dz_ref[:, gs] = (dyg * ypre * (sz * (1.0 + z * (1.0 - sz)))).astype(dz_ref.dtype)
            dY = dyg * silu_z
            xs = xbc_ref[:, gs]
            dd_cols.append(jnp.sum(dY * xs, axis=0, keepdims=True))
            Bf = xbc_ref[:, B_OFF + g * SSM_STATE:B_OFF + (g + 1) * SSM_STATE]
            Cf = xbc_ref[:, C_OFF + g * SSM_STATE:C_OFF + (g + 1) * SSM_STATE]
            Bb, Cb = Bf.astype(BF16), Cf.astype(BF16)
            BT, CT = Bf.T, Cf.T
            CB = _dot(Cb, Bb, NT)
            CBT = _dot(Bb, Cb, NT)
            Hg = hs_ref[0, :, gs]
            Gg = G[:, gs]
            gh_cols.append(jnp.sum(Gg * Hg, axis=0, keepdims=True))
            dCB = jnp.zeros((CHUNK, CHUNK), F32)
            dxs_d, dyes, xws, wsels = [], [], [], []
            for kk in range(PAIRS_PER_GROUP):
                k = g * PAIRS_PER_GROUP + kk
                ps = slice(kk * PAIR, (kk + 1) * PAIR)
                xs_p, dY_p = xs[:, ps], dY[:, ps]
                Ls, LTs, dtcols, ecols, wcols = [], [], [], [], []
                for j in range(2):
                    h = 2 * k + j
                    colb = _lane_bcast(acs, h)
                    seg = colb - acsT[h:h + 1, :]
                    Ls.append(jnp.exp(jnp.where(tril, seg, -jnp.inf)))
                    LTs.append(jnp.exp(jnp.where(triu, -seg, -jnp.inf)))
                    dtcol = _lane_bcast(dt, h)
                    dtcols.append(dtcol)
                    ecols.append(jnp.exp(colb))
                    wcols.append(dtcol * jnp.exp(acs[CHUNK - 1:CHUNK, h:h + 1] - colb))
                wsel = jnp.where(left, wcols[0], wcols[1])
                dYe_p = dY_p * jnp.where(left, ecols[0], ecols[1])
                bdx = _block_diag(xs_p, left)
                bddy = _block_diag(dY_p, left)
                dMx2 = _dot(dY_p.astype(BF16), bdx, NT)
                dMxT2 = _dot(xs_p.astype(BF16), bddy, NT)
                Q1 = _dot(Hg[:, ps].astype(BF16), _block_diag(dYe_p, left), NT)
                Q2 = _dot(Gg[:, ps].astype(BF16), bdx, NT)
                mts = []
                for j in range(2):
                    h = 2 * k + j
                    js = slice(j * CHUNK, (j + 1) * CHUNK)
                    dMx = dMx2[:, js]
                    A = CB * Ls[j]
                    AT = CBT * LTs[j]
                    ddt_rows.append(jnp.sum(A * dMx, axis=0, keepdims=True))
                    ATd = AT * dtcols[j]
                    rs_rows.append(jnp.sum(ATd * dMxT2[:, js], axis=0, keepdims=True))
                    dCB = dCB + dMx * Ls[j] * dtT[h:h + 1, :]
                    mts.append(ATd.astype(BF16))
                    deo_rows.append(jnp.sum(CT * Q1[:, js], axis=0, keepdims=True))
                    dw_rows.append(jnp.sum(BT * Q2[:, js], axis=0, keepdims=True))
                dxs_d.append(_dot(jnp.concatenate(mts, axis=1), bddy))
                dyes.append(dYe_p.astype(BF16))
                xws.append((xs_p * wsel).astype(BF16))
                wsels.append(wsel)
            dYe_g = jnp.concatenate(dyes, axis=1)
            xw_g = jnp.concatenate(xws, axis=1)
            Hgb, Ggb, dCBb = Hg.astype(BF16), Gg.astype(BF16), dCB.astype(BF16)
            cs = slice(C_OFF + g * SSM_STATE, C_OFF + (g + 1) * SSM_STATE)
            bs = slice(B_OFF + g * SSM_STATE, B_OFF + (g + 1) * SSM_STATE)
            dxbc_ref[:, cs] = (_dot(dYe_g, Hgb, NT) + _dot(dCBb, Bb)) * dsil_ref[:, cs]
            dxbc_ref[:, bs] = (_dot(xw_g, Ggb, NT) + _dot(dCBb, Cb, TN)) * dsil_ref[:, bs]
            BG = _dot(Bb, Ggb)
            dxbc_ref[:, gs] = (jnp.concatenate(dxs_d, axis=1) + BG * jnp.concatenate(wsels, axis=1)
                               + dY * d_ref[:, gs]) * dsil_ref[:, gs]
            cd_e = jnp.concatenate([jnp.broadcast_to(cd[:, g * 8 + e:g * 8 + e + 1], (1, SSMD)) for e in range(8)], axis=1)
            G[:, gs] = Gg * cd_e + _dot(Cb, dYe_g, TN)
        dnw_ref[...] += jnp.concatenate(dnw_cols, axis=1)
        eh = eh_ref[...]
        dd_ref[...] += jnp.sum(eh * jnp.concatenate(dd_cols, axis=1), axis=1, keepdims=True)
        dcd = jnp.sum(eh * jnp.concatenate(gh_cols, axis=1), axis=1, keepdims=True)
        DDT = jnp.concatenate(ddt_rows, axis=0)
        DW = jnp.concatenate(dw_rows, axis=0)
        DWw = DW * wT
        dacsT = jnp.concatenate(rs_rows, axis=0) - DDT * dtT + jnp.concatenate(deo_rows, axis=0) - DWw
        end = jnp.sum(DWw, axis=1, keepdims=True) + dcd * jnp.exp(lastT)
        lane = lax.broadcasted_iota(jnp.int32, (SSM_HEADS, CHUNK), 1)
        dacsT = dacsT + jnp.where(lane == CHUNK - 1, end, 0.0)
        dadtT = _dot3(dacsT, tril.astype(BF16))
        ddtT = dadtT * a_col + DDT + DW * dstT
        dalog_ref[...] += jnp.sum(dadtT * dtT, axis=1, keepdims=True) * a_col
        ddt_rawT = ddtT * _sigmoid(dt_rawT + dtb_ref[...])
        ddtb_ref[...] += jnp.sum(ddt_rawT, axis=1, keepdims=True)
        ddt_ref[...] = jnp.concatenate([ddt_rawT.T, jnp.zeros((CHUNK, 128 - SSM_HEADS), F32)], axis=1).astype(ddt_ref.dtype)

    rev = lambda c: nc - 1 - c
    vec = lambda n: _const_spec((1, n))
    colv = _const_spec((SSM_HEADS, 1))
    any_spec = pl.BlockSpec(memory_space=pl.ANY)
    return pl.pallas_call(
        body, name="ssd_bwd", grid=(nc,),
        in_specs=[pl.BlockSpec((CHUNK, SSM_INNER), lambda c: (rev(c), 0)),
                  pl.BlockSpec((CHUNK, SSM_INNER), lambda c: (rev(c), 0)),
                  pl.BlockSpec((CHUNK, XBC_COLS), lambda c: (rev(c), 0)),
                  pl.BlockSpec((CHUNK, XBC_COLS), lambda c: (rev(c), 0)),
                  pl.BlockSpec((1, SSM_STATE, SSM_INNER), lambda c: (rev(c), 0, 0)),
                  pl.BlockSpec((CHUNK, 128), lambda c: (rev(c), DT_OFF // 128)),
                  pl.BlockSpec((CHUNK, SSM_INNER), lambda c: (rev(c), Z_OFF // SSM_INNER)),
                  colv, colv, vec(SSM_INNER), vec(SSM_INNER), _const_spec((SSM_HEADS, SSM_INNER)), any_spec, any_spec],
        out_specs=[pl.BlockSpec((CHUNK, SSM_INNER), lambda c: (rev(c), Z_OFF // SSM_INNER)),
                   pl.BlockSpec((CHUNK, 128), lambda c: (rev(c), DT_OFF // 128)),
                   pl.BlockSpec((CHUNK, XBC_COLS), lambda c: (rev(c), 0)),
                   vec(SSM_INNER), colv, colv, colv],
        out_shape=[jax.ShapeDtypeStruct(dmain.shape, dmain.dtype), jax.ShapeDtypeStruct(dtail.shape, dtail.dtype),
                   jax.ShapeDtypeStruct((T, XBC_COLS), F32), jax.ShapeDtypeStruct((1, SSM_INNER), F32),
                   jax.ShapeDtypeStruct((SSM_HEADS, 1), F32), jax.ShapeDtypeStruct((SSM_HEADS, 1), F32),
                   jax.ShapeDtypeStruct((SSM_HEADS, 1), F32)],
        scratch_shapes=[pltpu.VMEM((SSM_STATE, SSM_INNER), F32)],
        input_output_aliases={12: 0, 13: 1},
        compiler_params=_cp(("arbitrary",)))(dyo, ypre, xbc, dsil, hs, proj_tail, proj_main, dtb_col, alog_col, d_exp,
                                             norm_w, ehead_t, dmain, dtail)


def _rel_bucket(rel):
    n = jnp.maximum(rel, 0)
    max_exact = REL_BUCKETS // 2
    nf = jnp.maximum(n, 1).astype(F32)
    large = max_exact + (jnp.log(nf / max_exact) / math.log(REL_MAX_DIST / max_exact)
                         * (REL_BUCKETS - max_exact)).astype(jnp.int32)
    large = jnp.minimum(large, REL_BUCKETS - 1)
    return jnp.where(n < max_exact, n, large)


def _band_geometry():
    qi = jnp.arange(WINDOW)[:, None] + WINDOW
    kj = jnp.arange(2 * WINDOW)[None, :]
    rel = qi - kj
    return _rel_bucket(rel), (rel >= 0) & (rel < WINDOW)


def _attn_logits(kband, qh, bias_h, first):
    s = _dot(kband, qh, NT) * (HEADDIM ** -0.5) + bias_h
    rowk = lax.broadcasted_iota(jnp.int32, (2 * WINDOW, WINDOW), 0)
    return jnp.where(jnp.logical_and(first, rowk < WINDOW), NEG, s)


def _attn_fwd(proj_main, proj_tail, bias_tbl, sinks):
    T = proj_main.shape[0]
    nb = T // WINDOW

    def body(q_ref, kv_ref, kvp_ref, bias_ref, sink_ref, o_ref, lse_ref):
        i = pl.program_id(0)
        first = i == 0
        outs, lses = [], []
        for kvh in range(ATTN_KV):
            ks = slice(K_OFF + kvh * HEADDIM, K_OFF + (kvh + 1) * HEADDIM)
            vs = slice(V_OFF + kvh * HEADDIM, V_OFF + (kvh + 1) * HEADDIM)
            kband = jnp.concatenate([kvp_ref[:, ks], kv_ref[:, ks]], axis=0).astype(BF16)
            vband = jnp.concatenate([kvp_ref[:, vs], kv_ref[:, vs]], axis=0).astype(BF16)
            heads = range(kvh * ATTN_GROUP, (kvh + 1) * ATTN_GROUP)
            logits = [_attn_logits(kband, q_ref[:, h * HEADDIM:(h + 1) * HEADDIM].astype(BF16), bias_ref[h], first)
                      for h in heads]
            probs = []
            for h, s in zip(heads, logits):
                sink = sink_ref[:, h:h + 1]
                m = jnp.maximum(jnp.max(s, axis=0, keepdims=True), sink)
                p = jnp.exp(s - m)
                den = jnp.sum(p, axis=0, keepdims=True) + jnp.exp(sink - m)
                probs.append((p * (1.0 / den)).astype(BF16))
                lses.append(m + jnp.log(den))
            outs += [_dot(pt, vband, TN) for pt in probs]
        o_ref[...] = jnp.concatenate(outs, axis=1).astype(BF16)
        lse_ref[...] = jnp.concatenate(lses, axis=0)

    return pl.pallas_call(
        body, name="attn_fwd", grid=(nb,),
        in_specs=[pl.BlockSpec((WINDOW, D_MODEL), lambda i: (i, Q_OFF // D_MODEL)),
                  pl.BlockSpec((WINDOW, 256), lambda i: (i, 0)),
                  pl.BlockSpec((WINDOW, 256), lambda i: (jnp.maximum(i - 1, 0), 0)),
                  _const_spec((ATTN_HEADS, 2 * WINDOW, WINDOW)), _const_spec((1, ATTN_HEADS))],
        out_specs=[pl.BlockSpec((WINDOW, D_MODEL), lambda i: (i, 0)),
                   pl.BlockSpec((ATTN_HEADS, WINDOW), lambda i: (0, i))],
        out_shape=[jax.ShapeDtypeStruct((T, D_MODEL), BF16), jax.ShapeDtypeStruct((ATTN_HEADS, T), F32)],
        compiler_params=_cp(("arbitrary",)))(proj_main, proj_tail, proj_tail, bias_tbl, sinks)


def _attn_bwd(dy, lse, proj_main, proj_tail, bias_tbl, sinks, dmain):
    T = proj_main.shape[0]
    nb = T // WINDOW

    def body(dy_ref, lse_ref, q_ref, kv_ref, kvp_ref, bias_ref, sink_ref, dmain_in,
             dq_ref, dkv_ref, dbias_ref, dsink_ref, carry):
        del dmain_in
        i = pl.program_id(0)
        first = i == 0

        @pl.when(first)
        def _():
            carry[...] = jnp.zeros_like(carry)
            dbias_ref[...] = jnp.zeros_like(dbias_ref)
            dsink_ref[...] = jnp.zeros_like(dsink_ref)

        @pl.when(i < nb)
        def _():
            scale = HEADDIM ** -0.5
            dqs, dsinks, dks, dvs = [], [], [], []
            for kvh in range(ATTN_KV):
                ks = slice(K_OFF + kvh * HEADDIM, K_OFF + (kvh + 1) * HEADDIM)
                vs = slice(V_OFF + kvh * HEADDIM, V_OFF + (kvh + 1) * HEADDIM)
                kband = jnp.concatenate([kvp_ref[:, ks], kv_ref[:, ks]], axis=0).astype(BF16)
                vband = jnp.concatenate([kvp_ref[:, vs], kv_ref[:, vs]], axis=0).astype(BF16)
                heads = range(kvh * ATTN_GROUP, (kvh + 1) * ATTN_GROUP)
                qs = [q_ref[:, h * HEADDIM:(h + 1) * HEADDIM].astype(BF16) for h in heads]
                dos = [dy_ref[:, h * HEADDIM:(h + 1) * HEADDIM] for h in heads]
                logits = [_attn_logits(kband, qh, bias_ref[h], first) for h, qh in zip(heads, qs)]
                dps = [_dot(vband, do, NT) for do in dos]
                pbs, dsbs = [], []
                for h, s, dp in zip(heads, logits, dps):
                    lse_h = lse_ref[h:h + 1, :]
                    p = jnp.exp(s - lse_h)
                    delta = jnp.sum(p * dp, axis=0, keepdims=True)
                    ds = p * (dp - delta)
                    psink = jnp.exp(sink_ref[:, h:h + 1] - lse_h)
                    dsinks.append(-jnp.sum(psink * delta, axis=1, keepdims=True))
                    dbias_ref[h] += ds
                    pbs.append(p.astype(BF16))
                    dsbs.append((ds * scale).astype(BF16))
                dqs += [_dot(dsb, kband, TN) for dsb in dsbs]
                dks.append(_dot(jnp.concatenate(dsbs, axis=1), jnp.concatenate(qs, axis=0)))
                dvs.append(_dot(jnp.concatenate(pbs, axis=1), jnp.concatenate(dos, axis=0)))
            dq_ref[...] = jnp.concatenate(dqs, axis=1).astype(dq_ref.dtype)
            dsink_ref[...] += jnp.concatenate(dsinks, axis=1)
            dkv = jnp.concatenate(dks + dvs, axis=1)
            dkv_ref[...] = (carry[...] + dkv[0:WINDOW, :]).astype(dkv_ref.dtype)
            carry[...] = dkv[WINDOW:, :]

        @pl.when(i == nb)
        def _():
            dkv_ref[...] = carry[...].astype(dkv_ref.dtype)

    cur = lambda i: jnp.minimum(i, nb - 1)
    return pl.pallas_call(
        body, name="attn_bwd", grid=(nb + 1,),
        in_specs=[pl.BlockSpec((WINDOW, D_MODEL), lambda i: (cur(i), 0)),
                  pl.BlockSpec((ATTN_HEADS, WINDOW), lambda i: (0, cur(i))),
                  pl.BlockSpec((WINDOW, D_MODEL), lambda i: (cur(i), Q_OFF // D_MODEL)),
                  pl.BlockSpec((WINDOW, 256), lambda i: (cur(i), 0)),
                  pl.BlockSpec((WINDOW, 256), lambda i: (jnp.maximum(cur(i) - 1, 0), 0)),
                  _const_spec((ATTN_HEADS, 2 * WINDOW, WINDOW)), _const_spec((1, ATTN_HEADS)),
                  pl.BlockSpec(memory_space=pl.ANY)],
        out_specs=[pl.BlockSpec((WINDOW, D_MODEL), lambda i: (cur(i), Q_OFF // D_MODEL)),
                   pl.BlockSpec((WINDOW, 256), lambda i: (jnp.maximum(i - 1, 0), 0)),
                   _const_spec((ATTN_HEADS, 2 * WINDOW, WINDOW)), _const_spec((1, ATTN_HEADS))],
        out_shape=[jax.ShapeDtypeStruct(dmain.shape, dmain.dtype), jax.ShapeDtypeStruct((T, TAIL_COLS), BF16),
                   jax.ShapeDtypeStruct((ATTN_HEADS, 2 * WINDOW, WINDOW), F32),
                   jax.ShapeDtypeStruct((1, ATTN_HEADS), F32)],
        scratch_shapes=[pltpu.VMEM((WINDOW, 256), F32)],
        input_output_aliases={7: 0},
        compiler_params=_cp(("arbitrary",)))(dy, lse, proj_main, proj_tail, proj_tail, bias_tbl, sinks, dmain)


def _bias_table(rel_bias_t, onehot_t, mask):
    def body(rb_ref, oh_ref, m_ref, o_ref):
        o_ref[...] = _dot3(rb_ref[...], oh_ref[...]) + m_ref[...]

    flat = pl.pallas_call(body, name="bias_table",
                          out_shape=jax.ShapeDtypeStruct((ATTN_HEADS, 2 * WINDOW * WINDOW), F32))(rel_bias_t, onehot_t, mask)
    return flat.reshape(ATTN_HEADS, 2 * WINDOW, WINDOW)


def _rel_bias_grad(dbias, onehot):
    def body(d_ref, oh_ref, o_ref):
        o_ref[...] = _dot(d_ref[...], oh_ref[...], NN, HIGHEST)

    return pl.pallas_call(body, name="rel_bias_grad",
                          out_shape=jax.ShapeDtypeStruct((ATTN_HEADS, REL_BUCKETS), F32))(dbias, onehot)


def _ln_fwd(r, g, b):
    mu = jnp.mean(r, axis=1, keepdims=True)
    xc = r - mu
    rstd = lax.rsqrt(jnp.mean(xc * xc, axis=1, keepdims=True) + LN_EPS)
    xhat = xc * rstd
    return xhat * g + b, xhat, rstd


def _ln_bwd(dy, xhat, rstd, g):
    dxh = dy * g
    return rstd * (dxh - jnp.mean(dxh, axis=1, keepdims=True) - xhat * jnp.mean(dxh * xhat, axis=1, keepdims=True))


def _merge_fwd(y_ssm, y_attn, proj_main, b_gate, w_bs, w_ba, tm=512):
    T = y_ssm.shape[0]

    def body(ys_ref, ya_ref, gs_ref, ga_ref, bg_ref, wbs_ref, wba_ref, m_ref, bs_ref, ba_ref):
        bs = _dot(ys_ref[...], wbs_ref[...])
        ba = _dot(ya_ref[...], wba_ref[...])
        g_s = _sigmoid(gs_ref[...] + bg_ref[:, 0:D_MODEL])
        g_a = _sigmoid(ga_ref[...] + bg_ref[:, D_MODEL:])
        m_ref[...] = (g_s * bs + g_a * ba).astype(BF16)
        bs_ref[...] = bs
        ba_ref[...] = ba

    row = lambda w, off=0: pl.BlockSpec((tm, w), lambda i: (i, off))
    return pl.pallas_call(
        body, name="merge_fwd", grid=(T // tm,),
        in_specs=[row(SSM_INNER), row(D_MODEL), row(D_MODEL, GATE_OFF // D_MODEL), row(D_MODEL, GATE_OFF // D_MODEL + 1),
                  _const_spec((1, 2 * D_MODEL)), _const_spec((SSM_INNER, D_MODEL)), _const_spec((D_MODEL, D_MODEL))],
        out_specs=[row(D_MODEL), row(D_MODEL), row(D_MODEL)],
        out_shape=[jax.ShapeDtypeStruct((T, D_MODEL), BF16), jax.ShapeDtypeStruct((T, D_MODEL), F32),
                   jax.ShapeDtypeStruct((T, D_MODEL), F32)],
        compiler_params=_cp(("parallel",)))(y_ssm, y_attn, proj_main, proj_main, b_gate, w_bs, w_ba)


def _mix_ln1(merged, w_mo, x, g1, b1, tm=1024):
    T = x.shape[0]
    tm = min(tm, T)

    def body(m_ref, w_ref, x_ref, g_ref, b_ref, r_ref, h_ref, hb_ref):
        r = ALPHA * x_ref[...] + _dot(m_ref[...], w_ref[...])
        r_ref[...] = r
        h = _ln_fwd(r, g_ref[...], b_ref[...])[0]
        h_ref[...] = h
        hb_ref[...] = h.astype(BF16)

    row = pl.BlockSpec((tm, D_MODEL), lambda i: (i, 0))
    return pl.pallas_call(
        body, name="mix_ln1", grid=(T // tm,),
        in_specs=[row, _const_spec((D_MODEL, D_MODEL)), row, _const_spec((1, D_MODEL)), _const_spec((1, D_MODEL))],
        out_specs=[row, row, row],
        out_shape=[jax.ShapeDtypeStruct((T, D_MODEL), F32), jax.ShapeDtypeStruct((T, D_MODEL), F32),
                   jax.ShapeDtypeStruct((T, D_MODEL), BF16)],
        compiler_params=_cp(("parallel",)))(merged, w_mo, x, g1, b1)


def _ffn_conv_glu(u_pre, w, b, tr=2048, tc=256):
    T = u_pre.shape[0]
    tr = min(tr, T)
    K = FFN_CONV
    nj = D_FF // tc
    hb = tr // HALO
    assert T % tr == 0 and D_FF % tc == 0

    def body(xg_ref, xgp_ref, xv_ref, xvp_ref, wg_ref, wv_ref, bg_ref, bv_ref, u_ref, a_ref, head_g, head_v):
        i = pl.program_id(1)
        halves = []
        for x_ref, xp_ref, w_ref, b_ref, head in ((xg_ref, xgp_ref, wg_ref, bg_ref, head_g),
                                                  (xv_ref, xvp_ref, wv_ref, bv_ref, head_v)):
            head[0:HALO, :] = jnp.where(i > 0, xp_ref[...], 0.0)
            head[HALO:, :] = x_ref[0:HALO, :]
            halves.append((x_ref, head, _taps(w_ref, K, tc), jnp.broadcast_to(b_ref[...], (HALO, tc))))

        def conv(half, r):
            x_ref, head, wk, bias = halves[half]
            lo = r * HALO
            acc = bias + wk[K - 1] * x_ref[lo:lo + HALO, :]
            for k in range(K - 1):
                s = K - 1 - k
                acc = acc + wk[k] * (head[HALO - s:2 * HALO - s, :] if r == 0 else x_ref[lo - s:lo + HALO - s, :])
            return acc

        for r2 in range(tr // (2 * HALO)):
            acts = []
            for r in (2 * r2, 2 * r2 + 1):
                lo = r * HALO
                ug, uv = conv(0, r), conv(1, r)
                u_ref[0, lo:lo + HALO, :] = ug
                u_ref[1, lo:lo + HALO, :] = uv
                acts.append(ug * _sigmoid(ug) * uv)
            a_ref[2 * r2 * HALO:(2 * r2 + 2) * HALO, :] = jnp.concatenate(acts, axis=0).astype(BF16)

    tile = lambda off: pl.BlockSpec((tr, tc), lambda j, i: (i, off + j))
    prev = lambda off: pl.BlockSpec((HALO, tc), lambda j, i: (jnp.maximum(i * hb - 1, 0), off + j))
    row = lambda rows, off: pl.BlockSpec((rows, tc), lambda j, i: (0, off + j))
    return pl.pallas_call(
        body, name="ffn_conv_glu", grid=(nj, T // tr),
        in_specs=[tile(0), prev(0), tile(nj), prev(nj), row(K, 0), row(K, nj), row(1, 0), row(1, nj)],
        out_specs=[pl.BlockSpec((2, tr, tc), lambda j, i: (0, i, j)), pl.BlockSpec((tr, tc), lambda j, i: (i, j))],
        out_shape=[jax.ShapeDtypeStruct((2, T, D_FF), F32), jax.ShapeDtypeStruct((T, D_FF), BF16)],
        scratch_shapes=[pltpu.VMEM((2 * HALO, tc), F32), pltpu.VMEM((2 * HALO, tc), F32)],
        compiler_params=_cp(("parallel", "arbitrary")))(u_pre, u_pre, u_pre, u_pre, w, w, b, b)


def _down_ln2_loss(act, w_down, h1, target, g2, b2, tm=512):
    T = h1.shape[0]

    def body(a_ref, w_ref, h_ref, t_ref, g_ref, b_ref, dr_ref, drb_ref, dg_ref, db_ref, l_ref):
        @pl.when(pl.program_id(0) == 0)
        def _():
            dg_ref[...] = jnp.zeros_like(dg_ref)
            db_ref[...] = jnp.zeros_like(db_ref)
            l_ref[...] = jnp.zeros_like(l_ref)

        r = ALPHA * h_ref[...] + _dot(a_ref[...], w_ref[...])
        y, xhat, rstd = _ln_fwd(r, g_ref[...], b_ref[...])
        err = y - t_ref[...]
        l_ref[...] += jnp.sum(err * err, keepdims=True)
        dy = err * (1.0 / D_MODEL)
        dg_ref[...] += jnp.sum(dy * xhat, axis=0, keepdims=True)
        db_ref[...] += jnp.sum(dy, axis=0, keepdims=True)
        dr = _ln_bwd(dy, xhat, rstd, g_ref[...])
        dr_ref[...] = dr
        drb_ref[...] = dr.astype(BF16)

    row = pl.BlockSpec((tm, D_MODEL), lambda i: (i, 0))
    vec = _const_spec((1, D_MODEL))
    return pl.pallas_call(
        body, name="down_ln2_loss", grid=(T // tm,),
        in_specs=[pl.BlockSpec((tm, D_FF), lambda i: (i, 0)), _const_spec((D_FF, D_MODEL)), row, row, vec, vec],
        out_specs=[row, row, vec, vec, _const_spec((1, 1))],
        out_shape=[jax.ShapeDtypeStruct((T, D_MODEL), F32), jax.ShapeDtypeStruct((T, D_MODEL), BF16),
                   jax.ShapeDtypeStruct((1, D_MODEL), F32), jax.ShapeDtypeStruct((1, D_MODEL), F32),
                   jax.ShapeDtypeStruct((1, 1), F32)],
        compiler_params=_cp(("arbitrary",)))(act, w_down, h1, target, g2, b2)


def _ffn_gate_conv_bwd(dact, u, u_pre, w, tr=2048, tc=256):
    T = dact.shape[0]
    tr = min(tr, T)
    K = FFN_CONV
    nj, nt, n, hb = D_FF // tc, T // tr, tr // HALO, tr // HALO
    last_hblock = T // HALO - 1
    assert T % tr == 0 and D_FF % tc == 0 and n % 2 == 0

    def body(da_ref, dan_ref, u_ref, un_ref, xg_ref, xv_ref, wg_ref, wv_ref,
             o_ref, dwg_ref, dwv_ref, dbg_ref, dbv_ref, gext_g, gext_v):
        i = pl.program_id(1)
        for r in range(n + 1):
            rows = slice(r * HALO, (r + 1) * HALO)
            if r < n:
                da, g, v = da_ref[rows, :], u_ref[0, rows, :], u_ref[1, rows, :]
            else:
                da, g, v = jnp.where(i < nt - 1, dan_ref[...], 0.0), un_ref[0], un_ref[1]
            sg = _sigmoid(g)
            gext_g[rows, :] = da * v * (sg * (1.0 + g * (1.0 - sg)))
            gext_v[rows, :] = da * (g * sg)
        for half, (gext, x_ref, w_ref, dw_ref, db_ref) in enumerate(((gext_g, xg_ref, wg_ref, dwg_ref, dbg_ref),
                                                                      (gext_v, xv_ref, wv_ref, dwv_ref, dbv_ref))):
            wk = _taps(w_ref, K, tc)
            acc_w = [jnp.zeros((HALO, tc), F32) for _ in range(K)]
            acc_b = jnp.zeros((HALO, tc), F32)
            for r2 in range(n // 2):
                pair = []
                for r in (2 * r2, 2 * r2 + 1):
                    lo = r * HALO
                    x = x_ref[lo:lo + HALO, :]
                    dpre = None
                    for s in range(K):
                        gs = gext[lo + s:lo + HALO + s, :]
                        dpre = wk[K - 1 - s] * gs if dpre is None else dpre + wk[K - 1 - s] * gs
                        acc_w[K - 1 - s] = acc_w[K - 1 - s] + gs * x
                        if s == 0:
                            acc_b = acc_b + gs
                    pair.append(dpre)
                o_ref[half, 2 * r2 * HALO:(2 * r2 + 2) * HALO, :] = jnp.concatenate(pair, axis=0).astype(o_ref.dtype)

            @pl.when(i == 0)
            def _():
                dw_ref[...] = jnp.zeros_like(dw_ref)
                db_ref[...] = jnp.zeros_like(db_ref)

            db_ref[...] += jnp.sum(acc_b, axis=0, keepdims=True)
            dw_ref[...] += jnp.concatenate([jnp.sum(a, axis=0, keepdims=True) for a in acc_w], axis=0)

    nxt = lambda i: jnp.minimum((i + 1) * hb, last_hblock)
    taps = lambda off: pl.BlockSpec((K, tc), lambda j, i: (0, off + j))
    dw_spec, db_spec = pl.BlockSpec((K, tc), lambda j, i: (0, j)), pl.BlockSpec((1, tc), lambda j, i: (0, j))
    du_pre, dwg, dwv, dbg, dbv = pl.pallas_call(
        body, name="ffn_gate_conv_bwd", grid=(nj, nt),
        in_specs=[pl.BlockSpec((tr, tc), lambda j, i: (i, j)), pl.BlockSpec((HALO, tc), lambda j, i: (nxt(i), j)),
                  pl.BlockSpec((2, tr, tc), lambda j, i: (0, i, j)), pl.BlockSpec((2, HALO, tc), lambda j, i: (0, nxt(i), j)),
                  pl.BlockSpec((tr, tc), lambda j, i: (i, j)), pl.BlockSpec((tr, tc), lambda j, i: (i, nj + j)),
                  taps(0), taps(nj)],
        out_specs=[pl.BlockSpec((2, tr, tc), lambda j, i: (0, i, j)), dw_spec, dw_spec, db_spec, db_spec],
        out_shape=[jax.ShapeDtypeStruct((2, T, D_FF), BF16), jax.ShapeDtypeStruct((K, D_FF), F32),
                   jax.ShapeDtypeStruct((K, D_FF), F32), jax.ShapeDtypeStruct((1, D_FF), F32),
                   jax.ShapeDtypeStruct((1, D_FF), F32)],
        scratch_shapes=[pltpu.VMEM((tr + HALO, tc), F32), pltpu.VMEM((tr + HALO, tc), F32)],
        compiler_params=_cp(("parallel", "arbitrary")))(dact, dact, u, u, u_pre, u_pre, w, w)
    return du_pre, jnp.concatenate([dwg, dwv], axis=1), jnp.concatenate([dbg, dbv], axis=1)


def _ffn_bwd_in(du_pre, w_up, dr2, r1, g1, b1, tm=512, tk=2816):
    T = dr2.shape[0]
    tm = min(tm, T)
    assert T % tm == 0 and D_FF % tk == 0
    nk = 2 * D_FF // tk
    kh = D_FF // tk

    def body(d_ref, w_ref, dr2_ref, r_ref, g_ref, b_ref, dr1_ref, dr1b_ref, dg_ref, db_ref, acc):
        i, k = pl.program_id(0), pl.program_id(1)

        @pl.when(jnp.logical_and(i == 0, k == 0))
        def _():
            dg_ref[...] = jnp.zeros_like(dg_ref)
            db_ref[...] = jnp.zeros_like(db_ref)

        @pl.when(k == 0)
        def _():
            acc[...] = ALPHA * dr2_ref[...]

        acc[...] += _dot(d_ref[...], w_ref[...], NT)

        @pl.when(k == nk - 1)
        def _():
            _, xhat, rstd = _ln_fwd(r_ref[...], g_ref[...], b_ref[...])
            dy = acc[...]
            dg_ref[...] += jnp.sum(dy * xhat, axis=0, keepdims=True)
            db_ref[...] += jnp.sum(dy, axis=0, keepdims=True)
            dr1 = _ln_bwd(dy, xhat, rstd, g_ref[...])
            dr1_ref[...] = dr1
            dr1b_ref[...] = dr1.astype(BF16)

    row = pl.BlockSpec((tm, D_MODEL), lambda i, k: (i, 0))
    vec = _const_spec((1, D_MODEL))
    return pl.pallas_call(
        body, name="ffn_bwd_in", grid=(T // tm, nk),
        in_specs=[pl.BlockSpec((None, tm, tk), lambda i, k: (k // kh, i, k % kh)),
                  pl.BlockSpec((D_MODEL, tk), lambda i, k: (0, k)), row, row, vec, vec],
        out_specs=[row, row, vec, vec],
        out_shape=[jax.ShapeDtypeStruct((T, D_MODEL), F32), jax.ShapeDtypeStruct((T, D_MODEL), BF16),
                   jax.ShapeDtypeStruct((1, D_MODEL), F32), jax.ShapeDtypeStruct((1, D_MODEL), F32)],
        scratch_shapes=[pltpu.VMEM((tm, D_MODEL), F32)],
        compiler_params=_cp(("arbitrary", "arbitrary")))(du_pre, w_up, dr2, r1, g1, b1)


def _mix_bwd(dr1, w_mo, w_bs, w_ba, bs, ba, proj_main, b_gate, tm=512):
    T = dr1.shape[0]

    def body(d_ref, wmo_ref, wbs_ref, wba_ref, bs_ref, ba_ref, gs_ref, ga_ref, bg_ref,
             dg_ref, dbs_ref, dba_ref, dys_ref, dya_ref, dbg_ref):
        @pl.when(pl.program_id(0) == 0)
        def _():
            dbg_ref[...] = jnp.zeros_like(dbg_ref)

        dm = _dot(d_ref[...].astype(BF16), wmo_ref[...], NT)
        g_s = _sigmoid(gs_ref[...] + bg_ref[:, 0:D_MODEL])
        g_a = _sigmoid(ga_ref[...] + bg_ref[:, D_MODEL:])
        dgs = dm * bs_ref[...] * g_s * (1.0 - g_s)
        dga = dm * ba_ref[...] * g_a * (1.0 - g_a)
        dg_ref[:, 0:D_MODEL] = dgs.astype(BF16)
        dg_ref[:, D_MODEL:] = dga.astype(BF16)
        dbg_ref[:, 0:D_MODEL] += jnp.sum(dgs, axis=0, keepdims=True)
        dbg_ref[:, D_MODEL:] += jnp.sum(dga, axis=0, keepdims=True)
        dbs = (dm * g_s).astype(BF16)
        dba = (dm * g_a).astype(BF16)
        dbs_ref[...] = dbs
        dba_ref[...] = dba
        dys_ref[...] = _dot(dbs, wbs_ref[...], NT)
        dya_ref[...] = _dot(dba, wba_ref[...], NT).astype(BF16)

    row = lambda w, off=0: pl.BlockSpec((tm, w), lambda i: (i, off))
    return pl.pallas_call(
        body, name="mix_bwd", grid=(T // tm,),
        in_specs=[row(D_MODEL), _const_spec((D_MODEL, D_MODEL)), _const_spec((SSM_INNER, D_MODEL)),
                  _const_spec((D_MODEL, D_MODEL)), row(D_MODEL), row(D_MODEL),
                  row(D_MODEL, GATE_OFF // D_MODEL), row(D_MODEL, GATE_OFF // D_MODEL + 1), _const_spec((1, 2 * D_MODEL))],
        out_specs=[row(2 * D_MODEL, GATE_OFF // (2 * D_MODEL)), row(D_MODEL), row(D_MODEL), row(SSM_INNER), row(D_MODEL),
                   _const_spec((1, 2 * D_MODEL))],
        out_shape=[jax.ShapeDtypeStruct((T, MAIN_COLS), BF16), jax.ShapeDtypeStruct((T, D_MODEL), BF16),
                   jax.ShapeDtypeStruct((T, D_MODEL), BF16), jax.ShapeDtypeStruct((T, SSM_INNER), F32),
                   jax.ShapeDtypeStruct((T, D_MODEL), BF16), jax.ShapeDtypeStruct((1, 2 * D_MODEL), F32)],
        compiler_params=_cp(("arbitrary",)))(dr1, w_mo, w_bs, w_ba, bs, ba, proj_main, proj_main, b_gate)


def _local_step(x, target, w, p, late_weights=None, early_grads=None):
    xb = x.astype(BF16)
    if late_weights is None:
        proj_main = _matmul(xb, w["in_main"], "nn", F32, "in_proj_main", tm=1024, tn=2048)
    else:
        proj_main, *landed = _matmul(xb, w["in_main"], "nn", F32, "in_proj_main", tm=1024, tn=2048,
                                     comm=("gather", late_weights[0]))
        w = {**w, **late_weights[1](landed)}
    proj_tail = _matmul(xb, w["in_tail"], "nn", F32, "in_proj_tail", tm=2048, tn=TAIL_COLS)
    xbc, dsil = _conv_silu_fwd(proj_main, XBC_OFF, XBC_COLS, p["ssm_conv_w"], p["ssm_conv_b"], SSM_CONV, "ssm_conv_fwd",
                               tr=2048)
    y_ssm, ypre, hs = _ssd_fwd(xbc, proj_main, proj_tail, p["dtb_col"], p["alog_col"], p["d_exp"], p["ssm_norm_w"])
    y_attn, lse = _attn_fwd(proj_main, proj_tail, p["bias_tbl"], p["attn_sinks"])
    merged, bs, ba = _merge_fwd(y_ssm, y_attn, proj_main, p["b_gate"], w["bs"], w["ba"])
    r1, h1, h1b = _mix_ln1(merged, w["mo"], x, p["ln1_g"], p["ln1_b"])
    u_pre = _matmul(h1b, w["up"], "nn", F32, "ffn_up", tm=1024, tn=2816)
    u, act = _ffn_conv_glu(u_pre, p["ffn_conv_w"], p["ffn_conv_b"])
    dr2, dr2b, dg2, db2, sq = _down_ln2_loss(act, w["down"], h1, target, p["ln2_g"], p["ln2_b"])
    g = {"ln2_g": dg2, "ln2_b": db2}
    g["w_down"] = _matmul(act, dr2b, "tn", BF16, "dw_down", tm=1408, tn=1024, tk=2048)
    dact = _matmul(dr2b, w["down"], "nt", F32, "ffn_dact", tm=1024, tn=2816, tk=1024)
    du_pre, g["ffn_conv_w"], g["ffn_conv_b"] = _ffn_gate_conv_bwd(dact, u, u_pre, p["ffn_conv_w"])
    g["w_up"] = _matmul(h1b, du_pre, "tn", BF16, "dw_up", tm=1024, tn=1408, tk=2048)
    dr1, dr1b, g["ln1_g"], g["ln1_b"] = _ffn_bwd_in(du_pre, w["up"], dr2, r1, p["ln1_g"], p["ln1_b"])
    g["w_mix_out"] = _matmul(merged, dr1b, "tn", BF16, "dw_mix_out", tm=1024, tn=1024, tk=2048)
    dmain, dbs, dba, dy_ssm, dy_attn, g["b_gate"] = _mix_bwd(dr1b, w["mo"], w["bs"], w["ba"], bs, ba, proj_main, p["b_gate"])
    g["w_branch_ssm"] = _matmul(y_ssm, dbs, "tn", BF16, "dw_branch_ssm", tm=1024, tn=1024, tk=2048)
    g["w_branch_attn"] = _matmul(y_attn, dba, "tn", BF16, "dw_branch_attn", tm=1024, tn=1024, tk=2048)
    dmain, dtail, dbias, g["attn_sinks"] = _attn_bwd(dy_attn, lse, proj_main, proj_tail, p["bias_tbl"], p["attn_sinks"], dmain)
    g["rel_bias"] = _rel_bias_grad(dbias.reshape(ATTN_HEADS, WINDOW * 2 * WINDOW), p["bucket_onehot"]).T
    dmain, dtail, dco, g["ssm_norm_w"], dd, dalog, ddtb = _ssd_bwd(
        dy_ssm, ypre, xbc, dsil, hs, proj_main, proj_tail, p["dtb_col"], p["alog_col"], p["d_exp"], p["ssm_norm_w"],
        p["ehead_t"], dmain, dtail)
    g["ssm_d"], g["ssm_a_log"], g["ssm_dt_bias"] = (a.reshape(1, SSM_HEADS) for a in (dd, dalog, ddtb))
    dmain, g["ssm_conv_w"], g["ssm_conv_b"] = _conv_bwd(
        dco, proj_main, XBC_OFF, XBC_COLS, p["ssm_conv_w"], SSM_CONV, dmain, XBC_OFF, "ssm_conv_bwd",
        tr=2048)
    g["in_tail"] = _matmul(xb, dtail, "tn", BF16, "dw_in_tail", tm=1024, tn=TAIL_COLS, tk=2048)
    landed = []
    if early_grads is None:
        g["in_main"] = _matmul(xb, dmain, "tn", BF16, "dw_in_main", tm=1024, tn=2048, tk=2048)
    else:
        g["in_main"], *landed = _matmul(xb, dmain, "tn", BF16, "dw_in_main", tm=1024, tn=2048, tk=2048,
                                        comm=("exchange", early_grads(g)))
    return sq, (dmain, dtail, dr1), w, g, landed


def _grad_x(dproj, w, exchange=None):
    dmain, dtail, dr1 = dproj
    landed = None
    if exchange is None:
        dx = _matmul(dmain, w["in_main"], "nt", F32, "dx_main", tm=1024, tk=2048, addend=dr1, addend_scale=ALPHA)
    else:
        dx, landed = _matmul(dmain, w["in_main"], "nt", F32, "dx_main", tm=1024, tk=2048, addend=dr1,
                             addend_scale=ALPHA, comm=("exchange", exchange))
    dx = _matmul(dtail, w["in_tail"], "nt", F32, "dx_tail", tm=2048, tk=TAIL_COLS, addend=dx)
    return dx if exchange is None else (dx, landed)


SHARD_COLS = IN_COLS // N_DEV
W_IN_SEGMENTS = ((O_Z, 2048, "main", Z_OFF), (O_XBC, XBC_COLS, "main", XBC_OFF), (O_DT, SSM_HEADS, "tail", DT_OFF),
                 (O_Q, D_MODEL, "main", Q_OFF), (O_K, 128, "tail", K_OFF), (O_V, 128, "tail", V_OFF),
                 (O_GATE, 2 * D_MODEL, "main", GATE_OFF))


def _w_in_from_shards(shards):
    def seg(off, n):
        pieces = []
        for j in range(off // SHARD_COLS, (off + n - 1) // SHARD_COLS + 1):
            lo, hi = max(off, j * SHARD_COLS), min(off + n, (j + 1) * SHARD_COLS)
            pieces.append(shards[j, :, lo - j * SHARD_COLS:hi - j * SHARD_COLS])
        return pieces

    by_name = {(where, koff): seg(off, n) for off, n, where, koff in W_IN_SEGMENTS}
    main = jnp.concatenate(by_name["main", Z_OFF] + by_name["main", XBC_OFF] + by_name["main", Q_OFF]
                           + by_name["main", GATE_OFF], axis=1)
    tail = jnp.concatenate(by_name["tail", K_OFF] + by_name["tail", V_OFF] + by_name["tail", DT_OFF]
                           + [jnp.zeros((shards.shape[1], 128 - SSM_HEADS), shards.dtype)], axis=1)
    return main, tail


def _w_in_grad_by_device(g_main, g_tail):
    slots = []
    for j in range(N_DEV):
        a, b = j * SHARD_COLS, (j + 1) * SHARD_COLS
        pieces = []
        for off, n, where, koff in W_IN_SEGMENTS:
            lo, hi = max(a, off), min(b, off + n)
            if lo < hi:
                pieces.append((g_main if where == "main" else g_tail)[:, koff + lo - off:koff + hi - off])
        slots.append(jnp.concatenate(pieces, axis=1))
    return jnp.stack(slots)


def _prep_params(rel_bias, b_gate, ssm_conv_w, ssm_conv_b, ssm_dt_bias, ssm_a_log, ssm_d, ssm_norm_w, attn_sinks,
                 ln1_g, ln1_b, ffn_conv_w, ffn_conv_b, ln2_g, ln2_b):
    bucket, in_window = _band_geometry()
    bucket, in_window = bucket.T, in_window.T
    onehot = jnp.logical_and(bucket.reshape(-1, 1) == jnp.arange(REL_BUCKETS)[None, :],
                             in_window.reshape(-1, 1)).astype(F32)
    onehot_t = jnp.logical_and(bucket.reshape(1, -1) == jnp.arange(REL_BUCKETS)[:, None],
                               in_window.reshape(1, -1)).astype(BF16)
    bias_tbl = _bias_table(rel_bias.T, onehot_t, jnp.where(in_window.reshape(1, -1), 0.0, NEG))
    ehead_t = (jnp.arange(SSM_INNER)[None, :] // SSMD == jnp.arange(SSM_HEADS)[:, None]).astype(F32)
    return {"bias_tbl": bias_tbl, "bucket_onehot": onehot, "b_gate": b_gate, "ssm_conv_w": ssm_conv_w,
            "ssm_conv_b": ssm_conv_b, "dtb_col": ssm_dt_bias.reshape(SSM_HEADS, 1),
            "alog_col": ssm_a_log.reshape(SSM_HEADS, 1), "ehead_t": ehead_t,
            "d_exp": jnp.repeat(ssm_d, SSMD, axis=1), "ssm_norm_w": ssm_norm_w, "attn_sinks": attn_sinks,
            "ln1_g": ln1_g, "ln1_b": ln1_b, "ffn_conv_w": ffn_conv_w, "ffn_conv_b": ffn_conv_b,
            "ln2_g": ln2_g, "ln2_b": ln2_b}


def _all_gather(shards, name):
    nb = len(shards)

    def body(*refs):
        for phase in _gather_phases(refs[:nb], refs[nb:2 * nb], *refs[2 * nb:]):
            phase()

    any_spec = pl.BlockSpec(memory_space=pl.ANY)
    return pl.pallas_call(
        body, name=name, out_shape=[jax.ShapeDtypeStruct((N_DEV,) + s.shape, s.dtype) for s in shards],
        in_specs=[any_spec] * nb, out_specs=[any_spec] * nb, scratch_shapes=_comm_sems(nb))(*shards)


def _adamw_math(w, g, m, v):
    m = ADAM_B1 * m + (1.0 - ADAM_B1) * g
    v = ADAM_B2 * v + (1.0 - ADAM_B2) * (g * g)
    m_hat = m / (1.0 - ADAM_B1 ** ADAM_STEP)
    v_hat = v / (1.0 - ADAM_B2 ** ADAM_STEP)
    return -ADAM_LR * (m_hat / (jnp.sqrt(v_hat) + ADAM_EPS) + ADAM_WD * w), m, v


def _slot_total(s_ref):
    g = s_ref[0].astype(F32)
    for i in range(1, N_DEV):
        g = g + s_ref[i].astype(F32)
    return g


def _adamw(landed, w, m, v, name):
    R, C = w.shape
    tr = 256 if R % 256 == 0 and R > 256 else R

    def body(s_ref, w_ref, m_ref, v_ref, g_ref, d_ref, nm_ref, nv_ref):
        g = _slot_total(s_ref)
        g_ref[...] = g
        d_ref[...], nm_ref[...], nv_ref[...] = _adamw_math(w_ref[...], g, m_ref[...], v_ref[...])

    spec = pl.BlockSpec((tr, C), lambda i: (i, 0))
    return pl.pallas_call(
        body, name=name, grid=(R // tr,), in_specs=[pl.BlockSpec((N_DEV, tr, C), lambda i: (0, i, 0))] + [spec] * 3,
        out_specs=[spec] * 4, out_shape=[jax.ShapeDtypeStruct((R, C), F32)] * 4,
        compiler_params=_cp(("parallel",)))(landed, w, m, v)


def _small_update(landed, ws, ms, vs):
    k = len(ws)

    def body(*refs):
        s_ref, w_refs, m_refs, v_refs = refs[0], refs[1:1 + k], refs[1 + k:1 + 2 * k], refs[1 + 2 * k:1 + 3 * k]
        outs = refs[1 + 3 * k:]
        g_all = _slot_total(s_ref)
        for i in range(k):
            n = w_refs[i].shape[1]
            g = g_all[i:i + 1, 0:n]
            outs[i][...] = g
            outs[k + i][...], outs[2 * k + i][...], outs[3 * k + i][...] = _adamw_math(
                w_refs[i][...], g, m_refs[i][...], v_refs[i][...])

    return pl.pallas_call(body, name="small_update",
                          out_shape=[jax.ShapeDtypeStruct(w.shape, F32) for w in ws] * 4)(landed, *ws, *ms, *vs)


SHARDED = {"w_in": "cols", "w_branch_ssm": "rows", "w_branch_attn": "rows", "w_mix_out": "rows", "w_up": "cols",
           "w_down": "rows", "ssm_conv_w": "cols", "ffn_conv_w": "cols"}
LATE = ("w_branch_ssm", "w_branch_attn", "w_mix_out", "w_up", "w_down")
SHORT = {"w_branch_ssm": "bs", "w_branch_attn": "ba", "w_mix_out": "mo", "w_up": "up", "w_down": "down"}
SMALL = ("rel_bias", "b_gate", "ssm_conv_b", "ssm_dt_bias", "ssm_a_log", "ssm_d", "ssm_norm_w", "attn_sinks",
         "ln1_g", "ln1_b", "ffn_conv_b", "ln2_g", "ln2_b")
WEIGHTS = ("rel_bias", "w_in", "b_gate", "ssm_conv_w", "ssm_conv_b", "ssm_dt_bias", "ssm_a_log", "ssm_d", "ssm_norm_w",
           "attn_sinks", "w_branch_ssm", "w_branch_attn", "w_mix_out", "ln1_g", "ln1_b", "w_up", "ffn_conv_w",
           "ffn_conv_b", "w_down", "ln2_g", "ln2_b")
SMALL_ROWS, SMALL_COLS = 16, 2 * D_FF


def _by_device(full, how):
    r, c = full.shape
    if how == "rows":
        return full.reshape(N_DEV, r // N_DEV, c)
    return full.reshape(r, N_DEV, c // N_DEV).transpose(1, 0, 2)


def _from_devices(slots, how):
    _, r, c = slots.shape
    if how == "rows":
        return slots.reshape(N_DEV * r, c)
    return slots.transpose(1, 0, 2).reshape(r, N_DEV * c)


def kernel(x, rel_bias, w_in, b_gate, ssm_conv_w, ssm_conv_b, ssm_dt_bias, ssm_a_log, ssm_d, ssm_norm_w, attn_sinks, w_branch_ssm, w_branch_attn, w_mix_out, ln1_g, ln1_b, w_up, ffn_conv_w, ffn_conv_b, w_down, ln2_g, ln2_b, loss_target, m_rel_bias, m_w_in, m_b_gate, m_ssm_conv_w, m_ssm_conv_b, m_ssm_dt_bias, m_ssm_a_log, m_ssm_d, m_ssm_norm_w, m_attn_sinks, m_w_branch_ssm, m_w_branch_attn, m_w_mix_out, m_ln1_g, m_ln1_b, m_w_up, m_ffn_conv_w, m_ffn_conv_b, m_w_down, m_ln2_g, m_ln2_b, v_rel_bias, v_w_in, v_b_gate, v_ssm_conv_w, v_ssm_conv_b, v_ssm_dt_bias, v_ssm_a_log, v_ssm_d, v_ssm_norm_w, v_attn_sinks, v_w_branch_ssm, v_w_branch_attn, v_w_mix_out, v_ln1_g, v_ln1_b, v_w_up, v_ffn_conv_w, v_ffn_conv_b, v_w_down, v_ln2_g, v_ln2_b):
    W = dict(zip(WEIGHTS, (rel_bias, w_in, b_gate, ssm_conv_w, ssm_conv_b, ssm_dt_bias, ssm_a_log, ssm_d, ssm_norm_w,
                           attn_sinks, w_branch_ssm, w_branch_attn, w_mix_out, ln1_g, ln1_b, w_up, ffn_conv_w,
                           ffn_conv_b, w_down, ln2_g, ln2_b)))
    M = dict(zip(WEIGHTS, (m_rel_bias, m_w_in, m_b_gate, m_ssm_conv_w, m_ssm_conv_b, m_ssm_dt_bias, m_ssm_a_log, m_ssm_d,
                           m_ssm_norm_w, m_attn_sinks, m_w_branch_ssm, m_w_branch_attn, m_w_mix_out, m_ln1_g, m_ln1_b,
                           m_w_up, m_ffn_conv_w, m_ffn_conv_b, m_w_down, m_ln2_g, m_ln2_b)))
    V = dict(zip(WEIGHTS, (v_rel_bias, v_w_in, v_b_gate, v_ssm_conv_w, v_ssm_conv_b, v_ssm_dt_bias, v_ssm_a_log, v_ssm_d,
                           v_ssm_norm_w, v_attn_sinks, v_w_branch_ssm, v_w_branch_attn, v_w_mix_out, v_ln1_g, v_ln1_b,
                           v_w_up, v_ffn_conv_w, v_ffn_conv_b, v_w_down, v_ln2_g, v_ln2_b)))
    shard2d = lambda a: a.reshape(a.shape[-2], a.shape[-1])

    (win_all,) = _all_gather([shard2d(w_in).astype(BF16)], "gather_w_in")
    main, tail = _w_in_from_shards(win_all)
    conv_all = _all_gather([shard2d(ssm_conv_w), shard2d(ffn_conv_w)], "gather_conv_weights")
    late_shards = [shard2d(W[n]).astype(BF16) for n in LATE]
    late = lambda landed: {SHORT[n]: _from_devices(a, SHARDED[n]) for n, a in zip(LATE, landed)}
    p = _prep_params(rel_bias, b_gate, _from_devices(conv_all[0], "cols"), ssm_conv_b, ssm_dt_bias, ssm_a_log, ssm_d,
                     ssm_norm_w, attn_sinks, ln1_g, ln1_b, _from_devices(conv_all[1], "cols"), ffn_conv_b, ln2_g, ln2_b)

    early_names = LATE + ("ssm_conv_w", "ffn_conv_w")
    early = lambda g: [_by_device(g[n], SHARDED[n]).astype(BF16 if n in LATE else F32) for n in early_names]
    sq, dproj, w, g, landed = _local_step(x[0], loss_target[0], {"in_main": main, "in_tail": tail}, p,
                                          (late_shards, late), early)
    landed = dict(zip(early_names, landed))
    dx, landed["w_in"] = _grad_x(dproj, w, exchange=[_w_in_grad_by_device(g.pop("in_main"), g.pop("in_tail"))])
    loss = (0.5 / D_MODEL) * lax.psum(sq[0, 0], ("x", "y", "c"))
    grads, deltas, new_m, new_v = {}, {}, {}, {}
    for n in SHARDED:
        outs = _adamw(landed[n], shard2d(W[n]), shard2d(M[n]), shard2d(V[n]), "adamw_" + n)
        grads[n], deltas[n], new_m[n], new_v[n] = (a.reshape(W[n].shape) for a in outs)

    row = lambda a: a.reshape(1, -1)
    packed = jnp.concatenate([jnp.pad(row(g[n]), ((0, 0), (0, SMALL_COLS - g[n].size))) for n in SMALL]
                             + [jnp.zeros((SMALL_ROWS - len(SMALL), SMALL_COLS), F32)], axis=0)
    (small_all,) = _all_gather([packed], "gather_small_grads")
    outs = _small_update(small_all, *[[row(src[n]) for n in SMALL] for src in (W, M, V)])
    for i, n in enumerate(SMALL):
        grads[n], deltas[n], new_m[n], new_v[n] = (outs[j * len(SMALL) + i].reshape(W[n].shape) for j in range(4))

    return (loss, dx[None], *[grads[n] for n in WEIGHTS], *[deltas[n] for n in WEIGHTS],
            *[new_m[n] for n in WEIGHTS], *[new_v[n] for n in WEIGHTS])
```

```python
import math

import jax
import jax.numpy as jnp
from jax import lax
from jax.experimental import pallas as pl
from jax.experimental.pallas import tpu as pltpu

F32, BF16 = jnp.float32, jnp.bfloat16
HIGHEST = lax.Precision.HIGHEST
MESH_ID = pl.DeviceIdType.MESH

N_DEV = 8
D_MODEL = 1024
SSM_INNER = 2048
SSM_HEADS = 32
SSM_HEADDIM = 64
SSMD = SSM_HEADDIM
SSM_GROUPS = 4
SSM_GROUP_COLS = SSM_INNER // SSM_GROUPS
SSM_STATE = 128
SSM_CONV = 4
CHUNK = 128
XBC_COLS = SSM_INNER + 2 * SSM_GROUPS * SSM_STATE
B_OFF = SSM_INNER
C_OFF = SSM_INNER + SSM_GROUPS * SSM_STATE
ATTN_HEADS = 16
ATTN_KV = 2
ATTN_GROUP = 8
HEADDIM = 64
WINDOW = 128
REL_BUCKETS = 32
REL_MAX_DIST = 128
D_FF = 2816
FFN_CONV = 3
ALPHA = 2.0 ** 0.25
LN_EPS = 1e-5
RMS_EPS = 1e-5
IN_COLS = 8480
Z_OFF, XBC_OFF, Q_OFF, GATE_OFF, MAIN_COLS = 0, 2048, 5120, 6144, 8192
K_OFF, V_OFF, DT_OFF, TAIL_COLS = 0, 128, 256, 384
O_Z, O_XBC, O_DT, O_Q, O_K, O_V, O_GATE = 0, 2048, 5120, 5152, 6176, 6304, 6432

ADAM_LR, ADAM_B1, ADAM_B2, ADAM_EPS, ADAM_WD, ADAM_STEP = 0.001, 0.9, 0.999, 1e-08, 0.01, 10
NEG = -1e30
HALO = 8
VMEM_LIMIT = 56 * 1024 * 1024


def _cp(sem):
    return pltpu.CompilerParams(dimension_semantics=sem, vmem_limit_bytes=VMEM_LIMIT)


def _const_spec(shape):
    nd = len(shape)
    return pl.BlockSpec(shape, lambda *_: (0,) * nd)


def _resident_spec(shape):
    nd = len(shape)
    return pl.BlockSpec(shape, lambda *_: (0,) * nd, pipeline_mode=pl.Buffered(1))


def _sigmoid(x):
    return 0.5 * jnp.tanh(0.5 * x) + 0.5


def _softplus(x):
    return jnp.maximum(x, 0.0) + jnp.log1p(jnp.exp(-jnp.abs(x)))


def _dot(a, b, dims=(((1,), (0,)), ((), ())), precision=None):
    return lax.dot_general(a, b, dims, preferred_element_type=F32, precision=precision)


NN = (((1,), (0,)), ((), ()))
NT = (((1,), (1,)), ((), ()))
TN = (((0,), (0,)), ((), ()))


def _mesh_pos():
    return lax.axis_index("x"), lax.axis_index("y"), lax.axis_index("c")


PEERS = N_DEV - 1


def _exchange_phases(in_refs, out_refs, send_sems, recv_sems, local_sems):
    def copies():
        x, y, c = _mesh_pos()
        me = 4 * x + 2 * y + c
        cps = []
        for b, (in_ref, out_ref) in enumerate(zip(in_refs, out_refs)):
            cps.append(pltpu.make_async_copy(in_ref.at[me], out_ref.at[me], local_sems.at[b]))
            for r in range(1, N_DEV):
                px = 1 - x if r & 4 else x
                py = 1 - y if r & 2 else y
                pc = 1 - c if r & 1 else c
                cps.append(pltpu.make_async_remote_copy(
                    src_ref=in_ref.at[4 * px + 2 * py + pc], dst_ref=out_ref.at[me],
                    send_sem=send_sems.at[b * PEERS + r - 1], recv_sem=recv_sems.at[b * PEERS + r - 1],
                    device_id=(px, py, pc), device_id_type=MESH_ID))
        return cps

    def start():
        for cp in copies():
            cp.start()

    def finish():
        for cp in copies():
            cp.wait()

    return [start, finish]


def _gather_phases(x_refs, out_refs, send_sems, recv_sems, local_sems):
    def parts(which):
        x, y, c = _mesh_pos()
        me, sibling = (x, y, c), (x, y, 1 - c)
        chips = [(1 - x, y), (x, 1 - y), (1 - x, 1 - y)]
        found = []
        for b, (x_ref, out_ref) in enumerate(zip(x_refs, out_refs)):
            def slot(px, py, pc):
                return out_ref.at[4 * px + 2 * py + pc]

            def copy(k, block, to, src=None):
                return pltpu.make_async_remote_copy(
                    src_ref=slot(*block) if src is None else src, dst_ref=slot(*block),
                    send_sem=send_sems.at[b * PEERS + k], recv_sem=recv_sems.at[b * PEERS + k],
                    device_id=to, device_id_type=MESH_ID)

            if which == "mine":
                found.append(pltpu.make_async_copy(x_ref, slot(*me), local_sems.at[b]))
            elif which == "first":
                found.append(copy(0, me, sibling, src=x_ref))
                found += [copy(1 + j, me, (*chip, c), src=x_ref) for j, chip in enumerate(chips)]
            elif which == "passed":
                found += [copy(4 + j, (*chip, c), sibling) for j, chip in enumerate(chips)]
            elif which == "arrived":
                found += [copy(1 + j, (*chip, c), me) for j, chip in enumerate(chips)]
            else:
                found.append(copy(0, sibling, me))
                found += [copy(4 + j, (*chip, 1 - c), me) for j, chip in enumerate(chips)]
        return found

    def start():
        for cp in parts("mine") + parts("first"):
            cp.start()

    def forward():
        for a, p in zip(parts("arrived"), parts("passed")):
            a.wait_recv()
            p.start()

    def finish():
        for cp in parts("late"):
            cp.wait_recv()
        for cp in parts("first") + parts("passed"):
            cp.wait_send()
        for cp in parts("mine"):
            cp.wait()

    return [start, forward, finish]


COMM = {"exchange": _exchange_phases, "gather": _gather_phases}


def _comm_sems(nb):
    return [pltpu.SemaphoreType.DMA((nb * PEERS,)), pltpu.SemaphoreType.DMA((nb * PEERS,)), pltpu.SemaphoreType.DMA((nb,))]


def _matmul(a, b, mode, out_dtype, name, tm=512, tn=1024, tk=1024, addend=None, addend_scale=1.0, comm=None):
    bufs = [] if comm is None else list(comm[1])
    nb = len(bufs)
    halves = b.ndim == 3
    if mode == "nn":
        (M, K), (K2, N) = a.shape, b.shape
    elif mode == "nt":
        (M, K), (N, K2) = a.shape, b.shape
    elif halves:
        (K, M), (K2, N) = a.shape, (b.shape[1], 2 * b.shape[2])
    else:
        (K, M), (K2, N) = a.shape, b.shape
    assert K == K2, (a.shape, b.shape, mode)
    tm, tn, tk = min(tm, M), min(tn, N), min(tk, K)
    assert M % tm == 0 and N % tn == 0 and K % tk == 0, (M, N, K, tm, tn, tk)
    nk = K // tk
    dims = {"nn": NN, "nt": NT, "tn": TN}[mode]
    a_spec = pl.BlockSpec((tk, tm), lambda i, j, k: (k, i)) if mode == "tn" else pl.BlockSpec((tm, tk), lambda i, j, k: (i, k))
    b_spec = pl.BlockSpec((tn, tk), lambda i, j, k: (j, k)) if mode == "nt" else pl.BlockSpec((tk, tn), lambda i, j, k: (k, j))
    o_spec = pl.BlockSpec((tm, tn), lambda i, j, k: (i, j))

    ni, nj = M // tm, N // tn
    if halves:
        assert mode == "tn" and nj % 2 == 0
        b_spec = pl.BlockSpec((None, tk, tn), lambda i, j, k: (j // (nj // 2), k, j % (nj // 2)))

    def body(*refs):
        refs = list(refs)
        a_ref, b_ref = refs[:2]
        c_ref = refs[2] if addend is not None else None
        n_in = 2 + (addend is not None) + nb
        o_ref = refs[n_in]
        acc = refs[n_in + 1 + nb] if nk > 1 else None
        i, j, k = pl.program_id(0), pl.program_id(1), pl.program_id(2)
        step = (i * nj + j) * nk + k
        if comm is not None:
            phases = COMM[comm[0]](refs[n_in - nb:n_in], refs[n_in + 1:n_in + 1 + nb], *refs[n_in + 1 + nb + (nk > 1):])
            at = [(ni * nj * nk - 1) * p // (len(phases) - 1) for p in range(len(phases))]
            for when, phase in zip(at[:-1], phases[:-1]):
                pl.when(step == when)(phase)

        d = _dot(a_ref[...].astype(BF16), b_ref[...].astype(BF16), dims)

        def finish(r):
            if addend is not None:
                r = r + addend_scale * c_ref[...].astype(F32)
            o_ref[...] = r.astype(out_dtype)

        if nk == 1:
            finish(d)
        else:
            @pl.when(k == 0)
            def _():
                acc[...] = d

            @pl.when(jnp.logical_and(k > 0, k < nk - 1))
            def _():
                acc[...] += d

            @pl.when(k == nk - 1)
            def _():
                finish(acc[...] + d)

        if comm is not None:
            pl.when(step == at[-1])(phases[-1])

    in_specs = [a_spec, b_spec] + ([o_spec] if addend is not None else [])
    args = (a, b) + ((addend,) if addend is not None else ())
    out_specs, out_shape = o_spec, jax.ShapeDtypeStruct((M, N), out_dtype)
    scratch = [pltpu.VMEM((tm, tn), F32)] if nk > 1 else []
    sem = ("parallel", "parallel", "arbitrary")
    if comm is not None:
        any_spec = pl.BlockSpec(memory_space=pl.ANY)
        in_specs, args = in_specs + [any_spec] * nb, args + tuple(bufs)
        landed = [x.shape if comm[0] == "exchange" else (N_DEV,) + x.shape for x in bufs]
        out_specs = [o_spec] + [any_spec] * nb
        out_shape = [out_shape] + [jax.ShapeDtypeStruct(s, x.dtype) for s, x in zip(landed, bufs)]
        scratch += _comm_sems(nb)
        sem = ("arbitrary", "arbitrary", "arbitrary")
    return pl.pallas_call(
        body, name=name, grid=(ni, nj, nk), in_specs=in_specs, out_specs=out_specs, out_shape=out_shape,
        scratch_shapes=scratch, compiler_params=_cp(sem))(*args)


def _taps(w_ref, K, tc):
    return [jnp.broadcast_to(w_ref[k:k + 1, :], (HALO, tc)) for k in range(K)]


def _conv_silu_fwd(pre, pre_col_off, C, w, b, K, name, tr=1024, tc=512):
    T = pre.shape[0]
    tr, tc = min(tr, T), min(tc, C)
    assert T % tr == 0 and C % tc == 0 and pre_col_off % tc == 0
    joff = pre_col_off // tc
    hb = tr // HALO

    def body(x_ref, xp_ref, w_ref, b_ref, o_ref, d_ref, head):
        i = pl.program_id(1)
        head[0:HALO, :] = jnp.where(i > 0, xp_ref[...], 0.0)
        head[HALO:, :] = x_ref[0:HALO, :]
        wk = _taps(w_ref, K, tc)
        bias = jnp.broadcast_to(b_ref[...], (HALO, tc))
        for r in range(tr // HALO):
            lo = r * HALO
            co = bias + wk[K - 1] * x_ref[lo:lo + HALO, :]
            for k in range(K - 1):
                s = K - 1 - k
                co = co + wk[k] * (head[HALO - s:2 * HALO - s, :] if r == 0 else x_ref[lo - s:lo + HALO - s, :])
            sg = _sigmoid(co)
            y = co * sg
            o_ref[lo:lo + HALO, :] = y
            d_ref[lo:lo + HALO, :] = sg + y * (1.0 - sg)

    out = pl.BlockSpec((tr, tc), lambda j, i: (i, j))
    return pl.pallas_call(
        body, name=name, grid=(C // tc, T // tr),
        in_specs=[pl.BlockSpec((tr, tc), lambda j, i: (i, joff + j)),
                  pl.BlockSpec((HALO, tc), lambda j, i: (jnp.maximum(i * hb - 1, 0), joff + j)),
                  pl.BlockSpec((K, tc), lambda j, i: (0, j)),
                  pl.BlockSpec((1, tc), lambda j, i: (0, j))],
        out_specs=[out, out], out_shape=[jax.ShapeDtypeStruct((T, C), F32)] * 2,
        scratch_shapes=[pltpu.VMEM((2 * HALO, tc), F32)],
        compiler_params=_cp(("parallel", "arbitrary")))(pre, pre, w, b)


def _conv_bwd(dout, pre, pre_col_off, C, w, K, dst, dst_col_off, name, tr=1024, tc=512):
    T = pre.shape[0]
    tr, tc = min(tr, T), min(tc, C)
    assert T % tr == 0 and C % tc == 0 and pre_col_off % tc == 0 and dst_col_off % tc == 0
    joff, doff = pre_col_off // tc, dst_col_off // tc
    hb = tr // HALO
    nt = T // tr
    n = tr // HALO
    last_hblock = T // HALO - 1

    def body(g_ref, gn_ref, x_ref, w_ref, *rest):
        o_ref, dw_ref, db_ref, edge = rest[-4:]
        i = pl.program_id(1)
        wk = _taps(w_ref, K, tc)
        edge[0:HALO, :] = g_ref[tr - HALO:tr, :]
        edge[HALO:, :] = jnp.where(i < nt - 1, gn_ref[...], 0.0)
        acc_w = [jnp.zeros((HALO, tc), F32) for _ in range(K)]
        acc_b = jnp.zeros((HALO, tc), F32)
        for r in range(n):
            lo = r * HALO
            x = x_ref[lo:lo + HALO, :]
            dpre = None
            for s in range(K):
                gs = edge[s:HALO + s, :] if (r == n - 1 and s > 0) else g_ref[lo + s:lo + HALO + s, :]
                dpre = wk[K - 1 - s] * gs if dpre is None else dpre + wk[K - 1 - s] * gs
                acc_w[K - 1 - s] = acc_w[K - 1 - s] + gs * x
                if s == 0:
                    acc_b = acc_b + gs
            o_ref[lo:lo + HALO, :] = dpre.astype(o_ref.dtype)

        @pl.when(i == 0)
        def _():
            dw_ref[...] = jnp.zeros_like(dw_ref)
            db_ref[...] = jnp.zeros_like(db_ref)

        db_ref[...] += jnp.sum(acc_b, axis=0, keepdims=True)
        dw_ref[...] += jnp.concatenate([jnp.sum(a, axis=0, keepdims=True) for a in acc_w], axis=0)

    tile = lambda off: pl.BlockSpec((tr, tc), lambda j, i: (i, off + j))
    in_specs = [tile(0), pl.BlockSpec((HALO, tc), lambda j, i: (jnp.minimum((i + 1) * hb, last_hblock), j)),
                tile(joff), pl.BlockSpec((K, tc), lambda j, i: (0, j))]
    args = (dout, dout, pre, w)
    if isinstance(dst, jax.ShapeDtypeStruct):
        aliases = {}
    else:
        in_specs.append(pl.BlockSpec(memory_space=pl.ANY))
        args += (dst,)
        aliases = {4: 0}
    return pl.pallas_call(
        body, name=name, grid=(C // tc, nt), in_specs=in_specs,
        out_specs=[tile(doff), pl.BlockSpec((K, tc), lambda j, i: (0, j)), pl.BlockSpec((1, tc), lambda j, i: (0, j))],
        out_shape=[jax.ShapeDtypeStruct(dst.shape, dst.dtype), jax.ShapeDtypeStruct((K, C), F32),
                   jax.ShapeDtypeStruct((1, C), F32)],
        scratch_shapes=[pltpu.VMEM((2 * HALO, tc), F32)],
        input_output_aliases=aliases,
        compiler_params=_cp(("parallel", "arbitrary")))(*args)


PAIR = 2 * SSMD
PAIRS_PER_GROUP = SSM_GROUP_COLS // PAIR


def _dot3(x, onehot):
    h1 = x.astype(BF16)
    r = x - h1.astype(F32)
    h2 = r.astype(BF16)
    h3 = (r - h2.astype(F32)).astype(BF16)
    return _dot(h1, onehot) + _dot(h2, onehot) + _dot(h3, onehot)


def _chunk_rows(dt_raw, dtb_col, alog_col):
    row = lax.broadcasted_iota(jnp.int32, (CHUNK, CHUNK), 0)
    col = lax.broadcasted_iota(jnp.int32, (CHUNK, CHUNK), 1)
    dt_rawT = dt_raw.T
    dtT = _softplus(dt_rawT + dtb_col)
    a_col = -jnp.exp(alog_col)
    acsT = _dot3(dtT * a_col, (row <= col).astype(BF16))
    return dt_rawT, dtT, a_col, acsT, row, col


def _block_diag(x, left):
    return jnp.concatenate([jnp.where(left, x, 0.0), jnp.where(left, 0.0, x)], axis=0).astype(BF16)


def _lane_bcast(v, h):
    return jnp.broadcast_to(v[:, h:h + 1], (CHUNK, CHUNK))


def _ssd_fwd(xbc, proj_main, proj_tail, dtb_col, alog_col, d_exp, norm_w):
    T = xbc.shape[0]
    nc = T // CHUNK

    def body(xbc_ref, dt_ref, z_ref, dtb_ref, alog_ref, d_ref, nw_ref, y_ref, ypre_ref, hs_ref, H):
        c = pl.program_id(0)

        @pl.when(c == 0)
        def _():
            H[...] = jnp.zeros_like(H)

        hs_ref[0] = H[...]
        _, dtT, _, acsT, row, col = _chunk_rows(dt_ref[:, 0:SSM_HEADS], dtb_ref[...], alog_ref[...])
        tril, left = row >= col, col < SSMD
        acs = acsT.T
        w = (dtT * jnp.exp(acsT[:, CHUNK - 1:CHUNK] - acsT)).T
        cd = jnp.exp(acs[CHUNK - 1:CHUNK, :])
        for g in range(SSM_GROUPS):
            gs = slice(g * SSM_GROUP_COLS, (g + 1) * SSM_GROUP_COLS)
            Bb = xbc_ref[:, B_OFF + g * SSM_STATE:B_OFF + (g + 1) * SSM_STATE].astype(BF16)
            Cb = xbc_ref[:, C_OFF + g * SSM_STATE:C_OFF + (g + 1) * SSM_STATE].astype(BF16)
            Hg = H[:, gs]
            CH = _dot(Cb, Hg.astype(BF16))
            CB = _dot(Cb, Bb, NT)
            ys, xws = [], []
            for kk in range(PAIRS_PER_GROUP):
                k = g * PAIRS_PER_GROUP + kk
                xs_p = xbc_ref[:, k * PAIR:(k + 1) * PAIR]
                mps, ecols, wcols = [], [], []
                for j in range(2):
                    h = 2 * k + j
                    colb = _lane_bcast(acs, h)
                    L = jnp.exp(jnp.where(tril, colb - acsT[h:h + 1, :], -jnp.inf))
                    mps.append((CB * L * dtT[h:h + 1, :]).astype(BF16))
                    ecols.append(jnp.exp(colb))
                    wcols.append(_lane_bcast(w, h))
                yd = _dot(jnp.concatenate(mps, axis=1), _block_diag(xs_p, left))
                ys.append(yd + CH[:, kk * PAIR:(kk + 1) * PAIR] * jnp.where(left, ecols[0], ecols[1]))
                xws.append((xs_p * jnp.where(left, wcols[0], wcols[1])).astype(BF16))
            cd_e = jnp.concatenate([jnp.broadcast_to(cd[:, g * 8 + e:g * 8 + e + 1], (1, SSMD)) for e in range(8)], axis=1)
            H[:, gs] = Hg * cd_e + _dot(Bb, jnp.concatenate(xws, axis=1), TN)
            ypre = jnp.concatenate(ys, axis=1) + xbc_ref[:, gs] * d_ref[:, gs]
            ypre_ref[:, gs] = ypre
            z = z_ref[:, gs]
            yg = ypre * (z * _sigmoid(z))
            r = lax.rsqrt(jnp.mean(yg * yg, axis=1, keepdims=True) + RMS_EPS)
            y_ref[:, gs] = (yg * r * nw_ref[:, gs]).astype(BF16)

    vec = lambda n: _const_spec((1, n))
    colv = _const_spec((SSM_HEADS, 1))
    return pl.pallas_call(
        body, name="ssd_fwd", grid=(nc,),
        in_specs=[pl.BlockSpec((CHUNK, XBC_COLS), lambda c: (c, 0)),
                  pl.BlockSpec((CHUNK, 128), lambda c: (c, DT_OFF // 128)),
                  pl.BlockSpec((CHUNK, SSM_INNER), lambda c: (c, Z_OFF // SSM_INNER)),
                  colv, colv, vec(SSM_INNER), vec(SSM_INNER)],
        out_specs=[pl.BlockSpec((CHUNK, SSM_INNER), lambda c: (c, 0)),
                   pl.BlockSpec((CHUNK, SSM_INNER), lambda c: (c, 0)),
                   pl.BlockSpec((1, SSM_STATE, SSM_INNER), lambda c: (c, 0, 0))],
        out_shape=[jax.ShapeDtypeStruct((T, SSM_INNER), BF16), jax.ShapeDtypeStruct((T, SSM_INNER), F32),
                   jax.ShapeDtypeStruct((nc, SSM_STATE, SSM_INNER), F32)],
        scratch_shapes=[pltpu.VMEM((SSM_STATE, SSM_INNER), F32)],
        compiler_params=_cp(("arbitrary",)))(xbc, proj_tail, proj_main, dtb_col, alog_col, d_exp, norm_w)


def _ssd_bwd(dyo, ypre, xbc, dsil, hs, proj_main, proj_tail, dtb_col, alog_col, d_exp, norm_w, ehead_t, dmain, dtail):
    T = xbc.shape[0]
    nc = T // CHUNK

    def body(dyo_ref, ypre_ref, xbc_ref, dsil_ref, hs_ref, dt_ref, z_ref, dtb_ref, alog_ref, d_ref, nw_ref, eh_ref,
             dmain_in, dtail_in, dz_ref, ddt_ref, dxbc_ref, dnw_ref, dd_ref, dalog_ref, ddtb_ref, G):
        del dmain_in, dtail_in
        c = pl.program_id(0)

        @pl.when(c == 0)
        def _():
            G[...] = jnp.zeros_like(G)
            dnw_ref[...] = jnp.zeros_like(dnw_ref)
            dd_ref[...] = jnp.zeros_like(dd_ref)
            dalog_ref[...] = jnp.zeros_like(dalog_ref)
            ddtb_ref[...] = jnp.zeros_like(ddtb_ref)

        dt_rawT, dtT, a_col, acsT, row, col = _chunk_rows(dt_ref[:, 0:SSM_HEADS], dtb_ref[...], alog_ref[...])
        tril, triu, left = row >= col, col >= row, col < SSMD
        acs = acsT.T
        dt = dtT.T
        lastT = acsT[:, CHUNK - 1:CHUNK]
        dstT = jnp.exp(lastT - acsT)
        wT = dtT * dstT
        cd = jnp.exp(acs[CHUNK - 1:CHUNK, :])
        ddt_rows, rs_rows, deo_rows, dw_rows = [], [], [], []
        dd_cols, gh_cols, dnw_cols = [], [], []
        for g in range(SSM_GROUPS):
            gs = slice(g * SSM_GROUP_COLS, (g + 1) * SSM_GROUP_COLS)
            z = z_ref[:, gs]
            sz = _sigmoid(z)
            silu_z = z * sz
            ypre = ypre_ref[:, gs]
            yg = ypre * silu_z
            r = lax.rsqrt(jnp.mean(yg * yg, axis=1, keepdims=True) + RMS_EPS)
            ygn = yg * r
            dyo = dyo_ref[:, gs]
            dyn = dyo * nw_ref[:, gs]
            dnw_cols.append(jnp.sum(dyo * ygn, axis=0, keepdims=True))
            dyg = r * (dyn - ygn * jnp.mean(dyn * ygn, axis=1, keepdims=True))
            dz_ref[:, gs] = (dyg * ypre * (sz * (1.0 + z * (1.0 - sz)))).astype(dz_ref.dtype)
            dY = dyg * silu_z
            xs = xbc_ref[:, gs]
            dd_cols.append(jnp.sum(dY * xs, axis=0, keepdims=True))
            Bf = xbc_ref[:, B_OFF + g * SSM_STATE:B_OFF + (g + 1) * SSM_STATE]
            Cf = xbc_ref[:, C_OFF + g * SSM_STATE:C_OFF + (g + 1) * SSM_STATE]
            Bb, Cb = Bf.astype(BF16), Cf.astype(BF16)
            BT, CT = Bf.T, Cf.T
            CB = _dot(Cb, Bb, NT)
            CBT = _dot(Bb, Cb, NT)
            Hg = hs_ref[0, :, gs]
            Gg = G[:, gs]
            gh_cols.append(jnp.sum(Gg * Hg, axis=0, keepdims=True))
            dCB = jnp.zeros((CHUNK, CHUNK), F32)
            dxs_d, dyes, xws, wsels = [], [], [], []
            for kk in range(PAIRS_PER_GROUP):
                k = g * PAIRS_PER_GROUP + kk
                ps = slice(kk * PAIR, (kk + 1) * PAIR)
                xs_p, dY_p = xs[:, ps], dY[:, ps]
                Ls, LTs, dtcols, ecols, wcols = [], [], [], [], []
                for j in range(2):
                    h = 2 * k + j
                    colb = _lane_bcast(acs, h)
                    seg = colb - acsT[h:h + 1, :]
                    Ls.append(jnp.exp(jnp.where(tril, seg, -jnp.inf)))
                    LTs.append(jnp.exp(jnp.where(triu, -seg, -jnp.inf)))
                    dtcol = _lane_bcast(dt, h)
                    dtcols.append(dtcol)
                    ecols.append(jnp.exp(colb))
                    wcols.append(dtcol * jnp.exp(acs[CHUNK - 1:CHUNK, h:h + 1] - colb))
                wsel = jnp.where(left, wcols[0], wcols[1])
                dYe_p = dY_p * jnp.where(left, ecols[0], ecols[1])
                bdx = _block_diag(xs_p, left)
                bddy = _block_diag(dY_p, left)
                dMx2 = _dot(dY_p.astype(BF16), bdx, NT)
                dMxT2 = _dot(xs_p.astype(BF16), bddy, NT)
                Q1 = _dot(Hg[:, ps].astype(BF16), _block_diag(dYe_p, left), NT)
                Q2 = _dot(Gg[:, ps].astype(BF16), bdx, NT)
                mts = []
                for j in range(2):
                    h = 2 * k + j
                    js = slice(j * CHUNK, (j + 1) * CHUNK)
                    dMx = dMx2[:, js]
                    A = CB * Ls[j]
                    AT = CBT * LTs[j]
                    ddt_rows.append(jnp.sum(A * dMx, axis=0, keepdims=True))
                    ATd = AT * dtcols[j]
                    rs_rows.append(jnp.sum(ATd * dMxT2[:, js], axis=0, keepdims=True))
                    dCB = dCB + dMx * Ls[j] * dtT[h:h + 1, :]
                    mts.append(ATd.astype(BF16))
                    deo_rows.append(jnp.sum(CT * Q1[:, js], axis=0, keepdims=True))
                    dw_rows.append(jnp.sum(BT * Q2[:, js], axis=0, keepdims=True))
                dxs_d.append(_dot(jnp.concatenate(mts, axis=1), bddy))
                dyes.append(dYe_p.astype(BF16))
                xws.append((xs_p * wsel).astype(BF16))
                wsels.append(wsel)
            dYe_g = jnp.concatenate(dyes, axis=1)
            xw_g = jnp.concatenate(xws, axis=1)
            Hgb, Ggb, dCBb = Hg.astype(BF16), Gg.astype(BF16), dCB.astype(BF16)
            cs = slice(C_OFF + g * SSM_STATE, C_OFF + (g + 1) * SSM_STATE)
            bs = slice(B_OFF + g * SSM_STATE, B_OFF + (g + 1) * SSM_STATE)
            dxbc_ref[:, cs] = (_dot(dYe_g, Hgb, NT) + _dot(dCBb, Bb)) * dsil_ref[:, cs]
            dxbc_ref[:, bs] = (_dot(xw_g, Ggb, NT) + _dot(dCBb, Cb, TN)) * dsil_ref[:, bs]
            BG = _dot(Bb, Ggb)
            dxbc_ref[:, gs] = (jnp.concatenate(dxs_d, axis=1) + BG * jnp.concatenate(wsels, axis=1)
                               + dY * d_ref[:, gs]) * dsil_ref[:, gs]
            cd_e = jnp.concatenate([jnp.broadcast_to(cd[:, g * 8 + e:g * 8 + e + 1], (1, SSMD)) for e in range(8)], axis=1)
            G[:, gs] = Gg * cd_e + _dot(Cb, dYe_g, TN)
        dnw_ref[...] += jnp.concatenate(dnw_cols, axis=1)
        eh = eh_ref[...]
        dd_ref[...] += jnp.sum(eh * jnp.concatenate(dd_cols, axis=1), axis=1, keepdims=True)
        dcd = jnp.sum(eh * jnp.concatenate(gh_cols, axis=1), axis=1, keepdims=True)
        DDT = jnp.concatenate(ddt_rows, axis=0)
        DW = jnp.concatenate(dw_rows, axis=0)
        DWw = DW * wT
        dacsT = jnp.concatenate(rs_rows, axis=0) - DDT * dtT + jnp.concatenate(deo_rows, axis=0) - DWw
        end = jnp.sum(DWw, axis=1, keepdims=True) + dcd * jnp.exp(lastT)
        lane = lax.broadcasted_iota(jnp.int32, (SSM_HEADS, CHUNK), 1)
        dacsT = dacsT + jnp.where(lane == CHUNK - 1, end, 0.0)
        dadtT = _dot3(dacsT, tril.astype(BF16))
        ddtT = dadtT * a_col + DDT + DW * dstT
        dalog_ref[...] += jnp.sum(dadtT * dtT, axis=1, keepdims=True) * a_col
        ddt_rawT = ddtT * _sigmoid(dt_rawT + dtb_ref[...])
        ddtb_ref[...] += jnp.sum(ddt_rawT, axis=1, keepdims=True)
        ddt_ref[...] = jnp.concatenate([ddt_rawT.T, jnp.zeros((CHUNK, 128 - SSM_HEADS), F32)], axis=1).astype(ddt_ref.dtype)

    rev = lambda c: nc - 1 - c
    vec = lambda n: _const_spec((1, n))
    colv = _const_spec((SSM_HEADS, 1))
    any_spec = pl.BlockSpec(memory_space=pl.ANY)
    return pl.pallas_call(
        body, name="ssd_bwd", grid=(nc,),
        in_specs=[pl.BlockSpec((CHUNK, SSM_INNER), lambda c: (rev(c), 0)),
                  pl.BlockSpec((CHUNK, SSM_INNER), lambda c: (rev(c), 0)),
                  pl.BlockSpec((CHUNK, XBC_COLS), lambda c: (rev(c), 0)),
                  pl.BlockSpec((CHUNK, XBC_COLS), lambda c: (rev(c), 0)),
                  pl.BlockSpec((1, SSM_STATE, SSM_INNER), lambda c: (rev(c), 0, 0)),
                  pl.BlockSpec((CHUNK, 128), lambda c: (rev(c), DT_OFF // 128)),
                  pl.BlockSpec((CHUNK, SSM_INNER), lambda c: (rev(c), Z_OFF // SSM_INNER)),
                  colv, colv, vec(SSM_INNER), vec(SSM_INNER), _const_spec((SSM_HEADS, SSM_INNER)), any_spec, any_spec],
        out_specs=[pl.BlockSpec((CHUNK, SSM_INNER), lambda c: (rev(c), Z_OFF // SSM_INNER)),
                   pl.BlockSpec((CHUNK, 128), lambda c: (rev(c), DT_OFF // 128)),
                   pl.BlockSpec((CHUNK, XBC_COLS), lambda c: (rev(c), 0)),
                   vec(SSM_INNER), colv, colv, colv],
        out_shape=[jax.ShapeDtypeStruct(dmain.shape, dmain.dtype), jax.ShapeDtypeStruct(dtail.shape, dtail.dtype),
                   jax.ShapeDtypeStruct((T, XBC_COLS), F32), jax.ShapeDtypeStruct((1, SSM_INNER), F32),
                   jax.ShapeDtypeStruct((SSM_HEADS, 1), F32), jax.ShapeDtypeStruct((SSM_HEADS, 1), F32),
                   jax.ShapeDtypeStruct((SSM_HEADS, 1), F32)],
        scratch_shapes=[pltpu.VMEM((SSM_STATE, SSM_INNER), F32)],
        input_output_aliases={12: 0, 13: 1},
        compiler_params=_cp(("arbitrary",)))(dyo, ypre, xbc, dsil, hs, proj_tail, proj_main, dtb_col, alog_col, d_exp,
                                             norm_w, ehead_t, dmain, dtail)


def _rel_bucket(rel):
    n = jnp.maximum(rel, 0)
    max_exact = REL_BUCKETS // 2
    nf = jnp.maximum(n, 1).astype(F32)
    large = max_exact + (jnp.log(nf / max_exact) / math.log(REL_MAX_DIST / max_exact)
                         * (REL_BUCKETS - max_exact)).astype(jnp.int32)
    large = jnp.minimum(large, REL_BUCKETS - 1)
    return jnp.where(n < max_exact, n, large)


def _band_geometry():
    qi = jnp.arange(WINDOW)[:, None] + WINDOW
    kj = jnp.arange(2 * WINDOW)[None, :]
    rel = qi - kj
    return _rel_bucket(rel), (rel >= 0) & (rel < WINDOW)


def _attn_logits(kband, qh, bias_h, first):
    s = _dot(kband, qh, NT) * (HEADDIM ** -0.5) + bias_h
    rowk = lax.broadcasted_iota(jnp.int32, (2 * WINDOW, WINDOW), 0)
    return jnp.where(jnp.logical_and(first, rowk < WINDOW), NEG, s)


def _attn_fwd(proj_main, proj_tail, bias_tbl, sinks):
    T = proj_main.shape[0]
    nb = T // WINDOW

    def body(q_ref, kv_ref, kvp_ref, bias_ref, sink_ref, o_ref, lse_ref):
        i = pl.program_id(0)
        first = i == 0
        outs, lses = [], []
        for kvh in range(ATTN_KV):
            ks = slice(K_OFF + kvh * HEADDIM, K_OFF + (kvh + 1) * HEADDIM)
            vs = slice(V_OFF + kvh * HEADDIM, V_OFF + (kvh + 1) * HEADDIM)
            kband = jnp.concatenate([kvp_ref[:, ks], kv_ref[:, ks]], axis=0).astype(BF16)
            vband = jnp.concatenate([kvp_ref[:, vs], kv_ref[:, vs]], axis=0).astype(BF16)
            heads = range(kvh * ATTN_GROUP, (kvh + 1) * ATTN_GROUP)
            logits = [_attn_logits(kband, q_ref[:, h * HEADDIM:(h + 1) * HEADDIM].astype(BF16), bias_ref[h], first)
                      for h in heads]
            probs = []
            for h, s in zip(heads, logits):
                sink = sink_ref[:, h:h + 1]
                m = jnp.maximum(jnp.max(s, axis=0, keepdims=True), sink)
                p = jnp.exp(s - m)
                den = jnp.sum(p, axis=0, keepdims=True) + jnp.exp(sink - m)
                probs.append((p * (1.0 / den)).astype(BF16))
                lses.append(m + jnp.log(den))
            outs += [_dot(pt, vband, TN) for pt in probs]
        o_ref[...] = jnp.concatenate(outs, axis=1).astype(BF16)
        lse_ref[...] = jnp.concatenate(lses, axis=0)

    return pl.pallas_call(
        body, name="attn_fwd", grid=(nb,),
        in_specs=[pl.BlockSpec((WINDOW, D_MODEL), lambda i: (i, Q_OFF // D_MODEL)),
                  pl.BlockSpec((WINDOW, 256), lambda i: (i, 0)),
                  pl.BlockSpec((WINDOW, 256), lambda i: (jnp.maximum(i - 1, 0), 0)),
                  _const_spec((ATTN_HEADS, 2 * WINDOW, WINDOW)), _const_spec((1, ATTN_HEADS))],
        out_specs=[pl.BlockSpec((WINDOW, D_MODEL), lambda i: (i, 0)),
                   pl.BlockSpec((ATTN_HEADS, WINDOW), lambda i: (0, i))],
        out_shape=[jax.ShapeDtypeStruct((T, D_MODEL), BF16), jax.ShapeDtypeStruct((ATTN_HEADS, T), F32)],
        compiler_params=_cp(("arbitrary",)))(proj_main, proj_tail, proj_tail, bias_tbl, sinks)


def _attn_bwd(dy, lse, proj_main, proj_tail, bias_tbl, sinks, dmain):
    T = proj_main.shape[0]
    nb = T // WINDOW

    def body(dy_ref, lse_ref, q_ref, kv_ref, kvp_ref, bias_ref, sink_ref, dmain_in,
             dq_ref, dkv_ref, dbias_ref, dsink_ref, carry):
        del dmain_in
        i = pl.program_id(0)
        first = i == 0

        @pl.when(first)
        def _():
            carry[...] = jnp.zeros_like(carry)
            dbias_ref[...] = jnp.zeros_like(dbias_ref)
            dsink_ref[...] = jnp.zeros_like(dsink_ref)

        @pl.when(i < nb)
        def _():
            scale = HEADDIM ** -0.5
            dqs, dsinks, dks, dvs = [], [], [], []
            for kvh in range(ATTN_KV):
                ks = slice(K_OFF + kvh * HEADDIM, K_OFF + (kvh + 1) * HEADDIM)
                vs = slice(V_OFF + kvh * HEADDIM, V_OFF + (kvh + 1) * HEADDIM)
                kband = jnp.concatenate([kvp_ref[:, ks], kv_ref[:, ks]], axis=0).astype(BF16)
                vband = jnp.concatenate([kvp_ref[:, vs], kv_ref[:, vs]], axis=0).astype(BF16)
                heads = range(kvh * ATTN_GROUP, (kvh + 1) * ATTN_GROUP)
                qs = [q_ref[:, h * HEADDIM:(h + 1) * HEADDIM].astype(BF16) for h in heads]
                dos = [dy_ref[:, h * HEADDIM:(h + 1) * HEADDIM] for h in heads]
                logits = [_attn_logits(kband, qh, bias_ref[h], first) for h, qh in zip(heads, qs)]
                dps = [_dot(vband, do, NT) for do in dos]
                pbs, dsbs = [], []
                for h, s, dp in zip(heads, logits, dps):
                    lse_h = lse_ref[h:h + 1, :]
                    p = jnp.exp(s - lse_h)
                    delta = jnp.sum(p * dp, axis=0, keepdims=True)
                    ds = p * (dp - delta)
                    psink = jnp.exp(sink_ref[:, h:h + 1] - lse_h)
                    dsinks.append(-jnp.sum(psink * delta, axis=1, keepdims=True))
                    dbias_ref[h] += ds
                    pbs.append(p.astype(BF16))
                    dsbs.append((ds * scale).astype(BF16))
                dqs += [_dot(dsb, kband, TN) for dsb in dsbs]
                dks.append(_dot(jnp.concatenate(dsbs, axis=1), jnp.concatenate(qs, axis=0)))
                dvs.append(_dot(jnp.concatenate(pbs, axis=1), jnp.concatenate(dos, axis=0)))
            dq_ref[...] = jnp.concatenate(dqs, axis=1).astype(dq_ref.dtype)
            dsink_ref[...] += jnp.concatenate(dsinks, axis=1)
            dkv = jnp.concatenate(dks + dvs, axis=1)
            dkv_ref[...] = (carry[...] + dkv[0:WINDOW, :]).astype(dkv_ref.dtype)
            carry[...] = dkv[WINDOW:, :]

        @pl.when(i == nb)
        def _():
            dkv_ref[...] = carry[...].astype(dkv_ref.dtype)

    cur = lambda i: jnp.minimum(i, nb - 1)
    return pl.pallas_call(
        body, name="attn_bwd", grid=(nb + 1,),
        in_specs=[pl.BlockSpec((WINDOW, D_MODEL), lambda i: (cur(i), 0)),
                  pl.BlockSpec((ATTN_HEADS, WINDOW), lambda i: (0, cur(i))),
                  pl.BlockSpec((WINDOW, D_MODEL), lambda i: (cur(i), Q_OFF // D_MODEL)),
                  pl.BlockSpec((WINDOW, 256), lambda i: (cur(i), 0)),
                  pl.BlockSpec((WINDOW, 256), lambda i: (jnp.maximum(cur(i) - 1, 0), 0)),
                  _const_spec((ATTN_HEADS, 2 * WINDOW, WINDOW)), _const_spec((1, ATTN_HEADS)),
                  pl.BlockSpec(memory_space=pl.ANY)],
        out_specs=[pl.BlockSpec((WINDOW, D_MODEL), lambda i: (cur(i), Q_OFF // D_MODEL)),
                   pl.BlockSpec((WINDOW, 256), lambda i: (jnp.maximum(i - 1, 0), 0)),
                   _const_spec((ATTN_HEADS, 2 * WINDOW, WINDOW)), _const_spec((1, ATTN_HEADS))],
        out_shape=[jax.ShapeDtypeStruct(dmain.shape, dmain.dtype), jax.ShapeDtypeStruct((T, TAIL_COLS), BF16),
                   jax.ShapeDtypeStruct((ATTN_HEADS, 2 * WINDOW, WINDOW), F32),
                   jax.ShapeDtypeStruct((1, ATTN_HEADS), F32)],
        scratch_shapes=[pltpu.VMEM((WINDOW, 256), F32)],
        input_output_aliases={7: 0},
        compiler_params=_cp(("arbitrary",)))(dy, lse, proj_main, proj_tail, proj_tail, bias_tbl, sinks, dmain)


def _bias_table(rel_bias_t, onehot_t, mask):
    def body(rb_ref, oh_ref, m_ref, o_ref):
        o_ref[...] = _dot3(rb_ref[...], oh_ref[...]) + m_ref[...]

    flat = pl.pallas_call(body, name="bias_table",
                          out_shape=jax.ShapeDtypeStruct((ATTN_HEADS, 2 * WINDOW * WINDOW), F32))(rel_bias_t, onehot_t, mask)
    return flat.reshape(ATTN_HEADS, 2 * WINDOW, WINDOW)


def _rel_bias_grad(dbias, onehot):
    def body(d_ref, oh_ref, o_ref):
        o_ref[...] = _dot(d_ref[...], oh_ref[...], NN, HIGHEST)

    return pl.pallas_call(body, name="rel_bias_grad",
                          out_shape=jax.ShapeDtypeStruct((ATTN_HEADS, REL_BUCKETS), F32))(dbias, onehot)


def _ln_fwd(r, g, b):
    mu = jnp.mean(r, axis=1, keepdims=True)
    xc = r - mu
    rstd = lax.rsqrt(jnp.mean(xc * xc, axis=1, keepdims=True) + LN_EPS)
    xhat = xc * rstd
    return xhat * g + b, xhat, rstd


def _ln_bwd(dy, xhat, rstd, g):
    dxh = dy * g
    return rstd * (dxh - jnp.mean(dxh, axis=1, keepdims=True) - xhat * jnp.mean(dxh * xhat, axis=1, keepdims=True))


def _merge_fwd(y_ssm, y_attn, proj_main, b_gate, w_bs, w_ba, tm=512):
    T = y_ssm.shape[0]

    def body(ys_ref, ya_ref, gs_ref, ga_ref, bg_ref, wbs_ref, wba_ref, m_ref, bs_ref, ba_ref):
        bs = _dot(ys_ref[...], wbs_ref[...])
        ba = _dot(ya_ref[...], wba_ref[...])
        g_s = _sigmoid(gs_ref[...] + bg_ref[:, 0:D_MODEL])
        g_a = _sigmoid(ga_ref[...] + bg_ref[:, D_MODEL:])
        m_ref[...] = (g_s * bs + g_a * ba).astype(BF16)
        bs_ref[...] = bs
        ba_ref[...] = ba

    row = lambda w, off=0: pl.BlockSpec((tm, w), lambda i: (i, off))
    return pl.pallas_call(
        body, name="merge_fwd", grid=(T // tm,),
        in_specs=[row(SSM_INNER), row(D_MODEL), row(D_MODEL, GATE_OFF // D_MODEL), row(D_MODEL, GATE_OFF // D_MODEL + 1),
                  _const_spec((1, 2 * D_MODEL)), _const_spec((SSM_INNER, D_MODEL)), _const_spec((D_MODEL, D_MODEL))],
        out_specs=[row(D_MODEL), row(D_MODEL), row(D_MODEL)],
        out_shape=[jax.ShapeDtypeStruct((T, D_MODEL), BF16), jax.ShapeDtypeStruct((T, D_MODEL), F32),
                   jax.ShapeDtypeStruct((T, D_MODEL), F32)],
        compiler_params=_cp(("parallel",)))(y_ssm, y_attn, proj_main, proj_main, b_gate, w_bs, w_ba)


def _mix_ln1(merged, w_mo, x, g1, b1, tm=1024):
    T = x.shape[0]
    tm = min(tm, T)

    def body(m_ref, w_ref, x_ref, g_ref, b_ref, r_ref, h_ref, hb_ref):
        r = ALPHA * x_ref[...] + _dot(m_ref[...], w_ref[...])
        r_ref[...] = r
        h = _ln_fwd(r, g_ref[...], b_ref[...])[0]
        h_ref[...] = h
        hb_ref[...] = h.astype(BF16)

    row = pl.BlockSpec((tm, D_MODEL), lambda i: (i, 0))
    return pl.pallas_call(
        body, name="mix_ln1", grid=(T // tm,),
        in_specs=[row, _const_spec((D_MODEL, D_MODEL)), row, _const_spec((1, D_MODEL)), _const_spec((1, D_MODEL))],
        out_specs=[row, row, row],
        out_shape=[jax.ShapeDtypeStruct((T, D_MODEL), F32), jax.ShapeDtypeStruct((T, D_MODEL), F32),
                   jax.ShapeDtypeStruct((T, D_MODEL), BF16)],
        compiler_params=_cp(("parallel",)))(merged, w_mo, x, g1, b1)


def _ffn_conv_glu(u_pre, w, b, tr=2048, tc=256):
    T = u_pre.shape[0]
    tr = min(tr, T)
    K = FFN_CONV
    nj = D_FF // tc
    hb = tr // HALO
    assert T % tr == 0 and D_FF % tc == 0

    def body(xg_ref, xgp_ref, xv_ref, xvp_ref, wg_ref, wv_ref, bg_ref, bv_ref, u_ref, a_ref, head_g, head_v):
        i = pl.program_id(1)
        halves = []
        for x_ref, xp_ref, w_ref, b_ref, head in ((xg_ref, xgp_ref, wg_ref, bg_ref, head_g),
                                                  (xv_ref, xvp_ref, wv_ref, bv_ref, head_v)):
            head[0:HALO, :] = jnp.where(i > 0, xp_ref[...], 0.0)
            head[HALO:, :] = x_ref[0:HALO, :]
            halves.append((x_ref, head, _taps(w_ref, K, tc), jnp.broadcast_to(b_ref[...], (HALO, tc))))

        def conv(half, r):
            x_ref, head, wk, bias = halves[half]
            lo = r * HALO
            acc = bias + wk[K - 1] * x_ref[lo:lo + HALO, :]
            for k in range(K - 1):
                s = K - 1 - k
                acc = acc + wk[k] * (head[HALO - s:2 * HALO - s, :] if r == 0 else x_ref[lo - s:lo + HALO - s, :])
            return acc

        for r2 in range(tr // (2 * HALO)):
            acts = []
            for r in (2 * r2, 2 * r2 + 1):
                lo = r * HALO
                ug, uv = conv(0, r), conv(1, r)
                u_ref[0, lo:lo + HALO, :] = ug
                u_ref[1, lo:lo + HALO, :] = uv
                acts.append(ug * _sigmoid(ug) * uv)
            a_ref[2 * r2 * HALO:(2 * r2 + 2) * HALO, :] = jnp.concatenate(acts, axis=0).astype(BF16)

    tile = lambda off: pl.BlockSpec((tr, tc), lambda j, i: (i, off + j))
    prev = lambda off: pl.BlockSpec((HALO, tc), lambda j, i: (jnp.maximum(i * hb - 1, 0), off + j))
    row = lambda rows, off: pl.BlockSpec((rows, tc), lambda j, i: (0, off + j))
    return pl.pallas_call(
        body, name="ffn_conv_glu", grid=(nj, T // tr),
        in_specs=[tile(0), prev(0), tile(nj), prev(nj), row(K, 0), row(K, nj), row(1, 0), row(1, nj)],
        out_specs=[pl.BlockSpec((2, tr, tc), lambda j, i: (0, i, j)), pl.BlockSpec((tr, tc), lambda j, i: (i, j))],
        out_shape=[jax.ShapeDtypeStruct((2, T, D_FF), F32), jax.ShapeDtypeStruct((T, D_FF), BF16)],
        scratch_shapes=[pltpu.VMEM((2 * HALO, tc), F32), pltpu.VMEM((2 * HALO, tc), F32)],
        compiler_params=_cp(("parallel", "arbitrary")))(u_pre, u_pre, u_pre, u_pre, w, w, b, b)


def _down_ln2_loss(act, w_down, h1, target, g2, b2, tm=1024):
    T = h1.shape[0]
    tm = min(tm, T)

    def body(a_ref, w_ref, h_ref, t_ref, g_ref, b_ref, dr_ref, drb_ref, dg_ref, db_ref, l_ref):
        @pl.when(pl.program_id(0) == 0)
        def _():
            dg_ref[...] = jnp.zeros_like(dg_ref)
            db_ref[...] = jnp.zeros_like(db_ref)
            l_ref[...] = jnp.zeros_like(l_ref)

        r = ALPHA * h_ref[...] + _dot(a_ref[...], w_ref[...])
        y, xhat, rstd = _ln_fwd(r, g_ref[...], b_ref[...])
        err = y - t_ref[...]
        l_ref[...] += jnp.sum(err * err, keepdims=True)
        dy = err * (1.0 / D_MODEL)
        dg_ref[...] += jnp.sum(dy * xhat, axis=0, keepdims=True)
        db_ref[...] += jnp.sum(dy, axis=0, keepdims=True)
        dr = _ln_bwd(dy, xhat, rstd, g_ref[...])
        dr_ref[...] = dr
        drb_ref[...] = dr.astype(BF16)

    row = pl.BlockSpec((tm, D_MODEL), lambda i: (i, 0))
    vec = _const_spec((1, D_MODEL))
    return pl.pallas_call(
        body, name="down_ln2_loss", grid=(T // tm,),
        in_specs=[pl.BlockSpec((tm, D_FF), lambda i: (i, 0)), _resident_spec((D_FF, D_MODEL)), row, row, vec, vec],
        out_specs=[row, row, vec, vec, _const_spec((1, 1))],
        out_shape=[jax.ShapeDtypeStruct((T, D_MODEL), F32), jax.ShapeDtypeStruct((T, D_MODEL), BF16),
                   jax.ShapeDtypeStruct((1, D_MODEL), F32), jax.ShapeDtypeStruct((1, D_MODEL), F32),
                   jax.ShapeDtypeStruct((1, 1), F32)],
        compiler_params=_cp(("arbitrary",)))(act, w_down, h1, target, g2, b2)


def _ffn_gate_conv_bwd(dact, u, u_pre, w, tr=2048, tc=256):
    T = dact.shape[0]
    tr = min(tr, T)
    K = FFN_CONV
    nj, nt, n, hb = D_FF // tc, T // tr, tr // HALO, tr // HALO
    last_hblock = T // HALO - 1
    assert T % tr == 0 and D_FF % tc == 0 and n % 2 == 0

    def body(da_ref, dan_ref, u_ref, un_ref, xg_ref, xv_ref, wg_ref, wv_ref,
             o_ref, dwg_ref, dwv_ref, dbg_ref, dbv_ref, gext_g, gext_v):
        i = pl.program_id(1)
        for r in range(n + 1):
            rows = slice(r * HALO, (r + 1) * HALO)
            if r < n:
                da, g, v = da_ref[rows, :], u_ref[0, rows, :], u_ref[1, rows, :]
            else:
                da, g, v = jnp.where(i < nt - 1, dan_ref[...], 0.0), un_ref[0], un_ref[1]
            sg = _sigmoid(g)
            gext_g[rows, :] = da * v * (sg * (1.0 + g * (1.0 - sg)))
            gext_v[rows, :] = da * (g * sg)
        for half, (gext, x_ref, w_ref, dw_ref, db_ref) in enumerate(((gext_g, xg_ref, wg_ref, dwg_ref, dbg_ref),
                                                                      (gext_v, xv_ref, wv_ref, dwv_ref, dbv_ref))):
            wk = _taps(w_ref, K, tc)
            acc_w = [jnp.zeros((HALO, tc), F32) for _ in range(K)]
            acc_b = jnp.zeros((HALO, tc), F32)
            for r2 in range(n // 2):
                pair = []
                for r in (2 * r2, 2 * r2 + 1):
                    lo = r * HALO
                    x = x_ref[lo:lo + HALO, :]
                    dpre = None
                    for s in range(K):
                        gs = gext[lo + s:lo + HALO + s, :]
                        dpre = wk[K - 1 - s] * gs if dpre is None else dpre + wk[K - 1 - s] * gs
                        acc_w[K - 1 - s] = acc_w[K - 1 - s] + gs * x
                        if s == 0:
                            acc_b = acc_b + gs
                    pair.append(dpre)
                o_ref[half, 2 * r2 * HALO:(2 * r2 + 2) * HALO, :] = jnp.concatenate(pair, axis=0).astype(o_ref.dtype)

            @pl.when(i == 0)
            def _():
                dw_ref[...] = jnp.zeros_like(dw_ref)
                db_ref[...] = jnp.zeros_like(db_ref)

            db_ref[...] += jnp.sum(acc_b, axis=0, keepdims=True)
            dw_ref[...] += jnp.concatenate([jnp.sum(a, axis=0, keepdims=True) for a in acc_w], axis=0)

    nxt = lambda i: jnp.minimum((i + 1) * hb, last_hblock)
    taps = lambda off: pl.BlockSpec((K, tc), lambda j, i: (0, off + j))
    dw_spec, db_spec = pl.BlockSpec((K, tc), lambda j, i: (0, j)), pl.BlockSpec((1, tc), lambda j, i: (0, j))
    du_pre, dwg, dwv, dbg, dbv = pl.pallas_call(
        body, name="ffn_gate_conv_bwd", grid=(nj, nt),
        in_specs=[pl.BlockSpec((tr, tc), lambda j, i: (i, j)), pl.BlockSpec((HALO, tc), lambda j, i: (nxt(i), j)),
                  pl.BlockSpec((2, tr, tc), lambda j, i: (0, i, j)), pl.BlockSpec((2, HALO, tc), lambda j, i: (0, nxt(i), j)),
                  pl.BlockSpec((tr, tc), lambda j, i: (i, j)), pl.BlockSpec((tr, tc), lambda j, i: (i, nj + j)),
                  taps(0), taps(nj)],
        out_specs=[pl.BlockSpec((2, tr, tc), lambda j, i: (0, i, j)), dw_spec, dw_spec, db_spec, db_spec],
        out_shape=[jax.ShapeDtypeStruct((2, T, D_FF), BF16), jax.ShapeDtypeStruct((K, D_FF), F32),
                   jax.ShapeDtypeStruct((K, D_FF), F32), jax.ShapeDtypeStruct((1, D_FF), F32),
                   jax.ShapeDtypeStruct((1, D_FF), F32)],
        scratch_shapes=[pltpu.VMEM((tr + HALO, tc), F32), pltpu.VMEM((tr + HALO, tc), F32)],
        compiler_params=_cp(("parallel", "arbitrary")))(dact, dact, u, u, u_pre, u_pre, w, w)
    return du_pre, jnp.concatenate([dwg, dwv], axis=1), jnp.concatenate([dbg, dbv], axis=1)


def _ffn_bwd_in(du_pre, w_up, dr2, r1, g1, b1, tm=512, tk=2816):
    T = dr2.shape[0]
    tm = min(tm, T)
    assert T % tm == 0 and D_FF % tk == 0
    nk = 2 * D_FF // tk
    kh = D_FF // tk

    def body(d_ref, w_ref, dr2_ref, r_ref, g_ref, b_ref, dr1_ref, dr1b_ref, dg_ref, db_ref, acc):
        i, k = pl.program_id(0), pl.program_id(1)

        @pl.when(jnp.logical_and(i == 0, k == 0))
        def _():
            dg_ref[...] = jnp.zeros_like(dg_ref)
            db_ref[...] = jnp.zeros_like(db_ref)

        @pl.when(k == 0)
        def _():
            acc[...] = ALPHA * dr2_ref[...]

        acc[...] += _dot(d_ref[...], w_ref[...], NT)

        @pl.when(k == nk - 1)
        def _():
            _, xhat, rstd = _ln_fwd(r_ref[...], g_ref[...], b_ref[...])
            dy = acc[...]
            dg_ref[...] += jnp.sum(dy * xhat, axis=0, keepdims=True)
            db_ref[...] += jnp.sum(dy, axis=0, keepdims=True)
            dr1 = _ln_bwd(dy, xhat, rstd, g_ref[...])
            dr1_ref[...] = dr1
            dr1b_ref[...] = dr1.astype(BF16)

    row = pl.BlockSpec((tm, D_MODEL), lambda i, k: (i, 0))
    vec = _const_spec((1, D_MODEL))
    return pl.pallas_call(
        body, name="ffn_bwd_in", grid=(T // tm, nk),
        in_specs=[pl.BlockSpec((None, tm, tk), lambda i, k: (k // kh, i, k % kh)),
                  pl.BlockSpec((D_MODEL, tk), lambda i, k: (0, k)), row, row, vec, vec],
        out_specs=[row, row, vec, vec],
        out_shape=[jax.ShapeDtypeStruct((T, D_MODEL), F32), jax.ShapeDtypeStruct((T, D_MODEL), BF16),
                   jax.ShapeDtypeStruct((1, D_MODEL), F32), jax.ShapeDtypeStruct((1, D_MODEL), F32)],
        scratch_shapes=[pltpu.VMEM((tm, D_MODEL), F32)],
        compiler_params=_cp(("arbitrary", "arbitrary")))(du_pre, w_up, dr2, r1, g1, b1)


def _mix_bwd(dr1, w_mo, w_bs, w_ba, bs, ba, proj_main, b_gate, tm=512):
    T = dr1.shape[0]

    def body(d_ref, wmo_ref, wbs_ref, wba_ref, bs_ref, ba_ref, gs_ref, ga_ref, bg_ref,
             dg_ref, dbs_ref, dba_ref, dys_ref, dya_ref, dbg_ref):
        @pl.when(pl.program_id(0) == 0)
        def _():
            dbg_ref[...] = jnp.zeros_like(dbg_ref)

        dm = _dot(d_ref[...].astype(BF16), wmo_ref[...], NT)
        g_s = _sigmoid(gs_ref[...] + bg_ref[:, 0:D_MODEL])
        g_a = _sigmoid(ga_ref[...] + bg_ref[:, D_MODEL:])
        dgs = dm * bs_ref[...] * g_s * (1.0 - g_s)
        dga = dm * ba_ref[...] * g_a * (1.0 - g_a)
        dg_ref[:, 0:D_MODEL] = dgs.astype(BF16)
        dg_ref[:, D_MODEL:] = dga.astype(BF16)
        dbg_ref[:, 0:D_MODEL] += jnp.sum(dgs, axis=0, keepdims=True)
        dbg_ref[:, D_MODEL:] += jnp.sum(dga, axis=0, keepdims=True)
        dbs = (dm * g_s).astype(BF16)
        dba = (dm * g_a).astype(BF16)
        dbs_ref[...] = dbs
        dba_ref[...] = dba
        dys_ref[...] = _dot(dbs, wbs_ref[...], NT)
        dya_ref[...] = _dot(dba, wba_ref[...], NT).astype(BF16)

    row = lambda w, off=0: pl.BlockSpec((tm, w), lambda i: (i, off))
    return pl.pallas_call(
        body, name="mix_bwd", grid=(T // tm,),
        in_specs=[row(D_MODEL), _const_spec((D_MODEL, D_MODEL)), _const_spec((SSM_INNER, D_MODEL)),
                  _const_spec((D_MODEL, D_MODEL)), row(D_MODEL), row(D_MODEL),
                  row(D_MODEL, GATE_OFF // D_MODEL), row(D_MODEL, GATE_OFF // D_MODEL + 1), _const_spec((1, 2 * D_MODEL))],
        out_specs=[row(2 * D_MODEL, GATE_OFF // (2 * D_MODEL)), row(D_MODEL), row(D_MODEL), row(SSM_INNER), row(D_MODEL),
                   _const_spec((1, 2 * D_MODEL))],
        out_shape=[jax.ShapeDtypeStruct((T, MAIN_COLS), BF16), jax.ShapeDtypeStruct((T, D_MODEL), BF16),
                   jax.ShapeDtypeStruct((T, D_MODEL), BF16), jax.ShapeDtypeStruct((T, SSM_INNER), F32),
                   jax.ShapeDtypeStruct((T, D_MODEL), BF16), jax.ShapeDtypeStruct((1, 2 * D_MODEL), F32)],
        compiler_params=_cp(("arbitrary",)))(dr1, w_mo, w_bs, w_ba, bs, ba, proj_main, proj_main, b_gate)


def _local_step(x, target, w, p, late_weights=None, early_grads=None):
    xb = x.astype(BF16)
    if late_weights is None:
        proj_main = _matmul(xb, w["in_main"], "nn", F32, "in_proj_main", tm=1024, tn=2048)
    else:
        proj_main, *landed = _matmul(xb, w["in_main"], "nn", F32, "in_proj_main", tm=1024, tn=2048,
                                     comm=("gather", late_weights[0]))
        w = {**w, **late_weights[1](landed)}
    proj_tail = _matmul(xb, w["in_tail"], "nn", F32, "in_proj_tail", tm=2048, tn=TAIL_COLS)
    xbc, dsil = _conv_silu_fwd(proj_main, XBC_OFF, XBC_COLS, p["ssm_conv_w"], p["ssm_conv_b"], SSM_CONV, "ssm_conv_fwd",
                               tr=2048)
    y_ssm, ypre, hs = _ssd_fwd(xbc, proj_main, proj_tail, p["dtb_col"], p["alog_col"], p["d_exp"], p["ssm_norm_w"])
    y_attn, lse = _attn_fwd(proj_main, proj_tail, p["bias_tbl"], p["attn_sinks"])
    merged, bs, ba = _merge_fwd(y_ssm, y_attn, proj_main, p["b_gate"], w["bs"], w["ba"])
    r1, h1, h1b = _mix_ln1(merged, w["mo"], x, p["ln1_g"], p["ln1_b"])
    u_pre = _matmul(h1b, w["up"], "nn", F32, "ffn_up", tm=1024, tn=2816)
    u, act = _ffn_conv_glu(u_pre, p["ffn_conv_w"], p["ffn_conv_b"])
    dr2, dr2b, dg2, db2, sq = _down_ln2_loss(act, w["down"], h1, target, p["ln2_g"], p["ln2_b"])
    g = {"ln2_g": dg2, "ln2_b": db2}
    g["w_down"] = _matmul(act, dr2b, "tn", BF16, "dw_down", tm=1408, tn=1024, tk=2048)
    dact = _matmul(dr2b, w["down"], "nt", F32, "ffn_dact", tm=1024, tn=2816, tk=1024)
    du_pre, g["ffn_conv_w"], g["ffn_conv_b"] = _ffn_gate_conv_bwd(dact, u, u_pre, p["ffn_conv_w"])
    g["w_up"] = _matmul(h1b, du_pre, "tn", BF16, "dw_up", tm=1024, tn=1408, tk=2048)
    dr1, dr1b, g["ln1_g"], g["ln1_b"] = _ffn_bwd_in(du_pre, w["up"], dr2, r1, p["ln1_g"], p["ln1_b"])
    g["w_mix_out"] = _matmul(merged, dr1b, "tn", BF16, "dw_mix_out", tm=1024, tn=1024, tk=2048)
    dmain, dbs, dba, dy_ssm, dy_attn, g["b_gate"] = _mix_bwd(dr1b, w["mo"], w["bs"], w["ba"], bs, ba, proj_main, p["b_gate"])
    g["w_branch_ssm"] = _matmul(y_ssm, dbs, "tn", BF16, "dw_branch_ssm", tm=1024, tn=1024, tk=2048)
    g["w_branch_attn"] = _matmul(y_attn, dba, "tn", BF16, "dw_branch_attn", tm=1024, tn=1024, tk=2048)
    dmain, dtail, dbias, g["attn_sinks"] = _attn_bwd(dy_attn, lse, proj_main, proj_tail, p["bias_tbl"], p["attn_sinks"], dmain)
    g["rel_bias"] = _rel_bias_grad(dbias.reshape(ATTN_HEADS, WINDOW * 2 * WINDOW), p["bucket_onehot"]).T
    dmain, dtail, dco, g["ssm_norm_w"], dd, dalog, ddtb = _ssd_bwd(
        dy_ssm, ypre, xbc, dsil, hs, proj_main, proj_tail, p["dtb_col"], p["alog_col"], p["d_exp"], p["ssm_norm_w"],
        p["ehead_t"], dmain, dtail)
    g["ssm_d"], g["ssm_a_log"], g["ssm_dt_bias"] = (a.reshape(1, SSM_HEADS) for a in (dd, dalog, ddtb))
    dmain, g["ssm_conv_w"], g["ssm_conv_b"] = _conv_bwd(
        dco, proj_main, XBC_OFF, XBC_COLS, p["ssm_conv_w"], SSM_CONV, dmain, XBC_OFF, "ssm_conv_bwd",
        tr=2048)
    g["in_tail"] = _matmul(xb, dtail, "tn", BF16, "dw_in_tail", tm=1024, tn=TAIL_COLS, tk=2048)
    landed = []
    if early_grads is None:
        g["in_main"] = _matmul(xb, dmain, "tn", BF16, "dw_in_main", tm=1024, tn=2048, tk=2048)
    else:
        g["in_main"], *landed = _matmul(xb, dmain, "tn", BF16, "dw_in_main", tm=1024, tn=2048, tk=2048,
                                        comm=("exchange", early_grads(g)))
    return sq, (dmain, dtail, dr1), w, g, landed


def _grad_x(dproj, w, exchange=None):
    dmain, dtail, dr1 = dproj
    landed = None
    if exchange is None:
        dx = _matmul(dmain, w["in_main"], "nt", F32, "dx_main", tm=1024, tk=2048, addend=dr1, addend_scale=ALPHA)
    else:
        dx, landed = _matmul(dmain, w["in_main"], "nt", F32, "dx_main", tm=1024, tk=2048, addend=dr1,
                             addend_scale=ALPHA, comm=("exchange", exchange))
    dx = _matmul(dtail, w["in_tail"], "nt", F32, "dx_tail", tm=2048, tk=TAIL_COLS, addend=dx)
    return dx if exchange is None else (dx, landed)


SHARD_COLS = IN_COLS // N_DEV
W_IN_SEGMENTS = ((O_Z, 2048, "main", Z_OFF), (O_XBC, XBC_COLS, "main", XBC_OFF), (O_DT, SSM_HEADS, "tail", DT_OFF),
                 (O_Q, D_MODEL, "main", Q_OFF), (O_K, 128, "tail", K_OFF), (O_V, 128, "tail", V_OFF),
                 (O_GATE, 2 * D_MODEL, "main", GATE_OFF))


def _w_in_from_shards(shards):
    def seg(off, n):
        pieces = []
        for j in range(off // SHARD_COLS, (off + n - 1) // SHARD_COLS + 1):
            lo, hi = max(off, j * SHARD_COLS), min(off + n, (j + 1) * SHARD_COLS)
            pieces.append(shards[j, :, lo - j * SHARD_COLS:hi - j * SHARD_COLS])
        return pieces

    by_name = {(where, koff): seg(off, n) for off, n, where, koff in W_IN_SEGMENTS}
    main = jnp.concatenate(by_name["main", Z_OFF] + by_name["main", XBC_OFF] + by_name["main", Q_OFF]
                           + by_name["main", GATE_OFF], axis=1)
    tail = jnp.concatenate(by_name["tail", K_OFF] + by_name["tail", V_OFF] + by_name["tail", DT_OFF]
                           + [jnp.zeros((shards.shape[1], 128 - SSM_HEADS), shards.dtype)], axis=1)
    return main, tail


def _w_in_grad_by_device(g_main, g_tail):
    slots = []
    for j in range(N_DEV):
        a, b = j * SHARD_COLS, (j + 1) * SHARD_COLS
        pieces = []
        for off, n, where, koff in W_IN_SEGMENTS:
            lo, hi = max(a, off), min(b, off + n)
            if lo < hi:
                pieces.append((g_main if where == "main" else g_tail)[:, koff + lo - off:koff + hi - off])
        slots.append(jnp.concatenate(pieces, axis=1))
    return jnp.stack(slots)


def _prep_params(rel_bias, b_gate, ssm_conv_w, ssm_conv_b, ssm_dt_bias, ssm_a_log, ssm_d, ssm_norm_w, attn_sinks,
                 ln1_g, ln1_b, ffn_conv_w, ffn_conv_b, ln2_g, ln2_b):
    bucket, in_window = _band_geometry()
    bucket, in_window = bucket.T, in_window.T
    onehot = jnp.logical_and(bucket.reshape(-1, 1) == jnp.arange(REL_BUCKETS)[None, :],
                             in_window.reshape(-1, 1)).astype(F32)
    onehot_t = jnp.logical_and(bucket.reshape(1, -1) == jnp.arange(REL_BUCKETS)[:, None],
                               in_window.reshape(1, -1)).astype(BF16)
    bias_tbl = _bias_table(rel_bias.T, onehot_t, jnp.where(in_window.reshape(1, -1), 0.0, NEG))
    ehead_t = (jnp.arange(SSM_INNER)[None, :] // SSMD == jnp.arange(SSM_HEADS)[:, None]).astype(F32)
    return {"bias_tbl": bias_tbl, "bucket_onehot": onehot, "b_gate": b_gate, "ssm_conv_w": ssm_conv_w,
            "ssm_conv_b": ssm_conv_b, "dtb_col": ssm_dt_bias.reshape(SSM_HEADS, 1),
            "alog_col": ssm_a_log.reshape(SSM_HEADS, 1), "ehead_t": ehead_t,
            "d_exp": jnp.repeat(ssm_d, SSMD, axis=1), "ssm_norm_w": ssm_norm_w, "attn_sinks": attn_sinks,
            "ln1_g": ln1_g, "ln1_b": ln1_b, "ffn_conv_w": ffn_conv_w, "ffn_conv_b": ffn_conv_b,
            "ln2_g": ln2_g, "ln2_b": ln2_b}


def _all_gather(shards, name):
    nb = len(shards)

    def body(*refs):
        for phase in _gather_phases(refs[:nb], refs[nb:2 * nb], *refs[2 * nb:]):
            phase()

    any_spec = pl.BlockSpec(memory_space=pl.ANY)
    return pl.pallas_call(
        body, name=name, out_shape=[jax.ShapeDtypeStruct((N_DEV,) + s.shape, s.dtype) for s in shards],
        in_specs=[any_spec] * nb, out_specs=[any_spec] * nb, scratch_shapes=_comm_sems(nb))(*shards)


def _adamw_math(w, g, m, v):
    m = ADAM_B1 * m + (1.0 - ADAM_B1) * g
    v = ADAM_B2 * v + (1.0 - ADAM_B2) * (g * g)
    m_hat = m / (1.0 - ADAM_B1 ** ADAM_STEP)
    v_hat = v / (1.0 - ADAM_B2 ** ADAM_STEP)
    return -ADAM_LR * (m_hat / (jnp.sqrt(v_hat) + ADAM_EPS) + ADAM_WD * w), m, v


def _slot_total(s_ref):
    g = s_ref[0].astype(F32)
    for i in range(1, N_DEV):
        g = g + s_ref[i].astype(F32)
    return g


def _adamw(landed, w, m, v, name):
    R, C = w.shape
    tr = 256 if R % 256 == 0 and R > 256 else R

    def body(s_ref, w_ref, m_ref, v_ref, g_ref, d_ref, nm_ref, nv_ref):
        g = _slot_total(s_ref)
        g_ref[...] = g
        d_ref[...], nm_ref[...], nv_ref[...] = _adamw_math(w_ref[...], g, m_ref[...], v_ref[...])

    spec = pl.BlockSpec((tr, C), lambda i: (i, 0))
    return pl.pallas_call(
        body, name=name, grid=(R // tr,), in_specs=[pl.BlockSpec((N_DEV, tr, C), lambda i: (0, i, 0))] + [spec] * 3,
        out_specs=[spec] * 4, out_shape=[jax.ShapeDtypeStruct((R, C), F32)] * 4,
        compiler_params=_cp(("parallel",)))(landed, w, m, v)


def _small_update(landed, ws, ms, vs):
    k = len(ws)

    def body(*refs):
        s_ref, w_refs, m_refs, v_refs = refs[0], refs[1:1 + k], refs[1 + k:1 + 2 * k], refs[1 + 2 * k:1 + 3 * k]
        outs = refs[1 + 3 * k:]
        g_all = _slot_total(s_ref)
        for i in range(k):
            n = w_refs[i].shape[1]
            g = g_all[i:i + 1, 0:n]
            outs[i][...] = g
            outs[k + i][...], outs[2 * k + i][...], outs[3 * k + i][...] = _adamw_math(
                w_refs[i][...], g, m_refs[i][...], v_refs[i][...])

    return pl.pallas_call(body, name="small_update",
                          out_shape=[jax.ShapeDtypeStruct(w.shape, F32) for w in ws] * 4)(landed, *ws, *ms, *vs)


SHARDED = {"w_in": "cols", "w_branch_ssm": "rows", "w_branch_attn": "rows", "w_mix_out": "rows", "w_up": "cols",
           "w_down": "rows", "ssm_conv_w": "cols", "ffn_conv_w": "cols"}
LATE = ("w_branch_ssm", "w_branch_attn", "w_mix_out", "w_up", "w_down")
SHORT = {"w_branch_ssm": "bs", "w_branch_attn": "ba", "w_mix_out": "mo", "w_up": "up", "w_down": "down"}
SMALL = ("rel_bias", "b_gate", "ssm_conv_b", "ssm_dt_bias", "ssm_a_log", "ssm_d", "ssm_norm_w", "attn_sinks",
         "ln1_g", "ln1_b", "ffn_conv_b", "ln2_g", "ln2_b")
WEIGHTS = ("rel_bias", "w_in", "b_gate", "ssm_conv_w", "ssm_conv_b", "ssm_dt_bias", "ssm_a_log", "ssm_d", "ssm_norm_w",
           "attn_sinks", "w_branch_ssm", "w_branch_attn", "w_mix_out", "ln1_g", "ln1_b", "w_up", "ffn_conv_w",
           "ffn_conv_b", "w_down", "ln2_g", "ln2_b")
SMALL_ROWS, SMALL_COLS = 16, 2 * D_FF


def _by_device(full, how):
    r, c = full.shape
    if how == "rows":
        return full.reshape(N_DEV, r // N_DEV, c)
    return full.reshape(r, N_DEV, c // N_DEV).transpose(1, 0, 2)


def _from_devices(slots, how):
    _, r, c = slots.shape
    if how == "rows":
        return slots.reshape(N_DEV * r, c)
    return slots.transpose(1, 0, 2).reshape(r, N_DEV * c)


def kernel(x, rel_bias, w_in, b_gate, ssm_conv_w, ssm_conv_b, ssm_dt_bias, ssm_a_log, ssm_d, ssm_norm_w, attn_sinks, w_branch_ssm, w_branch_attn, w_mix_out, ln1_g, ln1_b, w_up, ffn_conv_w, ffn_conv_b, w_down, ln2_g, ln2_b, loss_target, m_rel_bias, m_w_in, m_b_gate, m_ssm_conv_w, m_ssm_conv_b, m_ssm_dt_bias, m_ssm_a_log, m_ssm_d, m_ssm_norm_w, m_attn_sinks, m_w_branch_ssm, m_w_branch_attn, m_w_mix_out, m_ln1_g, m_ln1_b, m_w_up, m_ffn_conv_w, m_ffn_conv_b, m_w_down, m_ln2_g, m_ln2_b, v_rel_bias, v_w_in, v_b_gate, v_ssm_conv_w, v_ssm_conv_b, v_ssm_dt_bias, v_ssm_a_log, v_ssm_d, v_ssm_norm_w, v_attn_sinks, v_w_branch_ssm, v_w_branch_attn, v_w_mix_out, v_ln1_g, v_ln1_b, v_w_up, v_ffn_conv_w, v_ffn_conv_b, v_w_down, v_ln2_g, v_ln2_b):
    W = dict(zip(WEIGHTS, (rel_bias, w_in, b_gate, ssm_conv_w, ssm_conv_b, ssm_dt_bias, ssm_a_log, ssm_d, ssm_norm_w,
                           attn_sinks, w_branch_ssm, w_branch_attn, w_mix_out, ln1_g, ln1_b, w_up, ffn_conv_w,
                           ffn_conv_b, w_down, ln2_g, ln2_b)))
    M = dict(zip(WEIGHTS, (m_rel_bias, m_w_in, m_b_gate, m_ssm_conv_w, m_ssm_conv_b, m_ssm_dt_bias, m_ssm_a_log, m_ssm_d,
                           m_ssm_norm_w, m_attn_sinks, m_w_branch_ssm, m_w_branch_attn, m_w_mix_out, m_ln1_g, m_ln1_b,
                           m_w_up, m_ffn_conv_w, m_ffn_conv_b, m_w_down, m_ln2_g, m_ln2_b)))
    V = dict(zip(WEIGHTS, (v_rel_bias, v_w_in, v_b_gate, v_ssm_conv_w, v_ssm_conv_b, v_ssm_dt_bias, v_ssm_a_log, v_ssm_d,
                           v_ssm_norm_w, v_attn_sinks, v_w_branch_ssm, v_w_branch_attn, v_w_mix_out, v_ln1_g, v_ln1_b,
                           v_w_up, v_ffn_conv_w, v_ffn_conv_b, v_w_down, v_ln2_g, v_ln2_b)))
    shard2d = lambda a: a.reshape(a.shape[-2], a.shape[-1])

    (win_all,) = _all_gather([shard2d(w_in).astype(BF16)], "gather_w_in")
    main, tail = _w_in_from_shards(win_all)
    conv_all = _all_gather([shard2d(ssm_conv_w), shard2d(ffn_conv_w)], "gather_conv_weights")
    late_shards = [shard2d(W[n]).astype(BF16) for n in LATE]
    late = lambda landed: {SHORT[n]: _from_devices(a, SHARDED[n]) for n, a in zip(LATE, landed)}
    p = _prep_params(rel_bias, b_gate, _from_devices(conv_all[0], "cols"), ssm_conv_b, ssm_dt_bias, ssm_a_log, ssm_d,
                     ssm_norm_w, attn_sinks, ln1_g, ln1_b, _from_devices(conv_all[1], "cols"), ffn_conv_b, ln2_g, ln2_b)

    early_names = LATE + ("ssm_conv_w", "ffn_conv_w")
    early = lambda g: [_by_device(g[n], SHARDED[n]).astype(BF16 if n in LATE else F32) for n in early_names]
    sq, dproj, w, g, landed = _local_step(x[0], loss_target[0], {"in_main": main, "in_tail": tail}, p,
                                          (late_shards, late), early)
    landed = dict(zip(early_names, landed))
    dx, landed["w_in"] = _grad_x(dproj, w, exchange=[_w_in_grad_by_device(g.pop("in_main"), g.pop("in_tail"))])
    loss = (0.5 / D_MODEL) * lax.psum(sq[0, 0], ("x", "y", "c"))
    grads, deltas, new_m, new_v = {}, {}, {}, {}
    for n in SHARDED:
        outs = _adamw(landed[n], shard2d(W[n]), shard2d(M[n]), shard2d(V[n]), "adamw_" + n)
        grads[n], deltas[n], new_m[n], new_v[n] = (a.reshape(W[n].shape) for a in outs)

    row = lambda a: a.reshape(1, -1)
    packed = jnp.concatenate([jnp.pad(row(g[n]), ((0, 0), (0, SMALL_COLS - g[n].size))) for n in SMALL]
                             + [jnp.zeros((SMALL_ROWS - len(SMALL), SMALL_COLS), F32)], axis=0)
    (small_all,) = _all_gather([packed], "gather_small_grads")
    outs = _small_update(small_all, *[[row(src[n]) for n in SMALL] for src in (W, M, V)])
    for i, n in enumerate(SMALL):
        grads[n], deltas[n], new_m[n], new_v[n] = (outs[j * len(SMALL) + i].reshape(W[n].shape) for j in range(4))

    return (loss, dx[None], *[grads[n] for n in WEIGHTS], *[deltas[n] for n in WEIGHTS],
            *[new_m[n] for n in WEIGHTS], *[new_v[n] for n in WEIGHTS])
```
